```python
import jax, jax.numpy as jnp
from jax import lax
import numpy as np

D_MODEL = 1024
BATCH = 32
SEQ = 2048
DEPTH = 4

HEAD_DIM = 64
D_MIX = D_MODEL
D_ATTN = D_MIX // 2
D_POOL = D_MIX // 4
D_CONV = D_MIX // 4
N_ATTN_HEADS = D_ATTN // HEAD_DIM
POOL_WINDOWS = (2, 4, 8, 16)
N_POOL_GROUPS = len(POOL_WINDOWS)
POOL_GROUP_DIM = D_POOL // N_POOL_GROUPS
N_CONV_HEADS = D_CONV // HEAD_DIM
CONV_WIDTH = 3
D_FF = 2816
Q_BLOCK = 128
RMS_EPS = 1e-6
IN_SPLITS = (D_ATTN, D_ATTN, D_ATTN, N_ATTN_HEADS, D_POOL, D_CONV, D_CONV, D_CONV)
D_IN = sum(IN_SPLITS)

kernel_name = "hymba_style_fox_pool_shortconv_macaron"


def _rmsnorm(x, g):
    x32 = x.astype(jnp.float32)
    y = x32 * lax.rsqrt(jnp.mean(x32 * x32, axis=-1, keepdims=True) + RMS_EPS)
    return (y * g.astype(jnp.float32)).astype(x.dtype)


def _swiglu(h, w_in, w_out):
    gate, up = jnp.split(h @ w_in, 2, axis=-1)
    return (jax.nn.silu(gate) * up) @ w_out


def _fox_attention(q, k, v, f_logit, b_forget):
    b, s, _ = q.shape
    qh = q.reshape(b, s, N_ATTN_HEADS, HEAD_DIM).transpose(0, 2, 1, 3)
    kh = k.reshape(b, s, N_ATTN_HEADS, HEAD_DIM).transpose(0, 2, 1, 3)
    vh = v.reshape(b, s, N_ATTN_HEADS, HEAD_DIM).transpose(0, 2, 1, 3)
    log_f = jax.nn.log_sigmoid(f_logit.astype(jnp.float32) + b_forget.astype(jnp.float32))
    dcum = lax.cumsum(log_f, axis=1).transpose(0, 2, 1)
    scale = HEAD_DIM ** -0.5
    outs = []
    for i in range(s // Q_BLOCK):
        q0 = i * Q_BLOCK
        end = q0 + Q_BLOCK
        sc = jnp.einsum('bhqd,bhkd->bhqk', qh[:, :, q0:end], kh[:, :, :end],
                        preferred_element_type=jnp.float32) * scale
        sc = sc + dcum[:, :, q0:end, None] - dcum[:, :, None, :end]
        mask = jnp.arange(end)[None, :] <= (q0 + jnp.arange(Q_BLOCK))[:, None]
        sc = jnp.where(mask, sc, -jnp.inf)
        p = jax.nn.softmax(sc, axis=-1)
        outs.append(jnp.einsum('bhqk,bhkd->bhqd', p.astype(vh.dtype), vh[:, :, :end]))
    o = jnp.concatenate(outs, axis=2)
    return o.transpose(0, 2, 1, 3).reshape(b, s, D_ATTN)


def _multiscale_pool(u, w_pool, pool_scale):
    b, s, _ = u.shape
    groups = jnp.split(u, N_POOL_GROUPS, axis=-1)
    pos = jnp.arange(s)
    outs = []
    for gi, w in enumerate(POOL_WINDOWS):
        ug = groups[gi]
        cs = lax.cumsum(ug.astype(jnp.float32), axis=1)
        cs_pad = jnp.pad(cs, ((0, 0), (w, 0), (0, 0)))
        win_sum = cs - cs_pad[:, :s]
        count = jnp.minimum(pos + 1, w).astype(jnp.float32)[None, :, None]
        pooled = (win_sum / count).astype(ug.dtype) - ug
        outs.append(jnp.einsum('bsc,cd->bsd', pooled, w_pool[gi]))
    return jnp.concatenate(outs, axis=-1) * pool_scale


def _short_conv(b_gate, c_gate, h, conv_w):
    u = c_gate * h
    rhs = conv_w.reshape(CONV_WIDTH, 1, D_CONV).astype(u.dtype)
    y = lax.conv_general_dilated(u, rhs, window_strides=(1,), padding=((CONV_WIDTH - 1, 0),),
                                 dimension_numbers=('NWC', 'WIO', 'NWC'),
                                 feature_group_count=D_CONV)
    return b_gate * y


def _hybrid_mixer(xn, w_in, b_forget, w_pool, pool_scale, conv_w, w_out):
    proj = xn @ w_in
    idx = list(np.cumsum(IN_SPLITS)[:-1])
    q, k, v, f_logit, pool_in, cb, cc, ch = jnp.split(proj, idx, axis=-1)
    y_attn = _fox_attention(q, k, v, f_logit, b_forget)
    y_pool = _multiscale_pool(pool_in, w_pool, pool_scale)
    y_conv = _short_conv(cb, cc, ch, conv_w)
    return jnp.concatenate([y_attn, y_pool, y_conv], axis=-1) @ w_out


def _fwd_setup_inputs(seed: int = 0) -> dict:
    key = jax.random.key(seed)
    ks = jax.random.split(key, 16)
    f32 = jnp.float32

    def nrm(k, shape, fan_in):
        return jax.random.normal(k, shape, f32) * (fan_in ** -0.5)

    def gain(k, shape):
        return 1.0 + 0.1 * jax.random.normal(k, shape, f32)

    return {
        "x": jax.random.normal(ks[0], (BATCH, SEQ, D_MODEL), f32),
        "norm_ffn1": gain(ks[1], (DEPTH, D_MODEL)),
        "w_ffn1_in": nrm(ks[2], (DEPTH, D_MODEL, 2 * D_FF), D_MODEL),
        "w_ffn1_out": nrm(ks[3], (DEPTH, D_FF, D_MODEL), D_FF),
        "norm_mix": gain(ks[4], (DEPTH, D_MODEL)),
        "w_mix_in": nrm(ks[5], (DEPTH, D_MODEL, D_IN), D_MODEL),
        "b_forget": 3.0 + 0.5 * jax.random.normal(ks[6], (DEPTH, N_ATTN_HEADS), f32),
        "w_pool": nrm(ks[7], (DEPTH, N_POOL_GROUPS, POOL_GROUP_DIM, POOL_GROUP_DIM), POOL_GROUP_DIM),
        "pool_scale": gain(ks[8], (DEPTH, D_POOL)),
        "conv_w": nrm(ks[9], (DEPTH, CONV_WIDTH, D_CONV), CONV_WIDTH),
        "w_mix_out": nrm(ks[10], (DEPTH, D_MIX, D_MODEL), D_MIX),
        "norm_ffn2": gain(ks[11], (DEPTH, D_MODEL)),
        "w_ffn2_in": nrm(ks[12], (DEPTH, D_MODEL, 2 * D_FF), D_MODEL),
        "w_ffn2_out": nrm(ks[13], (DEPTH, D_FF, D_MODEL), D_FF),
        "norm_final": gain(ks[14], (D_MODEL,)),
    }


def _fwd_reference(x, norm_ffn1, w_ffn1_in, w_ffn1_out, norm_mix, w_mix_in, b_forget, w_pool,
              pool_scale, conv_w, w_mix_out, norm_ffn2, w_ffn2_in, w_ffn2_out, norm_final):
    for l in range(DEPTH):
        x = x + 0.5 * _swiglu(_rmsnorm(x, norm_ffn1[l]), w_ffn1_in[l], w_ffn1_out[l])
        x = x + _hybrid_mixer(_rmsnorm(x, norm_mix[l]), w_mix_in[l], b_forget[l], w_pool[l],
                              pool_scale[l], conv_w[l], w_mix_out[l])
        x = x + 0.5 * _swiglu(_rmsnorm(x, norm_ffn2[l]), w_ffn2_in[l], w_ffn2_out[l])
    return _rmsnorm(x, norm_final)


import jax as _jax
import jax.numpy as _jnp

TWIN_FORMAT = 'train_step'
FWD_PARAMS = ['x', 'norm_ffn1', 'w_ffn1_in', 'w_ffn1_out', 'norm_mix', 'w_mix_in', 'b_forget', 'w_pool', 'pool_scale', 'conv_w', 'w_mix_out', 'norm_ffn2', 'w_ffn2_in', 'w_ffn2_out', 'norm_final']
TWIN_WEIGHTS = ['norm_ffn1', 'w_ffn1_in', 'w_ffn1_out', 'norm_mix', 'w_mix_in', 'b_forget', 'w_pool', 'pool_scale', 'conv_w', 'w_mix_out', 'norm_ffn2', 'w_ffn2_in', 'w_ffn2_out', 'norm_final']
TWIN_DIFF_INPUT = 'x'
TWIN_INPUTS = ['x', 'norm_ffn1', 'w_ffn1_in', 'w_ffn1_out', 'norm_mix', 'w_mix_in', 'b_forget', 'w_pool', 'pool_scale', 'conv_w', 'w_mix_out', 'norm_ffn2', 'w_ffn2_in', 'w_ffn2_out', 'norm_final', 'loss_target', 'm_norm_ffn1', 'm_w_ffn1_in', 'm_w_ffn1_out', 'm_norm_mix', 'm_w_mix_in', 'm_b_forget', 'm_w_pool', 'm_pool_scale', 'm_conv_w', 'm_w_mix_out', 'm_norm_ffn2', 'm_w_ffn2_in', 'm_w_ffn2_out', 'm_norm_final', 'v_norm_ffn1', 'v_w_ffn1_in', 'v_w_ffn1_out', 'v_norm_mix', 'v_w_mix_in', 'v_b_forget', 'v_w_pool', 'v_pool_scale', 'v_conv_w', 'v_w_mix_out', 'v_norm_ffn2', 'v_w_ffn2_in', 'v_w_ffn2_out', 'v_norm_final']
TWIN_OUTPUTS = ['loss', 'grad_x', 'grad_norm_ffn1', 'grad_w_ffn1_in', 'grad_w_ffn1_out', 'grad_norm_mix', 'grad_w_mix_in', 'grad_b_forget', 'grad_w_pool', 'grad_pool_scale', 'grad_conv_w', 'grad_w_mix_out', 'grad_norm_ffn2', 'grad_w_ffn2_in', 'grad_w_ffn2_out', 'grad_norm_final', 'delta_norm_ffn1', 'delta_w_ffn1_in', 'delta_w_ffn1_out', 'delta_norm_mix', 'delta_w_mix_in', 'delta_b_forget', 'delta_w_pool', 'delta_pool_scale', 'delta_conv_w', 'delta_w_mix_out', 'delta_norm_ffn2', 'delta_w_ffn2_in', 'delta_w_ffn2_out', 'delta_norm_final', 'new_m_norm_ffn1', 'new_m_w_ffn1_in', 'new_m_w_ffn1_out', 'new_m_norm_mix', 'new_m_w_mix_in', 'new_m_b_forget', 'new_m_w_pool', 'new_m_pool_scale', 'new_m_conv_w', 'new_m_w_mix_out', 'new_m_norm_ffn2', 'new_m_w_ffn2_in', 'new_m_w_ffn2_out', 'new_m_norm_final', 'new_v_norm_ffn1', 'new_v_w_ffn1_in', 'new_v_w_ffn1_out', 'new_v_norm_mix', 'new_v_w_mix_in', 'new_v_b_forget', 'new_v_w_pool', 'new_v_pool_scale', 'new_v_conv_w', 'new_v_w_mix_out', 'new_v_norm_ffn2', 'new_v_w_ffn2_in', 'new_v_w_ffn2_out', 'new_v_norm_final']
TWIN_LEAF_KINDS = {'loss': 'loss', 'grad_x': 'grad_x', 'grad_norm_ffn1': 'grad_w', 'grad_w_ffn1_in': 'grad_w', 'grad_w_ffn1_out': 'grad_w', 'grad_norm_mix': 'grad_w', 'grad_w_mix_in': 'grad_w', 'grad_b_forget': 'grad_w', 'grad_w_pool': 'grad_w', 'grad_pool_scale': 'grad_w', 'grad_conv_w': 'grad_w', 'grad_w_mix_out': 'grad_w', 'grad_norm_ffn2': 'grad_w', 'grad_w_ffn2_in': 'grad_w', 'grad_w_ffn2_out': 'grad_w', 'grad_norm_final': 'grad_w', 'delta_norm_ffn1': 'delta_w', 'delta_w_ffn1_in': 'delta_w', 'delta_w_ffn1_out': 'delta_w', 'delta_norm_mix': 'delta_w', 'delta_w_mix_in': 'delta_w', 'delta_b_forget': 'delta_w', 'delta_w_pool': 'delta_w', 'delta_pool_scale': 'delta_w', 'delta_conv_w': 'delta_w', 'delta_w_mix_out': 'delta_w', 'delta_norm_ffn2': 'delta_w', 'delta_w_ffn2_in': 'delta_w', 'delta_w_ffn2_out': 'delta_w', 'delta_norm_final': 'delta_w', 'new_m_norm_ffn1': 'new_m', 'new_m_w_ffn1_in': 'new_m', 'new_m_w_ffn1_out': 'new_m', 'new_m_norm_mix': 'new_m', 'new_m_w_mix_in': 'new_m', 'new_m_b_forget': 'new_m', 'new_m_w_pool': 'new_m', 'new_m_pool_scale': 'new_m', 'new_m_conv_w': 'new_m', 'new_m_w_mix_out': 'new_m', 'new_m_norm_ffn2': 'new_m', 'new_m_w_ffn2_in': 'new_m', 'new_m_w_ffn2_out': 'new_m', 'new_m_norm_final': 'new_m', 'new_v_norm_ffn1': 'new_v', 'new_v_w_ffn1_in': 'new_v', 'new_v_w_ffn1_out': 'new_v', 'new_v_norm_mix': 'new_v', 'new_v_w_mix_in': 'new_v', 'new_v_b_forget': 'new_v', 'new_v_w_pool': 'new_v', 'new_v_pool_scale': 'new_v', 'new_v_conv_w': 'new_v', 'new_v_w_mix_out': 'new_v', 'new_v_norm_ffn2': 'new_v', 'new_v_w_ffn2_in': 'new_v', 'new_v_w_ffn2_out': 'new_v', 'new_v_norm_final': 'new_v'}


def _forward(args):
    return _fwd_reference(*[args[k] for k in FWD_PARAMS])


def _output_shape():
    out = _jax.eval_shape(lambda: _forward(_fwd_setup_inputs(0)))
    return out.shape, out.dtype

N_MICROBATCH = 1
ADAM_LR = 0.001
ADAM_B1 = 0.9
ADAM_B2 = 0.999
ADAM_EPS = 1e-08
ADAM_WD = 0.01
ADAM_STEP = 10
PER_EXAMPLE_BATCH_AXIS = {'x': 0, 'loss_target': 0}
SHARED_INPUTS = []
_WEIGHT_DTYPES = {'norm_ffn1': _jnp.float32, 'w_ffn1_in': _jnp.float32, 'w_ffn1_out': _jnp.float32, 'norm_mix': _jnp.float32, 'w_mix_in': _jnp.float32, 'b_forget': _jnp.float32, 'w_pool': _jnp.float32, 'pool_scale': _jnp.float32, 'conv_w': _jnp.float32, 'w_mix_out': _jnp.float32, 'norm_ffn2': _jnp.float32, 'w_ffn2_in': _jnp.float32, 'w_ffn2_out': _jnp.float32, 'norm_final': _jnp.float32}
MOMENT_SCALE = {'norm_ffn1': 1.262196e-01, 'w_ffn1_in': 5.378319e-02, 'w_ffn1_out': 8.805975e-02, 'norm_mix': 2.329958e-01, 'w_mix_in': 1.442972e-01, 'b_forget': 2.781671e-01, 'w_pool': 2.145941e-01, 'pool_scale': 2.246565e-01, 'conv_w': 2.343003e-01, 'w_mix_out': 1.623037e-01, 'norm_ffn2': 9.567511e-02, 'w_ffn2_in': 3.944739e-02, 'w_ffn2_out': 6.465829e-02, 'norm_final': 6.439864e+01}


def _to_microbatches(a, axis):
    t = _jnp.moveaxis(a, axis, 0)
    t = t.reshape((N_MICROBATCH, t.shape[0] // N_MICROBATCH) + t.shape[1:])
    return _jnp.moveaxis(t, 1, axis + 1)


def setup_inputs(seed: int = 0) -> dict:
    inp = _fwd_setup_inputs(seed)
    key = _jax.random.fold_in(_jax.random.key(seed), 7919)
    shape, _ = _output_shape()
    out = dict(inp)
    out["loss_target"] = _jax.random.normal(_jax.random.fold_in(key, 0), shape, _jnp.float32)
    for i, name in enumerate(TWIN_WEIGHTS):
        w = inp[name].astype(_jnp.float32)
        if MOMENT_SCALE is None:
            s = _jnp.sqrt(_jnp.mean(_jnp.square(w)) + 1e-30)
        else:
            s = MOMENT_SCALE[name]
        km, kv = _jax.random.split(_jax.random.fold_in(key, i + 1))
        out[name] = w
        out["m_" + name] = s * _jax.random.normal(km, w.shape, _jnp.float32)
        out["v_" + name] = (s * s) * _jax.random.uniform(kv, w.shape, _jnp.float32, 0.5, 1.5)
    if N_MICROBATCH > 1:
        for name, axis in PER_EXAMPLE_BATCH_AXIS.items():
            out[name] = _to_microbatches(out[name], axis)
    return {'x': out['x'], 'norm_ffn1': out['norm_ffn1'], 'w_ffn1_in': out['w_ffn1_in'], 'w_ffn1_out': out['w_ffn1_out'], 'norm_mix': out['norm_mix'], 'w_mix_in': out['w_mix_in'], 'b_forget': out['b_forget'], 'w_pool': out['w_pool'], 'pool_scale': out['pool_scale'], 'conv_w': out['conv_w'], 'w_mix_out': out['w_mix_out'], 'norm_ffn2': out['norm_ffn2'], 'w_ffn2_in': out['w_ffn2_in'], 'w_ffn2_out': out['w_ffn2_out'], 'norm_final': out['norm_final'], 'loss_target': out['loss_target'], 'm_norm_ffn1': out['m_norm_ffn1'], 'm_w_ffn1_in': out['m_w_ffn1_in'], 'm_w_ffn1_out': out['m_w_ffn1_out'], 'm_norm_mix': out['m_norm_mix'], 'm_w_mix_in': out['m_w_mix_in'], 'm_b_forget': out['m_b_forget'], 'm_w_pool': out['m_w_pool'], 'm_pool_scale': out['m_pool_scale'], 'm_conv_w': out['m_conv_w'], 'm_w_mix_out': out['m_w_mix_out'], 'm_norm_ffn2': out['m_norm_ffn2'], 'm_w_ffn2_in': out['m_w_ffn2_in'], 'm_w_ffn2_out': out['m_w_ffn2_out'], 'm_norm_final': out['m_norm_final'], 'v_norm_ffn1': out['v_norm_ffn1'], 'v_w_ffn1_in': out['v_w_ffn1_in'], 'v_w_ffn1_out': out['v_w_ffn1_out'], 'v_norm_mix': out['v_norm_mix'], 'v_w_mix_in': out['v_w_mix_in'], 'v_b_forget': out['v_b_forget'], 'v_w_pool': out['v_w_pool'], 'v_pool_scale': out['v_pool_scale'], 'v_conv_w': out['v_conv_w'], 'v_w_mix_out': out['v_w_mix_out'], 'v_norm_ffn2': out['v_norm_ffn2'], 'v_w_ffn2_in': out['v_w_ffn2_in'], 'v_w_ffn2_out': out['v_w_ffn2_out'], 'v_norm_final': out['v_norm_final']}


def _loss(weights, diff, rest, loss_target):
    with _jax.named_scope("forward"):
        args = {**rest, TWIN_DIFF_INPUT: diff, **{k: w.astype(_WEIGHT_DTYPES[k]) for k, w in weights.items()}}
        y = _forward(args)
    with _jax.named_scope("loss_head"):
        err = _jnp.square(y.astype(_jnp.float32) - loss_target)
        return 0.5 * _jnp.sum(_jnp.mean(err, axis=-1)) if err.ndim else 0.5 * err


def _adamw(w, g, m, v):
    m = ADAM_B1 * m + (1.0 - ADAM_B1) * g
    v = ADAM_B2 * v + (1.0 - ADAM_B2) * _jnp.square(g)
    m_hat = m / (1.0 - ADAM_B1 ** ADAM_STEP)
    v_hat = v / (1.0 - ADAM_B2 ** ADAM_STEP)
    delta = -ADAM_LR * (m_hat / (_jnp.sqrt(v_hat) + ADAM_EPS) + ADAM_WD * w)
    return delta, m, v


def reference(x, norm_ffn1, w_ffn1_in, w_ffn1_out, norm_mix, w_mix_in, b_forget, w_pool, pool_scale, conv_w, w_mix_out, norm_ffn2, w_ffn2_in, w_ffn2_out, norm_final, loss_target, m_norm_ffn1, m_w_ffn1_in, m_w_ffn1_out, m_norm_mix, m_w_mix_in, m_b_forget, m_w_pool, m_pool_scale, m_conv_w, m_w_mix_out, m_norm_ffn2, m_w_ffn2_in, m_w_ffn2_out, m_norm_final, v_norm_ffn1, v_w_ffn1_in, v_w_ffn1_out, v_norm_mix, v_w_mix_in, v_b_forget, v_w_pool, v_pool_scale, v_conv_w, v_w_mix_out, v_norm_ffn2, v_w_ffn2_in, v_w_ffn2_out, v_norm_final):
    given = dict(x=x, norm_ffn1=norm_ffn1, w_ffn1_in=w_ffn1_in, w_ffn1_out=w_ffn1_out, norm_mix=norm_mix, w_mix_in=w_mix_in, b_forget=b_forget, w_pool=w_pool, pool_scale=pool_scale, conv_w=conv_w, w_mix_out=w_mix_out, norm_ffn2=norm_ffn2, w_ffn2_in=w_ffn2_in, w_ffn2_out=w_ffn2_out, norm_final=norm_final, loss_target=loss_target, m_norm_ffn1=m_norm_ffn1, m_w_ffn1_in=m_w_ffn1_in, m_w_ffn1_out=m_w_ffn1_out, m_norm_mix=m_norm_mix, m_w_mix_in=m_w_mix_in, m_b_forget=m_b_forget, m_w_pool=m_w_pool, m_pool_scale=m_pool_scale, m_conv_w=m_conv_w, m_w_mix_out=m_w_mix_out, m_norm_ffn2=m_norm_ffn2, m_w_ffn2_in=m_w_ffn2_in, m_w_ffn2_out=m_w_ffn2_out, m_norm_final=m_norm_final, v_norm_ffn1=v_norm_ffn1, v_w_ffn1_in=v_w_ffn1_in, v_w_ffn1_out=v_w_ffn1_out, v_norm_mix=v_norm_mix, v_w_mix_in=v_w_mix_in, v_b_forget=v_b_forget, v_w_pool=v_w_pool, v_pool_scale=v_pool_scale, v_conv_w=v_conv_w, v_w_mix_out=v_w_mix_out, v_norm_ffn2=v_norm_ffn2, v_w_ffn2_in=v_w_ffn2_in, v_w_ffn2_out=v_w_ffn2_out, v_norm_final=v_norm_final)
    weights = {n: given[n] for n in TWIN_WEIGHTS}
    shared = {n: given[n] for n in SHARED_INPUTS}
    per_example = {n: given[n] for n in ['x']}
    grad_fn = _jax.value_and_grad(_loss, argnums=(0, 1))

    def one_microbatch(ex, loss_target):
        ex = dict(ex)
        diff = ex.pop(TWIN_DIFF_INPUT)
        return grad_fn(weights, diff, {**shared, **ex}, loss_target)

    if N_MICROBATCH == 1:
        loss, (grad_w, grad_x) = one_microbatch(per_example, given["loss_target"])
    else:
        def body(carry, xs):
            loss_sum, grad_sum = carry
            l_k, (gw_k, gx_k) = one_microbatch(xs[0], xs[1])
            with _jax.named_scope("update"):
                return (loss_sum + l_k, _jax.tree.map(_jnp.add, grad_sum, gw_k)), gx_k

        init = (_jnp.zeros((), _jnp.float32), _jax.tree.map(_jnp.zeros_like, weights))
        (loss, grad_w), grad_x = _jax.lax.scan(body, init, (per_example, given["loss_target"]))
    with _jax.named_scope("update"):
        delta_w, new_m, new_v = {}, {}, {}
        for n in TWIN_WEIGHTS:
            delta_w[n], new_m[n], new_v[n] = _adamw(weights[n], grad_w[n], given["m_" + n], given["v_" + n])
    return (loss, grad_x, *[grad_w[n] for n in TWIN_WEIGHTS], *[delta_w[n] for n in TWIN_WEIGHTS],
            *[new_m[n] for n in TWIN_WEIGHTS], *[new_v[n] for n in TWIN_WEIGHTS])
```

```python
import functools
import math

import jax
import jax.numpy as jnp
from jax import lax
from jax.experimental import pallas as pl
from jax.experimental.pallas import tpu as pltpu

F32 = jnp.float32
BF16 = jnp.bfloat16
MESH = pl.DeviceIdType.MESH

HEAD_DIM = 64
POOL_WINDOWS = (2, 4, 8, 16)
CONV_WIDTH = 3
RMS_EPS = 1e-6
ADAM_LR = 0.001
ADAM_B1 = 0.9
ADAM_B2 = 0.999
ADAM_EPS = 1e-08
ADAM_WD = 0.01
ADAM_STEP = 10

LANES = 128
VMEM_LIMIT = 56 * 1024 * 1024
N_CHIPS = 4
N_DEV = 8

NN = (((1,), (0,)), ((), ()))
NT = (((1,), (1,)), ((), ()))
TN = (((0,), (0,)), ((), ()))


def _tile(n, pref):
    t = pref
    while t >= 8:
        if n % t == 0:
            return t
        t //= 2
    return n


def _params(n_grid):
    return pltpu.CompilerParams(dimension_semantics=("arbitrary",) * n_grid, vmem_limit_bytes=VMEM_LIMIT)


def _dot(a, b, dims):
    return lax.dot_general(a, b, dims, preferred_element_type=F32)


def _mm(name, dims, operands, in_specs, out_shape, out_specs, grid, acc_shape, epilogue):
    n_in, n_out, nk = len(operands), len(out_shape), grid[-1]

    def kern(*refs):
        extras, outs = refs[2:n_in], refs[n_in:n_in + n_out]
        part = _dot(refs[0][...].astype(BF16), refs[1][...].astype(BF16), dims)
        if nk == 1:
            epilogue(part, extras, outs)
            return
        acc = refs[n_in + n_out]
        k = pl.program_id(len(grid) - 1)

        @pl.when(k == 0)
        def _():
            acc[...] = part

        @pl.when(k > 0)
        def _():
            acc[...] += part

        @pl.when(k == nk - 1)
        def _():
            epilogue(acc[...], extras, outs)

    return pl.pallas_call(
        kern, name=name, grid=grid, in_specs=in_specs, out_specs=out_specs, out_shape=out_shape,
        scratch_shapes=[pltpu.VMEM(acc_shape, F32)] if nk > 1 else [],
        compiler_params=_params(len(grid)),
    )(*operands)


def _store(scale=None, dtype=None):
    def ep(acc, extras, outs):
        v = acc if scale is None else acc * scale
        outs[0][...] = v.astype(outs[0].dtype)
    return ep


def _residual(scale):
    def ep(acc, extras, outs):
        outs[0][...] = extras[0][...] + scale * acc
    return ep


def _rmsnorm_fwd(x, g):
    T, D = x.shape
    tr = _tile(T, 512)

    def kern(x_ref, g_ref, o_ref):
        xv = x_ref[...]
        r = lax.rsqrt(jnp.mean(xv * xv, axis=-1, keepdims=True) + RMS_EPS)
        o_ref[...] = (xv * r * g_ref[...]).astype(BF16)

    return pl.pallas_call(
        kern, name="rmsnorm_fwd", grid=(T // tr,),
        in_specs=[pl.BlockSpec((tr, D), lambda i: (i, 0)), pl.BlockSpec((1, D), lambda i: (0, 0))],
        out_specs=pl.BlockSpec((tr, D), lambda i: (i, 0)),
        out_shape=jax.ShapeDtypeStruct((T, D), BF16), compiler_params=_params(1),
    )(x, g)


def _rmsnorm_bwd(x, g, dh, dres):
    T, D = x.shape
    tr = _tile(T, 256)

    def kern(x_ref, g_ref, dh_ref, dres_ref, dx_ref, dg_ref):
        xv, dhv = x_ref[...], dh_ref[...]
        r = lax.rsqrt(jnp.mean(xv * xv, axis=-1, keepdims=True) + RMS_EPS)
        y = xv * r
        dy = dhv * g_ref[...]
        dx_ref[...] = dres_ref[...] + r * (dy - y * jnp.mean(dy * y, axis=-1, keepdims=True))
        part = jnp.sum(dhv * y, axis=0, keepdims=True)

        @pl.when(pl.program_id(0) == 0)
        def _():
            dg_ref[...] = part

        @pl.when(pl.program_id(0) > 0)
        def _():
            dg_ref[...] += part

    row = pl.BlockSpec((tr, D), lambda i: (i, 0))
    vec = pl.BlockSpec((1, D), lambda i: (0, 0))
    return pl.pallas_call(
        kern, name="rmsnorm_bwd", grid=(T // tr,), in_specs=[row, vec, row, row], out_specs=[row, vec],
        out_shape=[jax.ShapeDtypeStruct((T, D), F32), jax.ShapeDtypeStruct((1, D), F32)],
        compiler_params=_params(1),
    )(x, g, dh, dres)


def _final_loss(x, g, target):
    T, D = x.shape
    tr = _tile(T, 256)

    def kern(x_ref, g_ref, t_ref, dx_ref, dg_ref, loss_ref):
        xv = x_ref[...]
        r = lax.rsqrt(jnp.mean(xv * xv, axis=-1, keepdims=True) + RMS_EPS)
        y = xv * r
        err = y * g_ref[...] - t_ref[...]
        lpart = 0.5 * jnp.sum(jnp.mean(err * err, axis=-1, keepdims=True), axis=0, keepdims=True)
        dh = err * (1.0 / D)
        dy = dh * g_ref[...]
        dx_ref[...] = r * (dy - y * jnp.mean(dy * y, axis=-1, keepdims=True))
        part = jnp.sum(dh * y, axis=0, keepdims=True)
        lrow = jnp.broadcast_to(lpart, (1, LANES))

        @pl.when(pl.program_id(0) == 0)
        def _():
            dg_ref[...] = part
            loss_ref[...] = lrow

        @pl.when(pl.program_id(0) > 0)
        def _():
            dg_ref[...] += part
            loss_ref[...] += lrow

    row = pl.BlockSpec((tr, D), lambda i: (i, 0))
    vec = pl.BlockSpec((1, D), lambda i: (0, 0))
    return pl.pallas_call(
        kern, name="final_loss", grid=(T // tr,), in_specs=[row, vec, row],
        out_specs=[row, vec, pl.BlockSpec((1, LANES), lambda i: (0, 0))],
        out_shape=[jax.ShapeDtypeStruct((T, D), F32), jax.ShapeDtypeStruct((1, D), F32),
                   jax.ShapeDtypeStruct((1, LANES), F32)],
        compiler_params=_params(1),
    )(x, g, target)


def _ffn_in(h, w4):
    T, D = h.shape
    Fh = w4.shape[2]
    F = 2 * Fh
    tm = _tile(T, 512)

    def kern(h_ref, wg_ref, wu_ref, gu_ref, act_ref):
        hv = h_ref[...]
        gate = _dot(hv, wg_ref[...], NN)
        up = _dot(hv, wu_ref[...], NN)
        gu_ref[0] = gate.astype(BF16)
        gu_ref[1] = up.astype(BF16)
        act_ref[...] = (gate * jax.nn.sigmoid(gate) * up).astype(BF16)

    return pl.pallas_call(
        kern, name="ffn_in", grid=(2, T // tm),
        in_specs=[pl.BlockSpec((tm, D), lambda j, i: (i, 0)),
                  pl.BlockSpec((None, D, Fh), lambda j, i: (j, 0, 0)),
                  pl.BlockSpec((None, D, Fh), lambda j, i: (2 + j, 0, 0))],
        out_specs=[pl.BlockSpec((2, tm, Fh), lambda j, i: (0, i, j)),
                   pl.BlockSpec((tm, Fh), lambda j, i: (i, j))],
        out_shape=[jax.ShapeDtypeStruct((2, T, F), BF16), jax.ShapeDtypeStruct((T, F), BF16)],
        compiler_params=_params(2),
    )(h, w4, w4)


def _ffn_out(act, w_out, x):
    T, F = act.shape
    D = w_out.shape[1]
    tm = _tile(T, 512)
    return _mm("ffn_out", NN, [act, w_out, x],
               [pl.BlockSpec((tm, F), lambda i, k: (i, 0)), pl.BlockSpec((F, D), lambda i, k: (0, 0)),
                pl.BlockSpec((tm, D), lambda i, k: (i, 0))],
               [jax.ShapeDtypeStruct((T, D), F32)], [pl.BlockSpec((tm, D), lambda i, k: (i, 0))],
               (T // tm, 1), None, _residual(0.5))


def _ffn_bwd_act(dres, w_out, gu):
    T, D = dres.shape
    F = w_out.shape[0]
    Fh = F // 2
    tm = _tile(T, 512)

    def kern(d_ref, w_ref, gu_ref, o_ref):
        dact = 0.5 * _dot(d_ref[...].astype(BF16), w_ref[...], NT)
        gate = gu_ref[0].astype(F32)
        up = gu_ref[1].astype(F32)
        sg = jax.nn.sigmoid(gate)
        o_ref[0] = (dact * up * (sg * (1.0 + gate * (1.0 - sg)))).astype(BF16)
        o_ref[1] = (dact * gate * sg).astype(BF16)

    return pl.pallas_call(
        kern, name="ffn_bwd_act", grid=(2, T // tm),
        in_specs=[pl.BlockSpec((tm, D), lambda j, i: (i, 0)), pl.BlockSpec((Fh, D), lambda j, i: (j, 0)),
                  pl.BlockSpec((2, tm, Fh), lambda j, i: (0, i, j))],
        out_specs=pl.BlockSpec((2, tm, Fh), lambda j, i: (0, i, j)),
        out_shape=jax.ShapeDtypeStruct((2, T, F), BF16), compiler_params=_params(2),
    )(dres, w_out, gu)


def _ffn_dw_out(act, dres):
    T, F = act.shape
    D = dres.shape[1]
    tm, tk = F // 2, _tile(T, 512)
    return _mm("ffn_dw_out", TN, [act, dres],
               [pl.BlockSpec((tk, tm), lambda i, k: (k, i)), pl.BlockSpec((tk, D), lambda i, k: (k, 0))],
               [jax.ShapeDtypeStruct((F, D), BF16)], [pl.BlockSpec((tm, D), lambda i, k: (i, 0))],
               (2, T // tk), (tm, D), _store(0.5))


def _ffn_dw_in(h, dgu):
    T, D = h.shape
    Fh = dgu.shape[2] // 2
    tk = _tile(T, 512)
    return _mm("ffn_dw_in", TN, [h, dgu],
               [pl.BlockSpec((tk, D), lambda j, k: (k, 0)),
                pl.BlockSpec((None, tk, Fh), lambda j, k: (j // 2, k, j % 2))],
               [jax.ShapeDtypeStruct((4, D, Fh), BF16)], [pl.BlockSpec((None, D, Fh), lambda j, k: (j, 0, 0))],
               (4, T // tk), (D, Fh), _store())


def _ffn_dh(dgu, w4):
    T = dgu.shape[1]
    D, Fh = w4.shape[1], w4.shape[2]
    tm = _tile(T, 512)
    return _mm("ffn_dh", NT, [dgu, w4],
               [pl.BlockSpec((None, tm, Fh), lambda i, k: (k // 2, i, k % 2)),
                pl.BlockSpec((None, D, Fh), lambda i, k: (k, 0, 0))],
               [jax.ShapeDtypeStruct((T, D), F32)], [pl.BlockSpec((tm, D), lambda i, k: (i, 0))],
               (T // tm, 4), (tm, D), _store())


def _proj(name, a, w, out_dtype, dims=NN, extra=None, scale=None):
    T, K = a.shape
    N = w.shape[1] if dims == NN else w.shape[0]
    tm = _tile(T, 512)
    ops = [a, w] + ([extra] if extra is not None else [])
    specs = [pl.BlockSpec((tm, K), lambda i, k: (i, 0)), pl.BlockSpec(w.shape, lambda i, k: (0, 0))]
    if extra is not None:
        specs.append(pl.BlockSpec((tm, N), lambda i, k: (i, 0)))
    ep = _residual(1.0) if extra is not None else _store(scale)
    return _mm(name, dims, ops, specs, [jax.ShapeDtypeStruct((T, N), out_dtype)],
               [pl.BlockSpec((tm, N), lambda i, k: (i, 0))], (T // tm, 1), None, ep)[0]


def _dw(name, a, d, out_dtype):
    T, M = a.shape
    N = d.shape[1]
    tk = _tile(T, 512)
    return _mm(name, TN, [a, d],
               [pl.BlockSpec((tk, M), lambda i, k: (k, 0)), pl.BlockSpec((tk, N), lambda i, k: (k, 0))],
               [jax.ShapeDtypeStruct((M, N), out_dtype)], [pl.BlockSpec((M, N), lambda i, k: (0, 0))],
               (1, T // tk), (M, N), _store())[0]


def _log_sigmoid(z):
    return jnp.minimum(z, 0.0) - jnp.log(1.0 + jnp.exp(-jnp.abs(z)))


def _decay_fwd(fl, bias):
    B, S, _ = fl.shape

    def kern(fl_ref, b_ref, o_ref):
        d = _log_sigmoid(fl_ref[...] + b_ref[...])
        row = lax.broadcasted_iota(jnp.int32, (S, LANES), 0)
        sh = 1
        while sh < S:
            d = d + jnp.where(row >= sh, pltpu.roll(d, sh, 0), 0.0)
            sh *= 2
        o_ref[...] = d.T[0:8, :]

    return pl.pallas_call(
        kern, name="decay_fwd", grid=(B,),
        in_specs=[pl.BlockSpec((None, S, LANES), lambda b: (b, 0, 0)), pl.BlockSpec((1, LANES), lambda b: (0, 0))],
        out_specs=pl.BlockSpec((None, 8, S), lambda b: (b, 0, 0)),
        out_shape=jax.ShapeDtypeStruct((B, 8, S), F32), compiler_params=_params(1),
    )(fl, bias)


def _decay_bwd(ddrow, ddcol, fl, bias, n_heads):
    B, S, _ = fl.shape

    def kern(dd_ref, ddc_ref, fl_ref, b_ref, dfl_ref, db_ref):
        dd = jnp.concatenate([dd_ref[...], jnp.zeros((LANES - 8, S), F32)], axis=0).T + ddc_ref[...]
        row = lax.broadcasted_iota(jnp.int32, (S, LANES), 0)
        lane = lax.broadcasted_iota(jnp.int32, (S, LANES), 1)
        sh = 1
        while sh < S:
            dd = dd + jnp.where(row < S - sh, pltpu.roll(dd, S - sh, 0), 0.0)
            sh *= 2
        z = fl_ref[...] + b_ref[...]
        dfl = jnp.where(lane < n_heads, dd / (1.0 + jnp.exp(z)), 0.0)
        dfl_ref[...] = dfl
        part = jnp.sum(dfl, axis=0, keepdims=True)

        @pl.when(pl.program_id(0) == 0)
        def _():
            db_ref[...] = part

        @pl.when(pl.program_id(0) > 0)
        def _():
            db_ref[...] += part

    return pl.pallas_call(
        kern, name="decay_bwd", grid=(B,),
        in_specs=[pl.BlockSpec((None, 8, S), lambda b: (b, 0, 0)), pl.BlockSpec((None, S, LANES), lambda b: (b, 0, 0)),
                  pl.BlockSpec((None, S, LANES), lambda b: (b, 0, 0)), pl.BlockSpec((1, LANES), lambda b: (0, 0))],
        out_specs=[pl.BlockSpec((None, S, LANES), lambda b: (b, 0, 0)), pl.BlockSpec((1, LANES), lambda b: (0, 0))],
        out_shape=[jax.ShapeDtypeStruct((B, S, LANES), F32), jax.ShapeDtypeStruct((1, LANES), F32)],
        compiler_params=_params(1),
    )(ddrow, ddcol, fl, bias)


def _attn_fwd(qkv, drow, n_heads, tq):
    B, S, _ = qkv.shape
    DA = n_heads * HEAD_DIM
    scale = HEAD_DIM ** -0.5

    def kern(q_ref, k_ref, v_ref, dr_ref, o_ref, lse_ref):
        i = pl.program_id(1)
        lane = lax.broadcasted_iota(jnp.int32, (tq, LANES), 1)
        low = lane < HEAD_DIM
        causal = lax.broadcasted_iota(jnp.int32, (tq, tq), 1) <= lax.broadcasted_iota(jnp.int32, (tq, tq), 0)
        lse_mat = jnp.zeros((tq, LANES), F32)
        for p in range(n_heads // 2):
            cols = slice(LANES * p, LANES * (p + 1))
            q2 = q_ref[:, cols] * scale
            outs = []
            for hh in range(2):
                h = 2 * p + hh
                qm = jnp.where(low if hh == 0 else ~low, q2, jnp.zeros_like(q2))

                def step(j, carry, masked, h=h, qm=qm, cols=cols):
                    m, l, acc = carry
                    ks = pl.multiple_of(j * tq, tq)
                    s = _dot(qm, k_ref[pl.ds(ks, tq), cols], NT) - dr_ref[h, pl.ds(j, 1), :]
                    if masked:
                        s = jnp.where(causal, s, -jnp.inf)
                    m_new = jnp.maximum(m, jnp.max(s, axis=1, keepdims=True))
                    alpha = jnp.exp(m - m_new)
                    pm = jnp.exp(s - m_new)
                    l = alpha * l + jnp.sum(pm, axis=1, keepdims=True)
                    acc = alpha * acc + _dot(pm.astype(BF16), v_ref[pl.ds(ks, tq), cols], NN)
                    return m_new, l, acc

                init = (jnp.full((tq, 1), -jnp.inf, F32), jnp.zeros((tq, 1), F32), jnp.zeros((tq, LANES), F32))
                carry = lax.fori_loop(0, i, functools.partial(step, masked=False), init)
                m, l, acc = step(i, carry, True)
                outs.append(acc / l)
                lse_mat = jnp.where(lane == h, m + jnp.log(l), lse_mat)
            o_ref[:, cols] = jnp.where(low, outs[0], outs[1]).astype(BF16)
        lse_ref[...] = lse_mat

    nq = S // tq
    return pl.pallas_call(
        kern, name="attn_fwd", grid=(B, nq),
        in_specs=[pl.BlockSpec((None, tq, DA), lambda b, i: (b, i, 0)),
                  pl.BlockSpec((None, S, DA), lambda b, i: (b, 0, 1)),
                  pl.BlockSpec((None, S, DA), lambda b, i: (b, 0, 2)),
                  pl.BlockSpec((None, 8, nq, tq), lambda b, i: (b, 0, 0, 0))],
        out_specs=[pl.BlockSpec((None, tq, DA), lambda b, i: (b, i, 0)),
                   pl.BlockSpec((None, tq, LANES), lambda b, i: (b, i, 0))],
        out_shape=[jax.ShapeDtypeStruct((B, S, DA), BF16), jax.ShapeDtypeStruct((B, S, LANES), F32)],
        compiler_params=_params(2),
    )(qkv, qkv, qkv, drow)


def _attn_bwd(qkv, drow, o, lse, dycat, n_heads, tq):
    B, S, _ = qkv.shape
    DA = n_heads * HEAD_DIM
    scale = HEAD_DIM ** -0.5
    nq = S // tq

    def kern(q_ref, k_ref, v_ref, dr_ref, o_ref, lse_ref, do_ref, dq_ref, dk_ref, dv_ref, ddr_ref, ddc_ref, dk_acc, dv_acc):
        i = pl.program_id(1)

        @pl.when(i == 0)
        def _():
            dk_acc[...] = jnp.zeros_like(dk_acc)
            dv_acc[...] = jnp.zeros_like(dv_acc)
            ddr_ref[...] = jnp.zeros_like(ddr_ref)

        lane = lax.broadcasted_iota(jnp.int32, (tq, LANES), 1)
        low = lane < HEAD_DIM
        causal = lax.broadcasted_iota(jnp.int32, (tq, tq), 1) <= lax.broadcasted_iota(jnp.int32, (tq, tq), 0)
        ddc = jnp.zeros((tq, LANES), F32)
        for p in range(n_heads // 2):
            cols = slice(LANES * p, LANES * (p + 1))
            q2 = q_ref[:, cols] * scale
            do_f = do_ref[:, cols]
            do2 = do_f.astype(BF16)
            prod = do_f * o_ref[:, cols].astype(F32)
            dqs = []
            for hh in range(2):
                h = 2 * p + hh
                msk = low if hh == 0 else ~low
                qm = jnp.where(msk, q2, jnp.zeros_like(q2))
                dom = jnp.where(msk, do2, jnp.zeros_like(do2))
                delta = jnp.sum(jnp.where(msk, prod, 0.0), axis=1, keepdims=True)
                lse_h = lse_ref[:, h:h + 1]

                def step(j, carry, masked, h=h, qm=qm, dom=dom, delta=delta, lse_h=lse_h, cols=cols):
                    dq, rs = carry
                    ks = pl.multiple_of(j * tq, tq)
                    k2 = k_ref[pl.ds(ks, tq), cols]
                    s = _dot(qm, k2, NT) - dr_ref[h, pl.ds(j, 1), :]
                    if masked:
                        s = jnp.where(causal, s, -jnp.inf)
                    pm = jnp.exp(s - lse_h)
                    ds = pm * (_dot(dom, v_ref[pl.ds(ks, tq), cols], NT) - delta)
                    ddr_ref[h, pl.ds(j, 1), :] -= jnp.sum(ds, axis=0, keepdims=True)
                    dsb = ds.astype(BF16)
                    dv_acc[pl.ds(ks, tq), cols] += _dot(pm.astype(BF16), dom, TN)
                    dk_acc[pl.ds(ks, tq), cols] += _dot(dsb, qm, TN)
                    return dq + _dot(dsb, k2, NN), rs + jnp.sum(ds, axis=1, keepdims=True)

                init = (jnp.zeros((tq, LANES), F32), jnp.zeros((tq, 1), F32))
                dq, rs = step(i, lax.fori_loop(0, i, functools.partial(step, masked=False), init), True)
                dqs.append(dq)
                ddc = jnp.where(lane == h, rs, ddc)
            dq_ref[:, cols] = (jnp.where(low, dqs[0], dqs[1]) * scale).astype(BF16)
        ddc_ref[...] = ddc

        @pl.when(i == nq - 1)
        def _():
            dk_ref[...] = dk_acc[...].astype(BF16)
            dv_ref[...] = dv_acc[...].astype(BF16)

    tile = pl.BlockSpec((None, tq, DA), lambda b, i: (b, i, 0))
    seq = pl.BlockSpec((None, S, DA), lambda b, i: (b, 0, 0))
    dec = pl.BlockSpec((None, 8, nq, tq), lambda b, i: (b, 0, 0, 0))
    return pl.pallas_call(
        kern, name="attn_bwd", grid=(B, nq),
        in_specs=[tile, pl.BlockSpec((None, S, DA), lambda b, i: (b, 0, 1)),
                  pl.BlockSpec((None, S, DA), lambda b, i: (b, 0, 2)), dec, tile,
                  pl.BlockSpec((None, tq, LANES), lambda b, i: (b, i, 0)), tile],
        out_specs=[tile, seq, seq, dec, pl.BlockSpec((None, tq, LANES), lambda b, i: (b, i, 0))],
        out_shape=[jax.ShapeDtypeStruct((B, S, DA), BF16)] * 3 + [jax.ShapeDtypeStruct((B, 8, nq, tq), F32),
                                                                  jax.ShapeDtypeStruct((B, S, LANES), F32)],
        scratch_shapes=[pltpu.VMEM((S, DA), F32), pltpu.VMEM((S, DA), F32)],
        compiler_params=_params(2),
    )(qkv, qkv, qkv, drow, o, lse, dycat)


def _down(v, d, row):
    return jnp.where(row >= d, pltpu.roll(v, d, 0), 0.0)


def _up(v, d, row, S):
    return jnp.where(row < S - d, pltpu.roll(v, S - d, 0), 0.0)


def _window(v, shift, group):
    sums, acc, d = [], v, 1
    for _ in POOL_WINDOWS:
        acc = acc + shift(acc, d)
        sums.append(acc)
        d *= 2
    out = sums[-1]
    for gi in range(len(POOL_WINDOWS) - 2, -1, -1):
        out = jnp.where(group == gi, sums[gi], out)
    return out


def _pool_count(row, group):
    w = jnp.full(row.shape, POOL_WINDOWS[-1], jnp.int32)
    for gi in range(len(POOL_WINDOWS) - 2, -1, -1):
        w = jnp.where(group == gi, POOL_WINDOWS[gi], w)
    return jnp.minimum(row + 1, w).astype(F32)


def _mix_local_fwd(rest, wbd, ps, cw):
    B, S, C4 = rest.shape
    C = C4 // 4
    gw = C // len(POOL_WINDOWS)

    def kern(r_ref, w_ref, ps_ref, cw_ref, y_ref, pooled_ref):
        row = lax.broadcasted_iota(jnp.int32, (S, C), 0)
        group = lax.broadcasted_iota(jnp.int32, (S, C), 1) // gw
        u = r_ref[:, 0:C]
        pooled = _window(u, lambda v, d: _down(v, d, row), group) / _pool_count(row, group) - u
        pb = pooled.astype(BF16)
        pooled_ref[...] = pb
        y_ref[:, 0:C] = (_dot(pb, w_ref[...], NN) * ps_ref[...]).astype(BF16)
        uc = r_ref[:, 2 * C:3 * C] * r_ref[:, 3 * C:4 * C]
        y = cw_ref[0:1, :] * _down(uc, 2, row) + cw_ref[1:2, :] * _down(uc, 1, row) + cw_ref[2:3, :] * uc
        y_ref[:, C:2 * C] = (r_ref[:, C:2 * C] * y).astype(BF16)

    return pl.pallas_call(
        kern, name="mix_local_fwd", grid=(B,),
        in_specs=[pl.BlockSpec((None, S, C4), lambda b: (b, 0, 0)), pl.BlockSpec((C, C), lambda b: (0, 0)),
                  pl.BlockSpec((1, C), lambda b: (0, 0)), pl.BlockSpec((8, C), lambda b: (0, 0))],
        out_specs=[pl.BlockSpec((None, S, 2 * C), lambda b: (b, 0, 0)), pl.BlockSpec((None, S, C), lambda b: (b, 0, 0))],
        out_shape=[jax.ShapeDtypeStruct((B, S, 2 * C), BF16), jax.ShapeDtypeStruct((B, S, C), BF16)],
        compiler_params=_params(1),
    )(rest, wbd, ps, cw)


def _mix_local_bwd(rest, pooled, dycat, wbd, ps, cw):
    B, S, C4 = rest.shape
    C = C4 // 4
    gw = C // len(POOL_WINDOWS)

    def kern(r_ref, pooled_ref, d_ref, w_ref, ps_ref, cw_ref, dr_ref, dw_ref, dps_ref, dcw_ref):
        row = lax.broadcasted_iota(jnp.int32, (S, C), 0)
        group = lax.broadcasted_iota(jnp.int32, (S, C), 1) // gw
        dyp = d_ref[:, 0:C]
        dyc = d_ref[:, C:2 * C]
        pb = pooled_ref[...]
        dps = jnp.sum(dyp * _dot(pb, w_ref[...], NN), axis=0, keepdims=True)
        dzb = (dyp * ps_ref[...]).astype(BF16)
        dw = _dot(pb, dzb, TN)
        dpooled = _dot(dzb, w_ref[...], NT)
        g = dpooled / _pool_count(row, group)
        dr_ref[:, 0:C] = (_window(g, lambda v, d: _up(v, d, row, S), group) - dpooled).astype(BF16)
        cc, ch = r_ref[:, 2 * C:3 * C], r_ref[:, 3 * C:4 * C]
        uc = cc * ch
        u1, u2 = _down(uc, 1, row), _down(uc, 2, row)
        y = cw_ref[0:1, :] * u2 + cw_ref[1:2, :] * u1 + cw_ref[2:3, :] * uc
        dr_ref[:, C:2 * C] = (dyc * y).astype(BF16)
        dy = dyc * r_ref[:, C:2 * C]
        duc = cw_ref[0:1, :] * _up(dy, 2, row, S) + cw_ref[1:2, :] * _up(dy, 1, row, S) + cw_ref[2:3, :] * dy
        dr_ref[:, 2 * C:3 * C] = (duc * ch).astype(BF16)
        dr_ref[:, 3 * C:4 * C] = (duc * cc).astype(BF16)
        dcw = jnp.concatenate([jnp.sum(dy * u2, axis=0, keepdims=True), jnp.sum(dy * u1, axis=0, keepdims=True),
                               jnp.sum(dy * uc, axis=0, keepdims=True), jnp.zeros((5, C), F32)], axis=0)

        @pl.when(pl.program_id(0) == 0)
        def _():
            dw_ref[...] = dw
            dps_ref[...] = dps
            dcw_ref[...] = dcw

        @pl.when(pl.program_id(0) > 0)
        def _():
            dw_ref[...] += dw
            dps_ref[...] += dps
            dcw_ref[...] += dcw

    full = lambda shape: pl.BlockSpec(shape, lambda b: (0, 0))
    return pl.pallas_call(
        kern, name="mix_local_bwd", grid=(B,),
        in_specs=[pl.BlockSpec((None, S, C4), lambda b: (b, 0, 0)), pl.BlockSpec((None, S, C), lambda b: (b, 0, 0)),
                  pl.BlockSpec((None, S, 2 * C), lambda b: (b, 0, 1)), full((C, C)), full((1, C)), full((8, C))],
        out_specs=[pl.BlockSpec((None, S, C4), lambda b: (b, 0, 0)), full((C, C)), full((1, C)), full((8, C))],
        out_shape=[jax.ShapeDtypeStruct((B, S, C4), BF16), jax.ShapeDtypeStruct((C, C), F32),
                   jax.ShapeDtypeStruct((1, C), F32), jax.ShapeDtypeStruct((8, C), F32)],
        compiler_params=_params(1),
    )(rest, pooled, dycat, wbd, ps, cw)


def _adamw(w, g, m, v):
    R, C = w.shape
    tr = _tile(R, 512)

    def kern(w_ref, g_ref, m_ref, v_ref, d_ref, nm_ref, nv_ref):
        gv = g_ref[...]
        nm = ADAM_B1 * m_ref[...] + (1.0 - ADAM_B1) * gv
        nv = ADAM_B2 * v_ref[...] + (1.0 - ADAM_B2) * (gv * gv)
        m_hat = nm / (1.0 - ADAM_B1 ** ADAM_STEP)
        v_hat = nv / (1.0 - ADAM_B2 ** ADAM_STEP)
        d_ref[...] = -ADAM_LR * (m_hat / (jnp.sqrt(v_hat) + ADAM_EPS) + ADAM_WD * w_ref[...])
        nm_ref[...] = nm
        nv_ref[...] = nv

    blk = pl.BlockSpec((tr, C), lambda i: (i, 0))
    return pl.pallas_call(
        kern, name="adamw", grid=(R // tr,), in_specs=[blk] * 4, out_specs=[blk] * 3,
        out_shape=[jax.ShapeDtypeStruct((R, C), F32)] * 3, compiler_params=_params(1),
    )(w, g, m, v)


def _place():
    x, y, c = lax.axis_index("x"), lax.axis_index("y"), lax.axis_index("c")
    return x, y, c, [(1 - x, y), (x, 1 - y), (1 - x, 1 - y)]


def _comm_call(name, body, operands, out_shape, n_sems, scratch=()):
    any_spec = pl.BlockSpec(memory_space=pl.ANY)
    return pl.pallas_call(
        body, name=name, in_specs=[any_spec] * len(operands), out_specs=[any_spec] * len(out_shape),
        out_shape=out_shape,
        scratch_shapes=[pltpu.SemaphoreType.DMA((n,)) for n in n_sems] + list(scratch),
    )(*operands)


def _all_gather(shards):
    n, hl = len(shards), shards[0].shape[0] // 2

    def body(*refs):
        ins, outs = refs[:n], refs[n:2 * n]
        send_sems, recv_sems, local_sems = refs[2 * n:]
        x, y, c, chips = _place()
        sibling = (x, y, 1 - c)

        def remote(k, j, chip, half, to, src=None):
            region = outs[k].at[pl.ds(half * hl, hl), 2 * chip[0] + chip[1]]
            return pltpu.make_async_remote_copy(
                src_ref=region if src is None else src, dst_ref=region, send_sem=send_sems.at[6 * k + j],
                recv_sem=recv_sems.at[6 * k + j], device_id=to, device_id_type=MESH)

        local = [pltpu.make_async_copy(ins[k], outs[k].at[:, 2 * x + y], local_sems.at[k]) for k in range(n)]
        for cp in local:
            cp.start()
        first = [remote(k, j, (x, y), c, (*chip, c), src=ins[k].at[pl.ds(c * hl, hl)])
                 for k in range(n) for j, chip in enumerate(chips)]
        for cp in first:
            cp.start()
        passed = []
        for k in range(n):
            for j, chip in enumerate(chips):
                remote(k, j, chip, c, (x, y, c)).wait_recv()
                passed.append(remote(k, 3 + j, chip, c, sibling))
                passed[-1].start()
        for k in range(n):
            for j, chip in enumerate(chips):
                remote(k, 3 + j, chip, 1 - c, (x, y, c)).wait_recv()
        for cp in first + passed:
            cp.wait_send()
        for cp in local:
            cp.wait()

    out_shape = [jax.ShapeDtypeStruct((s.shape[0], N_CHIPS) + s.shape[1:], s.dtype) for s in shards]
    return _comm_call("all_gather_weights", body, shards, out_shape, (6 * n, 6 * n, n))


def _rs_swap_halves(grads):
    n, hl = len(grads), grads[0].shape[0] // 2

    def body(*refs):
        ins, outs = refs[:n], refs[n:2 * n]
        send_sems, recv_sems = refs[2 * n:]
        x, y, c, _ = _place()
        copies = [pltpu.make_async_remote_copy(
            src_ref=ins[k].at[pl.ds((1 - c) * hl, hl)], dst_ref=outs[k], send_sem=send_sems.at[k],
            recv_sem=recv_sems.at[k], device_id=(x, y, 1 - c), device_id_type=MESH) for k in range(n)]
        for cp in copies:
            cp.start()
        for cp in copies:
            cp.wait()

    out_shape = [jax.ShapeDtypeStruct((hl,) + g.shape[1:], g.dtype) for g in grads]
    return _comm_call("rs_swap_halves", body, grads, out_shape, (n, n))


def _rs_exchange(parts):
    n = len(parts)

    def body(*refs):
        ins, outs = refs[:n], refs[n:2 * n]
        send_sems, recv_sems = refs[2 * n:]
        x, y, c, chips = _place()
        copies = [pltpu.make_async_remote_copy(
            src_ref=ins[k].at[:, 2 * chip[0] + chip[1]], dst_ref=outs[k].at[j], send_sem=send_sems.at[3 * k + j],
            recv_sem=recv_sems.at[3 * k + j], device_id=(*chip, c), device_id_type=MESH)
            for k in range(n) for j, chip in enumerate(chips)]
        for cp in copies:
            cp.start()
        for cp in copies:
            cp.wait()

    out_shape = [jax.ShapeDtypeStruct((3, p.shape[0]) + p.shape[2:], p.dtype) for p in parts]
    return _comm_call("rs_exchange", body, parts, out_shape, (3 * n, 3 * n))


def _rs_share(halves):
    n, hl = len(halves), halves[0].shape[0]

    def body(*refs):
        ins, outs = refs[:n], refs[n:2 * n]
        send_sems, recv_sems, local_sems = refs[2 * n:]
        x, y, c, _ = _place()
        local = [pltpu.make_async_copy(ins[k], outs[k].at[pl.ds(c * hl, hl)], local_sems.at[k]) for k in range(n)]
        sends = [pltpu.make_async_remote_copy(
            src_ref=ins[k], dst_ref=outs[k].at[pl.ds(c * hl, hl)], send_sem=send_sems.at[k],
            recv_sem=recv_sems.at[k], device_id=(x, y, 1 - c), device_id_type=MESH) for k in range(n)]
        for cp in local + sends:
            cp.start()
        for k in range(n):
            pltpu.make_async_remote_copy(
                src_ref=ins[k], dst_ref=outs[k].at[pl.ds((1 - c) * hl, hl)], send_sem=send_sems.at[k],
                recv_sem=recv_sems.at[k], device_id=(x, y, 1 - c), device_id_type=MESH).wait_recv()
        for cp in sends:
            cp.wait_send()
        for cp in local:
            cp.wait()

    out_shape = [jax.ShapeDtypeStruct((2 * hl,) + h.shape[1:], h.dtype) for h in halves]
    return _comm_call("rs_share", body, halves, out_shape, (n, n, n))


def _all_reduce_small(v):
    n = v.shape[0]

    def body(v_ref, o_ref, gbuf, send_sems, recv_sems):
        x, y, c, _ = _place()
        me = 4 * x + 2 * y + c
        gbuf[me] = v_ref[...]
        copies, waits = [], []
        for r in range(1, N_DEV):
            px = 1 - x if r & 4 else x
            py = 1 - y if r & 2 else y
            pc = 1 - c if r & 1 else c
            mk = functools.partial(pltpu.make_async_remote_copy, src_ref=v_ref, send_sem=send_sems.at[r - 1],
                                   recv_sem=recv_sems.at[r - 1], device_id=(px, py, pc), device_id_type=MESH)
            copies.append(mk(dst_ref=gbuf.at[me]))
            waits.append(mk(dst_ref=gbuf.at[4 * px + 2 * py + pc]))
        for cp in copies:
            cp.start()
        for cp in waits:
            cp.wait_recv()
        for cp in copies:
            cp.wait_send()
        acc = gbuf[0]
        for d in range(1, N_DEV):
            acc = acc + gbuf[d]
        o_ref[...] = acc

    vm = pl.BlockSpec(memory_space=pltpu.VMEM)
    return pl.pallas_call(
        body, name="all_reduce_small", in_specs=[vm], out_specs=vm, out_shape=jax.ShapeDtypeStruct(v.shape, F32),
        scratch_shapes=[pltpu.VMEM((N_DEV, n, LANES), F32), pltpu.SemaphoreType.DMA((N_DEV - 1,)),
                        pltpu.SemaphoreType.DMA((N_DEV - 1,))],
        compiler_params=pltpu.CompilerParams(vmem_limit_bytes=VMEM_LIMIT),
    )(v)


def _add_half(g, h1, c):
    hl, nb, R, C = h1.shape
    g3, h3 = g.reshape(2 * hl, nb * R, C), h1.reshape(hl, nb * R, C)
    tr = _tile(nb * R, 512)

    def kern(c_ref, g_ref, h_ref, o_ref):
        o_ref[...] = (g_ref[...].astype(F32) + h_ref[...].astype(F32)).astype(BF16)

    blk = pl.BlockSpec((None, tr, C), lambda l, i, c_ref: (l, i, 0))
    out = pl.pallas_call(
        kern, name="rs_add_half",
        grid_spec=pltpu.PrefetchScalarGridSpec(
            num_scalar_prefetch=1, grid=(hl, nb * R // tr),
            in_specs=[pl.BlockSpec((None, tr, C), lambda l, i, c_ref: (c_ref[0] * hl + l, i, 0)), blk],
            out_specs=blk),
        out_shape=jax.ShapeDtypeStruct(h3.shape, BF16), compiler_params=_params(2),
    )(c, g3, h3)
    return out.reshape(h1.shape)


def _add_blocks(p, h2, b):
    hl, nb, R, C = p.shape
    tr = _tile(R, 512)

    def kern(b_ref, p_ref, h0_ref, h1_ref, h2_ref, o_ref):
        o_ref[...] = ((p_ref[...].astype(F32) + h0_ref[...].astype(F32)) + h1_ref[...].astype(F32)) + h2_ref[...].astype(F32)

    def other(j):
        return pl.BlockSpec((None, None, tr, C), lambda l, i, b_ref: (j, l, i, 0))

    return pl.pallas_call(
        kern, name="rs_add_blocks",
        grid_spec=pltpu.PrefetchScalarGridSpec(
            num_scalar_prefetch=1, grid=(hl, R // tr),
            in_specs=[pl.BlockSpec((None, None, tr, C), lambda l, i, b_ref: (l, b_ref[0], i, 0)),
                      other(0), other(1), other(2)],
            out_specs=pl.BlockSpec((None, tr, C), lambda l, i, b_ref: (l, i, 0))),
        out_shape=jax.ShapeDtypeStruct((hl, R, C), F32), compiler_params=_params(2),
    )(b, p, h2, h2, h2)


def _reduce_scatter(grads):
    x, y, c = lax.axis_index("x"), lax.axis_index("y"), lax.axis_index("c")
    cs = jnp.reshape(c, (1,)).astype(jnp.int32)
    bs = jnp.reshape(2 * x + y, (1,)).astype(jnp.int32)
    sib = _rs_swap_halves(grads)
    parts = [_add_half(g, h, cs) for g, h in zip(grads, sib)]
    others = _rs_exchange(parts)
    halves = [_add_blocks(p, o, bs) for p, o in zip(parts, others)]
    return _rs_share(halves)


WEIGHTS = ("norm_ffn1", "w_ffn1_in", "w_ffn1_out", "norm_mix", "w_mix_in", "b_forget", "w_pool", "pool_scale",
           "conv_w", "w_mix_out", "norm_ffn2", "w_ffn2_in", "w_ffn2_out", "norm_final")
BIG = ("w_ffn1_in", "w_ffn1_out", "w_mix_in", "w_mix_out", "w_ffn2_in", "w_ffn2_out")
SMALL = ("norm_ffn1", "norm_mix", "b_forget", "w_pool", "pool_scale", "conv_w", "norm_ffn2", "norm_final")


def _pad_lanes(a, width=LANES):
    return jnp.pad(a, ((0, 0), (0, width - a.shape[1])))


def _layer_params(l, small, gathered, D):
    DA, C, H = D // 2, D // 4, D // 2 // HEAD_DIM
    w_in = jnp.concatenate([gathered["w_mix_in"][l, b] for b in range(N_CHIPS)], axis=1)
    wqkv, wf, wrest = w_in[:, :3 * DA], _pad_lanes(w_in[:, 3 * DA:3 * DA + H]), w_in[:, 3 * DA + H:]
    gw = C // len(POOL_WINDOWS)
    wbd = jnp.zeros((C, C), F32)
    for gi in range(len(POOL_WINDOWS)):
        wbd = wbd.at[gi * gw:(gi + 1) * gw, gi * gw:(gi + 1) * gw].set(small["w_pool"][l, gi])
    cw = jnp.concatenate([gathered["conv_w"][l, b] for b in range(N_CHIPS)], axis=1)
    return dict(
        g1=small["norm_ffn1"][l][None], gm=small["norm_mix"][l][None], g2=small["norm_ffn2"][l][None],
        w1in=gathered["w_ffn1_in"][l], w2in=gathered["w_ffn2_in"][l],
        w1out=gathered["w_ffn1_out"][l].reshape(-1, D), w2out=gathered["w_ffn2_out"][l].reshape(-1, D),
        wqkv=wqkv, wrest=wrest, wf=wf, wp=jnp.concatenate([wqkv, wrest, wf], axis=1),
        wmixout=gathered["w_mix_out"][l].reshape(D, D),
        bias=_pad_lanes(small["b_forget"][l][None]), wbd=wbd.astype(BF16), ps=small["pool_scale"][l][None],
        cw=jnp.pad(cw, ((0, 8 - CONV_WIDTH), (0, 0))),
    )


def _ffn_fwd(x, g, w_in, w_out):
    h = _rmsnorm_fwd(x, g)
    gu, act = _ffn_in(h, w_in)
    return _ffn_out(act, w_out, x)[0], (x, h, gu, act)


def _ffn_bwd(dres, saved, g, w_in, w_out):
    x, h, gu, act = saved
    dgu = _ffn_bwd_act(dres, w_out, gu)
    dw_out = _ffn_dw_out(act, dres)[0]
    dw_in = _ffn_dw_in(h, dgu)[0]
    dh = _ffn_dh(dgu, w_in)[0]
    dx, dg = _rmsnorm_bwd(x, g, dh, dres)
    return dx, dg, dw_in, dw_out.reshape(N_CHIPS, -1, dw_out.shape[1])


def _mixer_fwd(x, P, B, S, tq):
    T, D = x.shape
    DA, C, H = D // 2, D // 4, D // 2 // HEAD_DIM
    hn = _rmsnorm_fwd(x, P["gm"])
    qkv = _proj("mix_qkv", hn, P["wqkv"], BF16).reshape(B, S, 3 * DA)
    rest = _proj("mix_rest", hn, P["wrest"], F32).reshape(B, S, 4 * C)
    fl = _proj("mix_f", hn, P["wf"], F32).reshape(B, S, LANES)
    drow = _decay_fwd(fl, P["bias"]).reshape(B, 8, S // tq, tq)
    o, lse = _attn_fwd(qkv, drow, H, tq)
    ypc, pooled = _mix_local_fwd(rest, P["wbd"], P["ps"], P["cw"])
    ycat = jnp.concatenate([o, ypc], axis=-1).reshape(T, D)
    return _proj("mix_out", ycat, P["wmixout"], F32, extra=x), (x, hn, qkv, rest, fl, drow, o, lse, pooled, ycat)


def _mixer_bwd(dres, saved, P, B, S, tq):
    x, hn, qkv, rest, fl, drow, o, lse, pooled, ycat = saved
    T, D = x.shape
    DA, C, H = D // 2, D // 4, D // 2 // HEAD_DIM
    dycat = _proj("mix_out_bwd", dres, P["wmixout"], F32, NT).reshape(B, S, D)
    dw_out = _dw("mix_out_dw", ycat, dres, BF16)
    dq, dk, dv, ddrow, ddcol = _attn_bwd(qkv, drow, o, lse, dycat, H, tq)
    dfl, dbias = _decay_bwd(ddrow.reshape(B, 8, S), ddcol, fl, P["bias"], H)
    drest, dwbd, dps, dcw = _mix_local_bwd(rest, pooled, dycat, P["wbd"], P["ps"], P["cw"])
    dproj = jnp.concatenate([dq, dk, dv, drest, dfl.astype(BF16)], axis=-1).reshape(T, 3 * DA + 4 * C + LANES)
    dwp = _dw("mix_in_dw", hn, dproj, F32)
    dhn = _proj("mix_in_bwd", dproj, P["wp"], F32, NT)
    dx, dg = _rmsnorm_bwd(x, P["gm"], dhn, dres)
    n_q, n_r = 3 * DA, 4 * C
    dw_in = jnp.concatenate([dwp[:, :n_q], dwp[:, n_q + n_r:n_q + n_r + H], dwp[:, n_q:n_q + n_r]], axis=1)
    dw_in = dw_in.reshape(D, N_CHIPS, -1).transpose(1, 0, 2).astype(BF16)
    gw = C // len(POOL_WINDOWS)
    dw_pool = jnp.stack([dwbd[gi * gw:(gi + 1) * gw, gi * gw:(gi + 1) * gw] for gi in range(len(POOL_WINDOWS))])
    small = dict(norm_mix=dg[0], b_forget=dbias[0, :H], w_pool=dw_pool, pool_scale=dps[0], conv_w=dcw[:CONV_WIDTH])
    return dx, small, dw_in, dw_out.reshape(N_CHIPS, -1, D)


def _local_step(x, target, small, gathered):
    B, S, D = x.shape
    L = small["norm_ffn1"].shape[0]
    tq = _tile(S, 256)
    xt = x.reshape(B * S, D)
    saved, params = [], []
    for l in range(L):
        P = _layer_params(l, small, gathered, D)
        xt, s1 = _ffn_fwd(xt, P["g1"], P["w1in"], P["w1out"])
        xt, s2 = _mixer_fwd(xt, P, B, S, tq)
        xt, s3 = _ffn_fwd(xt, P["g2"], P["w2in"], P["w2out"])
        saved.append((s1, s2, s3))
        params.append(P)
    dres, dgf, loss = _final_loss(xt, small["norm_final"][None], target.reshape(B * S, D))
    big = {k: [None] * L for k in BIG}
    sm = {k: [None] * L for k in SMALL if k != "norm_final"}
    for l in reversed(range(L)):
        P, (s1, s2, s3) = params[l], saved[l]
        dres, dg2, big["w_ffn2_in"][l], big["w_ffn2_out"][l] = _ffn_bwd(dres, s3, P["g2"], P["w2in"], P["w2out"])
        dres, smix, big["w_mix_in"][l], big["w_mix_out"][l] = _mixer_bwd(dres, s2, P, B, S, tq)
        dres, dg1, big["w_ffn1_in"][l], big["w_ffn1_out"][l] = _ffn_bwd(dres, s1, P["g1"], P["w1in"], P["w1out"])
        sm["norm_ffn1"][l], sm["norm_ffn2"][l] = dg1[0], dg2[0]
        for k, val in smix.items():
            sm[k][l] = val
    big = {k: jnp.stack(val) for k, val in big.items()}
    sm = {k: jnp.stack(val) for k, val in sm.items()}
    sm["norm_final"] = dgf[0]
    return loss[0, 0], dres.reshape(B, S, D), big, sm


def _pack(parts, extra=()):
    flat = jnp.concatenate([p.reshape(-1) for p in parts] + [jnp.reshape(e, (1,)) for e in extra])
    n = -(-flat.shape[0] // (8 * LANES)) * 8
    return jnp.pad(flat, (0, n * LANES - flat.shape[0])).reshape(n, LANES)


def _unpack(buf, shapes):
    flat, out, at = buf.reshape(-1), [], 0
    for s in shapes:
        n = math.prod(s)
        out.append(flat[at:at + n].reshape(s))
        at += n
    return out, flat[at:]


def kernel(x, norm_ffn1, w_ffn1_in, w_ffn1_out, norm_mix, w_mix_in, b_forget, w_pool, pool_scale, conv_w, w_mix_out, norm_ffn2, w_ffn2_in, w_ffn2_out, norm_final, loss_target, m_norm_ffn1, m_w_ffn1_in, m_w_ffn1_out, m_norm_mix, m_w_mix_in, m_b_forget, m_w_pool, m_pool_scale, m_conv_w, m_w_mix_out, m_norm_ffn2, m_w_ffn2_in, m_w_ffn2_out, m_norm_final, v_norm_ffn1, v_w_ffn1_in, v_w_ffn1_out, v_norm_mix, v_w_mix_in, v_b_forget, v_w_pool, v_pool_scale, v_conv_w, v_w_mix_out, v_norm_ffn2, v_w_ffn2_in, v_w_ffn2_out, v_norm_final):
    w = dict(zip(WEIGHTS, (norm_ffn1, w_ffn1_in, w_ffn1_out, norm_mix, w_mix_in, b_forget, w_pool, pool_scale, conv_w, w_mix_out, norm_ffn2, w_ffn2_in, w_ffn2_out, norm_final)))
    m = dict(zip(WEIGHTS, (m_norm_ffn1, m_w_ffn1_in, m_w_ffn1_out, m_norm_mix, m_w_mix_in, m_b_forget, m_w_pool, m_pool_scale, m_conv_w, m_w_mix_out, m_norm_ffn2, m_w_ffn2_in, m_w_ffn2_out, m_norm_final)))
    v = dict(zip(WEIGHTS, (v_norm_ffn1, v_w_ffn1_in, v_w_ffn1_out, v_norm_mix, v_w_mix_in, v_b_forget, v_w_pool, v_pool_scale, v_conv_w, v_w_mix_out, v_norm_ffn2, v_w_ffn2_in, v_w_ffn2_out, v_norm_final)))
    block = 2 * lax.axis_index("x") + lax.axis_index("y")

    gathered = _all_gather([w[k].astype(BF16) for k in BIG] + [w["conv_w"]])
    gathered = dict(zip(BIG + ("conv_w",), gathered))
    small = {k: w[k] for k in SMALL}
    loss, grad_x, big, sm = _local_step(x, loss_target, small, gathered)

    grads = dict(zip(BIG, _reduce_scatter([big[k] for k in BIG])))
    order = [k for k in SMALL]
    total = _all_reduce_small(_pack([sm[k] for k in order], extra=(loss,)))
    parts, rest = _unpack(total, [sm[k].shape for k in order])
    grads.update(zip(order, parts))
    loss = rest[0]
    cs = conv_w.shape[2]
    grads["conv_w"] = lax.dynamic_slice_in_dim(grads["conv_w"], block * cs, cs, axis=2)

    delta, new_m, new_v = {}, {}, {}
    for k in BIG:
        two_d = lambda a: a.reshape(-1, a.shape[-1])
        d, nm, nv = _adamw(two_d(w[k]), two_d(grads[k]), two_d(m[k]), two_d(v[k]))
        delta[k], new_m[k], new_v[k] = d.reshape(w[k].shape), nm.reshape(w[k].shape), nv.reshape(w[k].shape)
    d, nm, nv = _adamw(*[_pack([t[k] for k in order]) for t in (w, grads, m, v)])
    shapes = [w[k].shape for k in order]
    for res, packed in ((delta, d), (new_m, nm), (new_v, nv)):
        res.update(zip(order, _unpack(packed, shapes)[0]))
    return (loss, grad_x, *[grads[k] for k in WEIGHTS], *[delta[k] for k in WEIGHTS],
            *[new_m[k] for k in WEIGHTS], *[new_v[k] for k in WEIGHTS])
```

```python
import functools
import math

import jax
import jax.numpy as jnp
from jax import lax
from jax.experimental import pallas as pl
from jax.experimental.pallas import tpu as pltpu

F32 = jnp.float32
BF16 = jnp.bfloat16
MESH = pl.DeviceIdType.MESH

HEAD_DIM = 64
POOL_WINDOWS = (2, 4, 8, 16)
CONV_WIDTH = 3
RMS_EPS = 1e-6
ADAM_LR = 0.001
ADAM_B1 = 0.9
ADAM_B2 = 0.999
ADAM_EPS = 1e-08
ADAM_WD = 0.01
ADAM_STEP = 10

LANES = 128
VMEM_LIMIT = 56 * 1024 * 1024
N_CHIPS = 4
N_DEV = 8

NN = (((1,), (0,)), ((), ()))
NT = (((1,), (1,)), ((), ()))
TN = (((0,), (0,)), ((), ()))


def _tile(n, pref):
    t = pref
    while t >= 8:
        if n % t == 0:
            return t
        t //= 2
    return n


def _params(n_grid):
    return pltpu.CompilerParams(dimension_semantics=("arbitrary",) * n_grid, vmem_limit_bytes=VMEM_LIMIT)


def _dot(a, b, dims):
    return lax.dot_general(a, b, dims, preferred_element_type=F32)


def _mm(name, dims, operands, in_specs, out_shape, out_specs, grid, acc_shape, epilogue):
    n_in, n_out, nk = len(operands), len(out_shape), grid[-1]

    def kern(*refs):
        extras, outs = refs[2:n_in], refs[n_in:n_in + n_out]
        part = _dot(refs[0][...].astype(BF16), refs[1][...].astype(BF16), dims)
        if nk == 1:
            epilogue(part, extras, outs)
            return
        acc = refs[n_in + n_out]
        k = pl.program_id(len(grid) - 1)

        @pl.when(k == 0)
        def _():
            acc[...] = part

        @pl.when(k > 0)
        def _():
            acc[...] += part

        @pl.when(k == nk - 1)
        def _():
            epilogue(acc[...], extras, outs)

    return pl.pallas_call(
        kern, name=name, grid=grid, in_specs=in_specs, out_specs=out_specs, out_shape=out_shape,
        scratch_shapes=[pltpu.VMEM(acc_shape, F32)] if nk > 1 else [],
        compiler_params=_params(len(grid)),
    )(*operands)


def _store(scale=None, dtype=None):
    def ep(acc, extras, outs):
        v = acc if scale is None else acc * scale
        outs[0][...] = v.astype(outs[0].dtype)
    return ep


def _residual(scale):
    def ep(acc, extras, outs):
        outs[0][...] = extras[0][...] + scale * acc
    return ep


def _rmsnorm_fwd(x, g):
    T, D = x.shape
    tr = _tile(T, 512)

    def kern(x_ref, g_ref, o_ref):
        xv = x_ref[...]
        r = lax.rsqrt(jnp.mean(xv * xv, axis=-1, keepdims=True) + RMS_EPS)
        o_ref[...] = (xv * r * g_ref[...]).astype(BF16)

    return pl.pallas_call(
        kern, name="rmsnorm_fwd", grid=(T // tr,),
        in_specs=[pl.BlockSpec((tr, D), lambda i: (i, 0)), pl.BlockSpec((1, D), lambda i: (0, 0))],
        out_specs=pl.BlockSpec((tr, D), lambda i: (i, 0)),
        out_shape=jax.ShapeDtypeStruct((T, D), BF16), compiler_params=_params(1),
    )(x, g)


def _rmsnorm_bwd(x, g, dh, dres):
    T, D = x.shape
    tr = _tile(T, 256)

    def kern(x_ref, g_ref, dh_ref, dres_ref, dx_ref, dg_ref):
        xv, dhv = x_ref[...], dh_ref[...]
        r = lax.rsqrt(jnp.mean(xv * xv, axis=-1, keepdims=True) + RMS_EPS)
        y = xv * r
        dy = dhv * g_ref[...]
        dx_ref[...] = dres_ref[...] + r * (dy - y * jnp.mean(dy * y, axis=-1, keepdims=True))
        part = jnp.sum(dhv * y, axis=0, keepdims=True)

        @pl.when(pl.program_id(0) == 0)
        def _():
            dg_ref[...] = part

        @pl.when(pl.program_id(0) > 0)
        def _():
            dg_ref[...] += part

    row = pl.BlockSpec((tr, D), lambda i: (i, 0))
    vec = pl.BlockSpec((1, D), lambda i: (0, 0))
    return pl.pallas_call(
        kern, name="rmsnorm_bwd", grid=(T // tr,), in_specs=[row, vec, row, row], out_specs=[row, vec],
        out_shape=[jax.ShapeDtypeStruct((T, D), F32), jax.ShapeDtypeStruct((1, D), F32)],
        compiler_params=_params(1),
    )(x, g, dh, dres)


def _final_loss(x, g, target):
    T, D = x.shape
    tr = _tile(T, 256)

    def kern(x_ref, g_ref, t_ref, dx_ref, dg_ref, loss_ref):
        xv = x_ref[...]
        r = lax.rsqrt(jnp.mean(xv * xv, axis=-1, keepdims=True) + RMS_EPS)
        y = xv * r
        err = y * g_ref[...] - t_ref[...]
        lpart = 0.5 * jnp.sum(jnp.mean(err * err, axis=-1, keepdims=True), axis=0, keepdims=True)
        dh = err * (1.0 / D)
        dy = dh * g_ref[...]
        dx_ref[...] = r * (dy - y * jnp.mean(dy * y, axis=-1, keepdims=True))
        part = jnp.sum(dh * y, axis=0, keepdims=True)
        lrow = jnp.broadcast_to(lpart, (1, LANES))

        @pl.when(pl.program_id(0) == 0)
        def _():
            dg_ref[...] = part
            loss_ref[...] = lrow

        @pl.when(pl.program_id(0) > 0)
        def _():
            dg_ref[...] += part
            loss_ref[...] += lrow

    row = pl.BlockSpec((tr, D), lambda i: (i, 0))
    vec = pl.BlockSpec((1, D), lambda i: (0, 0))
    return pl.pallas_call(
        kern, name="final_loss", grid=(T // tr,), in_specs=[row, vec, row],
        out_specs=[row, vec, pl.BlockSpec((1, LANES), lambda i: (0, 0))],
        out_shape=[jax.ShapeDtypeStruct((T, D), F32), jax.ShapeDtypeStruct((1, D), F32),
                   jax.ShapeDtypeStruct((1, LANES), F32)],
        compiler_params=_params(1),
    )(x, g, target)


def _ffn_in(h, w4):
    T, D = h.shape
    Fh = w4.shape[2]
    F = 2 * Fh
    tm = _tile(T, 512)

    def kern(h_ref, wg_ref, wu_ref, gu_ref, act_ref):
        hv = h_ref[...]
        gate = _dot(hv, wg_ref[...], NN)
        up = _dot(hv, wu_ref[...], NN)
        gu_ref[0] = gate.astype(BF16)
        gu_ref[1] = up.astype(BF16)
        act_ref[...] = (gate * jax.nn.sigmoid(gate) * up).astype(BF16)

    return pl.pallas_call(
        kern, name="ffn_in", grid=(2, T // tm),
        in_specs=[pl.BlockSpec((tm, D), lambda j, i: (i, 0)),
                  pl.BlockSpec((None, D, Fh), lambda j, i: (j, 0, 0)),
                  pl.BlockSpec((None, D, Fh), lambda j, i: (2 + j, 0, 0))],
        out_specs=[pl.BlockSpec((2, tm, Fh), lambda j, i: (0, i, j)),
                   pl.BlockSpec((tm, Fh), lambda j, i: (i, j))],
        out_shape=[jax.ShapeDtypeStruct((2, T, F), BF16), jax.ShapeDtypeStruct((T, F), BF16)],
        compiler_params=_params(2),
    )(h, w4, w4)


def _ffn_out(act, w_out, x):
    T, F = act.shape
    D = w_out.shape[1]
    tm = _tile(T, 512)
    return _mm("ffn_out", NN, [act, w_out, x],
               [pl.BlockSpec((tm, F), lambda i, k: (i, 0)), pl.BlockSpec((F, D), lambda i, k: (0, 0)),
                pl.BlockSpec((tm, D), lambda i, k: (i, 0))],
               [jax.ShapeDtypeStruct((T, D), F32)], [pl.BlockSpec((tm, D), lambda i, k: (i, 0))],
               (T // tm, 1), None, _residual(0.5))


def _ffn_bwd_act(dres, w_out, gu):
    T, D = dres.shape
    F = w_out.shape[0]
    Fh = F // 2
    tm = _tile(T, 512)

    def kern(d_ref, w_ref, gu_ref, o_ref):
        dact = 0.5 * _dot(d_ref[...].astype(BF16), w_ref[...], NT)
        gate = gu_ref[0].astype(F32)
        up = gu_ref[1].astype(F32)
        sg = jax.nn.sigmoid(gate)
        o_ref[0] = (dact * up * (sg * (1.0 + gate * (1.0 - sg)))).astype(BF16)
        o_ref[1] = (dact * gate * sg).astype(BF16)

    return pl.pallas_call(
        kern, name="ffn_bwd_act", grid=(2, T // tm),
        in_specs=[pl.BlockSpec((tm, D), lambda j, i: (i, 0)), pl.BlockSpec((Fh, D), lambda j, i: (j, 0)),
                  pl.BlockSpec((2, tm, Fh), lambda j, i: (0, i, j))],
        out_specs=pl.BlockSpec((2, tm, Fh), lambda j, i: (0, i, j)),
        out_shape=jax.ShapeDtypeStruct((2, T, F), BF16), compiler_params=_params(2),
    )(dres, w_out, gu)


def _ffn_dw_out(act, dres):
    T, F = act.shape
    D = dres.shape[1]
    tm, tk = F // 2, _tile(T, 512)
    return _mm("ffn_dw_out", TN, [act, dres],
               [pl.BlockSpec((tk, tm), lambda i, k: (k, i)), pl.BlockSpec((tk, D), lambda i, k: (k, 0))],
               [jax.ShapeDtypeStruct((F, D), BF16)], [pl.BlockSpec((tm, D), lambda i, k: (i, 0))],
               (2, T // tk), (tm, D), _store(0.5))


def _ffn_dw_in(h, dgu):
    T, D = h.shape
    Fh = dgu.shape[2] // 2
    tk = _tile(T, 512)
    return _mm("ffn_dw_in", TN, [h, dgu],
               [pl.BlockSpec((tk, D), lambda j, k: (k, 0)),
                pl.BlockSpec((None, tk, Fh), lambda j, k: (j // 2, k, j % 2))],
               [jax.ShapeDtypeStruct((4, D, Fh), BF16)], [pl.BlockSpec((None, D, Fh), lambda j, k: (j, 0, 0))],
               (4, T // tk), (D, Fh), _store())


def _ffn_dh(dgu, w4):
    T = dgu.shape[1]
    D, Fh = w4.shape[1], w4.shape[2]
    tm = _tile(T, 512)
    return _mm("ffn_dh", NT, [dgu, w4],
               [pl.BlockSpec((None, tm, Fh), lambda i, k: (k // 2, i, k % 2)),
                pl.BlockSpec((None, D, Fh), lambda i, k: (k, 0, 0))],
               [jax.ShapeDtypeStruct((T, D), F32)], [pl.BlockSpec((tm, D), lambda i, k: (i, 0))],
               (T // tm, 4), (tm, D), _store())


def _proj(name, a, w, out_dtype, dims=NN, extra=None, scale=None):
    T, K = a.shape
    N = w.shape[1] if dims == NN else w.shape[0]
    tm = _tile(T, 512)
    ops = [a, w] + ([extra] if extra is not None else [])
    specs = [pl.BlockSpec((tm, K), lambda i, k: (i, 0)), pl.BlockSpec(w.shape, lambda i, k: (0, 0))]
    if extra is not None:
        specs.append(pl.BlockSpec((tm, N), lambda i, k: (i, 0)))
    ep = _residual(1.0) if extra is not None else _store(scale)
    return _mm(name, dims, ops, specs, [jax.ShapeDtypeStruct((T, N), out_dtype)],
               [pl.BlockSpec((tm, N), lambda i, k: (i, 0))], (T // tm, 1), None, ep)[0]


def _dw(name, a, d, out_dtype):
    T, M = a.shape
    N = d.shape[1]
    tk = _tile(T, 512)
    return _mm(name, TN, [a, d],
               [pl.BlockSpec((tk, M), lambda i, k: (k, 0)), pl.BlockSpec((tk, N), lambda i, k: (k, 0))],
               [jax.ShapeDtypeStruct((M, N), out_dtype)], [pl.BlockSpec((M, N), lambda i, k: (0, 0))],
               (1, T // tk), (M, N), _store())[0]


def _log_sigmoid(z):
    return jnp.minimum(z, 0.0) - jnp.log(1.0 + jnp.exp(-jnp.abs(z)))


def _decay_fwd(fl, bias):
    B, S, _ = fl.shape

    def kern(fl_ref, b_ref, o_ref):
        d = _log_sigmoid(fl_ref[...] + b_ref[...])
        row = lax.broadcasted_iota(jnp.int32, (S, LANES), 0)
        sh = 1
        while sh < S:
            d = d + jnp.where(row >= sh, pltpu.roll(d, sh, 0), 0.0)
            sh *= 2
        o_ref[...] = d.T[0:8, :]

    return pl.pallas_call(
        kern, name="decay_fwd", grid=(B,),
        in_specs=[pl.BlockSpec((None, S, LANES), lambda b: (b, 0, 0)), pl.BlockSpec((1, LANES), lambda b: (0, 0))],
        out_specs=pl.BlockSpec((None, 8, S), lambda b: (b, 0, 0)),
        out_shape=jax.ShapeDtypeStruct((B, 8, S), F32), compiler_params=_params(1),
    )(fl, bias)


def _decay_bwd(ddrow, ddcol, fl, bias, n_heads):
    B, S, _ = fl.shape

    def kern(dd_ref, ddc_ref, fl_ref, b_ref, dfl_ref, db_ref):
        dd = jnp.concatenate([dd_ref[...], jnp.zeros((LANES - 8, S), F32)], axis=0).T + ddc_ref[...]
        row = lax.broadcasted_iota(jnp.int32, (S, LANES), 0)
        lane = lax.broadcasted_iota(jnp.int32, (S, LANES), 1)
        sh = 1
        while sh < S:
            dd = dd + jnp.where(row < S - sh, pltpu.roll(dd, S - sh, 0), 0.0)
            sh *= 2
        z = fl_ref[...] + b_ref[...]
        dfl = jnp.where(lane < n_heads, dd / (1.0 + jnp.exp(z)), 0.0)
        dfl_ref[...] = dfl
        part = jnp.sum(dfl, axis=0, keepdims=True)

        @pl.when(pl.program_id(0) == 0)
        def _():
            db_ref[...] = part

        @pl.when(pl.program_id(0) > 0)
        def _():
            db_ref[...] += part

    return pl.pallas_call(
        kern, name="decay_bwd", grid=(B,),
        in_specs=[pl.BlockSpec((None, 8, S), lambda b: (b, 0, 0)), pl.BlockSpec((None, S, LANES), lambda b: (b, 0, 0)),
                  pl.BlockSpec((None, S, LANES), lambda b: (b, 0, 0)), pl.BlockSpec((1, LANES), lambda b: (0, 0))],
        out_specs=[pl.BlockSpec((None, S, LANES), lambda b: (b, 0, 0)), pl.BlockSpec((1, LANES), lambda b: (0, 0))],
        out_shape=[jax.ShapeDtypeStruct((B, S, LANES), F32), jax.ShapeDtypeStruct((1, LANES), F32)],
        compiler_params=_params(1),
    )(ddrow, ddcol, fl, bias)


def _attn_fwd(qkv, drow, n_heads, tq):
    B, S, _ = qkv.shape
    DA = n_heads * HEAD_DIM
    scale = HEAD_DIM ** -0.5

    def kern(q_ref, k_ref, v_ref, dr_ref, o_ref, lse_ref):
        i = pl.program_id(1)
        lane = lax.broadcasted_iota(jnp.int32, (tq, LANES), 1)
        low = lane < HEAD_DIM
        causal = lax.broadcasted_iota(jnp.int32, (tq, tq), 1) <= lax.broadcasted_iota(jnp.int32, (tq, tq), 0)
        lse_mat = jnp.zeros((tq, LANES), F32)
        for p in range(n_heads // 2):
            cols = slice(LANES * p, LANES * (p + 1))
            q2 = q_ref[:, cols] * scale
            outs = []
            for hh in range(2):
                h = 2 * p + hh
                qm = jnp.where(low if hh == 0 else ~low, q2, jnp.zeros_like(q2))

                def step(j, carry, masked, h=h, qm=qm, cols=cols):
                    m, l, acc = carry
                    ks = pl.multiple_of(j * tq, tq)
                    s = _dot(qm, k_ref[pl.ds(ks, tq), cols], NT) - dr_ref[h, pl.ds(j, 1), :]
                    if masked:
                        s = jnp.where(causal, s, -jnp.inf)
                    m_new = jnp.maximum(m, jnp.max(s, axis=1, keepdims=True))
                    alpha = jnp.exp(m - m_new)
                    pm = jnp.exp(s - m_new)
                    l = alpha * l + jnp.sum(pm, axis=1, keepdims=True)
                    acc = alpha * acc + _dot(pm.astype(BF16), v_ref[pl.ds(ks, tq), cols], NN)
                    return m_new, l, acc

                init = (jnp.full((tq, 1), -jnp.inf, F32), jnp.zeros((tq, 1), F32), jnp.zeros((tq, LANES), F32))
                carry = lax.fori_loop(0, i, functools.partial(step, masked=False), init)
                m, l, acc = step(i, carry, True)
                outs.append(acc / l)
                lse_mat = jnp.where(lane == h, m + jnp.log(l), lse_mat)
            o_ref[:, cols] = jnp.where(low, outs[0], outs[1]).astype(BF16)
        lse_ref[...] = lse_mat

    nq = S // tq
    return pl.pallas_call(
        kern, name="attn_fwd", grid=(B, nq),
        in_specs=[pl.BlockSpec((None, tq, DA), lambda b, i: (b, i, 0)),
                  pl.BlockSpec((None, S, DA), lambda b, i: (b, 0, 1)),
                  pl.BlockSpec((None, S, DA), lambda b, i: (b, 0, 2)),
                  pl.BlockSpec((None, 8, nq, tq), lambda b, i: (b, 0, 0, 0))],
        out_specs=[pl.BlockSpec((None, tq, DA), lambda b, i: (b, i, 0)),
                   pl.BlockSpec((None, tq, LANES), lambda b, i: (b, i, 0))],
        out_shape=[jax.ShapeDtypeStruct((B, S, DA), BF16), jax.ShapeDtypeStruct((B, S, LANES), F32)],
        compiler_params=_params(2),
    )(qkv, qkv, qkv, drow)


def _attn_bwd(qkv, drow, o, lse, dycat, n_heads, tq):
    B, S, _ = qkv.shape
    DA = n_heads * HEAD_DIM
    scale = HEAD_DIM ** -0.5
    nq = S // tq

    def kern(q_ref, k_ref, v_ref, dr_ref, o_ref, lse_ref, do_ref, dq_ref, dk_ref, dv_ref, ddr_ref, ddc_ref, dk_acc, dv_acc):
        i = pl.program_id(1)

        @pl.when(i == 0)
        def _():
            dk_acc[...] = jnp.zeros_like(dk_acc)
            dv_acc[...] = jnp.zeros_like(dv_acc)
            ddr_ref[...] = jnp.zeros_like(ddr_ref)

        lane = lax.broadcasted_iota(jnp.int32, (tq, LANES), 1)
        low = lane < HEAD_DIM
        causal = lax.broadcasted_iota(jnp.int32, (tq, tq), 1) <= lax.broadcasted_iota(jnp.int32, (tq, tq), 0)
        ddc = jnp.zeros((tq, LANES), F32)
        for p in range(n_heads // 2):
            cols = slice(LANES * p, LANES * (p + 1))
            q2 = q_ref[:, cols] * scale
            do_f = do_ref[:, cols]
            do2 = do_f.astype(BF16)
            prod = do_f * o_ref[:, cols].astype(F32)
            dqs = []
            for hh in range(2):
                h = 2 * p + hh
                msk = low if hh == 0 else ~low
                qm = jnp.where(msk, q2, jnp.zeros_like(q2))
                dom = jnp.where(msk, do2, jnp.zeros_like(do2))
                delta = jnp.sum(jnp.where(msk, prod, 0.0), axis=1, keepdims=True)
                lse_h = lse_ref[:, h:h + 1]

                def step(j, carry, masked, h=h, qm=qm, dom=dom, delta=delta, lse_h=lse_h, cols=cols):
                    dq, rs = carry
                    ks = pl.multiple_of(j * tq, tq)
                    k2 = k_ref[pl.ds(ks, tq), cols]
                    s = _dot(qm, k2, NT) - dr_ref[h, pl.ds(j, 1), :]
                    if masked:
                        s = jnp.where(causal, s, -jnp.inf)
                    pm = jnp.exp(s - lse_h)
                    ds = pm * (_dot(dom, v_ref[pl.ds(ks, tq), cols], NT) - delta)
                    ddr_ref[h, pl.ds(j, 1), :] -= jnp.sum(ds, axis=0, keepdims=True)
                    dsb = ds.astype(BF16)
                    dv_acc[pl.ds(ks, tq), cols] += _dot(pm.astype(BF16), dom, TN)
                    dk_acc[pl.ds(ks, tq), cols] += _dot(dsb, qm, TN)
                    return dq + _dot(dsb, k2, NN), rs + jnp.sum(ds, axis=1, keepdims=True)

                init = (jnp.zeros((tq, LANES), F32), jnp.zeros((tq, 1), F32))
                dq, rs = step(i, lax.fori_loop(0, i, functools.partial(step, masked=False), init), True)
                dqs.append(dq)
                ddc = jnp.where(lane == h, rs, ddc)
            dq_ref[:, cols] = (jnp.where(low, dqs[0], dqs[1]) * scale).astype(BF16)
        ddc_ref[...] = ddc

        @pl.when(i == nq - 1)
        def _():
            dk_ref[...] = dk_acc[...].astype(BF16)
            dv_ref[...] = dv_acc[...].astype(BF16)

    tile = pl.BlockSpec((None, tq, DA), lambda b, i: (b, i, 0))
    seq = pl.BlockSpec((None, S, DA), lambda b, i: (b, 0, 0))
    dec = pl.BlockSpec((None, 8, nq, tq), lambda b, i: (b, 0, 0, 0))
    return pl.pallas_call(
        kern, name="attn_bwd", grid=(B, nq),
        in_specs=[tile, pl.BlockSpec((None, S, DA), lambda b, i: (b, 0, 1)),
                  pl.BlockSpec((None, S, DA), lambda b, i: (b, 0, 2)), dec, tile,
                  pl.BlockSpec((None, tq, LANES), lambda b, i: (b, i, 0)), tile],
        out_specs=[tile, seq, seq, dec, pl.BlockSpec((None, tq, LANES), lambda b, i: (b, i, 0))],
        out_shape=[jax.ShapeDtypeStruct((B, S, DA), BF16)] * 3 + [jax.ShapeDtypeStruct((B, 8, nq, tq), F32),
                                                                  jax.ShapeDtypeStruct((B, S, LANES), F32)],
        scratch_shapes=[pltpu.VMEM((S, DA), F32), pltpu.VMEM((S, DA), F32)],
        compiler_params=_params(2),
    )(qkv, qkv, qkv, drow, o, lse, dycat)


def _down(v, d, row):
    return jnp.where(row >= d, pltpu.roll(v, d, 0), 0.0)


def _up(v, d, row, S):
    return jnp.where(row < S - d, pltpu.roll(v, S - d, 0), 0.0)


def _window(v, shift, group):
    sums, acc, d = [], v, 1
    for _ in POOL_WINDOWS:
        acc = acc + shift(acc, d)
        sums.append(acc)
        d *= 2
    out = sums[-1]
    for gi in range(len(POOL_WINDOWS) - 2, -1, -1):
        out = jnp.where(group == gi, sums[gi], out)
    return out


def _pool_count(row, group):
    w = jnp.full(row.shape, POOL_WINDOWS[-1], jnp.int32)
    for gi in range(len(POOL_WINDOWS) - 2, -1, -1):
        w = jnp.where(group == gi, POOL_WINDOWS[gi], w)
    return jnp.minimum(row + 1, w).astype(F32)


def _mix_local_fwd(rest, wbd, ps, cw):
    B, S, C4 = rest.shape
    C = C4 // 4
    gw = C // len(POOL_WINDOWS)

    def kern(r_ref, w_ref, ps_ref, cw_ref, y_ref, pooled_ref):
        row = lax.broadcasted_iota(jnp.int32, (S, C), 0)
        group = lax.broadcasted_iota(jnp.int32, (S, C), 1) // gw
        u = r_ref[:, 0:C]
        pooled = _window(u, lambda v, d: _down(v, d, row), group) / _pool_count(row, group) - u
        pb = pooled.astype(BF16)
        pooled_ref[...] = pb
        y_ref[:, 0:C] = (_dot(pb, w_ref[...], NN) * ps_ref[...]).astype(BF16)
        uc = r_ref[:, 2 * C:3 * C] * r_ref[:, 3 * C:4 * C]
        y = cw_ref[0:1, :] * _down(uc, 2, row) + cw_ref[1:2, :] * _down(uc, 1, row) + cw_ref[2:3, :] * uc
        y_ref[:, C:2 * C] = (r_ref[:, C:2 * C] * y).astype(BF16)

    return pl.pallas_call(
        kern, name="mix_local_fwd", grid=(B,),
        in_specs=[pl.BlockSpec((None, S, C4), lambda b: (b, 0, 0)), pl.BlockSpec((C, C), lambda b: (0, 0)),
                  pl.BlockSpec((1, C), lambda b: (0, 0)), pl.BlockSpec((8, C), lambda b: (0, 0))],
        out_specs=[pl.BlockSpec((None, S, 2 * C), lambda b: (b, 0, 0)), pl.BlockSpec((None, S, C), lambda b: (b, 0, 0))],
        out_shape=[jax.ShapeDtypeStruct((B, S, 2 * C), BF16), jax.ShapeDtypeStruct((B, S, C), BF16)],
        compiler_params=_params(1),
    )(rest, wbd, ps, cw)


def _mix_local_bwd(rest, pooled, dycat, wbd, ps, cw):
    B, S, C4 = rest.shape
    C = C4 // 4
    gw = C // len(POOL_WINDOWS)

    def kern(r_ref, pooled_ref, d_ref, w_ref, ps_ref, cw_ref, dr_ref, dw_ref, dps_ref, dcw_ref):
        row = lax.broadcasted_iota(jnp.int32, (S, C), 0)
        group = lax.broadcasted_iota(jnp.int32, (S, C), 1) // gw
        dyp = d_ref[:, 0:C]
        dyc = d_ref[:, C:2 * C]
        pb = pooled_ref[...]
        dps = jnp.sum(dyp * _dot(pb, w_ref[...], NN), axis=0, keepdims=True)
        dzb = (dyp * ps_ref[...]).astype(BF16)
        dw = _dot(pb, dzb, TN)
        dpooled = _dot(dzb, w_ref[...], NT)
        g = dpooled / _pool_count(row, group)
        dr_ref[:, 0:C] = (_window(g, lambda v, d: _up(v, d, row, S), group) - dpooled).astype(BF16)
        cc, ch = r_ref[:, 2 * C:3 * C], r_ref[:, 3 * C:4 * C]
        uc = cc * ch
        u1, u2 = _down(uc, 1, row), _down(uc, 2, row)
        y = cw_ref[0:1, :] * u2 + cw_ref[1:2, :] * u1 + cw_ref[2:3, :] * uc
        dr_ref[:, C:2 * C] = (dyc * y).astype(BF16)
        dy = dyc * r_ref[:, C:2 * C]
        duc = cw_ref[0:1, :] * _up(dy, 2, row, S) + cw_ref[1:2, :] * _up(dy, 1, row, S) + cw_ref[2:3, :] * dy
        dr_ref[:, 2 * C:3 * C] = (duc * ch).astype(BF16)
        dr_ref[:, 3 * C:4 * C] = (duc * cc).astype(BF16)
        dcw = jnp.concatenate([jnp.sum(dy * u2, axis=0, keepdims=True), jnp.sum(dy * u1, axis=0, keepdims=True),
                               jnp.sum(dy * uc, axis=0, keepdims=True), jnp.zeros((5, C), F32)], axis=0)

        @pl.when(pl.program_id(0) == 0)
        def _():
            dw_ref[...] = dw
            dps_ref[...] = dps
            dcw_ref[...] = dcw

        @pl.when(pl.program_id(0) > 0)
        def _():
            dw_ref[...] += dw
            dps_ref[...] += dps
            dcw_ref[...] += dcw

    full = lambda shape: pl.BlockSpec(shape, lambda b: (0, 0))
    return pl.pallas_call(
        kern, name="mix_local_bwd", grid=(B,),
        in_specs=[pl.BlockSpec((None, S, C4), lambda b: (b, 0, 0)), pl.BlockSpec((None, S, C), lambda b: (b, 0, 0)),
                  pl.BlockSpec((None, S, 2 * C), lambda b: (b, 0, 1)), full((C, C)), full((1, C)), full((8, C))],
        out_specs=[pl.BlockSpec((None, S, C4), lambda b: (b, 0, 0)), full((C, C)), full((1, C)), full((8, C))],
        out_shape=[jax.ShapeDtypeStruct((B, S, C4), BF16), jax.ShapeDtypeStruct((C, C), F32),
                   jax.ShapeDtypeStruct((1, C), F32), jax.ShapeDtypeStruct((8, C), F32)],
        compiler_params=_params(1),
    )(rest, pooled, dycat, wbd, ps, cw)


def _adamw(w, g, m, v):
    R, C = w.shape
    tr = _tile(R, 512)

    def kern(w_ref, g_ref, m_ref, v_ref, d_ref, nm_ref, nv_ref):
        gv = g_ref[...]
        nm = ADAM_B1 * m_ref[...] + (1.0 - ADAM_B1) * gv
        nv = ADAM_B2 * v_ref[...] + (1.0 - ADAM_B2) * (gv * gv)
        m_hat = nm / (1.0 - ADAM_B1 ** ADAM_STEP)
        v_hat = nv / (1.0 - ADAM_B2 ** ADAM_STEP)
        d_ref[...] = -ADAM_LR * (m_hat / (jnp.sqrt(v_hat) + ADAM_EPS) + ADAM_WD * w_ref[...])
        nm_ref[...] = nm
        nv_ref[...] = nv

    blk = pl.BlockSpec((tr, C), lambda i: (i, 0))
    return pl.pallas_call(
        kern, name="adamw", grid=(R // tr,), in_specs=[blk] * 4, out_specs=[blk] * 3,
        out_shape=[jax.ShapeDtypeStruct((R, C), F32)] * 3, compiler_params=_params(1),
    )(w, g, m, v)


def _place():
    x, y, c = lax.axis_index("x"), lax.axis_index("y"), lax.axis_index("c")
    return x, y, c, [(1 - x, y), (x, 1 - y), (1 - x, 1 - y)]


def _comm_call(name, body, operands, out_shape, n_sems, aliases=None):
    any_spec = pl.BlockSpec(memory_space=pl.ANY)
    return pl.pallas_call(
        body, name=name, in_specs=[any_spec] * len(operands), out_specs=[any_spec] * len(out_shape),
        out_shape=out_shape, input_output_aliases=aliases or {},
        scratch_shapes=[pltpu.SemaphoreType.DMA((n,)) for n in n_sems],
    )(*operands)


def _place_shard(w, b, dtype):
    L, R, C = w.shape
    tr = _tile(R, 512)

    def kern(b_ref, w_ref, o_ref):
        o_ref[...] = w_ref[...].astype(dtype)

    return pl.pallas_call(
        kern, name="place_shard",
        grid_spec=pltpu.PrefetchScalarGridSpec(
            num_scalar_prefetch=1, grid=(L, R // tr),
            in_specs=[pl.BlockSpec((None, tr, C), lambda l, i, b_ref: (l, i, 0))],
            out_specs=pl.BlockSpec((None, None, tr, C), lambda l, i, b_ref: (l, b_ref[0], i, 0))),
        out_shape=jax.ShapeDtypeStruct((L, N_CHIPS, R, C), dtype), compiler_params=_params(2),
    )(b, w)


def _all_gather(bufs):
    n, hl = len(bufs), bufs[0].shape[0] // 2

    def body(*refs):
        outs = refs[n:2 * n]
        send_sems, recv_sems = refs[2 * n:]
        x, y, c, chips = _place()
        sibling = (x, y, 1 - c)

        def remote(k, j, chip, half, to):
            region = outs[k].at[pl.ds(half * hl, hl), 2 * chip[0] + chip[1]]
            return pltpu.make_async_remote_copy(
                src_ref=region, dst_ref=region, send_sem=send_sems.at[6 * k + j],
                recv_sem=recv_sems.at[6 * k + j], device_id=to, device_id_type=MESH)

        first = [remote(k, j, (x, y), c, (*chip, c)) for k in range(n) for j, chip in enumerate(chips)]
        for cp in first:
            cp.start()
        passed = []
        for k in range(n):
            for j, chip in enumerate(chips):
                remote(k, j, chip, c, (x, y, c)).wait_recv()
                passed.append(remote(k, 3 + j, chip, c, sibling))
                passed[-1].start()
        for k in range(n):
            for j, chip in enumerate(chips):
                remote(k, 3 + j, chip, 1 - c, (x, y, c)).wait_recv()
        for cp in first + passed:
            cp.wait_send()

    out_shape = [jax.ShapeDtypeStruct(s.shape, s.dtype) for s in bufs]
    return _comm_call("all_gather_weights", body, bufs, out_shape, (6 * n, 6 * n), aliases={k: k for k in range(n)})


def _rs_swap_halves(grads):
    n, hl = len(grads), grads[0].shape[0] // 2

    def body(*refs):
        ins, outs = refs[:n], refs[n:2 * n]
        send_sems, recv_sems = refs[2 * n:]
        x, y, c, _ = _place()
        copies = [pltpu.make_async_remote_copy(
            src_ref=ins[k].at[pl.ds((1 - c) * hl, hl)], dst_ref=outs[k], send_sem=send_sems.at[k],
            recv_sem=recv_sems.at[k], device_id=(x, y, 1 - c), device_id_type=MESH) for k in range(n)]
        for cp in copies:
            cp.start()
        for cp in copies:
            cp.wait()

    out_shape = [jax.ShapeDtypeStruct((hl,) + g.shape[1:], g.dtype) for g in grads]
    return _comm_call("rs_swap_halves", body, grads, out_shape, (n, n))


def _rs_exchange(parts):
    n = len(parts)

    def body(*refs):
        ins, outs = refs[:n], refs[n:2 * n]
        send_sems, recv_sems = refs[2 * n:]
        x, y, c, chips = _place()
        copies = [pltpu.make_async_remote_copy(
            src_ref=ins[k].at[:, 2 * chip[0] + chip[1]], dst_ref=outs[k].at[j], send_sem=send_sems.at[3 * k + j],
            recv_sem=recv_sems.at[3 * k + j], device_id=(*chip, c), device_id_type=MESH)
            for k in range(n) for j, chip in enumerate(chips)]
        for cp in copies:
            cp.start()
        for cp in copies:
            cp.wait()

    out_shape = [jax.ShapeDtypeStruct((3, p.shape[0]) + p.shape[2:], p.dtype) for p in parts]
    return _comm_call("rs_exchange", body, parts, out_shape, (3 * n, 3 * n))


def _rs_share(bufs):
    n, hl = len(bufs), bufs[0].shape[0] // 2

    def body(*refs):
        outs = refs[n:2 * n]
        send_sems, recv_sems = refs[2 * n:]
        x, y, c, _ = _place()

        def half(k, which):
            region = outs[k].at[pl.ds(which * hl, hl)]
            return pltpu.make_async_remote_copy(
                src_ref=region, dst_ref=region, send_sem=send_sems.at[k], recv_sem=recv_sems.at[k],
                device_id=(x, y, 1 - c), device_id_type=MESH)

        sends = [half(k, c) for k in range(n)]
        for cp in sends:
            cp.start()
        for k in range(n):
            half(k, 1 - c).wait_recv()
        for cp in sends:
            cp.wait_send()

    out_shape = [jax.ShapeDtypeStruct(h.shape, h.dtype) for h in bufs]
    return _comm_call("rs_share", body, bufs, out_shape, (n, n), aliases={k: k for k in range(n)})


def _all_reduce_small(v):
    n = v.shape[0]

    def body(v_ref, o_ref, gbuf, send_sems, recv_sems):
        x, y, c, _ = _place()
        me = 4 * x + 2 * y + c
        gbuf[me] = v_ref[...]
        copies, waits = [], []
        for r in range(1, N_DEV):
            px = 1 - x if r & 4 else x
            py = 1 - y if r & 2 else y
            pc = 1 - c if r & 1 else c
            mk = functools.partial(pltpu.make_async_remote_copy, src_ref=v_ref, send_sem=send_sems.at[r - 1],
                                   recv_sem=recv_sems.at[r - 1], device_id=(px, py, pc), device_id_type=MESH)
            copies.append(mk(dst_ref=gbuf.at[me]))
            waits.append(mk(dst_ref=gbuf.at[4 * px + 2 * py + pc]))
        for cp in copies:
            cp.start()
        for cp in waits:
            cp.wait_recv()
        for cp in copies:
            cp.wait_send()
        acc = gbuf[0]
        for d in range(1, N_DEV):
            acc = acc + gbuf[d]
        o_ref[...] = acc

    vm = pl.BlockSpec(memory_space=pltpu.VMEM)
    return pl.pallas_call(
        body, name="all_reduce_small", in_specs=[vm], out_specs=vm, out_shape=jax.ShapeDtypeStruct(v.shape, F32),
        scratch_shapes=[pltpu.VMEM((N_DEV, n, LANES), F32), pltpu.SemaphoreType.DMA((N_DEV - 1,)),
                        pltpu.SemaphoreType.DMA((N_DEV - 1,))],
        compiler_params=pltpu.CompilerParams(vmem_limit_bytes=VMEM_LIMIT),
    )(v)


def _add_half(g, h1, c):
    hl, nb, R, C = h1.shape
    g3, h3 = g.reshape(2 * hl, nb * R, C), h1.reshape(hl, nb * R, C)
    tr = _tile(nb * R, 512)

    def kern(c_ref, g_ref, h_ref, o_ref):
        o_ref[...] = (g_ref[...].astype(F32) + h_ref[...].astype(F32)).astype(BF16)

    blk = pl.BlockSpec((None, tr, C), lambda l, i, c_ref: (l, i, 0))
    out = pl.pallas_call(
        kern, name="rs_add_half",
        grid_spec=pltpu.PrefetchScalarGridSpec(
            num_scalar_prefetch=1, grid=(hl, nb * R // tr),
            in_specs=[pl.BlockSpec((None, tr, C), lambda l, i, c_ref: (c_ref[0] * hl + l, i, 0)), blk],
            out_specs=blk),
        out_shape=jax.ShapeDtypeStruct(h3.shape, BF16), compiler_params=_params(2),
    )(c, g3, h3)
    return out.reshape(h1.shape)


def _add_blocks(p, h2, bc):
    hl, nb, R, C = p.shape
    tr = _tile(R, 512)

    def kern(bc_ref, p_ref, h0_ref, h1_ref, h2_ref, o_ref):
        o_ref[...] = ((p_ref[...].astype(F32) + h0_ref[...].astype(F32)) + h1_ref[...].astype(F32)) + h2_ref[...].astype(F32)

    def other(j):
        return pl.BlockSpec((None, None, tr, C), lambda l, i, bc_ref: (j, l, i, 0))

    return pl.pallas_call(
        kern, name="rs_add_blocks",
        grid_spec=pltpu.PrefetchScalarGridSpec(
            num_scalar_prefetch=1, grid=(hl, R // tr),
            in_specs=[pl.BlockSpec((None, None, tr, C), lambda l, i, bc_ref: (l, bc_ref[0], i, 0)),
                      other(0), other(1), other(2)],
            out_specs=pl.BlockSpec((None, tr, C), lambda l, i, bc_ref: (bc_ref[1] * hl + l, i, 0))),
        out_shape=jax.ShapeDtypeStruct((2 * hl, R, C), F32), compiler_params=_params(2),
    )(bc, p, h2, h2, h2)


def _reduce_scatter(grads):
    x, y, c = lax.axis_index("x"), lax.axis_index("y"), lax.axis_index("c")
    cs = jnp.reshape(c, (1,)).astype(jnp.int32)
    bc = jnp.stack([2 * x + y, c]).astype(jnp.int32)
    sib = _rs_swap_halves(grads)
    parts = [_add_half(g, h, cs) for g, h in zip(grads, sib)]
    others = _rs_exchange(parts)
    return _rs_share([_add_blocks(p, o, bc) for p, o in zip(parts, others)])


WEIGHTS = ("norm_ffn1", "w_ffn1_in", "w_ffn1_out", "norm_mix", "w_mix_in", "b_forget", "w_pool", "pool_scale",
           "conv_w", "w_mix_out", "norm_ffn2", "w_ffn2_in", "w_ffn2_out", "norm_final")
BIG = ("w_ffn1_in", "w_ffn1_out", "w_mix_in", "w_mix_out", "w_ffn2_in", "w_ffn2_out")
SMALL = ("norm_ffn1", "norm_mix", "b_forget", "w_pool", "pool_scale", "conv_w", "norm_ffn2", "norm_final")


def _pad_lanes(a, width=LANES):
    return jnp.pad(a, ((0, 0), (0, width - a.shape[1])))


def _layer_params(l, small, gathered, D):
    DA, C, H = D // 2, D // 4, D // 2 // HEAD_DIM
    w_in = jnp.concatenate([gathered["w_mix_in"][l, b] for b in range(N_CHIPS)], axis=1)
    wqkv, wf, wrest = w_in[:, :3 * DA], _pad_lanes(w_in[:, 3 * DA:3 * DA + H]), w_in[:, 3 * DA + H:]
    gw = C // len(POOL_WINDOWS)
    wbd = jnp.zeros((C, C), F32)
    for gi in range(len(POOL_WINDOWS)):
        wbd = wbd.at[gi * gw:(gi + 1) * gw, gi * gw:(gi + 1) * gw].set(small["w_pool"][l, gi])
    cw = jnp.concatenate([gathered["conv_w"][l, b] for b in range(N_CHIPS)], axis=1)
    return dict(
        g1=small["norm_ffn1"][l][None], gm=small["norm_mix"][l][None], g2=small["norm_ffn2"][l][None],
        w1in=gathered["w_ffn1_in"][l], w2in=gathered["w_ffn2_in"][l],
        w1out=gathered["w_ffn1_out"][l].reshape(-1, D), w2out=gathered["w_ffn2_out"][l].reshape(-1, D),
        wqkv=wqkv, wrest=wrest, wf=wf, wp=jnp.concatenate([wqkv, wrest, wf], axis=1),
        wmixout=gathered["w_mix_out"][l].reshape(D, D),
        bias=_pad_lanes(small["b_forget"][l][None]), wbd=wbd.astype(BF16), ps=small["pool_scale"][l][None],
        cw=jnp.pad(cw, ((0, 8 - CONV_WIDTH), (0, 0))),
    )


def _ffn_fwd(x, g, w_in, w_out):
    h = _rmsnorm_fwd(x, g)
    gu, act = _ffn_in(h, w_in)
    return _ffn_out(act, w_out, x)[0], (x, h, gu, act)


def _ffn_bwd(dres, saved, g, w_in, w_out):
    x, h, gu, act = saved
    dgu = _ffn_bwd_act(dres, w_out, gu)
    dw_out = _ffn_dw_out(act, dres)[0]
    dw_in = _ffn_dw_in(h, dgu)[0]
    dh = _ffn_dh(dgu, w_in)[0]
    dx, dg = _rmsnorm_bwd(x, g, dh, dres)
    return dx, dg, dw_in, dw_out.reshape(N_CHIPS, -1, dw_out.shape[1])


def _mixer_fwd(x, P, B, S, tq):
    T, D = x.shape
    DA, C, H = D // 2, D // 4, D // 2 // HEAD_DIM
    hn = _rmsnorm_fwd(x, P["gm"])
    qkv = _proj("mix_qkv", hn, P["wqkv"], BF16).reshape(B, S, 3 * DA)
    rest = _proj("mix_rest", hn, P["wrest"], F32).reshape(B, S, 4 * C)
    fl = _proj("mix_f", hn, P["wf"], F32).reshape(B, S, LANES)
    drow = _decay_fwd(fl, P["bias"]).reshape(B, 8, S // tq, tq)
    o, lse = _attn_fwd(qkv, drow, H, tq)
    ypc, pooled = _mix_local_fwd(rest, P["wbd"], P["ps"], P["cw"])
    ycat = jnp.concatenate([o, ypc], axis=-1).reshape(T, D)
    return _proj("mix_out", ycat, P["wmixout"], F32, extra=x), (x, hn, qkv, rest, fl, drow, o, lse, pooled, ycat)


def _mixer_bwd(dres, saved, P, B, S, tq):
    x, hn, qkv, rest, fl, drow, o, lse, pooled, ycat = saved
    T, D = x.shape
    DA, C, H = D // 2, D // 4, D // 2 // HEAD_DIM
    dycat = _proj("mix_out_bwd", dres, P["wmixout"], F32, NT).reshape(B, S, D)
    dw_out = _dw("mix_out_dw", ycat, dres, BF16)
    dq, dk, dv, ddrow, ddcol = _attn_bwd(qkv, drow, o, lse, dycat, H, tq)
    dfl, dbias = _decay_bwd(ddrow.reshape(B, 8, S), ddcol, fl, P["bias"], H)
    drest, dwbd, dps, dcw = _mix_local_bwd(rest, pooled, dycat, P["wbd"], P["ps"], P["cw"])
    dproj = jnp.concatenate([dq, dk, dv, drest, dfl.astype(BF16)], axis=-1).reshape(T, 3 * DA + 4 * C + LANES)
    dwp = _dw("mix_in_dw", hn, dproj, F32)
    dhn = _proj("mix_in_bwd", dproj, P["wp"], F32, NT)
    dx, dg = _rmsnorm_bwd(x, P["gm"], dhn, dres)
    n_q, n_r = 3 * DA, 4 * C
    dw_in = jnp.concatenate([dwp[:, :n_q], dwp[:, n_q + n_r:n_q + n_r + H], dwp[:, n_q:n_q + n_r]], axis=1)
    dw_in = dw_in.reshape(D, N_CHIPS, -1).transpose(1, 0, 2).astype(BF16)
    gw = C // len(POOL_WINDOWS)
    dw_pool = jnp.stack([dwbd[gi * gw:(gi + 1) * gw, gi * gw:(gi + 1) * gw] for gi in range(len(POOL_WINDOWS))])
    small = dict(norm_mix=dg[0], b_forget=dbias[0, :H], w_pool=dw_pool, pool_scale=dps[0], conv_w=dcw[:CONV_WIDTH])
    return dx, small, dw_in, dw_out.reshape(N_CHIPS, -1, D)


def _local_step(x, target, small, gathered):
    B, S, D = x.shape
    L = small["norm_ffn1"].shape[0]
    tq = _tile(S, 256)
    xt = x.reshape(B * S, D)
    saved, params = [], []
    for l in range(L):
        P = _layer_params(l, small, gathered, D)
        xt, s1 = _ffn_fwd(xt, P["g1"], P["w1in"], P["w1out"])
        xt, s2 = _mixer_fwd(xt, P, B, S, tq)
        xt, s3 = _ffn_fwd(xt, P["g2"], P["w2in"], P["w2out"])
        saved.append((s1, s2, s3))
        params.append(P)
    dres, dgf, loss = _final_loss(xt, small["norm_final"][None], target.reshape(B * S, D))
    big = {k: [None] * L for k in BIG}
    sm = {k: [None] * L for k in SMALL if k != "norm_final"}
    for l in reversed(range(L)):
        P, (s1, s2, s3) = params[l], saved[l]
        dres, dg2, big["w_ffn2_in"][l], big["w_ffn2_out"][l] = _ffn_bwd(dres, s3, P["g2"], P["w2in"], P["w2out"])
        dres, smix, big["w_mix_in"][l], big["w_mix_out"][l] = _mixer_bwd(dres, s2, P, B, S, tq)
        dres, dg1, big["w_ffn1_in"][l], big["w_ffn1_out"][l] = _ffn_bwd(dres, s1, P["g1"], P["w1in"], P["w1out"])
        sm["norm_ffn1"][l], sm["norm_ffn2"][l] = dg1[0], dg2[0]
        for k, val in smix.items():
            sm[k][l] = val
    big = {k: jnp.stack(val) for k, val in big.items()}
    sm = {k: jnp.stack(val) for k, val in sm.items()}
    sm["norm_final"] = dgf[0]
    return loss[0, 0], dres.reshape(B, S, D), big, sm


def _pack(parts, extra=()):
    flat = jnp.concatenate([p.reshape(-1) for p in parts] + [jnp.reshape(e, (1,)) for e in extra])
    n = -(-flat.shape[0] // (8 * LANES)) * 8
    return jnp.pad(flat, (0, n * LANES - flat.shape[0])).reshape(n, LANES)


def _unpack(buf, shapes):
    flat, out, at = buf.reshape(-1), [], 0
    for s in shapes:
        n = math.prod(s)
        out.append(flat[at:at + n].reshape(s))
        at += n
    return out, flat[at:]


def kernel(x, norm_ffn1, w_ffn1_in, w_ffn1_out, norm_mix, w_mix_in, b_forget, w_pool, pool_scale, conv_w, w_mix_out, norm_ffn2, w_ffn2_in, w_ffn2_out, norm_final, loss_target, m_norm_ffn1, m_w_ffn1_in, m_w_ffn1_out, m_norm_mix, m_w_mix_in, m_b_forget, m_w_pool, m_pool_scale, m_conv_w, m_w_mix_out, m_norm_ffn2, m_w_ffn2_in, m_w_ffn2_out, m_norm_final, v_norm_ffn1, v_w_ffn1_in, v_w_ffn1_out, v_norm_mix, v_w_mix_in, v_b_forget, v_w_pool, v_pool_scale, v_conv_w, v_w_mix_out, v_norm_ffn2, v_w_ffn2_in, v_w_ffn2_out, v_norm_final):
    w = dict(zip(WEIGHTS, (norm_ffn1, w_ffn1_in, w_ffn1_out, norm_mix, w_mix_in, b_forget, w_pool, pool_scale, conv_w, w_mix_out, norm_ffn2, w_ffn2_in, w_ffn2_out, norm_final)))
    m = dict(zip(WEIGHTS, (m_norm_ffn1, m_w_ffn1_in, m_w_ffn1_out, m_norm_mix, m_w_mix_in, m_b_forget, m_w_pool, m_pool_scale, m_conv_w, m_w_mix_out, m_norm_ffn2, m_w_ffn2_in, m_w_ffn2_out, m_norm_final)))
    v = dict(zip(WEIGHTS, (v_norm_ffn1, v_w_ffn1_in, v_w_ffn1_out, v_norm_mix, v_w_mix_in, v_b_forget, v_w_pool, v_pool_scale, v_conv_w, v_w_mix_out, v_norm_ffn2, v_w_ffn2_in, v_w_ffn2_out, v_norm_final)))
    block = 2 * lax.axis_index("x") + lax.axis_index("y")

    bs = jnp.reshape(block, (1,)).astype(jnp.int32)
    gathered = _all_gather([_place_shard(w[k], bs, BF16) for k in BIG] + [_place_shard(w["conv_w"], bs, F32)])
    gathered = dict(zip(BIG + ("conv_w",), gathered))
    small = {k: w[k] for k in SMALL}
    loss, grad_x, big, sm = _local_step(x, loss_target, small, gathered)

    grads = dict(zip(BIG, _reduce_scatter([big[k] for k in BIG])))
    order = [k for k in SMALL]
    total = _all_reduce_small(_pack([sm[k] for k in order], extra=(loss,)))
    parts, rest = _unpack(total, [sm[k].shape for k in order])
    grads.update(zip(order, parts))
    loss = rest[0]
    cs = conv_w.shape[2]
    grads["conv_w"] = lax.dynamic_slice_in_dim(grads["conv_w"], block * cs, cs, axis=2)

    delta, new_m, new_v = {}, {}, {}
    for k in BIG:
        two_d = lambda a: a.reshape(-1, a.shape[-1])
        d, nm, nv = _adamw(two_d(w[k]), two_d(grads[k]), two_d(m[k]), two_d(v[k]))
        delta[k], new_m[k], new_v[k] = d.reshape(w[k].shape), nm.reshape(w[k].shape), nv.reshape(w[k].shape)
    d, nm, nv = _adamw(*[_pack([t[k] for k in order]) for t in (w, grads, m, v)])
    shapes = [w[k].shape for k in order]
    for res, packed in ((delta, d), (new_m, nm), (new_v, nv)):
        res.update(zip(order, _unpack(packed, shapes)[0]))
    return (loss, grad_x, *[grads[k] for k in WEIGHTS], *[delta[k] for k in WEIGHTS],
            *[new_m[k] for k in WEIGHTS], *[new_v[k] for k in WEIGHTS])
```

```python
import functools
import math

import jax
import jax.numpy as jnp
from jax import lax
from jax.experimental import pallas as pl
from jax.experimental.pallas import tpu as pltpu

F32 = jnp.float32
BF16 = jnp.bfloat16
MESH = pl.DeviceIdType.MESH

HEAD_DIM = 64
POOL_WINDOWS = (2, 4, 8, 16)
CONV_WIDTH = 3
RMS_EPS = 1e-6
ADAM_LR = 0.001
ADAM_B1 = 0.9
ADAM_B2 = 0.999
ADAM_EPS = 1e-08
ADAM_WD = 0.01
ADAM_STEP = 10

LANES = 128
VMEM_LIMIT = 56 * 1024 * 1024
N_CHIPS = 4
N_DEV = 8

NN = (((1,), (0,)), ((), ()))
NT = (((1,), (1,)), ((), ()))
TN = (((0,), (0,)), ((), ()))


def _tile(n, pref):
    t = pref
    while t >= 8:
        if n % t == 0:
            return t
        t //= 2
    return n


def _params(n_grid):
    return pltpu.CompilerParams(dimension_semantics=("arbitrary",) * n_grid, vmem_limit_bytes=VMEM_LIMIT)


def _dot(a, b, dims):
    return lax.dot_general(a, b, dims, preferred_element_type=F32)


def _mm(name, dims, operands, in_specs, out_shape, out_specs, grid, acc_shape, epilogue):
    n_in, n_out, nk = len(operands), len(out_shape), grid[-1]

    def kern(*refs):
        extras, outs = refs[2:n_in], refs[n_in:n_in + n_out]
        if nk == 1:
            epilogue(_dot(refs[0][...].astype(BF16), refs[1][...].astype(BF16), dims), extras, outs)
            return
        acc = refs[n_in + n_out]
        k = pl.program_id(len(grid) - 1)

        @pl.when(k == 0)
        def _():
            acc[...] = jnp.zeros_like(acc)

        acc[...] += _dot(refs[0][...].astype(BF16), refs[1][...].astype(BF16), dims)

        @pl.when(k == nk - 1)
        def _():
            epilogue(acc[...], extras, outs)

    return pl.pallas_call(
        kern, name=name, grid=grid, in_specs=in_specs, out_specs=out_specs, out_shape=out_shape,
        scratch_shapes=[pltpu.VMEM(acc_shape, F32)] if nk > 1 else [],
        compiler_params=_params(len(grid)),
    )(*operands)


def _store(scale=None, dtype=None):
    def ep(acc, extras, outs):
        v = acc if scale is None else acc * scale
        outs[0][...] = v.astype(outs[0].dtype)
    return ep


def _residual(scale):
    def ep(acc, extras, outs):
        outs[0][...] = extras[0][...] + scale * acc
    return ep


def _rmsnorm_fwd(x, g):
    T, D = x.shape
    tr = _tile(T, 512)

    def kern(x_ref, g_ref, o_ref):
        xv = x_ref[...]
        r = lax.rsqrt(jnp.mean(xv * xv, axis=-1, keepdims=True) + RMS_EPS)
        o_ref[...] = (xv * r * g_ref[...]).astype(BF16)

    return pl.pallas_call(
        kern, name="rmsnorm_fwd", grid=(T // tr,),
        in_specs=[pl.BlockSpec((tr, D), lambda i: (i, 0)), pl.BlockSpec((1, D), lambda i: (0, 0))],
        out_specs=pl.BlockSpec((tr, D), lambda i: (i, 0)),
        out_shape=jax.ShapeDtypeStruct((T, D), BF16), compiler_params=_params(1),
    )(x, g)


def _rmsnorm_bwd(x, g, dh, dres):
    T, D = x.shape
    tr = _tile(T, 256)

    def kern(x_ref, g_ref, dh_ref, dres_ref, dx_ref, dg_ref):
        xv, dhv = x_ref[...], dh_ref[...]
        r = lax.rsqrt(jnp.mean(xv * xv, axis=-1, keepdims=True) + RMS_EPS)
        y = xv * r
        dy = dhv * g_ref[...]
        dx_ref[...] = dres_ref[...] + r * (dy - y * jnp.mean(dy * y, axis=-1, keepdims=True))
        part = jnp.sum(dhv * y, axis=0, keepdims=True)

        @pl.when(pl.program_id(0) == 0)
        def _():
            dg_ref[...] = part

        @pl.when(pl.program_id(0) > 0)
        def _():
            dg_ref[...] += part

    row = pl.BlockSpec((tr, D), lambda i: (i, 0))
    vec = pl.BlockSpec((1, D), lambda i: (0, 0))
    return pl.pallas_call(
        kern, name="rmsnorm_bwd", grid=(T // tr,), in_specs=[row, vec, row, row], out_specs=[row, vec],
        out_shape=[jax.ShapeDtypeStruct((T, D), F32), jax.ShapeDtypeStruct((1, D), F32)],
        compiler_params=_params(1),
    )(x, g, dh, dres)


def _final_loss(x, g, target):
    T, D = x.shape
    tr = _tile(T, 256)

    def kern(x_ref, g_ref, t_ref, dx_ref, dg_ref, loss_ref):
        xv = x_ref[...]
        r = lax.rsqrt(jnp.mean(xv * xv, axis=-1, keepdims=True) + RMS_EPS)
        y = xv * r
        err = y * g_ref[...] - t_ref[...]
        lpart = 0.5 * jnp.sum(jnp.mean(err * err, axis=-1, keepdims=True), axis=0, keepdims=True)
        dh = err * (1.0 / D)
        dy = dh * g_ref[...]
        dx_ref[...] = r * (dy - y * jnp.mean(dy * y, axis=-1, keepdims=True))
        part = jnp.sum(dh * y, axis=0, keepdims=True)
        lrow = jnp.broadcast_to(lpart, (1, LANES))

        @pl.when(pl.program_id(0) == 0)
        def _():
            dg_ref[...] = part
            loss_ref[...] = lrow

        @pl.when(pl.program_id(0) > 0)
        def _():
            dg_ref[...] += part
            loss_ref[...] += lrow

    row = pl.BlockSpec((tr, D), lambda i: (i, 0))
    vec = pl.BlockSpec((1, D), lambda i: (0, 0))
    return pl.pallas_call(
        kern, name="final_loss", grid=(T // tr,), in_specs=[row, vec, row],
        out_specs=[row, vec, pl.BlockSpec((1, LANES), lambda i: (0, 0))],
        out_shape=[jax.ShapeDtypeStruct((T, D), F32), jax.ShapeDtypeStruct((1, D), F32),
                   jax.ShapeDtypeStruct((1, LANES), F32)],
        compiler_params=_params(1),
    )(x, g, target)


def _ffn_in(h, w4):
    T, D = h.shape
    w4, l = w4
    Fh = w4.shape[3]
    F = 2 * Fh
    tm = _tile(T, 512)

    def kern(h_ref, wg_ref, wu_ref, jac_ref, act_ref):
        hv = h_ref[...]
        gate = _dot(hv, wg_ref[...], NN)
        up = _dot(hv, wu_ref[...], NN)
        sg = jax.nn.sigmoid(gate)
        silu = gate * sg
        jac_ref[0] = (up * (sg + silu * (1.0 - sg))).astype(BF16)
        jac_ref[1] = silu.astype(BF16)
        act_ref[...] = (silu * up).astype(BF16)

    return pl.pallas_call(
        kern, name="ffn_in", grid=(2, T // tm),
        in_specs=[pl.BlockSpec((tm, D), lambda j, i: (i, 0)),
                  pl.BlockSpec((None, None, D, Fh), lambda j, i: (l, j, 0, 0)),
                  pl.BlockSpec((None, None, D, Fh), lambda j, i: (l, 2 + j, 0, 0))],
        out_specs=[pl.BlockSpec((2, tm, Fh), lambda j, i: (0, i, j)),
                   pl.BlockSpec((tm, Fh), lambda j, i: (i, j))],
        out_shape=[jax.ShapeDtypeStruct((2, T, F), BF16), jax.ShapeDtypeStruct((T, F), BF16)],
        compiler_params=_params(2),
    )(h, w4, w4)


def _ffn_out(act, w_out, x):
    T, F = act.shape
    w_out, l = w_out
    D = w_out.shape[2]
    tm = _tile(T, 512)
    return _mm("ffn_out", NN, [act, w_out, x],
               [pl.BlockSpec((tm, F), lambda i, k: (i, 0)), pl.BlockSpec((None, F, D), lambda i, k: (l, 0, 0)),
                pl.BlockSpec((tm, D), lambda i, k: (i, 0))],
               [jax.ShapeDtypeStruct((T, D), F32)], [pl.BlockSpec((tm, D), lambda i, k: (i, 0))],
               (T // tm, 1), None, _residual(0.5))


def _ffn_bwd_act(dres, w_out, jac):
    T, D = dres.shape
    w_out, l = w_out
    F = w_out.shape[1]
    Fh = F // 2
    tm = _tile(T, 512)

    def kern(d_ref, w_ref, jac_ref, o_ref):
        dact = 0.5 * _dot(d_ref[...].astype(BF16), w_ref[...], NT)
        o_ref[0] = (dact * jac_ref[0].astype(F32)).astype(BF16)
        o_ref[1] = (dact * jac_ref[1].astype(F32)).astype(BF16)

    return pl.pallas_call(
        kern, name="ffn_bwd_act", grid=(2, T // tm),
        in_specs=[pl.BlockSpec((tm, D), lambda j, i: (i, 0)), pl.BlockSpec((None, Fh, D), lambda j, i: (l, j, 0)),
                  pl.BlockSpec((2, tm, Fh), lambda j, i: (0, i, j))],
        out_specs=pl.BlockSpec((2, tm, Fh), lambda j, i: (0, i, j)),
        out_shape=jax.ShapeDtypeStruct((2, T, F), BF16), compiler_params=_params(2),
    )(dres, w_out, jac)


def _ffn_dw_out(act, dres):
    T, F = act.shape
    D = dres.shape[1]
    tm, tk = F // 2, _tile(T, 1024)
    return _mm("ffn_dw_out", TN, [act, dres],
               [pl.BlockSpec((tk, tm), lambda i, k: (k, i)), pl.BlockSpec((tk, D), lambda i, k: (k, 0))],
               [jax.ShapeDtypeStruct((F, D), BF16)], [pl.BlockSpec((tm, D), lambda i, k: (i, 0))],
               (2, T // tk), (tm, D), _store(0.5))


def _ffn_dw_in(h, dgu):
    T, D = h.shape
    Fh = dgu.shape[2] // 2
    tk = _tile(T, 1024)
    return _mm("ffn_dw_in", TN, [h, dgu],
               [pl.BlockSpec((tk, D), lambda j, k: (k, 0)),
                pl.BlockSpec((None, tk, Fh), lambda j, k: (j // 2, k, j % 2))],
               [jax.ShapeDtypeStruct((4, D, Fh), BF16)], [pl.BlockSpec((None, D, Fh), lambda j, k: (j, 0, 0))],
               (4, T // tk), (D, Fh), _store())


def _ffn_dh(dgu, w4):
    T = dgu.shape[1]
    w4, l = w4
    D, Fh = w4.shape[2], w4.shape[3]
    tm = _tile(T, 1024)
    return _mm("ffn_dh", NT, [dgu, w4],
               [pl.BlockSpec((None, tm, Fh), lambda i, k: (k // 2, i, k % 2)),
                pl.BlockSpec((None, None, D, Fh), lambda i, k: (l, k, 0, 0))],
               [jax.ShapeDtypeStruct((T, D), F32)], [pl.BlockSpec((tm, D), lambda i, k: (i, 0))],
               (T // tm, 4), (tm, D), _store())


def _proj(name, a, w, out_dtype, dims=NN, extra=None, scale=None):
    T, K = a.shape
    w, l = w
    N = w.shape[2] if dims == NN else w.shape[1]
    tm = _tile(T, 512)
    ops = [a, w] + ([extra] if extra is not None else [])
    specs = [pl.BlockSpec((tm, K), lambda i, k: (i, 0)), pl.BlockSpec((None,) + w.shape[1:], lambda i, k: (l, 0, 0))]
    if extra is not None:
        specs.append(pl.BlockSpec((tm, N), lambda i, k: (i, 0)))
    ep = _residual(1.0) if extra is not None else _store(scale)
    return _mm(name, dims, ops, specs, [jax.ShapeDtypeStruct((T, N), out_dtype)],
               [pl.BlockSpec((tm, N), lambda i, k: (i, 0))], (T // tm, 1), None, ep)[0]


def _dw(name, a, d, out_dtype):
    T, M = a.shape
    N = d.shape[1]
    tk = _tile(T, 1024 if M * N <= 1024 * 1408 else 512)
    return _mm(name, TN, [a, d],
               [pl.BlockSpec((tk, M), lambda i, k: (k, 0)), pl.BlockSpec((tk, N), lambda i, k: (k, 0))],
               [jax.ShapeDtypeStruct((M, N), out_dtype)], [pl.BlockSpec((M, N), lambda i, k: (0, 0))],
               (1, T // tk), (M, N), _store())[0]


def _log_sigmoid(z):
    return jnp.minimum(z, 0.0) - jnp.log(1.0 + jnp.exp(-jnp.abs(z)))


def _decay_fwd(fl, bias):
    B, S, _ = fl.shape

    def kern(fl_ref, b_ref, o_ref):
        d = _log_sigmoid(fl_ref[...] + b_ref[...])
        row = lax.broadcasted_iota(jnp.int32, (S, LANES), 0)
        sh = 1
        while sh < S:
            d = d + jnp.where(row >= sh, pltpu.roll(d, sh, 0), 0.0)
            sh *= 2
        o_ref[...] = d.T[0:8, :]

    return pl.pallas_call(
        kern, name="decay_fwd", grid=(B,),
        in_specs=[pl.BlockSpec((None, S, LANES), lambda b: (b, 0, 0)), pl.BlockSpec((1, LANES), lambda b: (0, 0))],
        out_specs=pl.BlockSpec((None, 8, S), lambda b: (b, 0, 0)),
        out_shape=jax.ShapeDtypeStruct((B, 8, S), F32), compiler_params=_params(1),
    )(fl, bias)


def _decay_bwd(ddrow, ddcol, fl, bias, n_heads):
    B, S, _ = fl.shape

    def kern(dd_ref, ddc_ref, fl_ref, b_ref, dfl_ref, db_ref):
        dd = jnp.concatenate([dd_ref[...], jnp.zeros((LANES - 8, S), F32)], axis=0).T + ddc_ref[...]
        row = lax.broadcasted_iota(jnp.int32, (S, LANES), 0)
        lane = lax.broadcasted_iota(jnp.int32, (S, LANES), 1)
        sh = 1
        while sh < S:
            dd = dd + jnp.where(row < S - sh, pltpu.roll(dd, S - sh, 0), 0.0)
            sh *= 2
        z = fl_ref[...] + b_ref[...]
        dfl = jnp.where(lane < n_heads, dd / (1.0 + jnp.exp(z)), 0.0)
        dfl_ref[...] = dfl
        part = jnp.sum(dfl, axis=0, keepdims=True)

        @pl.when(pl.program_id(0) == 0)
        def _():
            db_ref[...] = part

        @pl.when(pl.program_id(0) > 0)
        def _():
            db_ref[...] += part

    return pl.pallas_call(
        kern, name="decay_bwd", grid=(B,),
        in_specs=[pl.BlockSpec((None, 8, S), lambda b: (b, 0, 0)), pl.BlockSpec((None, S, LANES), lambda b: (b, 0, 0)),
                  pl.BlockSpec((None, S, LANES), lambda b: (b, 0, 0)), pl.BlockSpec((1, LANES), lambda b: (0, 0))],
        out_specs=[pl.BlockSpec((None, S, LANES), lambda b: (b, 0, 0)), pl.BlockSpec((1, LANES), lambda b: (0, 0))],
        out_shape=[jax.ShapeDtypeStruct((B, S, LANES), F32), jax.ShapeDtypeStruct((1, LANES), F32)],
        compiler_params=_params(1),
    )(ddrow, ddcol, fl, bias)


def _attn_fwd(qkv, drow, n_heads, tq):
    B, S, _ = qkv.shape
    DA = n_heads * HEAD_DIM
    scale = HEAD_DIM ** -0.5

    n_pairs = n_heads // 2

    def kern(q_ref, k_ref, v_ref, dr_ref, o_ref, lse_ref):
        i = pl.program_id(1)
        lane = lax.broadcasted_iota(jnp.int32, (tq, LANES), 1)
        low = lane < HEAD_DIM
        causal = lax.broadcasted_iota(jnp.int32, (tq, tq), 1) <= lax.broadcasted_iota(jnp.int32, (tq, tq), 0)
        qms = []
        for p in range(n_pairs):
            q2 = q_ref[:, LANES * p:LANES * (p + 1)] * scale
            qms += [jnp.where(low, q2, jnp.zeros_like(q2)), jnp.where(low, jnp.zeros_like(q2), q2)]

        def step(j, carry, masked):
            ms, ls, accs = carry
            ks = pl.multiple_of(j * tq, tq)
            new_m, new_l, new_acc = [], [], []
            for p in range(n_pairs):
                cols = slice(LANES * p, LANES * (p + 1))
                k2, v2 = k_ref[pl.ds(ks, tq), cols], v_ref[pl.ds(ks, tq), cols]
                alphas, pvs = [], []
                for h in (2 * p, 2 * p + 1):
                    s = _dot(qms[h], k2, NT) - dr_ref[h, pl.ds(j, 1), :]
                    if masked:
                        s = jnp.where(causal, s, -jnp.inf)
                    m_new = jnp.maximum(ms[h], jnp.max(s, axis=1, keepdims=True))
                    alpha = jnp.exp(ms[h] - m_new)
                    pm = jnp.exp(s - m_new)
                    new_m.append(m_new)
                    new_l.append(alpha * ls[h] + jnp.sum(pm, axis=1, keepdims=True))
                    alphas.append(alpha)
                    pvs.append(_dot(pm.astype(BF16), v2, NN))
                new_acc.append(jnp.where(low, alphas[0], alphas[1]) * accs[p] + jnp.where(low, pvs[0], pvs[1]))
            return tuple(new_m), tuple(new_l), tuple(new_acc)

        init = (tuple(jnp.full((tq, 1), -jnp.inf, F32) for _ in range(n_heads)),
                tuple(jnp.zeros((tq, 1), F32) for _ in range(n_heads)),
                tuple(jnp.zeros((tq, LANES), F32) for _ in range(n_pairs)))
        ms, ls, accs = step(i, lax.fori_loop(0, i, functools.partial(step, masked=False), init), True)
        lse_mat = jnp.zeros((tq, LANES), F32)
        for p in range(n_pairs):
            l0, l1 = ls[2 * p], ls[2 * p + 1]
            o_ref[:, LANES * p:LANES * (p + 1)] = (accs[p] / jnp.where(low, l0, l1)).astype(BF16)
            lse_mat = jnp.where(lane == 2 * p, ms[2 * p] + jnp.log(l0), lse_mat)
            lse_mat = jnp.where(lane == 2 * p + 1, ms[2 * p + 1] + jnp.log(l1), lse_mat)
        lse_ref[...] = lse_mat

    nq = S // tq
    return pl.pallas_call(
        kern, name="attn_fwd", grid=(B, nq),
        in_specs=[pl.BlockSpec((None, tq, DA), lambda b, i: (b, i, 0)),
                  pl.BlockSpec((None, S, DA), lambda b, i: (b, 0, 1)),
                  pl.BlockSpec((None, S, DA), lambda b, i: (b, 0, 2)),
                  pl.BlockSpec((None, 8, nq, tq), lambda b, i: (b, 0, 0, 0))],
        out_specs=[pl.BlockSpec((None, tq, DA), lambda b, i: (b, i, 0)),
                   pl.BlockSpec((None, tq, LANES), lambda b, i: (b, i, 0))],
        out_shape=[jax.ShapeDtypeStruct((B, S, DA), BF16), jax.ShapeDtypeStruct((B, S, LANES), F32)],
        compiler_params=_params(2),
    )(qkv, qkv, qkv, drow)


def _attn_bwd(qkv, drow, o, lse, dycat, n_heads, tq):
    B, S, _ = qkv.shape
    DA = n_heads * HEAD_DIM
    scale = HEAD_DIM ** -0.5
    nq = S // tq

    n_pairs = n_heads // 2

    def kern(q_ref, k_ref, v_ref, dr_ref, o_ref, lse_ref, do_ref, dq_ref, dk_ref, dv_ref, ddr_ref, ddc_ref,
             dk_acc, dv_acc, qm_s, dom_s, delta_s, rs_s, dq_s):
        i = pl.program_id(1)

        @pl.when(i == 0)
        def _():
            dk_acc[...] = jnp.zeros_like(dk_acc)
            dv_acc[...] = jnp.zeros_like(dv_acc)
            ddr_ref[...] = jnp.zeros_like(ddr_ref)

        lane = lax.broadcasted_iota(jnp.int32, (tq, LANES), 1)
        low = lane < HEAD_DIM
        causal = lax.broadcasted_iota(jnp.int32, (tq, tq), 1) <= lax.broadcasted_iota(jnp.int32, (tq, tq), 0)
        for p in range(n_pairs):
            cols = slice(LANES * p, LANES * (p + 1))
            q2 = q_ref[:, cols] * scale
            do_f = do_ref[:, cols]
            do2 = do_f.astype(BF16)
            prod = do_f * o_ref[:, cols].astype(F32)
            qm_s[2 * p] = jnp.where(low, q2, jnp.zeros_like(q2))
            qm_s[2 * p + 1] = jnp.where(low, jnp.zeros_like(q2), q2)
            dom_s[2 * p] = jnp.where(low, do2, jnp.zeros_like(do2))
            dom_s[2 * p + 1] = jnp.where(low, jnp.zeros_like(do2), do2)
            delta_s[2 * p] = jnp.sum(jnp.where(low, prod, 0.0), axis=1, keepdims=True)
            delta_s[2 * p + 1] = jnp.sum(jnp.where(low, 0.0, prod), axis=1, keepdims=True)
            dq_s[p] = jnp.zeros((tq, LANES), F32)
        rs_s[...] = jnp.zeros(rs_s.shape, F32)

        def step(j, masked):
            ks = pl.multiple_of(j * tq, tq)
            for p in range(n_pairs):
                cols = slice(LANES * p, LANES * (p + 1))
                k2, v2 = k_ref[pl.ds(ks, tq), cols], v_ref[pl.ds(ks, tq), cols]
                dvs, dks, dqs = [], [], []
                for h in (2 * p, 2 * p + 1):
                    qm, dom = qm_s[h], dom_s[h]
                    s = _dot(qm, k2, NT) - dr_ref[h, pl.ds(j, 1), :]
                    if masked:
                        s = jnp.where(causal, s, -jnp.inf)
                    pm = jnp.exp(s - lse_ref[:, h:h + 1])
                    ds = pm * (_dot(dom, v2, NT) - delta_s[h])
                    ddr_ref[h, pl.ds(j, 1), :] -= jnp.sum(ds, axis=0, keepdims=True)
                    rs_s[h] += jnp.sum(ds, axis=1, keepdims=True)
                    dsb = ds.astype(BF16)
                    dvs.append(_dot(pm.astype(BF16), dom, TN))
                    dks.append(_dot(dsb, qm, TN))
                    dqs.append(_dot(dsb, k2, NN))
                dv_acc[pl.ds(ks, tq), cols] += dvs[0] + dvs[1]
                dk_acc[pl.ds(ks, tq), cols] += dks[0] + dks[1]
                dq_s[p] += jnp.where(low, dqs[0], dqs[1])

        def body(j, carry):
            step(j, False)
            return carry

        lax.fori_loop(0, i, body, 0)
        step(i, True)
        ddc = jnp.zeros((tq, LANES), F32)
        for p in range(n_pairs):
            dq_ref[:, LANES * p:LANES * (p + 1)] = (dq_s[p] * scale).astype(BF16)
            ddc = jnp.where(lane == 2 * p, rs_s[2 * p], ddc)
            ddc = jnp.where(lane == 2 * p + 1, rs_s[2 * p + 1], ddc)
        ddc_ref[...] = ddc

        @pl.when(i == nq - 1)
        def _():
            dk_ref[...] = dk_acc[...].astype(BF16)
            dv_ref[...] = dv_acc[...].astype(BF16)

    tile = pl.BlockSpec((None, tq, DA), lambda b, i: (b, i, 0))
    seq = pl.BlockSpec((None, S, DA), lambda b, i: (b, 0, 0))
    dec = pl.BlockSpec((None, 8, nq, tq), lambda b, i: (b, 0, 0, 0))
    return pl.pallas_call(
        kern, name="attn_bwd", grid=(B, nq),
        in_specs=[tile, pl.BlockSpec((None, S, DA), lambda b, i: (b, 0, 1)),
                  pl.BlockSpec((None, S, DA), lambda b, i: (b, 0, 2)), dec, tile,
                  pl.BlockSpec((None, tq, LANES), lambda b, i: (b, i, 0)), tile],
        out_specs=[tile, seq, seq, dec, pl.BlockSpec((None, tq, LANES), lambda b, i: (b, i, 0))],
        out_shape=[jax.ShapeDtypeStruct((B, S, DA), BF16)] * 3 + [jax.ShapeDtypeStruct((B, 8, nq, tq), F32),
                                                                  jax.ShapeDtypeStruct((B, S, LANES), F32)],
        scratch_shapes=[pltpu.VMEM((S, DA), F32), pltpu.VMEM((S, DA), F32),
                        pltpu.VMEM((n_heads, tq, LANES), BF16), pltpu.VMEM((n_heads, tq, LANES), BF16),
                        pltpu.VMEM((n_heads, tq, 1), F32), pltpu.VMEM((n_heads, tq, 1), F32),
                        pltpu.VMEM((n_pairs, tq, LANES), F32)],
        compiler_params=_params(2),
    )(qkv, qkv, qkv, drow, o, lse, dycat)


def _down(v, d, row):
    return jnp.where(row >= d, pltpu.roll(v, d, 0), 0.0)


def _up(v, d, row, S):
    return jnp.where(row < S - d, pltpu.roll(v, S - d, 0), 0.0)


def _window(v, shift, group):
    sums, acc, d = [], v, 1
    for _ in POOL_WINDOWS:
        acc = acc + shift(acc, d)
        sums.append(acc)
        d *= 2
    out = sums[-1]
    for gi in range(len(POOL_WINDOWS) - 2, -1, -1):
        out = jnp.where(group == gi, sums[gi], out)
    return out


def _pool_count(row, group):
    w = jnp.full(row.shape, POOL_WINDOWS[-1], jnp.int32)
    for gi in range(len(POOL_WINDOWS) - 2, -1, -1):
        w = jnp.where(group == gi, POOL_WINDOWS[gi], w)
    return jnp.minimum(row + 1, w).astype(F32)


def _mix_local_fwd(rest, wbd, ps, cw):
    B, S, C4 = rest.shape
    C = C4 // 4
    gw = C // len(POOL_WINDOWS)

    def kern(r_ref, w_ref, ps_ref, cw_ref, y_ref, pooled_ref):
        row = lax.broadcasted_iota(jnp.int32, (S, C), 0)
        group = lax.broadcasted_iota(jnp.int32, (S, C), 1) // gw
        u = r_ref[:, 0:C]
        pooled = _window(u, lambda v, d: _down(v, d, row), group) / _pool_count(row, group) - u
        pb = pooled.astype(BF16)
        pooled_ref[...] = pb
        y_ref[:, 0:C] = (_dot(pb, w_ref[...], NN) * ps_ref[...]).astype(BF16)
        uc = r_ref[:, 2 * C:3 * C] * r_ref[:, 3 * C:4 * C]
        y = cw_ref[0:1, :] * _down(uc, 2, row) + cw_ref[1:2, :] * _down(uc, 1, row) + cw_ref[2:3, :] * uc
        y_ref[:, C:2 * C] = (r_ref[:, C:2 * C] * y).astype(BF16)

    return pl.pallas_call(
        kern, name="mix_local_fwd", grid=(B,),
        in_specs=[pl.BlockSpec((None, S, C4), lambda b: (b, 0, 0)), pl.BlockSpec((C, C), lambda b: (0, 0)),
                  pl.BlockSpec((1, C), lambda b: (0, 0)), pl.BlockSpec((8, C), lambda b: (0, 0))],
        out_specs=[pl.BlockSpec((None, S, 2 * C), lambda b: (b, 0, 0)), pl.BlockSpec((None, S, C), lambda b: (b, 0, 0))],
        out_shape=[jax.ShapeDtypeStruct((B, S, 2 * C), BF16), jax.ShapeDtypeStruct((B, S, C), BF16)],
        compiler_params=_params(1),
    )(rest, wbd, ps, cw)


def _mix_local_bwd(rest, pooled, dycat, wbd, ps, cw):
    B, S, C4 = rest.shape
    C = C4 // 4
    gw = C // len(POOL_WINDOWS)

    def kern(r_ref, pooled_ref, d_ref, w_ref, ps_ref, cw_ref, dr_ref, dw_ref, dps_ref, dcw_ref):
        row = lax.broadcasted_iota(jnp.int32, (S, C), 0)
        group = lax.broadcasted_iota(jnp.int32, (S, C), 1) // gw
        dyp = d_ref[:, 0:C]
        dyc = d_ref[:, C:2 * C]
        pb = pooled_ref[...]
        dps = jnp.sum(dyp * _dot(pb, w_ref[...], NN), axis=0, keepdims=True)
        dzb = (dyp * ps_ref[...]).astype(BF16)
        dw = _dot(pb, dzb, TN)
        dpooled = _dot(dzb, w_ref[...], NT)
        g = dpooled / _pool_count(row, group)
        dr_ref[:, 0:C] = (_window(g, lambda v, d: _up(v, d, row, S), group) - dpooled).astype(BF16)
        cc, ch = r_ref[:, 2 * C:3 * C], r_ref[:, 3 * C:4 * C]
        uc = cc * ch
        u1, u2 = _down(uc, 1, row), _down(uc, 2, row)
        y = cw_ref[0:1, :] * u2 + cw_ref[1:2, :] * u1 + cw_ref[2:3, :] * uc
        dr_ref[:, C:2 * C] = (dyc * y).astype(BF16)
        dy = dyc * r_ref[:, C:2 * C]
        duc = cw_ref[0:1, :] * _up(dy, 2, row, S) + cw_ref[1:2, :] * _up(dy, 1, row, S) + cw_ref[2:3, :] * dy
        dr_ref[:, 2 * C:3 * C] = (duc * ch).astype(BF16)
        dr_ref[:, 3 * C:4 * C] = (duc * cc).astype(BF16)
        dcw = jnp.concatenate([jnp.sum(dy * u2, axis=0, keepdims=True), jnp.sum(dy * u1, axis=0, keepdims=True),
                               jnp.sum(dy * uc, axis=0, keepdims=True), jnp.zeros((5, C), F32)], axis=0)

        @pl.when(pl.program_id(0) == 0)
        def _():
            dw_ref[...] = dw
            dps_ref[...] = dps
            dcw_ref[...] = dcw

        @pl.when(pl.program_id(0) > 0)
        def _():
            dw_ref[...] += dw
            dps_ref[...] += dps
            dcw_ref[...] += dcw

    full = lambda shape: pl.BlockSpec(shape, lambda b: (0, 0))
    return pl.pallas_call(
        kern, name="mix_local_bwd", grid=(B,),
        in_specs=[pl.BlockSpec((None, S, C4), lambda b: (b, 0, 0)), pl.BlockSpec((None, S, C), lambda b: (b, 0, 0)),
                  pl.BlockSpec((None, S, 2 * C), lambda b: (b, 0, 1)), full((C, C)), full((1, C)), full((8, C))],
        out_specs=[pl.BlockSpec((None, S, C4), lambda b: (b, 0, 0)), full((C, C)), full((1, C)), full((8, C))],
        out_shape=[jax.ShapeDtypeStruct((B, S, C4), BF16), jax.ShapeDtypeStruct((C, C), F32),
                   jax.ShapeDtypeStruct((1, C), F32), jax.ShapeDtypeStruct((8, C), F32)],
        compiler_params=_params(1),
    )(rest, pooled, dycat, wbd, ps, cw)


def _adamw(w, g, m, v):
    R, C = w.shape
    tr = _tile(R, 512)

    def kern(w_ref, g_ref, m_ref, v_ref, d_ref, nm_ref, nv_ref):
        gv = g_ref[...]
        nm = ADAM_B1 * m_ref[...] + (1.0 - ADAM_B1) * gv
        nv = ADAM_B2 * v_ref[...] + (1.0 - ADAM_B2) * (gv * gv)
        m_hat = nm / (1.0 - ADAM_B1 ** ADAM_STEP)
        v_hat = nv / (1.0 - ADAM_B2 ** ADAM_STEP)
        d_ref[...] = -ADAM_LR * (m_hat / (jnp.sqrt(v_hat) + ADAM_EPS) + ADAM_WD * w_ref[...])
        nm_ref[...] = nm
        nv_ref[...] = nv

    blk = pl.BlockSpec((tr, C), lambda i: (i, 0))
    return pl.pallas_call(
        kern, name="adamw", grid=(R // tr,), in_specs=[blk] * 4, out_specs=[blk] * 3,
        out_shape=[jax.ShapeDtypeStruct((R, C), F32)] * 3, compiler_params=_params(1),
    )(w, g, m, v)


def _place():
    x, y, c = lax.axis_index("x"), lax.axis_index("y"), lax.axis_index("c")
    return x, y, c, [(1 - x, y), (x, 1 - y), (1 - x, 1 - y)]


def _comm_call(name, body, operands, out_shape, n_sems, aliases=None):
    any_spec = pl.BlockSpec(memory_space=pl.ANY)
    return pl.pallas_call(
        body, name=name, in_specs=[any_spec] * len(operands), out_specs=[any_spec] * len(out_shape),
        out_shape=out_shape, input_output_aliases=aliases or {},
        scratch_shapes=[pltpu.SemaphoreType.DMA((n,)) for n in n_sems],
    )(*operands)


def _place_shard(w, b, dtype):
    L, R, C = w.shape
    tr = _tile(R, 512)

    def kern(b_ref, w_ref, o_ref):
        o_ref[...] = w_ref[...].astype(dtype)

    return pl.pallas_call(
        kern, name="place_shard",
        grid_spec=pltpu.PrefetchScalarGridSpec(
            num_scalar_prefetch=1, grid=(L, R // tr),
            in_specs=[pl.BlockSpec((None, tr, C), lambda l, i, b_ref: (l, i, 0))],
            out_specs=pl.BlockSpec((None, None, tr, C), lambda l, i, b_ref: (l, b_ref[0], i, 0))),
        out_shape=jax.ShapeDtypeStruct((L, N_CHIPS, R, C), dtype), compiler_params=_params(2),
    )(b, w)


def _all_gather(bufs):
    n, hl = len(bufs), bufs[0].shape[0] // 2

    def body(*refs):
        outs = refs[n:2 * n]
        send_sems, recv_sems = refs[2 * n:]
        x, y, c, chips = _place()
        sibling = (x, y, 1 - c)

        def remote(k, j, chip, half, to):
            region = outs[k].at[pl.ds(half * hl, hl), 2 * chip[0] + chip[1]]
            return pltpu.make_async_remote_copy(
                src_ref=region, dst_ref=region, send_sem=send_sems.at[6 * k + j],
                recv_sem=recv_sems.at[6 * k + j], device_id=to, device_id_type=MESH)

        first = [remote(k, j, (x, y), c, (*chip, c)) for k in range(n) for j, chip in enumerate(chips)]
        for cp in first:
            cp.start()
        passed = []
        for k in range(n):
            for j, chip in enumerate(chips):
                remote(k, j, chip, c, (x, y, c)).wait_recv()
                passed.append(remote(k, 3 + j, chip, c, sibling))
                passed[-1].start()
        for k in range(n):
            for j, chip in enumerate(chips):
                remote(k, 3 + j, chip, 1 - c, (x, y, c)).wait_recv()
        for cp in first + passed:
            cp.wait_send()

    out_shape = [jax.ShapeDtypeStruct(s.shape, s.dtype) for s in bufs]
    return _comm_call("all_gather_weights", body, bufs, out_shape, (6 * n, 6 * n), aliases={k: k for k in range(n)})


def _rs_swap_halves(grads):
    n, hl = len(grads), grads[0].shape[0] // 2

    def body(*refs):
        ins, outs = refs[:n], refs[n:2 * n]
        send_sems, recv_sems = refs[2 * n:]
        x, y, c, _ = _place()
        copies = [pltpu.make_async_remote_copy(
            src_ref=ins[k].at[pl.ds((1 - c) * hl, hl)], dst_ref=outs[k], send_sem=send_sems.at[k],
            recv_sem=recv_sems.at[k], device_id=(x, y, 1 - c), device_id_type=MESH) for k in range(n)]
        for cp in copies:
            cp.start()
        for cp in copies:
            cp.wait()

    out_shape = [jax.ShapeDtypeStruct((hl,) + g.shape[1:], g.dtype) for g in grads]
    return _comm_call("rs_swap_halves", body, grads, out_shape, (n, n))


def _rs_exchange(parts):
    n = len(parts)

    def body(*refs):
        ins, outs = refs[:n], refs[n:2 * n]
        send_sems, recv_sems = refs[2 * n:]
        x, y, c, chips = _place()
        copies = [pltpu.make_async_remote_copy(
            src_ref=ins[k].at[:, 2 * chip[0] + chip[1]], dst_ref=outs[k].at[j], send_sem=send_sems.at[3 * k + j],
            recv_sem=recv_sems.at[3 * k + j], device_id=(*chip, c), device_id_type=MESH)
            for k in range(n) for j, chip in enumerate(chips)]
        for cp in copies:
            cp.start()
        for cp in copies:
            cp.wait()

    out_shape = [jax.ShapeDtypeStruct((3, p.shape[0]) + p.shape[2:], p.dtype) for p in parts]
    return _comm_call("rs_exchange", body, parts, out_shape, (3 * n, 3 * n))


def _rs_share(bufs):
    n, hl = len(bufs), bufs[0].shape[0] // 2

    def body(*refs):
        outs = refs[n:2 * n]
        send_sems, recv_sems = refs[2 * n:]
        x, y, c, _ = _place()

        def half(k, which):
            region = outs[k].at[pl.ds(which * hl, hl)]
            return pltpu.make_async_remote_copy(
                src_ref=region, dst_ref=region, send_sem=send_sems.at[k], recv_sem=recv_sems.at[k],
                device_id=(x, y, 1 - c), device_id_type=MESH)

        sends = [half(k, c) for k in range(n)]
        for cp in sends:
            cp.start()
        for k in range(n):
            half(k, 1 - c).wait_recv()
        for cp in sends:
            cp.wait_send()

    out_shape = [jax.ShapeDtypeStruct(h.shape, h.dtype) for h in bufs]
    return _comm_call("rs_share", body, bufs, out_shape, (n, n), aliases={k: k for k in range(n)})


def _all_reduce_small(v):
    n = v.shape[0]

    def body(v_ref, o_ref, gbuf, send_sems, recv_sems):
        x, y, c, _ = _place()
        me = 4 * x + 2 * y + c
        gbuf[me] = v_ref[...]
        copies, waits = [], []
        for r in range(1, N_DEV):
            px = 1 - x if r & 4 else x
            py = 1 - y if r & 2 else y
            pc = 1 - c if r & 1 else c
            mk = functools.partial(pltpu.make_async_remote_copy, src_ref=v_ref, send_sem=send_sems.at[r - 1],
                                   recv_sem=recv_sems.at[r - 1], device_id=(px, py, pc), device_id_type=MESH)
            copies.append(mk(dst_ref=gbuf.at[me]))
            waits.append(mk(dst_ref=gbuf.at[4 * px + 2 * py + pc]))
        for cp in copies:
            cp.start()
        for cp in waits:
            cp.wait_recv()
        for cp in copies:
            cp.wait_send()
        acc = gbuf[0]
        for d in range(1, N_DEV):
            acc = acc + gbuf[d]
        o_ref[...] = acc

    vm = pl.BlockSpec(memory_space=pltpu.VMEM)
    return pl.pallas_call(
        body, name="all_reduce_small", in_specs=[vm], out_specs=vm, out_shape=jax.ShapeDtypeStruct(v.shape, F32),
        scratch_shapes=[pltpu.VMEM((N_DEV, n, LANES), F32), pltpu.SemaphoreType.DMA((N_DEV - 1,)),
                        pltpu.SemaphoreType.DMA((N_DEV - 1,))],
        compiler_params=pltpu.CompilerParams(vmem_limit_bytes=VMEM_LIMIT),
    )(v)


def _add_half(g, h1, c):
    hl, nb, R, C = h1.shape
    g3, h3 = g.reshape(2 * hl, nb * R, C), h1.reshape(hl, nb * R, C)
    tr = _tile(nb * R, 512)

    def kern(c_ref, g_ref, h_ref, o_ref):
        o_ref[...] = (g_ref[...].astype(F32) + h_ref[...].astype(F32)).astype(BF16)

    blk = pl.BlockSpec((None, tr, C), lambda l, i, c_ref: (l, i, 0))
    out = pl.pallas_call(
        kern, name="rs_add_half",
        grid_spec=pltpu.PrefetchScalarGridSpec(
            num_scalar_prefetch=1, grid=(hl, nb * R // tr),
            in_specs=[pl.BlockSpec((None, tr, C), lambda l, i, c_ref: (c_ref[0] * hl + l, i, 0)), blk],
            out_specs=blk),
        out_shape=jax.ShapeDtypeStruct(h3.shape, BF16), compiler_params=_params(2),
    )(c, g3, h3)
    return out.reshape(h1.shape)


def _add_blocks(p, h2, bc):
    hl, nb, R, C = p.shape
    tr = _tile(R, 512)

    def kern(bc_ref, p_ref, h0_ref, h1_ref, h2_ref, o_ref):
        o_ref[...] = ((p_ref[...].astype(F32) + h0_ref[...].astype(F32)) + h1_ref[...].astype(F32)) + h2_ref[...].astype(F32)

    def other(j):
        return pl.BlockSpec((None, None, tr, C), lambda l, i, bc_ref: (j, l, i, 0))

    return pl.pallas_call(
        kern, name="rs_add_blocks",
        grid_spec=pltpu.PrefetchScalarGridSpec(
            num_scalar_prefetch=1, grid=(hl, R // tr),
            in_specs=[pl.BlockSpec((None, None, tr, C), lambda l, i, bc_ref: (l, bc_ref[0], i, 0)),
                      other(0), other(1), other(2)],
            out_specs=pl.BlockSpec((None, tr, C), lambda l, i, bc_ref: (bc_ref[1] * hl + l, i, 0))),
        out_shape=jax.ShapeDtypeStruct((2 * hl, R, C), F32), compiler_params=_params(2),
    )(bc, p, h2, h2, h2)


def _reduce_scatter(grads):
    x, y, c = lax.axis_index("x"), lax.axis_index("y"), lax.axis_index("c")
    cs = jnp.reshape(c, (1,)).astype(jnp.int32)
    bc = jnp.stack([2 * x + y, c]).astype(jnp.int32)
    sib = _rs_swap_halves(grads)
    parts = [_add_half(g, h, cs) for g, h in zip(grads, sib)]
    others = _rs_exchange(parts)
    return _rs_share([_add_blocks(p, o, bc) for p, o in zip(parts, others)])


WEIGHTS = ("norm_ffn1", "w_ffn1_in", "w_ffn1_out", "norm_mix", "w_mix_in", "b_forget", "w_pool", "pool_scale",
           "conv_w", "w_mix_out", "norm_ffn2", "w_ffn2_in", "w_ffn2_out", "norm_final")
BIG = ("w_ffn1_in", "w_ffn1_out", "w_mix_in", "w_mix_out", "w_ffn2_in", "w_ffn2_out")
SMALL = ("norm_ffn1", "norm_mix", "b_forget", "w_pool", "pool_scale", "conv_w", "norm_ffn2", "norm_final")


def _prep_weights(small, gathered, D):
    DA, C, H = D // 2, D // 4, D // 2 // HEAD_DIM
    L = small["norm_ffn1"].shape[0]
    w_in = jnp.concatenate([gathered["w_mix_in"][:, b] for b in range(N_CHIPS)], axis=2)
    wqkv, wrest = w_in[:, :, :3 * DA], w_in[:, :, 3 * DA + H:]
    wf = jnp.pad(w_in[:, :, 3 * DA:3 * DA + H], ((0, 0), (0, 0), (0, LANES - H)))
    gw = C // len(POOL_WINDOWS)
    wbd = jnp.zeros((L, C, C), F32)
    for gi in range(len(POOL_WINDOWS)):
        wbd = wbd.at[:, gi * gw:(gi + 1) * gw, gi * gw:(gi + 1) * gw].set(small["w_pool"][:, gi])
    cw = jnp.concatenate([gathered["conv_w"][:, b] for b in range(N_CHIPS)], axis=2)
    return dict(
        g1=small["norm_ffn1"], gm=small["norm_mix"], g2=small["norm_ffn2"],
        w1in=gathered["w_ffn1_in"], w2in=gathered["w_ffn2_in"],
        w1out=gathered["w_ffn1_out"].reshape(L, -1, D), w2out=gathered["w_ffn2_out"].reshape(L, -1, D),
        wqkv=wqkv, wrest=wrest, wf=wf, wp=jnp.concatenate([wqkv, wrest, wf], axis=2),
        wmixout=gathered["w_mix_out"].reshape(L, D, D),
        bias=jnp.pad(small["b_forget"], ((0, 0), (0, LANES - H))), wbd=wbd.astype(BF16), ps=small["pool_scale"],
        cw=jnp.pad(cw, ((0, 0), (0, 8 - CONV_WIDTH), (0, 0))),
    )


def _layer_params(l, W):
    P = {k: (W[k], l) for k in ("w1in", "w2in", "w1out", "w2out", "wqkv", "wrest", "wf", "wp", "wmixout")}
    P.update({k: W[k][l][None] for k in ("g1", "gm", "g2", "bias", "ps")})
    P.update(wbd=W["wbd"][l], cw=W["cw"][l])
    return P


def _ffn_fwd(x, g, w_in, w_out):
    h = _rmsnorm_fwd(x, g)
    gu, act = _ffn_in(h, w_in)
    return _ffn_out(act, w_out, x)[0], (x, h, gu, act)


def _ffn_bwd(dres, saved, g, w_in, w_out):
    x, h, gu, act = saved
    dgu = _ffn_bwd_act(dres, w_out, gu)
    dw_out = _ffn_dw_out(act, dres)[0]
    dw_in = _ffn_dw_in(h, dgu)[0]
    dh = _ffn_dh(dgu, w_in)[0]
    dx, dg = _rmsnorm_bwd(x, g, dh, dres)
    return dx, dg, dw_in, dw_out.reshape(N_CHIPS, -1, dw_out.shape[1])


def _mixer_fwd(x, P, B, S, tq):
    T, D = x.shape
    DA, C, H = D // 2, D // 4, D // 2 // HEAD_DIM
    hn = _rmsnorm_fwd(x, P["gm"])
    qkv = _proj("mix_qkv", hn, P["wqkv"], BF16).reshape(B, S, 3 * DA)
    rest = _proj("mix_rest", hn, P["wrest"], F32).reshape(B, S, 4 * C)
    fl = _proj("mix_f", hn, P["wf"], F32).reshape(B, S, LANES)
    drow = _decay_fwd(fl, P["bias"]).reshape(B, 8, S // tq, tq)
    o, lse = _attn_fwd(qkv, drow, H, tq)
    ypc, pooled = _mix_local_fwd(rest, P["wbd"], P["ps"], P["cw"])
    ycat = jnp.concatenate([o, ypc], axis=-1).reshape(T, D)
    return _proj("mix_out", ycat, P["wmixout"], F32, extra=x), (x, hn, qkv, rest, fl, drow, o, lse, pooled, ycat)


def _mixer_bwd(dres, saved, P, B, S, tq):
    x, hn, qkv, rest, fl, drow, o, lse, pooled, ycat = saved
    T, D = x.shape
    DA, C, H = D // 2, D // 4, D // 2 // HEAD_DIM
    dycat = _proj("mix_out_bwd", dres, P["wmixout"], F32, NT).reshape(B, S, D)
    dw_out = _dw("mix_out_dw", ycat, dres, BF16)
    dq, dk, dv, ddrow, ddcol = _attn_bwd(qkv, drow, o, lse, dycat, H, tq)
    dfl, dbias = _decay_bwd(ddrow.reshape(B, 8, S), ddcol, fl, P["bias"], H)
    drest, dwbd, dps, dcw = _mix_local_bwd(rest, pooled, dycat, P["wbd"], P["ps"], P["cw"])
    dproj = jnp.concatenate([dq, dk, dv, drest, dfl.astype(BF16)], axis=-1).reshape(T, 3 * DA + 4 * C + LANES)
    dwp = _dw("mix_in_dw", hn, dproj, F32)
    dhn = _proj("mix_in_bwd", dproj, P["wp"], F32, NT)
    dx, dg = _rmsnorm_bwd(x, P["gm"], dhn, dres)
    n_q, n_r = 3 * DA, 4 * C
    dw_in = jnp.concatenate([dwp[:, :n_q], dwp[:, n_q + n_r:n_q + n_r + H], dwp[:, n_q:n_q + n_r]], axis=1)
    dw_in = dw_in.reshape(D, N_CHIPS, -1).transpose(1, 0, 2).astype(BF16)
    gw = C // len(POOL_WINDOWS)
    dw_pool = jnp.stack([dwbd[gi * gw:(gi + 1) * gw, gi * gw:(gi + 1) * gw] for gi in range(len(POOL_WINDOWS))])
    small = dict(norm_mix=dg[0], b_forget=dbias[0, :H], w_pool=dw_pool, pool_scale=dps[0], conv_w=dcw[:CONV_WIDTH])
    return dx, small, dw_in, dw_out.reshape(N_CHIPS, -1, D)


def _local_step(x, target, small, gathered):
    B, S, D = x.shape
    L = small["norm_ffn1"].shape[0]
    tq = _tile(S, 256)
    xt = x.reshape(B * S, D)
    saved, params = [], []
    W = _prep_weights(small, gathered, D)
    for l in range(L):
        P = _layer_params(l, W)
        xt, s1 = _ffn_fwd(xt, P["g1"], P["w1in"], P["w1out"])
        xt, s2 = _mixer_fwd(xt, P, B, S, tq)
        xt, s3 = _ffn_fwd(xt, P["g2"], P["w2in"], P["w2out"])
        saved.append((s1, s2, s3))
        params.append(P)
    dres, dgf, loss = _final_loss(xt, small["norm_final"][None], target.reshape(B * S, D))
    big = {k: [None] * L for k in BIG}
    sm = {k: [None] * L for k in SMALL if k != "norm_final"}
    for l in reversed(range(L)):
        P, (s1, s2, s3) = params[l], saved[l]
        dres, dg2, big["w_ffn2_in"][l], big["w_ffn2_out"][l] = _ffn_bwd(dres, s3, P["g2"], P["w2in"], P["w2out"])
        dres, smix, big["w_mix_in"][l], big["w_mix_out"][l] = _mixer_bwd(dres, s2, P, B, S, tq)
        dres, dg1, big["w_ffn1_in"][l], big["w_ffn1_out"][l] = _ffn_bwd(dres, s1, P["g1"], P["w1in"], P["w1out"])
        sm["norm_ffn1"][l], sm["norm_ffn2"][l] = dg1[0], dg2[0]
        for k, val in smix.items():
            sm[k][l] = val
    big = {k: jnp.stack(val) for k, val in big.items()}
    sm = {k: jnp.stack(val) for k, val in sm.items()}
    sm["norm_final"] = dgf[0]
    return loss[0, 0], dres.reshape(B, S, D), big, sm


def _pack(parts, extra=()):
    flat = jnp.concatenate([p.reshape(-1) for p in parts] + [jnp.reshape(e, (1,)) for e in extra])
    n = -(-flat.shape[0] // (8 * LANES)) * 8
    return jnp.pad(flat, (0, n * LANES - flat.shape[0])).reshape(n, LANES)


def _unpack(buf, shapes):
    flat, out, at = buf.reshape(-1), [], 0
    for s in shapes:
        n = math.prod(s)
        out.append(flat[at:at + n].reshape(s))
        at += n
    return out, flat[at:]


def kernel(x, norm_ffn1, w_ffn1_in, w_ffn1_out, norm_mix, w_mix_in, b_forget, w_pool, pool_scale, conv_w, w_mix_out, norm_ffn2, w_ffn2_in, w_ffn2_out, norm_final, loss_target, m_norm_ffn1, m_w_ffn1_in, m_w_ffn1_out, m_norm_mix, m_w_mix_in, m_b_forget, m_w_pool, m_pool_scale, m_conv_w, m_w_mix_out, m_norm_ffn2, m_w_ffn2_in, m_w_ffn2_out, m_norm_final, v_norm_ffn1, v_w_ffn1_in, v_w_ffn1_out, v_norm_mix, v_w_mix_in, v_b_forget, v_w_pool, v_pool_scale, v_conv_w, v_w_mix_out, v_norm_ffn2, v_w_ffn2_in, v_w_ffn2_out, v_norm_final):
    w = dict(zip(WEIGHTS, (norm_ffn1, w_ffn1_in, w_ffn1_out, norm_mix, w_mix_in, b_forget, w_pool, pool_scale, conv_w, w_mix_out, norm_ffn2, w_ffn2_in, w_ffn2_out, norm_final)))
    m = dict(zip(WEIGHTS, (m_norm_ffn1, m_w_ffn1_in, m_w_ffn1_out, m_norm_mix, m_w_mix_in, m_b_forget, m_w_pool, m_pool_scale, m_conv_w, m_w_mix_out, m_norm_ffn2, m_w_ffn2_in, m_w_ffn2_out, m_norm_final)))
    v = dict(zip(WEIGHTS, (v_norm_ffn1, v_w_ffn1_in, v_w_ffn1_out, v_norm_mix, v_w_mix_in, v_b_forget, v_w_pool, v_pool_scale, v_conv_w, v_w_mix_out, v_norm_ffn2, v_w_ffn2_in, v_w_ffn2_out, v_norm_final)))
    block = 2 * lax.axis_index("x") + lax.axis_index("y")

    bs = jnp.reshape(block, (1,)).astype(jnp.int32)
    gathered = _all_gather([_place_shard(w[k], bs, BF16) for k in BIG] + [_place_shard(w["conv_w"], bs, F32)])
    gathered = dict(zip(BIG + ("conv_w",), gathered))
    small = {k: w[k] for k in SMALL}
    loss, grad_x, big, sm = _local_step(x, loss_target, small, gathered)

    grads = dict(zip(BIG, _reduce_scatter([big[k] for k in BIG])))
    order = [k for k in SMALL]
    total = _all_reduce_small(_pack([sm[k] for k in order], extra=(loss,)))
    parts, rest = _unpack(total, [sm[k].shape for k in order])
    grads.update(zip(order, parts))
    loss = rest[0]
    cs = conv_w.shape[2]
    grads["conv_w"] = lax.dynamic_slice_in_dim(grads["conv_w"], block * cs, cs, axis=2)

    delta, new_m, new_v = {}, {}, {}
    for k in BIG:
        two_d = lambda a: a.reshape(-1, a.shape[-1])
        d, nm, nv = _adamw(two_d(w[k]), two_d(grads[k]), two_d(m[k]), two_d(v[k]))
        delta[k], new_m[k], new_v[k] = d.reshape(w[k].shape), nm.reshape(w[k].shape), nv.reshape(w[k].shape)
    d, nm, nv = _adamw(*[_pack([t[k] for k in order]) for t in (w, grads, m, v)])
    shapes = [w[k].shape for k in order]
    for res, packed in ((delta, d), (new_m, nm), (new_v, nv)):
        res.update(zip(order, _unpack(packed, shapes)[0]))
    return (loss, grad_x, *[grads[k] for k in WEIGHTS], *[delta[k] for k in WEIGHTS],
            *[new_m[k] for k in WEIGHTS], *[new_v[k] for k in WEIGHTS])
```

```python
import functools
import math

import jax
import jax.numpy as jnp
from jax import lax
from jax.experimental import pallas as pl
from jax.experimental.pallas import tpu as pltpu

F32 = jnp.float32
BF16 = jnp.bfloat16
MESH = pl.DeviceIdType.MESH

HEAD_DIM = 64
POOL_WINDOWS = (2, 4, 8, 16)
CONV_WIDTH = 3
RMS_EPS = 1e-6
ADAM_LR = 0.001
ADAM_B1 = 0.9
ADAM_B2 = 0.999
ADAM_EPS = 1e-08
ADAM_WD = 0.01
ADAM_STEP = 10

LANES = 128
VMEM_LIMIT = 56 * 1024 * 1024
N_CHIPS = 4
N_DEV = 8

NN = (((1,), (0,)), ((), ()))
NT = (((1,), (1,)), ((), ()))
TN = (((0,), (0,)), ((), ()))


def _tile(n, pref):
    t = pref
    while t >= 8:
        if n % t == 0:
            return t
        t //= 2
    return n


def _params(n_grid):
    return pltpu.CompilerParams(dimension_semantics=("arbitrary",) * n_grid, vmem_limit_bytes=VMEM_LIMIT)


def _dot(a, b, dims):
    return lax.dot_general(a, b, dims, preferred_element_type=F32)


def _mm(name, dims, operands, in_specs, out_shape, out_specs, grid, acc_shape, epilogue):
    n_in, n_out, nk = len(operands), len(out_shape), grid[-1]

    def kern(*refs):
        extras, outs = refs[2:n_in], refs[n_in:n_in + n_out]
        if nk == 1:
            epilogue(_dot(refs[0][...].astype(BF16), refs[1][...].astype(BF16), dims), extras, outs)
            return
        acc = refs[n_in + n_out]
        k = pl.program_id(len(grid) - 1)

        @pl.when(k == 0)
        def _():
            acc[...] = jnp.zeros_like(acc)

        acc[...] += _dot(refs[0][...].astype(BF16), refs[1][...].astype(BF16), dims)

        @pl.when(k == nk - 1)
        def _():
            epilogue(acc[...], extras, outs)

    return pl.pallas_call(
        kern, name=name, grid=grid, in_specs=in_specs, out_specs=out_specs, out_shape=out_shape,
        scratch_shapes=[pltpu.VMEM(acc_shape, F32)] if nk > 1 else [],
        compiler_params=_params(len(grid)),
    )(*operands)


def _store(scale=None, dtype=None):
    def ep(acc, extras, outs):
        v = acc if scale is None else acc * scale
        outs[0][...] = v.astype(outs[0].dtype)
    return ep


def _residual(scale):
    def ep(acc, extras, outs):
        outs[0][...] = extras[0][...] + scale * acc
    return ep


def _rmsnorm_fwd(x, g):
    T, D = x.shape
    tr = _tile(T, 512)

    def kern(x_ref, g_ref, o_ref):
        xv = x_ref[...]
        r = lax.rsqrt(jnp.mean(xv * xv, axis=-1, keepdims=True) + RMS_EPS)
        o_ref[...] = (xv * r * g_ref[...]).astype(BF16)

    return pl.pallas_call(
        kern, name="rmsnorm_fwd", grid=(T // tr,),
        in_specs=[pl.BlockSpec((tr, D), lambda i: (i, 0)), pl.BlockSpec((1, D), lambda i: (0, 0))],
        out_specs=pl.BlockSpec((tr, D), lambda i: (i, 0)),
        out_shape=jax.ShapeDtypeStruct((T, D), BF16), compiler_params=_params(1),
    )(x, g)


def _rmsnorm_bwd(x, g, dh, dres):
    T, D = x.shape
    tr = _tile(T, 256)

    def kern(x_ref, g_ref, dh_ref, dres_ref, dx_ref, dg_ref):
        xv, dhv = x_ref[...], dh_ref[...]
        r = lax.rsqrt(jnp.mean(xv * xv, axis=-1, keepdims=True) + RMS_EPS)
        y = xv * r
        dy = dhv * g_ref[...]
        dx_ref[...] = dres_ref[...] + r * (dy - y * jnp.mean(dy * y, axis=-1, keepdims=True))
        part = jnp.sum(dhv * y, axis=0, keepdims=True)

        @pl.when(pl.program_id(0) == 0)
        def _():
            dg_ref[...] = part

        @pl.when(pl.program_id(0) > 0)
        def _():
            dg_ref[...] += part

    row = pl.BlockSpec((tr, D), lambda i: (i, 0))
    vec = pl.BlockSpec((1, D), lambda i: (0, 0))
    return pl.pallas_call(
        kern, name="rmsnorm_bwd", grid=(T // tr,), in_specs=[row, vec, row, row], out_specs=[row, vec],
        out_shape=[jax.ShapeDtypeStruct((T, D), F32), jax.ShapeDtypeStruct((1, D), F32)],
        compiler_params=_params(1),
    )(x, g, dh, dres)


def _final_loss(x, g, target):
    T, D = x.shape
    tr = _tile(T, 256)

    def kern(x_ref, g_ref, t_ref, dx_ref, dg_ref, loss_ref):
        xv = x_ref[...]
        r = lax.rsqrt(jnp.mean(xv * xv, axis=-1, keepdims=True) + RMS_EPS)
        y = xv * r
        err = y * g_ref[...] - t_ref[...]
        lpart = 0.5 * jnp.sum(jnp.mean(err * err, axis=-1, keepdims=True), axis=0, keepdims=True)
        dh = err * (1.0 / D)
        dy = dh * g_ref[...]
        dx_ref[...] = r * (dy - y * jnp.mean(dy * y, axis=-1, keepdims=True))
        part = jnp.sum(dh * y, axis=0, keepdims=True)
        lrow = jnp.broadcast_to(lpart, (1, LANES))

        @pl.when(pl.program_id(0) == 0)
        def _():
            dg_ref[...] = part
            loss_ref[...] = lrow

        @pl.when(pl.program_id(0) > 0)
        def _():
            dg_ref[...] += part
            loss_ref[...] += lrow

    row = pl.BlockSpec((tr, D), lambda i: (i, 0))
    vec = pl.BlockSpec((1, D), lambda i: (0, 0))
    return pl.pallas_call(
        kern, name="final_loss", grid=(T // tr,), in_specs=[row, vec, row],
        out_specs=[row, vec, pl.BlockSpec((1, LANES), lambda i: (0, 0))],
        out_shape=[jax.ShapeDtypeStruct((T, D), F32), jax.ShapeDtypeStruct((1, D), F32),
                   jax.ShapeDtypeStruct((1, LANES), F32)],
        compiler_params=_params(1),
    )(x, g, target)


def _ffn_in(h, w4):
    T, D = h.shape
    w4, l = w4
    Fh = w4.shape[3]
    F = 2 * Fh
    tm = _tile(T, 512)

    def kern(h_ref, wg_ref, wu_ref, jac_ref, act_ref):
        hv = h_ref[...]
        gate = _dot(hv, wg_ref[...], NN)
        up = _dot(hv, wu_ref[...], NN)
        sg = jax.nn.sigmoid(gate)
        silu = gate * sg
        jac_ref[0] = (up * (sg + silu * (1.0 - sg))).astype(BF16)
        jac_ref[1] = silu.astype(BF16)
        act_ref[...] = (silu * up).astype(BF16)

    return pl.pallas_call(
        kern, name="ffn_in", grid=(2, T // tm),
        in_specs=[pl.BlockSpec((tm, D), lambda j, i: (i, 0)),
                  pl.BlockSpec((None, None, D, Fh), lambda j, i: (l, j, 0, 0)),
                  pl.BlockSpec((None, None, D, Fh), lambda j, i: (l, 2 + j, 0, 0))],
        out_specs=[pl.BlockSpec((2, tm, Fh), lambda j, i: (0, i, j)),
                   pl.BlockSpec((tm, Fh), lambda j, i: (i, j))],
        out_shape=[jax.ShapeDtypeStruct((2, T, F), BF16), jax.ShapeDtypeStruct((T, F), BF16)],
        compiler_params=_params(2),
    )(h, w4, w4)


def _resident(shape, index_map):
    return pl.BlockSpec(shape, index_map, pipeline_mode=pl.Buffered(1))


def _ffn_up(x, g, w4):
    T, D = x.shape
    w4, l = w4
    Fh = w4.shape[3]
    F = 2 * Fh
    tm = _tile(T, 512)

    def kern(x_ref, g_ref, w_ref, h_ref, jac_ref, act_ref):
        xv = x_ref[...]
        r = lax.rsqrt(jnp.mean(xv * xv, axis=-1, keepdims=True) + RMS_EPS)
        hv = (xv * r * g_ref[...]).astype(BF16)
        h_ref[...] = hv
        for j in range(2):
            cols = slice(j * Fh, (j + 1) * Fh)
            gate = _dot(hv, w_ref[j], NN)
            up = _dot(hv, w_ref[2 + j], NN)
            sg = jax.nn.sigmoid(gate)
            silu = gate * sg
            jac_ref[0, :, cols] = (up * (sg + silu * (1.0 - sg))).astype(BF16)
            jac_ref[1, :, cols] = silu.astype(BF16)
            act_ref[:, cols] = (silu * up).astype(BF16)

    return pl.pallas_call(
        kern, name="ffn_up", grid=(T // tm,),
        in_specs=[pl.BlockSpec((tm, D), lambda i: (i, 0)), pl.BlockSpec((1, D), lambda i: (0, 0)),
                  _resident((None, 4, D, Fh), lambda i: (l, 0, 0, 0))],
        out_specs=[pl.BlockSpec((tm, D), lambda i: (i, 0)), pl.BlockSpec((2, tm, F), lambda i: (0, i, 0)),
                   pl.BlockSpec((tm, F), lambda i: (i, 0))],
        out_shape=[jax.ShapeDtypeStruct((T, D), BF16), jax.ShapeDtypeStruct((2, T, F), BF16),
                   jax.ShapeDtypeStruct((T, F), BF16)],
        compiler_params=_params(1),
    )(x, g, w4)


def _ffn_bwd_main(dres, jac, x, g, w_out, w4):
    T, D = dres.shape
    w_out, l = w_out
    w4, _ = w4
    F = w_out.shape[1]
    Fh = F // 2
    tm = _tile(T, 256)

    def kern(d_ref, jac_ref, x_ref, g_ref, wo_ref, wi_ref, dgu_ref, dx_ref, dg_ref):
        dv = d_ref[...]
        d16 = dv.astype(BF16)
        dh = jnp.zeros((tm, D), F32)
        for j in range(2):
            cols = slice(j * Fh, (j + 1) * Fh)
            dact = 0.5 * _dot(d16, wo_ref[cols, :], NT)
            dgate = (dact * jac_ref[0, :, cols].astype(F32)).astype(BF16)
            dup = (dact * jac_ref[1, :, cols].astype(F32)).astype(BF16)
            dgu_ref[0, :, cols] = dgate
            dgu_ref[1, :, cols] = dup
            dh = dh + _dot(dgate, wi_ref[j], NT) + _dot(dup, wi_ref[2 + j], NT)
        xv = x_ref[...]
        r = lax.rsqrt(jnp.mean(xv * xv, axis=-1, keepdims=True) + RMS_EPS)
        y = xv * r
        dy = dh * g_ref[...]
        dx_ref[...] = dv + r * (dy - y * jnp.mean(dy * y, axis=-1, keepdims=True))
        part = jnp.sum(dh * y, axis=0, keepdims=True)

        @pl.when(pl.program_id(0) == 0)
        def _():
            dg_ref[...] = part

        @pl.when(pl.program_id(0) > 0)
        def _():
            dg_ref[...] += part

    row = pl.BlockSpec((tm, D), lambda i: (i, 0))
    vec = pl.BlockSpec((1, D), lambda i: (0, 0))
    wide = pl.BlockSpec((2, tm, F), lambda i: (0, i, 0))
    return pl.pallas_call(
        kern, name="ffn_bwd_main", grid=(T // tm,),
        in_specs=[row, wide, row, vec, _resident((None, F, D), lambda i: (l, 0, 0)),
                  _resident((None, 4, D, Fh), lambda i: (l, 0, 0, 0))],
        out_specs=[wide, row, vec],
        out_shape=[jax.ShapeDtypeStruct((2, T, F), BF16), jax.ShapeDtypeStruct((T, D), F32),
                   jax.ShapeDtypeStruct((1, D), F32)],
        compiler_params=_params(1),
    )(dres, jac, x, g, w_out, w4)


def _ffn_out(act, w_out, x):
    T, F = act.shape
    w_out, l = w_out
    D = w_out.shape[2]
    tm = _tile(T, 512)
    return _mm("ffn_out", NN, [act, w_out, x],
               [pl.BlockSpec((tm, F), lambda i, k: (i, 0)), pl.BlockSpec((None, F, D), lambda i, k: (l, 0, 0)),
                pl.BlockSpec((tm, D), lambda i, k: (i, 0))],
               [jax.ShapeDtypeStruct((T, D), F32)], [pl.BlockSpec((tm, D), lambda i, k: (i, 0))],
               (T // tm, 1), None, _residual(0.5))


def _ffn_bwd_act(dres, w_out, jac):
    T, D = dres.shape
    w_out, l = w_out
    F = w_out.shape[1]
    Fh = F // 2
    tm = _tile(T, 512)

    def kern(d_ref, w_ref, jac_ref, o_ref):
        dact = 0.5 * _dot(d_ref[...].astype(BF16), w_ref[...], NT)
        o_ref[0] = (dact * jac_ref[0].astype(F32)).astype(BF16)
        o_ref[1] = (dact * jac_ref[1].astype(F32)).astype(BF16)

    return pl.pallas_call(
        kern, name="ffn_bwd_act", grid=(2, T // tm),
        in_specs=[pl.BlockSpec((tm, D), lambda j, i: (i, 0)), pl.BlockSpec((None, Fh, D), lambda j, i: (l, j, 0)),
                  pl.BlockSpec((2, tm, Fh), lambda j, i: (0, i, j))],
        out_specs=pl.BlockSpec((2, tm, Fh), lambda j, i: (0, i, j)),
        out_shape=jax.ShapeDtypeStruct((2, T, F), BF16), compiler_params=_params(2),
    )(dres, w_out, jac)


def _ffn_dw_out(act, dres):
    T, F = act.shape
    D = dres.shape[1]
    tm, tk = F // 2, _tile(T, 1024)
    return _mm("ffn_dw_out", TN, [act, dres],
               [pl.BlockSpec((tk, tm), lambda i, k: (k, i)), pl.BlockSpec((tk, D), lambda i, k: (k, 0))],
               [jax.ShapeDtypeStruct((F, D), BF16)], [pl.BlockSpec((tm, D), lambda i, k: (i, 0))],
               (2, T // tk), (tm, D), _store(0.5))


def _ffn_dw_in(h, dgu):
    T, D = h.shape
    Fh = dgu.shape[2] // 2
    tk = _tile(T, 1024)
    return _mm("ffn_dw_in", TN, [h, dgu],
               [pl.BlockSpec((tk, D), lambda j, k: (k, 0)),
                pl.BlockSpec((None, tk, Fh), lambda j, k: (j // 2, k, j % 2))],
               [jax.ShapeDtypeStruct((4, D, Fh), BF16)], [pl.BlockSpec((None, D, Fh), lambda j, k: (j, 0, 0))],
               (4, T // tk), (D, Fh), _store())


def _ffn_dh(dgu, w4):
    T = dgu.shape[1]
    w4, l = w4
    D, Fh = w4.shape[2], w4.shape[3]
    tm = _tile(T, 1024)
    return _mm("ffn_dh", NT, [dgu, w4],
               [pl.BlockSpec((None, tm, Fh), lambda i, k: (k // 2, i, k % 2)),
                pl.BlockSpec((None, None, D, Fh), lambda i, k: (l, k, 0, 0))],
               [jax.ShapeDtypeStruct((T, D), F32)], [pl.BlockSpec((tm, D), lambda i, k: (i, 0))],
               (T // tm, 4), (tm, D), _store())


def _proj(name, a, w, out_dtype, dims=NN, extra=None, scale=None):
    T, K = a.shape
    w, l = w
    N = w.shape[2] if dims == NN else w.shape[1]
    tm = _tile(T, 512)
    ops = [a, w] + ([extra] if extra is not None else [])
    specs = [pl.BlockSpec((tm, K), lambda i, k: (i, 0)), pl.BlockSpec((None,) + w.shape[1:], lambda i, k: (l, 0, 0))]
    if extra is not None:
        specs.append(pl.BlockSpec((tm, N), lambda i, k: (i, 0)))
    ep = _residual(1.0) if extra is not None else _store(scale)
    return _mm(name, dims, ops, specs, [jax.ShapeDtypeStruct((T, N), out_dtype)],
               [pl.BlockSpec((tm, N), lambda i, k: (i, 0))], (T // tm, 1), None, ep)[0]


def _mix_up(x, g, wp, widths):
    T, D = x.shape
    wp, l = wp
    n_qkv, n_rest = widths
    NP = wp.shape[2]
    tm = _tile(T, 512)

    def kern(x_ref, g_ref, w_ref, h_ref, qkv_ref, rest_ref, fl_ref):
        xv = x_ref[...]
        r = lax.rsqrt(jnp.mean(xv * xv, axis=-1, keepdims=True) + RMS_EPS)
        hv = (xv * r * g_ref[...]).astype(BF16)
        h_ref[...] = hv
        qkv_ref[...] = _dot(hv, w_ref[:, 0:n_qkv], NN).astype(BF16)
        rest_ref[...] = _dot(hv, w_ref[:, n_qkv:n_qkv + n_rest], NN)
        fl_ref[...] = _dot(hv, w_ref[:, n_qkv + n_rest:NP], NN)

    row = lambda n: pl.BlockSpec((tm, n), lambda i: (i, 0))
    return pl.pallas_call(
        kern, name="mix_up", grid=(T // tm,),
        in_specs=[row(D), pl.BlockSpec((1, D), lambda i: (0, 0)), _resident((None, D, NP), lambda i: (l, 0, 0))],
        out_specs=[row(D), row(n_qkv), row(n_rest), row(LANES)],
        out_shape=[jax.ShapeDtypeStruct((T, D), BF16), jax.ShapeDtypeStruct((T, n_qkv), BF16),
                   jax.ShapeDtypeStruct((T, n_rest), F32), jax.ShapeDtypeStruct((T, LANES), F32)],
        compiler_params=_params(1),
    )(x, g, wp)


def _mix_in_bwd(dproj, x, g, dres, wp):
    T, D = x.shape
    wp, l = wp
    NP = wp.shape[2]
    tm = _tile(T, 512)

    def kern(dp_ref, x_ref, g_ref, d_ref, w_ref, dx_ref, dg_ref):
        dh = _dot(dp_ref[...], w_ref[...], NT)
        xv = x_ref[...]
        r = lax.rsqrt(jnp.mean(xv * xv, axis=-1, keepdims=True) + RMS_EPS)
        y = xv * r
        dy = dh * g_ref[...]
        dx_ref[...] = d_ref[...] + r * (dy - y * jnp.mean(dy * y, axis=-1, keepdims=True))
        part = jnp.sum(dh * y, axis=0, keepdims=True)

        @pl.when(pl.program_id(0) == 0)
        def _():
            dg_ref[...] = part

        @pl.when(pl.program_id(0) > 0)
        def _():
            dg_ref[...] += part

    row = lambda n: pl.BlockSpec((tm, n), lambda i: (i, 0))
    vec = pl.BlockSpec((1, D), lambda i: (0, 0))
    return pl.pallas_call(
        kern, name="mix_in_bwd", grid=(T // tm,),
        in_specs=[row(NP), row(D), vec, row(D), _resident((None, D, NP), lambda i: (l, 0, 0))],
        out_specs=[row(D), vec],
        out_shape=[jax.ShapeDtypeStruct((T, D), F32), jax.ShapeDtypeStruct((1, D), F32)],
        compiler_params=_params(1),
    )(dproj, x, g, dres, wp)


def _dw(name, a, d, out_dtype):
    T, M = a.shape
    N = d.shape[1]
    tk = _tile(T, 1024 if M * N <= 1024 * 1408 else 512)
    return _mm(name, TN, [a, d],
               [pl.BlockSpec((tk, M), lambda i, k: (k, 0)), pl.BlockSpec((tk, N), lambda i, k: (k, 0))],
               [jax.ShapeDtypeStruct((M, N), out_dtype)], [pl.BlockSpec((M, N), lambda i, k: (0, 0))],
               (1, T // tk), (M, N), _store())[0]


def _log_sigmoid(z):
    return jnp.minimum(z, 0.0) - jnp.log(1.0 + jnp.exp(-jnp.abs(z)))


def _decay_fwd(fl, bias):
    B, S, _ = fl.shape

    def kern(fl_ref, b_ref, o_ref):
        d = _log_sigmoid(fl_ref[...] + b_ref[...])
        row = lax.broadcasted_iota(jnp.int32, (S, LANES), 0)
        sh = 1
        while sh < S:
            d = d + jnp.where(row >= sh, pltpu.roll(d, sh, 0), 0.0)
            sh *= 2
        o_ref[...] = d.T[0:8, :]

    return pl.pallas_call(
        kern, name="decay_fwd", grid=(B,),
        in_specs=[pl.BlockSpec((None, S, LANES), lambda b: (b, 0, 0)), pl.BlockSpec((1, LANES), lambda b: (0, 0))],
        out_specs=pl.BlockSpec((None, 8, S), lambda b: (b, 0, 0)),
        out_shape=jax.ShapeDtypeStruct((B, 8, S), F32), compiler_params=_params(1),
    )(fl, bias)


def _decay_bwd(ddrow, ddcol, fl, bias, n_heads):
    B, S, _ = fl.shape

    def kern(dd_ref, ddc_ref, fl_ref, b_ref, dfl_ref, db_ref):
        dd = jnp.concatenate([dd_ref[...], jnp.zeros((LANES - 8, S), F32)], axis=0).T + ddc_ref[...]
        row = lax.broadcasted_iota(jnp.int32, (S, LANES), 0)
        lane = lax.broadcasted_iota(jnp.int32, (S, LANES), 1)
        sh = 1
        while sh < S:
            dd = dd + jnp.where(row < S - sh, pltpu.roll(dd, S - sh, 0), 0.0)
            sh *= 2
        z = fl_ref[...] + b_ref[...]
        dfl = jnp.where(lane < n_heads, dd / (1.0 + jnp.exp(z)), 0.0)
        dfl_ref[...] = dfl
        part = jnp.sum(dfl, axis=0, keepdims=True)

        @pl.when(pl.program_id(0) == 0)
        def _():
            db_ref[...] = part

        @pl.when(pl.program_id(0) > 0)
        def _():
            db_ref[...] += part

    return pl.pallas_call(
        kern, name="decay_bwd", grid=(B,),
        in_specs=[pl.BlockSpec((None, 8, S), lambda b: (b, 0, 0)), pl.BlockSpec((None, S, LANES), lambda b: (b, 0, 0)),
                  pl.BlockSpec((None, S, LANES), lambda b: (b, 0, 0)), pl.BlockSpec((1, LANES), lambda b: (0, 0))],
        out_specs=[pl.BlockSpec((None, S, LANES), lambda b: (b, 0, 0)), pl.BlockSpec((1, LANES), lambda b: (0, 0))],
        out_shape=[jax.ShapeDtypeStruct((B, S, LANES), F32), jax.ShapeDtypeStruct((1, LANES), F32)],
        compiler_params=_params(1),
    )(ddrow, ddcol, fl, bias)


def _attn_fwd(qkv, drow, n_heads, tq):
    B, S, _ = qkv.shape
    DA = n_heads * HEAD_DIM
    scale = HEAD_DIM ** -0.5

    n_pairs = n_heads // 2

    def kern(q_ref, k_ref, v_ref, dr_ref, o_ref, lse_ref):
        i = pl.program_id(1)
        lane = lax.broadcasted_iota(jnp.int32, (tq, LANES), 1)
        low = lane < HEAD_DIM
        causal = lax.broadcasted_iota(jnp.int32, (tq, tq), 1) <= lax.broadcasted_iota(jnp.int32, (tq, tq), 0)
        qms = []
        for p in range(n_pairs):
            q2 = q_ref[:, LANES * p:LANES * (p + 1)] * scale
            qms += [jnp.where(low, q2, jnp.zeros_like(q2)), jnp.where(low, jnp.zeros_like(q2), q2)]

        def step(j, carry, masked):
            ms, ls, accs = carry
            ks = pl.multiple_of(j * tq, tq)
            new_m, new_l, new_acc = [], [], []
            for p in range(n_pairs):
                cols = slice(LANES * p, LANES * (p + 1))
                k2, v2 = k_ref[pl.ds(ks, tq), cols], v_ref[pl.ds(ks, tq), cols]
                alphas, pvs = [], []
                for h in (2 * p, 2 * p + 1):
                    s = _dot(qms[h], k2, NT) - dr_ref[h, pl.ds(j, 1), :]
                    if masked:
                        s = jnp.where(causal, s, -jnp.inf)
                    m_new = jnp.maximum(ms[h], jnp.max(s, axis=1, keepdims=True))
                    alpha = jnp.exp(ms[h] - m_new)
                    pm = jnp.exp(s - m_new)
                    new_m.append(m_new)
                    new_l.append(alpha * ls[h] + jnp.sum(pm, axis=1, keepdims=True))
                    alphas.append(alpha)
                    pvs.append(_dot(pm.astype(BF16), v2, NN))
                new_acc.append(jnp.where(low, alphas[0], alphas[1]) * accs[p] + jnp.where(low, pvs[0], pvs[1]))
            return tuple(new_m), tuple(new_l), tuple(new_acc)

        init = (tuple(jnp.full((tq, 1), -jnp.inf, F32) for _ in range(n_heads)),
                tuple(jnp.zeros((tq, 1), F32) for _ in range(n_heads)),
                tuple(jnp.zeros((tq, LANES), F32) for _ in range(n_pairs)))
        ms, ls, accs = step(i, lax.fori_loop(0, i, functools.partial(step, masked=False), init), True)
        lse_mat = jnp.zeros((tq, LANES), F32)
        for p in range(n_pairs):
            l0, l1 = ls[2 * p], ls[2 * p + 1]
            o_ref[:, LANES * p:LANES * (p + 1)] = (accs[p] / jnp.where(low, l0, l1)).astype(BF16)
            lse_mat = jnp.where(lane == 2 * p, ms[2 * p] + jnp.log(l0), lse_mat)
            lse_mat = jnp.where(lane == 2 * p + 1, ms[2 * p + 1] + jnp.log(l1), lse_mat)
        lse_ref[...] = lse_mat

    nq = S // tq
    return pl.pallas_call(
        kern, name="attn_fwd", grid=(B, nq),
        in_specs=[pl.BlockSpec((None, tq, DA), lambda b, i: (b, i, 0)),
                  pl.BlockSpec((None, S, DA), lambda b, i: (b, 0, 1)),
                  pl.BlockSpec((None, S, DA), lambda b, i: (b, 0, 2)),
                  pl.BlockSpec((None, 8, nq, tq), lambda b, i: (b, 0, 0, 0))],
        out_specs=[pl.BlockSpec((None, tq, DA), lambda b, i: (b, i, 0)),
                   pl.BlockSpec((None, tq, LANES), lambda b, i: (b, i, 0))],
        out_shape=[jax.ShapeDtypeStruct((B, S, DA), BF16), jax.ShapeDtypeStruct((B, S, LANES), F32)],
        compiler_params=_params(2),
    )(qkv, qkv, qkv, drow)


def _attn_bwd(qkv, drow, o, lse, dycat, n_heads, tq):
    B, S, _ = qkv.shape
    DA = n_heads * HEAD_DIM
    scale = HEAD_DIM ** -0.5
    nq = S // tq

    n_pairs = n_heads // 2

    def kern(q_ref, k_ref, v_ref, dr_ref, o_ref, lse_ref, do_ref, dq_ref, dk_ref, dv_ref, ddr_ref, ddc_ref,
             dk_acc, dv_acc, qm_s, dom_s, delta_s, rs_s, dq_s):
        i = pl.program_id(1)

        @pl.when(i == 0)
        def _():
            dk_acc[...] = jnp.zeros_like(dk_acc)
            dv_acc[...] = jnp.zeros_like(dv_acc)
            ddr_ref[...] = jnp.zeros_like(ddr_ref)

        lane = lax.broadcasted_iota(jnp.int32, (tq, LANES), 1)
        low = lane < HEAD_DIM
        causal = lax.broadcasted_iota(jnp.int32, (tq, tq), 1) <= lax.broadcasted_iota(jnp.int32, (tq, tq), 0)
        for p in range(n_pairs):
            cols = slice(LANES * p, LANES * (p + 1))
            q2 = q_ref[:, cols] * scale
            do_f = do_ref[:, cols]
            do2 = do_f.astype(BF16)
            prod = do_f * o_ref[:, cols].astype(F32)
            qm_s[2 * p] = jnp.where(low, q2, jnp.zeros_like(q2))
            qm_s[2 * p + 1] = jnp.where(low, jnp.zeros_like(q2), q2)
            dom_s[2 * p] = jnp.where(low, do2, jnp.zeros_like(do2))
            dom_s[2 * p + 1] = jnp.where(low, jnp.zeros_like(do2), do2)
            delta_s[2 * p] = jnp.sum(jnp.where(low, prod, 0.0), axis=1, keepdims=True)
            delta_s[2 * p + 1] = jnp.sum(jnp.where(low, 0.0, prod), axis=1, keepdims=True)
            dq_s[p] = jnp.zeros((tq, LANES), F32)
        rs_s[...] = jnp.zeros(rs_s.shape, F32)

        def step(j, masked):
            ks = pl.multiple_of(j * tq, tq)
            for p in range(n_pairs):
                cols = slice(LANES * p, LANES * (p + 1))
                k2, v2 = k_ref[pl.ds(ks, tq), cols], v_ref[pl.ds(ks, tq), cols]
                dvs, dks, dqs = [], [], []
                for h in (2 * p, 2 * p + 1):
                    qm, dom = qm_s[h], dom_s[h]
                    s = _dot(qm, k2, NT) - dr_ref[h, pl.ds(j, 1), :]
                    if masked:
                        s = jnp.where(causal, s, -jnp.inf)
                    pm = jnp.exp(s - lse_ref[:, h:h + 1])
                    ds = pm * (_dot(dom, v2, NT) - delta_s[h])
                    ddr_ref[h, pl.ds(j, 1), :] -= jnp.sum(ds, axis=0, keepdims=True)
                    rs_s[h] += jnp.sum(ds, axis=1, keepdims=True)
                    dsb = ds.astype(BF16)
                    dvs.append(_dot(pm.astype(BF16), dom, TN))
                    dks.append(_dot(dsb, qm, TN))
                    dqs.append(_dot(dsb, k2, NN))
                dv_acc[pl.ds(ks, tq), cols] += dvs[0] + dvs[1]
                dk_acc[pl.ds(ks, tq), cols] += dks[0] + dks[1]
                dq_s[p] += jnp.where(low, dqs[0], dqs[1])

        def body(j, carry):
            step(j, False)
            return carry

        lax.fori_loop(0, i, body, 0)
        step(i, True)
        ddc = jnp.zeros((tq, LANES), F32)
        for p in range(n_pairs):
            dq_ref[:, LANES * p:LANES * (p + 1)] = (dq_s[p] * scale).astype(BF16)
            ddc = jnp.where(lane == 2 * p, rs_s[2 * p], ddc)
            ddc = jnp.where(lane == 2 * p + 1, rs_s[2 * p + 1], ddc)
        ddc_ref[...] = ddc

        @pl.when(i == nq - 1)
        def _():
            dk_ref[...] = dk_acc[...].astype(BF16)
            dv_ref[...] = dv_acc[...].astype(BF16)

    tile = pl.BlockSpec((None, tq, DA), lambda b, i: (b, i, 0))
    seq = pl.BlockSpec((None, S, DA), lambda b, i: (b, 0, 0))
    dec = pl.BlockSpec((None, 8, nq, tq), lambda b, i: (b, 0, 0, 0))
    return pl.pallas_call(
        kern, name="attn_bwd", grid=(B, nq),
        in_specs=[tile, pl.BlockSpec((None, S, DA), lambda b, i: (b, 0, 1)),
                  pl.BlockSpec((None, S, DA), lambda b, i: (b, 0, 2)), dec, tile,
                  pl.BlockSpec((None, tq, LANES), lambda b, i: (b, i, 0)), tile],
        out_specs=[tile, seq, seq, dec, pl.BlockSpec((None, tq, LANES), lambda b, i: (b, i, 0))],
        out_shape=[jax.ShapeDtypeStruct((B, S, DA), BF16)] * 3 + [jax.ShapeDtypeStruct((B, 8, nq, tq), F32),
                                                                  jax.ShapeDtypeStruct((B, S, LANES), F32)],
        scratch_shapes=[pltpu.VMEM((S, DA), F32), pltpu.VMEM((S, DA), F32),
                        pltpu.VMEM((n_heads, tq, LANES), BF16), pltpu.VMEM((n_heads, tq, LANES), BF16),
                        pltpu.VMEM((n_heads, tq, 1), F32), pltpu.VMEM((n_heads, tq, 1), F32),
                        pltpu.VMEM((n_pairs, tq, LANES), F32)],
        compiler_params=_params(2),
    )(qkv, qkv, qkv, drow, o, lse, dycat)


def _down(v, d, row):
    return jnp.where(row >= d, pltpu.roll(v, d, 0), 0.0)


def _up(v, d, row, S):
    return jnp.where(row < S - d, pltpu.roll(v, S - d, 0), 0.0)


def _window(v, shift, group):
    sums, acc, d = [], v, 1
    for _ in POOL_WINDOWS:
        acc = acc + shift(acc, d)
        sums.append(acc)
        d *= 2
    out = sums[-1]
    for gi in range(len(POOL_WINDOWS) - 2, -1, -1):
        out = jnp.where(group == gi, sums[gi], out)
    return out


def _pool_count(row, group):
    w = jnp.full(row.shape, POOL_WINDOWS[-1], jnp.int32)
    for gi in range(len(POOL_WINDOWS) - 2, -1, -1):
        w = jnp.where(group == gi, POOL_WINDOWS[gi], w)
    return jnp.minimum(row + 1, w).astype(F32)


def _mix_local_fwd(rest, wbd, ps, cw):
    B, S, C4 = rest.shape
    C = C4 // 4
    gw = C // len(POOL_WINDOWS)

    def kern(r_ref, w_ref, ps_ref, cw_ref, y_ref, pooled_ref):
        row = lax.broadcasted_iota(jnp.int32, (S, C), 0)
        group = lax.broadcasted_iota(jnp.int32, (S, C), 1) // gw
        u = r_ref[:, 0:C]
        pooled = _window(u, lambda v, d: _down(v, d, row), group) / _pool_count(row, group) - u
        pb = pooled.astype(BF16)
        pooled_ref[...] = pb
        y_ref[:, 0:C] = (_dot(pb, w_ref[...], NN) * ps_ref[...]).astype(BF16)
        uc = r_ref[:, 2 * C:3 * C] * r_ref[:, 3 * C:4 * C]
        y = cw_ref[0:1, :] * _down(uc, 2, row) + cw_ref[1:2, :] * _down(uc, 1, row) + cw_ref[2:3, :] * uc
        y_ref[:, C:2 * C] = (r_ref[:, C:2 * C] * y).astype(BF16)

    return pl.pallas_call(
        kern, name="mix_local_fwd", grid=(B,),
        in_specs=[pl.BlockSpec((None, S, C4), lambda b: (b, 0, 0)), pl.BlockSpec((C, C), lambda b: (0, 0)),
                  pl.BlockSpec((1, C), lambda b: (0, 0)), pl.BlockSpec((8, C), lambda b: (0, 0))],
        out_specs=[pl.BlockSpec((None, S, 2 * C), lambda b: (b, 0, 0)), pl.BlockSpec((None, S, C), lambda b: (b, 0, 0))],
        out_shape=[jax.ShapeDtypeStruct((B, S, 2 * C), BF16), jax.ShapeDtypeStruct((B, S, C), BF16)],
        compiler_params=_params(1),
    )(rest, wbd, ps, cw)


def _mix_local_bwd(rest, pooled, dycat, wbd, ps, cw):
    B, S, C4 = rest.shape
    C = C4 // 4
    gw = C // len(POOL_WINDOWS)

    def kern(r_ref, pooled_ref, d_ref, w_ref, ps_ref, cw_ref, dr_ref, dw_ref, dps_ref, dcw_ref):
        row = lax.broadcasted_iota(jnp.int32, (S, C), 0)
        group = lax.broadcasted_iota(jnp.int32, (S, C), 1) // gw
        dyp = d_ref[:, 0:C]
        dyc = d_ref[:, C:2 * C]
        pb = pooled_ref[...]
        dps = jnp.sum(dyp * _dot(pb, w_ref[...], NN), axis=0, keepdims=True)
        dzb = (dyp * ps_ref[...]).astype(BF16)
        dw = _dot(pb, dzb, TN)
        dpooled = _dot(dzb, w_ref[...], NT)
        g = dpooled / _pool_count(row, group)
        dr_ref[:, 0:C] = (_window(g, lambda v, d: _up(v, d, row, S), group) - dpooled).astype(BF16)
        cc, ch = r_ref[:, 2 * C:3 * C], r_ref[:, 3 * C:4 * C]
        uc = cc * ch
        u1, u2 = _down(uc, 1, row), _down(uc, 2, row)
        y = cw_ref[0:1, :] * u2 + cw_ref[1:2, :] * u1 + cw_ref[2:3, :] * uc
        dr_ref[:, C:2 * C] = (dyc * y).astype(BF16)
        dy = dyc * r_ref[:, C:2 * C]
        duc = cw_ref[0:1, :] * _up(dy, 2, row, S) + cw_ref[1:2, :] * _up(dy, 1, row, S) + cw_ref[2:3, :] * dy
        dr_ref[:, 2 * C:3 * C] = (duc * ch).astype(BF16)
        dr_ref[:, 3 * C:4 * C] = (duc * cc).astype(BF16)
        dcw = jnp.concatenate([jnp.sum(dy * u2, axis=0, keepdims=True), jnp.sum(dy * u1, axis=0, keepdims=True),
                               jnp.sum(dy * uc, axis=0, keepdims=True), jnp.zeros((5, C), F32)], axis=0)

        @pl.when(pl.program_id(0) == 0)
        def _():
            dw_ref[...] = dw
            dps_ref[...] = dps
            dcw_ref[...] = dcw

        @pl.when(pl.program_id(0) > 0)
        def _():
            dw_ref[...] += dw
            dps_ref[...] += dps
            dcw_ref[...] += dcw

    full = lambda shape: pl.BlockSpec(shape, lambda b: (0, 0))
    return pl.pallas_call(
        kern, name="mix_local_bwd", grid=(B,),
        in_specs=[pl.BlockSpec((None, S, C4), lambda b: (b, 0, 0)), pl.BlockSpec((None, S, C), lambda b: (b, 0, 0)),
                  pl.BlockSpec((None, S, 2 * C), lambda b: (b, 0, 1)), full((C, C)), full((1, C)), full((8, C))],
        out_specs=[pl.BlockSpec((None, S, C4), lambda b: (b, 0, 0)), full((C, C)), full((1, C)), full((8, C))],
        out_shape=[jax.ShapeDtypeStruct((B, S, C4), BF16), jax.ShapeDtypeStruct((C, C), F32),
                   jax.ShapeDtypeStruct((1, C), F32), jax.ShapeDtypeStruct((8, C), F32)],
        compiler_params=_params(1),
    )(rest, pooled, dycat, wbd, ps, cw)


def _adamw(w, g, m, v):
    R, C = w.shape
    tr = _tile(R, 512)

    def kern(w_ref, g_ref, m_ref, v_ref, d_ref, nm_ref, nv_ref):
        gv = g_ref[...]
        nm = ADAM_B1 * m_ref[...] + (1.0 - ADAM_B1) * gv
        nv = ADAM_B2 * v_ref[...] + (1.0 - ADAM_B2) * (gv * gv)
        m_hat = nm / (1.0 - ADAM_B1 ** ADAM_STEP)
        v_hat = nv / (1.0 - ADAM_B2 ** ADAM_STEP)
        d_ref[...] = -ADAM_LR * (m_hat / (jnp.sqrt(v_hat) + ADAM_EPS) + ADAM_WD * w_ref[...])
        nm_ref[...] = nm
        nv_ref[...] = nv

    blk = pl.BlockSpec((tr, C), lambda i: (i, 0))
    return pl.pallas_call(
        kern, name="adamw", grid=(R // tr,), in_specs=[blk] * 4, out_specs=[blk] * 3,
        out_shape=[jax.ShapeDtypeStruct((R, C), F32)] * 3, compiler_params=_params(1),
    )(w, g, m, v)


def _place():
    x, y, c = lax.axis_index("x"), lax.axis_index("y"), lax.axis_index("c")
    return x, y, c, [(1 - x, y), (x, 1 - y), (1 - x, 1 - y)]


def _comm_call(name, body, operands, out_shape, n_sems, aliases=None):
    any_spec = pl.BlockSpec(memory_space=pl.ANY)
    return pl.pallas_call(
        body, name=name, in_specs=[any_spec] * len(operands), out_specs=[any_spec] * len(out_shape),
        out_shape=out_shape, input_output_aliases=aliases or {},
        scratch_shapes=[pltpu.SemaphoreType.DMA((n,)) for n in n_sems],
    )(*operands)


def _place_shard(w, b, dtype):
    L, R, C = w.shape
    tr = _tile(R, 512)

    def kern(b_ref, w_ref, o_ref):
        o_ref[...] = w_ref[...].astype(dtype)

    return pl.pallas_call(
        kern, name="place_shard",
        grid_spec=pltpu.PrefetchScalarGridSpec(
            num_scalar_prefetch=1, grid=(L, R // tr),
            in_specs=[pl.BlockSpec((None, tr, C), lambda l, i, b_ref: (l, i, 0))],
            out_specs=pl.BlockSpec((None, None, tr, C), lambda l, i, b_ref: (l, b_ref[0], i, 0))),
        out_shape=jax.ShapeDtypeStruct((L, N_CHIPS, R, C), dtype), compiler_params=_params(2),
    )(b, w)


def _all_gather(bufs):
    n, hl = len(bufs), bufs[0].shape[0] // 2

    def body(*refs):
        outs = refs[n:2 * n]
        send_sems, recv_sems = refs[2 * n:]
        x, y, c, chips = _place()
        sibling = (x, y, 1 - c)

        def remote(k, j, chip, half, to):
            region = outs[k].at[pl.ds(half * hl, hl), 2 * chip[0] + chip[1]]
            return pltpu.make_async_remote_copy(
                src_ref=region, dst_ref=region, send_sem=send_sems.at[6 * k + j],
                recv_sem=recv_sems.at[6 * k + j], device_id=to, device_id_type=MESH)

        first = [remote(k, j, (x, y), c, (*chip, c)) for k in range(n) for j, chip in enumerate(chips)]
        for cp in first:
            cp.start()
        passed = []
        for k in range(n):
            for j, chip in enumerate(chips):
                remote(k, j, chip, c, (x, y, c)).wait_recv()
                passed.append(remote(k, 3 + j, chip, c, sibling))
                passed[-1].start()
        for k in range(n):
            for j, chip in enumerate(chips):
                remote(k, 3 + j, chip, 1 - c, (x, y, c)).wait_recv()
        for cp in first + passed:
            cp.wait_send()

    out_shape = [jax.ShapeDtypeStruct(s.shape, s.dtype) for s in bufs]
    return _comm_call("all_gather_weights", body, bufs, out_shape, (6 * n, 6 * n), aliases={k: k for k in range(n)})


def _rs_swap_halves(grads):
    n, hl = len(grads), grads[0].shape[0] // 2

    def body(*refs):
        ins, outs = refs[:n], refs[n:2 * n]
        send_sems, recv_sems = refs[2 * n:]
        x, y, c, _ = _place()
        copies = [pltpu.make_async_remote_copy(
            src_ref=ins[k].at[pl.ds((1 - c) * hl, hl)], dst_ref=outs[k], send_sem=send_sems.at[k],
            recv_sem=recv_sems.at[k], device_id=(x, y, 1 - c), device_id_type=MESH) for k in range(n)]
        for cp in copies:
            cp.start()
        for cp in copies:
            cp.wait()

    out_shape = [jax.ShapeDtypeStruct((hl,) + g.shape[1:], g.dtype) for g in grads]
    return _comm_call("rs_swap_halves", body, grads, out_shape, (n, n))


def _rs_exchange(parts):
    n = len(parts)

    def body(*refs):
        ins, outs = refs[:n], refs[n:2 * n]
        send_sems, recv_sems = refs[2 * n:]
        x, y, c, chips = _place()
        copies = [pltpu.make_async_remote_copy(
            src_ref=ins[k].at[:, 2 * chip[0] + chip[1]], dst_ref=outs[k].at[j], send_sem=send_sems.at[3 * k + j],
            recv_sem=recv_sems.at[3 * k + j], device_id=(*chip, c), device_id_type=MESH)
            for k in range(n) for j, chip in enumerate(chips)]
        for cp in copies:
            cp.start()
        for cp in copies:
            cp.wait()

    out_shape = [jax.ShapeDtypeStruct((3, p.shape[0]) + p.shape[2:], p.dtype) for p in parts]
    return _comm_call("rs_exchange", body, parts, out_shape, (3 * n, 3 * n))


def _rs_share(bufs):
    n, hl = len(bufs), bufs[0].shape[0] // 2

    def body(*refs):
        outs = refs[n:2 * n]
        send_sems, recv_sems = refs[2 * n:]
        x, y, c, _ = _place()

        def half(k, which):
            region = outs[k].at[pl.ds(which * hl, hl)]
            return pltpu.make_async_remote_copy(
                src_ref=region, dst_ref=region, send_sem=send_sems.at[k], recv_sem=recv_sems.at[k],
                device_id=(x, y, 1 - c), device_id_type=MESH)

        sends = [half(k, c) for k in range(n)]
        for cp in sends:
            cp.start()
        for k in range(n):
            half(k, 1 - c).wait_recv()
        for cp in sends:
            cp.wait_send()

    out_shape = [jax.ShapeDtypeStruct(h.shape, h.dtype) for h in bufs]
    return _comm_call("rs_share", body, bufs, out_shape, (n, n), aliases={k: k for k in range(n)})


def _all_reduce_small(v):
    n = v.shape[0]

    def body(v_ref, o_ref, gbuf, send_sems, recv_sems):
        x, y, c, _ = _place()
        me = 4 * x + 2 * y + c
        gbuf[me] = v_ref[...]
        copies, waits = [], []
        for r in range(1, N_DEV):
            px = 1 - x if r & 4 else x
            py = 1 - y if r & 2 else y
            pc = 1 - c if r & 1 else c
            mk = functools.partial(pltpu.make_async_remote_copy, src_ref=v_ref, send_sem=send_sems.at[r - 1],
                                   recv_sem=recv_sems.at[r - 1], device_id=(px, py, pc), device_id_type=MESH)
            copies.append(mk(dst_ref=gbuf.at[me]))
            waits.append(mk(dst_ref=gbuf.at[4 * px + 2 * py + pc]))
        for cp in copies:
            cp.start()
        for cp in waits:
            cp.wait_recv()
        for cp in copies:
            cp.wait_send()
        acc = gbuf[0]
        for d in range(1, N_DEV):
            acc = acc + gbuf[d]
        o_ref[...] = acc

    vm = pl.BlockSpec(memory_space=pltpu.VMEM)
    return pl.pallas_call(
        body, name="all_reduce_small", in_specs=[vm], out_specs=vm, out_shape=jax.ShapeDtypeStruct(v.shape, F32),
        scratch_shapes=[pltpu.VMEM((N_DEV, n, LANES), F32), pltpu.SemaphoreType.DMA((N_DEV - 1,)),
                        pltpu.SemaphoreType.DMA((N_DEV - 1,))],
        compiler_params=pltpu.CompilerParams(vmem_limit_bytes=VMEM_LIMIT),
    )(v)


def _add_half(g, h1, c):
    hl, nb, R, C = h1.shape
    g3, h3 = g.reshape(2 * hl, nb * R, C), h1.reshape(hl, nb * R, C)
    tr = _tile(nb * R, 512)

    def kern(c_ref, g_ref, h_ref, o_ref):
        o_ref[...] = (g_ref[...].astype(F32) + h_ref[...].astype(F32)).astype(BF16)

    blk = pl.BlockSpec((None, tr, C), lambda l, i, c_ref: (l, i, 0))
    out = pl.pallas_call(
        kern, name="rs_add_half",
        grid_spec=pltpu.PrefetchScalarGridSpec(
            num_scalar_prefetch=1, grid=(hl, nb * R // tr),
            in_specs=[pl.BlockSpec((None, tr, C), lambda l, i, c_ref: (c_ref[0] * hl + l, i, 0)), blk],
            out_specs=blk),
        out_shape=jax.ShapeDtypeStruct(h3.shape, BF16), compiler_params=_params(2),
    )(c, g3, h3)
    return out.reshape(h1.shape)


def _add_blocks(p, h2, bc):
    hl, nb, R, C = p.shape
    tr = _tile(R, 512)

    def kern(bc_ref, p_ref, h0_ref, h1_ref, h2_ref, o_ref):
        o_ref[...] = ((p_ref[...].astype(F32) + h0_ref[...].astype(F32)) + h1_ref[...].astype(F32)) + h2_ref[...].astype(F32)

    def other(j):
        return pl.BlockSpec((None, None, tr, C), lambda l, i, bc_ref: (j, l, i, 0))

    return pl.pallas_call(
        kern, name="rs_add_blocks",
        grid_spec=pltpu.PrefetchScalarGridSpec(
            num_scalar_prefetch=1, grid=(hl, R // tr),
            in_specs=[pl.BlockSpec((None, None, tr, C), lambda l, i, bc_ref: (l, bc_ref[0], i, 0)),
                      other(0), other(1), other(2)],
            out_specs=pl.BlockSpec((None, tr, C), lambda l, i, bc_ref: (bc_ref[1] * hl + l, i, 0))),
        out_shape=jax.ShapeDtypeStruct((2 * hl, R, C), F32), compiler_params=_params(2),
    )(bc, p, h2, h2, h2)


def _reduce_scatter(grads):
    x, y, c = lax.axis_index("x"), lax.axis_index("y"), lax.axis_index("c")
    cs = jnp.reshape(c, (1,)).astype(jnp.int32)
    bc = jnp.stack([2 * x + y, c]).astype(jnp.int32)
    sib = _rs_swap_halves(grads)
    parts = [_add_half(g, h, cs) for g, h in zip(grads, sib)]
    others = _rs_exchange(parts)
    return _rs_share([_add_blocks(p, o, bc) for p, o in zip(parts, others)])


WEIGHTS = ("norm_ffn1", "w_ffn1_in", "w_ffn1_out", "norm_mix", "w_mix_in", "b_forget", "w_pool", "pool_scale",
           "conv_w", "w_mix_out", "norm_ffn2", "w_ffn2_in", "w_ffn2_out", "norm_final")
BIG = ("w_ffn1_in", "w_ffn1_out", "w_mix_in", "w_mix_out", "w_ffn2_in", "w_ffn2_out")
SMALL = ("norm_ffn1", "norm_mix", "b_forget", "w_pool", "pool_scale", "conv_w", "norm_ffn2", "norm_final")


def _prep_weights(small, gathered, D):
    DA, C, H = D // 2, D // 4, D // 2 // HEAD_DIM
    L = small["norm_ffn1"].shape[0]
    w_in = jnp.concatenate([gathered["w_mix_in"][:, b] for b in range(N_CHIPS)], axis=2)
    wqkv, wrest = w_in[:, :, :3 * DA], w_in[:, :, 3 * DA + H:]
    wf = jnp.pad(w_in[:, :, 3 * DA:3 * DA + H], ((0, 0), (0, 0), (0, LANES - H)))
    gw = C // len(POOL_WINDOWS)
    wbd = jnp.zeros((L, C, C), F32)
    for gi in range(len(POOL_WINDOWS)):
        wbd = wbd.at[:, gi * gw:(gi + 1) * gw, gi * gw:(gi + 1) * gw].set(small["w_pool"][:, gi])
    cw = jnp.concatenate([gathered["conv_w"][:, b] for b in range(N_CHIPS)], axis=2)
    return dict(
        g1=small["norm_ffn1"], gm=small["norm_mix"], g2=small["norm_ffn2"],
        w1in=gathered["w_ffn1_in"], w2in=gathered["w_ffn2_in"],
        w1out=gathered["w_ffn1_out"].reshape(L, -1, D), w2out=gathered["w_ffn2_out"].reshape(L, -1, D),
        wqkv=wqkv, wrest=wrest, wf=wf, wp=jnp.concatenate([wqkv, wrest, wf], axis=2),
        wmixout=gathered["w_mix_out"].reshape(L, D, D),
        bias=jnp.pad(small["b_forget"], ((0, 0), (0, LANES - H))), wbd=wbd.astype(BF16), ps=small["pool_scale"],
        cw=jnp.pad(cw, ((0, 0), (0, 8 - CONV_WIDTH), (0, 0))),
    )


def _layer_params(l, W):
    P = {k: (W[k], l) for k in ("w1in", "w2in", "w1out", "w2out", "wqkv", "wrest", "wf", "wp", "wmixout")}
    P.update({k: W[k][l][None] for k in ("g1", "gm", "g2", "bias", "ps")})
    P.update(wbd=W["wbd"][l], cw=W["cw"][l])
    return P


def _ffn_fwd(x, g, w_in, w_out):
    h, jac, act = _ffn_up(x, g, w_in)
    return _ffn_out(act, w_out, x)[0], (x, h, jac, act)


def _ffn_bwd(dres, saved, g, w_in, w_out):
    x, h, jac, act = saved
    dgu, dx, dg = _ffn_bwd_main(dres, jac, x, g, w_out, w_in)
    dw_out = _ffn_dw_out(act, dres)[0]
    dw_in = _ffn_dw_in(h, dgu)[0]
    return dx, dg, dw_in, dw_out.reshape(N_CHIPS, -1, dw_out.shape[1])


def _mixer_fwd(x, P, B, S, tq):
    T, D = x.shape
    DA, C, H = D // 2, D // 4, D // 2 // HEAD_DIM
    hn, qkv, rest, fl = _mix_up(x, P["gm"], P["wp"], (3 * DA, 4 * C))
    qkv, rest, fl = qkv.reshape(B, S, 3 * DA), rest.reshape(B, S, 4 * C), fl.reshape(B, S, LANES)
    drow = _decay_fwd(fl, P["bias"]).reshape(B, 8, S // tq, tq)
    o, lse = _attn_fwd(qkv, drow, H, tq)
    ypc, pooled = _mix_local_fwd(rest, P["wbd"], P["ps"], P["cw"])
    ycat = jnp.concatenate([o, ypc], axis=-1).reshape(T, D)
    return _proj("mix_out", ycat, P["wmixout"], F32, extra=x), (x, hn, qkv, rest, fl, drow, o, lse, pooled, ycat)


def _mixer_bwd(dres, saved, P, B, S, tq):
    x, hn, qkv, rest, fl, drow, o, lse, pooled, ycat = saved
    T, D = x.shape
    DA, C, H = D // 2, D // 4, D // 2 // HEAD_DIM
    dycat = _proj("mix_out_bwd", dres, P["wmixout"], F32, NT).reshape(B, S, D)
    dw_out = _dw("mix_out_dw", ycat, dres, BF16)
    dq, dk, dv, ddrow, ddcol = _attn_bwd(qkv, drow, o, lse, dycat, H, tq)
    dfl, dbias = _decay_bwd(ddrow.reshape(B, 8, S), ddcol, fl, P["bias"], H)
    drest, dwbd, dps, dcw = _mix_local_bwd(rest, pooled, dycat, P["wbd"], P["ps"], P["cw"])
    dproj = jnp.concatenate([dq, dk, dv, drest, dfl.astype(BF16)], axis=-1).reshape(T, 3 * DA + 4 * C + LANES)
    dwp = _dw("mix_in_dw", hn, dproj, F32)
    dx, dg = _mix_in_bwd(dproj, x, P["gm"], dres, P["wp"])
    n_q, n_r = 3 * DA, 4 * C
    dw_in = jnp.concatenate([dwp[:, :n_q], dwp[:, n_q + n_r:n_q + n_r + H], dwp[:, n_q:n_q + n_r]], axis=1)
    dw_in = dw_in.reshape(D, N_CHIPS, -1).transpose(1, 0, 2).astype(BF16)
    gw = C // len(POOL_WINDOWS)
    dw_pool = jnp.stack([dwbd[gi * gw:(gi + 1) * gw, gi * gw:(gi + 1) * gw] for gi in range(len(POOL_WINDOWS))])
    small = dict(norm_mix=dg[0], b_forget=dbias[0, :H], w_pool=dw_pool, pool_scale=dps[0], conv_w=dcw[:CONV_WIDTH])
    return dx, small, dw_in, dw_out.reshape(N_CHIPS, -1, D)


def _local_step(x, target, small, gathered):
    B, S, D = x.shape
    L = small["norm_ffn1"].shape[0]
    tq = _tile(S, 256)
    xt = x.reshape(B * S, D)
    saved, params = [], []
    W = _prep_weights(small, gathered, D)
    for l in range(L):
        P = _layer_params(l, W)
        xt, s1 = _ffn_fwd(xt, P["g1"], P["w1in"], P["w1out"])
        xt, s2 = _mixer_fwd(xt, P, B, S, tq)
        xt, s3 = _ffn_fwd(xt, P["g2"], P["w2in"], P["w2out"])
        saved.append((s1, s2, s3))
        params.append(P)
    dres, dgf, loss = _final_loss(xt, small["norm_final"][None], target.reshape(B * S, D))
    big = {k: [None] * L for k in BIG}
    sm = {k: [None] * L for k in SMALL if k != "norm_final"}
    for l in reversed(range(L)):
        P, (s1, s2, s3) = params[l], saved[l]
        dres, dg2, big["w_ffn2_in"][l], big["w_ffn2_out"][l] = _ffn_bwd(dres, s3, P["g2"], P["w2in"], P["w2out"])
        dres, smix, big["w_mix_in"][l], big["w_mix_out"][l] = _mixer_bwd(dres, s2, P, B, S, tq)
        dres, dg1, big["w_ffn1_in"][l], big["w_ffn1_out"][l] = _ffn_bwd(dres, s1, P["g1"], P["w1in"], P["w1out"])
        sm["norm_ffn1"][l], sm["norm_ffn2"][l] = dg1[0], dg2[0]
        for k, val in smix.items():
            sm[k][l] = val
    big = {k: jnp.stack(val) for k, val in big.items()}
    sm = {k: jnp.stack(val) for k, val in sm.items()}
    sm["norm_final"] = dgf[0]
    return loss[0, 0], dres.reshape(B, S, D), big, sm


def _pack(parts, extra=()):
    flat = jnp.concatenate([p.reshape(-1) for p in parts] + [jnp.reshape(e, (1,)) for e in extra])
    n = -(-flat.shape[0] // (8 * LANES)) * 8
    return jnp.pad(flat, (0, n * LANES - flat.shape[0])).reshape(n, LANES)


def _unpack(buf, shapes):
    flat, out, at = buf.reshape(-1), [], 0
    for s in shapes:
        n = math.prod(s)
        out.append(flat[at:at + n].reshape(s))
        at += n
    return out, flat[at:]


def kernel(x, norm_ffn1, w_ffn1_in, w_ffn1_out, norm_mix, w_mix_in, b_forget, w_pool, pool_scale, conv_w, w_mix_out, norm_ffn2, w_ffn2_in, w_ffn2_out, norm_final, loss_target, m_norm_ffn1, m_w_ffn1_in, m_w_ffn1_out, m_norm_mix, m_w_mix_in, m_b_forget, m_w_pool, m_pool_scale, m_conv_w, m_w_mix_out, m_norm_ffn2, m_w_ffn2_in, m_w_ffn2_out, m_norm_final, v_norm_ffn1, v_w_ffn1_in, v_w_ffn1_out, v_norm_mix, v_w_mix_in, v_b_forget, v_w_pool, v_pool_scale, v_conv_w, v_w_mix_out, v_norm_ffn2, v_w_ffn2_in, v_w_ffn2_out, v_norm_final):
    w = dict(zip(WEIGHTS, (norm_ffn1, w_ffn1_in, w_ffn1_out, norm_mix, w_mix_in, b_forget, w_pool, pool_scale, conv_w, w_mix_out, norm_ffn2, w_ffn2_in, w_ffn2_out, norm_final)))
    m = dict(zip(WEIGHTS, (m_norm_ffn1, m_w_ffn1_in, m_w_ffn1_out, m_norm_mix, m_w_mix_in, m_b_forget, m_w_pool, m_pool_scale, m_conv_w, m_w_mix_out, m_norm_ffn2, m_w_ffn2_in, m_w_ffn2_out, m_norm_final)))
    v = dict(zip(WEIGHTS, (v_norm_ffn1, v_w_ffn1_in, v_w_ffn1_out, v_norm_mix, v_w_mix_in, v_b_forget, v_w_pool, v_pool_scale, v_conv_w, v_w_mix_out, v_norm_ffn2, v_w_ffn2_in, v_w_ffn2_out, v_norm_final)))
    block = 2 * lax.axis_index("x") + lax.axis_index("y")

    bs = jnp.reshape(block, (1,)).astype(jnp.int32)
    gathered = _all_gather([_place_shard(w[k], bs, BF16) for k in BIG] + [_place_shard(w["conv_w"], bs, F32)])
    gathered = dict(zip(BIG + ("conv_w",), gathered))
    small = {k: w[k] for k in SMALL}
    loss, grad_x, big, sm = _local_step(x, loss_target, small, gathered)

    grads = dict(zip(BIG, _reduce_scatter([big[k] for k in BIG])))
    order = [k for k in SMALL]
    total = _all_reduce_small(_pack([sm[k] for k in order], extra=(loss,)))
    parts, rest = _unpack(total, [sm[k].shape for k in order])
    grads.update(zip(order, parts))
    loss = rest[0]
    cs = conv_w.shape[2]
    grads["conv_w"] = lax.dynamic_slice_in_dim(grads["conv_w"], block * cs, cs, axis=2)

    delta, new_m, new_v = {}, {}, {}
    for k in BIG:
        two_d = lambda a: a.reshape(-1, a.shape[-1])
        d, nm, nv = _adamw(two_d(w[k]), two_d(grads[k]), two_d(m[k]), two_d(v[k]))
        delta[k], new_m[k], new_v[k] = d.reshape(w[k].shape), nm.reshape(w[k].shape), nv.reshape(w[k].shape)
    d, nm, nv = _adamw(*[_pack([t[k] for k in order]) for t in (w, grads, m, v)])
    shapes = [w[k].shape for k in order]
    for res, packed in ((delta, d), (new_m, nm), (new_v, nv)):
        res.update(zip(order, _unpack(packed, shapes)[0]))
    return (loss, grad_x, *[grads[k] for k in WEIGHTS], *[delta[k] for k in WEIGHTS],
            *[new_m[k] for k in WEIGHTS], *[new_v[k] for k in WEIGHTS])
```

```python
import functools
import math

import jax
import jax.numpy as jnp
from jax import lax
from jax.experimental import pallas as pl
from jax.experimental.pallas import tpu as pltpu

F32 = jnp.float32
BF16 = jnp.bfloat16
MESH = pl.DeviceIdType.MESH

HEAD_DIM = 64
POOL_WINDOWS = (2, 4, 8, 16)
CONV_WIDTH = 3
RMS_EPS = 1e-6
ADAM_LR = 0.001
ADAM_B1 = 0.9
ADAM_B2 = 0.999
ADAM_EPS = 1e-08
ADAM_WD = 0.01
ADAM_STEP = 10

LANES = 128
VMEM_LIMIT = 56 * 1024 * 1024
N_CHIPS = 4
N_DEV = 8

NN = (((1,), (0,)), ((), ()))
NT = (((1,), (1,)), ((), ()))
TN = (((0,), (0,)), ((), ()))


def _tile(n, pref):
    t = pref
    while t >= 8:
        if n % t == 0:
            return t
        t //= 2
    return n


def _params(n_grid):
    return pltpu.CompilerParams(dimension_semantics=("arbitrary",) * n_grid, vmem_limit_bytes=VMEM_LIMIT)


def _dot(a, b, dims):
    return lax.dot_general(a, b, dims, preferred_element_type=F32)


def _mm(name, dims, operands, in_specs, out_shape, out_specs, grid, acc_shape, epilogue):
    n_in, n_out, nk = len(operands), len(out_shape), grid[-1]

    def kern(*refs):
        extras, outs = refs[2:n_in], refs[n_in:n_in + n_out]
        if nk == 1:
            epilogue(_dot(refs[0][...].astype(BF16), refs[1][...].astype(BF16), dims), extras, outs)
            return
        acc = refs[n_in + n_out]
        k = pl.program_id(len(grid) - 1)

        @pl.when(k == 0)
        def _():
            acc[...] = jnp.zeros_like(acc)

        acc[...] += _dot(refs[0][...].astype(BF16), refs[1][...].astype(BF16), dims)

        @pl.when(k == nk - 1)
        def _():
            epilogue(acc[...], extras, outs)

    return pl.pallas_call(
        kern, name=name, grid=grid, in_specs=in_specs, out_specs=out_specs, out_shape=out_shape,
        scratch_shapes=[pltpu.VMEM(acc_shape, F32)] if nk > 1 else [],
        compiler_params=_params(len(grid)),
    )(*operands)


def _store(scale=None, dtype=None):
    def ep(acc, extras, outs):
        v = acc if scale is None else acc * scale
        outs[0][...] = v.astype(outs[0].dtype)
    return ep


def _residual(scale):
    def ep(acc, extras, outs):
        outs[0][...] = extras[0][...] + scale * acc
    return ep


def _rmsnorm_fwd(x, g):
    T, D = x.shape
    tr = _tile(T, 512)

    def kern(x_ref, g_ref, o_ref):
        xv = x_ref[...]
        r = lax.rsqrt(jnp.mean(xv * xv, axis=-1, keepdims=True) + RMS_EPS)
        o_ref[...] = (xv * r * g_ref[...]).astype(BF16)

    return pl.pallas_call(
        kern, name="rmsnorm_fwd", grid=(T // tr,),
        in_specs=[pl.BlockSpec((tr, D), lambda i: (i, 0)), pl.BlockSpec((1, D), lambda i: (0, 0))],
        out_specs=pl.BlockSpec((tr, D), lambda i: (i, 0)),
        out_shape=jax.ShapeDtypeStruct((T, D), BF16), compiler_params=_params(1),
    )(x, g)


def _rmsnorm_bwd(x, g, dh, dres):
    T, D = x.shape
    tr = _tile(T, 256)

    def kern(x_ref, g_ref, dh_ref, dres_ref, dx_ref, dg_ref):
        xv, dhv = x_ref[...], dh_ref[...]
        r = lax.rsqrt(jnp.mean(xv * xv, axis=-1, keepdims=True) + RMS_EPS)
        y = xv * r
        dy = dhv * g_ref[...]
        dx_ref[...] = dres_ref[...] + r * (dy - y * jnp.mean(dy * y, axis=-1, keepdims=True))
        part = jnp.sum(dhv * y, axis=0, keepdims=True)

        @pl.when(pl.program_id(0) == 0)
        def _():
            dg_ref[...] = part

        @pl.when(pl.program_id(0) > 0)
        def _():
            dg_ref[...] += part

    row = pl.BlockSpec((tr, D), lambda i: (i, 0))
    vec = pl.BlockSpec((1, D), lambda i: (0, 0))
    return pl.pallas_call(
        kern, name="rmsnorm_bwd", grid=(T // tr,), in_specs=[row, vec, row, row], out_specs=[row, vec],
        out_shape=[jax.ShapeDtypeStruct((T, D), F32), jax.ShapeDtypeStruct((1, D), F32)],
        compiler_params=_params(1),
    )(x, g, dh, dres)


def _final_loss(x, g, target):
    T, D = x.shape
    tr = _tile(T, 256)

    def kern(x_ref, g_ref, t_ref, dx_ref, dg_ref, loss_ref):
        xv = x_ref[...]
        r = lax.rsqrt(jnp.mean(xv * xv, axis=-1, keepdims=True) + RMS_EPS)
        y = xv * r
        err = y * g_ref[...] - t_ref[...]
        lpart = 0.5 * jnp.sum(jnp.mean(err * err, axis=-1, keepdims=True), axis=0, keepdims=True)
        dh = err * (1.0 / D)
        dy = dh * g_ref[...]
        dx_ref[...] = r * (dy - y * jnp.mean(dy * y, axis=-1, keepdims=True))
        part = jnp.sum(dh * y, axis=0, keepdims=True)
        lrow = jnp.broadcast_to(lpart, (1, LANES))

        @pl.when(pl.program_id(0) == 0)
        def _():
            dg_ref[...] = part
            loss_ref[...] = lrow

        @pl.when(pl.program_id(0) > 0)
        def _():
            dg_ref[...] += part
            loss_ref[...] += lrow

    row = pl.BlockSpec((tr, D), lambda i: (i, 0))
    vec = pl.BlockSpec((1, D), lambda i: (0, 0))
    return pl.pallas_call(
        kern, name="final_loss", grid=(T // tr,), in_specs=[row, vec, row],
        out_specs=[row, vec, pl.BlockSpec((1, LANES), lambda i: (0, 0))],
        out_shape=[jax.ShapeDtypeStruct((T, D), F32), jax.ShapeDtypeStruct((1, D), F32),
                   jax.ShapeDtypeStruct((1, LANES), F32)],
        compiler_params=_params(1),
    )(x, g, target)


def _ffn_in(h, w4):
    T, D = h.shape
    w4, l = w4
    Fh = w4.shape[3]
    F = 2 * Fh
    tm = _tile(T, 512)

    def kern(h_ref, wg_ref, wu_ref, jac_ref, act_ref):
        hv = h_ref[...]
        gate = _dot(hv, wg_ref[...], NN)
        up = _dot(hv, wu_ref[...], NN)
        sg = jax.nn.sigmoid(gate)
        silu = gate * sg
        jac_ref[0] = (up * (sg + silu * (1.0 - sg))).astype(BF16)
        jac_ref[1] = silu.astype(BF16)
        act_ref[...] = (silu * up).astype(BF16)

    return pl.pallas_call(
        kern, name="ffn_in", grid=(2, T // tm),
        in_specs=[pl.BlockSpec((tm, D), lambda j, i: (i, 0)),
                  pl.BlockSpec((None, None, D, Fh), lambda j, i: (l, j, 0, 0)),
                  pl.BlockSpec((None, None, D, Fh), lambda j, i: (l, 2 + j, 0, 0))],
        out_specs=[pl.BlockSpec((2, tm, Fh), lambda j, i: (0, i, j)),
                   pl.BlockSpec((tm, Fh), lambda j, i: (i, j))],
        out_shape=[jax.ShapeDtypeStruct((2, T, F), BF16), jax.ShapeDtypeStruct((T, F), BF16)],
        compiler_params=_params(2),
    )(h, w4, w4)


def _resident(shape, index_map):
    return pl.BlockSpec(shape, index_map, pipeline_mode=pl.Buffered(1))


def _token_operand(token):
    return ([], []) if token is None else ([token], [pl.BlockSpec(token.shape, lambda i: (0, 0))])


def _ffn_up(x, g, w4, token=None):
    T, D = x.shape
    w4, l = w4
    Fh = w4.shape[3]
    F = 2 * Fh
    tm = _tile(T, 512)
    tok_ops, tok_specs = _token_operand(token)

    def kern(x_ref, g_ref, w_ref, *rest):
        h_ref, jac_ref, act_ref = rest[len(tok_ops):]
        xv = x_ref[...]
        r = lax.rsqrt(jnp.mean(xv * xv, axis=-1, keepdims=True) + RMS_EPS)
        hv = (xv * r * g_ref[...]).astype(BF16)
        h_ref[...] = hv
        for j in range(2):
            cols = slice(j * Fh, (j + 1) * Fh)
            gate = _dot(hv, w_ref[j], NN)
            up = _dot(hv, w_ref[2 + j], NN)
            sg = jax.nn.sigmoid(gate)
            silu = gate * sg
            jac_ref[0, :, cols] = (up * (sg + silu * (1.0 - sg))).astype(BF16)
            jac_ref[1, :, cols] = silu.astype(BF16)
            act_ref[:, cols] = (silu * up).astype(BF16)

    return pl.pallas_call(
        kern, name="ffn_up", grid=(T // tm,),
        in_specs=[pl.BlockSpec((tm, D), lambda i: (i, 0)), pl.BlockSpec((1, D), lambda i: (0, 0)),
                  _resident((None, 4, D, Fh), lambda i: (l, 0, 0, 0))] + tok_specs,
        out_specs=[pl.BlockSpec((tm, D), lambda i: (i, 0)), pl.BlockSpec((2, tm, F), lambda i: (0, i, 0)),
                   pl.BlockSpec((tm, F), lambda i: (i, 0))],
        out_shape=[jax.ShapeDtypeStruct((T, D), BF16), jax.ShapeDtypeStruct((2, T, F), BF16),
                   jax.ShapeDtypeStruct((T, F), BF16)],
        compiler_params=_params(1),
    )(x, g, w4, *tok_ops)


def _ffn_bwd_main(dres, jac, x, g, w_out, w4, token=None):
    T, D = dres.shape
    w_out, l = w_out
    w4, _ = w4
    F = w_out.shape[1]
    Fh = F // 2
    tm = _tile(T, 256)
    tok_ops, tok_specs = _token_operand(token)

    def kern(d_ref, jac_ref, x_ref, g_ref, wo_ref, wi_ref, *rest):
        dgu_ref, dx_ref, dg_ref = rest[len(tok_ops):]
        dv = d_ref[...]
        d16 = dv.astype(BF16)
        dh = jnp.zeros((tm, D), F32)
        for j in range(2):
            cols = slice(j * Fh, (j + 1) * Fh)
            dact = 0.5 * _dot(d16, wo_ref[cols, :], NT)
            dgate = (dact * jac_ref[0, :, cols].astype(F32)).astype(BF16)
            dup = (dact * jac_ref[1, :, cols].astype(F32)).astype(BF16)
            dgu_ref[0, :, cols] = dgate
            dgu_ref[1, :, cols] = dup
            dh = dh + _dot(dgate, wi_ref[j], NT) + _dot(dup, wi_ref[2 + j], NT)
        xv = x_ref[...]
        r = lax.rsqrt(jnp.mean(xv * xv, axis=-1, keepdims=True) + RMS_EPS)
        y = xv * r
        dy = dh * g_ref[...]
        dx_ref[...] = dv + r * (dy - y * jnp.mean(dy * y, axis=-1, keepdims=True))
        part = jnp.sum(dh * y, axis=0, keepdims=True)

        @pl.when(pl.program_id(0) == 0)
        def _():
            dg_ref[...] = part

        @pl.when(pl.program_id(0) > 0)
        def _():
            dg_ref[...] += part

    row = pl.BlockSpec((tm, D), lambda i: (i, 0))
    vec = pl.BlockSpec((1, D), lambda i: (0, 0))
    wide = pl.BlockSpec((2, tm, F), lambda i: (0, i, 0))
    return pl.pallas_call(
        kern, name="ffn_bwd_main", grid=(T // tm,),
        in_specs=[row, wide, row, vec, _resident((None, F, D), lambda i: (l, 0, 0)),
                  _resident((None, 4, D, Fh), lambda i: (l, 0, 0, 0))] + tok_specs,
        out_specs=[wide, row, vec],
        out_shape=[jax.ShapeDtypeStruct((2, T, F), BF16), jax.ShapeDtypeStruct((T, D), F32),
                   jax.ShapeDtypeStruct((1, D), F32)],
        compiler_params=_params(1),
    )(dres, jac, x, g, w_out, w4, *tok_ops)


def _ffn_out(act, w_out, x):
    T, F = act.shape
    w_out, l = w_out
    D = w_out.shape[2]
    tm = _tile(T, 512)
    return _mm("ffn_out", NN, [act, w_out, x],
               [pl.BlockSpec((tm, F), lambda i, k: (i, 0)), pl.BlockSpec((None, F, D), lambda i, k: (l, 0, 0)),
                pl.BlockSpec((tm, D), lambda i, k: (i, 0))],
               [jax.ShapeDtypeStruct((T, D), F32)], [pl.BlockSpec((tm, D), lambda i, k: (i, 0))],
               (T // tm, 1), None, _residual(0.5))


def _ffn_bwd_act(dres, w_out, jac):
    T, D = dres.shape
    w_out, l = w_out
    F = w_out.shape[1]
    Fh = F // 2
    tm = _tile(T, 512)

    def kern(d_ref, w_ref, jac_ref, o_ref):
        dact = 0.5 * _dot(d_ref[...].astype(BF16), w_ref[...], NT)
        o_ref[0] = (dact * jac_ref[0].astype(F32)).astype(BF16)
        o_ref[1] = (dact * jac_ref[1].astype(F32)).astype(BF16)

    return pl.pallas_call(
        kern, name="ffn_bwd_act", grid=(2, T // tm),
        in_specs=[pl.BlockSpec((tm, D), lambda j, i: (i, 0)), pl.BlockSpec((None, Fh, D), lambda j, i: (l, j, 0)),
                  pl.BlockSpec((2, tm, Fh), lambda j, i: (0, i, j))],
        out_specs=pl.BlockSpec((2, tm, Fh), lambda j, i: (0, i, j)),
        out_shape=jax.ShapeDtypeStruct((2, T, F), BF16), compiler_params=_params(2),
    )(dres, w_out, jac)


def _ffn_dw_out(act, dres):
    T, F = act.shape
    D = dres.shape[1]
    tm, tk = F // 2, _tile(T, 1024)
    return _mm("ffn_dw_out", TN, [act, dres],
               [pl.BlockSpec((tk, tm), lambda i, k: (k, i)), pl.BlockSpec((tk, D), lambda i, k: (k, 0))],
               [jax.ShapeDtypeStruct((F, D), BF16)], [pl.BlockSpec((tm, D), lambda i, k: (i, 0))],
               (2, T // tk), (tm, D), _store(0.5))


def _ffn_dw_in(h, dgu):
    T, D = h.shape
    Fh = dgu.shape[2] // 2
    tk = _tile(T, 1024)
    return _mm("ffn_dw_in", TN, [h, dgu],
               [pl.BlockSpec((tk, D), lambda j, k: (k, 0)),
                pl.BlockSpec((None, tk, Fh), lambda j, k: (j // 2, k, j % 2))],
               [jax.ShapeDtypeStruct((4, D, Fh), BF16)], [pl.BlockSpec((None, D, Fh), lambda j, k: (j, 0, 0))],
               (4, T // tk), (D, Fh), _store())


def _ffn_dh(dgu, w4):
    T = dgu.shape[1]
    w4, l = w4
    D, Fh = w4.shape[2], w4.shape[3]
    tm = _tile(T, 1024)
    return _mm("ffn_dh", NT, [dgu, w4],
               [pl.BlockSpec((None, tm, Fh), lambda i, k: (k // 2, i, k % 2)),
                pl.BlockSpec((None, None, D, Fh), lambda i, k: (l, k, 0, 0))],
               [jax.ShapeDtypeStruct((T, D), F32)], [pl.BlockSpec((tm, D), lambda i, k: (i, 0))],
               (T // tm, 4), (tm, D), _store())


def _proj(name, a, w, out_dtype, dims=NN, extra=None, scale=None):
    T, K = a.shape
    w, l = w
    N = w.shape[2] if dims == NN else w.shape[1]
    tm = _tile(T, 512)
    ops = [a, w] + ([extra] if extra is not None else [])
    specs = [pl.BlockSpec((tm, K), lambda i, k: (i, 0)), pl.BlockSpec((None,) + w.shape[1:], lambda i, k: (l, 0, 0))]
    if extra is not None:
        specs.append(pl.BlockSpec((tm, N), lambda i, k: (i, 0)))
    ep = _residual(1.0) if extra is not None else _store(scale)
    return _mm(name, dims, ops, specs, [jax.ShapeDtypeStruct((T, N), out_dtype)],
               [pl.BlockSpec((tm, N), lambda i, k: (i, 0))], (T // tm, 1), None, ep)[0]


def _mix_up(x, g, wp, widths):
    T, D = x.shape
    wp, l = wp
    n_qkv, n_rest = widths
    NP = wp.shape[2]
    tm = _tile(T, 512)

    def kern(x_ref, g_ref, w_ref, h_ref, qkv_ref, rest_ref, fl_ref):
        xv = x_ref[...]
        r = lax.rsqrt(jnp.mean(xv * xv, axis=-1, keepdims=True) + RMS_EPS)
        hv = (xv * r * g_ref[...]).astype(BF16)
        h_ref[...] = hv
        qkv_ref[...] = _dot(hv, w_ref[:, 0:n_qkv], NN).astype(BF16)
        rest_ref[...] = _dot(hv, w_ref[:, n_qkv:n_qkv + n_rest], NN)
        fl_ref[...] = _dot(hv, w_ref[:, n_qkv + n_rest:NP], NN)

    row = lambda n: pl.BlockSpec((tm, n), lambda i: (i, 0))
    return pl.pallas_call(
        kern, name="mix_up", grid=(T // tm,),
        in_specs=[row(D), pl.BlockSpec((1, D), lambda i: (0, 0)), _resident((None, D, NP), lambda i: (l, 0, 0))],
        out_specs=[row(D), row(n_qkv), row(n_rest), row(LANES)],
        out_shape=[jax.ShapeDtypeStruct((T, D), BF16), jax.ShapeDtypeStruct((T, n_qkv), BF16),
                   jax.ShapeDtypeStruct((T, n_rest), F32), jax.ShapeDtypeStruct((T, LANES), F32)],
        compiler_params=_params(1),
    )(x, g, wp)


def _mix_in_bwd(dproj, x, g, dres, wp):
    T, D = x.shape
    wp, l = wp
    NP = wp.shape[2]
    tm = _tile(T, 512)

    def kern(dp_ref, x_ref, g_ref, d_ref, w_ref, dx_ref, dg_ref):
        dh = _dot(dp_ref[...], w_ref[...], NT)
        xv = x_ref[...]
        r = lax.rsqrt(jnp.mean(xv * xv, axis=-1, keepdims=True) + RMS_EPS)
        y = xv * r
        dy = dh * g_ref[...]
        dx_ref[...] = d_ref[...] + r * (dy - y * jnp.mean(dy * y, axis=-1, keepdims=True))
        part = jnp.sum(dh * y, axis=0, keepdims=True)

        @pl.when(pl.program_id(0) == 0)
        def _():
            dg_ref[...] = part

        @pl.when(pl.program_id(0) > 0)
        def _():
            dg_ref[...] += part

    row = lambda n: pl.BlockSpec((tm, n), lambda i: (i, 0))
    vec = pl.BlockSpec((1, D), lambda i: (0, 0))
    return pl.pallas_call(
        kern, name="mix_in_bwd", grid=(T // tm,),
        in_specs=[row(NP), row(D), vec, row(D), _resident((None, D, NP), lambda i: (l, 0, 0))],
        out_specs=[row(D), vec],
        out_shape=[jax.ShapeDtypeStruct((T, D), F32), jax.ShapeDtypeStruct((1, D), F32)],
        compiler_params=_params(1),
    )(dproj, x, g, dres, wp)


def _dw(name, a, d, out_dtype):
    T, M = a.shape
    N = d.shape[1]
    tk = _tile(T, 1024 if M * N <= 1024 * 1408 else 512)
    return _mm(name, TN, [a, d],
               [pl.BlockSpec((tk, M), lambda i, k: (k, 0)), pl.BlockSpec((tk, N), lambda i, k: (k, 0))],
               [jax.ShapeDtypeStruct((M, N), out_dtype)], [pl.BlockSpec((M, N), lambda i, k: (0, 0))],
               (1, T // tk), (M, N), _store())[0]


def _log_sigmoid(z):
    return jnp.minimum(z, 0.0) - jnp.log(1.0 + jnp.exp(-jnp.abs(z)))


def _decay_fwd(fl, bias):
    B, S, _ = fl.shape

    def kern(fl_ref, b_ref, o_ref):
        d = _log_sigmoid(fl_ref[...] + b_ref[...])
        row = lax.broadcasted_iota(jnp.int32, (S, LANES), 0)
        sh = 1
        while sh < S:
            d = d + jnp.where(row >= sh, pltpu.roll(d, sh, 0), 0.0)
            sh *= 2
        o_ref[...] = d.T[0:8, :]

    return pl.pallas_call(
        kern, name="decay_fwd", grid=(B,),
        in_specs=[pl.BlockSpec((None, S, LANES), lambda b: (b, 0, 0)), pl.BlockSpec((1, LANES), lambda b: (0, 0))],
        out_specs=pl.BlockSpec((None, 8, S), lambda b: (b, 0, 0)),
        out_shape=jax.ShapeDtypeStruct((B, 8, S), F32), compiler_params=_params(1),
    )(fl, bias)


def _decay_bwd(ddrow, ddcol, fl, bias, n_heads):
    B, S, _ = fl.shape

    def kern(dd_ref, ddc_ref, fl_ref, b_ref, dfl_ref, db_ref):
        dd = jnp.concatenate([dd_ref[...], jnp.zeros((LANES - 8, S), F32)], axis=0).T + ddc_ref[...]
        row = lax.broadcasted_iota(jnp.int32, (S, LANES), 0)
        lane = lax.broadcasted_iota(jnp.int32, (S, LANES), 1)
        sh = 1
        while sh < S:
            dd = dd + jnp.where(row < S - sh, pltpu.roll(dd, S - sh, 0), 0.0)
            sh *= 2
        z = fl_ref[...] + b_ref[...]
        dfl = jnp.where(lane < n_heads, dd / (1.0 + jnp.exp(z)), 0.0)
        dfl_ref[...] = dfl
        part = jnp.sum(dfl, axis=0, keepdims=True)

        @pl.when(pl.program_id(0) == 0)
        def _():
            db_ref[...] = part

        @pl.when(pl.program_id(0) > 0)
        def _():
            db_ref[...] += part

    return pl.pallas_call(
        kern, name="decay_bwd", grid=(B,),
        in_specs=[pl.BlockSpec((None, 8, S), lambda b: (b, 0, 0)), pl.BlockSpec((None, S, LANES), lambda b: (b, 0, 0)),
                  pl.BlockSpec((None, S, LANES), lambda b: (b, 0, 0)), pl.BlockSpec((1, LANES), lambda b: (0, 0))],
        out_specs=[pl.BlockSpec((None, S, LANES), lambda b: (b, 0, 0)), pl.BlockSpec((1, LANES), lambda b: (0, 0))],
        out_shape=[jax.ShapeDtypeStruct((B, S, LANES), F32), jax.ShapeDtypeStruct((1, LANES), F32)],
        compiler_params=_params(1),
    )(ddrow, ddcol, fl, bias)


def _attn_fwd(qkv, drow, n_heads, tq):
    B, S, _ = qkv.shape
    DA = n_heads * HEAD_DIM
    scale = HEAD_DIM ** -0.5

    n_pairs = n_heads // 2

    def kern(q_ref, k_ref, v_ref, dr_ref, o_ref, lse_ref):
        i = pl.program_id(1)
        lane = lax.broadcasted_iota(jnp.int32, (tq, LANES), 1)
        low = lane < HEAD_DIM
        causal = lax.broadcasted_iota(jnp.int32, (tq, tq), 1) <= lax.broadcasted_iota(jnp.int32, (tq, tq), 0)
        qms = []
        for p in range(n_pairs):
            q2 = q_ref[:, LANES * p:LANES * (p + 1)] * scale
            qms += [jnp.where(low, q2, jnp.zeros_like(q2)), jnp.where(low, jnp.zeros_like(q2), q2)]

        def step(j, carry, masked):
            ms, ls, accs = carry
            ks = pl.multiple_of(j * tq, tq)
            new_m, new_l, new_acc = [], [], []
            for p in range(n_pairs):
                cols = slice(LANES * p, LANES * (p + 1))
                k2, v2 = k_ref[pl.ds(ks, tq), cols], v_ref[pl.ds(ks, tq), cols]
                alphas, pvs = [], []
                for h in (2 * p, 2 * p + 1):
                    s = _dot(qms[h], k2, NT) - dr_ref[h, pl.ds(j, 1), :]
                    if masked:
                        s = jnp.where(causal, s, -jnp.inf)
                    m_new = jnp.maximum(ms[h], jnp.max(s, axis=1, keepdims=True))
                    alpha = jnp.exp(ms[h] - m_new)
                    pm = jnp.exp(s - m_new)
                    new_m.append(m_new)
                    new_l.append(alpha * ls[h] + jnp.sum(pm, axis=1, keepdims=True))
                    alphas.append(alpha)
                    pvs.append(_dot(pm.astype(BF16), v2, NN))
                new_acc.append(jnp.where(low, alphas[0], alphas[1]) * accs[p] + jnp.where(low, pvs[0], pvs[1]))
            return tuple(new_m), tuple(new_l), tuple(new_acc)

        init = (tuple(jnp.full((tq, 1), -jnp.inf, F32) for _ in range(n_heads)),
                tuple(jnp.zeros((tq, 1), F32) for _ in range(n_heads)),
                tuple(jnp.zeros((tq, LANES), F32) for _ in range(n_pairs)))
        ms, ls, accs = step(i, lax.fori_loop(0, i, functools.partial(step, masked=False), init), True)
        lse_mat = jnp.zeros((tq, LANES), F32)
        for p in range(n_pairs):
            l0, l1 = ls[2 * p], ls[2 * p + 1]
            o_ref[:, LANES * p:LANES * (p + 1)] = (accs[p] / jnp.where(low, l0, l1)).astype(BF16)
            lse_mat = jnp.where(lane == 2 * p, ms[2 * p] + jnp.log(l0), lse_mat)
            lse_mat = jnp.where(lane == 2 * p + 1, ms[2 * p + 1] + jnp.log(l1), lse_mat)
        lse_ref[...] = lse_mat

    nq = S // tq
    return pl.pallas_call(
        kern, name="attn_fwd", grid=(B, nq),
        in_specs=[pl.BlockSpec((None, tq, DA), lambda b, i: (b, i, 0)),
                  pl.BlockSpec((None, S, DA), lambda b, i: (b, 0, 1)),
                  pl.BlockSpec((None, S, DA), lambda b, i: (b, 0, 2)),
                  pl.BlockSpec((None, 8, nq, tq), lambda b, i: (b, 0, 0, 0))],
        out_specs=[pl.BlockSpec((None, tq, DA), lambda b, i: (b, i, 0)),
                   pl.BlockSpec((None, tq, LANES), lambda b, i: (b, i, 0))],
        out_shape=[jax.ShapeDtypeStruct((B, S, DA), BF16), jax.ShapeDtypeStruct((B, S, LANES), F32)],
        compiler_params=_params(2),
    )(qkv, qkv, qkv, drow)


def _attn_bwd(qkv, drow, o, lse, dycat, n_heads, tq):
    B, S, _ = qkv.shape
    DA = n_heads * HEAD_DIM
    scale = HEAD_DIM ** -0.5
    nq = S // tq

    n_pairs = n_heads // 2

    def kern(q_ref, k_ref, v_ref, dr_ref, o_ref, lse_ref, do_ref, dq_ref, dk_ref, dv_ref, ddr_ref, ddc_ref,
             dk_acc, dv_acc, qm_s, dom_s, delta_s, rs_s, dq_s):
        i = pl.program_id(1)

        @pl.when(i == 0)
        def _():
            dk_acc[...] = jnp.zeros_like(dk_acc)
            dv_acc[...] = jnp.zeros_like(dv_acc)
            ddr_ref[...] = jnp.zeros_like(ddr_ref)

        lane = lax.broadcasted_iota(jnp.int32, (tq, LANES), 1)
        low = lane < HEAD_DIM
        causal = lax.broadcasted_iota(jnp.int32, (tq, tq), 1) <= lax.broadcasted_iota(jnp.int32, (tq, tq), 0)
        for p in range(n_pairs):
            cols = slice(LANES * p, LANES * (p + 1))
            q2 = q_ref[:, cols] * scale
            do_f = do_ref[:, cols]
            do2 = do_f.astype(BF16)
            prod = do_f * o_ref[:, cols].astype(F32)
            qm_s[2 * p] = jnp.where(low, q2, jnp.zeros_like(q2))
            qm_s[2 * p + 1] = jnp.where(low, jnp.zeros_like(q2), q2)
            dom_s[2 * p] = jnp.where(low, do2, jnp.zeros_like(do2))
            dom_s[2 * p + 1] = jnp.where(low, jnp.zeros_like(do2), do2)
            delta_s[2 * p] = jnp.sum(jnp.where(low, prod, 0.0), axis=1, keepdims=True)
            delta_s[2 * p + 1] = jnp.sum(jnp.where(low, 0.0, prod), axis=1, keepdims=True)
            dq_s[p] = jnp.zeros((tq, LANES), F32)
        rs_s[...] = jnp.zeros(rs_s.shape, F32)

        def step(j, masked):
            ks = pl.multiple_of(j * tq, tq)
            for p in range(n_pairs):
                cols = slice(LANES * p, LANES * (p + 1))
                k2, v2 = k_ref[pl.ds(ks, tq), cols], v_ref[pl.ds(ks, tq), cols]
                dvs, dks, dqs = [], [], []
                for h in (2 * p, 2 * p + 1):
                    qm, dom = qm_s[h], dom_s[h]
                    s = _dot(qm, k2, NT) - dr_ref[h, pl.ds(j, 1), :]
                    if masked:
                        s = jnp.where(causal, s, -jnp.inf)
                    pm = jnp.exp(s - lse_ref[:, h:h + 1])
                    ds = pm * (_dot(dom, v2, NT) - delta_s[h])
                    ddr_ref[h, pl.ds(j, 1), :] -= jnp.sum(ds, axis=0, keepdims=True)
                    rs_s[h] += jnp.sum(ds, axis=1, keepdims=True)
                    dsb = ds.astype(BF16)
                    dvs.append(_dot(pm.astype(BF16), dom, TN))
                    dks.append(_dot(dsb, qm, TN))
                    dqs.append(_dot(dsb, k2, NN))
                dv_acc[pl.ds(ks, tq), cols] += dvs[0] + dvs[1]
                dk_acc[pl.ds(ks, tq), cols] += dks[0] + dks[1]
                dq_s[p] += jnp.where(low, dqs[0], dqs[1])

        def body(j, carry):
            step(j, False)
            return carry

        lax.fori_loop(0, i, body, 0)
        step(i, True)
        ddc = jnp.zeros((tq, LANES), F32)
        for p in range(n_pairs):
            dq_ref[:, LANES * p:LANES * (p + 1)] = (dq_s[p] * scale).astype(BF16)
            ddc = jnp.where(lane == 2 * p, rs_s[2 * p], ddc)
            ddc = jnp.where(lane == 2 * p + 1, rs_s[2 * p + 1], ddc)
        ddc_ref[...] = ddc

        @pl.when(i == nq - 1)
        def _():
            dk_ref[...] = dk_acc[...].astype(BF16)
            dv_ref[...] = dv_acc[...].astype(BF16)

    tile = pl.BlockSpec((None, tq, DA), lambda b, i: (b, i, 0))
    seq = pl.BlockSpec((None, S, DA), lambda b, i: (b, 0, 0))
    dec = pl.BlockSpec((None, 8, nq, tq), lambda b, i: (b, 0, 0, 0))
    return pl.pallas_call(
        kern, name="attn_bwd", grid=(B, nq),
        in_specs=[tile, pl.BlockSpec((None, S, DA), lambda b, i: (b, 0, 1)),
                  pl.BlockSpec((None, S, DA), lambda b, i: (b, 0, 2)), dec, tile,
                  pl.BlockSpec((None, tq, LANES), lambda b, i: (b, i, 0)), tile],
        out_specs=[tile, seq, seq, dec, pl.BlockSpec((None, tq, LANES), lambda b, i: (b, i, 0))],
        out_shape=[jax.ShapeDtypeStruct((B, S, DA), BF16)] * 3 + [jax.ShapeDtypeStruct((B, 8, nq, tq), F32),
                                                                  jax.ShapeDtypeStruct((B, S, LANES), F32)],
        scratch_shapes=[pltpu.VMEM((S, DA), F32), pltpu.VMEM((S, DA), F32),
                        pltpu.VMEM((n_heads, tq, LANES), BF16), pltpu.VMEM((n_heads, tq, LANES), BF16),
                        pltpu.VMEM((n_heads, tq, 1), F32), pltpu.VMEM((n_heads, tq, 1), F32),
                        pltpu.VMEM((n_pairs, tq, LANES), F32)],
        compiler_params=_params(2),
    )(qkv, qkv, qkv, drow, o, lse, dycat)


def _down(v, d, row):
    return jnp.where(row >= d, pltpu.roll(v, d, 0), 0.0)


def _up(v, d, row, S):
    return jnp.where(row < S - d, pltpu.roll(v, S - d, 0), 0.0)


def _window(v, shift, group):
    sums, acc, d = [], v, 1
    for _ in POOL_WINDOWS:
        acc = acc + shift(acc, d)
        sums.append(acc)
        d *= 2
    out = sums[-1]
    for gi in range(len(POOL_WINDOWS) - 2, -1, -1):
        out = jnp.where(group == gi, sums[gi], out)
    return out


def _pool_count(row, group):
    w = jnp.full(row.shape, POOL_WINDOWS[-1], jnp.int32)
    for gi in range(len(POOL_WINDOWS) - 2, -1, -1):
        w = jnp.where(group == gi, POOL_WINDOWS[gi], w)
    return jnp.minimum(row + 1, w).astype(F32)


def _mix_local_fwd(rest, wbd, ps, cw):
    B, S, C4 = rest.shape
    C = C4 // 4
    gw = C // len(POOL_WINDOWS)

    def kern(r_ref, w_ref, ps_ref, cw_ref, y_ref, pooled_ref):
        row = lax.broadcasted_iota(jnp.int32, (S, C), 0)
        group = lax.broadcasted_iota(jnp.int32, (S, C), 1) // gw
        u = r_ref[:, 0:C]
        pooled = _window(u, lambda v, d: _down(v, d, row), group) / _pool_count(row, group) - u
        pb = pooled.astype(BF16)
        pooled_ref[...] = pb
        y_ref[:, 0:C] = (_dot(pb, w_ref[...], NN) * ps_ref[...]).astype(BF16)
        uc = r_ref[:, 2 * C:3 * C] * r_ref[:, 3 * C:4 * C]
        y = cw_ref[0:1, :] * _down(uc, 2, row) + cw_ref[1:2, :] * _down(uc, 1, row) + cw_ref[2:3, :] * uc
        y_ref[:, C:2 * C] = (r_ref[:, C:2 * C] * y).astype(BF16)

    return pl.pallas_call(
        kern, name="mix_local_fwd", grid=(B,),
        in_specs=[pl.BlockSpec((None, S, C4), lambda b: (b, 0, 0)), pl.BlockSpec((C, C), lambda b: (0, 0)),
                  pl.BlockSpec((1, C), lambda b: (0, 0)), pl.BlockSpec((8, C), lambda b: (0, 0))],
        out_specs=[pl.BlockSpec((None, S, 2 * C), lambda b: (b, 0, 0)), pl.BlockSpec((None, S, C), lambda b: (b, 0, 0))],
        out_shape=[jax.ShapeDtypeStruct((B, S, 2 * C), BF16), jax.ShapeDtypeStruct((B, S, C), BF16)],
        compiler_params=_params(1),
    )(rest, wbd, ps, cw)


def _mix_local_bwd(rest, pooled, dycat, wbd, ps, cw):
    B, S, C4 = rest.shape
    C = C4 // 4
    gw = C // len(POOL_WINDOWS)

    def kern(r_ref, pooled_ref, d_ref, w_ref, ps_ref, cw_ref, dr_ref, dw_ref, dps_ref, dcw_ref):
        row = lax.broadcasted_iota(jnp.int32, (S, C), 0)
        group = lax.broadcasted_iota(jnp.int32, (S, C), 1) // gw
        dyp = d_ref[:, 0:C]
        dyc = d_ref[:, C:2 * C]
        pb = pooled_ref[...]
        dps = jnp.sum(dyp * _dot(pb, w_ref[...], NN), axis=0, keepdims=True)
        dzb = (dyp * ps_ref[...]).astype(BF16)
        dw = _dot(pb, dzb, TN)
        dpooled = _dot(dzb, w_ref[...], NT)
        g = dpooled / _pool_count(row, group)
        dr_ref[:, 0:C] = (_window(g, lambda v, d: _up(v, d, row, S), group) - dpooled).astype(BF16)
        cc, ch = r_ref[:, 2 * C:3 * C], r_ref[:, 3 * C:4 * C]
        uc = cc * ch
        u1, u2 = _down(uc, 1, row), _down(uc, 2, row)
        y = cw_ref[0:1, :] * u2 + cw_ref[1:2, :] * u1 + cw_ref[2:3, :] * uc
        dr_ref[:, C:2 * C] = (dyc * y).astype(BF16)
        dy = dyc * r_ref[:, C:2 * C]
        duc = cw_ref[0:1, :] * _up(dy, 2, row, S) + cw_ref[1:2, :] * _up(dy, 1, row, S) + cw_ref[2:3, :] * dy
        dr_ref[:, 2 * C:3 * C] = (duc * ch).astype(BF16)
        dr_ref[:, 3 * C:4 * C] = (duc * cc).astype(BF16)
        dcw = jnp.concatenate([jnp.sum(dy * u2, axis=0, keepdims=True), jnp.sum(dy * u1, axis=0, keepdims=True),
                               jnp.sum(dy * uc, axis=0, keepdims=True), jnp.zeros((5, C), F32)], axis=0)

        @pl.when(pl.program_id(0) == 0)
        def _():
            dw_ref[...] = dw
            dps_ref[...] = dps
            dcw_ref[...] = dcw

        @pl.when(pl.program_id(0) > 0)
        def _():
            dw_ref[...] += dw
            dps_ref[...] += dps
            dcw_ref[...] += dcw

    full = lambda shape: pl.BlockSpec(shape, lambda b: (0, 0))
    return pl.pallas_call(
        kern, name="mix_local_bwd", grid=(B,),
        in_specs=[pl.BlockSpec((None, S, C4), lambda b: (b, 0, 0)), pl.BlockSpec((None, S, C), lambda b: (b, 0, 0)),
                  pl.BlockSpec((None, S, 2 * C), lambda b: (b, 0, 1)), full((C, C)), full((1, C)), full((8, C))],
        out_specs=[pl.BlockSpec((None, S, C4), lambda b: (b, 0, 0)), full((C, C)), full((1, C)), full((8, C))],
        out_shape=[jax.ShapeDtypeStruct((B, S, C4), BF16), jax.ShapeDtypeStruct((C, C), F32),
                   jax.ShapeDtypeStruct((1, C), F32), jax.ShapeDtypeStruct((8, C), F32)],
        compiler_params=_params(1),
    )(rest, pooled, dycat, wbd, ps, cw)


def _adamw(w, g, m, v):
    R, C = w.shape
    tr = _tile(R, 512)

    def kern(w_ref, g_ref, m_ref, v_ref, d_ref, nm_ref, nv_ref):
        gv = g_ref[...]
        nm = ADAM_B1 * m_ref[...] + (1.0 - ADAM_B1) * gv
        nv = ADAM_B2 * v_ref[...] + (1.0 - ADAM_B2) * (gv * gv)
        m_hat = nm / (1.0 - ADAM_B1 ** ADAM_STEP)
        v_hat = nv / (1.0 - ADAM_B2 ** ADAM_STEP)
        d_ref[...] = -ADAM_LR * (m_hat / (jnp.sqrt(v_hat) + ADAM_EPS) + ADAM_WD * w_ref[...])
        nm_ref[...] = nm
        nv_ref[...] = nv

    blk = pl.BlockSpec((tr, C), lambda i: (i, 0))
    return pl.pallas_call(
        kern, name="adamw", grid=(R // tr,), in_specs=[blk] * 4, out_specs=[blk] * 3,
        out_shape=[jax.ShapeDtypeStruct((R, C), F32)] * 3, compiler_params=_params(1),
    )(w, g, m, v)


def _place():
    x, y, c = lax.axis_index("x"), lax.axis_index("y"), lax.axis_index("c")
    return x, y, c, [(1 - x, y), (x, 1 - y), (1 - x, 1 - y)]


def _comm_call(name, body, operands, out_shape, n_sems, aliases=None):
    any_spec = pl.BlockSpec(memory_space=pl.ANY)
    return pl.pallas_call(
        body, name=name, in_specs=[any_spec] * len(operands), out_specs=[any_spec] * len(out_shape),
        out_shape=out_shape, input_output_aliases=aliases or {},
        scratch_shapes=[pltpu.SemaphoreType.DMA((n,)) for n in n_sems],
    )(*operands)


def _my_block():
    return 2 * lax.axis_index("x") + lax.axis_index("y")


def _place_shard(w, dtype, first=0, count=None):
    L, R, C = w.shape
    count = L if count is None else count
    tr = _tile(R, 512)

    def kern(w_ref, o_ref):
        o_ref[...] = w_ref[...].astype(dtype)

    return pl.pallas_call(
        kern, name="place_shard", grid=(count, R // tr),
        in_specs=[pl.BlockSpec((None, tr, C), lambda l, i: (first + l, i, 0))],
        out_specs=pl.BlockSpec((None, None, tr, C), lambda l, i: (l, _my_block(), i, 0)),
        out_shape=jax.ShapeDtypeStruct((count, N_CHIPS, R, C), dtype), compiler_params=_params(2),
    )(w)


def _all_gather(bufs):
    n = len(bufs)

    def body(*refs):
        outs = refs[n:2 * n]
        send_sems, recv_sems = refs[2 * n:]
        x, y, c, chips = _place()
        sibling = (x, y, 1 - c)

        def remote(k, j, chip, half, to):
            hl = outs[k].shape[0] // 2
            region = outs[k].at[pl.ds(half * hl, hl), 2 * chip[0] + chip[1]]
            return pltpu.make_async_remote_copy(
                src_ref=region, dst_ref=region, send_sem=send_sems.at[6 * k + j],
                recv_sem=recv_sems.at[6 * k + j], device_id=to, device_id_type=MESH)

        first = [remote(k, j, (x, y), c, (*chip, c)) for k in range(n) for j, chip in enumerate(chips)]
        for cp in first:
            cp.start()
        passed = []
        for k in range(n):
            for j, chip in enumerate(chips):
                remote(k, j, chip, c, (x, y, c)).wait_recv()
                passed.append(remote(k, 3 + j, chip, c, sibling))
                passed[-1].start()
        for k in range(n):
            for j, chip in enumerate(chips):
                remote(k, 3 + j, chip, 1 - c, (x, y, c)).wait_recv()
        for cp in first + passed:
            cp.wait_send()

    out_shape = [jax.ShapeDtypeStruct(s.shape, s.dtype) for s in bufs]
    return _comm_call("all_gather_weights", body, bufs, out_shape, (6 * n, 6 * n), aliases={k: k for k in range(n)})


_HBM = pl.BlockSpec(memory_space=pltpu.HBM)
_SEM = pl.BlockSpec(memory_space=pltpu.SEMAPHORE)
_ANY = pl.BlockSpec(memory_space=pl.ANY)


def _split_start(name, bufs, n_copies, make_copies, after):
    n = len(bufs)

    def body(*refs):
        send_sems, recv_sems, token = refs[n + 1], refs[n + 2], refs[2 * n + 3]
        for cp in make_copies(refs[:n], send_sems, recv_sems):
            cp.start()
        token[...] = jnp.zeros_like(token)

    res = pl.pallas_call(
        body, name=name, in_specs=[_HBM] * n + [_ANY],
        out_shape=(pltpu.SemaphoreType.DMA((n_copies,)), pltpu.SemaphoreType.DMA((n_copies,)),
                   *[pltpu.HBM(b.shape, b.dtype) for b in bufs], jax.ShapeDtypeStruct((8, LANES), F32)),
        out_specs=(_SEM, _SEM, *[_HBM] * n, pl.BlockSpec(memory_space=pltpu.VMEM)),
        input_output_aliases={i: 2 + i for i in range(n)},
        compiler_params=pltpu.CompilerParams(has_side_effects=pltpu.SideEffectType.DATAFLOW_SIDE_EFFECTING),
    )(*[pltpu.with_memory_space_constraint(b, pltpu.HBM) for b in bufs], after)
    return res[0], res[1], list(res[2:2 + n]), res[2 + n]


def _split_wait(name, send_sems, recv_sems, bufs, make_copies, after):
    n = len(bufs)

    def body(*refs):
        for cp in make_copies(refs[:n], refs[n], refs[n + 1]):
            cp.wait_send()
            cp.wait_recv()

    return list(pl.pallas_call(
        body, name=name, in_specs=[_HBM] * n + [_SEM, _SEM, _ANY],
        out_shape=tuple(pltpu.HBM(b.shape, b.dtype) for b in bufs), out_specs=tuple([_HBM] * n),
        input_output_aliases={i: i for i in range(n)},
        compiler_params=pltpu.CompilerParams(has_side_effects=pltpu.SideEffectType.DATAFLOW_SIDE_EFFECTING),
    )(*bufs, send_sems, recv_sems, after))


def _gather_copies(refs, send_sems, recv_sems):
    x, y, c, chips = _place()
    return [pltpu.make_async_remote_copy(
        src_ref=ref.at[:, 2 * x + y], dst_ref=ref.at[:, 2 * x + y], send_sem=send_sems.at[3 * k + j],
        recv_sem=recv_sems.at[3 * k + j], device_id=(*chip, c), device_id_type=MESH)
        for k, ref in enumerate(refs) for j, chip in enumerate(chips)]


def _exchange_copies(refs, send_sems, recv_sems):
    n = len(refs) // 2
    x, y, c, chips = _place()
    return [pltpu.make_async_remote_copy(
        src_ref=refs[k].at[:, 2 * chip[0] + chip[1]], dst_ref=refs[n + k].at[j], send_sem=send_sems.at[3 * k + j],
        recv_sem=recv_sems.at[3 * k + j], device_id=(*chip, c), device_id_type=MESH)
        for k in range(n) for j, chip in enumerate(chips)]


def _rs_swap_halves(grads):
    n, hl = len(grads), grads[0].shape[0] // 2

    def body(*refs):
        ins, outs = refs[:n], refs[n:2 * n]
        send_sems, recv_sems = refs[2 * n:]
        x, y, c, _ = _place()
        copies = [pltpu.make_async_remote_copy(
            src_ref=ins[k].at[pl.ds((1 - c) * hl, hl)], dst_ref=outs[k], send_sem=send_sems.at[k],
            recv_sem=recv_sems.at[k], device_id=(x, y, 1 - c), device_id_type=MESH) for k in range(n)]
        for cp in copies:
            cp.start()
        for cp in copies:
            cp.wait()

    out_shape = [jax.ShapeDtypeStruct((hl,) + g.shape[1:], g.dtype) for g in grads]
    return _comm_call("rs_swap_halves", body, grads, out_shape, (n, n))


def _rs_exchange(parts):
    n = len(parts)

    def body(*refs):
        ins, outs = refs[:n], refs[n:2 * n]
        send_sems, recv_sems = refs[2 * n:]
        x, y, c, chips = _place()
        copies = [pltpu.make_async_remote_copy(
            src_ref=ins[k].at[:, 2 * chip[0] + chip[1]], dst_ref=outs[k].at[j], send_sem=send_sems.at[3 * k + j],
            recv_sem=recv_sems.at[3 * k + j], device_id=(*chip, c), device_id_type=MESH)
            for k in range(n) for j, chip in enumerate(chips)]
        for cp in copies:
            cp.start()
        for cp in copies:
            cp.wait()

    out_shape = [jax.ShapeDtypeStruct((3, p.shape[0]) + p.shape[2:], p.dtype) for p in parts]
    return _comm_call("rs_exchange", body, parts, out_shape, (3 * n, 3 * n))


def _rs_share(bufs):
    n, hl = len(bufs), bufs[0].shape[0] // 2

    def body(*refs):
        outs = refs[n:2 * n]
        send_sems, recv_sems = refs[2 * n:]
        x, y, c, _ = _place()

        def half(k, which):
            region = outs[k].at[pl.ds(which * hl, hl)]
            return pltpu.make_async_remote_copy(
                src_ref=region, dst_ref=region, send_sem=send_sems.at[k], recv_sem=recv_sems.at[k],
                device_id=(x, y, 1 - c), device_id_type=MESH)

        sends = [half(k, c) for k in range(n)]
        for cp in sends:
            cp.start()
        for k in range(n):
            half(k, 1 - c).wait_recv()
        for cp in sends:
            cp.wait_send()

    out_shape = [jax.ShapeDtypeStruct(h.shape, h.dtype) for h in bufs]
    return _comm_call("rs_share", body, bufs, out_shape, (n, n), aliases={k: k for k in range(n)})


def _all_reduce_small(v):
    n = v.shape[0]

    def body(v_ref, o_ref, gbuf, send_sems, recv_sems):
        x, y, c, _ = _place()
        me = 4 * x + 2 * y + c
        gbuf[me] = v_ref[...]
        copies, waits = [], []
        for r in range(1, N_DEV):
            px = 1 - x if r & 4 else x
            py = 1 - y if r & 2 else y
            pc = 1 - c if r & 1 else c
            mk = functools.partial(pltpu.make_async_remote_copy, src_ref=v_ref, send_sem=send_sems.at[r - 1],
                                   recv_sem=recv_sems.at[r - 1], device_id=(px, py, pc), device_id_type=MESH)
            copies.append(mk(dst_ref=gbuf.at[me]))
            waits.append(mk(dst_ref=gbuf.at[4 * px + 2 * py + pc]))
        for cp in copies:
            cp.start()
        for cp in waits:
            cp.wait_recv()
        for cp in copies:
            cp.wait_send()
        acc = gbuf[0]
        for d in range(1, N_DEV):
            acc = acc + gbuf[d]
        o_ref[...] = acc

    vm = pl.BlockSpec(memory_space=pltpu.VMEM)
    return pl.pallas_call(
        body, name="all_reduce_small", in_specs=[vm], out_specs=vm, out_shape=jax.ShapeDtypeStruct(v.shape, F32),
        scratch_shapes=[pltpu.VMEM((N_DEV, n, LANES), F32), pltpu.SemaphoreType.DMA((N_DEV - 1,)),
                        pltpu.SemaphoreType.DMA((N_DEV - 1,))],
        compiler_params=pltpu.CompilerParams(vmem_limit_bytes=VMEM_LIMIT),
    )(v)


def _add_half(g, h1):
    hl, nb, R, C = h1.shape
    g3, h3 = g.reshape(2 * hl, nb * R, C), h1.reshape(hl, nb * R, C)
    tr = _tile(nb * R, 512)

    def kern(g_ref, h_ref, o_ref):
        o_ref[...] = (g_ref[...].astype(F32) + h_ref[...].astype(F32)).astype(BF16)

    blk = pl.BlockSpec((None, tr, C), lambda l, i: (l, i, 0))
    out = pl.pallas_call(
        kern, name="rs_add_half", grid=(hl, nb * R // tr),
        in_specs=[pl.BlockSpec((None, tr, C), lambda l, i: (lax.axis_index("c") * hl + l, i, 0)), blk],
        out_specs=blk, out_shape=jax.ShapeDtypeStruct(h3.shape, BF16), compiler_params=_params(2),
    )(g3, h3)
    return out.reshape(h1.shape)


def _add_blocks(p, h2):
    hl, nb, R, C = p.shape
    tr = _tile(R, 512)

    def kern(p_ref, h0_ref, h1_ref, h2_ref, o_ref):
        o_ref[...] = ((p_ref[...].astype(F32) + h0_ref[...].astype(F32)) + h1_ref[...].astype(F32)) + h2_ref[...].astype(F32)

    def other(j):
        return pl.BlockSpec((None, None, tr, C), lambda l, i: (j, l, i, 0))

    return pl.pallas_call(
        kern, name="rs_add_blocks", grid=(hl, R // tr),
        in_specs=[pl.BlockSpec((None, None, tr, C), lambda l, i: (l, _my_block(), i, 0)), other(0), other(1), other(2)],
        out_specs=pl.BlockSpec((None, tr, C), lambda l, i: (lax.axis_index("c") * hl + l, i, 0)),
        out_shape=jax.ShapeDtypeStruct((2 * hl, R, C), F32), compiler_params=_params(2),
    )(p, h2, h2, h2)


def _reduce_scatter(grads):
    sib = _rs_swap_halves(grads)
    parts = [_add_half(g, h) for g, h in zip(grads, sib)]
    others = _rs_exchange(parts)
    return _rs_share([_add_blocks(p, o) for p, o in zip(parts, others)])


WEIGHTS = ("norm_ffn1", "w_ffn1_in", "w_ffn1_out", "norm_mix", "w_mix_in", "b_forget", "w_pool", "pool_scale",
           "conv_w", "w_mix_out", "norm_ffn2", "w_ffn2_in", "w_ffn2_out", "norm_final")
BIG = ("w_ffn1_in", "w_ffn1_out", "w_mix_in", "w_mix_out", "w_ffn2_in", "w_ffn2_out")
SMALL = ("norm_ffn1", "norm_mix", "b_forget", "w_pool", "pool_scale", "conv_w", "norm_ffn2", "norm_final")


def _prep_weights(small, gathered, conv_w, D, first):
    DA, C, H = D // 2, D // 4, D // 2 // HEAD_DIM
    L = gathered["w_mix_in"].shape[0]
    small = {k: val[first:first + L] for k, val in small.items() if k != "norm_final"}
    gathered = dict(gathered, conv_w=conv_w[first:first + L])
    w_in = jnp.concatenate([gathered["w_mix_in"][:, b] for b in range(N_CHIPS)], axis=2)
    wqkv, wrest = w_in[:, :, :3 * DA], w_in[:, :, 3 * DA + H:]
    wf = jnp.pad(w_in[:, :, 3 * DA:3 * DA + H], ((0, 0), (0, 0), (0, LANES - H)))
    gw = C // len(POOL_WINDOWS)
    wbd = jnp.zeros((L, C, C), F32)
    for gi in range(len(POOL_WINDOWS)):
        wbd = wbd.at[:, gi * gw:(gi + 1) * gw, gi * gw:(gi + 1) * gw].set(small["w_pool"][:, gi])
    cw = jnp.concatenate([gathered["conv_w"][:, b] for b in range(N_CHIPS)], axis=2)
    return dict(
        g1=small["norm_ffn1"], gm=small["norm_mix"], g2=small["norm_ffn2"],
        w1in=gathered["w_ffn1_in"], w2in=gathered["w_ffn2_in"],
        w1out=gathered["w_ffn1_out"].reshape(L, -1, D), w2out=gathered["w_ffn2_out"].reshape(L, -1, D),
        wp=jnp.concatenate([wqkv, wrest, wf], axis=2), wmixout=gathered["w_mix_out"].reshape(L, D, D),
        bias=jnp.pad(small["b_forget"], ((0, 0), (0, LANES - H))), wbd=wbd.astype(BF16), ps=small["pool_scale"],
        cw=jnp.pad(cw, ((0, 0), (0, 8 - CONV_WIDTH), (0, 0))),
    )


def _layer_params(l, W):
    P = {k: (W[k], l) for k in ("w1in", "w2in", "w1out", "w2out", "wp", "wmixout")}
    P.update({k: W[k][l][None] for k in ("g1", "gm", "g2", "bias", "ps")})
    P.update(wbd=W["wbd"][l], cw=W["cw"][l])
    return P


def _ffn_fwd(x, g, w_in, w_out, token=None):
    h, jac, act = _ffn_up(x, g, w_in, token)
    return _ffn_out(act, w_out, x)[0], (x, h, jac, act)


def _ffn_bwd(dres, saved, g, w_in, w_out, token=None):
    x, h, jac, act = saved
    dgu, dx, dg = _ffn_bwd_main(dres, jac, x, g, w_out, w_in, token)
    dw_out = _ffn_dw_out(act, dres)[0]
    dw_in = _ffn_dw_in(h, dgu)[0]
    return dx, dg, dw_in, dw_out.reshape(N_CHIPS, -1, dw_out.shape[1])


def _mixer_fwd(x, P, B, S, tq):
    T, D = x.shape
    DA, C, H = D // 2, D // 4, D // 2 // HEAD_DIM
    hn, qkv, rest, fl = _mix_up(x, P["gm"], P["wp"], (3 * DA, 4 * C))
    qkv, rest, fl = qkv.reshape(B, S, 3 * DA), rest.reshape(B, S, 4 * C), fl.reshape(B, S, LANES)
    drow = _decay_fwd(fl, P["bias"]).reshape(B, 8, S // tq, tq)
    o, lse = _attn_fwd(qkv, drow, H, tq)
    ypc, pooled = _mix_local_fwd(rest, P["wbd"], P["ps"], P["cw"])
    ycat = jnp.concatenate([o, ypc], axis=-1).reshape(T, D)
    return _proj("mix_out", ycat, P["wmixout"], F32, extra=x), (x, hn, qkv, rest, fl, drow, o, lse, pooled, ycat)


def _mixer_bwd(dres, saved, P, B, S, tq):
    x, hn, qkv, rest, fl, drow, o, lse, pooled, ycat = saved
    T, D = x.shape
    DA, C, H = D // 2, D // 4, D // 2 // HEAD_DIM
    dycat = _proj("mix_out_bwd", dres, P["wmixout"], F32, NT).reshape(B, S, D)
    dw_out = _dw("mix_out_dw", ycat, dres, BF16)
    dq, dk, dv, ddrow, ddcol = _attn_bwd(qkv, drow, o, lse, dycat, H, tq)
    dfl, dbias = _decay_bwd(ddrow.reshape(B, 8, S), ddcol, fl, P["bias"], H)
    drest, dwbd, dps, dcw = _mix_local_bwd(rest, pooled, dycat, P["wbd"], P["ps"], P["cw"])
    dproj = jnp.concatenate([dq, dk, dv, drest, dfl.astype(BF16)], axis=-1).reshape(T, 3 * DA + 4 * C + LANES)
    dwp = _dw("mix_in_dw", hn, dproj, F32)
    dx, dg = _mix_in_bwd(dproj, x, P["gm"], dres, P["wp"])
    n_q, n_r = 3 * DA, 4 * C
    dw_in = jnp.concatenate([dwp[:, :n_q], dwp[:, n_q + n_r:n_q + n_r + H], dwp[:, n_q:n_q + n_r]], axis=1)
    dw_in = dw_in.reshape(D, N_CHIPS, -1).transpose(1, 0, 2).astype(BF16)
    gw = C // len(POOL_WINDOWS)
    dw_pool = jnp.stack([dwbd[gi * gw:(gi + 1) * gw, gi * gw:(gi + 1) * gw] for gi in range(len(POOL_WINDOWS))])
    small = dict(norm_mix=dg[0], b_forget=dbias[0, :H], w_pool=dw_pool, pool_scale=dps[0], conv_w=dcw[:CONV_WIDTH])
    return dx, small, dw_in, dw_out.reshape(N_CHIPS, -1, D)


def _local_step(x, target, small, gathered_lo, conv_w, late):
    B, S, D = x.shape
    L = small["norm_ffn1"].shape[0]
    hl = L // 2
    tq = _tile(S, 256)
    xt = x.reshape(B * S, D)
    saved, params = [], []
    W = _prep_weights(small, gathered_lo, conv_w, D, 0)
    for l in range(L):
        if l == hl:
            W = _prep_weights(small, late.weights(xt), conv_w, D, hl)
        P = _layer_params(l % hl, W)
        xt, s1 = _ffn_fwd(xt, P["g1"], P["w1in"], P["w1out"], late.token if l == 0 else None)
        xt, s2 = _mixer_fwd(xt, P, B, S, tq)
        xt, s3 = _ffn_fwd(xt, P["g2"], P["w2in"], P["w2out"])
        saved.append((s1, s2, s3))
        params.append(P)
    dres, dgf, loss = _final_loss(xt, small["norm_final"][None], target.reshape(B * S, D))
    big = {k: [None] * L for k in BIG}
    sm = {k: [None] * L for k in SMALL if k != "norm_final"}
    token = None
    for l in reversed(range(L)):
        P, (s1, s2, s3) = params[l], saved[l]
        dres, dg2, big["w_ffn2_in"][l], big["w_ffn2_out"][l] = _ffn_bwd(dres, s3, P["g2"], P["w2in"], P["w2out"], token)
        dres, smix, big["w_mix_in"][l], big["w_mix_out"][l] = _mixer_bwd(dres, s2, P, B, S, tq)
        dres, dg1, big["w_ffn1_in"][l], big["w_ffn1_out"][l] = _ffn_bwd(dres, s1, P["g1"], P["w1in"], P["w1out"])
        sm["norm_ffn1"][l], sm["norm_ffn2"][l] = dg1[0], dg2[0]
        for k, val in smix.items():
            sm[k][l] = val
        token = late.grads({k: jnp.stack(big[k][hl:]) for k in BIG}) if l == hl else None
    big = {k: jnp.stack(val[:hl]) for k, val in big.items()}
    sm = {k: jnp.stack(val) for k, val in sm.items()}
    sm["norm_final"] = dgf[0]
    return loss[0, 0], dres.reshape(B, S, D), big, sm


class _Late:
    def __init__(self, w, first, count, after):
        placed = [_place_shard(w[k], BF16, first, count) for k in BIG]
        self._gather = _split_start("gather_late_start", placed, 3 * len(BIG), _gather_copies, after)
        self.token = self._gather[3]

    def weights(self, after):
        send_sems, recv_sems, bufs, _ = self._gather
        return dict(zip(BIG, _split_wait("gather_late_wait", send_sems, recv_sems, bufs, _gather_copies, after)))

    def grads(self, big):
        grads = [big[k] for k in BIG]
        self._parts = [_add_half(g, h) for g, h in zip(grads, _rs_swap_halves(grads))]
        lands = [lax.empty((3, p.shape[0]) + p.shape[2:], p.dtype) for p in self._parts]
        self._exchange = _split_start("reduce_late_start", self._parts + lands, 3 * len(BIG), _exchange_copies,
                                      self._parts[0])
        return self._exchange[3]

    def reduced(self, after):
        send_sems, recv_sems, bufs, _ = self._exchange
        bufs = _split_wait("reduce_late_wait", send_sems, recv_sems, bufs, _exchange_copies, after)
        n = len(BIG)
        return _rs_share([_add_blocks(p, o) for p, o in zip(bufs[:n], bufs[n:])])


def _pack(parts, extra=()):
    flat = jnp.concatenate([p.reshape(-1) for p in parts] + [jnp.reshape(e, (1,)) for e in extra])
    n = -(-flat.shape[0] // (8 * LANES)) * 8
    return jnp.pad(flat, (0, n * LANES - flat.shape[0])).reshape(n, LANES)


def _unpack(buf, shapes):
    flat, out, at = buf.reshape(-1), [], 0
    for s in shapes:
        n = math.prod(s)
        out.append(flat[at:at + n].reshape(s))
        at += n
    return out, flat[at:]


def kernel(x, norm_ffn1, w_ffn1_in, w_ffn1_out, norm_mix, w_mix_in, b_forget, w_pool, pool_scale, conv_w, w_mix_out, norm_ffn2, w_ffn2_in, w_ffn2_out, norm_final, loss_target, m_norm_ffn1, m_w_ffn1_in, m_w_ffn1_out, m_norm_mix, m_w_mix_in, m_b_forget, m_w_pool, m_pool_scale, m_conv_w, m_w_mix_out, m_norm_ffn2, m_w_ffn2_in, m_w_ffn2_out, m_norm_final, v_norm_ffn1, v_w_ffn1_in, v_w_ffn1_out, v_norm_mix, v_w_mix_in, v_b_forget, v_w_pool, v_pool_scale, v_conv_w, v_w_mix_out, v_norm_ffn2, v_w_ffn2_in, v_w_ffn2_out, v_norm_final):
    w = dict(zip(WEIGHTS, (norm_ffn1, w_ffn1_in, w_ffn1_out, norm_mix, w_mix_in, b_forget, w_pool, pool_scale, conv_w, w_mix_out, norm_ffn2, w_ffn2_in, w_ffn2_out, norm_final)))
    m = dict(zip(WEIGHTS, (m_norm_ffn1, m_w_ffn1_in, m_w_ffn1_out, m_norm_mix, m_w_mix_in, m_b_forget, m_w_pool, m_pool_scale, m_conv_w, m_w_mix_out, m_norm_ffn2, m_w_ffn2_in, m_w_ffn2_out, m_norm_final)))
    v = dict(zip(WEIGHTS, (v_norm_ffn1, v_w_ffn1_in, v_w_ffn1_out, v_norm_mix, v_w_mix_in, v_b_forget, v_w_pool, v_pool_scale, v_conv_w, v_w_mix_out, v_norm_ffn2, v_w_ffn2_in, v_w_ffn2_out, v_norm_final)))
    block = 2 * lax.axis_index("x") + lax.axis_index("y")

    L = norm_ffn1.shape[0]
    hl = L // 2
    gathered = _all_gather([_place_shard(w[k], BF16, 0, hl) for k in BIG] + [_place_shard(w["conv_w"], F32)])
    late = _Late(w, hl, L - hl, gathered[0])
    small = {k: w[k] for k in SMALL}
    loss, grad_x, big, sm = _local_step(x, loss_target, small, dict(zip(BIG, gathered)), gathered[-1], late)

    hi = late.reduced(grad_x)
    lo = _reduce_scatter([big[k] for k in BIG])
    grads = {k: jnp.concatenate([a, b]) for k, a, b in zip(BIG, lo, hi)}
    order = [k for k in SMALL]
    total = _all_reduce_small(_pack([sm[k] for k in order], extra=(loss,)))
    parts, rest = _unpack(total, [sm[k].shape for k in order])
    grads.update(zip(order, parts))
    loss = rest[0]
    cs = conv_w.shape[2]
    grads["conv_w"] = lax.dynamic_slice_in_dim(grads["conv_w"], block * cs, cs, axis=2)

    delta, new_m, new_v = {}, {}, {}
    for k in BIG:
        two_d = lambda a: a.reshape(-1, a.shape[-1])
        d, nm, nv = _adamw(two_d(w[k]), two_d(grads[k]), two_d(m[k]), two_d(v[k]))
        delta[k], new_m[k], new_v[k] = d.reshape(w[k].shape), nm.reshape(w[k].shape), nv.reshape(w[k].shape)
    d, nm, nv = _adamw(*[_pack([t[k] for k in order]) for t in (w, grads, m, v)])
    shapes = [w[k].shape for k in order]
    for res, packed in ((delta, d), (new_m, nm), (new_v, nv)):
        res.update(zip(order, _unpack(packed, shapes)[0]))
    return (loss, grad_x, *[grads[k] for k in WEIGHTS], *[delta[k] for k in WEIGHTS],
            *[new_m[k] for k in WEIGHTS], *[new_v[k] for k in WEIGHTS])
```

```python
import functools
import math

import jax
import jax.numpy as jnp
from jax import lax
from jax.experimental import pallas as pl
from jax.experimental.pallas import tpu as pltpu

F32 = jnp.float32
BF16 = jnp.bfloat16
MESH = pl.DeviceIdType.MESH

HEAD_DIM = 64
POOL_WINDOWS = (2, 4, 8, 16)
CONV_WIDTH = 3
RMS_EPS = 1e-6
ADAM_LR = 0.001
ADAM_B1 = 0.9
ADAM_B2 = 0.999
ADAM_EPS = 1e-08
ADAM_WD = 0.01
ADAM_STEP = 10

LANES = 128
VMEM_LIMIT = 56 * 1024 * 1024
N_CHIPS = 4
N_DEV = 8

NN = (((1,), (0,)), ((), ()))
NT = (((1,), (1,)), ((), ()))
TN = (((0,), (0,)), ((), ()))


def _tile(n, pref):
    t = pref
    while t >= 8:
        if n % t == 0:
            return t
        t //= 2
    return n


def _params(n_grid):
    return pltpu.CompilerParams(dimension_semantics=("arbitrary",) * n_grid, vmem_limit_bytes=VMEM_LIMIT)


def _dot(a, b, dims):
    return lax.dot_general(a, b, dims, preferred_element_type=F32)


def _mm(name, dims, operands, in_specs, out_shape, out_specs, grid, acc_shape, epilogue):
    n_in, n_out, nk = len(operands), len(out_shape), grid[-1]

    def kern(*refs):
        extras, outs = refs[2:n_in], refs[n_in:n_in + n_out]
        if nk == 1:
            epilogue(_dot(refs[0][...].astype(BF16), refs[1][...].astype(BF16), dims), extras, outs)
            return
        acc = refs[n_in + n_out]
        k = pl.program_id(len(grid) - 1)

        @pl.when(k == 0)
        def _():
            acc[...] = jnp.zeros_like(acc)

        acc[...] += _dot(refs[0][...].astype(BF16), refs[1][...].astype(BF16), dims)

        @pl.when(k == nk - 1)
        def _():
            epilogue(acc[...], extras, outs)

    return pl.pallas_call(
        kern, name=name, grid=grid, in_specs=in_specs, out_specs=out_specs, out_shape=out_shape,
        scratch_shapes=[pltpu.VMEM(acc_shape, F32)] if nk > 1 else [],
        compiler_params=_params(len(grid)),
    )(*operands)


def _store(scale=None, dtype=None):
    def ep(acc, extras, outs):
        v = acc if scale is None else acc * scale
        outs[0][...] = v.astype(outs[0].dtype)
    return ep


def _residual(scale):
    def ep(acc, extras, outs):
        outs[0][...] = extras[0][...] + scale * acc
    return ep


def _rmsnorm_fwd(x, g):
    T, D = x.shape
    tr = _tile(T, 512)

    def kern(x_ref, g_ref, o_ref):
        xv = x_ref[...]
        r = lax.rsqrt(jnp.mean(xv * xv, axis=-1, keepdims=True) + RMS_EPS)
        o_ref[...] = (xv * r * g_ref[...]).astype(BF16)

    return pl.pallas_call(
        kern, name="rmsnorm_fwd", grid=(T // tr,),
        in_specs=[pl.BlockSpec((tr, D), lambda i: (i, 0)), pl.BlockSpec((1, D), lambda i: (0, 0))],
        out_specs=pl.BlockSpec((tr, D), lambda i: (i, 0)),
        out_shape=jax.ShapeDtypeStruct((T, D), BF16), compiler_params=_params(1),
    )(x, g)


def _rmsnorm_bwd(x, g, dh, dres):
    T, D = x.shape
    tr = _tile(T, 256)

    def kern(x_ref, g_ref, dh_ref, dres_ref, dx_ref, dg_ref):
        xv, dhv = x_ref[...], dh_ref[...]
        r = lax.rsqrt(jnp.mean(xv * xv, axis=-1, keepdims=True) + RMS_EPS)
        y = xv * r
        dy = dhv * g_ref[...]
        dx_ref[...] = dres_ref[...] + r * (dy - y * jnp.mean(dy * y, axis=-1, keepdims=True))
        part = jnp.sum(dhv * y, axis=0, keepdims=True)

        @pl.when(pl.program_id(0) == 0)
        def _():
            dg_ref[...] = part

        @pl.when(pl.program_id(0) > 0)
        def _():
            dg_ref[...] += part

    row = pl.BlockSpec((tr, D), lambda i: (i, 0))
    vec = pl.BlockSpec((1, D), lambda i: (0, 0))
    return pl.pallas_call(
        kern, name="rmsnorm_bwd", grid=(T // tr,), in_specs=[row, vec, row, row], out_specs=[row, vec],
        out_shape=[jax.ShapeDtypeStruct((T, D), F32), jax.ShapeDtypeStruct((1, D), F32)],
        compiler_params=_params(1),
    )(x, g, dh, dres)


def _final_loss(x, g, target):
    T, D = x.shape
    tr = _tile(T, 256)

    def kern(x_ref, g_ref, t_ref, dx_ref, dg_ref, loss_ref):
        xv = x_ref[...]
        r = lax.rsqrt(jnp.mean(xv * xv, axis=-1, keepdims=True) + RMS_EPS)
        y = xv * r
        err = y * g_ref[...] - t_ref[...]
        lpart = 0.5 * jnp.sum(jnp.mean(err * err, axis=-1, keepdims=True), axis=0, keepdims=True)
        dh = err * (1.0 / D)
        dy = dh * g_ref[...]
        dx_ref[...] = r * (dy - y * jnp.mean(dy * y, axis=-1, keepdims=True))
        part = jnp.sum(dh * y, axis=0, keepdims=True)
        lrow = jnp.broadcast_to(lpart, (1, LANES))

        @pl.when(pl.program_id(0) == 0)
        def _():
            dg_ref[...] = part
            loss_ref[...] = lrow

        @pl.when(pl.program_id(0) > 0)
        def _():
            dg_ref[...] += part
            loss_ref[...] += lrow

    row = pl.BlockSpec((tr, D), lambda i: (i, 0))
    vec = pl.BlockSpec((1, D), lambda i: (0, 0))
    return pl.pallas_call(
        kern, name="final_loss", grid=(T // tr,), in_specs=[row, vec, row],
        out_specs=[row, vec, pl.BlockSpec((1, LANES), lambda i: (0, 0))],
        out_shape=[jax.ShapeDtypeStruct((T, D), F32), jax.ShapeDtypeStruct((1, D), F32),
                   jax.ShapeDtypeStruct((1, LANES), F32)],
        compiler_params=_params(1),
    )(x, g, target)


def _ffn_in(h, w4):
    T, D = h.shape
    w4, l = w4
    Fh = w4.shape[3]
    F = 2 * Fh
    tm = _tile(T, 512)

    def kern(h_ref, wg_ref, wu_ref, jac_ref, act_ref):
        hv = h_ref[...]
        gate = _dot(hv, wg_ref[...], NN)
        up = _dot(hv, wu_ref[...], NN)
        sg = jax.nn.sigmoid(gate)
        silu = gate * sg
        jac_ref[0] = (up * (sg + silu * (1.0 - sg))).astype(BF16)
        jac_ref[1] = silu.astype(BF16)
        act_ref[...] = (silu * up).astype(BF16)

    return pl.pallas_call(
        kern, name="ffn_in", grid=(2, T // tm),
        in_specs=[pl.BlockSpec((tm, D), lambda j, i: (i, 0)),
                  pl.BlockSpec((None, None, D, Fh), lambda j, i: (l, j, 0, 0)),
                  pl.BlockSpec((None, None, D, Fh), lambda j, i: (l, 2 + j, 0, 0))],
        out_specs=[pl.BlockSpec((2, tm, Fh), lambda j, i: (0, i, j)),
                   pl.BlockSpec((tm, Fh), lambda j, i: (i, j))],
        out_shape=[jax.ShapeDtypeStruct((2, T, F), BF16), jax.ShapeDtypeStruct((T, F), BF16)],
        compiler_params=_params(2),
    )(h, w4, w4)


def _resident(shape, index_map):
    return pl.BlockSpec(shape, index_map, pipeline_mode=pl.Buffered(1))


def _token_operand(token):
    return ([], []) if token is None else ([token], [pl.BlockSpec(token.shape, lambda i: (0, 0))])


def _ffn_up(x, g, w4, token=None):
    T, D = x.shape
    w4, l = w4
    Fh = w4.shape[3]
    F = 2 * Fh
    tm = _tile(T, 512)
    tok_ops, tok_specs = _token_operand(token)

    def kern(x_ref, g_ref, w_ref, *rest):
        h_ref, jac_ref, act_ref = rest[len(tok_ops):]
        xv = x_ref[...]
        r = lax.rsqrt(jnp.mean(xv * xv, axis=-1, keepdims=True) + RMS_EPS)
        hv = (xv * r * g_ref[...]).astype(BF16)
        h_ref[...] = hv
        for j in range(2):
            cols = slice(j * Fh, (j + 1) * Fh)
            gate = _dot(hv, w_ref[j], NN)
            up = _dot(hv, w_ref[2 + j], NN)
            sg = jax.nn.sigmoid(gate)
            silu = gate * sg
            jac_ref[0, :, cols] = (up * (sg + silu * (1.0 - sg))).astype(BF16)
            jac_ref[1, :, cols] = silu.astype(BF16)
            act_ref[:, cols] = (silu * up).astype(BF16)

    return pl.pallas_call(
        kern, name="ffn_up", grid=(T // tm,),
        in_specs=[pl.BlockSpec((tm, D), lambda i: (i, 0)), pl.BlockSpec((1, D), lambda i: (0, 0)),
                  _resident((None, 4, D, Fh), lambda i: (l, 0, 0, 0))] + tok_specs,
        out_specs=[pl.BlockSpec((tm, D), lambda i: (i, 0)), pl.BlockSpec((2, tm, F), lambda i: (0, i, 0)),
                   pl.BlockSpec((tm, F), lambda i: (i, 0))],
        out_shape=[jax.ShapeDtypeStruct((T, D), BF16), jax.ShapeDtypeStruct((2, T, F), BF16),
                   jax.ShapeDtypeStruct((T, F), BF16)],
        compiler_params=_params(1),
    )(x, g, w4, *tok_ops)


def _ffn_bwd_main(dres, jac, x, g, w_out, w4, token=None):
    T, D = dres.shape
    w_out, l = w_out
    w4, _ = w4
    F = w_out.shape[1]
    Fh = F // 2
    tm = _tile(T, 256)
    tok_ops, tok_specs = _token_operand(token)

    def kern(d_ref, jac_ref, x_ref, g_ref, wo_ref, wi_ref, *rest):
        dgu_ref, dx_ref, dg_ref = rest[len(tok_ops):]
        dv = d_ref[...]
        d16 = dv.astype(BF16)
        dh = jnp.zeros((tm, D), F32)
        for j in range(2):
            cols = slice(j * Fh, (j + 1) * Fh)
            dact = 0.5 * _dot(d16, wo_ref[cols, :], NT)
            dgate = (dact * jac_ref[0, :, cols].astype(F32)).astype(BF16)
            dup = (dact * jac_ref[1, :, cols].astype(F32)).astype(BF16)
            dgu_ref[0, :, cols] = dgate
            dgu_ref[1, :, cols] = dup
            dh = dh + _dot(dgate, wi_ref[j], NT) + _dot(dup, wi_ref[2 + j], NT)
        xv = x_ref[...]
        r = lax.rsqrt(jnp.mean(xv * xv, axis=-1, keepdims=True) + RMS_EPS)
        y = xv * r
        dy = dh * g_ref[...]
        dx_ref[...] = dv + r * (dy - y * jnp.mean(dy * y, axis=-1, keepdims=True))
        part = jnp.sum(dh * y, axis=0, keepdims=True)

        @pl.when(pl.program_id(0) == 0)
        def _():
            dg_ref[...] = part

        @pl.when(pl.program_id(0) > 0)
        def _():
            dg_ref[...] += part

    row = pl.BlockSpec((tm, D), lambda i: (i, 0))
    vec = pl.BlockSpec((1, D), lambda i: (0, 0))
    wide = pl.BlockSpec((2, tm, F), lambda i: (0, i, 0))
    return pl.pallas_call(
        kern, name="ffn_bwd_main", grid=(T // tm,),
        in_specs=[row, wide, row, vec, _resident((None, F, D), lambda i: (l, 0, 0)),
                  _resident((None, 4, D, Fh), lambda i: (l, 0, 0, 0))] + tok_specs,
        out_specs=[wide, row, vec],
        out_shape=[jax.ShapeDtypeStruct((2, T, F), BF16), jax.ShapeDtypeStruct((T, D), F32),
                   jax.ShapeDtypeStruct((1, D), F32)],
        compiler_params=_params(1),
    )(dres, jac, x, g, w_out, w4, *tok_ops)


def _ffn_out(act, w_out, x):
    T, F = act.shape
    w_out, l = w_out
    D = w_out.shape[2]
    tm = _tile(T, 512)
    return _mm("ffn_out", NN, [act, w_out, x],
               [pl.BlockSpec((tm, F), lambda i, k: (i, 0)), pl.BlockSpec((None, F, D), lambda i, k: (l, 0, 0)),
                pl.BlockSpec((tm, D), lambda i, k: (i, 0))],
               [jax.ShapeDtypeStruct((T, D), F32)], [pl.BlockSpec((tm, D), lambda i, k: (i, 0))],
               (T // tm, 1), None, _residual(0.5))


def _ffn_bwd_act(dres, w_out, jac):
    T, D = dres.shape
    w_out, l = w_out
    F = w_out.shape[1]
    Fh = F // 2
    tm = _tile(T, 512)

    def kern(d_ref, w_ref, jac_ref, o_ref):
        dact = 0.5 * _dot(d_ref[...].astype(BF16), w_ref[...], NT)
        o_ref[0] = (dact * jac_ref[0].astype(F32)).astype(BF16)
        o_ref[1] = (dact * jac_ref[1].astype(F32)).astype(BF16)

    return pl.pallas_call(
        kern, name="ffn_bwd_act", grid=(2, T // tm),
        in_specs=[pl.BlockSpec((tm, D), lambda j, i: (i, 0)), pl.BlockSpec((None, Fh, D), lambda j, i: (l, j, 0)),
                  pl.BlockSpec((2, tm, Fh), lambda j, i: (0, i, j))],
        out_specs=pl.BlockSpec((2, tm, Fh), lambda j, i: (0, i, j)),
        out_shape=jax.ShapeDtypeStruct((2, T, F), BF16), compiler_params=_params(2),
    )(dres, w_out, jac)


def _ffn_dw_out(act, dres):
    T, F = act.shape
    D = dres.shape[1]
    tm, tk = F // 2, _tile(T, 1024)
    return _mm("ffn_dw_out", TN, [act, dres],
               [pl.BlockSpec((tk, tm), lambda i, k: (k, i)), pl.BlockSpec((tk, D), lambda i, k: (k, 0))],
               [jax.ShapeDtypeStruct((F, D), BF16)], [pl.BlockSpec((tm, D), lambda i, k: (i, 0))],
               (2, T // tk), (tm, D), _store(0.5))


def _ffn_dw_in(h, dgu):
    T, D = h.shape
    Fh = dgu.shape[2] // 2
    tk = _tile(T, 1024)
    return _mm("ffn_dw_in", TN, [h, dgu],
               [pl.BlockSpec((tk, D), lambda j, k: (k, 0)),
                pl.BlockSpec((None, tk, Fh), lambda j, k: (j // 2, k, j % 2))],
               [jax.ShapeDtypeStruct((4, D, Fh), BF16)], [pl.BlockSpec((None, D, Fh), lambda j, k: (j, 0, 0))],
               (4, T // tk), (D, Fh), _store())


def _ffn_dh(dgu, w4):
    T = dgu.shape[1]
    w4, l = w4
    D, Fh = w4.shape[2], w4.shape[3]
    tm = _tile(T, 1024)
    return _mm("ffn_dh", NT, [dgu, w4],
               [pl.BlockSpec((None, tm, Fh), lambda i, k: (k // 2, i, k % 2)),
                pl.BlockSpec((None, None, D, Fh), lambda i, k: (l, k, 0, 0))],
               [jax.ShapeDtypeStruct((T, D), F32)], [pl.BlockSpec((tm, D), lambda i, k: (i, 0))],
               (T // tm, 4), (tm, D), _store())


def _proj(name, a, w, out_dtype, dims=NN, extra=None, scale=None):
    T, K = a.shape
    w, l = w
    N = w.shape[2] if dims == NN else w.shape[1]
    tm = _tile(T, 512)
    ops = [a, w] + ([extra] if extra is not None else [])
    specs = [pl.BlockSpec((tm, K), lambda i, k: (i, 0)), pl.BlockSpec((None,) + w.shape[1:], lambda i, k: (l, 0, 0))]
    if extra is not None:
        specs.append(pl.BlockSpec((tm, N), lambda i, k: (i, 0)))
    ep = _residual(1.0) if extra is not None else _store(scale)
    return _mm(name, dims, ops, specs, [jax.ShapeDtypeStruct((T, N), out_dtype)],
               [pl.BlockSpec((tm, N), lambda i, k: (i, 0))], (T // tm, 1), None, ep)[0]


def _mix_up(x, g, wp, widths):
    T, D = x.shape
    wp, l = wp
    n_qkv, n_rest = widths
    NP = wp.shape[2]
    tm = _tile(T, 512)

    def kern(x_ref, g_ref, w_ref, h_ref, qkv_ref, rest_ref, fl_ref):
        xv = x_ref[...]
        r = lax.rsqrt(jnp.mean(xv * xv, axis=-1, keepdims=True) + RMS_EPS)
        hv = (xv * r * g_ref[...]).astype(BF16)
        h_ref[...] = hv
        qkv_ref[...] = _dot(hv, w_ref[:, 0:n_qkv], NN).astype(BF16)
        rest_ref[...] = _dot(hv, w_ref[:, n_qkv:n_qkv + n_rest], NN)
        fl_ref[...] = _dot(hv, w_ref[:, n_qkv + n_rest:NP], NN)

    row = lambda n: pl.BlockSpec((tm, n), lambda i: (i, 0))
    return pl.pallas_call(
        kern, name="mix_up", grid=(T // tm,),
        in_specs=[row(D), pl.BlockSpec((1, D), lambda i: (0, 0)), _resident((None, D, NP), lambda i: (l, 0, 0))],
        out_specs=[row(D), row(n_qkv), row(n_rest), row(LANES)],
        out_shape=[jax.ShapeDtypeStruct((T, D), BF16), jax.ShapeDtypeStruct((T, n_qkv), BF16),
                   jax.ShapeDtypeStruct((T, n_rest), F32), jax.ShapeDtypeStruct((T, LANES), F32)],
        compiler_params=_params(1),
    )(x, g, wp)


def _mix_in_bwd(dproj, x, g, dres, wp):
    T, D = x.shape
    wp, l = wp
    NP = wp.shape[2]
    tm = _tile(T, 512)

    def kern(dp_ref, x_ref, g_ref, d_ref, w_ref, dx_ref, dg_ref):
        dh = _dot(dp_ref[...], w_ref[...], NT)
        xv = x_ref[...]
        r = lax.rsqrt(jnp.mean(xv * xv, axis=-1, keepdims=True) + RMS_EPS)
        y = xv * r
        dy = dh * g_ref[...]
        dx_ref[...] = d_ref[...] + r * (dy - y * jnp.mean(dy * y, axis=-1, keepdims=True))
        part = jnp.sum(dh * y, axis=0, keepdims=True)

        @pl.when(pl.program_id(0) == 0)
        def _():
            dg_ref[...] = part

        @pl.when(pl.program_id(0) > 0)
        def _():
            dg_ref[...] += part

    row = lambda n: pl.BlockSpec((tm, n), lambda i: (i, 0))
    vec = pl.BlockSpec((1, D), lambda i: (0, 0))
    return pl.pallas_call(
        kern, name="mix_in_bwd", grid=(T // tm,),
        in_specs=[row(NP), row(D), vec, row(D), _resident((None, D, NP), lambda i: (l, 0, 0))],
        out_specs=[row(D), vec],
        out_shape=[jax.ShapeDtypeStruct((T, D), F32), jax.ShapeDtypeStruct((1, D), F32)],
        compiler_params=_params(1),
    )(dproj, x, g, dres, wp)


def _dw(name, a, d, out_dtype):
    T, M = a.shape
    N = d.shape[1]
    tk = _tile(T, 1024 if M * N <= 1024 * 1408 else 512)
    return _mm(name, TN, [a, d],
               [pl.BlockSpec((tk, M), lambda i, k: (k, 0)), pl.BlockSpec((tk, N), lambda i, k: (k, 0))],
               [jax.ShapeDtypeStruct((M, N), out_dtype)], [pl.BlockSpec((M, N), lambda i, k: (0, 0))],
               (1, T // tk), (M, N), _store())[0]


def _log_sigmoid(z):
    return jnp.minimum(z, 0.0) - jnp.log(1.0 + jnp.exp(-jnp.abs(z)))


def _decay_fwd(fl, bias):
    B, S, _ = fl.shape

    def kern(fl_ref, b_ref, o_ref):
        d = _log_sigmoid(fl_ref[...] + b_ref[...])
        row = lax.broadcasted_iota(jnp.int32, (S, LANES), 0)
        sh = 1
        while sh < S:
            d = d + jnp.where(row >= sh, pltpu.roll(d, sh, 0), 0.0)
            sh *= 2
        o_ref[...] = d.T[0:8, :]

    return pl.pallas_call(
        kern, name="decay_fwd", grid=(B,),
        in_specs=[pl.BlockSpec((None, S, LANES), lambda b: (b, 0, 0)), pl.BlockSpec((1, LANES), lambda b: (0, 0))],
        out_specs=pl.BlockSpec((None, 8, S), lambda b: (b, 0, 0)),
        out_shape=jax.ShapeDtypeStruct((B, 8, S), F32), compiler_params=_params(1),
    )(fl, bias)


def _decay_bwd(ddrow, ddcol, fl, bias, n_heads):
    B, S, _ = fl.shape

    def kern(dd_ref, ddc_ref, fl_ref, b_ref, dfl_ref, db_ref):
        dd = jnp.concatenate([dd_ref[...], jnp.zeros((LANES - 8, S), F32)], axis=0).T + ddc_ref[...]
        row = lax.broadcasted_iota(jnp.int32, (S, LANES), 0)
        lane = lax.broadcasted_iota(jnp.int32, (S, LANES), 1)
        sh = 1
        while sh < S:
            dd = dd + jnp.where(row < S - sh, pltpu.roll(dd, S - sh, 0), 0.0)
            sh *= 2
        z = fl_ref[...] + b_ref[...]
        dfl = jnp.where(lane < n_heads, dd / (1.0 + jnp.exp(z)), 0.0)
        dfl_ref[...] = dfl
        part = jnp.sum(dfl, axis=0, keepdims=True)

        @pl.when(pl.program_id(0) == 0)
        def _():
            db_ref[...] = part

        @pl.when(pl.program_id(0) > 0)
        def _():
            db_ref[...] += part

    return pl.pallas_call(
        kern, name="decay_bwd", grid=(B,),
        in_specs=[pl.BlockSpec((None, 8, S), lambda b: (b, 0, 0)), pl.BlockSpec((None, S, LANES), lambda b: (b, 0, 0)),
                  pl.BlockSpec((None, S, LANES), lambda b: (b, 0, 0)), pl.BlockSpec((1, LANES), lambda b: (0, 0))],
        out_specs=[pl.BlockSpec((None, S, LANES), lambda b: (b, 0, 0)), pl.BlockSpec((1, LANES), lambda b: (0, 0))],
        out_shape=[jax.ShapeDtypeStruct((B, S, LANES), F32), jax.ShapeDtypeStruct((1, LANES), F32)],
        compiler_params=_params(1),
    )(ddrow, ddcol, fl, bias)


def _attn_fwd(qkv, drow, n_heads, tq):
    B, S, _ = qkv.shape
    DA = n_heads * HEAD_DIM
    scale = HEAD_DIM ** -0.5

    n_pairs = n_heads // 2

    def kern(q_ref, k_ref, v_ref, dr_ref, o_ref, lse_ref):
        i = pl.program_id(1)
        lane = lax.broadcasted_iota(jnp.int32, (tq, LANES), 1)
        low = lane < HEAD_DIM
        causal = lax.broadcasted_iota(jnp.int32, (tq, tq), 1) <= lax.broadcasted_iota(jnp.int32, (tq, tq), 0)
        qms = []
        for p in range(n_pairs):
            q2 = q_ref[:, LANES * p:LANES * (p + 1)] * scale
            qms += [jnp.where(low, q2, jnp.zeros_like(q2)), jnp.where(low, jnp.zeros_like(q2), q2)]

        def step(j, carry, masked):
            ms, ls, accs = carry
            ks = pl.multiple_of(j * tq, tq)
            new_m, new_l, new_acc = [], [], []
            for p in range(n_pairs):
                cols = slice(LANES * p, LANES * (p + 1))
                k2, v2 = k_ref[pl.ds(ks, tq), cols], v_ref[pl.ds(ks, tq), cols]
                alphas, pvs = [], []
                for h in (2 * p, 2 * p + 1):
                    s = _dot(qms[h], k2, NT) - dr_ref[h, pl.ds(j, 1), :]
                    if masked:
                        s = jnp.where(causal, s, -jnp.inf)
                    m_new = jnp.maximum(ms[h], jnp.max(s, axis=1, keepdims=True))
                    alpha = jnp.exp(ms[h] - m_new)
                    pm = jnp.exp(s - m_new)
                    new_m.append(m_new)
                    new_l.append(alpha * ls[h] + jnp.sum(pm, axis=1, keepdims=True))
                    alphas.append(alpha)
                    pvs.append(_dot(pm.astype(BF16), v2, NN))
                new_acc.append(jnp.where(low, alphas[0], alphas[1]) * accs[p] + jnp.where(low, pvs[0], pvs[1]))
            return tuple(new_m), tuple(new_l), tuple(new_acc)

        init = (tuple(jnp.full((tq, 1), -jnp.inf, F32) for _ in range(n_heads)),
                tuple(jnp.zeros((tq, 1), F32) for _ in range(n_heads)),
                tuple(jnp.zeros((tq, LANES), F32) for _ in range(n_pairs)))
        ms, ls, accs = step(i, lax.fori_loop(0, i, functools.partial(step, masked=False), init), True)
        lse_mat = jnp.zeros((tq, LANES), F32)
        for p in range(n_pairs):
            l0, l1 = ls[2 * p], ls[2 * p + 1]
            o_ref[:, LANES * p:LANES * (p + 1)] = (accs[p] / jnp.where(low, l0, l1)).astype(BF16)
            lse_mat = jnp.where(lane == 2 * p, ms[2 * p] + jnp.log(l0), lse_mat)
            lse_mat = jnp.where(lane == 2 * p + 1, ms[2 * p + 1] + jnp.log(l1), lse_mat)
        lse_ref[...] = lse_mat

    nq = S // tq
    return pl.pallas_call(
        kern, name="attn_fwd", grid=(B, nq),
        in_specs=[pl.BlockSpec((None, tq, DA), lambda b, i: (b, i, 0)),
                  pl.BlockSpec((None, S, DA), lambda b, i: (b, 0, 1)),
                  pl.BlockSpec((None, S, DA), lambda b, i: (b, 0, 2)),
                  pl.BlockSpec((None, 8, nq, tq), lambda b, i: (b, 0, 0, 0))],
        out_specs=[pl.BlockSpec((None, tq, DA), lambda b, i: (b, i, 0)),
                   pl.BlockSpec((None, tq, LANES), lambda b, i: (b, i, 0))],
        out_shape=[jax.ShapeDtypeStruct((B, S, DA), BF16), jax.ShapeDtypeStruct((B, S, LANES), F32)],
        compiler_params=_params(2),
    )(qkv, qkv, qkv, drow)


def _attn_bwd(qkv, drow, o, lse, dycat, n_heads, tq):
    B, S, _ = qkv.shape
    DA = n_heads * HEAD_DIM
    scale = HEAD_DIM ** -0.5
    nq = S // tq

    n_pairs = n_heads // 2

    def kern(q_ref, k_ref, v_ref, dr_ref, o_ref, lse_ref, do_ref, dq_ref, dk_ref, dv_ref, ddr_ref, ddc_ref,
             dk_acc, dv_acc, qm_s, dom_s, delta_s, rs_s, dq_s):
        i = pl.program_id(1)

        @pl.when(i == 0)
        def _():
            dk_acc[...] = jnp.zeros_like(dk_acc)
            dv_acc[...] = jnp.zeros_like(dv_acc)
            ddr_ref[...] = jnp.zeros_like(ddr_ref)

        lane = lax.broadcasted_iota(jnp.int32, (tq, LANES), 1)
        low = lane < HEAD_DIM
        causal = lax.broadcasted_iota(jnp.int32, (tq, tq), 1) <= lax.broadcasted_iota(jnp.int32, (tq, tq), 0)
        for p in range(n_pairs):
            cols = slice(LANES * p, LANES * (p + 1))
            q2 = q_ref[:, cols] * scale
            do_f = do_ref[:, cols]
            do2 = do_f.astype(BF16)
            prod = do_f * o_ref[:, cols].astype(F32)
            qm_s[2 * p] = jnp.where(low, q2, jnp.zeros_like(q2))
            qm_s[2 * p + 1] = jnp.where(low, jnp.zeros_like(q2), q2)
            dom_s[2 * p] = jnp.where(low, do2, jnp.zeros_like(do2))
            dom_s[2 * p + 1] = jnp.where(low, jnp.zeros_like(do2), do2)
            delta_s[2 * p] = jnp.sum(jnp.where(low, prod, 0.0), axis=1, keepdims=True)
            delta_s[2 * p + 1] = jnp.sum(jnp.where(low, 0.0, prod), axis=1, keepdims=True)
            dq_s[p] = jnp.zeros((tq, LANES), F32)
        rs_s[...] = jnp.zeros(rs_s.shape, F32)

        def step(j, masked):
            ks = pl.multiple_of(j * tq, tq)
            for p in range(n_pairs):
                cols = slice(LANES * p, LANES * (p + 1))
                k2, v2 = k_ref[pl.ds(ks, tq), cols], v_ref[pl.ds(ks, tq), cols]
                dvs, dks, dqs = [], [], []
                for h in (2 * p, 2 * p + 1):
                    qm, dom = qm_s[h], dom_s[h]
                    s = _dot(qm, k2, NT) - dr_ref[h, pl.ds(j, 1), :]
                    if masked:
                        s = jnp.where(causal, s, -jnp.inf)
                    pm = jnp.exp(s - lse_ref[:, h:h + 1])
                    ds = pm * (_dot(dom, v2, NT) - delta_s[h])
                    ddr_ref[h, pl.ds(j, 1), :] -= jnp.sum(ds, axis=0, keepdims=True)
                    rs_s[h] += jnp.sum(ds, axis=1, keepdims=True)
                    dsb = ds.astype(BF16)
                    dvs.append(_dot(pm.astype(BF16), dom, TN))
                    dks.append(_dot(dsb, qm, TN))
                    dqs.append(_dot(dsb, k2, NN))
                dv_acc[pl.ds(ks, tq), cols] += dvs[0] + dvs[1]
                dk_acc[pl.ds(ks, tq), cols] += dks[0] + dks[1]
                dq_s[p] += jnp.where(low, dqs[0], dqs[1])

        def body(j, carry):
            step(j, False)
            return carry

        lax.fori_loop(0, i, body, 0)
        step(i, True)
        ddc = jnp.zeros((tq, LANES), F32)
        for p in range(n_pairs):
            dq_ref[:, LANES * p:LANES * (p + 1)] = (dq_s[p] * scale).astype(BF16)
            ddc = jnp.where(lane == 2 * p, rs_s[2 * p], ddc)
            ddc = jnp.where(lane == 2 * p + 1, rs_s[2 * p + 1], ddc)
        ddc_ref[...] = ddc

        @pl.when(i == nq - 1)
        def _():
            dk_ref[...] = dk_acc[...].astype(BF16)
            dv_ref[...] = dv_acc[...].astype(BF16)

    tile = pl.BlockSpec((None, tq, DA), lambda b, i: (b, i, 0))
    seq = pl.BlockSpec((None, S, DA), lambda b, i: (b, 0, 0))
    dec = pl.BlockSpec((None, 8, nq, tq), lambda b, i: (b, 0, 0, 0))
    return pl.pallas_call(
        kern, name="attn_bwd", grid=(B, nq),
        in_specs=[tile, pl.BlockSpec((None, S, DA), lambda b, i: (b, 0, 1)),
                  pl.BlockSpec((None, S, DA), lambda b, i: (b, 0, 2)), dec, tile,
                  pl.BlockSpec((None, tq, LANES), lambda b, i: (b, i, 0)), tile],
        out_specs=[tile, seq, seq, dec, pl.BlockSpec((None, tq, LANES), lambda b, i: (b, i, 0))],
        out_shape=[jax.ShapeDtypeStruct((B, S, DA), BF16)] * 3 + [jax.ShapeDtypeStruct((B, 8, nq, tq), F32),
                                                                  jax.ShapeDtypeStruct((B, S, LANES), F32)],
        scratch_shapes=[pltpu.VMEM((S, DA), F32), pltpu.VMEM((S, DA), F32),
                        pltpu.VMEM((n_heads, tq, LANES), BF16), pltpu.VMEM((n_heads, tq, LANES), BF16),
                        pltpu.VMEM((n_heads, tq, 1), F32), pltpu.VMEM((n_heads, tq, 1), F32),
                        pltpu.VMEM((n_pairs, tq, LANES), F32)],
        compiler_params=_params(2),
    )(qkv, qkv, qkv, drow, o, lse, dycat)


def _down(v, d, row):
    return jnp.where(row >= d, pltpu.roll(v, d, 0), 0.0)


def _up(v, d, row, S):
    return jnp.where(row < S - d, pltpu.roll(v, S - d, 0), 0.0)


def _window(v, shift, group):
    sums, acc, d = [], v, 1
    for _ in POOL_WINDOWS:
        acc = acc + shift(acc, d)
        sums.append(acc)
        d *= 2
    out = sums[-1]
    for gi in range(len(POOL_WINDOWS) - 2, -1, -1):
        out = jnp.where(group == gi, sums[gi], out)
    return out


def _pool_count(row, group):
    w = jnp.full(row.shape, POOL_WINDOWS[-1], jnp.int32)
    for gi in range(len(POOL_WINDOWS) - 2, -1, -1):
        w = jnp.where(group == gi, POOL_WINDOWS[gi], w)
    return jnp.minimum(row + 1, w).astype(F32)


def _mix_local_fwd(rest, wbd, ps, cw):
    B, S, C4 = rest.shape
    C = C4 // 4
    gw = C // len(POOL_WINDOWS)

    def kern(r_ref, w_ref, ps_ref, cw_ref, y_ref, pooled_ref):
        row = lax.broadcasted_iota(jnp.int32, (S, C), 0)
        group = lax.broadcasted_iota(jnp.int32, (S, C), 1) // gw
        u = r_ref[:, 0:C]
        pooled = _window(u, lambda v, d: _down(v, d, row), group) / _pool_count(row, group) - u
        pb = pooled.astype(BF16)
        pooled_ref[...] = pb
        y_ref[:, 0:C] = (_dot(pb, w_ref[...], NN) * ps_ref[...]).astype(BF16)
        uc = r_ref[:, 2 * C:3 * C] * r_ref[:, 3 * C:4 * C]
        y = cw_ref[0:1, :] * _down(uc, 2, row) + cw_ref[1:2, :] * _down(uc, 1, row) + cw_ref[2:3, :] * uc
        y_ref[:, C:2 * C] = (r_ref[:, C:2 * C] * y).astype(BF16)

    return pl.pallas_call(
        kern, name="mix_local_fwd", grid=(B,),
        in_specs=[pl.BlockSpec((None, S, C4), lambda b: (b, 0, 0)), pl.BlockSpec((C, C), lambda b: (0, 0)),
                  pl.BlockSpec((1, C), lambda b: (0, 0)), pl.BlockSpec((8, C), lambda b: (0, 0))],
        out_specs=[pl.BlockSpec((None, S, 2 * C), lambda b: (b, 0, 0)), pl.BlockSpec((None, S, C), lambda b: (b, 0, 0))],
        out_shape=[jax.ShapeDtypeStruct((B, S, 2 * C), BF16), jax.ShapeDtypeStruct((B, S, C), BF16)],
        compiler_params=_params(1),
    )(rest, wbd, ps, cw)


def _mix_local_bwd(rest, pooled, dycat, wbd, ps, cw):
    B, S, C4 = rest.shape
    C = C4 // 4
    gw = C // len(POOL_WINDOWS)

    def kern(r_ref, pooled_ref, d_ref, w_ref, ps_ref, cw_ref, dr_ref, dw_ref, dps_ref, dcw_ref):
        row = lax.broadcasted_iota(jnp.int32, (S, C), 0)
        group = lax.broadcasted_iota(jnp.int32, (S, C), 1) // gw
        dyp = d_ref[:, 0:C]
        dyc = d_ref[:, C:2 * C]
        pb = pooled_ref[...]
        dps = jnp.sum(dyp * _dot(pb, w_ref[...], NN), axis=0, keepdims=True)
        dzb = (dyp * ps_ref[...]).astype(BF16)
        dw = _dot(pb, dzb, TN)
        dpooled = _dot(dzb, w_ref[...], NT)
        g = dpooled / _pool_count(row, group)
        dr_ref[:, 0:C] = (_window(g, lambda v, d: _up(v, d, row, S), group) - dpooled).astype(BF16)
        cc, ch = r_ref[:, 2 * C:3 * C], r_ref[:, 3 * C:4 * C]
        uc = cc * ch
        u1, u2 = _down(uc, 1, row), _down(uc, 2, row)
        y = cw_ref[0:1, :] * u2 + cw_ref[1:2, :] * u1 + cw_ref[2:3, :] * uc
        dr_ref[:, C:2 * C] = (dyc * y).astype(BF16)
        dy = dyc * r_ref[:, C:2 * C]
        duc = cw_ref[0:1, :] * _up(dy, 2, row, S) + cw_ref[1:2, :] * _up(dy, 1, row, S) + cw_ref[2:3, :] * dy
        dr_ref[:, 2 * C:3 * C] = (duc * ch).astype(BF16)
        dr_ref[:, 3 * C:4 * C] = (duc * cc).astype(BF16)
        dcw = jnp.concatenate([jnp.sum(dy * u2, axis=0, keepdims=True), jnp.sum(dy * u1, axis=0, keepdims=True),
                               jnp.sum(dy * uc, axis=0, keepdims=True), jnp.zeros((5, C), F32)], axis=0)

        @pl.when(pl.program_id(0) == 0)
        def _():
            dw_ref[...] = dw
            dps_ref[...] = dps
            dcw_ref[...] = dcw

        @pl.when(pl.program_id(0) > 0)
        def _():
            dw_ref[...] += dw
            dps_ref[...] += dps
            dcw_ref[...] += dcw

    full = lambda shape: pl.BlockSpec(shape, lambda b: (0, 0))
    return pl.pallas_call(
        kern, name="mix_local_bwd", grid=(B,),
        in_specs=[pl.BlockSpec((None, S, C4), lambda b: (b, 0, 0)), pl.BlockSpec((None, S, C), lambda b: (b, 0, 0)),
                  pl.BlockSpec((None, S, 2 * C), lambda b: (b, 0, 1)), full((C, C)), full((1, C)), full((8, C))],
        out_specs=[pl.BlockSpec((None, S, C4), lambda b: (b, 0, 0)), full((C, C)), full((1, C)), full((8, C))],
        out_shape=[jax.ShapeDtypeStruct((B, S, C4), BF16), jax.ShapeDtypeStruct((C, C), F32),
                   jax.ShapeDtypeStruct((1, C), F32), jax.ShapeDtypeStruct((8, C), F32)],
        compiler_params=_params(1),
    )(rest, pooled, dycat, wbd, ps, cw)


def _adamw(w, gs, m, v):
    R, C = w.shape
    n = len(gs)
    rows = R // n
    tr = _tile(rows, 256)
    per = rows // tr

    def kern(w_ref, *refs):
        g_refs, (m_ref, v_ref, g_out, d_ref, nm_ref, nv_ref) = refs[:n], refs[n:]
        gv = g_refs[0][...]
        for s in range(1, n):
            gv = jnp.where(pl.program_id(0) // per == s, g_refs[s][...], gv)
        nm = ADAM_B1 * m_ref[...] + (1.0 - ADAM_B1) * gv
        nv = ADAM_B2 * v_ref[...] + (1.0 - ADAM_B2) * (gv * gv)
        m_hat = nm / (1.0 - ADAM_B1 ** ADAM_STEP)
        v_hat = nv / (1.0 - ADAM_B2 ** ADAM_STEP)
        g_out[...] = gv
        d_ref[...] = -ADAM_LR * (m_hat / (jnp.sqrt(v_hat) + ADAM_EPS) + ADAM_WD * w_ref[...])
        nm_ref[...] = nm
        nv_ref[...] = nv

    def piece(s):
        return pl.BlockSpec((tr, C), lambda i: (jnp.clip(i - s * per, 0, per - 1), 0))

    blk = pl.BlockSpec((tr, C), lambda i: (i, 0))
    return pl.pallas_call(
        kern, name="adamw", grid=(R // tr,), in_specs=[blk] + [piece(s) for s in range(n)] + [blk] * 2,
        out_specs=[blk] * 4, out_shape=[jax.ShapeDtypeStruct((R, C), F32)] * 4, compiler_params=_params(1),
    )(w, *gs, m, v)


def _place():
    x, y, c = lax.axis_index("x"), lax.axis_index("y"), lax.axis_index("c")
    return x, y, c, [(1 - x, y), (x, 1 - y), (1 - x, 1 - y)]


def _comm_call(name, body, operands, out_shape, n_sems, aliases=None):
    any_spec = pl.BlockSpec(memory_space=pl.ANY)
    return pl.pallas_call(
        body, name=name, in_specs=[any_spec] * len(operands), out_specs=[any_spec] * len(out_shape),
        out_shape=out_shape, input_output_aliases=aliases or {},
        scratch_shapes=[pltpu.SemaphoreType.DMA((n,)) for n in n_sems],
    )(*operands)


def _my_block():
    return 2 * lax.axis_index("x") + lax.axis_index("y")


def _place_shard(w, dtype, first=0, count=None):
    L, R, C = w.shape
    count = L if count is None else count
    tr = _tile(R, 512)

    def kern(w_ref, o_ref):
        o_ref[...] = w_ref[...].astype(dtype)

    return pl.pallas_call(
        kern, name="place_shard", grid=(count, R // tr),
        in_specs=[pl.BlockSpec((None, tr, C), lambda l, i: (first + l, i, 0))],
        out_specs=pl.BlockSpec((None, None, tr, C), lambda l, i: (l, _my_block(), i, 0)),
        out_shape=jax.ShapeDtypeStruct((count, N_CHIPS, R, C), dtype), compiler_params=_params(2),
    )(w)


HALF_ROWS = 16


def _rows(ref, half):
    hr = ref.shape[-2] // 2
    return ref.at[(slice(None),) * (len(ref.shape) - 2) + (pl.ds(half * hr, hr),)]


def _all_gather(bufs):
    n = len(bufs)

    def body(*refs):
        outs = refs[n:2 * n]
        send_sems, recv_sems = refs[2 * n:]
        x, y, c, chips = _place()
        sibling = (x, y, 1 - c)

        def remote(k, j, chip, half, to):
            blk = 2 * chip[0] + chip[1]
            if outs[k].shape[2] % (2 * HALF_ROWS) == 0:
                region = _rows(outs[k].at[:, blk], half)
            else:
                hl = outs[k].shape[0] // 2
                region = outs[k].at[pl.ds(half * hl, hl), blk]
            return pltpu.make_async_remote_copy(
                src_ref=region, dst_ref=region, send_sem=send_sems.at[6 * k + j],
                recv_sem=recv_sems.at[6 * k + j], device_id=to, device_id_type=MESH)

        first = [remote(k, j, (x, y), c, (*chip, c)) for k in range(n) for j, chip in enumerate(chips)]
        for cp in first:
            cp.start()
        passed = []
        for k in range(n):
            for j, chip in enumerate(chips):
                remote(k, j, chip, c, (x, y, c)).wait_recv()
                passed.append(remote(k, 3 + j, chip, c, sibling))
                passed[-1].start()
        for k in range(n):
            for j, chip in enumerate(chips):
                remote(k, 3 + j, chip, 1 - c, (x, y, c)).wait_recv()
        for cp in first + passed:
            cp.wait_send()

    out_shape = [jax.ShapeDtypeStruct(s.shape, s.dtype) for s in bufs]
    return _comm_call("all_gather_weights", body, bufs, out_shape, (6 * n, 6 * n), aliases={k: k for k in range(n)})


_HBM = pl.BlockSpec(memory_space=pltpu.HBM)
_SEM = pl.BlockSpec(memory_space=pltpu.SEMAPHORE)
_ANY = pl.BlockSpec(memory_space=pl.ANY)


def _split_start(name, bufs, n_copies, make_copies, after):
    n = len(bufs)

    def body(*refs):
        send_sems, recv_sems, token = refs[n + 1], refs[n + 2], refs[2 * n + 3]
        for cp in make_copies(refs[:n], send_sems, recv_sems):
            cp.start()
        token[...] = jnp.zeros_like(token)

    res = pl.pallas_call(
        body, name=name, in_specs=[_HBM] * n + [_ANY],
        out_shape=(pltpu.SemaphoreType.DMA((n_copies,)), pltpu.SemaphoreType.DMA((n_copies,)),
                   *[pltpu.HBM(b.shape, b.dtype) for b in bufs], jax.ShapeDtypeStruct((8, LANES), F32)),
        out_specs=(_SEM, _SEM, *[_HBM] * n, pl.BlockSpec(memory_space=pltpu.VMEM)),
        input_output_aliases={i: 2 + i for i in range(n)},
        compiler_params=pltpu.CompilerParams(has_side_effects=pltpu.SideEffectType.DATAFLOW_SIDE_EFFECTING),
    )(*[pltpu.with_memory_space_constraint(b, pltpu.HBM) for b in bufs], after)
    return res[0], res[1], list(res[2:2 + n]), res[2 + n]


def _split_wait(name, send_sems, recv_sems, bufs, make_copies, after):
    n = len(bufs)

    def body(*refs):
        for cp in make_copies(refs[:n], refs[n], refs[n + 1]):
            cp.wait_send()
            cp.wait_recv()

    return list(pl.pallas_call(
        body, name=name, in_specs=[_HBM] * n + [_SEM, _SEM, _ANY],
        out_shape=tuple(pltpu.HBM(b.shape, b.dtype) for b in bufs), out_specs=tuple([_HBM] * n),
        input_output_aliases={i: i for i in range(n)},
        compiler_params=pltpu.CompilerParams(has_side_effects=pltpu.SideEffectType.DATAFLOW_SIDE_EFFECTING),
    )(*bufs, send_sems, recv_sems, after))


def _gather_copies(refs, send_sems, recv_sems):
    x, y, c, chips = _place()
    return [pltpu.make_async_remote_copy(
        src_ref=ref.at[:, 2 * x + y], dst_ref=ref.at[:, 2 * x + y], send_sem=send_sems.at[3 * k + j],
        recv_sem=recv_sems.at[3 * k + j], device_id=(*chip, c), device_id_type=MESH)
        for k, ref in enumerate(refs) for j, chip in enumerate(chips)]


def _exchange_copies(refs, send_sems, recv_sems):
    n = len(refs) // 2
    x, y, c, chips = _place()
    return [pltpu.make_async_remote_copy(
        src_ref=refs[k].at[:, 2 * chip[0] + chip[1]], dst_ref=refs[n + k].at[j], send_sem=send_sems.at[3 * k + j],
        recv_sem=recv_sems.at[3 * k + j], device_id=(*chip, c), device_id_type=MESH)
        for k in range(n) for j, chip in enumerate(chips)]


def _rs_swap_halves(grads):
    n = len(grads)

    def body(*refs):
        ins, outs = refs[:n], refs[n:2 * n]
        send_sems, recv_sems = refs[2 * n:]
        x, y, c, _ = _place()
        copies = [pltpu.make_async_remote_copy(
            src_ref=_rows(ins[k], 1 - c), dst_ref=outs[k], send_sem=send_sems.at[k],
            recv_sem=recv_sems.at[k], device_id=(x, y, 1 - c), device_id_type=MESH) for k in range(n)]
        for cp in copies:
            cp.start()
        for cp in copies:
            cp.wait()

    out_shape = [jax.ShapeDtypeStruct(g.shape[:2] + (g.shape[2] // 2, g.shape[3]), g.dtype) for g in grads]
    return _comm_call("rs_swap_halves", body, grads, out_shape, (n, n))


def _rs_exchange(parts):
    n = len(parts)

    def body(*refs):
        ins, outs = refs[:n], refs[n:2 * n]
        send_sems, recv_sems = refs[2 * n:]
        x, y, c, chips = _place()
        copies = [pltpu.make_async_remote_copy(
            src_ref=ins[k].at[:, 2 * chip[0] + chip[1]], dst_ref=outs[k].at[j], send_sem=send_sems.at[3 * k + j],
            recv_sem=recv_sems.at[3 * k + j], device_id=(*chip, c), device_id_type=MESH)
            for k in range(n) for j, chip in enumerate(chips)]
        for cp in copies:
            cp.start()
        for cp in copies:
            cp.wait()

    out_shape = [jax.ShapeDtypeStruct((3, p.shape[0]) + p.shape[2:], p.dtype) for p in parts]
    return _comm_call("rs_exchange", body, parts, out_shape, (3 * n, 3 * n))


def _rs_share(bufs):
    n = len(bufs)

    def body(*refs):
        outs = refs[n:2 * n]
        send_sems, recv_sems = refs[2 * n:]
        x, y, c, _ = _place()

        def half(k, which):
            region = _rows(outs[k], which)
            return pltpu.make_async_remote_copy(
                src_ref=region, dst_ref=region, send_sem=send_sems.at[k], recv_sem=recv_sems.at[k],
                device_id=(x, y, 1 - c), device_id_type=MESH)

        sends = [half(k, c) for k in range(n)]
        for cp in sends:
            cp.start()
        for k in range(n):
            half(k, 1 - c).wait_recv()
        for cp in sends:
            cp.wait_send()

    out_shape = [jax.ShapeDtypeStruct(h.shape, h.dtype) for h in bufs]
    return _comm_call("rs_share", body, bufs, out_shape, (n, n), aliases={k: k for k in range(n)})


def _all_reduce_small(v):
    n = v.shape[0]

    def body(v_ref, o_ref, gbuf, send_sems, recv_sems):
        x, y, c, _ = _place()
        me = 4 * x + 2 * y + c
        gbuf[me] = v_ref[...]
        copies, waits = [], []
        for r in range(1, N_DEV):
            px = 1 - x if r & 4 else x
            py = 1 - y if r & 2 else y
            pc = 1 - c if r & 1 else c
            mk = functools.partial(pltpu.make_async_remote_copy, src_ref=v_ref, send_sem=send_sems.at[r - 1],
                                   recv_sem=recv_sems.at[r - 1], device_id=(px, py, pc), device_id_type=MESH)
            copies.append(mk(dst_ref=gbuf.at[me]))
            waits.append(mk(dst_ref=gbuf.at[4 * px + 2 * py + pc]))
        for cp in copies:
            cp.start()
        for cp in waits:
            cp.wait_recv()
        for cp in copies:
            cp.wait_send()
        acc = gbuf[0]
        for d in range(1, N_DEV):
            acc = acc + gbuf[d]
        o_ref[...] = acc

    vm = pl.BlockSpec(memory_space=pltpu.VMEM)
    return pl.pallas_call(
        body, name="all_reduce_small", in_specs=[vm], out_specs=vm, out_shape=jax.ShapeDtypeStruct(v.shape, F32),
        scratch_shapes=[pltpu.VMEM((N_DEV, n, LANES), F32), pltpu.SemaphoreType.DMA((N_DEV - 1,)),
                        pltpu.SemaphoreType.DMA((N_DEV - 1,))],
        compiler_params=pltpu.CompilerParams(vmem_limit_bytes=VMEM_LIMIT),
    )(v)


def _add_half(g, h1):
    L, nb, hr, C = h1.shape
    g3, h3 = g.reshape(L * nb, 2 * hr, C), h1.reshape(L * nb, hr, C)
    tr = _tile(hr, 512)

    def kern(g_ref, h_ref, o_ref):
        o_ref[...] = (g_ref[...].astype(F32) + h_ref[...].astype(F32)).astype(BF16)

    blk = pl.BlockSpec((None, tr, C), lambda l, i: (l, i, 0))
    out = pl.pallas_call(
        kern, name="rs_add_half", grid=(L * nb, hr // tr),
        in_specs=[pl.BlockSpec((None, tr, C), lambda l, i: (l, lax.axis_index("c") * (hr // tr) + i, 0)), blk],
        out_specs=blk, out_shape=jax.ShapeDtypeStruct(h3.shape, BF16), compiler_params=_params(2),
    )(g3, h3)
    return out.reshape(h1.shape)


def _add_blocks(p, h2):
    L, nb, hr, C = p.shape
    tr = _tile(hr, 512)

    def kern(p_ref, h0_ref, h1_ref, h2_ref, o_ref):
        o_ref[...] = ((p_ref[...].astype(F32) + h0_ref[...].astype(F32)) + h1_ref[...].astype(F32)) + h2_ref[...].astype(F32)

    def other(j):
        return pl.BlockSpec((None, None, tr, C), lambda l, i: (j, l, i, 0))

    return pl.pallas_call(
        kern, name="rs_add_blocks", grid=(L, hr // tr),
        in_specs=[pl.BlockSpec((None, None, tr, C), lambda l, i: (l, _my_block(), i, 0)), other(0), other(1), other(2)],
        out_specs=pl.BlockSpec((None, tr, C), lambda l, i: (l, lax.axis_index("c") * (hr // tr) + i, 0)),
        out_shape=jax.ShapeDtypeStruct((L, 2 * hr, C), F32), compiler_params=_params(2),
    )(p, h2, h2, h2)


def _reduce_scatter(grads):
    sib = _rs_swap_halves(grads)
    parts = [_add_half(g, h) for g, h in zip(grads, sib)]
    others = _rs_exchange(parts)
    return _rs_share([_add_blocks(p, o) for p, o in zip(parts, others)])


WEIGHTS = ("norm_ffn1", "w_ffn1_in", "w_ffn1_out", "norm_mix", "w_mix_in", "b_forget", "w_pool", "pool_scale",
           "conv_w", "w_mix_out", "norm_ffn2", "w_ffn2_in", "w_ffn2_out", "norm_final")
BIG = ("w_ffn1_in", "w_ffn1_out", "w_mix_in", "w_mix_out", "w_ffn2_in", "w_ffn2_out")
SMALL = ("norm_ffn1", "norm_mix", "b_forget", "w_pool", "pool_scale", "conv_w", "norm_ffn2", "norm_final")


def _prep_weights(small, gathered, conv_w, D, first):
    DA, C, H = D // 2, D // 4, D // 2 // HEAD_DIM
    L = gathered["w_mix_in"].shape[0]
    small = {k: val[first:first + L] for k, val in small.items() if k != "norm_final"}
    gathered = dict(gathered, conv_w=conv_w[first:first + L])
    w_in = jnp.concatenate([gathered["w_mix_in"][:, b] for b in range(N_CHIPS)], axis=2)
    wqkv, wrest = w_in[:, :, :3 * DA], w_in[:, :, 3 * DA + H:]
    wf = jnp.pad(w_in[:, :, 3 * DA:3 * DA + H], ((0, 0), (0, 0), (0, LANES - H)))
    gw = C // len(POOL_WINDOWS)
    wbd = jnp.zeros((L, C, C), F32)
    for gi in range(len(POOL_WINDOWS)):
        wbd = wbd.at[:, gi * gw:(gi + 1) * gw, gi * gw:(gi + 1) * gw].set(small["w_pool"][:, gi])
    cw = jnp.concatenate([gathered["conv_w"][:, b] for b in range(N_CHIPS)], axis=2)
    return dict(
        g1=small["norm_ffn1"], gm=small["norm_mix"], g2=small["norm_ffn2"],
        w1in=gathered["w_ffn1_in"], w2in=gathered["w_ffn2_in"],
        w1out=gathered["w_ffn1_out"].reshape(L, -1, D), w2out=gathered["w_ffn2_out"].reshape(L, -1, D),
        wp=jnp.concatenate([wqkv, wrest, wf], axis=2), wmixout=gathered["w_mix_out"].reshape(L, D, D),
        bias=jnp.pad(small["b_forget"], ((0, 0), (0, LANES - H))), wbd=wbd.astype(BF16), ps=small["pool_scale"],
        cw=jnp.pad(cw, ((0, 0), (0, 8 - CONV_WIDTH), (0, 0))),
    )


def _layer_params(l, W):
    P = {k: (W[k], l) for k in ("w1in", "w2in", "w1out", "w2out", "wp", "wmixout")}
    P.update({k: W[k][l][None] for k in ("g1", "gm", "g2", "bias", "ps")})
    P.update(wbd=W["wbd"][l], cw=W["cw"][l])
    return P


def _ffn_fwd(x, g, w_in, w_out, token=None):
    h, jac, act = _ffn_up(x, g, w_in, token)
    return _ffn_out(act, w_out, x)[0], (x, h, jac, act)


def _ffn_bwd(dres, saved, g, w_in, w_out, token=None):
    x, h, jac, act = saved
    dgu, dx, dg = _ffn_bwd_main(dres, jac, x, g, w_out, w_in, token)
    dw_out = _ffn_dw_out(act, dres)[0]
    dw_in = _ffn_dw_in(h, dgu)[0]
    return dx, dg, dw_in, dw_out.reshape(N_CHIPS, -1, dw_out.shape[1])


def _mixer_fwd(x, P, B, S, tq):
    T, D = x.shape
    DA, C, H = D // 2, D // 4, D // 2 // HEAD_DIM
    hn, qkv, rest, fl = _mix_up(x, P["gm"], P["wp"], (3 * DA, 4 * C))
    qkv, rest, fl = qkv.reshape(B, S, 3 * DA), rest.reshape(B, S, 4 * C), fl.reshape(B, S, LANES)
    drow = _decay_fwd(fl, P["bias"]).reshape(B, 8, S // tq, tq)
    o, lse = _attn_fwd(qkv, drow, H, tq)
    ypc, pooled = _mix_local_fwd(rest, P["wbd"], P["ps"], P["cw"])
    ycat = jnp.concatenate([o, ypc], axis=-1).reshape(T, D)
    return _proj("mix_out", ycat, P["wmixout"], F32, extra=x), (x, hn, qkv, rest, fl, drow, o, lse, pooled, ycat)


def _mixer_bwd(dres, saved, P, B, S, tq):
    x, hn, qkv, rest, fl, drow, o, lse, pooled, ycat = saved
    T, D = x.shape
    DA, C, H = D // 2, D // 4, D // 2 // HEAD_DIM
    dycat = _proj("mix_out_bwd", dres, P["wmixout"], F32, NT).reshape(B, S, D)
    dw_out = _dw("mix_out_dw", ycat, dres, BF16)
    dq, dk, dv, ddrow, ddcol = _attn_bwd(qkv, drow, o, lse, dycat, H, tq)
    dfl, dbias = _decay_bwd(ddrow.reshape(B, 8, S), ddcol, fl, P["bias"], H)
    drest, dwbd, dps, dcw = _mix_local_bwd(rest, pooled, dycat, P["wbd"], P["ps"], P["cw"])
    dproj = jnp.concatenate([dq, dk, dv, drest, dfl.astype(BF16)], axis=-1).reshape(T, 3 * DA + 4 * C + LANES)
    dwp = _dw("mix_in_dw", hn, dproj, F32)
    dx, dg = _mix_in_bwd(dproj, x, P["gm"], dres, P["wp"])
    n_q, n_r = 3 * DA, 4 * C
    dw_in = jnp.concatenate([dwp[:, :n_q], dwp[:, n_q + n_r:n_q + n_r + H], dwp[:, n_q:n_q + n_r]], axis=1)
    dw_in = dw_in.reshape(D, N_CHIPS, -1).transpose(1, 0, 2).astype(BF16)
    gw = C // len(POOL_WINDOWS)
    dw_pool = jnp.stack([dwbd[gi * gw:(gi + 1) * gw, gi * gw:(gi + 1) * gw] for gi in range(len(POOL_WINDOWS))])
    small = dict(norm_mix=dg[0], b_forget=dbias[0, :H], w_pool=dw_pool, pool_scale=dps[0], conv_w=dcw[:CONV_WIDTH])
    return dx, small, dw_in, dw_out.reshape(N_CHIPS, -1, D)


def _local_step(x, target, small, conv_w, pipe):
    B, S, D = x.shape
    L = small["norm_ffn1"].shape[0]
    tq = _tile(S, 256)
    xt = x.reshape(B * S, D)
    saved, params = [], []
    for l in range(L):
        P = _layer_params(0, _prep_weights(small, pipe.weights(l, xt), conv_w, D, l))
        xt, s1 = _ffn_fwd(xt, P["g1"], P["w1in"], P["w1out"], pipe.token(l))
        xt, s2 = _mixer_fwd(xt, P, B, S, tq)
        xt, s3 = _ffn_fwd(xt, P["g2"], P["w2in"], P["w2out"])
        saved.append((s1, s2, s3))
        params.append(P)
    dres, dgf, loss = _final_loss(xt, small["norm_final"][None], target.reshape(B * S, D))
    sm = {k: [None] * L for k in SMALL if k != "norm_final"}
    token = None
    for l in reversed(range(L)):
        P, (s1, s2, s3) = params[l], saved[l]
        big = {}
        dres, dg2, big["w_ffn2_in"], big["w_ffn2_out"] = _ffn_bwd(dres, s3, P["g2"], P["w2in"], P["w2out"], token)
        dres, smix, big["w_mix_in"], big["w_mix_out"] = _mixer_bwd(dres, s2, P, B, S, tq)
        dres, dg1, big["w_ffn1_in"], big["w_ffn1_out"] = _ffn_bwd(dres, s1, P["g1"], P["w1in"], P["w1out"])
        sm["norm_ffn1"][l], sm["norm_ffn2"][l] = dg1[0], dg2[0]
        for k, val in smix.items():
            sm[k][l] = val
        token = pipe.grads(l, {k: val[None] for k, val in big.items()}, dres)
    sm = {k: jnp.stack(val) for k, val in sm.items()}
    sm["norm_final"] = dgf[0]
    return loss[0, 0], dres.reshape(B, S, D), sm


class _Pipeline:
    def __init__(self, w):
        self.w, self.n_layers = w, w[BIG[0]].shape[0]
        first = _all_gather([_place_shard(w[k], BF16, 0, 1) for k in BIG] + [_place_shard(w["conv_w"], F32)])
        self.conv_w = first[-1]
        self._ready = dict(zip(BIG, first[:-1]))
        self._gather = self._reduce = None
        self.reduced = [None] * self.n_layers
        self._start_gather(1, first[0])

    def _start_gather(self, l, after):
        if l < self.n_layers:
            placed = [_place_shard(self.w[k], BF16, l, 1) for k in BIG]
            self._gather = (l, _split_start(f"gather_start_{l}", placed, 3 * len(BIG), _gather_copies, after))

    def token(self, l):
        return self._gather[1][3] if self._gather is not None and self._gather[0] == l + 1 else None

    def weights(self, l, after):
        if l == 0:
            return self._ready
        (_, (send_sems, recv_sems, bufs, _)), self._gather = self._gather, None
        got = _split_wait(f"gather_wait_{l}", send_sems, recv_sems, bufs, _gather_copies, after)
        self._start_gather(l + 1, got[0])
        return dict(zip(BIG, got))

    def grads(self, l, big, after):
        n, grads = len(BIG), [big[k] for k in BIG]
        if self._reduce is not None:
            above, (send_sems, recv_sems, bufs, _) = self._reduce
            bufs = _split_wait(f"reduce_wait_{above}", send_sems, recv_sems, bufs, _exchange_copies, after)
            self.reduced[above] = _rs_share([_add_blocks(p, o) for p, o in zip(bufs[:n], bufs[n:])])
            self._reduce = None
        if l == 0:
            self.reduced[0] = _reduce_scatter(grads)
            return None
        parts = [_add_half(g, h) for g, h in zip(grads, _rs_swap_halves(grads))]
        lands = [lax.empty((3, p.shape[0]) + p.shape[2:], p.dtype) for p in parts]
        self._reduce = (l, _split_start(f"reduce_start_{l}", parts + lands, 3 * n, _exchange_copies, parts[0]))
        return self._reduce[1][3]


def _pack(parts, extra=()):
    flat = jnp.concatenate([p.reshape(-1) for p in parts] + [jnp.reshape(e, (1,)) for e in extra])
    n = -(-flat.shape[0] // (8 * LANES)) * 8
    return jnp.pad(flat, (0, n * LANES - flat.shape[0])).reshape(n, LANES)


def _unpack(buf, shapes):
    flat, out, at = buf.reshape(-1), [], 0
    for s in shapes:
        n = math.prod(s)
        out.append(flat[at:at + n].reshape(s))
        at += n
    return out, flat[at:]


def kernel(x, norm_ffn1, w_ffn1_in, w_ffn1_out, norm_mix, w_mix_in, b_forget, w_pool, pool_scale, conv_w, w_mix_out, norm_ffn2, w_ffn2_in, w_ffn2_out, norm_final, loss_target, m_norm_ffn1, m_w_ffn1_in, m_w_ffn1_out, m_norm_mix, m_w_mix_in, m_b_forget, m_w_pool, m_pool_scale, m_conv_w, m_w_mix_out, m_norm_ffn2, m_w_ffn2_in, m_w_ffn2_out, m_norm_final, v_norm_ffn1, v_w_ffn1_in, v_w_ffn1_out, v_norm_mix, v_w_mix_in, v_b_forget, v_w_pool, v_pool_scale, v_conv_w, v_w_mix_out, v_norm_ffn2, v_w_ffn2_in, v_w_ffn2_out, v_norm_final):
    w = dict(zip(WEIGHTS, (norm_ffn1, w_ffn1_in, w_ffn1_out, norm_mix, w_mix_in, b_forget, w_pool, pool_scale, conv_w, w_mix_out, norm_ffn2, w_ffn2_in, w_ffn2_out, norm_final)))
    m = dict(zip(WEIGHTS, (m_norm_ffn1, m_w_ffn1_in, m_w_ffn1_out, m_norm_mix, m_w_mix_in, m_b_forget, m_w_pool, m_pool_scale, m_conv_w, m_w_mix_out, m_norm_ffn2, m_w_ffn2_in, m_w_ffn2_out, m_norm_final)))
    v = dict(zip(WEIGHTS, (v_norm_ffn1, v_w_ffn1_in, v_w_ffn1_out, v_norm_mix, v_w_mix_in, v_b_forget, v_w_pool, v_pool_scale, v_conv_w, v_w_mix_out, v_norm_ffn2, v_w_ffn2_in, v_w_ffn2_out, v_norm_final)))
    block = 2 * lax.axis_index("x") + lax.axis_index("y")

    pipe = _Pipeline(w)
    small = {k: w[k] for k in SMALL}
    loss, grad_x, sm = _local_step(x, loss_target, small, pipe.conv_w, pipe)
    grads, order = {}, list(SMALL)
    total = _all_reduce_small(_pack([sm[k] for k in order], extra=(loss,)))
    parts, rest = _unpack(total, [sm[k].shape for k in order])
    grads.update(zip(order, parts))
    loss = rest[0]
    cs = conv_w.shape[2]
    grads["conv_w"] = lax.dynamic_slice_in_dim(grads["conv_w"], block * cs, cs, axis=2)

    delta, new_m, new_v = {}, {}, {}
    for i, k in enumerate(BIG):
        two_d = lambda a: a.reshape(-1, a.shape[-1])
        res = _adamw(two_d(w[k]), [two_d(layer[i]) for layer in pipe.reduced], two_d(m[k]), two_d(v[k]))
        grads[k], delta[k], new_m[k], new_v[k] = [r.reshape(w[k].shape) for r in res]
    packed = [_pack([t[k] for k in order]) for t in (w, grads, m, v)]
    _, d, nm, nv = _adamw(packed[0], [packed[1]], packed[2], packed[3])
    shapes = [w[k].shape for k in order]
    for res, flat in ((delta, d), (new_m, nm), (new_v, nv)):
        res.update(zip(order, _unpack(flat, shapes)[0]))
    return (loss, grad_x, *[grads[k] for k in WEIGHTS], *[delta[k] for k in WEIGHTS],
            *[new_m[k] for k in WEIGHTS], *[new_v[k] for k in WEIGHTS])
```

```python
import functools
import math

import jax
import jax.numpy as jnp
from jax import lax
from jax.experimental import pallas as pl
from jax.experimental.pallas import tpu as pltpu

F32 = jnp.float32
BF16 = jnp.bfloat16
MESH = pl.DeviceIdType.MESH

HEAD_DIM = 64
POOL_WINDOWS = (2, 4, 8, 16)
CONV_WIDTH = 3
RMS_EPS = 1e-6
ADAM_LR = 0.001
ADAM_B1 = 0.9
ADAM_B2 = 0.999
ADAM_EPS = 1e-08
ADAM_WD = 0.01
ADAM_STEP = 10

LANES = 128
VMEM_LIMIT = 56 * 1024 * 1024
N_CHIPS = 4
N_DEV = 8

NN = (((1,), (0,)), ((), ()))
NT = (((1,), (1,)), ((), ()))
TN = (((0,), (0,)), ((), ()))


def _tile(n, pref):
    for t in range(pref - pref % 16, 15, -16):
        if n % t == 0:
            return t
    return n


def _params(n_grid):
    return pltpu.CompilerParams(dimension_semantics=("arbitrary",) * n_grid, vmem_limit_bytes=VMEM_LIMIT)


def _dot(a, b, dims):
    return lax.dot_general(a, b, dims, preferred_element_type=F32)


def _mm(name, dims, operands, in_specs, out_shape, out_specs, grid, acc_shape, epilogue):
    n_in, n_out, nk = len(operands), len(out_shape), grid[-1]

    def kern(*refs):
        extras, outs = refs[2:n_in], refs[n_in:n_in + n_out]
        if nk == 1:
            epilogue(_dot(refs[0][...].astype(BF16), refs[1][...].astype(BF16), dims), extras, outs)
            return
        acc = refs[n_in + n_out]
        k = pl.program_id(len(grid) - 1)

        @pl.when(k == 0)
        def _():
            acc[...] = jnp.zeros_like(acc)

        acc[...] += _dot(refs[0][...].astype(BF16), refs[1][...].astype(BF16), dims)

        @pl.when(k == nk - 1)
        def _():
            epilogue(acc[...], extras, outs)

    return pl.pallas_call(
        kern, name=name, grid=grid, in_specs=in_specs, out_specs=out_specs, out_shape=out_shape,
        scratch_shapes=[pltpu.VMEM(acc_shape, F32)] if nk > 1 else [],
        compiler_params=_params(len(grid)),
    )(*operands)


def _store(scale=None, dtype=None):
    def ep(acc, extras, outs):
        v = acc if scale is None else acc * scale
        outs[0][...] = v.astype(outs[0].dtype)
    return ep


def _residual(scale):
    def ep(acc, extras, outs):
        outs[0][...] = extras[0][...] + scale * acc
    return ep


def _rmsnorm_fwd(x, g):
    T, D = x.shape
    tr = _tile(T, 512)

    def kern(x_ref, g_ref, o_ref):
        xv = x_ref[...]
        r = lax.rsqrt(jnp.mean(xv * xv, axis=-1, keepdims=True) + RMS_EPS)
        o_ref[...] = (xv * r * g_ref[...]).astype(BF16)

    return pl.pallas_call(
        kern, name="rmsnorm_fwd", grid=(T // tr,),
        in_specs=[pl.BlockSpec((tr, D), lambda i: (i, 0)), pl.BlockSpec((1, D), lambda i: (0, 0))],
        out_specs=pl.BlockSpec((tr, D), lambda i: (i, 0)),
        out_shape=jax.ShapeDtypeStruct((T, D), BF16), compiler_params=_params(1),
    )(x, g)


def _rmsnorm_bwd(x, g, dh, dres):
    T, D = x.shape
    tr = _tile(T, 256)

    def kern(x_ref, g_ref, dh_ref, dres_ref, dx_ref, dg_ref):
        xv, dhv = x_ref[...], dh_ref[...]
        r = lax.rsqrt(jnp.mean(xv * xv, axis=-1, keepdims=True) + RMS_EPS)
        y = xv * r
        dy = dhv * g_ref[...]
        dx_ref[...] = dres_ref[...] + r * (dy - y * jnp.mean(dy * y, axis=-1, keepdims=True))
        part = jnp.sum(dhv * y, axis=0, keepdims=True)

        @pl.when(pl.program_id(0) == 0)
        def _():
            dg_ref[...] = part

        @pl.when(pl.program_id(0) > 0)
        def _():
            dg_ref[...] += part

    row = pl.BlockSpec((tr, D), lambda i: (i, 0))
    vec = pl.BlockSpec((1, D), lambda i: (0, 0))
    return pl.pallas_call(
        kern, name="rmsnorm_bwd", grid=(T // tr,), in_specs=[row, vec, row, row], out_specs=[row, vec],
        out_shape=[jax.ShapeDtypeStruct((T, D), F32), jax.ShapeDtypeStruct((1, D), F32)],
        compiler_params=_params(1),
    )(x, g, dh, dres)


def _final_loss(x, g, target):
    T, D = x.shape
    tr = _tile(T, 256)

    def kern(x_ref, g_ref, t_ref, dx_ref, dg_ref, loss_ref):
        xv = x_ref[...]
        r = lax.rsqrt(jnp.mean(xv * xv, axis=-1, keepdims=True) + RMS_EPS)
        y = xv * r
        err = y * g_ref[...] - t_ref[...]
        lpart = 0.5 * jnp.sum(jnp.mean(err * err, axis=-1, keepdims=True), axis=0, keepdims=True)
        dh = err * (1.0 / D)
        dy = dh * g_ref[...]
        dx_ref[...] = r * (dy - y * jnp.mean(dy * y, axis=-1, keepdims=True))
        part = jnp.sum(dh * y, axis=0, keepdims=True)
        lrow = jnp.broadcast_to(lpart, (1, LANES))

        @pl.when(pl.program_id(0) == 0)
        def _():
            dg_ref[...] = part
            loss_ref[...] = lrow

        @pl.when(pl.program_id(0) > 0)
        def _():
            dg_ref[...] += part
            loss_ref[...] += lrow

    row = pl.BlockSpec((tr, D), lambda i: (i, 0))
    vec = pl.BlockSpec((1, D), lambda i: (0, 0))
    return pl.pallas_call(
        kern, name="final_loss", grid=(T // tr,), in_specs=[row, vec, row],
        out_specs=[row, vec, pl.BlockSpec((1, LANES), lambda i: (0, 0))],
        out_shape=[jax.ShapeDtypeStruct((T, D), F32), jax.ShapeDtypeStruct((1, D), F32),
                   jax.ShapeDtypeStruct((1, LANES), F32)],
        compiler_params=_params(1),
    )(x, g, target)


def _ffn_in(h, w4):
    T, D = h.shape
    w4, l = w4
    Fh = w4.shape[3]
    F = 2 * Fh
    tm = _tile(T, 512)

    def kern(h_ref, wg_ref, wu_ref, jac_ref, act_ref):
        hv = h_ref[...]
        gate = _dot(hv, wg_ref[...], NN)
        up = _dot(hv, wu_ref[...], NN)
        sg = jax.nn.sigmoid(gate)
        silu = gate * sg
        jac_ref[0] = (up * (sg + silu * (1.0 - sg))).astype(BF16)
        jac_ref[1] = silu.astype(BF16)
        act_ref[...] = (silu * up).astype(BF16)

    return pl.pallas_call(
        kern, name="ffn_in", grid=(2, T // tm),
        in_specs=[pl.BlockSpec((tm, D), lambda j, i: (i, 0)),
                  pl.BlockSpec((None, None, D, Fh), lambda j, i: (l, j, 0, 0)),
                  pl.BlockSpec((None, None, D, Fh), lambda j, i: (l, 2 + j, 0, 0))],
        out_specs=[pl.BlockSpec((2, tm, Fh), lambda j, i: (0, i, j)),
                   pl.BlockSpec((tm, Fh), lambda j, i: (i, j))],
        out_shape=[jax.ShapeDtypeStruct((2, T, F), BF16), jax.ShapeDtypeStruct((T, F), BF16)],
        compiler_params=_params(2),
    )(h, w4, w4)


def _resident(shape, index_map):
    return pl.BlockSpec(shape, index_map, pipeline_mode=pl.Buffered(1))


def _token_operand(token):
    return ([], []) if token is None else ([token], [pl.BlockSpec(token.shape, lambda i: (0, 0))])


def _ffn_up(x, g, w4, token=None):
    T, D = x.shape
    w4, l = w4
    Fh = w4.shape[3]
    F = 2 * Fh
    tm = _tile(T, 512)
    tok_ops, tok_specs = _token_operand(token)

    def kern(x_ref, g_ref, w_ref, *rest):
        h_ref, jac_ref, act_ref = rest[len(tok_ops):]
        xv = x_ref[...]
        r = lax.rsqrt(jnp.mean(xv * xv, axis=-1, keepdims=True) + RMS_EPS)
        hv = (xv * r * g_ref[...]).astype(BF16)
        h_ref[...] = hv
        for j in range(2):
            cols = slice(j * Fh, (j + 1) * Fh)
            gate = _dot(hv, w_ref[j], NN)
            up = _dot(hv, w_ref[2 + j], NN)
            sg = jax.nn.sigmoid(gate)
            silu = gate * sg
            jac_ref[0, :, cols] = (up * (sg + silu * (1.0 - sg))).astype(BF16)
            jac_ref[1, :, cols] = silu.astype(BF16)
            act_ref[:, cols] = (silu * up).astype(BF16)

    return pl.pallas_call(
        kern, name="ffn_up", grid=(T // tm,),
        in_specs=[pl.BlockSpec((tm, D), lambda i: (i, 0)), pl.BlockSpec((1, D), lambda i: (0, 0)),
                  _resident((None, 4, D, Fh), lambda i: (l, 0, 0, 0))] + tok_specs,
        out_specs=[pl.BlockSpec((tm, D), lambda i: (i, 0)), pl.BlockSpec((2, tm, F), lambda i: (0, i, 0)),
                   pl.BlockSpec((tm, F), lambda i: (i, 0))],
        out_shape=[jax.ShapeDtypeStruct((T, D), BF16), jax.ShapeDtypeStruct((2, T, F), BF16),
                   jax.ShapeDtypeStruct((T, F), BF16)],
        compiler_params=_params(1),
    )(x, g, w4, *tok_ops)


def _ffn_bwd_main(dres, jac, x, g, w_out, w4, token=None):
    T, D = dres.shape
    w_out, l = w_out
    w4, _ = w4
    F = w_out.shape[1]
    Fh = F // 2
    tm = _tile(T, 256)
    tok_ops, tok_specs = _token_operand(token)

    def kern(d_ref, jac_ref, x_ref, g_ref, wo_ref, wi_ref, *rest):
        dgu_ref, dx_ref, dg_ref = rest[len(tok_ops):]
        dv = d_ref[...]
        d16 = dv.astype(BF16)
        dh = jnp.zeros((tm, D), F32)
        for j in range(2):
            cols = slice(j * Fh, (j + 1) * Fh)
            dact = 0.5 * _dot(d16, wo_ref[cols, :], NT)
            dgate = (dact * jac_ref[0, :, cols].astype(F32)).astype(BF16)
            dup = (dact * jac_ref[1, :, cols].astype(F32)).astype(BF16)
            dgu_ref[0, :, cols] = dgate
            dgu_ref[1, :, cols] = dup
            dh = dh + _dot(dgate, wi_ref[j], NT) + _dot(dup, wi_ref[2 + j], NT)
        xv = x_ref[...]
        r = lax.rsqrt(jnp.mean(xv * xv, axis=-1, keepdims=True) + RMS_EPS)
        y = xv * r
        dy = dh * g_ref[...]
        dx_ref[...] = dv + r * (dy - y * jnp.mean(dy * y, axis=-1, keepdims=True))
        part = jnp.sum(dh * y, axis=0, keepdims=True)

        @pl.when(pl.program_id(0) == 0)
        def _():
            dg_ref[...] = part

        @pl.when(pl.program_id(0) > 0)
        def _():
            dg_ref[...] += part

    row = pl.BlockSpec((tm, D), lambda i: (i, 0))
    vec = pl.BlockSpec((1, D), lambda i: (0, 0))
    wide = pl.BlockSpec((2, tm, F), lambda i: (0, i, 0))
    return pl.pallas_call(
        kern, name="ffn_bwd_main", grid=(T // tm,),
        in_specs=[row, wide, row, vec, _resident((None, F, D), lambda i: (l, 0, 0)),
                  _resident((None, 4, D, Fh), lambda i: (l, 0, 0, 0))] + tok_specs,
        out_specs=[wide, row, vec],
        out_shape=[jax.ShapeDtypeStruct((2, T, F), BF16), jax.ShapeDtypeStruct((T, D), F32),
                   jax.ShapeDtypeStruct((1, D), F32)],
        compiler_params=_params(1),
    )(dres, jac, x, g, w_out, w4, *tok_ops)


def _ffn_out(act, w_out, x):
    T, F = act.shape
    w_out, l = w_out
    D = w_out.shape[2]
    tm = _tile(T, 512)
    return _mm("ffn_out", NN, [act, w_out, x],
               [pl.BlockSpec((tm, F), lambda i, k: (i, 0)), pl.BlockSpec((None, F, D), lambda i, k: (l, 0, 0)),
                pl.BlockSpec((tm, D), lambda i, k: (i, 0))],
               [jax.ShapeDtypeStruct((T, D), F32)], [pl.BlockSpec((tm, D), lambda i, k: (i, 0))],
               (T // tm, 1), None, _residual(0.5))


def _ffn_bwd_act(dres, w_out, jac):
    T, D = dres.shape
    w_out, l = w_out
    F = w_out.shape[1]
    Fh = F // 2
    tm = _tile(T, 512)

    def kern(d_ref, w_ref, jac_ref, o_ref):
        dact = 0.5 * _dot(d_ref[...].astype(BF16), w_ref[...], NT)
        o_ref[0] = (dact * jac_ref[0].astype(F32)).astype(BF16)
        o_ref[1] = (dact * jac_ref[1].astype(F32)).astype(BF16)

    return pl.pallas_call(
        kern, name="ffn_bwd_act", grid=(2, T // tm),
        in_specs=[pl.BlockSpec((tm, D), lambda j, i: (i, 0)), pl.BlockSpec((None, Fh, D), lambda j, i: (l, j, 0)),
                  pl.BlockSpec((2, tm, Fh), lambda j, i: (0, i, j))],
        out_specs=pl.BlockSpec((2, tm, Fh), lambda j, i: (0, i, j)),
        out_shape=jax.ShapeDtypeStruct((2, T, F), BF16), compiler_params=_params(2),
    )(dres, w_out, jac)


def _ffn_dw_out(act, dres):
    T, F = act.shape
    D = dres.shape[1]
    tm, tk = F // 2, _tile(T, 1024)
    return _mm("ffn_dw_out", TN, [act, dres],
               [pl.BlockSpec((tk, tm), lambda i, k: (k, i)), pl.BlockSpec((tk, D), lambda i, k: (k, 0))],
               [jax.ShapeDtypeStruct((F, D), BF16)], [pl.BlockSpec((tm, D), lambda i, k: (i, 0))],
               (2, T // tk), (tm, D), _store(0.5))


def _ffn_dw_in(h, dgu):
    T, D = h.shape
    Fh = dgu.shape[2] // 2
    tk = _tile(T, 1024)
    return _mm("ffn_dw_in", TN, [h, dgu],
               [pl.BlockSpec((tk, D), lambda j, k: (k, 0)),
                pl.BlockSpec((None, tk, Fh), lambda j, k: (j // 2, k, j % 2))],
               [jax.ShapeDtypeStruct((4, D, Fh), BF16)], [pl.BlockSpec((None, D, Fh), lambda j, k: (j, 0, 0))],
               (4, T // tk), (D, Fh), _store())


def _ffn_dh(dgu, w4):
    T = dgu.shape[1]
    w4, l = w4
    D, Fh = w4.shape[2], w4.shape[3]
    tm = _tile(T, 1024)
    return _mm("ffn_dh", NT, [dgu, w4],
               [pl.BlockSpec((None, tm, Fh), lambda i, k: (k // 2, i, k % 2)),
                pl.BlockSpec((None, None, D, Fh), lambda i, k: (l, k, 0, 0))],
               [jax.ShapeDtypeStruct((T, D), F32)], [pl.BlockSpec((tm, D), lambda i, k: (i, 0))],
               (T // tm, 4), (tm, D), _store())


def _proj(name, a, w, out_dtype, dims=NN, extra=None, scale=None):
    T, K = a.shape
    w, l = w
    N = w.shape[2] if dims == NN else w.shape[1]
    tm = _tile(T, 512)
    ops = [a, w] + ([extra] if extra is not None else [])
    specs = [pl.BlockSpec((tm, K), lambda i, k: (i, 0)), pl.BlockSpec((None,) + w.shape[1:], lambda i, k: (l, 0, 0))]
    if extra is not None:
        specs.append(pl.BlockSpec((tm, N), lambda i, k: (i, 0)))
    ep = _residual(1.0) if extra is not None else _store(scale)
    return _mm(name, dims, ops, specs, [jax.ShapeDtypeStruct((T, N), out_dtype)],
               [pl.BlockSpec((tm, N), lambda i, k: (i, 0))], (T // tm, 1), None, ep)[0]


def _mix_up(x, g, wp, widths):
    T, D = x.shape
    wp, l = wp
    n_qkv, n_rest = widths
    NP = wp.shape[2]
    tm = _tile(T, 512)

    def kern(x_ref, g_ref, w_ref, h_ref, qkv_ref, rest_ref, fl_ref):
        xv = x_ref[...]
        r = lax.rsqrt(jnp.mean(xv * xv, axis=-1, keepdims=True) + RMS_EPS)
        hv = (xv * r * g_ref[...]).astype(BF16)
        h_ref[...] = hv
        qkv_ref[...] = _dot(hv, w_ref[:, 0:n_qkv], NN).astype(BF16)
        rest_ref[...] = _dot(hv, w_ref[:, n_qkv:n_qkv + n_rest], NN)
        fl_ref[...] = _dot(hv, w_ref[:, n_qkv + n_rest:NP], NN)

    row = lambda n: pl.BlockSpec((tm, n), lambda i: (i, 0))
    return pl.pallas_call(
        kern, name="mix_up", grid=(T // tm,),
        in_specs=[row(D), pl.BlockSpec((1, D), lambda i: (0, 0)), _resident((None, D, NP), lambda i: (l, 0, 0))],
        out_specs=[row(D), row(n_qkv), row(n_rest), row(LANES)],
        out_shape=[jax.ShapeDtypeStruct((T, D), BF16), jax.ShapeDtypeStruct((T, n_qkv), BF16),
                   jax.ShapeDtypeStruct((T, n_rest), F32), jax.ShapeDtypeStruct((T, LANES), F32)],
        compiler_params=_params(1),
    )(x, g, wp)


def _mix_in_bwd(dproj, x, g, dres, wp):
    T, D = x.shape
    wp, l = wp
    NP = wp.shape[2]
    tm = _tile(T, 512)

    def kern(dp_ref, x_ref, g_ref, d_ref, w_ref, dx_ref, dg_ref):
        dh = _dot(dp_ref[...], w_ref[...], NT)
        xv = x_ref[...]
        r = lax.rsqrt(jnp.mean(xv * xv, axis=-1, keepdims=True) + RMS_EPS)
        y = xv * r
        dy = dh * g_ref[...]
        dx_ref[...] = d_ref[...] + r * (dy - y * jnp.mean(dy * y, axis=-1, keepdims=True))
        part = jnp.sum(dh * y, axis=0, keepdims=True)

        @pl.when(pl.program_id(0) == 0)
        def _():
            dg_ref[...] = part

        @pl.when(pl.program_id(0) > 0)
        def _():
            dg_ref[...] += part

    row = lambda n: pl.BlockSpec((tm, n), lambda i: (i, 0))
    vec = pl.BlockSpec((1, D), lambda i: (0, 0))
    return pl.pallas_call(
        kern, name="mix_in_bwd", grid=(T // tm,),
        in_specs=[row(NP), row(D), vec, row(D), _resident((None, D, NP), lambda i: (l, 0, 0))],
        out_specs=[row(D), vec],
        out_shape=[jax.ShapeDtypeStruct((T, D), F32), jax.ShapeDtypeStruct((1, D), F32)],
        compiler_params=_params(1),
    )(dproj, x, g, dres, wp)


def _dw(name, a, d, out_dtype):
    T, M = a.shape
    N = d.shape[1]
    tk = _tile(T, 1024 if M * N <= 1024 * 1408 else 512)
    return _mm(name, TN, [a, d],
               [pl.BlockSpec((tk, M), lambda i, k: (k, 0)), pl.BlockSpec((tk, N), lambda i, k: (k, 0))],
               [jax.ShapeDtypeStruct((M, N), out_dtype)], [pl.BlockSpec((M, N), lambda i, k: (0, 0))],
               (1, T // tk), (M, N), _store())[0]


def _log_sigmoid(z):
    return jnp.minimum(z, 0.0) - jnp.log(1.0 + jnp.exp(-jnp.abs(z)))


def _decay_fwd(fl, bias):
    B, S, _ = fl.shape

    def kern(fl_ref, b_ref, o_ref):
        d = _log_sigmoid(fl_ref[...] + b_ref[...])
        row = lax.broadcasted_iota(jnp.int32, (S, LANES), 0)
        sh = 1
        while sh < S:
            d = d + jnp.where(row >= sh, pltpu.roll(d, sh, 0), 0.0)
            sh *= 2
        o_ref[...] = d.T[0:8, :]

    return pl.pallas_call(
        kern, name="decay_fwd", grid=(B,),
        in_specs=[pl.BlockSpec((None, S, LANES), lambda b: (b, 0, 0)), pl.BlockSpec((1, LANES), lambda b: (0, 0))],
        out_specs=pl.BlockSpec((None, 8, S), lambda b: (b, 0, 0)),
        out_shape=jax.ShapeDtypeStruct((B, 8, S), F32), compiler_params=_params(1),
    )(fl, bias)


def _decay_bwd(ddrow, ddcol, fl, bias, n_heads):
    B, S, _ = fl.shape

    def kern(dd_ref, ddc_ref, fl_ref, b_ref, dfl_ref, db_ref):
        dd = jnp.concatenate([dd_ref[...], jnp.zeros((LANES - 8, S), F32)], axis=0).T + ddc_ref[...]
        row = lax.broadcasted_iota(jnp.int32, (S, LANES), 0)
        lane = lax.broadcasted_iota(jnp.int32, (S, LANES), 1)
        sh = 1
        while sh < S:
            dd = dd + jnp.where(row < S - sh, pltpu.roll(dd, S - sh, 0), 0.0)
            sh *= 2
        z = fl_ref[...] + b_ref[...]
        dfl = jnp.where(lane < n_heads, dd / (1.0 + jnp.exp(z)), 0.0)
        dfl_ref[...] = dfl
        part = jnp.sum(dfl, axis=0, keepdims=True)

        @pl.when(pl.program_id(0) == 0)
        def _():
            db_ref[...] = part

        @pl.when(pl.program_id(0) > 0)
        def _():
            db_ref[...] += part

    return pl.pallas_call(
        kern, name="decay_bwd", grid=(B,),
        in_specs=[pl.BlockSpec((None, 8, S), lambda b: (b, 0, 0)), pl.BlockSpec((None, S, LANES), lambda b: (b, 0, 0)),
                  pl.BlockSpec((None, S, LANES), lambda b: (b, 0, 0)), pl.BlockSpec((1, LANES), lambda b: (0, 0))],
        out_specs=[pl.BlockSpec((None, S, LANES), lambda b: (b, 0, 0)), pl.BlockSpec((1, LANES), lambda b: (0, 0))],
        out_shape=[jax.ShapeDtypeStruct((B, S, LANES), F32), jax.ShapeDtypeStruct((1, LANES), F32)],
        compiler_params=_params(1),
    )(ddrow, ddcol, fl, bias)


def _attn_fwd(qkv, drow, n_heads, tq):
    B, S, _ = qkv.shape
    DA = n_heads * HEAD_DIM
    scale = HEAD_DIM ** -0.5

    n_pairs = n_heads // 2

    def kern(q_ref, k_ref, v_ref, dr_ref, o_ref, lse_ref):
        i = pl.program_id(1)
        lane = lax.broadcasted_iota(jnp.int32, (tq, LANES), 1)
        low = lane < HEAD_DIM
        causal = lax.broadcasted_iota(jnp.int32, (tq, tq), 1) <= lax.broadcasted_iota(jnp.int32, (tq, tq), 0)
        qms = []
        for p in range(n_pairs):
            q2 = q_ref[:, LANES * p:LANES * (p + 1)] * scale
            qms += [jnp.where(low, q2, jnp.zeros_like(q2)), jnp.where(low, jnp.zeros_like(q2), q2)]

        def step(j, carry, masked):
            ms, ls, accs = carry
            ks = pl.multiple_of(j * tq, tq)
            new_m, new_l, new_acc = [], [], []
            for p in range(n_pairs):
                cols = slice(LANES * p, LANES * (p + 1))
                k2, v2 = k_ref[pl.ds(ks, tq), cols], v_ref[pl.ds(ks, tq), cols]
                alphas, pvs = [], []
                for h in (2 * p, 2 * p + 1):
                    s = _dot(qms[h], k2, NT) - dr_ref[h, pl.ds(j, 1), :]
                    if masked:
                        s = jnp.where(causal, s, -jnp.inf)
                    m_new = jnp.maximum(ms[h], jnp.max(s, axis=1, keepdims=True))
                    alpha = jnp.exp(ms[h] - m_new)
                    pm = jnp.exp(s - m_new)
                    new_m.append(m_new)
                    new_l.append(alpha * ls[h] + jnp.sum(pm, axis=1, keepdims=True))
                    alphas.append(alpha)
                    pvs.append(_dot(pm.astype(BF16), v2, NN))
                new_acc.append(jnp.where(low, alphas[0], alphas[1]) * accs[p] + jnp.where(low, pvs[0], pvs[1]))
            return tuple(new_m), tuple(new_l), tuple(new_acc)

        init = (tuple(jnp.full((tq, 1), -jnp.inf, F32) for _ in range(n_heads)),
                tuple(jnp.zeros((tq, 1), F32) for _ in range(n_heads)),
                tuple(jnp.zeros((tq, LANES), F32) for _ in range(n_pairs)))
        ms, ls, accs = step(i, lax.fori_loop(0, i, functools.partial(step, masked=False), init), True)
        lse_mat = jnp.zeros((tq, LANES), F32)
        for p in range(n_pairs):
            l0, l1 = ls[2 * p], ls[2 * p + 1]
            o_ref[:, LANES * p:LANES * (p + 1)] = (accs[p] / jnp.where(low, l0, l1)).astype(BF16)
            lse_mat = jnp.where(lane == 2 * p, ms[2 * p] + jnp.log(l0), lse_mat)
            lse_mat = jnp.where(lane == 2 * p + 1, ms[2 * p + 1] + jnp.log(l1), lse_mat)
        lse_ref[...] = lse_mat

    nq = S // tq
    return pl.pallas_call(
        kern, name="attn_fwd", grid=(B, nq),
        in_specs=[pl.BlockSpec((None, tq, DA), lambda b, i: (b, i, 0)),
                  pl.BlockSpec((None, S, DA), lambda b, i: (b, 0, 1)),
                  pl.BlockSpec((None, S, DA), lambda b, i: (b, 0, 2)),
                  pl.BlockSpec((None, 8, nq, tq), lambda b, i: (b, 0, 0, 0))],
        out_specs=[pl.BlockSpec((None, tq, DA), lambda b, i: (b, i, 0)),
                   pl.BlockSpec((None, tq, LANES), lambda b, i: (b, i, 0))],
        out_shape=[jax.ShapeDtypeStruct((B, S, DA), BF16), jax.ShapeDtypeStruct((B, S, LANES), F32)],
        compiler_params=_params(2),
    )(qkv, qkv, qkv, drow)


def _attn_bwd(qkv, drow, o, lse, dycat, n_heads, tq):
    B, S, _ = qkv.shape
    DA = n_heads * HEAD_DIM
    scale = HEAD_DIM ** -0.5
    nq = S // tq

    n_pairs = n_heads // 2

    def kern(q_ref, k_ref, v_ref, dr_ref, o_ref, lse_ref, do_ref, dq_ref, dk_ref, dv_ref, ddr_ref, ddc_ref,
             dk_acc, dv_acc, qm_s, dom_s, delta_s, rs_s, dq_s):
        i = pl.program_id(1)

        @pl.when(i == 0)
        def _():
            dk_acc[...] = jnp.zeros_like(dk_acc)
            dv_acc[...] = jnp.zeros_like(dv_acc)
            ddr_ref[...] = jnp.zeros_like(ddr_ref)

        lane = lax.broadcasted_iota(jnp.int32, (tq, LANES), 1)
        low = lane < HEAD_DIM
        causal = lax.broadcasted_iota(jnp.int32, (tq, tq), 1) <= lax.broadcasted_iota(jnp.int32, (tq, tq), 0)
        for p in range(n_pairs):
            cols = slice(LANES * p, LANES * (p + 1))
            q2 = q_ref[:, cols] * scale
            do_f = do_ref[:, cols]
            do2 = do_f.astype(BF16)
            prod = do_f * o_ref[:, cols].astype(F32)
            qm_s[2 * p] = jnp.where(low, q2, jnp.zeros_like(q2))
            qm_s[2 * p + 1] = jnp.where(low, jnp.zeros_like(q2), q2)
            dom_s[2 * p] = jnp.where(low, do2, jnp.zeros_like(do2))
            dom_s[2 * p + 1] = jnp.where(low, jnp.zeros_like(do2), do2)
            delta_s[2 * p] = jnp.sum(jnp.where(low, prod, 0.0), axis=1, keepdims=True)
            delta_s[2 * p + 1] = jnp.sum(jnp.where(low, 0.0, prod), axis=1, keepdims=True)
            dq_s[p] = jnp.zeros((tq, LANES), F32)
        rs_s[...] = jnp.zeros(rs_s.shape, F32)

        def step(j, masked):
            ks = pl.multiple_of(j * tq, tq)
            for p in range(n_pairs):
                cols = slice(LANES * p, LANES * (p + 1))
                k2, v2 = k_ref[pl.ds(ks, tq), cols], v_ref[pl.ds(ks, tq), cols]
                dvs, dks, dqs = [], [], []
                for h in (2 * p, 2 * p + 1):
                    qm, dom = qm_s[h], dom_s[h]
                    s = _dot(qm, k2, NT) - dr_ref[h, pl.ds(j, 1), :]
                    if masked:
                        s = jnp.where(causal, s, -jnp.inf)
                    pm = jnp.exp(s - lse_ref[:, h:h + 1])
                    ds = pm * (_dot(dom, v2, NT) - delta_s[h])
                    ddr_ref[h, pl.ds(j, 1), :] -= jnp.sum(ds, axis=0, keepdims=True)
                    rs_s[h] += jnp.sum(ds, axis=1, keepdims=True)
                    dsb = ds.astype(BF16)
                    dvs.append(_dot(pm.astype(BF16), dom, TN))
                    dks.append(_dot(dsb, qm, TN))
                    dqs.append(_dot(dsb, k2, NN))
                dv_acc[pl.ds(ks, tq), cols] += dvs[0] + dvs[1]
                dk_acc[pl.ds(ks, tq), cols] += dks[0] + dks[1]
                dq_s[p] += jnp.where(low, dqs[0], dqs[1])

        def body(j, carry):
            step(j, False)
            return carry

        lax.fori_loop(0, i, body, 0)
        step(i, True)
        ddc = jnp.zeros((tq, LANES), F32)
        for p in range(n_pairs):
            dq_ref[:, LANES * p:LANES * (p + 1)] = (dq_s[p] * scale).astype(BF16)
            ddc = jnp.where(lane == 2 * p, rs_s[2 * p], ddc)
            ddc = jnp.where(lane == 2 * p + 1, rs_s[2 * p + 1], ddc)
        ddc_ref[...] = ddc

        @pl.when(i == nq - 1)
        def _():
            dk_ref[...] = dk_acc[...].astype(BF16)
            dv_ref[...] = dv_acc[...].astype(BF16)

    tile = pl.BlockSpec((None, tq, DA), lambda b, i: (b, i, 0))
    seq = pl.BlockSpec((None, S, DA), lambda b, i: (b, 0, 0))
    dec = pl.BlockSpec((None, 8, nq, tq), lambda b, i: (b, 0, 0, 0))
    return pl.pallas_call(
        kern, name="attn_bwd", grid=(B, nq),
        in_specs=[tile, pl.BlockSpec((None, S, DA), lambda b, i: (b, 0, 1)),
                  pl.BlockSpec((None, S, DA), lambda b, i: (b, 0, 2)), dec, tile,
                  pl.BlockSpec((None, tq, LANES), lambda b, i: (b, i, 0)), tile],
        out_specs=[tile, seq, seq, dec, pl.BlockSpec((None, tq, LANES), lambda b, i: (b, i, 0))],
        out_shape=[jax.ShapeDtypeStruct((B, S, DA), BF16)] * 3 + [jax.ShapeDtypeStruct((B, 8, nq, tq), F32),
                                                                  jax.ShapeDtypeStruct((B, S, LANES), F32)],
        scratch_shapes=[pltpu.VMEM((S, DA), F32), pltpu.VMEM((S, DA), F32),
                        pltpu.VMEM((n_heads, tq, LANES), BF16), pltpu.VMEM((n_heads, tq, LANES), BF16),
                        pltpu.VMEM((n_heads, tq, 1), F32), pltpu.VMEM((n_heads, tq, 1), F32),
                        pltpu.VMEM((n_pairs, tq, LANES), F32)],
        compiler_params=_params(2),
    )(qkv, qkv, qkv, drow, o, lse, dycat)


def _down(v, d, row):
    return jnp.where(row >= d, pltpu.roll(v, d, 0), 0.0)


def _up(v, d, row, S):
    return jnp.where(row < S - d, pltpu.roll(v, S - d, 0), 0.0)


def _window(v, shift, group):
    sums, acc, d = [], v, 1
    for _ in POOL_WINDOWS:
        acc = acc + shift(acc, d)
        sums.append(acc)
        d *= 2
    out = sums[-1]
    for gi in range(len(POOL_WINDOWS) - 2, -1, -1):
        out = jnp.where(group == gi, sums[gi], out)
    return out


def _pool_count(row, group):
    w = jnp.full(row.shape, POOL_WINDOWS[-1], jnp.int32)
    for gi in range(len(POOL_WINDOWS) - 2, -1, -1):
        w = jnp.where(group == gi, POOL_WINDOWS[gi], w)
    return jnp.minimum(row + 1, w).astype(F32)


def _mix_local_fwd(rest, wbd, ps, cw):
    B, S, C4 = rest.shape
    C = C4 // 4
    gw = C // len(POOL_WINDOWS)

    def kern(r_ref, w_ref, ps_ref, cw_ref, y_ref, pooled_ref):
        row = lax.broadcasted_iota(jnp.int32, (S, C), 0)
        group = lax.broadcasted_iota(jnp.int32, (S, C), 1) // gw
        u = r_ref[:, 0:C]
        pooled = _window(u, lambda v, d: _down(v, d, row), group) / _pool_count(row, group) - u
        pb = pooled.astype(BF16)
        pooled_ref[...] = pb
        y_ref[:, 0:C] = (_dot(pb, w_ref[...], NN) * ps_ref[...]).astype(BF16)
        uc = r_ref[:, 2 * C:3 * C] * r_ref[:, 3 * C:4 * C]
        y = cw_ref[0:1, :] * _down(uc, 2, row) + cw_ref[1:2, :] * _down(uc, 1, row) + cw_ref[2:3, :] * uc
        y_ref[:, C:2 * C] = (r_ref[:, C:2 * C] * y).astype(BF16)

    return pl.pallas_call(
        kern, name="mix_local_fwd", grid=(B,),
        in_specs=[pl.BlockSpec((None, S, C4), lambda b: (b, 0, 0)), pl.BlockSpec((C, C), lambda b: (0, 0)),
                  pl.BlockSpec((1, C), lambda b: (0, 0)), pl.BlockSpec((8, C), lambda b: (0, 0))],
        out_specs=[pl.BlockSpec((None, S, 2 * C), lambda b: (b, 0, 0)), pl.BlockSpec((None, S, C), lambda b: (b, 0, 0))],
        out_shape=[jax.ShapeDtypeStruct((B, S, 2 * C), BF16), jax.ShapeDtypeStruct((B, S, C), BF16)],
        compiler_params=_params(1),
    )(rest, wbd, ps, cw)


def _mix_local_bwd(rest, pooled, dycat, wbd, ps, cw):
    B, S, C4 = rest.shape
    C = C4 // 4
    gw = C // len(POOL_WINDOWS)

    def kern(r_ref, pooled_ref, d_ref, w_ref, ps_ref, cw_ref, dr_ref, dw_ref, dps_ref, dcw_ref):
        row = lax.broadcasted_iota(jnp.int32, (S, C), 0)
        group = lax.broadcasted_iota(jnp.int32, (S, C), 1) // gw
        dyp = d_ref[:, 0:C]
        dyc = d_ref[:, C:2 * C]
        pb = pooled_ref[...]
        dps = jnp.sum(dyp * _dot(pb, w_ref[...], NN), axis=0, keepdims=True)
        dzb = (dyp * ps_ref[...]).astype(BF16)
        dw = _dot(pb, dzb, TN)
        dpooled = _dot(dzb, w_ref[...], NT)
        g = dpooled / _pool_count(row, group)
        dr_ref[:, 0:C] = (_window(g, lambda v, d: _up(v, d, row, S), group) - dpooled).astype(BF16)
        cc, ch = r_ref[:, 2 * C:3 * C], r_ref[:, 3 * C:4 * C]
        uc = cc * ch
        u1, u2 = _down(uc, 1, row), _down(uc, 2, row)
        y = cw_ref[0:1, :] * u2 + cw_ref[1:2, :] * u1 + cw_ref[2:3, :] * uc
        dr_ref[:, C:2 * C] = (dyc * y).astype(BF16)
        dy = dyc * r_ref[:, C:2 * C]
        duc = cw_ref[0:1, :] * _up(dy, 2, row, S) + cw_ref[1:2, :] * _up(dy, 1, row, S) + cw_ref[2:3, :] * dy
        dr_ref[:, 2 * C:3 * C] = (duc * ch).astype(BF16)
        dr_ref[:, 3 * C:4 * C] = (duc * cc).astype(BF16)
        dcw = jnp.concatenate([jnp.sum(dy * u2, axis=0, keepdims=True), jnp.sum(dy * u1, axis=0, keepdims=True),
                               jnp.sum(dy * uc, axis=0, keepdims=True), jnp.zeros((5, C), F32)], axis=0)

        @pl.when(pl.program_id(0) == 0)
        def _():
            dw_ref[...] = dw
            dps_ref[...] = dps
            dcw_ref[...] = dcw

        @pl.when(pl.program_id(0) > 0)
        def _():
            dw_ref[...] += dw
            dps_ref[...] += dps
            dcw_ref[...] += dcw

    full = lambda shape: pl.BlockSpec(shape, lambda b: (0, 0))
    return pl.pallas_call(
        kern, name="mix_local_bwd", grid=(B,),
        in_specs=[pl.BlockSpec((None, S, C4), lambda b: (b, 0, 0)), pl.BlockSpec((None, S, C), lambda b: (b, 0, 0)),
                  pl.BlockSpec((None, S, 2 * C), lambda b: (b, 0, 1)), full((C, C)), full((1, C)), full((8, C))],
        out_specs=[pl.BlockSpec((None, S, C4), lambda b: (b, 0, 0)), full((C, C)), full((1, C)), full((8, C))],
        out_shape=[jax.ShapeDtypeStruct((B, S, C4), BF16), jax.ShapeDtypeStruct((C, C), F32),
                   jax.ShapeDtypeStruct((1, C), F32), jax.ShapeDtypeStruct((8, C), F32)],
        compiler_params=_params(1),
    )(rest, pooled, dycat, wbd, ps, cw)


def _adamw(w, gs, m, v):
    R, C = w.shape
    pieces = [p if isinstance(p, tuple) else (p,) for p in gs]
    owner = [s for s, p in enumerate(pieces) for _ in p]
    flat = [a for p in pieces for a in p]
    n = len(flat)
    rows = R // len(pieces)
    tr = _tile(rows, 256)
    per = rows // tr

    def kern(w_ref, *refs):
        g_refs, (m_ref, v_ref, g_out, d_ref, nm_ref, nv_ref) = refs[:n], refs[n:]
        vals, at = [], 0
        for p in pieces:
            vals.append(g_refs[at][...] if len(p) == 1 else g_refs[at][...] + g_refs[at + 1][...])
            at += len(p)
        gv = vals[0]
        for s in range(1, len(pieces)):
            gv = jnp.where(pl.program_id(0) // per == s, vals[s], gv)
        nm = ADAM_B1 * m_ref[...] + (1.0 - ADAM_B1) * gv
        nv = ADAM_B2 * v_ref[...] + (1.0 - ADAM_B2) * (gv * gv)
        m_hat = nm / (1.0 - ADAM_B1 ** ADAM_STEP)
        v_hat = nv / (1.0 - ADAM_B2 ** ADAM_STEP)
        g_out[...] = gv
        d_ref[...] = -ADAM_LR * (m_hat / (jnp.sqrt(v_hat) + ADAM_EPS) + ADAM_WD * w_ref[...])
        nm_ref[...] = nm
        nv_ref[...] = nv

    def piece(s):
        return pl.BlockSpec((tr, C), lambda i: (jnp.clip(i - s * per, 0, per - 1), 0))

    blk = pl.BlockSpec((tr, C), lambda i: (i, 0))
    return pl.pallas_call(
        kern, name="adamw", grid=(R // tr,), in_specs=[blk] + [piece(s) for s in owner] + [blk] * 2,
        out_specs=[blk] * 4, out_shape=[jax.ShapeDtypeStruct((R, C), F32)] * 4, compiler_params=_params(1),
    )(w, *flat, m, v)


def _place():
    x, y, c = lax.axis_index("x"), lax.axis_index("y"), lax.axis_index("c")
    return x, y, c, [(1 - x, y), (x, 1 - y), (1 - x, 1 - y)]


def _comm_call(name, body, operands, out_shape, n_sems, aliases=None):
    any_spec = pl.BlockSpec(memory_space=pl.ANY)
    return pl.pallas_call(
        body, name=name, in_specs=[any_spec] * len(operands), out_specs=[any_spec] * len(out_shape),
        out_shape=out_shape, input_output_aliases=aliases or {},
        scratch_shapes=[pltpu.SemaphoreType.DMA((n,)) for n in n_sems],
    )(*operands)


def _my_block():
    return 2 * lax.axis_index("x") + lax.axis_index("y")


def _place_shard(w, dtype, first=0, count=None):
    L, R, C = w.shape
    count = L if count is None else count
    tr = _tile(R, 512)

    def kern(w_ref, o_ref):
        o_ref[...] = w_ref[...].astype(dtype)

    return pl.pallas_call(
        kern, name="place_shard", grid=(count, R // tr),
        in_specs=[pl.BlockSpec((None, tr, C), lambda l, i: (first + l, i, 0))],
        out_specs=pl.BlockSpec((None, None, tr, C), lambda l, i: (l, _my_block(), i, 0)),
        out_shape=jax.ShapeDtypeStruct((count, N_CHIPS, R, C), dtype), compiler_params=_params(2),
    )(w)


HALF_ROWS = 16


def _rows(ref, half):
    hr = ref.shape[-2] // 2
    return ref.at[(slice(None),) * (len(ref.shape) - 2) + (pl.ds(half * hr, hr),)]


def _all_gather(bufs):
    n = len(bufs)

    def body(*refs):
        outs = refs[n:2 * n]
        send_sems, recv_sems = refs[2 * n:]
        x, y, c, chips = _place()
        sibling = (x, y, 1 - c)

        def remote(k, j, chip, half, to):
            blk = 2 * chip[0] + chip[1]
            if outs[k].shape[2] % (2 * HALF_ROWS) == 0:
                region = _rows(outs[k].at[:, blk], half)
            else:
                hl = outs[k].shape[0] // 2
                region = outs[k].at[pl.ds(half * hl, hl), blk]
            return pltpu.make_async_remote_copy(
                src_ref=region, dst_ref=region, send_sem=send_sems.at[6 * k + j],
                recv_sem=recv_sems.at[6 * k + j], device_id=to, device_id_type=MESH)

        first = [remote(k, j, (x, y), c, (*chip, c)) for k in range(n) for j, chip in enumerate(chips)]
        for cp in first:
            cp.start()
        passed = []
        for k in range(n):
            for j, chip in enumerate(chips):
                remote(k, j, chip, c, (x, y, c)).wait_recv()
                passed.append(remote(k, 3 + j, chip, c, sibling))
                passed[-1].start()
        for k in range(n):
            for j, chip in enumerate(chips):
                remote(k, 3 + j, chip, 1 - c, (x, y, c)).wait_recv()
        for cp in first + passed:
            cp.wait_send()

    out_shape = [jax.ShapeDtypeStruct(s.shape, s.dtype) for s in bufs]
    return _comm_call("all_gather_weights", body, bufs, out_shape, (6 * n, 6 * n), aliases={k: k for k in range(n)})


_HBM = pl.BlockSpec(memory_space=pltpu.HBM)
_SEM = pl.BlockSpec(memory_space=pltpu.SEMAPHORE)
_ANY = pl.BlockSpec(memory_space=pl.ANY)


def _split_start(name, bufs, n_copies, make_copies, after):
    n = len(bufs)

    def body(*refs):
        send_sems, recv_sems, token = refs[n + 1], refs[n + 2], refs[2 * n + 3]
        for cp in make_copies(refs[:n], send_sems, recv_sems):
            cp.start()
        token[...] = jnp.zeros_like(token)

    res = pl.pallas_call(
        body, name=name, in_specs=[_HBM] * n + [_ANY],
        out_shape=(pltpu.SemaphoreType.DMA((n_copies,)), pltpu.SemaphoreType.DMA((n_copies,)),
                   *[pltpu.HBM(b.shape, b.dtype) for b in bufs], jax.ShapeDtypeStruct((8, LANES), F32)),
        out_specs=(_SEM, _SEM, *[_HBM] * n, pl.BlockSpec(memory_space=pltpu.VMEM)),
        input_output_aliases={i: 2 + i for i in range(n)},
        compiler_params=pltpu.CompilerParams(has_side_effects=pltpu.SideEffectType.DATAFLOW_SIDE_EFFECTING),
    )(*[pltpu.with_memory_space_constraint(b, pltpu.HBM) for b in bufs], after)
    return res[0], res[1], list(res[2:2 + n]), res[2 + n]


def _split_wait(name, send_sems, recv_sems, bufs, make_copies, after):
    n = len(bufs)

    def body(*refs):
        for cp in make_copies(refs[:n], refs[n], refs[n + 1]):
            cp.wait_send()
            cp.wait_recv()

    return list(pl.pallas_call(
        body, name=name, in_specs=[_HBM] * n + [_SEM, _SEM, _ANY],
        out_shape=tuple(pltpu.HBM(b.shape, b.dtype) for b in bufs), out_specs=tuple([_HBM] * n),
        input_output_aliases={i: i for i in range(n)},
        compiler_params=pltpu.CompilerParams(has_side_effects=pltpu.SideEffectType.DATAFLOW_SIDE_EFFECTING),
    )(*bufs, send_sems, recv_sems, after))


def _gather_copies(refs, send_sems, recv_sems):
    x, y, c, chips = _place()
    return [pltpu.make_async_remote_copy(
        src_ref=ref.at[:, 2 * x + y], dst_ref=ref.at[:, 2 * x + y], send_sem=send_sems.at[3 * k + j],
        recv_sem=recv_sems.at[3 * k + j], device_id=(*chip, c), device_id_type=MESH)
        for k, ref in enumerate(refs) for j, chip in enumerate(chips)]


def _exchange_copies(refs, send_sems, recv_sems):
    n = len(refs) // 2
    x, y, c, chips = _place()
    return [pltpu.make_async_remote_copy(
        src_ref=refs[k].at[:, 2 * chip[0] + chip[1]], dst_ref=refs[n + k].at[j], send_sem=send_sems.at[3 * k + j],
        recv_sem=recv_sems.at[3 * k + j], device_id=(*chip, c), device_id_type=MESH)
        for k in range(n) for j, chip in enumerate(chips)]


def _rs_swap_halves(grads):
    n = len(grads)

    def body(*refs):
        ins, outs = refs[:n], refs[n:2 * n]
        send_sems, recv_sems = refs[2 * n:]
        x, y, c, _ = _place()
        copies = [pltpu.make_async_remote_copy(
            src_ref=_rows(ins[k], 1 - c), dst_ref=outs[k], send_sem=send_sems.at[k],
            recv_sem=recv_sems.at[k], device_id=(x, y, 1 - c), device_id_type=MESH) for k in range(n)]
        for cp in copies:
            cp.start()
        for cp in copies:
            cp.wait()

    out_shape = [jax.ShapeDtypeStruct(g.shape[:2] + (g.shape[2] // 2, g.shape[3]), g.dtype) for g in grads]
    return _comm_call("rs_swap_halves", body, grads, out_shape, (n, n))


def _rs_exchange(parts):
    n = len(parts)

    def body(*refs):
        ins, outs = refs[:n], refs[n:2 * n]
        send_sems, recv_sems = refs[2 * n:]
        x, y, c, chips = _place()
        copies = [pltpu.make_async_remote_copy(
            src_ref=ins[k].at[:, 2 * chip[0] + chip[1]], dst_ref=outs[k].at[j], send_sem=send_sems.at[3 * k + j],
            recv_sem=recv_sems.at[3 * k + j], device_id=(*chip, c), device_id_type=MESH)
            for k in range(n) for j, chip in enumerate(chips)]
        for cp in copies:
            cp.start()
        for cp in copies:
            cp.wait()

    out_shape = [jax.ShapeDtypeStruct((3, p.shape[0]) + p.shape[2:], p.dtype) for p in parts]
    return _comm_call("rs_exchange", body, parts, out_shape, (3 * n, 3 * n))


def _rs_share(bufs):
    n = len(bufs)

    def body(*refs):
        outs = refs[n:2 * n]
        send_sems, recv_sems = refs[2 * n:]
        x, y, c, _ = _place()

        def half(k, which):
            region = _rows(outs[k], which)
            return pltpu.make_async_remote_copy(
                src_ref=region, dst_ref=region, send_sem=send_sems.at[k], recv_sem=recv_sems.at[k],
                device_id=(x, y, 1 - c), device_id_type=MESH)

        sends = [half(k, c) for k in range(n)]
        for cp in sends:
            cp.start()
        for k in range(n):
            half(k, 1 - c).wait_recv()
        for cp in sends:
            cp.wait_send()

    out_shape = [jax.ShapeDtypeStruct(h.shape, h.dtype) for h in bufs]
    return _comm_call("rs_share", body, bufs, out_shape, (n, n), aliases={k: k for k in range(n)})


def _all_reduce_small(v):
    n = v.shape[0]

    def body(v_ref, o_ref, gbuf, send_sems, recv_sems):
        x, y, c, _ = _place()
        me = 4 * x + 2 * y + c
        gbuf[me] = v_ref[...]
        copies, waits = [], []
        for r in range(1, N_DEV):
            px = 1 - x if r & 4 else x
            py = 1 - y if r & 2 else y
            pc = 1 - c if r & 1 else c
            mk = functools.partial(pltpu.make_async_remote_copy, src_ref=v_ref, send_sem=send_sems.at[r - 1],
                                   recv_sem=recv_sems.at[r - 1], device_id=(px, py, pc), device_id_type=MESH)
            copies.append(mk(dst_ref=gbuf.at[me]))
            waits.append(mk(dst_ref=gbuf.at[4 * px + 2 * py + pc]))
        for cp in copies:
            cp.start()
        for cp in waits:
            cp.wait_recv()
        for cp in copies:
            cp.wait_send()
        acc = gbuf[0]
        for d in range(1, N_DEV):
            acc = acc + gbuf[d]
        o_ref[...] = acc

    vm = pl.BlockSpec(memory_space=pltpu.VMEM)
    return pl.pallas_call(
        body, name="all_reduce_small", in_specs=[vm], out_specs=vm, out_shape=jax.ShapeDtypeStruct(v.shape, F32),
        scratch_shapes=[pltpu.VMEM((N_DEV, n, LANES), F32), pltpu.SemaphoreType.DMA((N_DEV - 1,)),
                        pltpu.SemaphoreType.DMA((N_DEV - 1,))],
        compiler_params=pltpu.CompilerParams(vmem_limit_bytes=VMEM_LIMIT),
    )(v)


def _add_half(g, h1):
    L, nb, hr, C = h1.shape
    g3, h3 = g.reshape(L * nb, 2 * hr, C), h1.reshape(L * nb, hr, C)
    tr = _tile(hr, 512)

    def kern(g_ref, h_ref, o_ref):
        o_ref[...] = (g_ref[...].astype(F32) + h_ref[...].astype(F32)).astype(BF16)

    blk = pl.BlockSpec((None, tr, C), lambda l, i: (l, i, 0))
    out = pl.pallas_call(
        kern, name="rs_add_half", grid=(L * nb, hr // tr),
        in_specs=[pl.BlockSpec((None, tr, C), lambda l, i: (l, lax.axis_index("c") * (hr // tr) + i, 0)), blk],
        out_specs=blk, out_shape=jax.ShapeDtypeStruct(h3.shape, BF16), compiler_params=_params(2),
    )(g3, h3)
    return out.reshape(h1.shape)


def _add_blocks(p, h2, half=True):
    L, nb, hr, C = p.shape
    tr = _tile(hr, 512)
    shift = lambda: lax.axis_index("c") * (hr // tr) if half else 0

    def kern(p_ref, h0_ref, h1_ref, h2_ref, o_ref):
        o_ref[...] = ((p_ref[...].astype(F32) + h0_ref[...].astype(F32)) + h1_ref[...].astype(F32)) + h2_ref[...].astype(F32)

    def other(j):
        return pl.BlockSpec((None, None, tr, C), lambda l, i: (j, l, i, 0))

    return pl.pallas_call(
        kern, name="rs_add_blocks", grid=(L, hr // tr),
        in_specs=[pl.BlockSpec((None, None, tr, C), lambda l, i: (l, _my_block(), i, 0)), other(0), other(1), other(2)],
        out_specs=pl.BlockSpec((None, tr, C), lambda l, i: (l, shift() + i, 0)),
        out_shape=jax.ShapeDtypeStruct((L, (2 if half else 1) * hr, C), F32), compiler_params=_params(2),
    )(p, h2, h2, h2)


def _d2d_swap(arrays):
    n = len(arrays)

    def body(*refs):
        ins, outs = refs[:n], refs[n:2 * n]
        send_sems, recv_sems = refs[2 * n:]
        x, y, c, _ = _place()
        copies = [pltpu.make_async_remote_copy(
            src_ref=ins[k], dst_ref=outs[k], send_sem=send_sems.at[k], recv_sem=recv_sems.at[k],
            device_id=(x, y, 1 - c), device_id_type=MESH) for k in range(n)]
        for cp in copies:
            cp.start()
        for cp in copies:
            cp.wait()

    return _comm_call("d2d_swap", body, arrays, [jax.ShapeDtypeStruct(a.shape, a.dtype) for a in arrays], (n, n))


def _reduce_scatter(grads):
    sib = _rs_swap_halves(grads)
    parts = [_add_half(g, h) for g, h in zip(grads, sib)]
    others = _rs_exchange(parts)
    return _rs_share([_add_blocks(p, o) for p, o in zip(parts, others)])


WEIGHTS = ("norm_ffn1", "w_ffn1_in", "w_ffn1_out", "norm_mix", "w_mix_in", "b_forget", "w_pool", "pool_scale",
           "conv_w", "w_mix_out", "norm_ffn2", "w_ffn2_in", "w_ffn2_out", "norm_final")
BIG = ("w_ffn1_in", "w_ffn1_out", "w_mix_in", "w_mix_out", "w_ffn2_in", "w_ffn2_out")
SMALL = ("norm_ffn1", "norm_mix", "b_forget", "w_pool", "pool_scale", "conv_w", "norm_ffn2", "norm_final")


def _prep_weights(small, gathered, conv_w, D, first):
    DA, C, H = D // 2, D // 4, D // 2 // HEAD_DIM
    L = gathered["w_mix_in"].shape[0]
    small = {k: val[first:first + L] for k, val in small.items() if k != "norm_final"}
    gathered = dict(gathered, conv_w=conv_w[first:first + L])
    w_in = jnp.concatenate([gathered["w_mix_in"][:, b] for b in range(N_CHIPS)], axis=2)
    wqkv, wrest = w_in[:, :, :3 * DA], w_in[:, :, 3 * DA + H:]
    wf = jnp.pad(w_in[:, :, 3 * DA:3 * DA + H], ((0, 0), (0, 0), (0, LANES - H)))
    ng = len(POOL_WINDOWS)
    same_group = jnp.eye(ng, dtype=bool)[None, :, None, :, None]
    wbd = jnp.where(same_group, small["w_pool"][:, :, :, None, :], 0.0).reshape(L, C, C)
    cw = jnp.concatenate([gathered["conv_w"][:, b] for b in range(N_CHIPS)], axis=2)
    return dict(
        g1=small["norm_ffn1"], gm=small["norm_mix"], g2=small["norm_ffn2"],
        w1in=gathered["w_ffn1_in"], w2in=gathered["w_ffn2_in"],
        w1out=gathered["w_ffn1_out"].reshape(L, -1, D), w2out=gathered["w_ffn2_out"].reshape(L, -1, D),
        wp=jnp.concatenate([wqkv, wrest, wf], axis=2), wmixout=gathered["w_mix_out"].reshape(L, D, D),
        bias=jnp.pad(small["b_forget"], ((0, 0), (0, LANES - H))), wbd=wbd.astype(BF16), ps=small["pool_scale"],
        cw=jnp.pad(cw, ((0, 0), (0, 8 - CONV_WIDTH), (0, 0))),
    )


def _layer_params(l, W):
    P = {k: (W[k], l) for k in ("w1in", "w2in", "w1out", "w2out", "wp", "wmixout")}
    P.update({k: W[k][l][None] for k in ("g1", "gm", "g2", "bias", "ps")})
    P.update(wbd=W["wbd"][l], cw=W["cw"][l])
    return P


def _ffn_fwd(x, g, w_in, w_out, token=None):
    h, jac, act = _ffn_up(x, g, w_in, token)
    return _ffn_out(act, w_out, x)[0], (x, h, jac, act)


def _ffn_bwd(dres, saved, g, w_in, w_out, token=None):
    x, h, jac, act = saved
    dgu, dx, dg = _ffn_bwd_main(dres, jac, x, g, w_out, w_in, token)
    dw_out = _ffn_dw_out(act, dres)[0]
    dw_in = _ffn_dw_in(h, dgu)[0]
    return dx, dg, dw_in, dw_out.reshape(N_CHIPS, -1, dw_out.shape[1])


def _mixer_fwd(x, P, B, S, tq):
    T, D = x.shape
    DA, C, H = D // 2, D // 4, D // 2 // HEAD_DIM
    hn, qkv, rest, fl = _mix_up(x, P["gm"], P["wp"], (3 * DA, 4 * C))
    qkv, rest, fl = qkv.reshape(B, S, 3 * DA), rest.reshape(B, S, 4 * C), fl.reshape(B, S, LANES)
    drow = _decay_fwd(fl, P["bias"]).reshape(B, 8, S // tq, tq)
    o, lse = _attn_fwd(qkv, drow, H, tq)
    ypc, pooled = _mix_local_fwd(rest, P["wbd"], P["ps"], P["cw"])
    ycat = jnp.concatenate([o, ypc], axis=-1).reshape(T, D)
    return _proj("mix_out", ycat, P["wmixout"], F32, extra=x), (x, hn, qkv, rest, fl, drow, o, lse, pooled, ycat)


def _mixer_bwd(dres, saved, P, B, S, tq):
    x, hn, qkv, rest, fl, drow, o, lse, pooled, ycat = saved
    T, D = x.shape
    DA, C, H = D // 2, D // 4, D // 2 // HEAD_DIM
    dycat = _proj("mix_out_bwd", dres, P["wmixout"], F32, NT).reshape(B, S, D)
    dw_out = _dw("mix_out_dw", ycat, dres, BF16)
    dq, dk, dv, ddrow, ddcol = _attn_bwd(qkv, drow, o, lse, dycat, H, tq)
    dfl, dbias = _decay_bwd(ddrow.reshape(B, 8, S), ddcol, fl, P["bias"], H)
    drest, dwbd, dps, dcw = _mix_local_bwd(rest, pooled, dycat, P["wbd"], P["ps"], P["cw"])
    dproj = jnp.concatenate([dq, dk, dv, drest, dfl.astype(BF16)], axis=-1).reshape(T, 3 * DA + 4 * C + LANES)
    dwp = _dw("mix_in_dw", hn, dproj, F32)
    dx, dg = _mix_in_bwd(dproj, x, P["gm"], dres, P["wp"])
    n_q, n_r = 3 * DA, 4 * C
    dw_in = jnp.concatenate([dwp[:, :n_q], dwp[:, n_q + n_r:n_q + n_r + H], dwp[:, n_q:n_q + n_r]], axis=1)
    dw_in = dw_in.reshape(D, N_CHIPS, -1).transpose(1, 0, 2).astype(BF16)
    ng = len(POOL_WINDOWS)
    same_group = jnp.eye(ng, dtype=bool)[:, None, :, None]
    dw_pool = jnp.where(same_group, dwbd.reshape(ng, C // ng, ng, C // ng), 0.0).sum(axis=2)
    small = dict(norm_mix=dg[0], b_forget=dbias[0, :H], w_pool=dw_pool, pool_scale=dps[0], conv_w=dcw[:CONV_WIDTH])
    return dx, small, dw_in, dw_out.reshape(N_CHIPS, -1, D)


def _local_step(x, target, small, conv_w, pipe):
    B, S, D = x.shape
    L = small["norm_ffn1"].shape[0]
    tq = _tile(S, 256)
    xt = x.reshape(B * S, D)
    saved, params = [], []
    for l in range(L):
        P = _layer_params(0, _prep_weights(small, pipe.weights(l, xt), conv_w, D, l))
        xt, s1 = _ffn_fwd(xt, P["g1"], P["w1in"], P["w1out"], pipe.token(l))
        xt, s2 = _mixer_fwd(xt, P, B, S, tq)
        xt, s3 = _ffn_fwd(xt, P["g2"], P["w2in"], P["w2out"])
        saved.append((s1, s2, s3))
        params.append(P)
    dres, dgf, loss = _final_loss(xt, small["norm_final"][None], target.reshape(B * S, D))
    sm = {k: [None] * L for k in SMALL if k != "norm_final"}
    token = None
    for l in reversed(range(L)):
        P, (s1, s2, s3) = params[l], saved[l]
        big = {}
        dres, dg2, big["w_ffn2_in"], big["w_ffn2_out"] = _ffn_bwd(dres, s3, P["g2"], P["w2in"], P["w2out"], token)
        dres, smix, big["w_mix_in"], big["w_mix_out"] = _mixer_bwd(dres, s2, P, B, S, tq)
        dres, dg1, big["w_ffn1_in"], big["w_ffn1_out"] = _ffn_bwd(dres, s1, P["g1"], P["w1in"], P["w1out"])
        sm["norm_ffn1"][l], sm["norm_ffn2"][l] = dg1[0], dg2[0]
        for k, val in smix.items():
            sm[k][l] = val
        token = pipe.grads(l, {k: val[None] for k, val in big.items()}, dres)
    sm = {k: jnp.stack(val) for k, val in sm.items()}
    sm["norm_final"] = dgf[0]
    return loss[0, 0], dres.reshape(B, S, D), sm


class _Pipeline:
    def __init__(self, w):
        self.w, self.n_layers = w, w[BIG[0]].shape[0]
        first = _all_gather([_place_shard(w[k], BF16, 0, 1) for k in BIG] + [_place_shard(w["conv_w"], F32)])
        self.conv_w = first[-1]
        self._ready = dict(zip(BIG, first[:-1]))
        self._gather = self._reduce = None
        self.reduced = [None] * self.n_layers
        self._start_gather(1, first[0])

    def _start_gather(self, l, after):
        if l < self.n_layers:
            placed = [_place_shard(self.w[k], BF16, l, 1) for k in BIG]
            self._gather = (l, _split_start(f"gather_start_{l}", placed, 3 * len(BIG), _gather_copies, after))

    def token(self, l):
        return self._gather[1][3] if self._gather is not None and self._gather[0] == l + 1 else None

    def weights(self, l, after):
        if l == 0:
            return self._ready
        (_, (send_sems, recv_sems, bufs, _)), self._gather = self._gather, None
        got = _split_wait(f"gather_wait_{l}", send_sems, recv_sems, bufs, _gather_copies, after)
        self._start_gather(l + 1, got[0])
        return dict(zip(BIG, got))

    def grads(self, l, big, after):
        n, grads = len(BIG), [big[k] for k in BIG]
        if self._reduce is not None:
            above, (send_sems, recv_sems, bufs, _) = self._reduce
            bufs = _split_wait(f"reduce_wait_{above}", send_sems, recv_sems, bufs, _exchange_copies, after)
            mine = [_add_blocks(p, o, half=False) for p, o in zip(bufs[:n], bufs[n:])]
            self.reduced[above] = list(zip(mine, _d2d_swap(mine)))
            self._reduce = None
        if l == 0:
            self.reduced[0] = _reduce_scatter(grads)
            return None
        lands = [lax.empty((3, g.shape[0]) + g.shape[2:], g.dtype) for g in grads]
        self._reduce = (l, _split_start(f"reduce_start_{l}", grads + lands, 3 * n, _exchange_copies, grads[0]))
        return self._reduce[1][3]


def _pack(parts, extra=()):
    flat = jnp.concatenate([p.reshape(-1) for p in parts] + [jnp.reshape(e, (1,)) for e in extra])
    n = -(-flat.shape[0] // (8 * LANES)) * 8
    return jnp.pad(flat, (0, n * LANES - flat.shape[0])).reshape(n, LANES)


def _unpack(buf, shapes):
    flat, out, at = buf.reshape(-1), [], 0
    for s in shapes:
        n = math.prod(s)
        out.append(flat[at:at + n].reshape(s))
        at += n
    return out, flat[at:]


def kernel(x, norm_ffn1, w_ffn1_in, w_ffn1_out, norm_mix, w_mix_in, b_forget, w_pool, pool_scale, conv_w, w_mix_out, norm_ffn2, w_ffn2_in, w_ffn2_out, norm_final, loss_target, m_norm_ffn1, m_w_ffn1_in, m_w_ffn1_out, m_norm_mix, m_w_mix_in, m_b_forget, m_w_pool, m_pool_scale, m_conv_w, m_w_mix_out, m_norm_ffn2, m_w_ffn2_in, m_w_ffn2_out, m_norm_final, v_norm_ffn1, v_w_ffn1_in, v_w_ffn1_out, v_norm_mix, v_w_mix_in, v_b_forget, v_w_pool, v_pool_scale, v_conv_w, v_w_mix_out, v_norm_ffn2, v_w_ffn2_in, v_w_ffn2_out, v_norm_final):
    w = dict(zip(WEIGHTS, (norm_ffn1, w_ffn1_in, w_ffn1_out, norm_mix, w_mix_in, b_forget, w_pool, pool_scale, conv_w, w_mix_out, norm_ffn2, w_ffn2_in, w_ffn2_out, norm_final)))
    m = dict(zip(WEIGHTS, (m_norm_ffn1, m_w_ffn1_in, m_w_ffn1_out, m_norm_mix, m_w_mix_in, m_b_forget, m_w_pool, m_pool_scale, m_conv_w, m_w_mix_out, m_norm_ffn2, m_w_ffn2_in, m_w_ffn2_out, m_norm_final)))
    v = dict(zip(WEIGHTS, (v_norm_ffn1, v_w_ffn1_in, v_w_ffn1_out, v_norm_mix, v_w_mix_in, v_b_forget, v_w_pool, v_pool_scale, v_conv_w, v_w_mix_out, v_norm_ffn2, v_w_ffn2_in, v_w_ffn2_out, v_norm_final)))
    block = 2 * lax.axis_index("x") + lax.axis_index("y")

    pipe = _Pipeline(w)
    small = {k: w[k] for k in SMALL}
    loss, grad_x, sm = _local_step(x, loss_target, small, pipe.conv_w, pipe)
    grads, order = {}, list(SMALL)
    total = _all_reduce_small(_pack([sm[k] for k in order], extra=(loss,)))
    parts, rest = _unpack(total, [sm[k].shape for k in order])
    grads.update(zip(order, parts))
    loss = rest[0]
    cs = conv_w.shape[2]
    grads["conv_w"] = lax.dynamic_slice_in_dim(grads["conv_w"], block * cs, cs, axis=2)

    delta, new_m, new_v = {}, {}, {}
    for i, k in enumerate(BIG):
        two_d = lambda a: a.reshape(-1, a.shape[-1])
        pieces = [tuple(map(two_d, g)) if isinstance(g, tuple) else two_d(g) for g in (layer[i] for layer in pipe.reduced)]
        res = _adamw(two_d(w[k]), pieces, two_d(m[k]), two_d(v[k]))
        grads[k], delta[k], new_m[k], new_v[k] = [r.reshape(w[k].shape) for r in res]
    packed = [_pack([t[k] for k in order]) for t in (w, grads, m, v)]
    _, d, nm, nv = _adamw(packed[0], [packed[1]], packed[2], packed[3])
    shapes = [w[k].shape for k in order]
    for res, flat in ((delta, d), (new_m, nm), (new_v, nv)):
        res.update(zip(order, _unpack(flat, shapes)[0]))
    return (loss, grad_x, *[grads[k] for k in WEIGHTS], *[delta[k] for k in WEIGHTS],
            *[new_m[k] for k in WEIGHTS], *[new_v[k] for k in WEIGHTS])
```

```python
import functools
import math

import jax
import jax.numpy as jnp
from jax import lax
from jax.experimental import pallas as pl
from jax.experimental.pallas import tpu as pltpu

F32 = jnp.float32
BF16 = jnp.bfloat16
MESH = pl.DeviceIdType.MESH

HEAD_DIM = 64
POOL_WINDOWS = (2, 4, 8, 16)
CONV_WIDTH = 3
RMS_EPS = 1e-6
ADAM_LR = 0.001
ADAM_B1 = 0.9
ADAM_B2 = 0.999
ADAM_EPS = 1e-08
ADAM_WD = 0.01
ADAM_STEP = 10

LANES = 128
VMEM_LIMIT = 56 * 1024 * 1024
N_CHIPS = 4
N_DEV = 8

NN = (((1,), (0,)), ((), ()))
NT = (((1,), (1,)), ((), ()))
TN = (((0,), (0,)), ((), ()))


def _tile(n, pref):
    for t in range(pref - pref % 16, 15, -16):
        if n % t == 0:
            return t
    return n


def _params(n_grid):
    return pltpu.CompilerParams(dimension_semantics=("arbitrary",) * n_grid, vmem_limit_bytes=VMEM_LIMIT)


def _dot(a, b, dims):
    return lax.dot_general(a, b, dims, preferred_element_type=F32)


def _mm(name, dims, operands, in_specs, out_shape, out_specs, grid, acc_shape, epilogue):
    n_in, n_out, nk = len(operands), len(out_shape), grid[-1]

    def kern(*refs):
        extras, outs = refs[2:n_in], refs[n_in:n_in + n_out]
        if nk == 1:
            epilogue(_dot(refs[0][...].astype(BF16), refs[1][...].astype(BF16), dims), extras, outs)
            return
        acc = refs[n_in + n_out]
        k = pl.program_id(len(grid) - 1)

        @pl.when(k == 0)
        def _():
            acc[...] = jnp.zeros_like(acc)

        acc[...] += _dot(refs[0][...].astype(BF16), refs[1][...].astype(BF16), dims)

        @pl.when(k == nk - 1)
        def _():
            epilogue(acc[...], extras, outs)

    return pl.pallas_call(
        kern, name=name, grid=grid, in_specs=in_specs, out_specs=out_specs, out_shape=out_shape,
        scratch_shapes=[pltpu.VMEM(acc_shape, F32)] if nk > 1 else [],
        compiler_params=_params(len(grid)),
    )(*operands)


def _store(scale=None, dtype=None):
    def ep(acc, extras, outs):
        v = acc if scale is None else acc * scale
        outs[0][...] = v.astype(outs[0].dtype)
    return ep


def _residual(scale):
    def ep(acc, extras, outs):
        outs[0][...] = extras[0][...] + scale * acc
    return ep


def _rmsnorm_fwd(x, g):
    T, D = x.shape
    tr = _tile(T, 512)

    def kern(x_ref, g_ref, o_ref):
        xv = x_ref[...]
        r = lax.rsqrt(jnp.mean(xv * xv, axis=-1, keepdims=True) + RMS_EPS)
        o_ref[...] = (xv * r * g_ref[...]).astype(BF16)

    return pl.pallas_call(
        kern, name="rmsnorm_fwd", grid=(T // tr,),
        in_specs=[pl.BlockSpec((tr, D), lambda i: (i, 0)), pl.BlockSpec((1, D), lambda i: (0, 0))],
        out_specs=pl.BlockSpec((tr, D), lambda i: (i, 0)),
        out_shape=jax.ShapeDtypeStruct((T, D), BF16), compiler_params=_params(1),
    )(x, g)


def _rmsnorm_bwd(x, g, dh, dres):
    T, D = x.shape
    tr = _tile(T, 256)

    def kern(x_ref, g_ref, dh_ref, dres_ref, dx_ref, dg_ref):
        xv, dhv = x_ref[...], dh_ref[...]
        r = lax.rsqrt(jnp.mean(xv * xv, axis=-1, keepdims=True) + RMS_EPS)
        y = xv * r
        dy = dhv * g_ref[...]
        dx_ref[...] = dres_ref[...] + r * (dy - y * jnp.mean(dy * y, axis=-1, keepdims=True))
        part = jnp.sum(dhv * y, axis=0, keepdims=True)

        @pl.when(pl.program_id(0) == 0)
        def _():
            dg_ref[...] = part

        @pl.when(pl.program_id(0) > 0)
        def _():
            dg_ref[...] += part

    row = pl.BlockSpec((tr, D), lambda i: (i, 0))
    vec = pl.BlockSpec((1, D), lambda i: (0, 0))
    return pl.pallas_call(
        kern, name="rmsnorm_bwd", grid=(T // tr,), in_specs=[row, vec, row, row], out_specs=[row, vec],
        out_shape=[jax.ShapeDtypeStruct((T, D), F32), jax.ShapeDtypeStruct((1, D), F32)],
        compiler_params=_params(1),
    )(x, g, dh, dres)


def _final_loss(x, g, target):
    T, D = x.shape
    tr = _tile(T, 256)

    def kern(x_ref, g_ref, t_ref, dx_ref, dg_ref, loss_ref):
        xv = x_ref[...]
        r = lax.rsqrt(jnp.mean(xv * xv, axis=-1, keepdims=True) + RMS_EPS)
        y = xv * r
        err = y * g_ref[...] - t_ref[...]
        lpart = 0.5 * jnp.sum(jnp.mean(err * err, axis=-1, keepdims=True), axis=0, keepdims=True)
        dh = err * (1.0 / D)
        dy = dh * g_ref[...]
        dx_ref[...] = r * (dy - y * jnp.mean(dy * y, axis=-1, keepdims=True))
        part = jnp.sum(dh * y, axis=0, keepdims=True)
        lrow = jnp.broadcast_to(lpart, (1, LANES))

        @pl.when(pl.program_id(0) == 0)
        def _():
            dg_ref[...] = part
            loss_ref[...] = lrow

        @pl.when(pl.program_id(0) > 0)
        def _():
            dg_ref[...] += part
            loss_ref[...] += lrow

    row = pl.BlockSpec((tr, D), lambda i: (i, 0))
    vec = pl.BlockSpec((1, D), lambda i: (0, 0))
    return pl.pallas_call(
        kern, name="final_loss", grid=(T // tr,), in_specs=[row, vec, row],
        out_specs=[row, vec, pl.BlockSpec((1, LANES), lambda i: (0, 0))],
        out_shape=[jax.ShapeDtypeStruct((T, D), F32), jax.ShapeDtypeStruct((1, D), F32),
                   jax.ShapeDtypeStruct((1, LANES), F32)],
        compiler_params=_params(1),
    )(x, g, target)


def _ffn_in(h, w4):
    T, D = h.shape
    w4, l = w4
    Fh = w4.shape[3]
    F = 2 * Fh
    tm = _tile(T, 512)

    def kern(h_ref, wg_ref, wu_ref, jac_ref, act_ref):
        hv = h_ref[...]
        gate = _dot(hv, wg_ref[...], NN)
        up = _dot(hv, wu_ref[...], NN)
        sg = jax.nn.sigmoid(gate)
        silu = gate * sg
        jac_ref[0] = (up * (sg + silu * (1.0 - sg))).astype(BF16)
        jac_ref[1] = silu.astype(BF16)
        act_ref[...] = (silu * up).astype(BF16)

    return pl.pallas_call(
        kern, name="ffn_in", grid=(2, T // tm),
        in_specs=[pl.BlockSpec((tm, D), lambda j, i: (i, 0)),
                  pl.BlockSpec((None, None, D, Fh), lambda j, i: (l, j, 0, 0)),
                  pl.BlockSpec((None, None, D, Fh), lambda j, i: (l, 2 + j, 0, 0))],
        out_specs=[pl.BlockSpec((2, tm, Fh), lambda j, i: (0, i, j)),
                   pl.BlockSpec((tm, Fh), lambda j, i: (i, j))],
        out_shape=[jax.ShapeDtypeStruct((2, T, F), BF16), jax.ShapeDtypeStruct((T, F), BF16)],
        compiler_params=_params(2),
    )(h, w4, w4)


def _resident(shape, index_map):
    return pl.BlockSpec(shape, index_map, pipeline_mode=pl.Buffered(1))


def _token_operand(token):
    return ([], []) if token is None else ([token], [pl.BlockSpec(token.shape, lambda i: (0, 0))])


def _ffn_up(x, g, w4, token=None):
    T, D = x.shape
    w4, l = w4
    Fh = w4.shape[3]
    F = 2 * Fh
    tm = _tile(T, 512)
    tok_ops, tok_specs = _token_operand(token)

    def kern(x_ref, g_ref, w_ref, *rest):
        h_ref, jac_ref, act_ref = rest[len(tok_ops):]
        xv = x_ref[...]
        r = lax.rsqrt(jnp.mean(xv * xv, axis=-1, keepdims=True) + RMS_EPS)
        hv = (xv * r * g_ref[...]).astype(BF16)
        h_ref[...] = hv
        for j in range(2):
            cols = slice(j * Fh, (j + 1) * Fh)
            gate = _dot(hv, w_ref[j], NN)
            up = _dot(hv, w_ref[2 + j], NN)
            sg = jax.nn.sigmoid(gate)
            silu = gate * sg
            jac_ref[0, :, cols] = (up * (sg + silu * (1.0 - sg))).astype(BF16)
            jac_ref[1, :, cols] = silu.astype(BF16)
            act_ref[:, cols] = (silu * up).astype(BF16)

    return pl.pallas_call(
        kern, name="ffn_up", grid=(T // tm,),
        in_specs=[pl.BlockSpec((tm, D), lambda i: (i, 0)), pl.BlockSpec((1, D), lambda i: (0, 0)),
                  _resident((None, 4, D, Fh), lambda i: (l, 0, 0, 0))] + tok_specs,
        out_specs=[pl.BlockSpec((tm, D), lambda i: (i, 0)), pl.BlockSpec((2, tm, F), lambda i: (0, i, 0)),
                   pl.BlockSpec((tm, F), lambda i: (i, 0))],
        out_shape=[jax.ShapeDtypeStruct((T, D), BF16), jax.ShapeDtypeStruct((2, T, F), BF16),
                   jax.ShapeDtypeStruct((T, F), BF16)],
        compiler_params=_params(1),
    )(x, g, w4, *tok_ops)


def _ffn_bwd_main(dres, jac, x, g, w_out, w4, token=None):
    T, D = dres.shape
    w_out, l = w_out
    w4, _ = w4
    F = w_out.shape[1]
    Fh = F // 2
    tm = _tile(T, 256)
    tok_ops, tok_specs = _token_operand(token)

    def kern(d_ref, jac_ref, x_ref, g_ref, wo_ref, wi_ref, *rest):
        dgu_ref, dx_ref, dg_ref = rest[len(tok_ops):]
        dv = d_ref[...]
        d16 = dv.astype(BF16)
        dh = jnp.zeros((tm, D), F32)
        for j in range(2):
            cols = slice(j * Fh, (j + 1) * Fh)
            dact = 0.5 * _dot(d16, wo_ref[cols, :], NT)
            dgate = (dact * jac_ref[0, :, cols].astype(F32)).astype(BF16)
            dup = (dact * jac_ref[1, :, cols].astype(F32)).astype(BF16)
            dgu_ref[0, :, cols] = dgate
            dgu_ref[1, :, cols] = dup
            dh = dh + _dot(dgate, wi_ref[j], NT) + _dot(dup, wi_ref[2 + j], NT)
        xv = x_ref[...]
        r = lax.rsqrt(jnp.mean(xv * xv, axis=-1, keepdims=True) + RMS_EPS)
        y = xv * r
        dy = dh * g_ref[...]
        dx_ref[...] = dv + r * (dy - y * jnp.mean(dy * y, axis=-1, keepdims=True))
        part = jnp.sum(dh * y, axis=0, keepdims=True)

        @pl.when(pl.program_id(0) == 0)
        def _():
            dg_ref[...] = part

        @pl.when(pl.program_id(0) > 0)
        def _():
            dg_ref[...] += part

    row = pl.BlockSpec((tm, D), lambda i: (i, 0))
    vec = pl.BlockSpec((1, D), lambda i: (0, 0))
    wide = pl.BlockSpec((2, tm, F), lambda i: (0, i, 0))
    return pl.pallas_call(
        kern, name="ffn_bwd_main", grid=(T // tm,),
        in_specs=[row, wide, row, vec, _resident((None, F, D), lambda i: (l, 0, 0)),
                  _resident((None, 4, D, Fh), lambda i: (l, 0, 0, 0))] + tok_specs,
        out_specs=[wide, row, vec],
        out_shape=[jax.ShapeDtypeStruct((2, T, F), BF16), jax.ShapeDtypeStruct((T, D), F32),
                   jax.ShapeDtypeStruct((1, D), F32)],
        compiler_params=_params(1),
    )(dres, jac, x, g, w_out, w4, *tok_ops)


def _ffn_out(act, w_out, x):
    T, F = act.shape
    w_out, l = w_out
    D = w_out.shape[2]
    tm = _tile(T, 512)
    return _mm("ffn_out", NN, [act, w_out, x],
               [pl.BlockSpec((tm, F), lambda i, k: (i, 0)), pl.BlockSpec((None, F, D), lambda i, k: (l, 0, 0)),
                pl.BlockSpec((tm, D), lambda i, k: (i, 0))],
               [jax.ShapeDtypeStruct((T, D), F32)], [pl.BlockSpec((tm, D), lambda i, k: (i, 0))],
               (T // tm, 1), None, _residual(0.5))


def _ffn_bwd_act(dres, w_out, jac):
    T, D = dres.shape
    w_out, l = w_out
    F = w_out.shape[1]
    Fh = F // 2
    tm = _tile(T, 512)

    def kern(d_ref, w_ref, jac_ref, o_ref):
        dact = 0.5 * _dot(d_ref[...].astype(BF16), w_ref[...], NT)
        o_ref[0] = (dact * jac_ref[0].astype(F32)).astype(BF16)
        o_ref[1] = (dact * jac_ref[1].astype(F32)).astype(BF16)

    return pl.pallas_call(
        kern, name="ffn_bwd_act", grid=(2, T // tm),
        in_specs=[pl.BlockSpec((tm, D), lambda j, i: (i, 0)), pl.BlockSpec((None, Fh, D), lambda j, i: (l, j, 0)),
                  pl.BlockSpec((2, tm, Fh), lambda j, i: (0, i, j))],
        out_specs=pl.BlockSpec((2, tm, Fh), lambda j, i: (0, i, j)),
        out_shape=jax.ShapeDtypeStruct((2, T, F), BF16), compiler_params=_params(2),
    )(dres, w_out, jac)


def _ffn_dw_out(act, dres):
    T, F = act.shape
    D = dres.shape[1]
    tm, tk = F // 2, _tile(T, 1024)
    return _mm("ffn_dw_out", TN, [act, dres],
               [pl.BlockSpec((tk, tm), lambda i, k: (k, i)), pl.BlockSpec((tk, D), lambda i, k: (k, 0))],
               [jax.ShapeDtypeStruct((F, D), BF16)], [pl.BlockSpec((tm, D), lambda i, k: (i, 0))],
               (2, T // tk), (tm, D), _store(0.5))


def _ffn_dw_in(h, dgu):
    T, D = h.shape
    Fh = dgu.shape[2] // 2
    tk = _tile(T, 1024)
    return _mm("ffn_dw_in", TN, [h, dgu],
               [pl.BlockSpec((tk, D), lambda j, k: (k, 0)),
                pl.BlockSpec((None, tk, Fh), lambda j, k: (j // 2, k, j % 2))],
               [jax.ShapeDtypeStruct((4, D, Fh), BF16)], [pl.BlockSpec((None, D, Fh), lambda j, k: (j, 0, 0))],
               (4, T // tk), (D, Fh), _store())


def _ffn_dh(dgu, w4):
    T = dgu.shape[1]
    w4, l = w4
    D, Fh = w4.shape[2], w4.shape[3]
    tm = _tile(T, 1024)
    return _mm("ffn_dh", NT, [dgu, w4],
               [pl.BlockSpec((None, tm, Fh), lambda i, k: (k // 2, i, k % 2)),
                pl.BlockSpec((None, None, D, Fh), lambda i, k: (l, k, 0, 0))],
               [jax.ShapeDtypeStruct((T, D), F32)], [pl.BlockSpec((tm, D), lambda i, k: (i, 0))],
               (T // tm, 4), (tm, D), _store())


def _proj(name, a, w, out_dtype, dims=NN, extra=None, scale=None):
    T, K = a.shape
    w, l = w
    N = w.shape[2] if dims == NN else w.shape[1]
    tm = _tile(T, 512)
    ops = [a, w] + ([extra] if extra is not None else [])
    specs = [pl.BlockSpec((tm, K), lambda i, k: (i, 0)), pl.BlockSpec((None,) + w.shape[1:], lambda i, k: (l, 0, 0))]
    if extra is not None:
        specs.append(pl.BlockSpec((tm, N), lambda i, k: (i, 0)))
    ep = _residual(1.0) if extra is not None else _store(scale)
    return _mm(name, dims, ops, specs, [jax.ShapeDtypeStruct((T, N), out_dtype)],
               [pl.BlockSpec((tm, N), lambda i, k: (i, 0))], (T // tm, 1), None, ep)[0]


def _mix_up(x, g, wp, widths):
    T, D = x.shape
    wp, l = wp
    n_qkv, n_rest = widths
    NP = wp.shape[2]
    tm = _tile(T, 512)

    def kern(x_ref, g_ref, w_ref, h_ref, qkv_ref, rest_ref, fl_ref):
        xv = x_ref[...]
        r = lax.rsqrt(jnp.mean(xv * xv, axis=-1, keepdims=True) + RMS_EPS)
        hv = (xv * r * g_ref[...]).astype(BF16)
        h_ref[...] = hv
        qkv_ref[...] = _dot(hv, w_ref[:, 0:n_qkv], NN).astype(BF16)
        rest_ref[...] = _dot(hv, w_ref[:, n_qkv:n_qkv + n_rest], NN)
        fl_ref[...] = _dot(hv, w_ref[:, n_qkv + n_rest:NP], NN)

    row = lambda n: pl.BlockSpec((tm, n), lambda i: (i, 0))
    return pl.pallas_call(
        kern, name="mix_up", grid=(T // tm,),
        in_specs=[row(D), pl.BlockSpec((1, D), lambda i: (0, 0)), _resident((None, D, NP), lambda i: (l, 0, 0))],
        out_specs=[row(D), row(n_qkv), row(n_rest), row(LANES)],
        out_shape=[jax.ShapeDtypeStruct((T, D), BF16), jax.ShapeDtypeStruct((T, n_qkv), BF16),
                   jax.ShapeDtypeStruct((T, n_rest), F32), jax.ShapeDtypeStruct((T, LANES), F32)],
        compiler_params=_params(1),
    )(x, g, wp)


def _column_starts(pieces):
    starts, at = [], 0
    for p in pieces:
        starts.append(at)
        at += p.shape[1]
    return starts


def _mix_in_bwd(pieces, x, g, dres, wp):
    T, D = x.shape
    wp, l = wp
    NP = wp.shape[2]
    tm = _tile(T, 512)
    n, starts = len(pieces), _column_starts(pieces)

    def kern(*refs):
        x_ref, g_ref, d_ref, w_ref, dx_ref, dg_ref = refs[n:]
        dh = jnp.zeros((tm, D), F32)
        for p_ref, at in zip(refs[:n], starts):
            dh = dh + _dot(p_ref[...].astype(BF16), w_ref[:, at:at + p_ref.shape[1]], NT)
        xv = x_ref[...]
        r = lax.rsqrt(jnp.mean(xv * xv, axis=-1, keepdims=True) + RMS_EPS)
        y = xv * r
        dy = dh * g_ref[...]
        dx_ref[...] = d_ref[...] + r * (dy - y * jnp.mean(dy * y, axis=-1, keepdims=True))
        part = jnp.sum(dh * y, axis=0, keepdims=True)

        @pl.when(pl.program_id(0) == 0)
        def _():
            dg_ref[...] = part

        @pl.when(pl.program_id(0) > 0)
        def _():
            dg_ref[...] += part

    row = lambda n: pl.BlockSpec((tm, n), lambda i: (i, 0))
    vec = pl.BlockSpec((1, D), lambda i: (0, 0))
    return pl.pallas_call(
        kern, name="mix_in_bwd", grid=(T // tm,),
        in_specs=[row(p.shape[1]) for p in pieces] + [row(D), vec, row(D), _resident((None, D, NP), lambda i: (l, 0, 0))],
        out_specs=[row(D), vec],
        out_shape=[jax.ShapeDtypeStruct((T, D), F32), jax.ShapeDtypeStruct((1, D), F32)],
        compiler_params=_params(1),
    )(*pieces, x, g, dres, wp)


def _pieces_dw(name, a, pieces, out_dtype, tk_pref):
    T, M = a.shape
    n, starts = len(pieces), _column_starts(pieces)
    N = starts[-1] + pieces[-1].shape[1]
    tk = _tile(T, tk_pref)
    nk = T // tk

    def kern(a_ref, *refs):
        o_ref, acc = refs[n], refs[n + 1]
        k = pl.program_id(0)

        @pl.when(k == 0)
        def _():
            acc[...] = jnp.zeros_like(acc)

        av = a_ref[...].astype(BF16)
        for p_ref, at in zip(refs[:n], starts):
            acc[:, at:at + p_ref.shape[1]] += _dot(av, p_ref[...].astype(BF16), TN)

        @pl.when(k == nk - 1)
        def _():
            o_ref[...] = acc[...].astype(out_dtype)

    return pl.pallas_call(
        kern, name=name, grid=(nk,),
        in_specs=[pl.BlockSpec((tk, M), lambda k: (k, 0))] + [pl.BlockSpec((tk, p.shape[1]), lambda k: (k, 0)) for p in pieces],
        out_specs=pl.BlockSpec((M, N), lambda k: (0, 0)), out_shape=jax.ShapeDtypeStruct((M, N), out_dtype),
        scratch_shapes=[pltpu.VMEM((M, N), F32)], compiler_params=_params(1),
    )(a, *pieces)


def _rows_dw(name, pieces, d, out_dtype):
    T, N = d.shape
    n, starts = len(pieces), _column_starts(pieces)
    M = starts[-1] + pieces[-1].shape[1]
    tk = _tile(T, 1024)
    nk = T // tk

    def kern(*refs):
        d_ref, o_ref, acc = refs[n], refs[n + 1], refs[n + 2]
        k = pl.program_id(0)

        @pl.when(k == 0)
        def _():
            acc[...] = jnp.zeros_like(acc)

        dv = d_ref[...].astype(BF16)
        for p_ref, at in zip(refs[:n], starts):
            acc[at:at + p_ref.shape[1], :] += _dot(p_ref[...], dv, TN)

        @pl.when(k == nk - 1)
        def _():
            o_ref[...] = acc[...].astype(out_dtype)

    return pl.pallas_call(
        kern, name=name, grid=(nk,),
        in_specs=[pl.BlockSpec((tk, p.shape[1]), lambda k: (k, 0)) for p in pieces] + [pl.BlockSpec((tk, N), lambda k: (k, 0))],
        out_specs=pl.BlockSpec((M, N), lambda k: (0, 0)), out_shape=jax.ShapeDtypeStruct((M, N), out_dtype),
        scratch_shapes=[pltpu.VMEM((M, N), F32)], compiler_params=_params(1),
    )(*pieces, d)


def _mix_out(pieces, w, x):
    T, D = x.shape
    w, l = w
    n, starts = len(pieces), _column_starts(pieces)
    tm = _tile(T, 512)

    def kern(*refs):
        w_ref, x_ref, o_ref = refs[n:]
        acc = x_ref[...]
        for p_ref, at in zip(refs[:n], starts):
            acc = acc + _dot(p_ref[...], w_ref[at:at + p_ref.shape[1], :], NN)
        o_ref[...] = acc

    row = lambda m: pl.BlockSpec((tm, m), lambda i: (i, 0))
    return pl.pallas_call(
        kern, name="mix_out", grid=(T // tm,),
        in_specs=[row(p.shape[1]) for p in pieces] + [_resident((None,) + w.shape[1:], lambda i: (l, 0, 0)), row(D)],
        out_specs=row(D), out_shape=jax.ShapeDtypeStruct((T, D), F32), compiler_params=_params(1),
    )(*pieces, w, x)


def _dw(name, a, d, out_dtype):
    T, M = a.shape
    N = d.shape[1]
    tk = _tile(T, 1024 if M * N <= 1024 * 1408 else 512)
    return _mm(name, TN, [a, d],
               [pl.BlockSpec((tk, M), lambda i, k: (k, 0)), pl.BlockSpec((tk, N), lambda i, k: (k, 0))],
               [jax.ShapeDtypeStruct((M, N), out_dtype)], [pl.BlockSpec((M, N), lambda i, k: (0, 0))],
               (1, T // tk), (M, N), _store())[0]


def _log_sigmoid(z):
    return jnp.minimum(z, 0.0) - jnp.log(1.0 + jnp.exp(-jnp.abs(z)))


def _decay_fwd(fl, bias):
    B, S, _ = fl.shape

    def kern(fl_ref, b_ref, o_ref):
        d = _log_sigmoid(fl_ref[...] + b_ref[...])
        row = lax.broadcasted_iota(jnp.int32, (S, LANES), 0)
        sh = 1
        while sh < S:
            d = d + jnp.where(row >= sh, pltpu.roll(d, sh, 0), 0.0)
            sh *= 2
        o_ref[...] = d.T[0:8, :]

    return pl.pallas_call(
        kern, name="decay_fwd", grid=(B,),
        in_specs=[pl.BlockSpec((None, S, LANES), lambda b: (b, 0, 0)), pl.BlockSpec((1, LANES), lambda b: (0, 0))],
        out_specs=pl.BlockSpec((None, 8, S), lambda b: (b, 0, 0)),
        out_shape=jax.ShapeDtypeStruct((B, 8, S), F32), compiler_params=_params(1),
    )(fl, bias)


def _decay_bwd(ddrow, ddcol, fl, bias, n_heads):
    B, S, _ = fl.shape

    def kern(dd_ref, ddc_ref, fl_ref, b_ref, dfl_ref, db_ref):
        dd = jnp.concatenate([dd_ref[...], jnp.zeros((LANES - 8, S), F32)], axis=0).T + ddc_ref[...]
        row = lax.broadcasted_iota(jnp.int32, (S, LANES), 0)
        lane = lax.broadcasted_iota(jnp.int32, (S, LANES), 1)
        sh = 1
        while sh < S:
            dd = dd + jnp.where(row < S - sh, pltpu.roll(dd, S - sh, 0), 0.0)
            sh *= 2
        z = fl_ref[...] + b_ref[...]
        dfl = jnp.where(lane < n_heads, dd / (1.0 + jnp.exp(z)), 0.0)
        dfl_ref[...] = dfl
        part = jnp.sum(dfl, axis=0, keepdims=True)

        @pl.when(pl.program_id(0) == 0)
        def _():
            db_ref[...] = part

        @pl.when(pl.program_id(0) > 0)
        def _():
            db_ref[...] += part

    return pl.pallas_call(
        kern, name="decay_bwd", grid=(B,),
        in_specs=[pl.BlockSpec((None, 8, S), lambda b: (b, 0, 0)), pl.BlockSpec((None, S, LANES), lambda b: (b, 0, 0)),
                  pl.BlockSpec((None, S, LANES), lambda b: (b, 0, 0)), pl.BlockSpec((1, LANES), lambda b: (0, 0))],
        out_specs=[pl.BlockSpec((None, S, LANES), lambda b: (b, 0, 0)), pl.BlockSpec((1, LANES), lambda b: (0, 0))],
        out_shape=[jax.ShapeDtypeStruct((B, S, LANES), F32), jax.ShapeDtypeStruct((1, LANES), F32)],
        compiler_params=_params(1),
    )(ddrow, ddcol, fl, bias)


def _attn_fwd(qkv, drow, n_heads, tq):
    B, S, _ = qkv.shape
    DA = n_heads * HEAD_DIM
    scale = HEAD_DIM ** -0.5

    n_pairs = n_heads // 2

    def kern(q_ref, k_ref, v_ref, dr_ref, o_ref, lse_ref):
        i = pl.program_id(1)
        lane = lax.broadcasted_iota(jnp.int32, (tq, LANES), 1)
        low = lane < HEAD_DIM
        causal = lax.broadcasted_iota(jnp.int32, (tq, tq), 1) <= lax.broadcasted_iota(jnp.int32, (tq, tq), 0)
        qms = []
        for p in range(n_pairs):
            q2 = q_ref[:, LANES * p:LANES * (p + 1)] * scale
            qms += [jnp.where(low, q2, jnp.zeros_like(q2)), jnp.where(low, jnp.zeros_like(q2), q2)]

        def step(j, carry, masked):
            ms, ls, accs = carry
            ks = pl.multiple_of(j * tq, tq)
            new_m, new_l, new_acc = [], [], []
            for p in range(n_pairs):
                cols = slice(LANES * p, LANES * (p + 1))
                k2, v2 = k_ref[pl.ds(ks, tq), cols], v_ref[pl.ds(ks, tq), cols]
                alphas, pvs = [], []
                for h in (2 * p, 2 * p + 1):
                    s = _dot(qms[h], k2, NT) - dr_ref[h, pl.ds(j, 1), :]
                    if masked:
                        s = jnp.where(causal, s, -jnp.inf)
                    m_new = jnp.maximum(ms[h], jnp.max(s, axis=1, keepdims=True))
                    alpha = jnp.exp(ms[h] - m_new)
                    pm = jnp.exp(s - m_new)
                    new_m.append(m_new)
                    new_l.append(alpha * ls[h] + jnp.sum(pm, axis=1, keepdims=True))
                    alphas.append(alpha)
                    pvs.append(_dot(pm.astype(BF16), v2, NN))
                new_acc.append(jnp.where(low, alphas[0], alphas[1]) * accs[p] + jnp.where(low, pvs[0], pvs[1]))
            return tuple(new_m), tuple(new_l), tuple(new_acc)

        init = (tuple(jnp.full((tq, 1), -jnp.inf, F32) for _ in range(n_heads)),
                tuple(jnp.zeros((tq, 1), F32) for _ in range(n_heads)),
                tuple(jnp.zeros((tq, LANES), F32) for _ in range(n_pairs)))
        ms, ls, accs = step(i, lax.fori_loop(0, i, functools.partial(step, masked=False), init), True)
        lse_mat = jnp.zeros((tq, LANES), F32)
        for p in range(n_pairs):
            l0, l1 = ls[2 * p], ls[2 * p + 1]
            o_ref[:, LANES * p:LANES * (p + 1)] = (accs[p] / jnp.where(low, l0, l1)).astype(BF16)
            lse_mat = jnp.where(lane == 2 * p, ms[2 * p] + jnp.log(l0), lse_mat)
            lse_mat = jnp.where(lane == 2 * p + 1, ms[2 * p + 1] + jnp.log(l1), lse_mat)
        lse_ref[...] = lse_mat

    nq = S // tq
    return pl.pallas_call(
        kern, name="attn_fwd", grid=(B, nq),
        in_specs=[pl.BlockSpec((None, tq, DA), lambda b, i: (b, i, 0)),
                  pl.BlockSpec((None, S, DA), lambda b, i: (b, 0, 1)),
                  pl.BlockSpec((None, S, DA), lambda b, i: (b, 0, 2)),
                  pl.BlockSpec((None, 8, nq, tq), lambda b, i: (b, 0, 0, 0))],
        out_specs=[pl.BlockSpec((None, tq, DA), lambda b, i: (b, i, 0)),
                   pl.BlockSpec((None, tq, LANES), lambda b, i: (b, i, 0))],
        out_shape=[jax.ShapeDtypeStruct((B, S, DA), BF16), jax.ShapeDtypeStruct((B, S, LANES), F32)],
        compiler_params=_params(2),
    )(qkv, qkv, qkv, drow)


def _attn_bwd(qkv, drow, o, lse, dycat, n_heads, tq):
    B, S, _ = qkv.shape
    DA = n_heads * HEAD_DIM
    scale = HEAD_DIM ** -0.5
    nq = S // tq

    n_pairs = n_heads // 2

    def kern(q_ref, k_ref, v_ref, dr_ref, o_ref, lse_ref, do_ref, dq_ref, dk_ref, dv_ref, ddr_ref, ddc_ref,
             dk_acc, dv_acc, qm_s, dom_s, delta_s, rs_s, dq_s):
        i = pl.program_id(1)

        @pl.when(i == 0)
        def _():
            dk_acc[...] = jnp.zeros_like(dk_acc)
            dv_acc[...] = jnp.zeros_like(dv_acc)
            ddr_ref[...] = jnp.zeros_like(ddr_ref)

        lane = lax.broadcasted_iota(jnp.int32, (tq, LANES), 1)
        low = lane < HEAD_DIM
        causal = lax.broadcasted_iota(jnp.int32, (tq, tq), 1) <= lax.broadcasted_iota(jnp.int32, (tq, tq), 0)
        for p in range(n_pairs):
            cols = slice(LANES * p, LANES * (p + 1))
            q2 = q_ref[:, cols] * scale
            do_f = do_ref[:, cols]
            do2 = do_f.astype(BF16)
            prod = do_f * o_ref[:, cols].astype(F32)
            qm_s[2 * p] = jnp.where(low, q2, jnp.zeros_like(q2))
            qm_s[2 * p + 1] = jnp.where(low, jnp.zeros_like(q2), q2)
            dom_s[2 * p] = jnp.where(low, do2, jnp.zeros_like(do2))
            dom_s[2 * p + 1] = jnp.where(low, jnp.zeros_like(do2), do2)
            delta_s[2 * p] = jnp.sum(jnp.where(low, prod, 0.0), axis=1, keepdims=True)
            delta_s[2 * p + 1] = jnp.sum(jnp.where(low, 0.0, prod), axis=1, keepdims=True)
            dq_s[p] = jnp.zeros((tq, LANES), F32)
        rs_s[...] = jnp.zeros(rs_s.shape, F32)

        def step(j, masked):
            ks = pl.multiple_of(j * tq, tq)
            for p in range(n_pairs):
                cols = slice(LANES * p, LANES * (p + 1))
                k2, v2 = k_ref[pl.ds(ks, tq), cols], v_ref[pl.ds(ks, tq), cols]
                dvs, dks, dqs = [], [], []
                for h in (2 * p, 2 * p + 1):
                    qm, dom = qm_s[h], dom_s[h]
                    s = _dot(qm, k2, NT) - dr_ref[h, pl.ds(j, 1), :]
                    if masked:
                        s = jnp.where(causal, s, -jnp.inf)
                    pm = jnp.exp(s - lse_ref[:, h:h + 1])
                    ds = pm * (_dot(dom, v2, NT) - delta_s[h])
                    ddr_ref[h, pl.ds(j, 1), :] -= jnp.sum(ds, axis=0, keepdims=True)
                    rs_s[h] += jnp.sum(ds, axis=1, keepdims=True)
                    dsb = ds.astype(BF16)
                    dvs.append(_dot(pm.astype(BF16), dom, TN))
                    dks.append(_dot(dsb, qm, TN))
                    dqs.append(_dot(dsb, k2, NN))
                dv_acc[pl.ds(ks, tq), cols] += dvs[0] + dvs[1]
                dk_acc[pl.ds(ks, tq), cols] += dks[0] + dks[1]
                dq_s[p] += jnp.where(low, dqs[0], dqs[1])

        def body(j, carry):
            step(j, False)
            return carry

        lax.fori_loop(0, i, body, 0)
        step(i, True)
        ddc = jnp.zeros((tq, LANES), F32)
        for p in range(n_pairs):
            dq_ref[:, LANES * p:LANES * (p + 1)] = (dq_s[p] * scale).astype(BF16)
            ddc = jnp.where(lane == 2 * p, rs_s[2 * p], ddc)
            ddc = jnp.where(lane == 2 * p + 1, rs_s[2 * p + 1], ddc)
        ddc_ref[...] = ddc

        @pl.when(i == nq - 1)
        def _():
            dk_ref[...] = dk_acc[...].astype(BF16)
            dv_ref[...] = dv_acc[...].astype(BF16)

    tile = pl.BlockSpec((None, tq, DA), lambda b, i: (b, i, 0))
    seq = pl.BlockSpec((None, S, DA), lambda b, i: (b, 0, 0))
    dec = pl.BlockSpec((None, 8, nq, tq), lambda b, i: (b, 0, 0, 0))
    return pl.pallas_call(
        kern, name="attn_bwd", grid=(B, nq),
        in_specs=[tile, pl.BlockSpec((None, S, DA), lambda b, i: (b, 0, 1)),
                  pl.BlockSpec((None, S, DA), lambda b, i: (b, 0, 2)), dec, tile,
                  pl.BlockSpec((None, tq, LANES), lambda b, i: (b, i, 0)), tile],
        out_specs=[tile, seq, seq, dec, pl.BlockSpec((None, tq, LANES), lambda b, i: (b, i, 0))],
        out_shape=[jax.ShapeDtypeStruct((B, S, DA), BF16)] * 3 + [jax.ShapeDtypeStruct((B, 8, nq, tq), F32),
                                                                  jax.ShapeDtypeStruct((B, S, LANES), F32)],
        scratch_shapes=[pltpu.VMEM((S, DA), F32), pltpu.VMEM((S, DA), F32),
                        pltpu.VMEM((n_heads, tq, LANES), BF16), pltpu.VMEM((n_heads, tq, LANES), BF16),
                        pltpu.VMEM((n_heads, tq, 1), F32), pltpu.VMEM((n_heads, tq, 1), F32),
                        pltpu.VMEM((n_pairs, tq, LANES), F32)],
        compiler_params=_params(2),
    )(qkv, qkv, qkv, drow, o, lse, dycat)


def _down(v, d, row):
    return jnp.where(row >= d, pltpu.roll(v, d, 0), 0.0)


def _up(v, d, row, S):
    return jnp.where(row < S - d, pltpu.roll(v, S - d, 0), 0.0)


def _window(v, shift, group):
    sums, acc, d = [], v, 1
    for _ in POOL_WINDOWS:
        acc = acc + shift(acc, d)
        sums.append(acc)
        d *= 2
    out = sums[-1]
    for gi in range(len(POOL_WINDOWS) - 2, -1, -1):
        out = jnp.where(group == gi, sums[gi], out)
    return out


def _pool_count(row, group):
    w = jnp.full(row.shape, POOL_WINDOWS[-1], jnp.int32)
    for gi in range(len(POOL_WINDOWS) - 2, -1, -1):
        w = jnp.where(group == gi, POOL_WINDOWS[gi], w)
    return jnp.minimum(row + 1, w).astype(F32)


def _mix_local_fwd(rest, wbd, ps, cw):
    B, S, C4 = rest.shape
    C = C4 // 4
    gw = C // len(POOL_WINDOWS)

    def kern(r_ref, w_ref, ps_ref, cw_ref, y_ref, pooled_ref):
        row = lax.broadcasted_iota(jnp.int32, (S, C), 0)
        group = lax.broadcasted_iota(jnp.int32, (S, C), 1) // gw
        u = r_ref[:, 0:C]
        pooled = _window(u, lambda v, d: _down(v, d, row), group) / _pool_count(row, group) - u
        pb = pooled.astype(BF16)
        pooled_ref[...] = pb
        y_ref[:, 0:C] = (_dot(pb, w_ref[...], NN) * ps_ref[...]).astype(BF16)
        uc = r_ref[:, 2 * C:3 * C] * r_ref[:, 3 * C:4 * C]
        y = cw_ref[0:1, :] * _down(uc, 2, row) + cw_ref[1:2, :] * _down(uc, 1, row) + cw_ref[2:3, :] * uc
        y_ref[:, C:2 * C] = (r_ref[:, C:2 * C] * y).astype(BF16)

    return pl.pallas_call(
        kern, name="mix_local_fwd", grid=(B,),
        in_specs=[pl.BlockSpec((None, S, C4), lambda b: (b, 0, 0)), pl.BlockSpec((C, C), lambda b: (0, 0)),
                  pl.BlockSpec((1, C), lambda b: (0, 0)), pl.BlockSpec((8, C), lambda b: (0, 0))],
        out_specs=[pl.BlockSpec((None, S, 2 * C), lambda b: (b, 0, 0)), pl.BlockSpec((None, S, C), lambda b: (b, 0, 0))],
        out_shape=[jax.ShapeDtypeStruct((B, S, 2 * C), BF16), jax.ShapeDtypeStruct((B, S, C), BF16)],
        compiler_params=_params(1),
    )(rest, wbd, ps, cw)


def _mix_local_bwd(rest, pooled, dycat, wbd, ps, cw):
    B, S, C4 = rest.shape
    C = C4 // 4
    gw = C // len(POOL_WINDOWS)

    def kern(r_ref, pooled_ref, d_ref, w_ref, ps_ref, cw_ref, dr_ref, dw_ref, dps_ref, dcw_ref):
        row = lax.broadcasted_iota(jnp.int32, (S, C), 0)
        group = lax.broadcasted_iota(jnp.int32, (S, C), 1) // gw
        dyp = d_ref[:, 0:C]
        dyc = d_ref[:, C:2 * C]
        pb = pooled_ref[...]
        dps = jnp.sum(dyp * _dot(pb, w_ref[...], NN), axis=0, keepdims=True)
        dzb = (dyp * ps_ref[...]).astype(BF16)
        dw = _dot(pb, dzb, TN)
        dpooled = _dot(dzb, w_ref[...], NT)
        g = dpooled / _pool_count(row, group)
        dr_ref[:, 0:C] = (_window(g, lambda v, d: _up(v, d, row, S), group) - dpooled).astype(BF16)
        cc, ch = r_ref[:, 2 * C:3 * C], r_ref[:, 3 * C:4 * C]
        uc = cc * ch
        u1, u2 = _down(uc, 1, row), _down(uc, 2, row)
        y = cw_ref[0:1, :] * u2 + cw_ref[1:2, :] * u1 + cw_ref[2:3, :] * uc
        dr_ref[:, C:2 * C] = (dyc * y).astype(BF16)
        dy = dyc * r_ref[:, C:2 * C]
        duc = cw_ref[0:1, :] * _up(dy, 2, row, S) + cw_ref[1:2, :] * _up(dy, 1, row, S) + cw_ref[2:3, :] * dy
        dr_ref[:, 2 * C:3 * C] = (duc * ch).astype(BF16)
        dr_ref[:, 3 * C:4 * C] = (duc * cc).astype(BF16)
        dcw = jnp.concatenate([jnp.sum(dy * u2, axis=0, keepdims=True), jnp.sum(dy * u1, axis=0, keepdims=True),
                               jnp.sum(dy * uc, axis=0, keepdims=True), jnp.zeros((5, C), F32)], axis=0)

        @pl.when(pl.program_id(0) == 0)
        def _():
            dw_ref[...] = dw
            dps_ref[...] = dps
            dcw_ref[...] = dcw

        @pl.when(pl.program_id(0) > 0)
        def _():
            dw_ref[...] += dw
            dps_ref[...] += dps
            dcw_ref[...] += dcw

    full = lambda shape: pl.BlockSpec(shape, lambda b: (0, 0))
    return pl.pallas_call(
        kern, name="mix_local_bwd", grid=(B,),
        in_specs=[pl.BlockSpec((None, S, C4), lambda b: (b, 0, 0)), pl.BlockSpec((None, S, C), lambda b: (b, 0, 0)),
                  pl.BlockSpec((None, S, 2 * C), lambda b: (b, 0, 1)), full((C, C)), full((1, C)), full((8, C))],
        out_specs=[pl.BlockSpec((None, S, C4), lambda b: (b, 0, 0)), full((C, C)), full((1, C)), full((8, C))],
        out_shape=[jax.ShapeDtypeStruct((B, S, C4), BF16), jax.ShapeDtypeStruct((C, C), F32),
                   jax.ShapeDtypeStruct((1, C), F32), jax.ShapeDtypeStruct((8, C), F32)],
        compiler_params=_params(1),
    )(rest, pooled, dycat, wbd, ps, cw)


def _adamw(w, gs, m, v):
    R, C = w.shape
    pieces = [p if isinstance(p, tuple) else (p,) for p in gs]
    owner = [s for s, p in enumerate(pieces) for _ in p]
    flat = [a for p in pieces for a in p]
    n = len(flat)
    rows = R // len(pieces)
    tr = _tile(rows, 256)
    per = rows // tr

    def kern(w_ref, *refs):
        g_refs, (m_ref, v_ref, g_out, d_ref, nm_ref, nv_ref) = refs[:n], refs[n:]
        vals, at = [], 0
        for p in pieces:
            vals.append(g_refs[at][...] if len(p) == 1 else g_refs[at][...] + g_refs[at + 1][...])
            at += len(p)
        gv = vals[0]
        for s in range(1, len(pieces)):
            gv = jnp.where(pl.program_id(0) // per == s, vals[s], gv)
        nm = ADAM_B1 * m_ref[...] + (1.0 - ADAM_B1) * gv
        nv = ADAM_B2 * v_ref[...] + (1.0 - ADAM_B2) * (gv * gv)
        m_hat = nm / (1.0 - ADAM_B1 ** ADAM_STEP)
        v_hat = nv / (1.0 - ADAM_B2 ** ADAM_STEP)
        g_out[...] = gv
        d_ref[...] = -ADAM_LR * (m_hat / (jnp.sqrt(v_hat) + ADAM_EPS) + ADAM_WD * w_ref[...])
        nm_ref[...] = nm
        nv_ref[...] = nv

    def piece(s):
        return pl.BlockSpec((tr, C), lambda i: (jnp.clip(i - s * per, 0, per - 1), 0))

    blk = pl.BlockSpec((tr, C), lambda i: (i, 0))
    return pl.pallas_call(
        kern, name="adamw", grid=(R // tr,), in_specs=[blk] + [piece(s) for s in owner] + [blk] * 2,
        out_specs=[blk] * 4, out_shape=[jax.ShapeDtypeStruct((R, C), F32)] * 4, compiler_params=_params(1),
    )(w, *flat, m, v)


def _place():
    x, y, c = lax.axis_index("x"), lax.axis_index("y"), lax.axis_index("c")
    return x, y, c, [(1 - x, y), (x, 1 - y), (1 - x, 1 - y)]


def _comm_call(name, body, operands, out_shape, n_sems, aliases=None):
    any_spec = pl.BlockSpec(memory_space=pl.ANY)
    return pl.pallas_call(
        body, name=name, in_specs=[any_spec] * len(operands), out_specs=[any_spec] * len(out_shape),
        out_shape=out_shape, input_output_aliases=aliases or {},
        scratch_shapes=[pltpu.SemaphoreType.DMA((n,)) for n in n_sems],
    )(*operands)


def _my_block():
    return 2 * lax.axis_index("x") + lax.axis_index("y")


def _place_shard(w, dtype, first=0, count=None):
    L, R, C = w.shape
    count = L if count is None else count
    tr = _tile(R, 512)

    def kern(w_ref, o_ref):
        o_ref[...] = w_ref[...].astype(dtype)

    return pl.pallas_call(
        kern, name="place_shard", grid=(count, R // tr),
        in_specs=[pl.BlockSpec((None, tr, C), lambda l, i: (first + l, i, 0))],
        out_specs=pl.BlockSpec((None, None, tr, C), lambda l, i: (l, _my_block(), i, 0)),
        out_shape=jax.ShapeDtypeStruct((count, N_CHIPS, R, C), dtype), compiler_params=_params(2),
    )(w)


HALF_ROWS = 16


def _rows(ref, half):
    hr = ref.shape[-2] // 2
    return ref.at[(slice(None),) * (len(ref.shape) - 2) + (pl.ds(half * hr, hr),)]


def _all_gather(bufs):
    n = len(bufs)

    def body(*refs):
        outs = refs[n:2 * n]
        send_sems, recv_sems = refs[2 * n:]
        x, y, c, chips = _place()
        sibling = (x, y, 1 - c)

        def remote(k, j, chip, half, to):
            blk = 2 * chip[0] + chip[1]
            if outs[k].shape[2] % (2 * HALF_ROWS) == 0:
                region = _rows(outs[k].at[:, blk], half)
            else:
                hl = outs[k].shape[0] // 2
                region = outs[k].at[pl.ds(half * hl, hl), blk]
            return pltpu.make_async_remote_copy(
                src_ref=region, dst_ref=region, send_sem=send_sems.at[6 * k + j],
                recv_sem=recv_sems.at[6 * k + j], device_id=to, device_id_type=MESH)

        first = [remote(k, j, (x, y), c, (*chip, c)) for k in range(n) for j, chip in enumerate(chips)]
        for cp in first:
            cp.start()
        passed = []
        for k in range(n):
            for j, chip in enumerate(chips):
                remote(k, j, chip, c, (x, y, c)).wait_recv()
                passed.append(remote(k, 3 + j, chip, c, sibling))
                passed[-1].start()
        for k in range(n):
            for j, chip in enumerate(chips):
                remote(k, 3 + j, chip, 1 - c, (x, y, c)).wait_recv()
        for cp in first + passed:
            cp.wait_send()

    out_shape = [jax.ShapeDtypeStruct(s.shape, s.dtype) for s in bufs]
    return _comm_call("all_gather_weights", body, bufs, out_shape, (6 * n, 6 * n), aliases={k: k for k in range(n)})


_HBM = pl.BlockSpec(memory_space=pltpu.HBM)
_SEM = pl.BlockSpec(memory_space=pltpu.SEMAPHORE)
_ANY = pl.BlockSpec(memory_space=pl.ANY)


def _split_start(name, bufs, n_copies, make_copies, after):
    n = len(bufs)

    def body(*refs):
        send_sems, recv_sems, token = refs[n + 1], refs[n + 2], refs[2 * n + 3]
        for cp in make_copies(refs[:n], send_sems, recv_sems):
            cp.start()
        token[...] = jnp.zeros_like(token)

    res = pl.pallas_call(
        body, name=name, in_specs=[_HBM] * n + [_ANY],
        out_shape=(pltpu.SemaphoreType.DMA((n_copies,)), pltpu.SemaphoreType.DMA((n_copies,)),
                   *[pltpu.HBM(b.shape, b.dtype) for b in bufs], jax.ShapeDtypeStruct((8, LANES), F32)),
        out_specs=(_SEM, _SEM, *[_HBM] * n, pl.BlockSpec(memory_space=pltpu.VMEM)),
        input_output_aliases={i: 2 + i for i in range(n)},
        compiler_params=pltpu.CompilerParams(has_side_effects=pltpu.SideEffectType.DATAFLOW_SIDE_EFFECTING),
    )(*[pltpu.with_memory_space_constraint(b, pltpu.HBM) for b in bufs], after)
    return res[0], res[1], list(res[2:2 + n]), res[2 + n]


def _split_wait(name, send_sems, recv_sems, bufs, make_copies, after):
    n = len(bufs)

    def body(*refs):
        for cp in make_copies(refs[:n], refs[n], refs[n + 1]):
            cp.wait_send()
            cp.wait_recv()

    return list(pl.pallas_call(
        body, name=name, in_specs=[_HBM] * n + [_SEM, _SEM, _ANY],
        out_shape=tuple(pltpu.HBM(b.shape, b.dtype) for b in bufs), out_specs=tuple([_HBM] * n),
        input_output_aliases={i: i for i in range(n)},
        compiler_params=pltpu.CompilerParams(has_side_effects=pltpu.SideEffectType.DATAFLOW_SIDE_EFFECTING),
    )(*bufs, send_sems, recv_sems, after))


def _gather_copies(refs, send_sems, recv_sems):
    x, y, c, chips = _place()
    return [pltpu.make_async_remote_copy(
        src_ref=ref.at[:, 2 * x + y], dst_ref=ref.at[:, 2 * x + y], send_sem=send_sems.at[3 * k + j],
        recv_sem=recv_sems.at[3 * k + j], device_id=(*chip, c), device_id_type=MESH)
        for k, ref in enumerate(refs) for j, chip in enumerate(chips)]


def _exchange_copies(refs, send_sems, recv_sems):
    n = len(refs) // 2
    x, y, c, chips = _place()
    return [pltpu.make_async_remote_copy(
        src_ref=refs[k].at[:, 2 * chip[0] + chip[1]], dst_ref=refs[n + k].at[j], send_sem=send_sems.at[3 * k + j],
        recv_sem=recv_sems.at[3 * k + j], device_id=(*chip, c), device_id_type=MESH)
        for k in range(n) for j, chip in enumerate(chips)]


def _rs_swap_halves(grads):
    n = len(grads)

    def body(*refs):
        ins, outs = refs[:n], refs[n:2 * n]
        send_sems, recv_sems = refs[2 * n:]
        x, y, c, _ = _place()
        copies = [pltpu.make_async_remote_copy(
            src_ref=_rows(ins[k], 1 - c), dst_ref=outs[k], send_sem=send_sems.at[k],
            recv_sem=recv_sems.at[k], device_id=(x, y, 1 - c), device_id_type=MESH) for k in range(n)]
        for cp in copies:
            cp.start()
        for cp in copies:
            cp.wait()

    out_shape = [jax.ShapeDtypeStruct(g.shape[:2] + (g.shape[2] // 2, g.shape[3]), g.dtype) for g in grads]
    return _comm_call("rs_swap_halves", body, grads, out_shape, (n, n))


def _rs_exchange(parts):
    n = len(parts)

    def body(*refs):
        ins, outs = refs[:n], refs[n:2 * n]
        send_sems, recv_sems = refs[2 * n:]
        x, y, c, chips = _place()
        copies = [pltpu.make_async_remote_copy(
            src_ref=ins[k].at[:, 2 * chip[0] + chip[1]], dst_ref=outs[k].at[j], send_sem=send_sems.at[3 * k + j],
            recv_sem=recv_sems.at[3 * k + j], device_id=(*chip, c), device_id_type=MESH)
            for k in range(n) for j, chip in enumerate(chips)]
        for cp in copies:
            cp.start()
        for cp in copies:
            cp.wait()

    out_shape = [jax.ShapeDtypeStruct((3, p.shape[0]) + p.shape[2:], p.dtype) for p in parts]
    return _comm_call("rs_exchange", body, parts, out_shape, (3 * n, 3 * n))


def _rs_share(bufs):
    n = len(bufs)

    def body(*refs):
        outs = refs[n:2 * n]
        send_sems, recv_sems = refs[2 * n:]
        x, y, c, _ = _place()

        def half(k, which):
            region = _rows(outs[k], which)
            return pltpu.make_async_remote_copy(
                src_ref=region, dst_ref=region, send_sem=send_sems.at[k], recv_sem=recv_sems.at[k],
                device_id=(x, y, 1 - c), device_id_type=MESH)

        sends = [half(k, c) for k in range(n)]
        for cp in sends:
            cp.start()
        for k in range(n):
            half(k, 1 - c).wait_recv()
        for cp in sends:
            cp.wait_send()

    out_shape = [jax.ShapeDtypeStruct(h.shape, h.dtype) for h in bufs]
    return _comm_call("rs_share", body, bufs, out_shape, (n, n), aliases={k: k for k in range(n)})


def _all_reduce_small(v):
    n = v.shape[0]

    def body(v_ref, o_ref, gbuf, send_sems, recv_sems):
        x, y, c, _ = _place()
        me = 4 * x + 2 * y + c
        gbuf[me] = v_ref[...]
        copies, waits = [], []
        for r in range(1, N_DEV):
            px = 1 - x if r & 4 else x
            py = 1 - y if r & 2 else y
            pc = 1 - c if r & 1 else c
            mk = functools.partial(pltpu.make_async_remote_copy, src_ref=v_ref, send_sem=send_sems.at[r - 1],
                                   recv_sem=recv_sems.at[r - 1], device_id=(px, py, pc), device_id_type=MESH)
            copies.append(mk(dst_ref=gbuf.at[me]))
            waits.append(mk(dst_ref=gbuf.at[4 * px + 2 * py + pc]))
        for cp in copies:
            cp.start()
        for cp in waits:
            cp.wait_recv()
        for cp in copies:
            cp.wait_send()
        acc = gbuf[0]
        for d in range(1, N_DEV):
            acc = acc + gbuf[d]
        o_ref[...] = acc

    vm = pl.BlockSpec(memory_space=pltpu.VMEM)
    return pl.pallas_call(
        body, name="all_reduce_small", in_specs=[vm], out_specs=vm, out_shape=jax.ShapeDtypeStruct(v.shape, F32),
        scratch_shapes=[pltpu.VMEM((N_DEV, n, LANES), F32), pltpu.SemaphoreType.DMA((N_DEV - 1,)),
                        pltpu.SemaphoreType.DMA((N_DEV - 1,))],
        compiler_params=pltpu.CompilerParams(vmem_limit_bytes=VMEM_LIMIT),
    )(v)


def _add_half(g, h1):
    L, nb, hr, C = h1.shape
    g3, h3 = g.reshape(L * nb, 2 * hr, C), h1.reshape(L * nb, hr, C)
    tr = _tile(hr, 512)

    def kern(g_ref, h_ref, o_ref):
        o_ref[...] = (g_ref[...].astype(F32) + h_ref[...].astype(F32)).astype(BF16)

    blk = pl.BlockSpec((None, tr, C), lambda l, i: (l, i, 0))
    out = pl.pallas_call(
        kern, name="rs_add_half", grid=(L * nb, hr // tr),
        in_specs=[pl.BlockSpec((None, tr, C), lambda l, i: (l, lax.axis_index("c") * (hr // tr) + i, 0)), blk],
        out_specs=blk, out_shape=jax.ShapeDtypeStruct(h3.shape, BF16), compiler_params=_params(2),
    )(g3, h3)
    return out.reshape(h1.shape)


def _add_blocks(p, h2, half=True):
    L, nb, hr, C = p.shape
    tr = _tile(hr, 512)
    shift = lambda: lax.axis_index("c") * (hr // tr) if half else 0

    def kern(p_ref, h0_ref, h1_ref, h2_ref, o_ref):
        o_ref[...] = ((p_ref[...].astype(F32) + h0_ref[...].astype(F32)) + h1_ref[...].astype(F32)) + h2_ref[...].astype(F32)

    def other(j):
        return pl.BlockSpec((None, None, tr, C), lambda l, i: (j, l, i, 0))

    return pl.pallas_call(
        kern, name="rs_add_blocks", grid=(L, hr // tr),
        in_specs=[pl.BlockSpec((None, None, tr, C), lambda l, i: (l, _my_block(), i, 0)), other(0), other(1), other(2)],
        out_specs=pl.BlockSpec((None, tr, C), lambda l, i: (l, shift() + i, 0)),
        out_shape=jax.ShapeDtypeStruct((L, (2 if half else 1) * hr, C), F32), compiler_params=_params(2),
    )(p, h2, h2, h2)


def _d2d_swap(arrays):
    n = len(arrays)

    def body(*refs):
        ins, outs = refs[:n], refs[n:2 * n]
        send_sems, recv_sems = refs[2 * n:]
        x, y, c, _ = _place()
        copies = [pltpu.make_async_remote_copy(
            src_ref=ins[k], dst_ref=outs[k], send_sem=send_sems.at[k], recv_sem=recv_sems.at[k],
            device_id=(x, y, 1 - c), device_id_type=MESH) for k in range(n)]
        for cp in copies:
            cp.start()
        for cp in copies:
            cp.wait()

    return _comm_call("d2d_swap", body, arrays, [jax.ShapeDtypeStruct(a.shape, a.dtype) for a in arrays], (n, n))


def _reduce_scatter(grads):
    sib = _rs_swap_halves(grads)
    parts = [_add_half(g, h) for g, h in zip(grads, sib)]
    others = _rs_exchange(parts)
    return _rs_share([_add_blocks(p, o) for p, o in zip(parts, others)])


WEIGHTS = ("norm_ffn1", "w_ffn1_in", "w_ffn1_out", "norm_mix", "w_mix_in", "b_forget", "w_pool", "pool_scale",
           "conv_w", "w_mix_out", "norm_ffn2", "w_ffn2_in", "w_ffn2_out", "norm_final")
BIG = ("w_ffn1_in", "w_ffn1_out", "w_mix_in", "w_mix_out", "w_ffn2_in", "w_ffn2_out")
SMALL = ("norm_ffn1", "norm_mix", "b_forget", "w_pool", "pool_scale", "conv_w", "norm_ffn2", "norm_final")


def _prep_weights(small, gathered, conv_w, D, first):
    DA, C, H = D // 2, D // 4, D // 2 // HEAD_DIM
    L = gathered["w_mix_in"].shape[0]
    small = {k: val[first:first + L] for k, val in small.items() if k != "norm_final"}
    gathered = dict(gathered, conv_w=conv_w[first:first + L])
    w_in = jnp.concatenate([gathered["w_mix_in"][:, b] for b in range(N_CHIPS)], axis=2)
    wqkv, wrest = w_in[:, :, :3 * DA], w_in[:, :, 3 * DA + H:]
    wf = jnp.pad(w_in[:, :, 3 * DA:3 * DA + H], ((0, 0), (0, 0), (0, LANES - H)))
    ng = len(POOL_WINDOWS)
    same_group = jnp.eye(ng, dtype=bool)[None, :, None, :, None]
    wbd = jnp.where(same_group, small["w_pool"][:, :, :, None, :], 0.0).reshape(L, C, C)
    cw = jnp.concatenate([gathered["conv_w"][:, b] for b in range(N_CHIPS)], axis=2)
    return dict(
        g1=small["norm_ffn1"], gm=small["norm_mix"], g2=small["norm_ffn2"],
        w1in=gathered["w_ffn1_in"], w2in=gathered["w_ffn2_in"],
        w1out=gathered["w_ffn1_out"].reshape(L, -1, D), w2out=gathered["w_ffn2_out"].reshape(L, -1, D),
        wp=jnp.concatenate([wqkv, wrest, wf], axis=2), wmixout=gathered["w_mix_out"].reshape(L, D, D),
        bias=jnp.pad(small["b_forget"], ((0, 0), (0, LANES - H))), wbd=wbd.astype(BF16), ps=small["pool_scale"],
        cw=jnp.pad(cw, ((0, 0), (0, 8 - CONV_WIDTH), (0, 0))),
    )


def _layer_params(l, W):
    P = {k: (W[k], l) for k in ("w1in", "w2in", "w1out", "w2out", "wp", "wmixout")}
    P.update({k: W[k][l][None] for k in ("g1", "gm", "g2", "bias", "ps")})
    P.update(wbd=W["wbd"][l], cw=W["cw"][l])
    return P


def _ffn_fwd(x, g, w_in, w_out, token=None):
    h, jac, act = _ffn_up(x, g, w_in, token)
    return _ffn_out(act, w_out, x)[0], (x, h, jac, act)


def _ffn_bwd(dres, saved, g, w_in, w_out, token=None):
    x, h, jac, act = saved
    dgu, dx, dg = _ffn_bwd_main(dres, jac, x, g, w_out, w_in, token)
    dw_out = _ffn_dw_out(act, dres)[0]
    dw_in = _ffn_dw_in(h, dgu)[0]
    return dx, dg, dw_in, dw_out.reshape(N_CHIPS, -1, dw_out.shape[1])


def _mixer_fwd(x, P, B, S, tq):
    T, D = x.shape
    DA, C, H = D // 2, D // 4, D // 2 // HEAD_DIM
    hn, qkv, rest, fl = _mix_up(x, P["gm"], P["wp"], (3 * DA, 4 * C))
    qkv, rest, fl = qkv.reshape(B, S, 3 * DA), rest.reshape(B, S, 4 * C), fl.reshape(B, S, LANES)
    drow = _decay_fwd(fl, P["bias"]).reshape(B, 8, S // tq, tq)
    o, lse = _attn_fwd(qkv, drow, H, tq)
    ypc, pooled = _mix_local_fwd(rest, P["wbd"], P["ps"], P["cw"])
    x_out = _mix_out([o.reshape(T, DA), ypc.reshape(T, 2 * C)], P["wmixout"], x)
    return x_out, (x, hn, qkv, rest, fl, drow, o, lse, pooled, ypc)


def _mixer_bwd(dres, saved, P, B, S, tq):
    x, hn, qkv, rest, fl, drow, o, lse, pooled, ypc = saved
    T, D = x.shape
    DA, C, H = D // 2, D // 4, D // 2 // HEAD_DIM
    dycat = _proj("mix_out_bwd", dres, P["wmixout"], F32, NT).reshape(B, S, D)
    dw_out = _rows_dw("mix_out_dw", [o.reshape(T, DA), ypc.reshape(T, 2 * C)], dres, BF16)
    dq, dk, dv, ddrow, ddcol = _attn_bwd(qkv, drow, o, lse, dycat, H, tq)
    dfl, dbias = _decay_bwd(ddrow.reshape(B, 8, S), ddcol, fl, P["bias"], H)
    drest, dwbd, dps, dcw = _mix_local_bwd(rest, pooled, dycat, P["wbd"], P["ps"], P["cw"])
    pieces = [a.reshape(T, a.shape[-1]) for a in (dq, dk, dv, drest, dfl)]
    dwp = _pieces_dw("mix_in_dw", hn, pieces, F32, 512)
    dx, dg = _mix_in_bwd(pieces, x, P["gm"], dres, P["wp"])
    n_q, n_r = 3 * DA, 4 * C
    dw_in = jnp.concatenate([dwp[:, :n_q], dwp[:, n_q + n_r:n_q + n_r + H], dwp[:, n_q:n_q + n_r]], axis=1)
    dw_in = dw_in.reshape(D, N_CHIPS, -1).transpose(1, 0, 2).astype(BF16)
    ng = len(POOL_WINDOWS)
    same_group = jnp.eye(ng, dtype=bool)[:, None, :, None]
    dw_pool = jnp.where(same_group, dwbd.reshape(ng, C // ng, ng, C // ng), 0.0).sum(axis=2)
    small = dict(norm_mix=dg[0], b_forget=dbias[0, :H], w_pool=dw_pool, pool_scale=dps[0], conv_w=dcw[:CONV_WIDTH])
    return dx, small, dw_in, dw_out.reshape(N_CHIPS, -1, D)


def _local_step(x, target, small, conv_w, pipe):
    B, S, D = x.shape
    L = small["norm_ffn1"].shape[0]
    tq = _tile(S, 256)
    xt = x.reshape(B * S, D)
    saved, params = [], []
    for l in range(L):
        P = _layer_params(0, _prep_weights(small, pipe.weights(l, xt), conv_w, D, l))
        xt, s1 = _ffn_fwd(xt, P["g1"], P["w1in"], P["w1out"], pipe.token(l))
        xt, s2 = _mixer_fwd(xt, P, B, S, tq)
        xt, s3 = _ffn_fwd(xt, P["g2"], P["w2in"], P["w2out"])
        saved.append((s1, s2, s3))
        params.append(P)
    dres, dgf, loss = _final_loss(xt, small["norm_final"][None], target.reshape(B * S, D))
    sm = {k: [None] * L for k in SMALL if k != "norm_final"}
    token = None
    for l in reversed(range(L)):
        P, (s1, s2, s3) = params[l], saved[l]
        big = {}
        dres, dg2, big["w_ffn2_in"], big["w_ffn2_out"] = _ffn_bwd(dres, s3, P["g2"], P["w2in"], P["w2out"], token)
        dres, smix, big["w_mix_in"], big["w_mix_out"] = _mixer_bwd(dres, s2, P, B, S, tq)
        dres, dg1, big["w_ffn1_in"], big["w_ffn1_out"] = _ffn_bwd(dres, s1, P["g1"], P["w1in"], P["w1out"])
        sm["norm_ffn1"][l], sm["norm_ffn2"][l] = dg1[0], dg2[0]
        for k, val in smix.items():
            sm[k][l] = val
        token = pipe.grads(l, {k: val[None] for k, val in big.items()}, dres)
    sm = {k: jnp.stack(val) for k, val in sm.items()}
    sm["norm_final"] = dgf[0]
    return loss[0, 0], dres.reshape(B, S, D), sm


class _Pipeline:
    def __init__(self, w):
        self.w, self.n_layers = w, w[BIG[0]].shape[0]
        first = _all_gather([_place_shard(w[k], BF16, 0, 1) for k in BIG] + [_place_shard(w["conv_w"], F32)])
        self.conv_w = first[-1]
        self._ready = dict(zip(BIG, first[:-1]))
        self._gather = self._reduce = None
        self.reduced = [None] * self.n_layers
        self._start_gather(1, first[0])

    def _start_gather(self, l, after):
        if l < self.n_layers:
            placed = [_place_shard(self.w[k], BF16, l, 1) for k in BIG]
            self._gather = (l, _split_start(f"gather_start_{l}", placed, 3 * len(BIG), _gather_copies, after))

    def token(self, l):
        return self._gather[1][3] if self._gather is not None and self._gather[0] == l + 1 else None

    def weights(self, l, after):
        if l == 0:
            return self._ready
        (_, (send_sems, recv_sems, bufs, _)), self._gather = self._gather, None
        got = _split_wait(f"gather_wait_{l}", send_sems, recv_sems, bufs, _gather_copies, after)
        self._start_gather(l + 1, got[0])
        return dict(zip(BIG, got))

    def grads(self, l, big, after):
        n, grads = len(BIG), [big[k] for k in BIG]
        if self._reduce is not None:
            above, (send_sems, recv_sems, bufs, _) = self._reduce
            bufs = _split_wait(f"reduce_wait_{above}", send_sems, recv_sems, bufs, _exchange_copies, after)
            mine = [_add_blocks(p, o, half=False) for p, o in zip(bufs[:n], bufs[n:])]
            self.reduced[above] = list(zip(mine, _d2d_swap(mine)))
            self._reduce = None
        if l == 0:
            self.reduced[0] = _reduce_scatter(grads)
            return None
        lands = [lax.empty((3, g.shape[0]) + g.shape[2:], g.dtype) for g in grads]
        self._reduce = (l, _split_start(f"reduce_start_{l}", grads + lands, 3 * n, _exchange_copies, grads[0]))
        return self._reduce[1][3]


def _pack(parts, extra=()):
    flat = jnp.concatenate([p.reshape(-1) for p in parts] + [jnp.reshape(e, (1,)) for e in extra])
    n = -(-flat.shape[0] // (8 * LANES)) * 8
    return jnp.pad(flat, (0, n * LANES - flat.shape[0])).reshape(n, LANES)


def _unpack(buf, shapes):
    flat, out, at = buf.reshape(-1), [], 0
    for s in shapes:
        n = math.prod(s)
        out.append(flat[at:at + n].reshape(s))
        at += n
    return out, flat[at:]


def kernel(x, norm_ffn1, w_ffn1_in, w_ffn1_out, norm_mix, w_mix_in, b_forget, w_pool, pool_scale, conv_w, w_mix_out, norm_ffn2, w_ffn2_in, w_ffn2_out, norm_final, loss_target, m_norm_ffn1, m_w_ffn1_in, m_w_ffn1_out, m_norm_mix, m_w_mix_in, m_b_forget, m_w_pool, m_pool_scale, m_conv_w, m_w_mix_out, m_norm_ffn2, m_w_ffn2_in, m_w_ffn2_out, m_norm_final, v_norm_ffn1, v_w_ffn1_in, v_w_ffn1_out, v_norm_mix, v_w_mix_in, v_b_forget, v_w_pool, v_pool_scale, v_conv_w, v_w_mix_out, v_norm_ffn2, v_w_ffn2_in, v_w_ffn2_out, v_norm_final):
    w = dict(zip(WEIGHTS, (norm_ffn1, w_ffn1_in, w_ffn1_out, norm_mix, w_mix_in, b_forget, w_pool, pool_scale, conv_w, w_mix_out, norm_ffn2, w_ffn2_in, w_ffn2_out, norm_final)))
    m = dict(zip(WEIGHTS, (m_norm_ffn1, m_w_ffn1_in, m_w_ffn1_out, m_norm_mix, m_w_mix_in, m_b_forget, m_w_pool, m_pool_scale, m_conv_w, m_w_mix_out, m_norm_ffn2, m_w_ffn2_in, m_w_ffn2_out, m_norm_final)))
    v = dict(zip(WEIGHTS, (v_norm_ffn1, v_w_ffn1_in, v_w_ffn1_out, v_norm_mix, v_w_mix_in, v_b_forget, v_w_pool, v_pool_scale, v_conv_w, v_w_mix_out, v_norm_ffn2, v_w_ffn2_in, v_w_ffn2_out, v_norm_final)))
    block = 2 * lax.axis_index("x") + lax.axis_index("y")

    pipe = _Pipeline(w)
    small = {k: w[k] for k in SMALL}
    loss, grad_x, sm = _local_step(x, loss_target, small, pipe.conv_w, pipe)
    grads, order = {}, list(SMALL)
    total = _all_reduce_small(_pack([sm[k] for k in order], extra=(loss,)))
    parts, rest = _unpack(total, [sm[k].shape for k in order])
    grads.update(zip(order, parts))
    loss = rest[0]
    cs = conv_w.shape[2]
    grads["conv_w"] = lax.dynamic_slice_in_dim(grads["conv_w"], block * cs, cs, axis=2)

    delta, new_m, new_v = {}, {}, {}
    for i, k in enumerate(BIG):
        two_d = lambda a: a.reshape(-1, a.shape[-1])
        pieces = [tuple(map(two_d, g)) if isinstance(g, tuple) else two_d(g) for g in (layer[i] for layer in pipe.reduced)]
        res = _adamw(two_d(w[k]), pieces, two_d(m[k]), two_d(v[k]))
        grads[k], delta[k], new_m[k], new_v[k] = [r.reshape(w[k].shape) for r in res]
    packed = [_pack([t[k] for k in order]) for t in (w, grads, m, v)]
    _, d, nm, nv = _adamw(packed[0], [packed[1]], packed[2], packed[3])
    shapes = [w[k].shape for k in order]
    for res, flat in ((delta, d), (new_m, nm), (new_v, nv)):
        res.update(zip(order, _unpack(flat, shapes)[0]))
    return (loss, grad_x, *[grads[k] for k in WEIGHTS], *[delta[k] for k in WEIGHTS],
            *[new_m[k] for k in WEIGHTS], *[new_v[k] for k in WEIGHTS])
```

```python
import functools
import math

import jax
import jax.numpy as jnp
from jax import lax
from jax.experimental import pallas as pl
from jax.experimental.pallas import tpu as pltpu

F32 = jnp.float32
BF16 = jnp.bfloat16
MESH = pl.DeviceIdType.MESH

HEAD_DIM = 64
POOL_WINDOWS = (2, 4, 8, 16)
CONV_WIDTH = 3
RMS_EPS = 1e-6
ADAM_LR = 0.001
ADAM_B1 = 0.9
ADAM_B2 = 0.999
ADAM_EPS = 1e-08
ADAM_WD = 0.01
ADAM_STEP = 10

LANES = 128
VMEM_LIMIT = 56 * 1024 * 1024
N_CHIPS = 4
N_DEV = 8

NN = (((1,), (0,)), ((), ()))
NT = (((1,), (1,)), ((), ()))
TN = (((0,), (0,)), ((), ()))


def _tile(n, pref):
    for t in range(pref - pref % 16, 15, -16):
        if n % t == 0:
            return t
    return n


def _params(n_grid):
    return pltpu.CompilerParams(dimension_semantics=("arbitrary",) * n_grid, vmem_limit_bytes=VMEM_LIMIT)


def _dot(a, b, dims):
    return lax.dot_general(a, b, dims, preferred_element_type=F32)


def _mm(name, dims, operands, in_specs, out_shape, out_specs, grid, acc_shape, epilogue):
    n_in, n_out, nk = len(operands), len(out_shape), grid[-1]

    def kern(*refs):
        extras, outs = refs[2:n_in], refs[n_in:n_in + n_out]
        if nk == 1:
            epilogue(_dot(refs[0][...].astype(BF16), refs[1][...].astype(BF16), dims), extras, outs)
            return
        acc = refs[n_in + n_out]
        k = pl.program_id(len(grid) - 1)

        @pl.when(k == 0)
        def _():
            acc[...] = jnp.zeros_like(acc)

        acc[...] += _dot(refs[0][...].astype(BF16), refs[1][...].astype(BF16), dims)

        @pl.when(k == nk - 1)
        def _():
            epilogue(acc[...], extras, outs)

    return pl.pallas_call(
        kern, name=name, grid=grid, in_specs=in_specs, out_specs=out_specs, out_shape=out_shape,
        scratch_shapes=[pltpu.VMEM(acc_shape, F32)] if nk > 1 else [],
        compiler_params=_params(len(grid)),
    )(*operands)


def _store(scale=None, dtype=None):
    def ep(acc, extras, outs):
        v = acc if scale is None else acc * scale
        outs[0][...] = v.astype(outs[0].dtype)
    return ep


def _residual(scale):
    def ep(acc, extras, outs):
        outs[0][...] = extras[0][...] + scale * acc
    return ep


def _rmsnorm_fwd(x, g):
    T, D = x.shape
    tr = _tile(T, 512)

    def kern(x_ref, g_ref, o_ref):
        xv = x_ref[...]
        r = lax.rsqrt(jnp.mean(xv * xv, axis=-1, keepdims=True) + RMS_EPS)
        o_ref[...] = (xv * r * g_ref[...]).astype(BF16)

    return pl.pallas_call(
        kern, name="rmsnorm_fwd", grid=(T // tr,),
        in_specs=[pl.BlockSpec((tr, D), lambda i: (i, 0)), pl.BlockSpec((1, D), lambda i: (0, 0))],
        out_specs=pl.BlockSpec((tr, D), lambda i: (i, 0)),
        out_shape=jax.ShapeDtypeStruct((T, D), BF16), compiler_params=_params(1),
    )(x, g)


def _rmsnorm_bwd(x, g, dh, dres):
    T, D = x.shape
    tr = _tile(T, 256)

    def kern(x_ref, g_ref, dh_ref, dres_ref, dx_ref, dg_ref):
        xv, dhv = x_ref[...], dh_ref[...]
        r = lax.rsqrt(jnp.mean(xv * xv, axis=-1, keepdims=True) + RMS_EPS)
        y = xv * r
        dy = dhv * g_ref[...]
        dx_ref[...] = dres_ref[...] + r * (dy - y * jnp.mean(dy * y, axis=-1, keepdims=True))
        part = jnp.sum(dhv * y, axis=0, keepdims=True)

        @pl.when(pl.program_id(0) == 0)
        def _():
            dg_ref[...] = part

        @pl.when(pl.program_id(0) > 0)
        def _():
            dg_ref[...] += part

    row = pl.BlockSpec((tr, D), lambda i: (i, 0))
    vec = pl.BlockSpec((1, D), lambda i: (0, 0))
    return pl.pallas_call(
        kern, name="rmsnorm_bwd", grid=(T // tr,), in_specs=[row, vec, row, row], out_specs=[row, vec],
        out_shape=[jax.ShapeDtypeStruct((T, D), F32), jax.ShapeDtypeStruct((1, D), F32)],
        compiler_params=_params(1),
    )(x, g, dh, dres)


def _final_loss(x, g, target):
    T, D = x.shape
    tr = _tile(T, 256)

    def kern(x_ref, g_ref, t_ref, dx_ref, dg_ref, loss_ref):
        xv = x_ref[...]
        r = lax.rsqrt(jnp.mean(xv * xv, axis=-1, keepdims=True) + RMS_EPS)
        y = xv * r
        err = y * g_ref[...] - t_ref[...]
        lpart = 0.5 * jnp.sum(jnp.mean(err * err, axis=-1, keepdims=True), axis=0, keepdims=True)
        dh = err * (1.0 / D)
        dy = dh * g_ref[...]
        dx_ref[...] = r * (dy - y * jnp.mean(dy * y, axis=-1, keepdims=True))
        part = jnp.sum(dh * y, axis=0, keepdims=True)
        lrow = jnp.broadcast_to(lpart, (1, LANES))

        @pl.when(pl.program_id(0) == 0)
        def _():
            dg_ref[...] = part
            loss_ref[...] = lrow

        @pl.when(pl.program_id(0) > 0)
        def _():
            dg_ref[...] += part
            loss_ref[...] += lrow

    row = pl.BlockSpec((tr, D), lambda i: (i, 0))
    vec = pl.BlockSpec((1, D), lambda i: (0, 0))
    return pl.pallas_call(
        kern, name="final_loss", grid=(T // tr,), in_specs=[row, vec, row],
        out_specs=[row, vec, pl.BlockSpec((1, LANES), lambda i: (0, 0))],
        out_shape=[jax.ShapeDtypeStruct((T, D), F32), jax.ShapeDtypeStruct((1, D), F32),
                   jax.ShapeDtypeStruct((1, LANES), F32)],
        compiler_params=_params(1),
    )(x, g, target)


def _ffn_in(h, w4):
    T, D = h.shape
    w4, l = w4
    Fh = w4.shape[3]
    F = 2 * Fh
    tm = _tile(T, 512)

    def kern(h_ref, wg_ref, wu_ref, jac_ref, act_ref):
        hv = h_ref[...]
        gate = _dot(hv, wg_ref[...], NN)
        up = _dot(hv, wu_ref[...], NN)
        sg = jax.nn.sigmoid(gate)
        silu = gate * sg
        jac_ref[0] = (up * (sg + silu * (1.0 - sg))).astype(BF16)
        jac_ref[1] = silu.astype(BF16)
        act_ref[...] = (silu * up).astype(BF16)

    return pl.pallas_call(
        kern, name="ffn_in", grid=(2, T // tm),
        in_specs=[pl.BlockSpec((tm, D), lambda j, i: (i, 0)),
                  pl.BlockSpec((None, None, D, Fh), lambda j, i: (l, j, 0, 0)),
                  pl.BlockSpec((None, None, D, Fh), lambda j, i: (l, 2 + j, 0, 0))],
        out_specs=[pl.BlockSpec((2, tm, Fh), lambda j, i: (0, i, j)),
                   pl.BlockSpec((tm, Fh), lambda j, i: (i, j))],
        out_shape=[jax.ShapeDtypeStruct((2, T, F), BF16), jax.ShapeDtypeStruct((T, F), BF16)],
        compiler_params=_params(2),
    )(h, w4, w4)


def _resident(shape, index_map):
    return pl.BlockSpec(shape, index_map, pipeline_mode=pl.Buffered(1))


def _token_operand(token):
    return ([], []) if token is None else ([token], [pl.BlockSpec(token.shape, lambda i: (0, 0))])


def _ffn_up(x, g, w4, token=None):
    T, D = x.shape
    w4, l = w4
    Fh = w4.shape[3]
    F = 2 * Fh
    tm = _tile(T, 512)
    tok_ops, tok_specs = _token_operand(token)

    def kern(x_ref, g_ref, w_ref, *rest):
        h_ref, jac_ref, act_ref = rest[len(tok_ops):]
        xv = x_ref[...]
        r = lax.rsqrt(jnp.mean(xv * xv, axis=-1, keepdims=True) + RMS_EPS)
        hv = (xv * r * g_ref[...]).astype(BF16)
        h_ref[...] = hv
        for j in range(2):
            cols = slice(j * Fh, (j + 1) * Fh)
            gate = _dot(hv, w_ref[j], NN)
            up = _dot(hv, w_ref[2 + j], NN)
            sg = jax.nn.sigmoid(gate)
            silu = gate * sg
            jac_ref[0, :, cols] = (up * (sg + silu * (1.0 - sg))).astype(BF16)
            jac_ref[1, :, cols] = silu.astype(BF16)
            act_ref[:, cols] = (silu * up).astype(BF16)

    return pl.pallas_call(
        kern, name="ffn_up", grid=(T // tm,),
        in_specs=[pl.BlockSpec((tm, D), lambda i: (i, 0)), pl.BlockSpec((1, D), lambda i: (0, 0)),
                  _resident((None, 4, D, Fh), lambda i: (l, 0, 0, 0))] + tok_specs,
        out_specs=[pl.BlockSpec((tm, D), lambda i: (i, 0)), pl.BlockSpec((2, tm, F), lambda i: (0, i, 0)),
                   pl.BlockSpec((tm, F), lambda i: (i, 0))],
        out_shape=[jax.ShapeDtypeStruct((T, D), BF16), jax.ShapeDtypeStruct((2, T, F), BF16),
                   jax.ShapeDtypeStruct((T, F), BF16)],
        compiler_params=_params(1),
    )(x, g, w4, *tok_ops)


def _ffn_bwd_main(dres, jac, x, g, w_out, w4, token=None):
    T, D = dres.shape
    w_out, l = w_out
    w4, _ = w4
    F = w_out.shape[1]
    Fh = F // 2
    tm = _tile(T, 256)
    tok_ops, tok_specs = _token_operand(token)

    def kern(d_ref, jac_ref, x_ref, g_ref, wo_ref, wi_ref, *rest):
        dgu_ref, dx_ref, dg_ref = rest[len(tok_ops):]
        dv = d_ref[...]
        d16 = dv.astype(BF16)
        dh = jnp.zeros((tm, D), F32)
        for j in range(2):
            cols = slice(j * Fh, (j + 1) * Fh)
            dact = 0.5 * _dot(d16, wo_ref[cols, :], NT)
            dgate = (dact * jac_ref[0, :, cols].astype(F32)).astype(BF16)
            dup = (dact * jac_ref[1, :, cols].astype(F32)).astype(BF16)
            dgu_ref[0, :, cols] = dgate
            dgu_ref[1, :, cols] = dup
            dh = dh + _dot(dgate, wi_ref[j], NT) + _dot(dup, wi_ref[2 + j], NT)
        xv = x_ref[...]
        r = lax.rsqrt(jnp.mean(xv * xv, axis=-1, keepdims=True) + RMS_EPS)
        y = xv * r
        dy = dh * g_ref[...]
        dx_ref[...] = dv + r * (dy - y * jnp.mean(dy * y, axis=-1, keepdims=True))
        part = jnp.sum(dh * y, axis=0, keepdims=True)

        @pl.when(pl.program_id(0) == 0)
        def _():
            dg_ref[...] = part

        @pl.when(pl.program_id(0) > 0)
        def _():
            dg_ref[...] += part

    row = pl.BlockSpec((tm, D), lambda i: (i, 0))
    vec = pl.BlockSpec((1, D), lambda i: (0, 0))
    wide = pl.BlockSpec((2, tm, F), lambda i: (0, i, 0))
    return pl.pallas_call(
        kern, name="ffn_bwd_main", grid=(T // tm,),
        in_specs=[row, wide, row, vec, _resident((None, F, D), lambda i: (l, 0, 0)),
                  _resident((None, 4, D, Fh), lambda i: (l, 0, 0, 0))] + tok_specs,
        out_specs=[wide, row, vec],
        out_shape=[jax.ShapeDtypeStruct((2, T, F), BF16), jax.ShapeDtypeStruct((T, D), F32),
                   jax.ShapeDtypeStruct((1, D), F32)],
        compiler_params=_params(1),
    )(dres, jac, x, g, w_out, w4, *tok_ops)


def _ffn_out(act, w_out, x):
    T, F = act.shape
    w_out, l = w_out
    D = w_out.shape[2]
    tm = _tile(T, 512)
    return _mm("ffn_out", NN, [act, w_out, x],
               [pl.BlockSpec((tm, F), lambda i, k: (i, 0)), pl.BlockSpec((None, F, D), lambda i, k: (l, 0, 0)),
                pl.BlockSpec((tm, D), lambda i, k: (i, 0))],
               [jax.ShapeDtypeStruct((T, D), F32)], [pl.BlockSpec((tm, D), lambda i, k: (i, 0))],
               (T // tm, 1), None, _residual(0.5))


def _ffn_bwd_act(dres, w_out, jac):
    T, D = dres.shape
    w_out, l = w_out
    F = w_out.shape[1]
    Fh = F // 2
    tm = _tile(T, 512)

    def kern(d_ref, w_ref, jac_ref, o_ref):
        dact = 0.5 * _dot(d_ref[...].astype(BF16), w_ref[...], NT)
        o_ref[0] = (dact * jac_ref[0].astype(F32)).astype(BF16)
        o_ref[1] = (dact * jac_ref[1].astype(F32)).astype(BF16)

    return pl.pallas_call(
        kern, name="ffn_bwd_act", grid=(2, T // tm),
        in_specs=[pl.BlockSpec((tm, D), lambda j, i: (i, 0)), pl.BlockSpec((None, Fh, D), lambda j, i: (l, j, 0)),
                  pl.BlockSpec((2, tm, Fh), lambda j, i: (0, i, j))],
        out_specs=pl.BlockSpec((2, tm, Fh), lambda j, i: (0, i, j)),
        out_shape=jax.ShapeDtypeStruct((2, T, F), BF16), compiler_params=_params(2),
    )(dres, w_out, jac)


def _ffn_dw_out(act, dres):
    T, F = act.shape
    D = dres.shape[1]
    tm, tk = F // 2, _tile(T, 1024)
    return _mm("ffn_dw_out", TN, [act, dres],
               [pl.BlockSpec((tk, tm), lambda i, k: (k, i)), pl.BlockSpec((tk, D), lambda i, k: (k, 0))],
               [jax.ShapeDtypeStruct((F, D), BF16)], [pl.BlockSpec((tm, D), lambda i, k: (i, 0))],
               (2, T // tk), (tm, D), _store(0.5))


def _ffn_dw_in(h, dgu):
    T, D = h.shape
    Fh = dgu.shape[2] // 2
    tk = _tile(T, 1024)
    return _mm("ffn_dw_in", TN, [h, dgu],
               [pl.BlockSpec((tk, D), lambda j, k: (k, 0)),
                pl.BlockSpec((None, tk, Fh), lambda j, k: (j // 2, k, j % 2))],
               [jax.ShapeDtypeStruct((4, D, Fh), BF16)], [pl.BlockSpec((None, D, Fh), lambda j, k: (j, 0, 0))],
               (4, T // tk), (D, Fh), _store())


def _ffn_dh(dgu, w4):
    T = dgu.shape[1]
    w4, l = w4
    D, Fh = w4.shape[2], w4.shape[3]
    tm = _tile(T, 1024)
    return _mm("ffn_dh", NT, [dgu, w4],
               [pl.BlockSpec((None, tm, Fh), lambda i, k: (k // 2, i, k % 2)),
                pl.BlockSpec((None, None, D, Fh), lambda i, k: (l, k, 0, 0))],
               [jax.ShapeDtypeStruct((T, D), F32)], [pl.BlockSpec((tm, D), lambda i, k: (i, 0))],
               (T // tm, 4), (tm, D), _store())


def _proj(name, a, w, out_dtype, dims=NN, extra=None, scale=None):
    T, K = a.shape
    w, l = w
    N = w.shape[2] if dims == NN else w.shape[1]
    tm = _tile(T, 512)
    ops = [a, w] + ([extra] if extra is not None else [])
    specs = [pl.BlockSpec((tm, K), lambda i, k: (i, 0)), pl.BlockSpec((None,) + w.shape[1:], lambda i, k: (l, 0, 0))]
    if extra is not None:
        specs.append(pl.BlockSpec((tm, N), lambda i, k: (i, 0)))
    ep = _residual(1.0) if extra is not None else _store(scale)
    return _mm(name, dims, ops, specs, [jax.ShapeDtypeStruct((T, N), out_dtype)],
               [pl.BlockSpec((tm, N), lambda i, k: (i, 0))], (T // tm, 1), None, ep)[0]


def _mix_up(x, g, wp, widths):
    T, D = x.shape
    wp, l = wp
    n_qkv, n_rest = widths
    NP = wp.shape[2]
    tm = _tile(T, 512)

    def kern(x_ref, g_ref, w_ref, h_ref, qkv_ref, rest_ref, fl_ref):
        xv = x_ref[...]
        r = lax.rsqrt(jnp.mean(xv * xv, axis=-1, keepdims=True) + RMS_EPS)
        hv = (xv * r * g_ref[...]).astype(BF16)
        h_ref[...] = hv
        qkv_ref[...] = _dot(hv, w_ref[:, 0:n_qkv], NN).astype(BF16)
        rest_ref[...] = _dot(hv, w_ref[:, n_qkv:n_qkv + n_rest], NN)
        fl_ref[...] = _dot(hv, w_ref[:, n_qkv + n_rest:NP], NN)

    row = lambda n: pl.BlockSpec((tm, n), lambda i: (i, 0))
    return pl.pallas_call(
        kern, name="mix_up", grid=(T // tm,),
        in_specs=[row(D), pl.BlockSpec((1, D), lambda i: (0, 0)), _resident((None, D, NP), lambda i: (l, 0, 0))],
        out_specs=[row(D), row(n_qkv), row(n_rest), row(LANES)],
        out_shape=[jax.ShapeDtypeStruct((T, D), BF16), jax.ShapeDtypeStruct((T, n_qkv), BF16),
                   jax.ShapeDtypeStruct((T, n_rest), F32), jax.ShapeDtypeStruct((T, LANES), F32)],
        compiler_params=_params(1),
    )(x, g, wp)


def _column_starts(pieces):
    starts, at = [], 0
    for p in pieces:
        starts.append(at)
        at += p.shape[1]
    return starts


def _mix_in_bwd(pieces, x, g, dres, wp):
    T, D = x.shape
    wp, l = wp
    NP = wp.shape[2]
    tm = _tile(T, 512)
    n, starts = len(pieces), _column_starts(pieces)

    def kern(*refs):
        x_ref, g_ref, d_ref, w_ref, dx_ref, dg_ref = refs[n:]
        dh = jnp.zeros((tm, D), F32)
        for p_ref, at in zip(refs[:n], starts):
            dh = dh + _dot(p_ref[...].astype(BF16), w_ref[:, at:at + p_ref.shape[1]], NT)
        xv = x_ref[...]
        r = lax.rsqrt(jnp.mean(xv * xv, axis=-1, keepdims=True) + RMS_EPS)
        y = xv * r
        dy = dh * g_ref[...]
        dx_ref[...] = d_ref[...] + r * (dy - y * jnp.mean(dy * y, axis=-1, keepdims=True))
        part = jnp.sum(dh * y, axis=0, keepdims=True)

        @pl.when(pl.program_id(0) == 0)
        def _():
            dg_ref[...] = part

        @pl.when(pl.program_id(0) > 0)
        def _():
            dg_ref[...] += part

    row = lambda n: pl.BlockSpec((tm, n), lambda i: (i, 0))
    vec = pl.BlockSpec((1, D), lambda i: (0, 0))
    return pl.pallas_call(
        kern, name="mix_in_bwd", grid=(T // tm,),
        in_specs=[row(p.shape[1]) for p in pieces] + [row(D), vec, row(D), _resident((None, D, NP), lambda i: (l, 0, 0))],
        out_specs=[row(D), vec],
        out_shape=[jax.ShapeDtypeStruct((T, D), F32), jax.ShapeDtypeStruct((1, D), F32)],
        compiler_params=_params(1),
    )(*pieces, x, g, dres, wp)


def _pieces_dw(name, a, pieces, out_dtype, tk_pref):
    T, M = a.shape
    n, starts = len(pieces), _column_starts(pieces)
    N = starts[-1] + pieces[-1].shape[1]
    tk = _tile(T, tk_pref)
    nk = T // tk

    def kern(a_ref, *refs):
        o_ref, acc = refs[n], refs[n + 1]
        k = pl.program_id(0)

        @pl.when(k == 0)
        def _():
            acc[...] = jnp.zeros_like(acc)

        av = a_ref[...].astype(BF16)
        for p_ref, at in zip(refs[:n], starts):
            acc[:, at:at + p_ref.shape[1]] += _dot(av, p_ref[...].astype(BF16), TN)

        @pl.when(k == nk - 1)
        def _():
            o_ref[...] = acc[...].astype(out_dtype)

    return pl.pallas_call(
        kern, name=name, grid=(nk,),
        in_specs=[pl.BlockSpec((tk, M), lambda k: (k, 0))] + [pl.BlockSpec((tk, p.shape[1]), lambda k: (k, 0)) for p in pieces],
        out_specs=pl.BlockSpec((M, N), lambda k: (0, 0)), out_shape=jax.ShapeDtypeStruct((M, N), out_dtype),
        scratch_shapes=[pltpu.VMEM((M, N), F32)], compiler_params=_params(1),
    )(a, *pieces)


def _rows_dw(name, pieces, d, out_dtype):
    T, N = d.shape
    n, starts = len(pieces), _column_starts(pieces)
    M = starts[-1] + pieces[-1].shape[1]
    tk = _tile(T, 1024)
    nk = T // tk

    def kern(*refs):
        d_ref, o_ref, acc = refs[n], refs[n + 1], refs[n + 2]
        k = pl.program_id(0)

        @pl.when(k == 0)
        def _():
            acc[...] = jnp.zeros_like(acc)

        dv = d_ref[...].astype(BF16)
        for p_ref, at in zip(refs[:n], starts):
            acc[at:at + p_ref.shape[1], :] += _dot(p_ref[...], dv, TN)

        @pl.when(k == nk - 1)
        def _():
            o_ref[...] = acc[...].astype(out_dtype)

    return pl.pallas_call(
        kern, name=name, grid=(nk,),
        in_specs=[pl.BlockSpec((tk, p.shape[1]), lambda k: (k, 0)) for p in pieces] + [pl.BlockSpec((tk, N), lambda k: (k, 0))],
        out_specs=pl.BlockSpec((M, N), lambda k: (0, 0)), out_shape=jax.ShapeDtypeStruct((M, N), out_dtype),
        scratch_shapes=[pltpu.VMEM((M, N), F32)], compiler_params=_params(1),
    )(*pieces, d)


def _mix_out(pieces, w, x):
    T, D = x.shape
    w, l = w
    n, starts = len(pieces), _column_starts(pieces)
    tm = _tile(T, 512)

    def kern(*refs):
        w_ref, x_ref, o_ref = refs[n:]
        acc = x_ref[...]
        for p_ref, at in zip(refs[:n], starts):
            acc = acc + _dot(p_ref[...], w_ref[at:at + p_ref.shape[1], :], NN)
        o_ref[...] = acc

    row = lambda m: pl.BlockSpec((tm, m), lambda i: (i, 0))
    return pl.pallas_call(
        kern, name="mix_out", grid=(T // tm,),
        in_specs=[row(p.shape[1]) for p in pieces] + [_resident((None,) + w.shape[1:], lambda i: (l, 0, 0)), row(D)],
        out_specs=row(D), out_shape=jax.ShapeDtypeStruct((T, D), F32), compiler_params=_params(1),
    )(*pieces, w, x)


def _dw(name, a, d, out_dtype):
    T, M = a.shape
    N = d.shape[1]
    tk = _tile(T, 1024 if M * N <= 1024 * 1408 else 512)
    return _mm(name, TN, [a, d],
               [pl.BlockSpec((tk, M), lambda i, k: (k, 0)), pl.BlockSpec((tk, N), lambda i, k: (k, 0))],
               [jax.ShapeDtypeStruct((M, N), out_dtype)], [pl.BlockSpec((M, N), lambda i, k: (0, 0))],
               (1, T // tk), (M, N), _store())[0]


def _log_sigmoid(z):
    return jnp.minimum(z, 0.0) - jnp.log(1.0 + jnp.exp(-jnp.abs(z)))


def _decay_fwd(fl, bias):
    B, S, _ = fl.shape

    def kern(fl_ref, b_ref, o_ref):
        d = _log_sigmoid(fl_ref[...] + b_ref[...])
        row = lax.broadcasted_iota(jnp.int32, (S, LANES), 0)
        sh = 1
        while sh < S:
            d = d + jnp.where(row >= sh, pltpu.roll(d, sh, 0), 0.0)
            sh *= 2
        o_ref[...] = d.T[0:8, :]

    return pl.pallas_call(
        kern, name="decay_fwd", grid=(B,),
        in_specs=[pl.BlockSpec((None, S, LANES), lambda b: (b, 0, 0)), pl.BlockSpec((1, LANES), lambda b: (0, 0))],
        out_specs=pl.BlockSpec((None, 8, S), lambda b: (b, 0, 0)),
        out_shape=jax.ShapeDtypeStruct((B, 8, S), F32), compiler_params=_params(1),
    )(fl, bias)


def _decay_bwd(ddrow, ddcol, fl, bias, n_heads):
    B, S, _ = fl.shape

    def kern(dd_ref, ddc_ref, fl_ref, b_ref, dfl_ref, db_ref):
        dd = jnp.concatenate([dd_ref[...], jnp.zeros((LANES - 8, S), F32)], axis=0).T + ddc_ref[...]
        row = lax.broadcasted_iota(jnp.int32, (S, LANES), 0)
        lane = lax.broadcasted_iota(jnp.int32, (S, LANES), 1)
        sh = 1
        while sh < S:
            dd = dd + jnp.where(row < S - sh, pltpu.roll(dd, S - sh, 0), 0.0)
            sh *= 2
        z = fl_ref[...] + b_ref[...]
        dfl = jnp.where(lane < n_heads, dd / (1.0 + jnp.exp(z)), 0.0)
        dfl_ref[...] = dfl
        part = jnp.sum(dfl, axis=0, keepdims=True)

        @pl.when(pl.program_id(0) == 0)
        def _():
            db_ref[...] = part

        @pl.when(pl.program_id(0) > 0)
        def _():
            db_ref[...] += part

    return pl.pallas_call(
        kern, name="decay_bwd", grid=(B,),
        in_specs=[pl.BlockSpec((None, 8, S), lambda b: (b, 0, 0)), pl.BlockSpec((None, S, LANES), lambda b: (b, 0, 0)),
                  pl.BlockSpec((None, S, LANES), lambda b: (b, 0, 0)), pl.BlockSpec((1, LANES), lambda b: (0, 0))],
        out_specs=[pl.BlockSpec((None, S, LANES), lambda b: (b, 0, 0)), pl.BlockSpec((1, LANES), lambda b: (0, 0))],
        out_shape=[jax.ShapeDtypeStruct((B, S, LANES), F32), jax.ShapeDtypeStruct((1, LANES), F32)],
        compiler_params=_params(1),
    )(ddrow, ddcol, fl, bias)


def _attn_fwd(qkv, drow, n_heads, tq):
    B, S, _ = qkv.shape
    DA = n_heads * HEAD_DIM
    scale = HEAD_DIM ** -0.5

    n_pairs = n_heads // 2

    def kern(q_ref, k_ref, v_ref, dr_ref, o_ref, lse_ref):
        i = pl.program_id(1)
        lane = lax.broadcasted_iota(jnp.int32, (tq, LANES), 1)
        low = lane < HEAD_DIM
        causal = lax.broadcasted_iota(jnp.int32, (tq, tq), 1) <= lax.broadcasted_iota(jnp.int32, (tq, tq), 0)
        qms = []
        for p in range(n_pairs):
            q2 = q_ref[:, LANES * p:LANES * (p + 1)] * scale
            qms += [jnp.where(low, q2, jnp.zeros_like(q2)), jnp.where(low, jnp.zeros_like(q2), q2)]

        def step(j, carry, masked):
            ms, ls, accs = carry
            ks = pl.multiple_of(j * tq, tq)
            new_m, new_l, new_acc = [], [], []
            for p in range(n_pairs):
                cols = slice(LANES * p, LANES * (p + 1))
                k2, v2 = k_ref[pl.ds(ks, tq), cols], v_ref[pl.ds(ks, tq), cols]
                alphas, pvs = [], []
                for h in (2 * p, 2 * p + 1):
                    s = _dot(qms[h], k2, NT) - dr_ref[h, pl.ds(j, 1), :]
                    if masked:
                        s = jnp.where(causal, s, -jnp.inf)
                    m_new = jnp.maximum(ms[h], jnp.max(s, axis=1, keepdims=True))
                    alpha = jnp.exp(ms[h] - m_new)
                    pm = jnp.exp(s - m_new)
                    new_m.append(m_new)
                    new_l.append(alpha * ls[h] + jnp.sum(pm, axis=1, keepdims=True))
                    alphas.append(alpha)
                    pvs.append(_dot(pm.astype(BF16), v2, NN))
                new_acc.append(jnp.where(low, alphas[0], alphas[1]) * accs[p] + jnp.where(low, pvs[0], pvs[1]))
            return tuple(new_m), tuple(new_l), tuple(new_acc)

        init = (tuple(jnp.full((tq, 1), -jnp.inf, F32) for _ in range(n_heads)),
                tuple(jnp.zeros((tq, 1), F32) for _ in range(n_heads)),
                tuple(jnp.zeros((tq, LANES), F32) for _ in range(n_pairs)))
        ms, ls, accs = step(i, lax.fori_loop(0, i, functools.partial(step, masked=False), init), True)
        lse_mat = jnp.zeros((tq, LANES), F32)
        for p in range(n_pairs):
            l0, l1 = ls[2 * p], ls[2 * p + 1]
            o_ref[:, LANES * p:LANES * (p + 1)] = (accs[p] / jnp.where(low, l0, l1)).astype(BF16)
            lse_mat = jnp.where(lane == 2 * p, ms[2 * p] + jnp.log(l0), lse_mat)
            lse_mat = jnp.where(lane == 2 * p + 1, ms[2 * p + 1] + jnp.log(l1), lse_mat)
        lse_ref[...] = lse_mat

    nq = S // tq
    return pl.pallas_call(
        kern, name="attn_fwd", grid=(B, nq),
        in_specs=[pl.BlockSpec((None, tq, DA), lambda b, i: (b, i, 0)),
                  pl.BlockSpec((None, S, DA), lambda b, i: (b, 0, 1)),
                  pl.BlockSpec((None, S, DA), lambda b, i: (b, 0, 2)),
                  pl.BlockSpec((None, 8, nq, tq), lambda b, i: (b, 0, 0, 0))],
        out_specs=[pl.BlockSpec((None, tq, DA), lambda b, i: (b, i, 0)),
                   pl.BlockSpec((None, tq, LANES), lambda b, i: (b, i, 0))],
        out_shape=[jax.ShapeDtypeStruct((B, S, DA), BF16), jax.ShapeDtypeStruct((B, S, LANES), F32)],
        compiler_params=_params(2),
    )(qkv, qkv, qkv, drow)


def _attn_bwd(qkv, drow, o, lse, dycat, n_heads, tq):
    B, S, _ = qkv.shape
    DA = n_heads * HEAD_DIM
    scale = HEAD_DIM ** -0.5
    nq = S // tq

    n_pairs = n_heads // 2

    def kern(q_ref, k_ref, v_ref, dr_ref, o_ref, lse_ref, do_ref, dq_ref, dk_ref, dv_ref, ddr_ref, ddc_ref,
             dk_acc, dv_acc, qm_s, dom_s, delta_s, rs_s, dq_s):
        i = pl.program_id(1)

        @pl.when(i == 0)
        def _():
            dk_acc[...] = jnp.zeros_like(dk_acc)
            dv_acc[...] = jnp.zeros_like(dv_acc)
            ddr_ref[...] = jnp.zeros_like(ddr_ref)

        lane = lax.broadcasted_iota(jnp.int32, (tq, LANES), 1)
        low = lane < HEAD_DIM
        causal = lax.broadcasted_iota(jnp.int32, (tq, tq), 1) <= lax.broadcasted_iota(jnp.int32, (tq, tq), 0)
        for p in range(n_pairs):
            cols = slice(LANES * p, LANES * (p + 1))
            q2 = q_ref[:, cols] * scale
            do_f = do_ref[:, cols]
            do2 = do_f.astype(BF16)
            prod = do_f * o_ref[:, cols].astype(F32)
            qm_s[2 * p] = jnp.where(low, q2, jnp.zeros_like(q2))
            qm_s[2 * p + 1] = jnp.where(low, jnp.zeros_like(q2), q2)
            dom_s[2 * p] = jnp.where(low, do2, jnp.zeros_like(do2))
            dom_s[2 * p + 1] = jnp.where(low, jnp.zeros_like(do2), do2)
            delta_s[2 * p] = jnp.sum(jnp.where(low, prod, 0.0), axis=1, keepdims=True)
            delta_s[2 * p + 1] = jnp.sum(jnp.where(low, 0.0, prod), axis=1, keepdims=True)
            dq_s[p] = jnp.zeros((tq, LANES), F32)
        rs_s[...] = jnp.zeros(rs_s.shape, F32)

        def step(j, masked):
            ks = pl.multiple_of(j * tq, tq)
            for p in range(n_pairs):
                cols = slice(LANES * p, LANES * (p + 1))
                k2, v2 = k_ref[pl.ds(ks, tq), cols], v_ref[pl.ds(ks, tq), cols]
                dvs, dks, dqs = [], [], []
                for h in (2 * p, 2 * p + 1):
                    qm, dom = qm_s[h], dom_s[h]
                    s = _dot(qm, k2, NT) - dr_ref[h, pl.ds(j, 1), :]
                    if masked:
                        s = jnp.where(causal, s, -jnp.inf)
                    pm = jnp.exp(s - lse_ref[:, h:h + 1])
                    ds = pm * (_dot(dom, v2, NT) - delta_s[h])
                    ddr_ref[h, pl.ds(j, 1), :] -= jnp.sum(ds, axis=0, keepdims=True)
                    rs_s[h] += jnp.sum(ds, axis=1, keepdims=True)
                    dsb = ds.astype(BF16)
                    dvs.append(_dot(pm.astype(BF16), dom, TN))
                    dks.append(_dot(dsb, qm, TN))
                    dqs.append(_dot(dsb, k2, NN))
                dv_acc[pl.ds(ks, tq), cols] += dvs[0] + dvs[1]
                dk_acc[pl.ds(ks, tq), cols] += dks[0] + dks[1]
                dq_s[p] += jnp.where(low, dqs[0], dqs[1])

        def body(j, carry):
            step(j, False)
            return carry

        lax.fori_loop(0, i, body, 0)
        step(i, True)
        ddc = jnp.zeros((tq, LANES), F32)
        for p in range(n_pairs):
            dq_ref[:, LANES * p:LANES * (p + 1)] = (dq_s[p] * scale).astype(BF16)
            ddc = jnp.where(lane == 2 * p, rs_s[2 * p], ddc)
            ddc = jnp.where(lane == 2 * p + 1, rs_s[2 * p + 1], ddc)
        ddc_ref[...] = ddc

        @pl.when(i == nq - 1)
        def _():
            dk_ref[...] = dk_acc[...].astype(BF16)
            dv_ref[...] = dv_acc[...].astype(BF16)

    tile = pl.BlockSpec((None, tq, DA), lambda b, i: (b, i, 0))
    seq = pl.BlockSpec((None, S, DA), lambda b, i: (b, 0, 0))
    dec = pl.BlockSpec((None, 8, nq, tq), lambda b, i: (b, 0, 0, 0))
    return pl.pallas_call(
        kern, name="attn_bwd", grid=(B, nq),
        in_specs=[tile, pl.BlockSpec((None, S, DA), lambda b, i: (b, 0, 1)),
                  pl.BlockSpec((None, S, DA), lambda b, i: (b, 0, 2)), dec, tile,
                  pl.BlockSpec((None, tq, LANES), lambda b, i: (b, i, 0)), tile],
        out_specs=[tile, seq, seq, dec, pl.BlockSpec((None, tq, LANES), lambda b, i: (b, i, 0))],
        out_shape=[jax.ShapeDtypeStruct((B, S, DA), BF16)] * 3 + [jax.ShapeDtypeStruct((B, 8, nq, tq), F32),
                                                                  jax.ShapeDtypeStruct((B, S, LANES), F32)],
        scratch_shapes=[pltpu.VMEM((S, DA), F32), pltpu.VMEM((S, DA), F32),
                        pltpu.VMEM((n_heads, tq, LANES), BF16), pltpu.VMEM((n_heads, tq, LANES), BF16),
                        pltpu.VMEM((n_heads, tq, 1), F32), pltpu.VMEM((n_heads, tq, 1), F32),
                        pltpu.VMEM((n_pairs, tq, LANES), F32)],
        compiler_params=_params(2),
    )(qkv, qkv, qkv, drow, o, lse, dycat)


def _down(v, d, row):
    return jnp.where(row >= d, pltpu.roll(v, d, 0), 0.0)


def _up(v, d, row, S):
    return jnp.where(row < S - d, pltpu.roll(v, S - d, 0), 0.0)


def _window(v, shift, group):
    sums, acc, d = [], v, 1
    for _ in POOL_WINDOWS:
        acc = acc + shift(acc, d)
        sums.append(acc)
        d *= 2
    out = sums[-1]
    for gi in range(len(POOL_WINDOWS) - 2, -1, -1):
        out = jnp.where(group == gi, sums[gi], out)
    return out


def _pool_count(row, group):
    w = jnp.full(row.shape, POOL_WINDOWS[-1], jnp.int32)
    for gi in range(len(POOL_WINDOWS) - 2, -1, -1):
        w = jnp.where(group == gi, POOL_WINDOWS[gi], w)
    return jnp.minimum(row + 1, w).astype(F32)


def _mix_local_fwd(rest, wbd, ps, cw):
    B, S, C4 = rest.shape
    C = C4 // 4
    gw = C // len(POOL_WINDOWS)

    def kern(r_ref, w_ref, ps_ref, cw_ref, y_ref, pooled_ref):
        row = lax.broadcasted_iota(jnp.int32, (S, C), 0)
        group = lax.broadcasted_iota(jnp.int32, (S, C), 1) // gw
        u = r_ref[:, 0:C]
        pooled = _window(u, lambda v, d: _down(v, d, row), group) / _pool_count(row, group) - u
        pb = pooled.astype(BF16)
        pooled_ref[...] = pb
        y_ref[:, 0:C] = (_dot(pb, w_ref[...], NN) * ps_ref[...]).astype(BF16)
        uc = r_ref[:, 2 * C:3 * C] * r_ref[:, 3 * C:4 * C]
        y = cw_ref[0:1, :] * _down(uc, 2, row) + cw_ref[1:2, :] * _down(uc, 1, row) + cw_ref[2:3, :] * uc
        y_ref[:, C:2 * C] = (r_ref[:, C:2 * C] * y).astype(BF16)

    return pl.pallas_call(
        kern, name="mix_local_fwd", grid=(B,),
        in_specs=[pl.BlockSpec((None, S, C4), lambda b: (b, 0, 0)), pl.BlockSpec((C, C), lambda b: (0, 0)),
                  pl.BlockSpec((1, C), lambda b: (0, 0)), pl.BlockSpec((8, C), lambda b: (0, 0))],
        out_specs=[pl.BlockSpec((None, S, 2 * C), lambda b: (b, 0, 0)), pl.BlockSpec((None, S, C), lambda b: (b, 0, 0))],
        out_shape=[jax.ShapeDtypeStruct((B, S, 2 * C), BF16), jax.ShapeDtypeStruct((B, S, C), BF16)],
        compiler_params=_params(1),
    )(rest, wbd, ps, cw)


def _mix_local_bwd(rest, pooled, dycat, wbd, ps, cw):
    B, S, C4 = rest.shape
    C = C4 // 4
    gw = C // len(POOL_WINDOWS)

    def kern(r_ref, pooled_ref, d_ref, w_ref, ps_ref, cw_ref, dr_ref, dw_ref, dps_ref, dcw_ref):
        row = lax.broadcasted_iota(jnp.int32, (S, C), 0)
        group = lax.broadcasted_iota(jnp.int32, (S, C), 1) // gw
        dyp = d_ref[:, 0:C]
        dyc = d_ref[:, C:2 * C]
        pb = pooled_ref[...]
        dps = jnp.sum(dyp * _dot(pb, w_ref[...], NN), axis=0, keepdims=True)
        dzb = (dyp * ps_ref[...]).astype(BF16)
        dw = _dot(pb, dzb, TN)
        dpooled = _dot(dzb, w_ref[...], NT)
        g = dpooled / _pool_count(row, group)
        dr_ref[:, 0:C] = (_window(g, lambda v, d: _up(v, d, row, S), group) - dpooled).astype(BF16)
        cc, ch = r_ref[:, 2 * C:3 * C], r_ref[:, 3 * C:4 * C]
        uc = cc * ch
        u1, u2 = _down(uc, 1, row), _down(uc, 2, row)
        y = cw_ref[0:1, :] * u2 + cw_ref[1:2, :] * u1 + cw_ref[2:3, :] * uc
        dr_ref[:, C:2 * C] = (dyc * y).astype(BF16)
        dy = dyc * r_ref[:, C:2 * C]
        duc = cw_ref[0:1, :] * _up(dy, 2, row, S) + cw_ref[1:2, :] * _up(dy, 1, row, S) + cw_ref[2:3, :] * dy
        dr_ref[:, 2 * C:3 * C] = (duc * ch).astype(BF16)
        dr_ref[:, 3 * C:4 * C] = (duc * cc).astype(BF16)
        dcw = jnp.concatenate([jnp.sum(dy * u2, axis=0, keepdims=True), jnp.sum(dy * u1, axis=0, keepdims=True),
                               jnp.sum(dy * uc, axis=0, keepdims=True), jnp.zeros((5, C), F32)], axis=0)

        @pl.when(pl.program_id(0) == 0)
        def _():
            dw_ref[...] = dw
            dps_ref[...] = dps
            dcw_ref[...] = dcw

        @pl.when(pl.program_id(0) > 0)
        def _():
            dw_ref[...] += dw
            dps_ref[...] += dps
            dcw_ref[...] += dcw

    full = lambda shape: pl.BlockSpec(shape, lambda b: (0, 0))
    return pl.pallas_call(
        kern, name="mix_local_bwd", grid=(B,),
        in_specs=[pl.BlockSpec((None, S, C4), lambda b: (b, 0, 0)), pl.BlockSpec((None, S, C), lambda b: (b, 0, 0)),
                  pl.BlockSpec((None, S, 2 * C), lambda b: (b, 0, 1)), full((C, C)), full((1, C)), full((8, C))],
        out_specs=[pl.BlockSpec((None, S, C4), lambda b: (b, 0, 0)), full((C, C)), full((1, C)), full((8, C))],
        out_shape=[jax.ShapeDtypeStruct((B, S, C4), BF16), jax.ShapeDtypeStruct((C, C), F32),
                   jax.ShapeDtypeStruct((1, C), F32), jax.ShapeDtypeStruct((8, C), F32)],
        compiler_params=_params(1),
    )(rest, pooled, dycat, wbd, ps, cw)


def _adamw(w, gs, m, v):
    R, C = w.shape
    pieces = [p if isinstance(p, tuple) else (p,) for p in gs]
    owner = [s for s, p in enumerate(pieces) for _ in p]
    flat = [a for p in pieces for a in p]
    n = len(flat)
    rows = R // len(pieces)
    tr = _tile(rows, 256)
    per = rows // tr

    def kern(w_ref, *refs):
        g_refs, (m_ref, v_ref, g_out, d_ref, nm_ref, nv_ref) = refs[:n], refs[n:]
        vals, at = [], 0
        for p in pieces:
            vals.append(g_refs[at][...] if len(p) == 1 else g_refs[at][...] + g_refs[at + 1][...])
            at += len(p)
        gv = vals[0]
        for s in range(1, len(pieces)):
            gv = jnp.where(pl.program_id(0) // per == s, vals[s], gv)
        nm = ADAM_B1 * m_ref[...] + (1.0 - ADAM_B1) * gv
        nv = ADAM_B2 * v_ref[...] + (1.0 - ADAM_B2) * (gv * gv)
        m_hat = nm / (1.0 - ADAM_B1 ** ADAM_STEP)
        v_hat = nv / (1.0 - ADAM_B2 ** ADAM_STEP)
        g_out[...] = gv
        d_ref[...] = -ADAM_LR * (m_hat / (jnp.sqrt(v_hat) + ADAM_EPS) + ADAM_WD * w_ref[...])
        nm_ref[...] = nm
        nv_ref[...] = nv

    def piece(s):
        return pl.BlockSpec((tr, C), lambda i: (jnp.clip(i - s * per, 0, per - 1), 0))

    blk = pl.BlockSpec((tr, C), lambda i: (i, 0))
    return pl.pallas_call(
        kern, name="adamw", grid=(R // tr,), in_specs=[blk] + [piece(s) for s in owner] + [blk] * 2,
        out_specs=[blk] * 4, out_shape=[jax.ShapeDtypeStruct((R, C), F32)] * 4, compiler_params=_params(1),
    )(w, *flat, m, v)


def _place():
    x, y, c = lax.axis_index("x"), lax.axis_index("y"), lax.axis_index("c")
    return x, y, c, [(1 - x, y), (x, 1 - y), (1 - x, 1 - y)]


def _comm_call(name, body, operands, out_shape, n_sems, aliases=None):
    any_spec = pl.BlockSpec(memory_space=pl.ANY)
    return pl.pallas_call(
        body, name=name, in_specs=[any_spec] * len(operands), out_specs=[any_spec] * len(out_shape),
        out_shape=out_shape, input_output_aliases=aliases or {},
        scratch_shapes=[pltpu.SemaphoreType.DMA((n,)) for n in n_sems],
    )(*operands)


def _my_block():
    return 2 * lax.axis_index("x") + lax.axis_index("y")


def _place_shard(w, dtype, first=0, count=None):
    L, R, C = w.shape
    count = L if count is None else count
    tr = _tile(R, 512)

    def kern(w_ref, o_ref):
        o_ref[...] = w_ref[...].astype(dtype)

    return pl.pallas_call(
        kern, name="place_shard", grid=(count, R // tr),
        in_specs=[pl.BlockSpec((None, tr, C), lambda l, i: (first + l, i, 0))],
        out_specs=pl.BlockSpec((None, None, tr, C), lambda l, i: (l, _my_block(), i, 0)),
        out_shape=jax.ShapeDtypeStruct((count, N_CHIPS, R, C), dtype), compiler_params=_params(2),
    )(w)


HALF_ROWS = 16


def _rows(ref, half):
    hr = ref.shape[-2] // 2
    return ref.at[(slice(None),) * (len(ref.shape) - 2) + (pl.ds(half * hr, hr),)]


def _all_gather(bufs):
    n = len(bufs)

    def body(*refs):
        outs = refs[n:2 * n]
        send_sems, recv_sems = refs[2 * n:]
        x, y, c, chips = _place()
        sibling = (x, y, 1 - c)

        def remote(k, j, chip, half, to):
            blk = 2 * chip[0] + chip[1]
            if outs[k].shape[2] % (2 * HALF_ROWS) == 0:
                region = _rows(outs[k].at[:, blk], half)
            else:
                hl = outs[k].shape[0] // 2
                region = outs[k].at[pl.ds(half * hl, hl), blk]
            return pltpu.make_async_remote_copy(
                src_ref=region, dst_ref=region, send_sem=send_sems.at[6 * k + j],
                recv_sem=recv_sems.at[6 * k + j], device_id=to, device_id_type=MESH)

        first = [remote(k, j, (x, y), c, (*chip, c)) for k in range(n) for j, chip in enumerate(chips)]
        for cp in first:
            cp.start()
        passed = []
        for k in range(n):
            for j, chip in enumerate(chips):
                remote(k, j, chip, c, (x, y, c)).wait_recv()
                passed.append(remote(k, 3 + j, chip, c, sibling))
                passed[-1].start()
        for k in range(n):
            for j, chip in enumerate(chips):
                remote(k, 3 + j, chip, 1 - c, (x, y, c)).wait_recv()
        for cp in first + passed:
            cp.wait_send()

    out_shape = [jax.ShapeDtypeStruct(s.shape, s.dtype) for s in bufs]
    return _comm_call("all_gather_weights", body, bufs, out_shape, (6 * n, 6 * n), aliases={k: k for k in range(n)})


_HBM = pl.BlockSpec(memory_space=pltpu.HBM)
_SEM = pl.BlockSpec(memory_space=pltpu.SEMAPHORE)
_ANY = pl.BlockSpec(memory_space=pl.ANY)


def _split_start(name, bufs, n_copies, make_copies, after):
    n = len(bufs)

    def body(*refs):
        send_sems, recv_sems, token = refs[n + 1], refs[n + 2], refs[2 * n + 3]
        for cp in make_copies(refs[:n], send_sems, recv_sems):
            cp.start()
        token[...] = jnp.zeros_like(token)

    res = pl.pallas_call(
        body, name=name, in_specs=[_HBM] * n + [_ANY],
        out_shape=(pltpu.SemaphoreType.DMA((n_copies,)), pltpu.SemaphoreType.DMA((n_copies,)),
                   *[pltpu.HBM(b.shape, b.dtype) for b in bufs], jax.ShapeDtypeStruct((8, LANES), F32)),
        out_specs=(_SEM, _SEM, *[_HBM] * n, pl.BlockSpec(memory_space=pltpu.VMEM)),
        input_output_aliases={i: 2 + i for i in range(n)},
        compiler_params=pltpu.CompilerParams(has_side_effects=pltpu.SideEffectType.DATAFLOW_SIDE_EFFECTING),
    )(*[pltpu.with_memory_space_constraint(b, pltpu.HBM) for b in bufs], after)
    return res[0], res[1], list(res[2:2 + n]), res[2 + n]


def _split_wait(name, send_sems, recv_sems, bufs, make_copies, after):
    n = len(bufs)

    def body(*refs):
        for cp in make_copies(refs[:n], refs[n], refs[n + 1]):
            cp.wait_send()
            cp.wait_recv()

    return list(pl.pallas_call(
        body, name=name, in_specs=[_HBM] * n + [_SEM, _SEM, _ANY],
        out_shape=tuple(pltpu.HBM(b.shape, b.dtype) for b in bufs), out_specs=tuple([_HBM] * n),
        input_output_aliases={i: i for i in range(n)},
        compiler_params=pltpu.CompilerParams(has_side_effects=pltpu.SideEffectType.DATAFLOW_SIDE_EFFECTING),
    )(*bufs, send_sems, recv_sems, after))


def _gather_copies(refs, send_sems, recv_sems):
    x, y, c, chips = _place()
    return [pltpu.make_async_remote_copy(
        src_ref=ref.at[:, 2 * x + y], dst_ref=ref.at[:, 2 * x + y], send_sem=send_sems.at[3 * k + j],
        recv_sem=recv_sems.at[3 * k + j], device_id=(*chip, c), device_id_type=MESH)
        for k, ref in enumerate(refs) for j, chip in enumerate(chips)]


def _exchange_copies(refs, send_sems, recv_sems):
    n = len(refs) // 2
    x, y, c, chips = _place()
    return [pltpu.make_async_remote_copy(
        src_ref=refs[k].at[:, 2 * chip[0] + chip[1]], dst_ref=refs[n + k].at[j], send_sem=send_sems.at[3 * k + j],
        recv_sem=recv_sems.at[3 * k + j], device_id=(*chip, c), device_id_type=MESH)
        for k in range(n) for j, chip in enumerate(chips)]


def _rs_swap_halves(grads):
    n = len(grads)

    def body(*refs):
        ins, outs = refs[:n], refs[n:2 * n]
        send_sems, recv_sems = refs[2 * n:]
        x, y, c, _ = _place()
        copies = [pltpu.make_async_remote_copy(
            src_ref=_rows(ins[k], 1 - c), dst_ref=outs[k], send_sem=send_sems.at[k],
            recv_sem=recv_sems.at[k], device_id=(x, y, 1 - c), device_id_type=MESH) for k in range(n)]
        for cp in copies:
            cp.start()
        for cp in copies:
            cp.wait()

    out_shape = [jax.ShapeDtypeStruct(g.shape[:2] + (g.shape[2] // 2, g.shape[3]), g.dtype) for g in grads]
    return _comm_call("rs_swap_halves", body, grads, out_shape, (n, n))


def _rs_exchange(parts):
    n = len(parts)

    def body(*refs):
        ins, outs = refs[:n], refs[n:2 * n]
        send_sems, recv_sems = refs[2 * n:]
        x, y, c, chips = _place()
        copies = [pltpu.make_async_remote_copy(
            src_ref=ins[k].at[:, 2 * chip[0] + chip[1]], dst_ref=outs[k].at[j], send_sem=send_sems.at[3 * k + j],
            recv_sem=recv_sems.at[3 * k + j], device_id=(*chip, c), device_id_type=MESH)
            for k in range(n) for j, chip in enumerate(chips)]
        for cp in copies:
            cp.start()
        for cp in copies:
            cp.wait()

    out_shape = [jax.ShapeDtypeStruct((3, p.shape[0]) + p.shape[2:], p.dtype) for p in parts]
    return _comm_call("rs_exchange", body, parts, out_shape, (3 * n, 3 * n))


def _rs_share(bufs):
    n = len(bufs)

    def body(*refs):
        outs = refs[n:2 * n]
        send_sems, recv_sems = refs[2 * n:]
        x, y, c, _ = _place()

        def half(k, which):
            region = _rows(outs[k], which)
            return pltpu.make_async_remote_copy(
                src_ref=region, dst_ref=region, send_sem=send_sems.at[k], recv_sem=recv_sems.at[k],
                device_id=(x, y, 1 - c), device_id_type=MESH)

        sends = [half(k, c) for k in range(n)]
        for cp in sends:
            cp.start()
        for k in range(n):
            half(k, 1 - c).wait_recv()
        for cp in sends:
            cp.wait_send()

    out_shape = [jax.ShapeDtypeStruct(h.shape, h.dtype) for h in bufs]
    return _comm_call("rs_share", body, bufs, out_shape, (n, n), aliases={k: k for k in range(n)})


def _all_reduce_small(v):
    n = v.shape[0]

    def body(v_ref, o_ref, gbuf, send_sems, recv_sems):
        x, y, c, _ = _place()
        me = 4 * x + 2 * y + c
        gbuf[me] = v_ref[...]
        copies, waits = [], []
        for r in range(1, N_DEV):
            px = 1 - x if r & 4 else x
            py = 1 - y if r & 2 else y
            pc = 1 - c if r & 1 else c
            mk = functools.partial(pltpu.make_async_remote_copy, src_ref=v_ref, send_sem=send_sems.at[r - 1],
                                   recv_sem=recv_sems.at[r - 1], device_id=(px, py, pc), device_id_type=MESH)
            copies.append(mk(dst_ref=gbuf.at[me]))
            waits.append(mk(dst_ref=gbuf.at[4 * px + 2 * py + pc]))
        for cp in copies:
            cp.start()
        for cp in waits:
            cp.wait_recv()
        for cp in copies:
            cp.wait_send()
        acc = gbuf[0]
        for d in range(1, N_DEV):
            acc = acc + gbuf[d]
        o_ref[...] = acc

    vm = pl.BlockSpec(memory_space=pltpu.VMEM)
    return pl.pallas_call(
        body, name="all_reduce_small", in_specs=[vm], out_specs=vm, out_shape=jax.ShapeDtypeStruct(v.shape, F32),
        scratch_shapes=[pltpu.VMEM((N_DEV, n, LANES), F32), pltpu.SemaphoreType.DMA((N_DEV - 1,)),
                        pltpu.SemaphoreType.DMA((N_DEV - 1,))],
        compiler_params=pltpu.CompilerParams(vmem_limit_bytes=VMEM_LIMIT),
    )(v)


def _add_half(g, h1):
    L, nb, hr, C = h1.shape
    g3, h3 = g.reshape(L * nb, 2 * hr, C), h1.reshape(L * nb, hr, C)
    tr = _tile(hr, 512)

    def kern(g_ref, h_ref, o_ref):
        o_ref[...] = (g_ref[...].astype(F32) + h_ref[...].astype(F32)).astype(BF16)

    blk = pl.BlockSpec((None, tr, C), lambda l, i: (l, i, 0))
    out = pl.pallas_call(
        kern, name="rs_add_half", grid=(L * nb, hr // tr),
        in_specs=[pl.BlockSpec((None, tr, C), lambda l, i: (l, lax.axis_index("c") * (hr // tr) + i, 0)), blk],
        out_specs=blk, out_shape=jax.ShapeDtypeStruct(h3.shape, BF16), compiler_params=_params(2),
    )(g3, h3)
    return out.reshape(h1.shape)


def _add_blocks(p, h2, half=True):
    L, nb, hr, C = p.shape
    tr = _tile(hr, 512)
    shift = lambda: lax.axis_index("c") * (hr // tr) if half else 0

    def kern(p_ref, h0_ref, h1_ref, h2_ref, o_ref):
        o_ref[...] = ((p_ref[...].astype(F32) + h0_ref[...].astype(F32)) + h1_ref[...].astype(F32)) + h2_ref[...].astype(F32)

    def other(j):
        return pl.BlockSpec((None, None, tr, C), lambda l, i: (j, l, i, 0))

    return pl.pallas_call(
        kern, name="rs_add_blocks", grid=(L, hr // tr),
        in_specs=[pl.BlockSpec((None, None, tr, C), lambda l, i: (l, _my_block(), i, 0)), other(0), other(1), other(2)],
        out_specs=pl.BlockSpec((None, tr, C), lambda l, i: (l, shift() + i, 0)),
        out_shape=jax.ShapeDtypeStruct((L, (2 if half else 1) * hr, C), F32), compiler_params=_params(2),
    )(p, h2, h2, h2)


def _d2d_swap(arrays):
    n = len(arrays)

    def body(*refs):
        ins, outs = refs[:n], refs[n:2 * n]
        send_sems, recv_sems = refs[2 * n:]
        x, y, c, _ = _place()
        copies = [pltpu.make_async_remote_copy(
            src_ref=ins[k], dst_ref=outs[k], send_sem=send_sems.at[k], recv_sem=recv_sems.at[k],
            device_id=(x, y, 1 - c), device_id_type=MESH) for k in range(n)]
        for cp in copies:
            cp.start()
        for cp in copies:
            cp.wait()

    return _comm_call("d2d_swap", body, arrays, [jax.ShapeDtypeStruct(a.shape, a.dtype) for a in arrays], (n, n))


def _reduce_scatter(grads):
    sib = _rs_swap_halves(grads)
    parts = [_add_half(g, h) for g, h in zip(grads, sib)]
    others = _rs_exchange(parts)
    return _rs_share([_add_blocks(p, o) for p, o in zip(parts, others)])


WEIGHTS = ("norm_ffn1", "w_ffn1_in", "w_ffn1_out", "norm_mix", "w_mix_in", "b_forget", "w_pool", "pool_scale",
           "conv_w", "w_mix_out", "norm_ffn2", "w_ffn2_in", "w_ffn2_out", "norm_final")
BIG = ("w_ffn1_in", "w_ffn1_out", "w_mix_in", "w_mix_out", "w_ffn2_in", "w_ffn2_out")
SMALL = ("norm_ffn1", "norm_mix", "b_forget", "w_pool", "pool_scale", "conv_w", "norm_ffn2", "norm_final")


def _prep_weights(small, gathered, conv_w, D, first):
    DA, C, H = D // 2, D // 4, D // 2 // HEAD_DIM
    L = gathered["w_mix_in"].shape[0]
    small = {k: val[first:first + L] for k, val in small.items() if k != "norm_final"}
    gathered = dict(gathered, conv_w=conv_w[first:first + L])
    w_in = jnp.concatenate([gathered["w_mix_in"][:, b] for b in range(N_CHIPS)], axis=2)
    wqkv, wrest = w_in[:, :, :3 * DA], w_in[:, :, 3 * DA + H:]
    wf = jnp.pad(w_in[:, :, 3 * DA:3 * DA + H], ((0, 0), (0, 0), (0, LANES - H)))
    ng = len(POOL_WINDOWS)
    same_group = jnp.eye(ng, dtype=bool)[None, :, None, :, None]
    wbd = jnp.where(same_group, small["w_pool"][:, :, :, None, :], 0.0).reshape(L, C, C)
    cw = jnp.concatenate([gathered["conv_w"][:, b] for b in range(N_CHIPS)], axis=2)
    return dict(
        g1=small["norm_ffn1"], gm=small["norm_mix"], g2=small["norm_ffn2"],
        w1in=gathered["w_ffn1_in"], w1out=gathered["w_ffn1_out"].reshape(L, -1, D),
        wp=jnp.concatenate([wqkv, wrest, wf], axis=2), wmixout=gathered["w_mix_out"].reshape(L, D, D),
        bias=jnp.pad(small["b_forget"], ((0, 0), (0, LANES - H))), wbd=wbd.astype(BF16), ps=small["pool_scale"],
        cw=jnp.pad(cw, ((0, 0), (0, 8 - CONV_WIDTH), (0, 0))),
    )


def _layer_params(l, W):
    P = {k: (W[k], l) for k in ("w1in", "w1out", "wp", "wmixout")}
    P.update({k: W[k][l][None] for k in ("g1", "gm", "g2", "bias", "ps")})
    P.update(wbd=W["wbd"][l], cw=W["cw"][l])
    return P


def _ffn_fwd(x, g, w_in, w_out, token=None):
    h, jac, act = _ffn_up(x, g, w_in, token)
    return _ffn_out(act, w_out, x)[0], (x, h, jac, act)


def _ffn_bwd(dres, saved, g, w_in, w_out, token=None):
    x, h, jac, act = saved
    dgu, dx, dg = _ffn_bwd_main(dres, jac, x, g, w_out, w_in, token)
    dw_out = _ffn_dw_out(act, dres)[0]
    dw_in = _ffn_dw_in(h, dgu)[0]
    return dx, dg, dw_in, dw_out.reshape(N_CHIPS, -1, dw_out.shape[1])


def _mixer_fwd(x, P, B, S, tq):
    T, D = x.shape
    DA, C, H = D // 2, D // 4, D // 2 // HEAD_DIM
    hn, qkv, rest, fl = _mix_up(x, P["gm"], P["wp"], (3 * DA, 4 * C))
    qkv, rest, fl = qkv.reshape(B, S, 3 * DA), rest.reshape(B, S, 4 * C), fl.reshape(B, S, LANES)
    drow = _decay_fwd(fl, P["bias"]).reshape(B, 8, S // tq, tq)
    o, lse = _attn_fwd(qkv, drow, H, tq)
    ypc, pooled = _mix_local_fwd(rest, P["wbd"], P["ps"], P["cw"])
    x_out = _mix_out([o.reshape(T, DA), ypc.reshape(T, 2 * C)], P["wmixout"], x)
    return x_out, (x, hn, qkv, rest, fl, drow, o, lse, pooled, ypc)


def _mixer_bwd(dres, saved, P, B, S, tq):
    x, hn, qkv, rest, fl, drow, o, lse, pooled, ypc = saved
    T, D = x.shape
    DA, C, H = D // 2, D // 4, D // 2 // HEAD_DIM
    dycat = _proj("mix_out_bwd", dres, P["wmixout"], F32, NT).reshape(B, S, D)
    dw_out = _rows_dw("mix_out_dw", [o.reshape(T, DA), ypc.reshape(T, 2 * C)], dres, BF16)
    dq, dk, dv, ddrow, ddcol = _attn_bwd(qkv, drow, o, lse, dycat, H, tq)
    dfl, dbias = _decay_bwd(ddrow.reshape(B, 8, S), ddcol, fl, P["bias"], H)
    drest, dwbd, dps, dcw = _mix_local_bwd(rest, pooled, dycat, P["wbd"], P["ps"], P["cw"])
    pieces = [a.reshape(T, a.shape[-1]) for a in (dq, dk, dv, drest, dfl)]
    dwp = _pieces_dw("mix_in_dw", hn, pieces, F32, 512)
    dx, dg = _mix_in_bwd(pieces, x, P["gm"], dres, P["wp"])
    n_q, n_r = 3 * DA, 4 * C
    dw_in = jnp.concatenate([dwp[:, :n_q], dwp[:, n_q + n_r:n_q + n_r + H], dwp[:, n_q:n_q + n_r]], axis=1)
    dw_in = dw_in.reshape(D, N_CHIPS, -1).transpose(1, 0, 2).astype(BF16)
    ng = len(POOL_WINDOWS)
    same_group = jnp.eye(ng, dtype=bool)[:, None, :, None]
    dw_pool = jnp.where(same_group, dwbd.reshape(ng, C // ng, ng, C // ng), 0.0).sum(axis=2)
    small = dict(norm_mix=dg[0], b_forget=dbias[0, :H], w_pool=dw_pool, pool_scale=dps[0], conv_w=dcw[:CONV_WIDTH])
    return dx, small, dw_in, dw_out.reshape(N_CHIPS, -1, D)


EARLY = ("w_ffn1_in", "w_ffn1_out", "w_mix_in", "w_mix_out")
FFN2 = ("w_ffn2_in", "w_ffn2_out")


def _local_step(x, target, small, conv_w, pipe):
    B, S, D = x.shape
    L = small["norm_ffn1"].shape[0]
    tq = _tile(S, 256)
    xt = x.reshape(B * S, D)
    saved, params = [], []
    for l in range(L):
        P = _layer_params(0, _prep_weights(small, pipe.weights(l, xt), conv_w, D, l))
        xt, s1 = _ffn_fwd(xt, P["g1"], P["w1in"], P["w1out"], pipe.token(l))
        xt, s2 = _mixer_fwd(xt, P, B, S, tq)
        w2 = pipe.weights_ffn2(l, xt)
        P.update(w2in=(w2["w_ffn2_in"], 0), w2out=(w2["w_ffn2_out"].reshape(1, -1, D), 0))
        xt, s3 = _ffn_fwd(xt, P["g2"], P["w2in"], P["w2out"])
        saved.append((s1, s2, s3))
        params.append(P)
    dres, dgf, loss = _final_loss(xt, small["norm_final"][None], target.reshape(B * S, D))
    sm = {k: [None] * L for k in SMALL if k != "norm_final"}
    token = None
    for l in reversed(range(L)):
        P, (s1, s2, s3) = params[l], saved[l]
        big = {}
        dres, dg2, big["w_ffn2_in"], big["w_ffn2_out"] = _ffn_bwd(dres, s3, P["g2"], P["w2in"], P["w2out"], token)
        dres, smix, big["w_mix_in"], big["w_mix_out"] = _mixer_bwd(dres, s2, P, B, S, tq)
        big = {k: val[None] for k, val in big.items()}
        token = pipe.grads(l, FFN2 + EARLY[2:], big, dres) if l == 0 else None
        dres, dg1, dw_in, dw_out = _ffn_bwd(dres, s1, P["g1"], P["w1in"], P["w1out"], token)
        big.update(w_ffn1_in=dw_in[None], w_ffn1_out=dw_out[None])
        sm["norm_ffn1"][l], sm["norm_ffn2"][l] = dg1[0], dg2[0]
        for k, val in smix.items():
            sm[k][l] = val
        token = pipe.grads(l, EARLY[:2] if l == 0 else BIG, big, dres)
    sm = {k: jnp.stack(val) for k, val in sm.items()}
    sm["norm_final"] = dgf[0]
    return loss[0, 0], dres.reshape(B, S, D), sm


def _sibling_copies(refs, send_sems, recv_sems):
    n = len(refs) // 2
    x, y, c, _ = _place()
    return [pltpu.make_async_remote_copy(
        src_ref=refs[k], dst_ref=refs[n + k], send_sem=send_sems.at[k], recv_sem=recv_sems.at[k],
        device_id=(x, y, 1 - c), device_id_type=MESH) for k in range(n)]


class _Pipeline:
    def __init__(self, w):
        self.w, self.n_layers = w, w[BIG[0]].shape[0]
        first = _all_gather([_place_shard(w[k], BF16, 0, 1) for k in EARLY] + [_place_shard(w["conv_w"], F32)])
        self.conv_w = first[-1]
        self._ready = dict(zip(EARLY, first[:-1]))
        self._ffn2 = self._start_gather("0b", FFN2, 0, first[0])
        self._next = (1, self._start_gather("1", BIG, 1, self._ffn2[1][3]))
        self._reduce, self._swaps = None, []
        self.reduced = [dict() for _ in range(self.n_layers)]

    def _start_gather(self, tag, kinds, l, after):
        placed = [_place_shard(self.w[k], BF16, l, 1) for k in kinds]
        return kinds, _split_start(f"gather_start_{tag}", placed, 3 * len(kinds), _gather_copies, after)

    def _wait_gather(self, tag, started, after):
        kinds, (send_sems, recv_sems, bufs, _) = started
        return dict(zip(kinds, _split_wait(f"gather_wait_{tag}", send_sems, recv_sems, bufs, _gather_copies, after)))

    def token(self, l):
        return self._next[1][1][3] if self._next is not None and self._next[0] == l + 1 else None

    def weights(self, l, after):
        if l == 0:
            return self._ready
        self._layer = self._wait_gather(str(l), self._next[1], after)
        first = next(iter(self._layer.values()))
        self._next = (l + 1, self._start_gather(str(l + 1), BIG, l + 1, first)) if l + 1 < self.n_layers else None
        return self._layer

    def weights_ffn2(self, l, after):
        return self._wait_gather("0b", self._ffn2, after) if l == 0 else self._layer

    def _finish_reduce(self, after):
        if self._reduce is None:
            return
        tag, l, kinds, (send_sems, recv_sems, bufs, _) = self._reduce
        n = len(kinds)
        bufs = _split_wait(f"reduce_wait_{tag}", send_sems, recv_sems, bufs, _exchange_copies, after)
        mine = [_add_blocks(p, o, half=False) for p, o in zip(bufs[:n], bufs[n:])]
        lands = [lax.empty(q.shape, q.dtype) for q in mine]
        self._swaps.append((tag, l, kinds, _split_start(f"swap_start_{tag}", mine + lands, n, _sibling_copies, mine[0])))
        self._reduce = None

    def grads(self, l, kinds, big, after):
        self._finish_reduce(after)
        grads = [big[k] for k in kinds]
        if l == 0 and kinds[0] == EARLY[0]:
            self.reduced[0].update(zip(kinds, _reduce_scatter(grads)))
            return None
        tag = f"{l}{'b' if len(kinds) < len(BIG) else ''}"
        lands = [lax.empty((3, g.shape[0]) + g.shape[2:], g.dtype) for g in grads]
        started = _split_start(f"reduce_start_{tag}", grads + lands, 3 * len(kinds), _exchange_copies, grads[0])
        self._reduce = (tag, l, kinds, started)
        return started[3]

    def finish(self, after):
        self._finish_reduce(after)
        for tag, l, kinds, (send_sems, recv_sems, bufs, _) in self._swaps:
            n = len(kinds)
            bufs = _split_wait(f"swap_wait_{tag}", send_sems, recv_sems, bufs, _sibling_copies, after)
            self.reduced[l].update(zip(kinds, zip(bufs[:n], bufs[n:])))
        return self.reduced


def _pack(parts, extra=()):
    flat = jnp.concatenate([p.reshape(-1) for p in parts] + [jnp.reshape(e, (1,)) for e in extra])
    n = -(-flat.shape[0] // (8 * LANES)) * 8
    return jnp.pad(flat, (0, n * LANES - flat.shape[0])).reshape(n, LANES)


def _unpack(buf, shapes):
    flat, out, at = buf.reshape(-1), [], 0
    for s in shapes:
        n = math.prod(s)
        out.append(flat[at:at + n].reshape(s))
        at += n
    return out, flat[at:]


def kernel(x, norm_ffn1, w_ffn1_in, w_ffn1_out, norm_mix, w_mix_in, b_forget, w_pool, pool_scale, conv_w, w_mix_out, norm_ffn2, w_ffn2_in, w_ffn2_out, norm_final, loss_target, m_norm_ffn1, m_w_ffn1_in, m_w_ffn1_out, m_norm_mix, m_w_mix_in, m_b_forget, m_w_pool, m_pool_scale, m_conv_w, m_w_mix_out, m_norm_ffn2, m_w_ffn2_in, m_w_ffn2_out, m_norm_final, v_norm_ffn1, v_w_ffn1_in, v_w_ffn1_out, v_norm_mix, v_w_mix_in, v_b_forget, v_w_pool, v_pool_scale, v_conv_w, v_w_mix_out, v_norm_ffn2, v_w_ffn2_in, v_w_ffn2_out, v_norm_final):
    w = dict(zip(WEIGHTS, (norm_ffn1, w_ffn1_in, w_ffn1_out, norm_mix, w_mix_in, b_forget, w_pool, pool_scale, conv_w, w_mix_out, norm_ffn2, w_ffn2_in, w_ffn2_out, norm_final)))
    m = dict(zip(WEIGHTS, (m_norm_ffn1, m_w_ffn1_in, m_w_ffn1_out, m_norm_mix, m_w_mix_in, m_b_forget, m_w_pool, m_pool_scale, m_conv_w, m_w_mix_out, m_norm_ffn2, m_w_ffn2_in, m_w_ffn2_out, m_norm_final)))
    v = dict(zip(WEIGHTS, (v_norm_ffn1, v_w_ffn1_in, v_w_ffn1_out, v_norm_mix, v_w_mix_in, v_b_forget, v_w_pool, v_pool_scale, v_conv_w, v_w_mix_out, v_norm_ffn2, v_w_ffn2_in, v_w_ffn2_out, v_norm_final)))
    block = 2 * lax.axis_index("x") + lax.axis_index("y")

    pipe = _Pipeline(w)
    small = {k: w[k] for k in SMALL}
    loss, grad_x, sm = _local_step(x, loss_target, small, pipe.conv_w, pipe)
    reduced = pipe.finish(grad_x)
    grads, order = {}, list(SMALL)
    total = _all_reduce_small(_pack([sm[k] for k in order], extra=(loss,)))
    parts, rest = _unpack(total, [sm[k].shape for k in order])
    grads.update(zip(order, parts))
    loss = rest[0]
    cs = conv_w.shape[2]
    grads["conv_w"] = lax.dynamic_slice_in_dim(grads["conv_w"], block * cs, cs, axis=2)

    delta, new_m, new_v = {}, {}, {}
    for k in BIG:
        two_d = lambda a: a.reshape(-1, a.shape[-1])
        pieces = [tuple(map(two_d, g)) if isinstance(g, tuple) else two_d(g) for g in (layer[k] for layer in reduced)]
        res = _adamw(two_d(w[k]), pieces, two_d(m[k]), two_d(v[k]))
        grads[k], delta[k], new_m[k], new_v[k] = [r.reshape(w[k].shape) for r in res]
    packed = [_pack([t[k] for k in order]) for t in (w, grads, m, v)]
    _, d, nm, nv = _adamw(packed[0], [packed[1]], packed[2], packed[3])
    shapes = [w[k].shape for k in order]
    for res, flat in ((delta, d), (new_m, nm), (new_v, nv)):
        res.update(zip(order, _unpack(flat, shapes)[0]))
    return (loss, grad_x, *[grads[k] for k in WEIGHTS], *[delta[k] for k in WEIGHTS],
            *[new_m[k] for k in WEIGHTS], *[new_v[k] for k in WEIGHTS])
```

```python
import functools
import math

import jax
import jax.numpy as jnp
from jax import lax
from jax.experimental import pallas as pl
from jax.experimental.pallas import tpu as pltpu

F32 = jnp.float32
BF16 = jnp.bfloat16
MESH = pl.DeviceIdType.MESH

HEAD_DIM = 64
POOL_WINDOWS = (2, 4, 8, 16)
CONV_WIDTH = 3
RMS_EPS = 1e-6
ADAM_LR = 0.001
ADAM_B1 = 0.9
ADAM_B2 = 0.999
ADAM_EPS = 1e-08
ADAM_WD = 0.01
ADAM_STEP = 10

LANES = 128
VMEM_LIMIT = 56 * 1024 * 1024
N_CHIPS = 4
N_DEV = 8

NN = (((1,), (0,)), ((), ()))
NT = (((1,), (1,)), ((), ()))
TN = (((0,), (0,)), ((), ()))


def _tile(n, pref):
    for t in range(pref - pref % 16, 15, -16):
        if n % t == 0:
            return t
    return n


def _params(n_grid):
    return pltpu.CompilerParams(dimension_semantics=("arbitrary",) * n_grid, vmem_limit_bytes=VMEM_LIMIT)


def _dot(a, b, dims):
    return lax.dot_general(a, b, dims, preferred_element_type=F32)


def _mm(name, dims, operands, in_specs, out_shape, out_specs, grid, acc_shape, epilogue):
    n_in, n_out, nk = len(operands), len(out_shape), grid[-1]

    def kern(*refs):
        extras, outs = refs[2:n_in], refs[n_in:n_in + n_out]
        if nk == 1:
            epilogue(_dot(refs[0][...].astype(BF16), refs[1][...].astype(BF16), dims), extras, outs)
            return
        acc = refs[n_in + n_out]
        k = pl.program_id(len(grid) - 1)

        @pl.when(k == 0)
        def _():
            acc[...] = jnp.zeros_like(acc)

        acc[...] += _dot(refs[0][...].astype(BF16), refs[1][...].astype(BF16), dims)

        @pl.when(k == nk - 1)
        def _():
            epilogue(acc[...], extras, outs)

    return pl.pallas_call(
        kern, name=name, grid=grid, in_specs=in_specs, out_specs=out_specs, out_shape=out_shape,
        scratch_shapes=[pltpu.VMEM(acc_shape, F32)] if nk > 1 else [],
        compiler_params=_params(len(grid)),
    )(*operands)


def _store(scale=None, dtype=None):
    def ep(acc, extras, outs):
        v = acc if scale is None else acc * scale
        outs[0][...] = v.astype(outs[0].dtype)
    return ep


def _residual(scale):
    def ep(acc, extras, outs):
        outs[0][...] = extras[0][...] + scale * acc
    return ep


def _rmsnorm_fwd(x, g):
    T, D = x.shape
    tr = _tile(T, 512)

    def kern(x_ref, g_ref, o_ref):
        xv = x_ref[...]
        r = lax.rsqrt(jnp.mean(xv * xv, axis=-1, keepdims=True) + RMS_EPS)
        o_ref[...] = (xv * r * g_ref[...]).astype(BF16)

    return pl.pallas_call(
        kern, name="rmsnorm_fwd", grid=(T // tr,),
        in_specs=[pl.BlockSpec((tr, D), lambda i: (i, 0)), pl.BlockSpec((1, D), lambda i: (0, 0))],
        out_specs=pl.BlockSpec((tr, D), lambda i: (i, 0)),
        out_shape=jax.ShapeDtypeStruct((T, D), BF16), compiler_params=_params(1),
    )(x, g)


def _rmsnorm_bwd(x, g, dh, dres):
    T, D = x.shape
    tr = _tile(T, 256)

    def kern(x_ref, g_ref, dh_ref, dres_ref, dx_ref, dg_ref):
        xv, dhv = x_ref[...], dh_ref[...]
        r = lax.rsqrt(jnp.mean(xv * xv, axis=-1, keepdims=True) + RMS_EPS)
        y = xv * r
        dy = dhv * g_ref[...]
        dx_ref[...] = dres_ref[...] + r * (dy - y * jnp.mean(dy * y, axis=-1, keepdims=True))
        part = jnp.sum(dhv * y, axis=0, keepdims=True)

        @pl.when(pl.program_id(0) == 0)
        def _():
            dg_ref[...] = part

        @pl.when(pl.program_id(0) > 0)
        def _():
            dg_ref[...] += part

    row = pl.BlockSpec((tr, D), lambda i: (i, 0))
    vec = pl.BlockSpec((1, D), lambda i: (0, 0))
    return pl.pallas_call(
        kern, name="rmsnorm_bwd", grid=(T // tr,), in_specs=[row, vec, row, row], out_specs=[row, vec],
        out_shape=[jax.ShapeDtypeStruct((T, D), F32), jax.ShapeDtypeStruct((1, D), F32)],
        compiler_params=_params(1),
    )(x, g, dh, dres)


def _final_loss(x, g, target):
    T, D = x.shape
    tr = _tile(T, 256)

    def kern(x_ref, g_ref, t_ref, dx_ref, dg_ref, loss_ref):
        xv = x_ref[...]
        r = lax.rsqrt(jnp.mean(xv * xv, axis=-1, keepdims=True) + RMS_EPS)
        y = xv * r
        err = y * g_ref[...] - t_ref[...]
        lpart = 0.5 * jnp.sum(jnp.mean(err * err, axis=-1, keepdims=True), axis=0, keepdims=True)
        dh = err * (1.0 / D)
        dy = dh * g_ref[...]
        dx_ref[...] = r * (dy - y * jnp.mean(dy * y, axis=-1, keepdims=True))
        part = jnp.sum(dh * y, axis=0, keepdims=True)
        lrow = jnp.broadcast_to(lpart, (1, LANES))

        @pl.when(pl.program_id(0) == 0)
        def _():
            dg_ref[...] = part
            loss_ref[...] = lrow

        @pl.when(pl.program_id(0) > 0)
        def _():
            dg_ref[...] += part
            loss_ref[...] += lrow

    row = pl.BlockSpec((tr, D), lambda i: (i, 0))
    vec = pl.BlockSpec((1, D), lambda i: (0, 0))
    return pl.pallas_call(
        kern, name="final_loss", grid=(T // tr,), in_specs=[row, vec, row],
        out_specs=[row, vec, pl.BlockSpec((1, LANES), lambda i: (0, 0))],
        out_shape=[jax.ShapeDtypeStruct((T, D), F32), jax.ShapeDtypeStruct((1, D), F32),
                   jax.ShapeDtypeStruct((1, LANES), F32)],
        compiler_params=_params(1),
    )(x, g, target)


def _ffn_in(h, w4):
    T, D = h.shape
    w4, l = w4
    Fh = w4.shape[3]
    F = 2 * Fh
    tm = _tile(T, 512)

    def kern(h_ref, wg_ref, wu_ref, jac_ref, act_ref):
        hv = h_ref[...]
        gate = _dot(hv, wg_ref[...], NN)
        up = _dot(hv, wu_ref[...], NN)
        sg = jax.nn.sigmoid(gate)
        silu = gate * sg
        jac_ref[0] = (up * (sg + silu * (1.0 - sg))).astype(BF16)
        jac_ref[1] = silu.astype(BF16)
        act_ref[...] = (silu * up).astype(BF16)

    return pl.pallas_call(
        kern, name="ffn_in", grid=(2, T // tm),
        in_specs=[pl.BlockSpec((tm, D), lambda j, i: (i, 0)),
                  pl.BlockSpec((None, None, D, Fh), lambda j, i: (l, j, 0, 0)),
                  pl.BlockSpec((None, None, D, Fh), lambda j, i: (l, 2 + j, 0, 0))],
        out_specs=[pl.BlockSpec((2, tm, Fh), lambda j, i: (0, i, j)),
                   pl.BlockSpec((tm, Fh), lambda j, i: (i, j))],
        out_shape=[jax.ShapeDtypeStruct((2, T, F), BF16), jax.ShapeDtypeStruct((T, F), BF16)],
        compiler_params=_params(2),
    )(h, w4, w4)


def _resident(shape, index_map):
    return pl.BlockSpec(shape, index_map, pipeline_mode=pl.Buffered(1))


def _token_operand(token):
    return ([], []) if token is None else ([token], [pl.BlockSpec(token.shape, lambda i: (0, 0))])


def _ffn_up(x, g, w4, token=None):
    T, D = x.shape
    w4, l = w4
    Fh = w4.shape[3]
    F = 2 * Fh
    tm = _tile(T, 512)
    tok_ops, tok_specs = _token_operand(token)

    def kern(x_ref, g_ref, w_ref, *rest):
        h_ref, jac_ref, act_ref = rest[len(tok_ops):]
        xv = x_ref[...]
        r = lax.rsqrt(jnp.mean(xv * xv, axis=-1, keepdims=True) + RMS_EPS)
        hv = (xv * r * g_ref[...]).astype(BF16)
        h_ref[...] = hv
        for j in range(2):
            cols = slice(j * Fh, (j + 1) * Fh)
            gate = _dot(hv, w_ref[j], NN)
            up = _dot(hv, w_ref[2 + j], NN)
            sg = jax.nn.sigmoid(gate)
            silu = gate * sg
            jac_ref[0, :, cols] = (up * (sg + silu * (1.0 - sg))).astype(BF16)
            jac_ref[1, :, cols] = silu.astype(BF16)
            act_ref[:, cols] = (silu * up).astype(BF16)

    return pl.pallas_call(
        kern, name="ffn_up", grid=(T // tm,),
        in_specs=[pl.BlockSpec((tm, D), lambda i: (i, 0)), pl.BlockSpec((1, D), lambda i: (0, 0)),
                  _resident((None, 4, D, Fh), lambda i: (l, 0, 0, 0))] + tok_specs,
        out_specs=[pl.BlockSpec((tm, D), lambda i: (i, 0)), pl.BlockSpec((2, tm, F), lambda i: (0, i, 0)),
                   pl.BlockSpec((tm, F), lambda i: (i, 0))],
        out_shape=[jax.ShapeDtypeStruct((T, D), BF16), jax.ShapeDtypeStruct((2, T, F), BF16),
                   jax.ShapeDtypeStruct((T, F), BF16)],
        compiler_params=_params(1),
    )(x, g, w4, *tok_ops)


def _ffn_bwd_main(dres, jac, x, g, w_out, w4, token=None):
    T, D = dres.shape
    w_out, l = w_out
    w4, _ = w4
    F = w_out.shape[1]
    Fh = F // 2
    tm = _tile(T, 256)
    tok_ops, tok_specs = _token_operand(token)

    def kern(d_ref, jac_ref, x_ref, g_ref, wo_ref, wi_ref, *rest):
        dgu_ref, dx_ref, dg_ref = rest[len(tok_ops):]
        dv = d_ref[...]
        d16 = dv.astype(BF16)
        dh = jnp.zeros((tm, D), F32)
        for j in range(2):
            cols = slice(j * Fh, (j + 1) * Fh)
            dact = 0.5 * _dot(d16, wo_ref[cols, :], NT)
            dgate = (dact * jac_ref[0, :, cols].astype(F32)).astype(BF16)
            dup = (dact * jac_ref[1, :, cols].astype(F32)).astype(BF16)
            dgu_ref[0, :, cols] = dgate
            dgu_ref[1, :, cols] = dup
            dh = dh + _dot(dgate, wi_ref[j], NT) + _dot(dup, wi_ref[2 + j], NT)
        xv = x_ref[...]
        r = lax.rsqrt(jnp.mean(xv * xv, axis=-1, keepdims=True) + RMS_EPS)
        y = xv * r
        dy = dh * g_ref[...]
        dx_ref[...] = dv + r * (dy - y * jnp.mean(dy * y, axis=-1, keepdims=True))
        part = jnp.sum(dh * y, axis=0, keepdims=True)

        @pl.when(pl.program_id(0) == 0)
        def _():
            dg_ref[...] = part

        @pl.when(pl.program_id(0) > 0)
        def _():
            dg_ref[...] += part

    row = pl.BlockSpec((tm, D), lambda i: (i, 0))
    vec = pl.BlockSpec((1, D), lambda i: (0, 0))
    wide = pl.BlockSpec((2, tm, F), lambda i: (0, i, 0))
    return pl.pallas_call(
        kern, name="ffn_bwd_main", grid=(T // tm,),
        in_specs=[row, wide, row, vec, _resident((None, F, D), lambda i: (l, 0, 0)),
                  _resident((None, 4, D, Fh), lambda i: (l, 0, 0, 0))] + tok_specs,
        out_specs=[wide, row, vec],
        out_shape=[jax.ShapeDtypeStruct((2, T, F), BF16), jax.ShapeDtypeStruct((T, D), F32),
                   jax.ShapeDtypeStruct((1, D), F32)],
        compiler_params=_params(1),
    )(dres, jac, x, g, w_out, w4, *tok_ops)


def _ffn_out(act, w_out, x):
    T, F = act.shape
    w_out, l = w_out
    D = w_out.shape[2]
    tm = _tile(T, 512)
    return _mm("ffn_out", NN, [act, w_out, x],
               [pl.BlockSpec((tm, F), lambda i, k: (i, 0)), pl.BlockSpec((None, F, D), lambda i, k: (l, 0, 0)),
                pl.BlockSpec((tm, D), lambda i, k: (i, 0))],
               [jax.ShapeDtypeStruct((T, D), F32)], [pl.BlockSpec((tm, D), lambda i, k: (i, 0))],
               (T // tm, 1), None, _residual(0.5))


def _ffn_bwd_act(dres, w_out, jac):
    T, D = dres.shape
    w_out, l = w_out
    F = w_out.shape[1]
    Fh = F // 2
    tm = _tile(T, 512)

    def kern(d_ref, w_ref, jac_ref, o_ref):
        dact = 0.5 * _dot(d_ref[...].astype(BF16), w_ref[...], NT)
        o_ref[0] = (dact * jac_ref[0].astype(F32)).astype(BF16)
        o_ref[1] = (dact * jac_ref[1].astype(F32)).astype(BF16)

    return pl.pallas_call(
        kern, name="ffn_bwd_act", grid=(2, T // tm),
        in_specs=[pl.BlockSpec((tm, D), lambda j, i: (i, 0)), pl.BlockSpec((None, Fh, D), lambda j, i: (l, j, 0)),
                  pl.BlockSpec((2, tm, Fh), lambda j, i: (0, i, j))],
        out_specs=pl.BlockSpec((2, tm, Fh), lambda j, i: (0, i, j)),
        out_shape=jax.ShapeDtypeStruct((2, T, F), BF16), compiler_params=_params(2),
    )(dres, w_out, jac)


def _ffn_dw_out(act, dres):
    T, F = act.shape
    D = dres.shape[1]
    tm, tk = F // 2, _tile(T, 1024)
    return _mm("ffn_dw_out", TN, [act, dres],
               [pl.BlockSpec((tk, tm), lambda i, k: (k, i)), pl.BlockSpec((tk, D), lambda i, k: (k, 0))],
               [jax.ShapeDtypeStruct((F, D), BF16)], [pl.BlockSpec((tm, D), lambda i, k: (i, 0))],
               (2, T // tk), (tm, D), _store(0.5))


def _ffn_dw_in(h, dgu):
    T, D = h.shape
    Fh = dgu.shape[2] // 2
    tk = _tile(T, 1024)
    return _mm("ffn_dw_in", TN, [h, dgu],
               [pl.BlockSpec((tk, D), lambda j, k: (k, 0)),
                pl.BlockSpec((None, tk, Fh), lambda j, k: (j // 2, k, j % 2))],
               [jax.ShapeDtypeStruct((4, D, Fh), BF16)], [pl.BlockSpec((None, D, Fh), lambda j, k: (j, 0, 0))],
               (4, T // tk), (D, Fh), _store())


def _ffn_dh(dgu, w4):
    T = dgu.shape[1]
    w4, l = w4
    D, Fh = w4.shape[2], w4.shape[3]
    tm = _tile(T, 1024)
    return _mm("ffn_dh", NT, [dgu, w4],
               [pl.BlockSpec((None, tm, Fh), lambda i, k: (k // 2, i, k % 2)),
                pl.BlockSpec((None, None, D, Fh), lambda i, k: (l, k, 0, 0))],
               [jax.ShapeDtypeStruct((T, D), F32)], [pl.BlockSpec((tm, D), lambda i, k: (i, 0))],
               (T // tm, 4), (tm, D), _store())


def _proj(name, a, w, out_dtype, dims=NN, extra=None, scale=None):
    T, K = a.shape
    w, l = w
    N = w.shape[2] if dims == NN else w.shape[1]
    tm = _tile(T, 512)
    ops = [a, w] + ([extra] if extra is not None else [])
    specs = [pl.BlockSpec((tm, K), lambda i, k: (i, 0)), pl.BlockSpec((None,) + w.shape[1:], lambda i, k: (l, 0, 0))]
    if extra is not None:
        specs.append(pl.BlockSpec((tm, N), lambda i, k: (i, 0)))
    ep = _residual(1.0) if extra is not None else _store(scale)
    return _mm(name, dims, ops, specs, [jax.ShapeDtypeStruct((T, N), out_dtype)],
               [pl.BlockSpec((tm, N), lambda i, k: (i, 0))], (T // tm, 1), None, ep)[0]


def _mix_up(x, g, wp, widths):
    T, D = x.shape
    wp, l = wp
    n_qkv, n_rest = widths
    NP = wp.shape[2]
    tm = _tile(T, 512)

    def kern(x_ref, g_ref, w_ref, h_ref, qkv_ref, rest_ref, fl_ref):
        xv = x_ref[...]
        r = lax.rsqrt(jnp.mean(xv * xv, axis=-1, keepdims=True) + RMS_EPS)
        hv = (xv * r * g_ref[...]).astype(BF16)
        h_ref[...] = hv
        qkv_ref[...] = _dot(hv, w_ref[:, 0:n_qkv], NN).astype(BF16)
        rest_ref[...] = _dot(hv, w_ref[:, n_qkv:n_qkv + n_rest], NN)
        fl_ref[...] = _dot(hv, w_ref[:, n_qkv + n_rest:NP], NN)

    row = lambda n: pl.BlockSpec((tm, n), lambda i: (i, 0))
    return pl.pallas_call(
        kern, name="mix_up", grid=(T // tm,),
        in_specs=[row(D), pl.BlockSpec((1, D), lambda i: (0, 0)), _resident((None, D, NP), lambda i: (l, 0, 0))],
        out_specs=[row(D), row(n_qkv), row(n_rest), row(LANES)],
        out_shape=[jax.ShapeDtypeStruct((T, D), BF16), jax.ShapeDtypeStruct((T, n_qkv), BF16),
                   jax.ShapeDtypeStruct((T, n_rest), F32), jax.ShapeDtypeStruct((T, LANES), F32)],
        compiler_params=_params(1),
    )(x, g, wp)


def _column_starts(pieces):
    starts, at = [], 0
    for p in pieces:
        starts.append(at)
        at += p.shape[1]
    return starts


def _mix_in_bwd(pieces, x, g, dres, wp):
    T, D = x.shape
    wp, l = wp
    NP = wp.shape[2]
    tm = _tile(T, 512)
    n, starts = len(pieces), _column_starts(pieces)

    def kern(*refs):
        x_ref, g_ref, d_ref, w_ref, dx_ref, dg_ref = refs[n:]
        dh = jnp.zeros((tm, D), F32)
        for p_ref, at in zip(refs[:n], starts):
            dh = dh + _dot(p_ref[...].astype(BF16), w_ref[:, at:at + p_ref.shape[1]], NT)
        xv = x_ref[...]
        r = lax.rsqrt(jnp.mean(xv * xv, axis=-1, keepdims=True) + RMS_EPS)
        y = xv * r
        dy = dh * g_ref[...]
        dx_ref[...] = d_ref[...] + r * (dy - y * jnp.mean(dy * y, axis=-1, keepdims=True))
        part = jnp.sum(dh * y, axis=0, keepdims=True)

        @pl.when(pl.program_id(0) == 0)
        def _():
            dg_ref[...] = part

        @pl.when(pl.program_id(0) > 0)
        def _():
            dg_ref[...] += part

    row = lambda n: pl.BlockSpec((tm, n), lambda i: (i, 0))
    vec = pl.BlockSpec((1, D), lambda i: (0, 0))
    return pl.pallas_call(
        kern, name="mix_in_bwd", grid=(T // tm,),
        in_specs=[row(p.shape[1]) for p in pieces] + [row(D), vec, row(D), _resident((None, D, NP), lambda i: (l, 0, 0))],
        out_specs=[row(D), vec],
        out_shape=[jax.ShapeDtypeStruct((T, D), F32), jax.ShapeDtypeStruct((1, D), F32)],
        compiler_params=_params(1),
    )(*pieces, x, g, dres, wp)


def _pieces_dw(name, a, pieces, out_dtype, tk_pref):
    T, M = a.shape
    n, starts = len(pieces), _column_starts(pieces)
    N = starts[-1] + pieces[-1].shape[1]
    tk = _tile(T, tk_pref)
    nk = T // tk

    def kern(a_ref, *refs):
        o_ref, acc = refs[n], refs[n + 1]
        k = pl.program_id(0)

        @pl.when(k == 0)
        def _():
            acc[...] = jnp.zeros_like(acc)

        av = a_ref[...].astype(BF16)
        for p_ref, at in zip(refs[:n], starts):
            acc[:, at:at + p_ref.shape[1]] += _dot(av, p_ref[...].astype(BF16), TN)

        @pl.when(k == nk - 1)
        def _():
            o_ref[...] = acc[...].astype(out_dtype)

    return pl.pallas_call(
        kern, name=name, grid=(nk,),
        in_specs=[pl.BlockSpec((tk, M), lambda k: (k, 0))] + [pl.BlockSpec((tk, p.shape[1]), lambda k: (k, 0)) for p in pieces],
        out_specs=pl.BlockSpec((M, N), lambda k: (0, 0)), out_shape=jax.ShapeDtypeStruct((M, N), out_dtype),
        scratch_shapes=[pltpu.VMEM((M, N), F32)], compiler_params=_params(1),
    )(a, *pieces)


def _rows_dw(name, pieces, d, out_dtype):
    T, N = d.shape
    n, starts = len(pieces), _column_starts(pieces)
    M = starts[-1] + pieces[-1].shape[1]
    tk = _tile(T, 1024)
    nk = T // tk

    def kern(*refs):
        d_ref, o_ref, acc = refs[n], refs[n + 1], refs[n + 2]
        k = pl.program_id(0)

        @pl.when(k == 0)
        def _():
            acc[...] = jnp.zeros_like(acc)

        dv = d_ref[...].astype(BF16)
        for p_ref, at in zip(refs[:n], starts):
            acc[at:at + p_ref.shape[1], :] += _dot(p_ref[...], dv, TN)

        @pl.when(k == nk - 1)
        def _():
            o_ref[...] = acc[...].astype(out_dtype)

    return pl.pallas_call(
        kern, name=name, grid=(nk,),
        in_specs=[pl.BlockSpec((tk, p.shape[1]), lambda k: (k, 0)) for p in pieces] + [pl.BlockSpec((tk, N), lambda k: (k, 0))],
        out_specs=pl.BlockSpec((M, N), lambda k: (0, 0)), out_shape=jax.ShapeDtypeStruct((M, N), out_dtype),
        scratch_shapes=[pltpu.VMEM((M, N), F32)], compiler_params=_params(1),
    )(*pieces, d)


def _mix_out(pieces, w, x):
    T, D = x.shape
    w, l = w
    n, starts = len(pieces), _column_starts(pieces)
    tm = _tile(T, 512)

    def kern(*refs):
        w_ref, x_ref, o_ref = refs[n:]
        acc = x_ref[...]
        for p_ref, at in zip(refs[:n], starts):
            acc = acc + _dot(p_ref[...], w_ref[at:at + p_ref.shape[1], :], NN)
        o_ref[...] = acc

    row = lambda m: pl.BlockSpec((tm, m), lambda i: (i, 0))
    return pl.pallas_call(
        kern, name="mix_out", grid=(T // tm,),
        in_specs=[row(p.shape[1]) for p in pieces] + [_resident((None,) + w.shape[1:], lambda i: (l, 0, 0)), row(D)],
        out_specs=row(D), out_shape=jax.ShapeDtypeStruct((T, D), F32), compiler_params=_params(1),
    )(*pieces, w, x)


def _dw(name, a, d, out_dtype):
    T, M = a.shape
    N = d.shape[1]
    tk = _tile(T, 1024 if M * N <= 1024 * 1408 else 512)
    return _mm(name, TN, [a, d],
               [pl.BlockSpec((tk, M), lambda i, k: (k, 0)), pl.BlockSpec((tk, N), lambda i, k: (k, 0))],
               [jax.ShapeDtypeStruct((M, N), out_dtype)], [pl.BlockSpec((M, N), lambda i, k: (0, 0))],
               (1, T // tk), (M, N), _store())[0]


def _log_sigmoid(z):
    return jnp.minimum(z, 0.0) - jnp.log(1.0 + jnp.exp(-jnp.abs(z)))


def _decay_fwd(fl, bias):
    B, S, _ = fl.shape

    def kern(fl_ref, b_ref, o_ref):
        d = _log_sigmoid(fl_ref[...] + b_ref[...])
        row = lax.broadcasted_iota(jnp.int32, (S, LANES), 0)
        sh = 1
        while sh < S:
            d = d + jnp.where(row >= sh, pltpu.roll(d, sh, 0), 0.0)
            sh *= 2
        o_ref[...] = d.T[0:8, :]

    return pl.pallas_call(
        kern, name="decay_fwd", grid=(B,),
        in_specs=[pl.BlockSpec((None, S, LANES), lambda b: (b, 0, 0)), pl.BlockSpec((1, LANES), lambda b: (0, 0))],
        out_specs=pl.BlockSpec((None, 8, S), lambda b: (b, 0, 0)),
        out_shape=jax.ShapeDtypeStruct((B, 8, S), F32), compiler_params=_params(1),
    )(fl, bias)


def _decay_bwd(ddrow, ddcol, fl, bias, n_heads):
    B, S, _ = fl.shape

    def kern(dd_ref, ddc_ref, fl_ref, b_ref, dfl_ref, db_ref):
        dd = jnp.concatenate([dd_ref[...], jnp.zeros((LANES - 8, S), F32)], axis=0).T + ddc_ref[...]
        row = lax.broadcasted_iota(jnp.int32, (S, LANES), 0)
        lane = lax.broadcasted_iota(jnp.int32, (S, LANES), 1)
        sh = 1
        while sh < S:
            dd = dd + jnp.where(row < S - sh, pltpu.roll(dd, S - sh, 0), 0.0)
            sh *= 2
        z = fl_ref[...] + b_ref[...]
        dfl = jnp.where(lane < n_heads, dd / (1.0 + jnp.exp(z)), 0.0)
        dfl_ref[...] = dfl
        part = jnp.sum(dfl, axis=0, keepdims=True)

        @pl.when(pl.program_id(0) == 0)
        def _():
            db_ref[...] = part

        @pl.when(pl.program_id(0) > 0)
        def _():
            db_ref[...] += part

    return pl.pallas_call(
        kern, name="decay_bwd", grid=(B,),
        in_specs=[pl.BlockSpec((None, 8, S), lambda b: (b, 0, 0)), pl.BlockSpec((None, S, LANES), lambda b: (b, 0, 0)),
                  pl.BlockSpec((None, S, LANES), lambda b: (b, 0, 0)), pl.BlockSpec((1, LANES), lambda b: (0, 0))],
        out_specs=[pl.BlockSpec((None, S, LANES), lambda b: (b, 0, 0)), pl.BlockSpec((1, LANES), lambda b: (0, 0))],
        out_shape=[jax.ShapeDtypeStruct((B, S, LANES), F32), jax.ShapeDtypeStruct((1, LANES), F32)],
        compiler_params=_params(1),
    )(ddrow, ddcol, fl, bias)


def _attn_fwd(qkv, drow, n_heads, tq):
    B, S, _ = qkv.shape
    DA = n_heads * HEAD_DIM
    scale = HEAD_DIM ** -0.5

    n_pairs = n_heads // 2

    def kern(q_ref, k_ref, v_ref, dr_ref, o_ref, lse_ref):
        i = pl.program_id(1)
        lane = lax.broadcasted_iota(jnp.int32, (tq, LANES), 1)
        low = lane < HEAD_DIM
        causal = lax.broadcasted_iota(jnp.int32, (tq, tq), 1) <= lax.broadcasted_iota(jnp.int32, (tq, tq), 0)
        qms = []
        for p in range(n_pairs):
            q2 = q_ref[:, LANES * p:LANES * (p + 1)] * scale
            qms += [jnp.where(low, q2, jnp.zeros_like(q2)), jnp.where(low, jnp.zeros_like(q2), q2)]

        def step(j, carry, masked):
            ms, ls, accs = carry
            ks = pl.multiple_of(j * tq, tq)
            new_m, new_l, new_acc = [], [], []
            for p in range(n_pairs):
                cols = slice(LANES * p, LANES * (p + 1))
                k2, v2 = k_ref[pl.ds(ks, tq), cols], v_ref[pl.ds(ks, tq), cols]
                alphas, pvs = [], []
                for h in (2 * p, 2 * p + 1):
                    s = _dot(qms[h], k2, NT) - dr_ref[h, pl.ds(j, 1), :]
                    if masked:
                        s = jnp.where(causal, s, -jnp.inf)
                    m_new = jnp.maximum(ms[h], jnp.max(s, axis=1, keepdims=True))
                    alpha = jnp.exp(ms[h] - m_new)
                    pm = jnp.exp(s - m_new)
                    new_m.append(m_new)
                    new_l.append(alpha * ls[h] + jnp.sum(pm, axis=1, keepdims=True))
                    alphas.append(alpha)
                    pvs.append(_dot(pm.astype(BF16), v2, NN))
                new_acc.append(jnp.where(low, alphas[0], alphas[1]) * accs[p] + jnp.where(low, pvs[0], pvs[1]))
            return tuple(new_m), tuple(new_l), tuple(new_acc)

        init = (tuple(jnp.full((tq, 1), -jnp.inf, F32) for _ in range(n_heads)),
                tuple(jnp.zeros((tq, 1), F32) for _ in range(n_heads)),
                tuple(jnp.zeros((tq, LANES), F32) for _ in range(n_pairs)))
        ms, ls, accs = step(i, lax.fori_loop(0, i, functools.partial(step, masked=False), init), True)
        lse_mat = jnp.zeros((tq, LANES), F32)
        for p in range(n_pairs):
            l0, l1 = ls[2 * p], ls[2 * p + 1]
            o_ref[:, LANES * p:LANES * (p + 1)] = (accs[p] / jnp.where(low, l0, l1)).astype(BF16)
            lse_mat = jnp.where(lane == 2 * p, ms[2 * p] + jnp.log(l0), lse_mat)
            lse_mat = jnp.where(lane == 2 * p + 1, ms[2 * p + 1] + jnp.log(l1), lse_mat)
        lse_ref[...] = lse_mat

    nq = S // tq
    return pl.pallas_call(
        kern, name="attn_fwd", grid=(B, nq),
        in_specs=[pl.BlockSpec((None, tq, DA), lambda b, i: (b, i, 0)),
                  pl.BlockSpec((None, S, DA), lambda b, i: (b, 0, 1)),
                  pl.BlockSpec((None, S, DA), lambda b, i: (b, 0, 2)),
                  pl.BlockSpec((None, 8, nq, tq), lambda b, i: (b, 0, 0, 0))],
        out_specs=[pl.BlockSpec((None, tq, DA), lambda b, i: (b, i, 0)),
                   pl.BlockSpec((None, tq, LANES), lambda b, i: (b, i, 0))],
        out_shape=[jax.ShapeDtypeStruct((B, S, DA), BF16), jax.ShapeDtypeStruct((B, S, LANES), F32)],
        compiler_params=_params(2),
    )(qkv, qkv, qkv, drow)


def _attn_bwd(qkv, drow, o, lse, dycat, n_heads, tq):
    B, S, _ = qkv.shape
    DA = n_heads * HEAD_DIM
    scale = HEAD_DIM ** -0.5
    nq = S // tq

    n_pairs = n_heads // 2

    def kern(q_ref, k_ref, v_ref, dr_ref, o_ref, lse_ref, do_ref, dq_ref, dk_ref, dv_ref, ddr_ref, ddc_ref,
             dk_acc, dv_acc, qm_s, dom_s, delta_s, rs_s, dq_s):
        i = pl.program_id(1)

        @pl.when(i == 0)
        def _():
            dk_acc[...] = jnp.zeros_like(dk_acc)
            dv_acc[...] = jnp.zeros_like(dv_acc)
            ddr_ref[...] = jnp.zeros_like(ddr_ref)

        lane = lax.broadcasted_iota(jnp.int32, (tq, LANES), 1)
        low = lane < HEAD_DIM
        causal = lax.broadcasted_iota(jnp.int32, (tq, tq), 1) <= lax.broadcasted_iota(jnp.int32, (tq, tq), 0)
        for p in range(n_pairs):
            cols = slice(LANES * p, LANES * (p + 1))
            q2 = q_ref[:, cols] * scale
            do_f = do_ref[:, cols]
            do2 = do_f.astype(BF16)
            prod = do_f * o_ref[:, cols].astype(F32)
            qm_s[2 * p] = jnp.where(low, q2, jnp.zeros_like(q2))
            qm_s[2 * p + 1] = jnp.where(low, jnp.zeros_like(q2), q2)
            dom_s[2 * p] = jnp.where(low, do2, jnp.zeros_like(do2))
            dom_s[2 * p + 1] = jnp.where(low, jnp.zeros_like(do2), do2)
            delta_s[2 * p] = jnp.sum(jnp.where(low, prod, 0.0), axis=1, keepdims=True)
            delta_s[2 * p + 1] = jnp.sum(jnp.where(low, 0.0, prod), axis=1, keepdims=True)
            dq_s[p] = jnp.zeros((tq, LANES), F32)
        rs_s[...] = jnp.zeros(rs_s.shape, F32)

        def step(j, masked):
            ks = pl.multiple_of(j * tq, tq)
            for p in range(n_pairs):
                cols = slice(LANES * p, LANES * (p + 1))
                k2, v2 = k_ref[pl.ds(ks, tq), cols], v_ref[pl.ds(ks, tq), cols]
                dvs, dks, dqs = [], [], []
                for h in (2 * p, 2 * p + 1):
                    qm, dom = qm_s[h], dom_s[h]
                    s = _dot(qm, k2, NT) - dr_ref[h, pl.ds(j, 1), :]
                    if masked:
                        s = jnp.where(causal, s, -jnp.inf)
                    pm = jnp.exp(s - lse_ref[:, h:h + 1])
                    ds = pm * (_dot(dom, v2, NT) - delta_s[h])
                    ddr_ref[h, pl.ds(j, 1), :] -= jnp.sum(ds, axis=0, keepdims=True)
                    rs_s[h] += jnp.sum(ds, axis=1, keepdims=True)
                    dsb = ds.astype(BF16)
                    dvs.append(_dot(pm.astype(BF16), dom, TN))
                    dks.append(_dot(dsb, qm, TN))
                    dqs.append(_dot(dsb, k2, NN))
                dv_acc[pl.ds(ks, tq), cols] += dvs[0] + dvs[1]
                dk_acc[pl.ds(ks, tq), cols] += dks[0] + dks[1]
                dq_s[p] += jnp.where(low, dqs[0], dqs[1])

        def body(j, carry):
            step(j, False)
            return carry

        lax.fori_loop(0, i, body, 0)
        step(i, True)
        ddc = jnp.zeros((tq, LANES), F32)
        for p in range(n_pairs):
            dq_ref[:, LANES * p:LANES * (p + 1)] = (dq_s[p] * scale).astype(BF16)
            ddc = jnp.where(lane == 2 * p, rs_s[2 * p], ddc)
            ddc = jnp.where(lane == 2 * p + 1, rs_s[2 * p + 1], ddc)
        ddc_ref[...] = ddc

        @pl.when(i == nq - 1)
        def _():
            dk_ref[...] = dk_acc[...].astype(BF16)
            dv_ref[...] = dv_acc[...].astype(BF16)

    tile = pl.BlockSpec((None, tq, DA), lambda b, i: (b, i, 0))
    seq = pl.BlockSpec((None, S, DA), lambda b, i: (b, 0, 0))
    dec = pl.BlockSpec((None, 8, nq, tq), lambda b, i: (b, 0, 0, 0))
    return pl.pallas_call(
        kern, name="attn_bwd", grid=(B, nq),
        in_specs=[tile, pl.BlockSpec((None, S, DA), lambda b, i: (b, 0, 1)),
                  pl.BlockSpec((None, S, DA), lambda b, i: (b, 0, 2)), dec, tile,
                  pl.BlockSpec((None, tq, LANES), lambda b, i: (b, i, 0)), tile],
        out_specs=[tile, seq, seq, dec, pl.BlockSpec((None, tq, LANES), lambda b, i: (b, i, 0))],
        out_shape=[jax.ShapeDtypeStruct((B, S, DA), BF16)] * 3 + [jax.ShapeDtypeStruct((B, 8, nq, tq), F32),
                                                                  jax.ShapeDtypeStruct((B, S, LANES), F32)],
        scratch_shapes=[pltpu.VMEM((S, DA), F32), pltpu.VMEM((S, DA), F32),
                        pltpu.VMEM((n_heads, tq, LANES), BF16), pltpu.VMEM((n_heads, tq, LANES), BF16),
                        pltpu.VMEM((n_heads, tq, 1), F32), pltpu.VMEM((n_heads, tq, 1), F32),
                        pltpu.VMEM((n_pairs, tq, LANES), F32)],
        compiler_params=_params(2),
    )(qkv, qkv, qkv, drow, o, lse, dycat)


def _down(v, d, row):
    return jnp.where(row >= d, pltpu.roll(v, d, 0), 0.0)


def _up(v, d, row, S):
    return jnp.where(row < S - d, pltpu.roll(v, S - d, 0), 0.0)


def _window(v, shift, group):
    sums, acc, d = [], v, 1
    for _ in POOL_WINDOWS:
        acc = acc + shift(acc, d)
        sums.append(acc)
        d *= 2
    out = sums[-1]
    for gi in range(len(POOL_WINDOWS) - 2, -1, -1):
        out = jnp.where(group == gi, sums[gi], out)
    return out


def _pool_count(row, group):
    w = jnp.full(row.shape, POOL_WINDOWS[-1], jnp.int32)
    for gi in range(len(POOL_WINDOWS) - 2, -1, -1):
        w = jnp.where(group == gi, POOL_WINDOWS[gi], w)
    return jnp.minimum(row + 1, w).astype(F32)


def _mix_local_fwd(rest, wbd, ps, cw):
    B, S, C4 = rest.shape
    C = C4 // 4
    gw = C // len(POOL_WINDOWS)

    def kern(r_ref, w_ref, ps_ref, cw_ref, y_ref, pooled_ref):
        row = lax.broadcasted_iota(jnp.int32, (S, C), 0)
        group = lax.broadcasted_iota(jnp.int32, (S, C), 1) // gw
        u = r_ref[:, 0:C]
        pooled = _window(u, lambda v, d: _down(v, d, row), group) / _pool_count(row, group) - u
        pb = pooled.astype(BF16)
        pooled_ref[...] = pb
        y_ref[:, 0:C] = (_dot(pb, w_ref[...], NN) * ps_ref[...]).astype(BF16)
        uc = r_ref[:, 2 * C:3 * C] * r_ref[:, 3 * C:4 * C]
        y = cw_ref[0:1, :] * _down(uc, 2, row) + cw_ref[1:2, :] * _down(uc, 1, row) + cw_ref[2:3, :] * uc
        y_ref[:, C:2 * C] = (r_ref[:, C:2 * C] * y).astype(BF16)

    return pl.pallas_call(
        kern, name="mix_local_fwd", grid=(B,),
        in_specs=[pl.BlockSpec((None, S, C4), lambda b: (b, 0, 0)), pl.BlockSpec((C, C), lambda b: (0, 0)),
                  pl.BlockSpec((1, C), lambda b: (0, 0)), pl.BlockSpec((8, C), lambda b: (0, 0))],
        out_specs=[pl.BlockSpec((None, S, 2 * C), lambda b: (b, 0, 0)), pl.BlockSpec((None, S, C), lambda b: (b, 0, 0))],
        out_shape=[jax.ShapeDtypeStruct((B, S, 2 * C), BF16), jax.ShapeDtypeStruct((B, S, C), BF16)],
        compiler_params=_params(1),
    )(rest, wbd, ps, cw)


def _mix_local_bwd(rest, pooled, dycat, wbd, ps, cw):
    B, S, C4 = rest.shape
    C = C4 // 4
    gw = C // len(POOL_WINDOWS)

    def kern(r_ref, pooled_ref, d_ref, w_ref, ps_ref, cw_ref, dr_ref, dw_ref, dps_ref, dcw_ref):
        row = lax.broadcasted_iota(jnp.int32, (S, C), 0)
        group = lax.broadcasted_iota(jnp.int32, (S, C), 1) // gw
        dyp = d_ref[:, 0:C]
        dyc = d_ref[:, C:2 * C]
        pb = pooled_ref[...]
        dps = jnp.sum(dyp * _dot(pb, w_ref[...], NN), axis=0, keepdims=True)
        dzb = (dyp * ps_ref[...]).astype(BF16)
        dw = _dot(pb, dzb, TN)
        dpooled = _dot(dzb, w_ref[...], NT)
        g = dpooled / _pool_count(row, group)
        dr_ref[:, 0:C] = (_window(g, lambda v, d: _up(v, d, row, S), group) - dpooled).astype(BF16)
        cc, ch = r_ref[:, 2 * C:3 * C], r_ref[:, 3 * C:4 * C]
        uc = cc * ch
        u1, u2 = _down(uc, 1, row), _down(uc, 2, row)
        y = cw_ref[0:1, :] * u2 + cw_ref[1:2, :] * u1 + cw_ref[2:3, :] * uc
        dr_ref[:, C:2 * C] = (dyc * y).astype(BF16)
        dy = dyc * r_ref[:, C:2 * C]
        duc = cw_ref[0:1, :] * _up(dy, 2, row, S) + cw_ref[1:2, :] * _up(dy, 1, row, S) + cw_ref[2:3, :] * dy
        dr_ref[:, 2 * C:3 * C] = (duc * ch).astype(BF16)
        dr_ref[:, 3 * C:4 * C] = (duc * cc).astype(BF16)
        dcw = jnp.concatenate([jnp.sum(dy * u2, axis=0, keepdims=True), jnp.sum(dy * u1, axis=0, keepdims=True),
                               jnp.sum(dy * uc, axis=0, keepdims=True), jnp.zeros((5, C), F32)], axis=0)

        @pl.when(pl.program_id(0) == 0)
        def _():
            dw_ref[...] = dw
            dps_ref[...] = dps
            dcw_ref[...] = dcw

        @pl.when(pl.program_id(0) > 0)
        def _():
            dw_ref[...] += dw
            dps_ref[...] += dps
            dcw_ref[...] += dcw

    full = lambda shape: pl.BlockSpec(shape, lambda b: (0, 0))
    return pl.pallas_call(
        kern, name="mix_local_bwd", grid=(B,),
        in_specs=[pl.BlockSpec((None, S, C4), lambda b: (b, 0, 0)), pl.BlockSpec((None, S, C), lambda b: (b, 0, 0)),
                  pl.BlockSpec((None, S, 2 * C), lambda b: (b, 0, 1)), full((C, C)), full((1, C)), full((8, C))],
        out_specs=[pl.BlockSpec((None, S, C4), lambda b: (b, 0, 0)), full((C, C)), full((1, C)), full((8, C))],
        out_shape=[jax.ShapeDtypeStruct((B, S, C4), BF16), jax.ShapeDtypeStruct((C, C), F32),
                   jax.ShapeDtypeStruct((1, C), F32), jax.ShapeDtypeStruct((8, C), F32)],
        compiler_params=_params(1),
    )(rest, pooled, dycat, wbd, ps, cw)


def _adamw(w, gs, m, v):
    R, C = w.shape
    pieces = [p if isinstance(p, tuple) else (p,) for p in gs]
    owner = [s for s, p in enumerate(pieces) for _ in p]
    flat = [a for p in pieces for a in p]
    n = len(flat)
    rows = R // len(pieces)
    tr = _tile(rows, 256)
    per = rows // tr

    def kern(w_ref, *refs):
        g_refs, (m_ref, v_ref, g_out, d_ref, nm_ref, nv_ref) = refs[:n], refs[n:]
        vals, at = [], 0
        for p in pieces:
            vals.append(g_refs[at][...] if len(p) == 1 else g_refs[at][...] + g_refs[at + 1][...])
            at += len(p)
        gv = vals[0]
        for s in range(1, len(pieces)):
            gv = jnp.where(pl.program_id(0) // per == s, vals[s], gv)
        nm = ADAM_B1 * m_ref[...] + (1.0 - ADAM_B1) * gv
        nv = ADAM_B2 * v_ref[...] + (1.0 - ADAM_B2) * (gv * gv)
        m_hat = nm / (1.0 - ADAM_B1 ** ADAM_STEP)
        v_hat = nv / (1.0 - ADAM_B2 ** ADAM_STEP)
        g_out[...] = gv
        d_ref[...] = -ADAM_LR * (m_hat / (jnp.sqrt(v_hat) + ADAM_EPS) + ADAM_WD * w_ref[...])
        nm_ref[...] = nm
        nv_ref[...] = nv

    def piece(s):
        return pl.BlockSpec((tr, C), lambda i: (jnp.clip(i - s * per, 0, per - 1), 0))

    blk = pl.BlockSpec((tr, C), lambda i: (i, 0))
    return pl.pallas_call(
        kern, name="adamw", grid=(R // tr,), in_specs=[blk] + [piece(s) for s in owner] + [blk] * 2,
        out_specs=[blk] * 4, out_shape=[jax.ShapeDtypeStruct((R, C), F32)] * 4, compiler_params=_params(1),
    )(w, *flat, m, v)


def _place():
    x, y, c = lax.axis_index("x"), lax.axis_index("y"), lax.axis_index("c")
    return x, y, c, [(1 - x, y), (x, 1 - y), (1 - x, 1 - y)]


def _comm_call(name, body, operands, out_shape, n_sems, aliases=None):
    any_spec = pl.BlockSpec(memory_space=pl.ANY)
    return pl.pallas_call(
        body, name=name, in_specs=[any_spec] * len(operands), out_specs=[any_spec] * len(out_shape),
        out_shape=out_shape, input_output_aliases=aliases or {},
        scratch_shapes=[pltpu.SemaphoreType.DMA((n,)) for n in n_sems],
    )(*operands)


def _my_block():
    return 2 * lax.axis_index("x") + lax.axis_index("y")


def _place_shard(w, dtype, first=0, count=None):
    L, R, C = w.shape
    count = L if count is None else count
    tr = _tile(R, 512)

    def kern(w_ref, o_ref):
        o_ref[...] = w_ref[...].astype(dtype)

    return pl.pallas_call(
        kern, name="place_shard", grid=(count, R // tr),
        in_specs=[pl.BlockSpec((None, tr, C), lambda l, i: (first + l, i, 0))],
        out_specs=pl.BlockSpec((None, None, tr, C), lambda l, i: (l, _my_block(), i, 0)),
        out_shape=jax.ShapeDtypeStruct((count, N_CHIPS, R, C), dtype), compiler_params=_params(2),
    )(w)


HALF_ROWS = 16


def _rows(ref, half):
    hr = ref.shape[-2] // 2
    return ref.at[(slice(None),) * (len(ref.shape) - 2) + (pl.ds(half * hr, hr),)]


def _all_gather(bufs):
    n = len(bufs)

    def body(*refs):
        outs = refs[n:2 * n]
        send_sems, recv_sems = refs[2 * n:]
        x, y, c, chips = _place()
        sibling = (x, y, 1 - c)

        def remote(k, j, chip, half, to):
            blk = 2 * chip[0] + chip[1]
            if outs[k].shape[2] % (2 * HALF_ROWS) == 0:
                region = _rows(outs[k].at[:, blk], half)
            else:
                hl = outs[k].shape[0] // 2
                region = outs[k].at[pl.ds(half * hl, hl), blk]
            return pltpu.make_async_remote_copy(
                src_ref=region, dst_ref=region, send_sem=send_sems.at[6 * k + j],
                recv_sem=recv_sems.at[6 * k + j], device_id=to, device_id_type=MESH)

        first = [remote(k, j, (x, y), c, (*chip, c)) for k in range(n) for j, chip in enumerate(chips)]
        for cp in first:
            cp.start()
        passed = []
        for k in range(n):
            for j, chip in enumerate(chips):
                remote(k, j, chip, c, (x, y, c)).wait_recv()
                passed.append(remote(k, 3 + j, chip, c, sibling))
                passed[-1].start()
        for k in range(n):
            for j, chip in enumerate(chips):
                remote(k, 3 + j, chip, 1 - c, (x, y, c)).wait_recv()
        for cp in first + passed:
            cp.wait_send()

    out_shape = [jax.ShapeDtypeStruct(s.shape, s.dtype) for s in bufs]
    return _comm_call("all_gather_weights", body, bufs, out_shape, (6 * n, 6 * n), aliases={k: k for k in range(n)})


_HBM = pl.BlockSpec(memory_space=pltpu.HBM)
_SEM = pl.BlockSpec(memory_space=pltpu.SEMAPHORE)
_ANY = pl.BlockSpec(memory_space=pl.ANY)


def _split_start(name, bufs, n_copies, make_copies, after):
    n = len(bufs)

    def body(*refs):
        send_sems, recv_sems, token = refs[n + 1], refs[n + 2], refs[2 * n + 3]
        for cp in make_copies(refs[:n], send_sems, recv_sems):
            cp.start()
        token[...] = jnp.zeros_like(token)

    res = pl.pallas_call(
        body, name=name, in_specs=[_HBM] * n + [_ANY],
        out_shape=(pltpu.SemaphoreType.DMA((n_copies,)), pltpu.SemaphoreType.DMA((n_copies,)),
                   *[pltpu.HBM(b.shape, b.dtype) for b in bufs], jax.ShapeDtypeStruct((8, LANES), F32)),
        out_specs=(_SEM, _SEM, *[_HBM] * n, pl.BlockSpec(memory_space=pltpu.VMEM)),
        input_output_aliases={i: 2 + i for i in range(n)},
        compiler_params=pltpu.CompilerParams(has_side_effects=pltpu.SideEffectType.DATAFLOW_SIDE_EFFECTING),
    )(*[pltpu.with_memory_space_constraint(b, pltpu.HBM) for b in bufs], after)
    return res[0], res[1], list(res[2:2 + n]), res[2 + n]


def _split_wait(name, send_sems, recv_sems, bufs, make_copies, after):
    n = len(bufs)

    def body(*refs):
        for cp in make_copies(refs[:n], refs[n], refs[n + 1]):
            cp.wait_send()
            cp.wait_recv()

    return list(pl.pallas_call(
        body, name=name, in_specs=[_HBM] * n + [_SEM, _SEM, _ANY],
        out_shape=tuple(pltpu.HBM(b.shape, b.dtype) for b in bufs), out_specs=tuple([_HBM] * n),
        input_output_aliases={i: i for i in range(n)},
        compiler_params=pltpu.CompilerParams(has_side_effects=pltpu.SideEffectType.DATAFLOW_SIDE_EFFECTING),
    )(*bufs, send_sems, recv_sems, after))


def _gather_copies(refs, send_sems, recv_sems):
    x, y, c, chips = _place()
    return [pltpu.make_async_remote_copy(
        src_ref=ref.at[:, 2 * x + y], dst_ref=ref.at[:, 2 * x + y], send_sem=send_sems.at[3 * k + j],
        recv_sem=recv_sems.at[3 * k + j], device_id=(*chip, c), device_id_type=MESH)
        for k, ref in enumerate(refs) for j, chip in enumerate(chips)]


def _exchange_copies(refs, send_sems, recv_sems):
    n = len(refs) // 2
    x, y, c, chips = _place()
    return [pltpu.make_async_remote_copy(
        src_ref=refs[k].at[:, 2 * chip[0] + chip[1]], dst_ref=refs[n + k].at[j], send_sem=send_sems.at[3 * k + j],
        recv_sem=recv_sems.at[3 * k + j], device_id=(*chip, c), device_id_type=MESH)
        for k in range(n) for j, chip in enumerate(chips)]


def _rs_swap_halves(grads):
    n = len(grads)

    def body(*refs):
        ins, outs = refs[:n], refs[n:2 * n]
        send_sems, recv_sems = refs[2 * n:]
        x, y, c, _ = _place()
        copies = [pltpu.make_async_remote_copy(
            src_ref=_rows(ins[k], 1 - c), dst_ref=outs[k], send_sem=send_sems.at[k],
            recv_sem=recv_sems.at[k], device_id=(x, y, 1 - c), device_id_type=MESH) for k in range(n)]
        for cp in copies:
            cp.start()
        for cp in copies:
            cp.wait()

    out_shape = [jax.ShapeDtypeStruct(g.shape[:2] + (g.shape[2] // 2, g.shape[3]), g.dtype) for g in grads]
    return _comm_call("rs_swap_halves", body, grads, out_shape, (n, n))


def _rs_exchange(parts):
    n = len(parts)

    def body(*refs):
        ins, outs = refs[:n], refs[n:2 * n]
        send_sems, recv_sems = refs[2 * n:]
        x, y, c, chips = _place()
        copies = [pltpu.make_async_remote_copy(
            src_ref=ins[k].at[:, 2 * chip[0] + chip[1]], dst_ref=outs[k].at[j], send_sem=send_sems.at[3 * k + j],
            recv_sem=recv_sems.at[3 * k + j], device_id=(*chip, c), device_id_type=MESH)
            for k in range(n) for j, chip in enumerate(chips)]
        for cp in copies:
            cp.start()
        for cp in copies:
            cp.wait()

    out_shape = [jax.ShapeDtypeStruct((3, p.shape[0]) + p.shape[2:], p.dtype) for p in parts]
    return _comm_call("rs_exchange", body, parts, out_shape, (3 * n, 3 * n))


def _rs_share(bufs):
    n = len(bufs)

    def body(*refs):
        outs = refs[n:2 * n]
        send_sems, recv_sems = refs[2 * n:]
        x, y, c, _ = _place()

        def half(k, which):
            region = _rows(outs[k], which)
            return pltpu.make_async_remote_copy(
                src_ref=region, dst_ref=region, send_sem=send_sems.at[k], recv_sem=recv_sems.at[k],
                device_id=(x, y, 1 - c), device_id_type=MESH)

        sends = [half(k, c) for k in range(n)]
        for cp in sends:
            cp.start()
        for k in range(n):
            half(k, 1 - c).wait_recv()
        for cp in sends:
            cp.wait_send()

    out_shape = [jax.ShapeDtypeStruct(h.shape, h.dtype) for h in bufs]
    return _comm_call("rs_share", body, bufs, out_shape, (n, n), aliases={k: k for k in range(n)})


def _all_reduce_small(v):
    n = v.shape[0]

    def body(v_ref, o_ref, gbuf, send_sems, recv_sems):
        x, y, c, _ = _place()
        me = 4 * x + 2 * y + c
        gbuf[me] = v_ref[...]
        copies, waits = [], []
        for r in range(1, N_DEV):
            px = 1 - x if r & 4 else x
            py = 1 - y if r & 2 else y
            pc = 1 - c if r & 1 else c
            mk = functools.partial(pltpu.make_async_remote_copy, src_ref=v_ref, send_sem=send_sems.at[r - 1],
                                   recv_sem=recv_sems.at[r - 1], device_id=(px, py, pc), device_id_type=MESH)
            copies.append(mk(dst_ref=gbuf.at[me]))
            waits.append(mk(dst_ref=gbuf.at[4 * px + 2 * py + pc]))
        for cp in copies:
            cp.start()
        for cp in waits:
            cp.wait_recv()
        for cp in copies:
            cp.wait_send()
        acc = gbuf[0]
        for d in range(1, N_DEV):
            acc = acc + gbuf[d]
        o_ref[...] = acc

    vm = pl.BlockSpec(memory_space=pltpu.VMEM)
    return pl.pallas_call(
        body, name="all_reduce_small", in_specs=[vm], out_specs=vm, out_shape=jax.ShapeDtypeStruct(v.shape, F32),
        scratch_shapes=[pltpu.VMEM((N_DEV, n, LANES), F32), pltpu.SemaphoreType.DMA((N_DEV - 1,)),
                        pltpu.SemaphoreType.DMA((N_DEV - 1,))],
        compiler_params=pltpu.CompilerParams(vmem_limit_bytes=VMEM_LIMIT),
    )(v)


def _add_half(g, h1):
    L, nb, hr, C = h1.shape
    g3, h3 = g.reshape(L * nb, 2 * hr, C), h1.reshape(L * nb, hr, C)
    tr = _tile(hr, 512)

    def kern(g_ref, h_ref, o_ref):
        o_ref[...] = (g_ref[...].astype(F32) + h_ref[...].astype(F32)).astype(BF16)

    blk = pl.BlockSpec((None, tr, C), lambda l, i: (l, i, 0))
    out = pl.pallas_call(
        kern, name="rs_add_half", grid=(L * nb, hr // tr),
        in_specs=[pl.BlockSpec((None, tr, C), lambda l, i: (l, lax.axis_index("c") * (hr // tr) + i, 0)), blk],
        out_specs=blk, out_shape=jax.ShapeDtypeStruct(h3.shape, BF16), compiler_params=_params(2),
    )(g3, h3)
    return out.reshape(h1.shape)


def _add_blocks(p, h2, half=True):
    L, nb, hr, C = p.shape
    tr = _tile(hr, 512)
    shift = lambda: lax.axis_index("c") * (hr // tr) if half else 0

    def kern(p_ref, h0_ref, h1_ref, h2_ref, o_ref):
        o_ref[...] = ((p_ref[...].astype(F32) + h0_ref[...].astype(F32)) + h1_ref[...].astype(F32)) + h2_ref[...].astype(F32)

    def other(j):
        return pl.BlockSpec((None, None, tr, C), lambda l, i: (j, l, i, 0))

    return pl.pallas_call(
        kern, name="rs_add_blocks", grid=(L, hr // tr),
        in_specs=[pl.BlockSpec((None, None, tr, C), lambda l, i: (l, _my_block(), i, 0)), other(0), other(1), other(2)],
        out_specs=pl.BlockSpec((None, tr, C), lambda l, i: (l, shift() + i, 0)),
        out_shape=jax.ShapeDtypeStruct((L, (2 if half else 1) * hr, C), F32), compiler_params=_params(2),
    )(p, h2, h2, h2)


def _d2d_swap(arrays):
    n = len(arrays)

    def body(*refs):
        ins, outs = refs[:n], refs[n:2 * n]
        send_sems, recv_sems = refs[2 * n:]
        x, y, c, _ = _place()
        copies = [pltpu.make_async_remote_copy(
            src_ref=ins[k], dst_ref=outs[k], send_sem=send_sems.at[k], recv_sem=recv_sems.at[k],
            device_id=(x, y, 1 - c), device_id_type=MESH) for k in range(n)]
        for cp in copies:
            cp.start()
        for cp in copies:
            cp.wait()

    return _comm_call("d2d_swap", body, arrays, [jax.ShapeDtypeStruct(a.shape, a.dtype) for a in arrays], (n, n))


def _reduce_scatter(grads):
    sib = _rs_swap_halves(grads)
    parts = [_add_half(g, h) for g, h in zip(grads, sib)]
    others = _rs_exchange(parts)
    return _rs_share([_add_blocks(p, o) for p, o in zip(parts, others)])


WEIGHTS = ("norm_ffn1", "w_ffn1_in", "w_ffn1_out", "norm_mix", "w_mix_in", "b_forget", "w_pool", "pool_scale",
           "conv_w", "w_mix_out", "norm_ffn2", "w_ffn2_in", "w_ffn2_out", "norm_final")
BIG = ("w_ffn1_in", "w_ffn1_out", "w_mix_in", "w_mix_out", "w_ffn2_in", "w_ffn2_out")
SMALL = ("norm_ffn1", "norm_mix", "b_forget", "w_pool", "pool_scale", "conv_w", "norm_ffn2", "norm_final")


def _prep_weights(small, gathered, conv_w, D, first):
    DA, C, H = D // 2, D // 4, D // 2 // HEAD_DIM
    L = gathered["w_mix_in"].shape[0]
    small = {k: val[first:first + L] for k, val in small.items() if k != "norm_final"}
    gathered = dict(gathered, conv_w=conv_w[first:first + L])
    w_in = jnp.concatenate([gathered["w_mix_in"][:, b] for b in range(N_CHIPS)], axis=2)
    wqkv, wrest = w_in[:, :, :3 * DA], w_in[:, :, 3 * DA + H:]
    wf = jnp.pad(w_in[:, :, 3 * DA:3 * DA + H], ((0, 0), (0, 0), (0, LANES - H)))
    ng = len(POOL_WINDOWS)
    same_group = jnp.eye(ng, dtype=bool)[None, :, None, :, None]
    wbd = jnp.where(same_group, small["w_pool"][:, :, :, None, :], 0.0).reshape(L, C, C)
    cw = jnp.concatenate([gathered["conv_w"][:, b] for b in range(N_CHIPS)], axis=2)
    return dict(
        g1=small["norm_ffn1"], gm=small["norm_mix"], g2=small["norm_ffn2"],
        w1in=gathered["w_ffn1_in"], w1out=gathered["w_ffn1_out"].reshape(L, -1, D),
        wp=jnp.concatenate([wqkv, wrest, wf], axis=2), wmixout=gathered["w_mix_out"].reshape(L, D, D),
        bias=jnp.pad(small["b_forget"], ((0, 0), (0, LANES - H))), wbd=wbd.astype(BF16), ps=small["pool_scale"],
        cw=jnp.pad(cw, ((0, 0), (0, 8 - CONV_WIDTH), (0, 0))),
    )


def _layer_params(l, W):
    P = {k: (W[k], l) for k in ("w1in", "w1out", "wp", "wmixout")}
    P.update({k: W[k][l][None] for k in ("g1", "gm", "g2", "bias", "ps")})
    P.update(wbd=W["wbd"][l], cw=W["cw"][l])
    return P


def _ffn_fwd(x, g, w_in, w_out, token=None):
    h, jac, act = _ffn_up(x, g, w_in, token)
    return _ffn_out(act, w_out, x)[0], (x, h, jac, act)


def _ffn_bwd(dres, saved, g, w_in, w_out, token=None):
    x, h, jac, act = saved
    dgu, dx, dg = _ffn_bwd_main(dres, jac, x, g, w_out, w_in, token)
    dw_out = _ffn_dw_out(act, dres)[0]
    dw_in = _ffn_dw_in(h, dgu)[0]
    return dx, dg, dw_in, dw_out.reshape(N_CHIPS, -1, dw_out.shape[1])


def _mixer_fwd(x, P, B, S, tq):
    T, D = x.shape
    DA, C, H = D // 2, D // 4, D // 2 // HEAD_DIM
    hn, qkv, rest, fl = _mix_up(x, P["gm"], P["wp"], (3 * DA, 4 * C))
    qkv, rest, fl = qkv.reshape(B, S, 3 * DA), rest.reshape(B, S, 4 * C), fl.reshape(B, S, LANES)
    drow = _decay_fwd(fl, P["bias"]).reshape(B, 8, S // tq, tq)
    o, lse = _attn_fwd(qkv, drow, H, tq)
    ypc, pooled = _mix_local_fwd(rest, P["wbd"], P["ps"], P["cw"])
    x_out = _mix_out([o.reshape(T, DA), ypc.reshape(T, 2 * C)], P["wmixout"], x)
    return x_out, (x, hn, qkv, rest, fl, drow, o, lse, pooled, ypc)


def _mixer_bwd(dres, saved, P, B, S, tq):
    x, hn, qkv, rest, fl, drow, o, lse, pooled, ypc = saved
    T, D = x.shape
    DA, C, H = D // 2, D // 4, D // 2 // HEAD_DIM
    dycat = _proj("mix_out_bwd", dres, P["wmixout"], F32, NT).reshape(B, S, D)
    dw_out = _rows_dw("mix_out_dw", [o.reshape(T, DA), ypc.reshape(T, 2 * C)], dres, BF16)
    dq, dk, dv, ddrow, ddcol = _attn_bwd(qkv, drow, o, lse, dycat, H, tq)
    dfl, dbias = _decay_bwd(ddrow.reshape(B, 8, S), ddcol, fl, P["bias"], H)
    drest, dwbd, dps, dcw = _mix_local_bwd(rest, pooled, dycat, P["wbd"], P["ps"], P["cw"])
    pieces = [a.reshape(T, a.shape[-1]) for a in (dq, dk, dv, drest, dfl)]
    dwp = _pieces_dw("mix_in_dw", hn, pieces, F32, 512)
    dx, dg = _mix_in_bwd(pieces, x, P["gm"], dres, P["wp"])
    n_q, n_r = 3 * DA, 4 * C
    dw_in = jnp.concatenate([dwp[:, :n_q], dwp[:, n_q + n_r:n_q + n_r + H], dwp[:, n_q:n_q + n_r]], axis=1)
    dw_in = dw_in.reshape(D, N_CHIPS, -1).transpose(1, 0, 2).astype(BF16)
    ng = len(POOL_WINDOWS)
    same_group = jnp.eye(ng, dtype=bool)[:, None, :, None]
    dw_pool = jnp.where(same_group, dwbd.reshape(ng, C // ng, ng, C // ng), 0.0).sum(axis=2)
    small = dict(norm_mix=dg[0], b_forget=dbias[0, :H], w_pool=dw_pool, pool_scale=dps[0], conv_w=dcw[:CONV_WIDTH])
    return dx, small, dw_in, dw_out.reshape(N_CHIPS, -1, D)


EARLY = ("w_ffn1_in", "w_ffn1_out", "w_mix_in", "w_mix_out")
FFN2 = ("w_ffn2_in", "w_ffn2_out")


def _local_step(x, target, small, conv_w, pipe):
    B, S, D = x.shape
    L = small["norm_ffn1"].shape[0]
    tq = _tile(S, 256)
    xt = x.reshape(B * S, D)
    saved, params = [], []
    for l in range(L):
        P = _layer_params(0, _prep_weights(small, pipe.weights(l, xt), conv_w, D, l))
        xt, s1 = _ffn_fwd(xt, P["g1"], P["w1in"], P["w1out"], pipe.token(l))
        xt, s2 = _mixer_fwd(xt, P, B, S, tq)
        w2 = pipe.weights_ffn2(l, xt)
        P.update(w2in=(w2["w_ffn2_in"], 0), w2out=(w2["w_ffn2_out"].reshape(1, -1, D), 0))
        xt, s3 = _ffn_fwd(xt, P["g2"], P["w2in"], P["w2out"])
        saved.append((s1, s2, s3))
        params.append(P)
    dres, dgf, loss = _final_loss(xt, small["norm_final"][None], target.reshape(B * S, D))
    sm = {k: [None] * L for k in SMALL if k != "norm_final"}
    token = None
    for l in reversed(range(L)):
        P, (s1, s2, s3) = params[l], saved[l]
        big = {}
        dres, dg2, big["w_ffn2_in"], big["w_ffn2_out"] = _ffn_bwd(dres, s3, P["g2"], P["w2in"], P["w2out"], token)
        dres, smix, big["w_mix_in"], big["w_mix_out"] = _mixer_bwd(dres, s2, P, B, S, tq)
        big = {k: val[None] for k, val in big.items()}
        token = pipe.grads(l, FFN2 + EARLY[2:], big, dres) if l == 0 else None
        dres, dg1, dw_in, dw_out = _ffn_bwd(dres, s1, P["g1"], P["w1in"], P["w1out"], token)
        big.update(w_ffn1_in=dw_in[None], w_ffn1_out=dw_out[None])
        sm["norm_ffn1"][l], sm["norm_ffn2"][l] = dg1[0], dg2[0]
        for k, val in smix.items():
            sm[k][l] = val
        token = pipe.grads(l, EARLY[:2] if l == 0 else BIG, big, big["w_ffn1_in"])
    sm = {k: jnp.stack(val) for k, val in sm.items()}
    sm["norm_final"] = dgf[0]
    return loss[0, 0], dres.reshape(B, S, D), sm


def _sibling_copies(refs, send_sems, recv_sems):
    n = len(refs) // 2
    x, y, c, _ = _place()
    return [pltpu.make_async_remote_copy(
        src_ref=refs[k], dst_ref=refs[n + k], send_sem=send_sems.at[k], recv_sem=recv_sems.at[k],
        device_id=(x, y, 1 - c), device_id_type=MESH) for k in range(n)]


class _Pipeline:
    def __init__(self, w):
        self.w, self.n_layers = w, w[BIG[0]].shape[0]
        first = _all_gather([_place_shard(w[k], BF16, 0, 1) for k in EARLY] + [_place_shard(w["conv_w"], F32)])
        self.conv_w = first[-1]
        self._ready = dict(zip(EARLY, first[:-1]))
        self._ffn2 = self._start_gather("0b", FFN2, 0, first[0])
        self._next = (1, self._start_gather("1", BIG, 1, self._ffn2[1][3]))
        self._reduce, self._swaps = None, []
        self.reduced = [dict() for _ in range(self.n_layers)]

    def _start_gather(self, tag, kinds, l, after):
        placed = [_place_shard(self.w[k], BF16, l, 1) for k in kinds]
        return kinds, _split_start(f"gather_start_{tag}", placed, 3 * len(kinds), _gather_copies, after)

    def _wait_gather(self, tag, started, after):
        kinds, (send_sems, recv_sems, bufs, _) = started
        return dict(zip(kinds, _split_wait(f"gather_wait_{tag}", send_sems, recv_sems, bufs, _gather_copies, after)))

    def token(self, l):
        return self._next[1][1][3] if self._next is not None and self._next[0] == l + 1 else None

    def weights(self, l, after):
        if l == 0:
            return self._ready
        self._layer = self._wait_gather(str(l), self._next[1], after)
        first = next(iter(self._layer.values()))
        self._next = (l + 1, self._start_gather(str(l + 1), BIG, l + 1, first)) if l + 1 < self.n_layers else None
        return self._layer

    def weights_ffn2(self, l, after):
        return self._wait_gather("0b", self._ffn2, after) if l == 0 else self._layer

    def _finish_reduce(self, after):
        if self._reduce is None:
            return None
        tag, l, kinds, (send_sems, recv_sems, bufs, _) = self._reduce
        n = len(kinds)
        bufs = _split_wait(f"reduce_wait_{tag}", send_sems, recv_sems, bufs, _exchange_copies, after)
        mine = [_add_blocks(p, o, half=False) for p, o in zip(bufs[:n], bufs[n:])]
        lands = [lax.empty(q.shape, q.dtype) for q in mine]
        self._swaps.append((tag, l, kinds, _split_start(f"swap_start_{tag}", mine + lands, n, _sibling_copies, mine[0])))
        self._reduce = None
        return self._swaps[-1][3][3]

    def grads(self, l, kinds, big, after):
        swap_token = self._finish_reduce(after)
        grads = [big[k] for k in kinds]
        if l == 0 and kinds[0] == EARLY[0]:
            self.reduced[0].update(zip(kinds, _reduce_scatter(grads)))
            return None
        tag = f"{l}{'b' if len(kinds) < len(BIG) else ''}"
        lands = [lax.empty((3, g.shape[0]) + g.shape[2:], g.dtype) for g in grads]
        started = _split_start(f"reduce_start_{tag}", grads + lands, 3 * len(kinds), _exchange_copies,
                               grads[0] if swap_token is None else swap_token)
        self._reduce = (tag, l, kinds, started)
        return started[3]

    def finish(self, after):
        self._finish_reduce(after)
        for tag, l, kinds, (send_sems, recv_sems, bufs, _) in self._swaps:
            n = len(kinds)
            bufs = _split_wait(f"swap_wait_{tag}", send_sems, recv_sems, bufs, _sibling_copies, after)
            self.reduced[l].update(zip(kinds, zip(bufs[:n], bufs[n:])))
        return self.reduced


def _pack(parts, extra=()):
    flat = jnp.concatenate([p.reshape(-1) for p in parts] + [jnp.reshape(e, (1,)) for e in extra])
    n = -(-flat.shape[0] // (8 * LANES)) * 8
    return jnp.pad(flat, (0, n * LANES - flat.shape[0])).reshape(n, LANES)


def _unpack(buf, shapes):
    flat, out, at = buf.reshape(-1), [], 0
    for s in shapes:
        n = math.prod(s)
        out.append(flat[at:at + n].reshape(s))
        at += n
    return out, flat[at:]


def kernel(x, norm_ffn1, w_ffn1_in, w_ffn1_out, norm_mix, w_mix_in, b_forget, w_pool, pool_scale, conv_w, w_mix_out, norm_ffn2, w_ffn2_in, w_ffn2_out, norm_final, loss_target, m_norm_ffn1, m_w_ffn1_in, m_w_ffn1_out, m_norm_mix, m_w_mix_in, m_b_forget, m_w_pool, m_pool_scale, m_conv_w, m_w_mix_out, m_norm_ffn2, m_w_ffn2_in, m_w_ffn2_out, m_norm_final, v_norm_ffn1, v_w_ffn1_in, v_w_ffn1_out, v_norm_mix, v_w_mix_in, v_b_forget, v_w_pool, v_pool_scale, v_conv_w, v_w_mix_out, v_norm_ffn2, v_w_ffn2_in, v_w_ffn2_out, v_norm_final):
    w = dict(zip(WEIGHTS, (norm_ffn1, w_ffn1_in, w_ffn1_out, norm_mix, w_mix_in, b_forget, w_pool, pool_scale, conv_w, w_mix_out, norm_ffn2, w_ffn2_in, w_ffn2_out, norm_final)))
    m = dict(zip(WEIGHTS, (m_norm_ffn1, m_w_ffn1_in, m_w_ffn1_out, m_norm_mix, m_w_mix_in, m_b_forget, m_w_pool, m_pool_scale, m_conv_w, m_w_mix_out, m_norm_ffn2, m_w_ffn2_in, m_w_ffn2_out, m_norm_final)))
    v = dict(zip(WEIGHTS, (v_norm_ffn1, v_w_ffn1_in, v_w_ffn1_out, v_norm_mix, v_w_mix_in, v_b_forget, v_w_pool, v_pool_scale, v_conv_w, v_w_mix_out, v_norm_ffn2, v_w_ffn2_in, v_w_ffn2_out, v_norm_final)))
    block = 2 * lax.axis_index("x") + lax.axis_index("y")

    pipe = _Pipeline(w)
    small = {k: w[k] for k in SMALL}
    loss, grad_x, sm = _local_step(x, loss_target, small, pipe.conv_w, pipe)
    reduced = pipe.finish(grad_x)
    grads, order = {}, list(SMALL)
    total = _all_reduce_small(_pack([sm[k] for k in order], extra=(loss,)))
    parts, rest = _unpack(total, [sm[k].shape for k in order])
    grads.update(zip(order, parts))
    loss = rest[0]
    cs = conv_w.shape[2]
    grads["conv_w"] = lax.dynamic_slice_in_dim(grads["conv_w"], block * cs, cs, axis=2)

    delta, new_m, new_v = {}, {}, {}
    for k in BIG:
        two_d = lambda a: a.reshape(-1, a.shape[-1])
        pieces = [tuple(map(two_d, g)) if isinstance(g, tuple) else two_d(g) for g in (layer[k] for layer in reduced)]
        res = _adamw(two_d(w[k]), pieces, two_d(m[k]), two_d(v[k]))
        grads[k], delta[k], new_m[k], new_v[k] = [r.reshape(w[k].shape) for r in res]
    packed = [_pack([t[k] for k in order]) for t in (w, grads, m, v)]
    _, d, nm, nv = _adamw(packed[0], [packed[1]], packed[2], packed[3])
    shapes = [w[k].shape for k in order]
    for res, flat in ((delta, d), (new_m, nm), (new_v, nv)):
        res.update(zip(order, _unpack(flat, shapes)[0]))
    return (loss, grad_x, *[grads[k] for k in WEIGHTS], *[delta[k] for k in WEIGHTS],
            *[new_m[k] for k in WEIGHTS], *[new_v[k] for k in WEIGHTS])
```

```python
import functools
import math

import jax
import jax.numpy as jnp
from jax import lax
from jax.experimental import pallas as pl
from jax.experimental.pallas import tpu as pltpu

F32 = jnp.float32
BF16 = jnp.bfloat16
MESH = pl.DeviceIdType.MESH

HEAD_DIM = 64
POOL_WINDOWS = (2, 4, 8, 16)
CONV_WIDTH = 3
RMS_EPS = 1e-6
ADAM_LR = 0.001
ADAM_B1 = 0.9
ADAM_B2 = 0.999
ADAM_EPS = 1e-08
ADAM_WD = 0.01
ADAM_STEP = 10

LANES = 128
VMEM_LIMIT = 56 * 1024 * 1024
N_CHIPS = 4
N_DEV = 8

NN = (((1,), (0,)), ((), ()))
NT = (((1,), (1,)), ((), ()))
TN = (((0,), (0,)), ((), ()))


def _tile(n, pref):
    for t in range(pref - pref % 16, 15, -16):
        if n % t == 0:
            return t
    return n


def _params(n_grid):
    return pltpu.CompilerParams(dimension_semantics=("arbitrary",) * n_grid, vmem_limit_bytes=VMEM_LIMIT)


def _dot(a, b, dims):
    return lax.dot_general(a, b, dims, preferred_element_type=F32)


def _mm(name, dims, operands, in_specs, out_shape, out_specs, grid, acc_shape, epilogue):
    n_in, n_out, nk = len(operands), len(out_shape), grid[-1]

    def kern(*refs):
        extras, outs = refs[2:n_in], refs[n_in:n_in + n_out]
        if nk == 1:
            epilogue(_dot(refs[0][...].astype(BF16), refs[1][...].astype(BF16), dims), extras, outs)
            return
        acc = refs[n_in + n_out]
        k = pl.program_id(len(grid) - 1)

        @pl.when(k == 0)
        def _():
            acc[...] = jnp.zeros_like(acc)

        acc[...] += _dot(refs[0][...].astype(BF16), refs[1][...].astype(BF16), dims)

        @pl.when(k == nk - 1)
        def _():
            epilogue(acc[...], extras, outs)

    return pl.pallas_call(
        kern, name=name, grid=grid, in_specs=in_specs, out_specs=out_specs, out_shape=out_shape,
        scratch_shapes=[pltpu.VMEM(acc_shape, F32)] if nk > 1 else [],
        compiler_params=_params(len(grid)),
    )(*operands)


def _store(scale=None, dtype=None):
    def ep(acc, extras, outs):
        v = acc if scale is None else acc * scale
        outs[0][...] = v.astype(outs[0].dtype)
    return ep


def _residual(scale):
    def ep(acc, extras, outs):
        outs[0][...] = extras[0][...] + scale * acc
    return ep


def _rmsnorm_fwd(x, g):
    T, D = x.shape
    tr = _tile(T, 512)

    def kern(x_ref, g_ref, o_ref):
        xv = x_ref[...]
        r = lax.rsqrt(jnp.mean(xv * xv, axis=-1, keepdims=True) + RMS_EPS)
        o_ref[...] = (xv * r * g_ref[...]).astype(BF16)

    return pl.pallas_call(
        kern, name="rmsnorm_fwd", grid=(T // tr,),
        in_specs=[pl.BlockSpec((tr, D), lambda i: (i, 0)), pl.BlockSpec((1, D), lambda i: (0, 0))],
        out_specs=pl.BlockSpec((tr, D), lambda i: (i, 0)),
        out_shape=jax.ShapeDtypeStruct((T, D), BF16), compiler_params=_params(1),
    )(x, g)


def _rmsnorm_bwd(x, g, dh, dres):
    T, D = x.shape
    tr = _tile(T, 256)

    def kern(x_ref, g_ref, dh_ref, dres_ref, dx_ref, dg_ref):
        xv, dhv = x_ref[...], dh_ref[...]
        r = lax.rsqrt(jnp.mean(xv * xv, axis=-1, keepdims=True) + RMS_EPS)
        y = xv * r
        dy = dhv * g_ref[...]
        dx_ref[...] = dres_ref[...] + r * (dy - y * jnp.mean(dy * y, axis=-1, keepdims=True))
        part = jnp.sum(dhv * y, axis=0, keepdims=True)

        @pl.when(pl.program_id(0) == 0)
        def _():
            dg_ref[...] = part

        @pl.when(pl.program_id(0) > 0)
        def _():
            dg_ref[...] += part

    row = pl.BlockSpec((tr, D), lambda i: (i, 0))
    vec = pl.BlockSpec((1, D), lambda i: (0, 0))
    return pl.pallas_call(
        kern, name="rmsnorm_bwd", grid=(T // tr,), in_specs=[row, vec, row, row], out_specs=[row, vec],
        out_shape=[jax.ShapeDtypeStruct((T, D), F32), jax.ShapeDtypeStruct((1, D), F32)],
        compiler_params=_params(1),
    )(x, g, dh, dres)


def _final_loss(x, g, target):
    T, D = x.shape
    tr = _tile(T, 256)

    def kern(x_ref, g_ref, t_ref, dx_ref, dg_ref, loss_ref):
        xv = x_ref[...]
        r = lax.rsqrt(jnp.mean(xv * xv, axis=-1, keepdims=True) + RMS_EPS)
        y = xv * r
        err = y * g_ref[...] - t_ref[...]
        lpart = 0.5 * jnp.sum(jnp.mean(err * err, axis=-1, keepdims=True), axis=0, keepdims=True)
        dh = err * (1.0 / D)
        dy = dh * g_ref[...]
        dx_ref[...] = r * (dy - y * jnp.mean(dy * y, axis=-1, keepdims=True))
        part = jnp.sum(dh * y, axis=0, keepdims=True)
        lrow = jnp.broadcast_to(lpart, (1, LANES))

        @pl.when(pl.program_id(0) == 0)
        def _():
            dg_ref[...] = part
            loss_ref[...] = lrow

        @pl.when(pl.program_id(0) > 0)
        def _():
            dg_ref[...] += part
            loss_ref[...] += lrow

    row = pl.BlockSpec((tr, D), lambda i: (i, 0))
    vec = pl.BlockSpec((1, D), lambda i: (0, 0))
    return pl.pallas_call(
        kern, name="final_loss", grid=(T // tr,), in_specs=[row, vec, row],
        out_specs=[row, vec, pl.BlockSpec((1, LANES), lambda i: (0, 0))],
        out_shape=[jax.ShapeDtypeStruct((T, D), F32), jax.ShapeDtypeStruct((1, D), F32),
                   jax.ShapeDtypeStruct((1, LANES), F32)],
        compiler_params=_params(1),
    )(x, g, target)


def _ffn_in(h, w4):
    T, D = h.shape
    w4, l = w4
    Fh = w4.shape[3]
    F = 2 * Fh
    tm = _tile(T, 512)

    def kern(h_ref, wg_ref, wu_ref, jac_ref, act_ref):
        hv = h_ref[...]
        gate = _dot(hv, wg_ref[...], NN)
        up = _dot(hv, wu_ref[...], NN)
        sg = jax.nn.sigmoid(gate)
        silu = gate * sg
        jac_ref[0] = (up * (sg + silu * (1.0 - sg))).astype(BF16)
        jac_ref[1] = silu.astype(BF16)
        act_ref[...] = (silu * up).astype(BF16)

    return pl.pallas_call(
        kern, name="ffn_in", grid=(2, T // tm),
        in_specs=[pl.BlockSpec((tm, D), lambda j, i: (i, 0)),
                  pl.BlockSpec((None, None, D, Fh), lambda j, i: (l, j, 0, 0)),
                  pl.BlockSpec((None, None, D, Fh), lambda j, i: (l, 2 + j, 0, 0))],
        out_specs=[pl.BlockSpec((2, tm, Fh), lambda j, i: (0, i, j)),
                   pl.BlockSpec((tm, Fh), lambda j, i: (i, j))],
        out_shape=[jax.ShapeDtypeStruct((2, T, F), BF16), jax.ShapeDtypeStruct((T, F), BF16)],
        compiler_params=_params(2),
    )(h, w4, w4)


def _resident(shape, index_map):
    return pl.BlockSpec(shape, index_map, pipeline_mode=pl.Buffered(1))


def _token_operand(token):
    return ([], []) if token is None else ([token], [pl.BlockSpec(token.shape, lambda i: (0, 0))])


def _ffn_up(x, g, w4, token=None):
    T, D = x.shape
    w4, l = w4
    Fh = w4.shape[3]
    F = 2 * Fh
    tm = _tile(T, 512)
    tok_ops, tok_specs = _token_operand(token)

    def kern(x_ref, g_ref, w_ref, *rest):
        h_ref, jac_ref, act_ref = rest[len(tok_ops):]
        xv = x_ref[...]
        r = lax.rsqrt(jnp.mean(xv * xv, axis=-1, keepdims=True) + RMS_EPS)
        hv = (xv * r * g_ref[...]).astype(BF16)
        h_ref[...] = hv
        for j in range(2):
            cols = slice(j * Fh, (j + 1) * Fh)
            gate = _dot(hv, w_ref[j], NN)
            up = _dot(hv, w_ref[2 + j], NN)
            sg = jax.nn.sigmoid(gate)
            silu = gate * sg
            jac_ref[0, :, cols] = (up * (sg + silu * (1.0 - sg))).astype(BF16)
            jac_ref[1, :, cols] = silu.astype(BF16)
            act_ref[:, cols] = (silu * up).astype(BF16)

    return pl.pallas_call(
        kern, name="ffn_up", grid=(T // tm,),
        in_specs=[pl.BlockSpec((tm, D), lambda i: (i, 0)), pl.BlockSpec((1, D), lambda i: (0, 0)),
                  _resident((None, 4, D, Fh), lambda i: (l, 0, 0, 0))] + tok_specs,
        out_specs=[pl.BlockSpec((tm, D), lambda i: (i, 0)), pl.BlockSpec((2, tm, F), lambda i: (0, i, 0)),
                   pl.BlockSpec((tm, F), lambda i: (i, 0))],
        out_shape=[jax.ShapeDtypeStruct((T, D), BF16), jax.ShapeDtypeStruct((2, T, F), BF16),
                   jax.ShapeDtypeStruct((T, F), BF16)],
        compiler_params=_params(1),
    )(x, g, w4, *tok_ops)


def _ffn_bwd_main(dres, jac, x, g, w_out, w4, token=None):
    T, D = dres.shape
    w_out, l = w_out
    w4, _ = w4
    F = w_out.shape[1]
    Fh = F // 2
    tm = _tile(T, 256)
    tok_ops, tok_specs = _token_operand(token)

    def kern(d_ref, jac_ref, x_ref, g_ref, wo_ref, wi_ref, *rest):
        dgu_ref, dx_ref, dg_ref = rest[len(tok_ops):]
        dv = d_ref[...]
        d16 = dv.astype(BF16)
        dh = jnp.zeros((tm, D), F32)
        for j in range(2):
            cols = slice(j * Fh, (j + 1) * Fh)
            dact = 0.5 * _dot(d16, wo_ref[cols, :], NT)
            dgate = (dact * jac_ref[0, :, cols].astype(F32)).astype(BF16)
            dup = (dact * jac_ref[1, :, cols].astype(F32)).astype(BF16)
            dgu_ref[0, :, cols] = dgate
            dgu_ref[1, :, cols] = dup
            dh = dh + _dot(dgate, wi_ref[j], NT) + _dot(dup, wi_ref[2 + j], NT)
        xv = x_ref[...]
        r = lax.rsqrt(jnp.mean(xv * xv, axis=-1, keepdims=True) + RMS_EPS)
        y = xv * r
        dy = dh * g_ref[...]
        dx_ref[...] = dv + r * (dy - y * jnp.mean(dy * y, axis=-1, keepdims=True))
        part = jnp.sum(dh * y, axis=0, keepdims=True)

        @pl.when(pl.program_id(0) == 0)
        def _():
            dg_ref[...] = part

        @pl.when(pl.program_id(0) > 0)
        def _():
            dg_ref[...] += part

    row = pl.BlockSpec((tm, D), lambda i: (i, 0))
    vec = pl.BlockSpec((1, D), lambda i: (0, 0))
    wide = pl.BlockSpec((2, tm, F), lambda i: (0, i, 0))
    return pl.pallas_call(
        kern, name="ffn_bwd_main", grid=(T // tm,),
        in_specs=[row, wide, row, vec, _resident((None, F, D), lambda i: (l, 0, 0)),
                  _resident((None, 4, D, Fh), lambda i: (l, 0, 0, 0))] + tok_specs,
        out_specs=[wide, row, vec],
        out_shape=[jax.ShapeDtypeStruct((2, T, F), BF16), jax.ShapeDtypeStruct((T, D), F32),
                   jax.ShapeDtypeStruct((1, D), F32)],
        compiler_params=_params(1),
    )(dres, jac, x, g, w_out, w4, *tok_ops)


def _ffn_out(act, w_out, x):
    T, F = act.shape
    w_out, l = w_out
    D = w_out.shape[2]
    tm = _tile(T, 512)
    return _mm("ffn_out", NN, [act, w_out, x],
               [pl.BlockSpec((tm, F), lambda i, k: (i, 0)), pl.BlockSpec((None, F, D), lambda i, k: (l, 0, 0)),
                pl.BlockSpec((tm, D), lambda i, k: (i, 0))],
               [jax.ShapeDtypeStruct((T, D), F32)], [pl.BlockSpec((tm, D), lambda i, k: (i, 0))],
               (T // tm, 1), None, _residual(0.5))


def _ffn_bwd_act(dres, w_out, jac):
    T, D = dres.shape
    w_out, l = w_out
    F = w_out.shape[1]
    Fh = F // 2
    tm = _tile(T, 512)

    def kern(d_ref, w_ref, jac_ref, o_ref):
        dact = 0.5 * _dot(d_ref[...].astype(BF16), w_ref[...], NT)
        o_ref[0] = (dact * jac_ref[0].astype(F32)).astype(BF16)
        o_ref[1] = (dact * jac_ref[1].astype(F32)).astype(BF16)

    return pl.pallas_call(
        kern, name="ffn_bwd_act", grid=(2, T // tm),
        in_specs=[pl.BlockSpec((tm, D), lambda j, i: (i, 0)), pl.BlockSpec((None, Fh, D), lambda j, i: (l, j, 0)),
                  pl.BlockSpec((2, tm, Fh), lambda j, i: (0, i, j))],
        out_specs=pl.BlockSpec((2, tm, Fh), lambda j, i: (0, i, j)),
        out_shape=jax.ShapeDtypeStruct((2, T, F), BF16), compiler_params=_params(2),
    )(dres, w_out, jac)


def _ffn_dw_out(act, dres):
    T, F = act.shape
    D = dres.shape[1]
    tm, tk = F // 2, _tile(T, 1024)
    return _mm("ffn_dw_out", TN, [act, dres],
               [pl.BlockSpec((tk, tm), lambda i, k: (k, i)), pl.BlockSpec((tk, D), lambda i, k: (k, 0))],
               [jax.ShapeDtypeStruct((F, D), BF16)], [pl.BlockSpec((tm, D), lambda i, k: (i, 0))],
               (2, T // tk), (tm, D), _store(0.5))


def _ffn_dw_in(h, dgu):
    T, D = h.shape
    Fh = dgu.shape[2] // 2
    tk = _tile(T, 1024)
    return _mm("ffn_dw_in", TN, [h, dgu],
               [pl.BlockSpec((tk, D), lambda j, k: (k, 0)),
                pl.BlockSpec((None, tk, Fh), lambda j, k: (j // 2, k, j % 2))],
               [jax.ShapeDtypeStruct((4, D, Fh), BF16)], [pl.BlockSpec((None, D, Fh), lambda j, k: (j, 0, 0))],
               (4, T // tk), (D, Fh), _store())


def _ffn_dh(dgu, w4):
    T = dgu.shape[1]
    w4, l = w4
    D, Fh = w4.shape[2], w4.shape[3]
    tm = _tile(T, 1024)
    return _mm("ffn_dh", NT, [dgu, w4],
               [pl.BlockSpec((None, tm, Fh), lambda i, k: (k // 2, i, k % 2)),
                pl.BlockSpec((None, None, D, Fh), lambda i, k: (l, k, 0, 0))],
               [jax.ShapeDtypeStruct((T, D), F32)], [pl.BlockSpec((tm, D), lambda i, k: (i, 0))],
               (T // tm, 4), (tm, D), _store())


def _proj(name, a, w, out_dtype, dims=NN, extra=None, scale=None):
    T, K = a.shape
    w, l = w
    N = w.shape[2] if dims == NN else w.shape[1]
    tm = _tile(T, 512)
    ops = [a, w] + ([extra] if extra is not None else [])
    specs = [pl.BlockSpec((tm, K), lambda i, k: (i, 0)), pl.BlockSpec((None,) + w.shape[1:], lambda i, k: (l, 0, 0))]
    if extra is not None:
        specs.append(pl.BlockSpec((tm, N), lambda i, k: (i, 0)))
    ep = _residual(1.0) if extra is not None else _store(scale)
    return _mm(name, dims, ops, specs, [jax.ShapeDtypeStruct((T, N), out_dtype)],
               [pl.BlockSpec((tm, N), lambda i, k: (i, 0))], (T // tm, 1), None, ep)[0]


def _mix_up(x, g, wp, widths):
    T, D = x.shape
    wp, l = wp
    n_qkv, n_rest = widths
    NP = wp.shape[2]
    tm = _tile(T, 512)

    def kern(x_ref, g_ref, w_ref, h_ref, qkv_ref, rest_ref, fl_ref):
        xv = x_ref[...]
        r = lax.rsqrt(jnp.mean(xv * xv, axis=-1, keepdims=True) + RMS_EPS)
        hv = (xv * r * g_ref[...]).astype(BF16)
        h_ref[...] = hv
        qkv_ref[...] = _dot(hv, w_ref[:, 0:n_qkv], NN).astype(BF16)
        rest_ref[...] = _dot(hv, w_ref[:, n_qkv:n_qkv + n_rest], NN)
        fl_ref[...] = _dot(hv, w_ref[:, n_qkv + n_rest:NP], NN)

    row = lambda n: pl.BlockSpec((tm, n), lambda i: (i, 0))
    return pl.pallas_call(
        kern, name="mix_up", grid=(T // tm,),
        in_specs=[row(D), pl.BlockSpec((1, D), lambda i: (0, 0)), _resident((None, D, NP), lambda i: (l, 0, 0))],
        out_specs=[row(D), row(n_qkv), row(n_rest), row(LANES)],
        out_shape=[jax.ShapeDtypeStruct((T, D), BF16), jax.ShapeDtypeStruct((T, n_qkv), BF16),
                   jax.ShapeDtypeStruct((T, n_rest), F32), jax.ShapeDtypeStruct((T, LANES), F32)],
        compiler_params=_params(1),
    )(x, g, wp)


def _column_starts(pieces):
    starts, at = [], 0
    for p in pieces:
        starts.append(at)
        at += p.shape[1]
    return starts


def _mix_in_bwd(pieces, x, g, dres, wp):
    T, D = x.shape
    wp, l = wp
    NP = wp.shape[2]
    tm = _tile(T, 512)
    n, starts = len(pieces), _column_starts(pieces)

    def kern(*refs):
        x_ref, g_ref, d_ref, w_ref, dx_ref, dg_ref = refs[n:]
        dh = jnp.zeros((tm, D), F32)
        for p_ref, at in zip(refs[:n], starts):
            dh = dh + _dot(p_ref[...].astype(BF16), w_ref[:, at:at + p_ref.shape[1]], NT)
        xv = x_ref[...]
        r = lax.rsqrt(jnp.mean(xv * xv, axis=-1, keepdims=True) + RMS_EPS)
        y = xv * r
        dy = dh * g_ref[...]
        dx_ref[...] = d_ref[...] + r * (dy - y * jnp.mean(dy * y, axis=-1, keepdims=True))
        part = jnp.sum(dh * y, axis=0, keepdims=True)

        @pl.when(pl.program_id(0) == 0)
        def _():
            dg_ref[...] = part

        @pl.when(pl.program_id(0) > 0)
        def _():
            dg_ref[...] += part

    row = lambda n: pl.BlockSpec((tm, n), lambda i: (i, 0))
    vec = pl.BlockSpec((1, D), lambda i: (0, 0))
    return pl.pallas_call(
        kern, name="mix_in_bwd", grid=(T // tm,),
        in_specs=[row(p.shape[1]) for p in pieces] + [row(D), vec, row(D), _resident((None, D, NP), lambda i: (l, 0, 0))],
        out_specs=[row(D), vec],
        out_shape=[jax.ShapeDtypeStruct((T, D), F32), jax.ShapeDtypeStruct((1, D), F32)],
        compiler_params=_params(1),
    )(*pieces, x, g, dres, wp)


def _pieces_dw(name, a, pieces, out_dtype, tk_pref):
    T, M = a.shape
    n, starts = len(pieces), _column_starts(pieces)
    N = starts[-1] + pieces[-1].shape[1]
    tk = _tile(T, tk_pref)
    nk = T // tk

    def kern(a_ref, *refs):
        o_ref, acc = refs[n], refs[n + 1]
        k = pl.program_id(0)

        @pl.when(k == 0)
        def _():
            acc[...] = jnp.zeros_like(acc)

        av = a_ref[...].astype(BF16)
        for p_ref, at in zip(refs[:n], starts):
            acc[:, at:at + p_ref.shape[1]] += _dot(av, p_ref[...].astype(BF16), TN)

        @pl.when(k == nk - 1)
        def _():
            o_ref[...] = acc[...].astype(out_dtype)

    return pl.pallas_call(
        kern, name=name, grid=(nk,),
        in_specs=[pl.BlockSpec((tk, M), lambda k: (k, 0))] + [pl.BlockSpec((tk, p.shape[1]), lambda k: (k, 0)) for p in pieces],
        out_specs=pl.BlockSpec((M, N), lambda k: (0, 0)), out_shape=jax.ShapeDtypeStruct((M, N), out_dtype),
        scratch_shapes=[pltpu.VMEM((M, N), F32)], compiler_params=_params(1),
    )(a, *pieces)


def _rows_dw(name, pieces, d, out_dtype):
    T, N = d.shape
    n, starts = len(pieces), _column_starts(pieces)
    M = starts[-1] + pieces[-1].shape[1]
    tk = _tile(T, 1024)
    nk = T // tk

    def kern(*refs):
        d_ref, o_ref, acc = refs[n], refs[n + 1], refs[n + 2]
        k = pl.program_id(0)

        @pl.when(k == 0)
        def _():
            acc[...] = jnp.zeros_like(acc)

        dv = d_ref[...].astype(BF16)
        for p_ref, at in zip(refs[:n], starts):
            acc[at:at + p_ref.shape[1], :] += _dot(p_ref[...], dv, TN)

        @pl.when(k == nk - 1)
        def _():
            o_ref[...] = acc[...].astype(out_dtype)

    return pl.pallas_call(
        kern, name=name, grid=(nk,),
        in_specs=[pl.BlockSpec((tk, p.shape[1]), lambda k: (k, 0)) for p in pieces] + [pl.BlockSpec((tk, N), lambda k: (k, 0))],
        out_specs=pl.BlockSpec((M, N), lambda k: (0, 0)), out_shape=jax.ShapeDtypeStruct((M, N), out_dtype),
        scratch_shapes=[pltpu.VMEM((M, N), F32)], compiler_params=_params(1),
    )(*pieces, d)


def _mix_out(pieces, w, x):
    T, D = x.shape
    w, l = w
    n, starts = len(pieces), _column_starts(pieces)
    tm = _tile(T, 512)

    def kern(*refs):
        w_ref, x_ref, o_ref = refs[n:]
        acc = x_ref[...]
        for p_ref, at in zip(refs[:n], starts):
            acc = acc + _dot(p_ref[...], w_ref[at:at + p_ref.shape[1], :], NN)
        o_ref[...] = acc

    row = lambda m: pl.BlockSpec((tm, m), lambda i: (i, 0))
    return pl.pallas_call(
        kern, name="mix_out", grid=(T // tm,),
        in_specs=[row(p.shape[1]) for p in pieces] + [_resident((None,) + w.shape[1:], lambda i: (l, 0, 0)), row(D)],
        out_specs=row(D), out_shape=jax.ShapeDtypeStruct((T, D), F32), compiler_params=_params(1),
    )(*pieces, w, x)


def _dw(name, a, d, out_dtype):
    T, M = a.shape
    N = d.shape[1]
    tk = _tile(T, 1024 if M * N <= 1024 * 1408 else 512)
    return _mm(name, TN, [a, d],
               [pl.BlockSpec((tk, M), lambda i, k: (k, 0)), pl.BlockSpec((tk, N), lambda i, k: (k, 0))],
               [jax.ShapeDtypeStruct((M, N), out_dtype)], [pl.BlockSpec((M, N), lambda i, k: (0, 0))],
               (1, T // tk), (M, N), _store())[0]


def _log_sigmoid(z):
    return jnp.minimum(z, 0.0) - jnp.log(1.0 + jnp.exp(-jnp.abs(z)))


def _decay_fwd(fl, bias):
    B, S, _ = fl.shape

    def kern(fl_ref, b_ref, o_ref):
        d = _log_sigmoid(fl_ref[...] + b_ref[...])
        row = lax.broadcasted_iota(jnp.int32, (S, LANES), 0)
        sh = 1
        while sh < S:
            d = d + jnp.where(row >= sh, pltpu.roll(d, sh, 0), 0.0)
            sh *= 2
        o_ref[...] = d.T[0:8, :]

    return pl.pallas_call(
        kern, name="decay_fwd", grid=(B,),
        in_specs=[pl.BlockSpec((None, S, LANES), lambda b: (b, 0, 0)), pl.BlockSpec((1, LANES), lambda b: (0, 0))],
        out_specs=pl.BlockSpec((None, 8, S), lambda b: (b, 0, 0)),
        out_shape=jax.ShapeDtypeStruct((B, 8, S), F32), compiler_params=_params(1),
    )(fl, bias)


def _decay_bwd(ddrow, ddcol, fl, bias, n_heads):
    B, S, _ = fl.shape

    def kern(dd_ref, ddc_ref, fl_ref, b_ref, dfl_ref, db_ref):
        dd = jnp.concatenate([dd_ref[...], jnp.zeros((LANES - 8, S), F32)], axis=0).T + ddc_ref[...]
        row = lax.broadcasted_iota(jnp.int32, (S, LANES), 0)
        lane = lax.broadcasted_iota(jnp.int32, (S, LANES), 1)
        sh = 1
        while sh < S:
            dd = dd + jnp.where(row < S - sh, pltpu.roll(dd, S - sh, 0), 0.0)
            sh *= 2
        z = fl_ref[...] + b_ref[...]
        dfl = jnp.where(lane < n_heads, dd / (1.0 + jnp.exp(z)), 0.0)
        dfl_ref[...] = dfl
        part = jnp.sum(dfl, axis=0, keepdims=True)

        @pl.when(pl.program_id(0) == 0)
        def _():
            db_ref[...] = part

        @pl.when(pl.program_id(0) > 0)
        def _():
            db_ref[...] += part

    return pl.pallas_call(
        kern, name="decay_bwd", grid=(B,),
        in_specs=[pl.BlockSpec((None, 8, S), lambda b: (b, 0, 0)), pl.BlockSpec((None, S, LANES), lambda b: (b, 0, 0)),
                  pl.BlockSpec((None, S, LANES), lambda b: (b, 0, 0)), pl.BlockSpec((1, LANES), lambda b: (0, 0))],
        out_specs=[pl.BlockSpec((None, S, LANES), lambda b: (b, 0, 0)), pl.BlockSpec((1, LANES), lambda b: (0, 0))],
        out_shape=[jax.ShapeDtypeStruct((B, S, LANES), F32), jax.ShapeDtypeStruct((1, LANES), F32)],
        compiler_params=_params(1),
    )(ddrow, ddcol, fl, bias)


def _attn_fwd(qkv, drow, n_heads, tq):
    B, S, _ = qkv.shape
    DA = n_heads * HEAD_DIM
    scale = HEAD_DIM ** -0.5

    n_pairs = n_heads // 2

    def kern(q_ref, k_ref, v_ref, dr_ref, o_ref, lse_ref, vm_s):
        i = pl.program_id(1)

        @pl.when(i == 0)
        def _():
            lane_s = lax.broadcasted_iota(jnp.int32, (S, LANES), 1)
            for p in range(n_pairs):
                v2 = v_ref[:, LANES * p:LANES * (p + 1)]
                one, zero = jnp.ones_like(v2), jnp.zeros_like(v2)
                vm_s[2 * p] = jnp.where(lane_s == HEAD_DIM, one, jnp.where(lane_s < HEAD_DIM, v2, zero))
                vm_s[2 * p + 1] = jnp.where(lane_s == 0, one, jnp.where(lane_s < HEAD_DIM, zero, v2))

        lane = lax.broadcasted_iota(jnp.int32, (tq, LANES), 1)
        low = lane < HEAD_DIM
        causal = lax.broadcasted_iota(jnp.int32, (tq, tq), 1) <= lax.broadcasted_iota(jnp.int32, (tq, tq), 0)
        qms = []
        for p in range(n_pairs):
            q2 = q_ref[:, LANES * p:LANES * (p + 1)] * scale
            qms += [jnp.where(low, q2, jnp.zeros_like(q2)), jnp.where(low, jnp.zeros_like(q2), q2)]

        def step(j, carry, masked):
            ms, accs = carry
            ks = pl.multiple_of(j * tq, tq)
            new_m, new_acc = [], []
            for p in range(n_pairs):
                k2 = k_ref[pl.ds(ks, tq), LANES * p:LANES * (p + 1)]
                for h in (2 * p, 2 * p + 1):
                    s = _dot(qms[h], k2, NT) - dr_ref[h, pl.ds(j, 1), :]
                    if masked:
                        s = jnp.where(causal, s, -jnp.inf)
                    m_new = jnp.maximum(ms[h], jnp.max(s, axis=1, keepdims=True))
                    alpha = jnp.exp(ms[h] - m_new)
                    pm = jnp.exp(s - m_new)
                    new_m.append(m_new)
                    new_acc.append(alpha * accs[h] + _dot(pm.astype(BF16), vm_s[h, pl.ds(ks, tq), :], NN))
            return tuple(new_m), tuple(new_acc)

        init = (tuple(jnp.full((tq, 1), -jnp.inf, F32) for _ in range(n_heads)),
                tuple(jnp.zeros((tq, LANES), F32) for _ in range(n_heads)))
        ms, accs = step(i, lax.fori_loop(0, i, functools.partial(step, masked=False), init), True)
        lse_mat = jnp.zeros((tq, LANES), F32)
        for p in range(n_pairs):
            a0, a1 = accs[2 * p], accs[2 * p + 1]
            l0, l1 = a0[:, HEAD_DIM:HEAD_DIM + 1], a1[:, 0:1]
            o_ref[:, LANES * p:LANES * (p + 1)] = jnp.where(low, a0 / l0, a1 / l1).astype(BF16)
            lse_mat = jnp.where(lane == 2 * p, ms[2 * p] + jnp.log(l0), lse_mat)
            lse_mat = jnp.where(lane == 2 * p + 1, ms[2 * p + 1] + jnp.log(l1), lse_mat)
        lse_ref[...] = lse_mat

    nq = S // tq
    return pl.pallas_call(
        kern, name="attn_fwd", grid=(B, nq),
        in_specs=[pl.BlockSpec((None, tq, DA), lambda b, i: (b, i, 0)),
                  pl.BlockSpec((None, S, DA), lambda b, i: (b, 0, 1)),
                  pl.BlockSpec((None, S, DA), lambda b, i: (b, 0, 2)),
                  pl.BlockSpec((None, 8, nq, tq), lambda b, i: (b, 0, 0, 0))],
        out_specs=[pl.BlockSpec((None, tq, DA), lambda b, i: (b, i, 0)),
                   pl.BlockSpec((None, tq, LANES), lambda b, i: (b, i, 0))],
        out_shape=[jax.ShapeDtypeStruct((B, S, DA), BF16), jax.ShapeDtypeStruct((B, S, LANES), F32)],
        scratch_shapes=[pltpu.VMEM((n_heads, S, LANES), BF16)],
        compiler_params=_params(2),
    )(qkv, qkv, qkv, drow)


def _attn_bwd(qkv, drow, o, lse, dycat, n_heads, tq):
    B, S, _ = qkv.shape
    DA = n_heads * HEAD_DIM
    scale = HEAD_DIM ** -0.5
    nq = S // tq

    n_pairs = n_heads // 2

    def kern(q_ref, k_ref, v_ref, dr_ref, o_ref, lse_ref, do_ref, dq_ref, dk_ref, dv_ref, ddr_ref, ddc_ref,
             dk_acc, dv_acc, qm_s, dom_s, delta_s, rs_s, dq_s):
        i = pl.program_id(1)

        @pl.when(i == 0)
        def _():
            dk_acc[...] = jnp.zeros_like(dk_acc)
            dv_acc[...] = jnp.zeros_like(dv_acc)
            ddr_ref[...] = jnp.zeros_like(ddr_ref)

        lane = lax.broadcasted_iota(jnp.int32, (tq, LANES), 1)
        low = lane < HEAD_DIM
        causal = lax.broadcasted_iota(jnp.int32, (tq, tq), 1) <= lax.broadcasted_iota(jnp.int32, (tq, tq), 0)
        for p in range(n_pairs):
            cols = slice(LANES * p, LANES * (p + 1))
            q2 = q_ref[:, cols] * scale
            do_f = do_ref[:, cols]
            do2 = do_f.astype(BF16)
            prod = do_f * o_ref[:, cols].astype(F32)
            qm_s[2 * p] = jnp.where(low, q2, jnp.zeros_like(q2))
            qm_s[2 * p + 1] = jnp.where(low, jnp.zeros_like(q2), q2)
            dom_s[2 * p] = jnp.where(low, do2, jnp.zeros_like(do2))
            dom_s[2 * p + 1] = jnp.where(low, jnp.zeros_like(do2), do2)
            delta_s[2 * p] = jnp.sum(jnp.where(low, prod, 0.0), axis=1, keepdims=True)
            delta_s[2 * p + 1] = jnp.sum(jnp.where(low, 0.0, prod), axis=1, keepdims=True)
            dq_s[p] = jnp.zeros((tq, LANES), F32)
        rs_s[...] = jnp.zeros(rs_s.shape, F32)

        def step(j, masked):
            ks = pl.multiple_of(j * tq, tq)
            for p in range(n_pairs):
                cols = slice(LANES * p, LANES * (p + 1))
                k2, v2 = k_ref[pl.ds(ks, tq), cols], v_ref[pl.ds(ks, tq), cols]
                dvs, dks, dqs = [], [], []
                for h in (2 * p, 2 * p + 1):
                    qm, dom = qm_s[h], dom_s[h]
                    s = _dot(qm, k2, NT) - dr_ref[h, pl.ds(j, 1), :]
                    if masked:
                        s = jnp.where(causal, s, -jnp.inf)
                    pm = jnp.exp(s - lse_ref[:, h:h + 1])
                    ds = pm * (_dot(dom, v2, NT) - delta_s[h])
                    ddr_ref[h, pl.ds(j, 1), :] -= jnp.sum(ds, axis=0, keepdims=True)
                    rs_s[h] += jnp.sum(ds, axis=1, keepdims=True)
                    dsb = ds.astype(BF16)
                    dvs.append(_dot(pm.astype(BF16), dom, TN))
                    dks.append(_dot(dsb, qm, TN))
                    dqs.append(_dot(dsb, k2, NN))
                dv_acc[pl.ds(ks, tq), cols] += dvs[0] + dvs[1]
                dk_acc[pl.ds(ks, tq), cols] += dks[0] + dks[1]
                dq_s[p] += jnp.where(low, dqs[0], dqs[1])

        def body(j, carry):
            step(j, False)
            return carry

        lax.fori_loop(0, i, body, 0)
        step(i, True)
        ddc = jnp.zeros((tq, LANES), F32)
        for p in range(n_pairs):
            dq_ref[:, LANES * p:LANES * (p + 1)] = (dq_s[p] * scale).astype(BF16)
            ddc = jnp.where(lane == 2 * p, rs_s[2 * p], ddc)
            ddc = jnp.where(lane == 2 * p + 1, rs_s[2 * p + 1], ddc)
        ddc_ref[...] = ddc

        @pl.when(i == nq - 1)
        def _():
            dk_ref[...] = dk_acc[...].astype(BF16)
            dv_ref[...] = dv_acc[...].astype(BF16)

    tile = pl.BlockSpec((None, tq, DA), lambda b, i: (b, i, 0))
    seq = pl.BlockSpec((None, S, DA), lambda b, i: (b, 0, 0))
    dec = pl.BlockSpec((None, 8, nq, tq), lambda b, i: (b, 0, 0, 0))
    return pl.pallas_call(
        kern, name="attn_bwd", grid=(B, nq),
        in_specs=[tile, pl.BlockSpec((None, S, DA), lambda b, i: (b, 0, 1)),
                  pl.BlockSpec((None, S, DA), lambda b, i: (b, 0, 2)), dec, tile,
                  pl.BlockSpec((None, tq, LANES), lambda b, i: (b, i, 0)), tile],
        out_specs=[tile, seq, seq, dec, pl.BlockSpec((None, tq, LANES), lambda b, i: (b, i, 0))],
        out_shape=[jax.ShapeDtypeStruct((B, S, DA), BF16)] * 3 + [jax.ShapeDtypeStruct((B, 8, nq, tq), F32),
                                                                  jax.ShapeDtypeStruct((B, S, LANES), F32)],
        scratch_shapes=[pltpu.VMEM((S, DA), F32), pltpu.VMEM((S, DA), F32),
                        pltpu.VMEM((n_heads, tq, LANES), BF16), pltpu.VMEM((n_heads, tq, LANES), BF16),
                        pltpu.VMEM((n_heads, tq, 1), F32), pltpu.VMEM((n_heads, tq, 1), F32),
                        pltpu.VMEM((n_pairs, tq, LANES), F32)],
        compiler_params=_params(2),
    )(qkv, qkv, qkv, drow, o, lse, dycat)


def _down(v, d, row):
    return jnp.where(row >= d, pltpu.roll(v, d, 0), 0.0)


def _up(v, d, row, S):
    return jnp.where(row < S - d, pltpu.roll(v, S - d, 0), 0.0)


def _window(v, shift, group):
    sums, acc, d = [], v, 1
    for _ in POOL_WINDOWS:
        acc = acc + shift(acc, d)
        sums.append(acc)
        d *= 2
    out = sums[-1]
    for gi in range(len(POOL_WINDOWS) - 2, -1, -1):
        out = jnp.where(group == gi, sums[gi], out)
    return out


def _pool_count(row, group):
    w = jnp.full(row.shape, POOL_WINDOWS[-1], jnp.int32)
    for gi in range(len(POOL_WINDOWS) - 2, -1, -1):
        w = jnp.where(group == gi, POOL_WINDOWS[gi], w)
    return jnp.minimum(row + 1, w).astype(F32)


def _mix_local_fwd(rest, wbd, ps, cw):
    B, S, C4 = rest.shape
    C = C4 // 4
    gw = C // len(POOL_WINDOWS)

    def kern(r_ref, w_ref, ps_ref, cw_ref, y_ref, pooled_ref):
        row = lax.broadcasted_iota(jnp.int32, (S, C), 0)
        group = lax.broadcasted_iota(jnp.int32, (S, C), 1) // gw
        u = r_ref[:, 0:C]
        pooled = _window(u, lambda v, d: _down(v, d, row), group) / _pool_count(row, group) - u
        pb = pooled.astype(BF16)
        pooled_ref[...] = pb
        y_ref[:, 0:C] = (_dot(pb, w_ref[...], NN) * ps_ref[...]).astype(BF16)
        uc = r_ref[:, 2 * C:3 * C] * r_ref[:, 3 * C:4 * C]
        y = cw_ref[0:1, :] * _down(uc, 2, row) + cw_ref[1:2, :] * _down(uc, 1, row) + cw_ref[2:3, :] * uc
        y_ref[:, C:2 * C] = (r_ref[:, C:2 * C] * y).astype(BF16)

    return pl.pallas_call(
        kern, name="mix_local_fwd", grid=(B,),
        in_specs=[pl.BlockSpec((None, S, C4), lambda b: (b, 0, 0)), pl.BlockSpec((C, C), lambda b: (0, 0)),
                  pl.BlockSpec((1, C), lambda b: (0, 0)), pl.BlockSpec((8, C), lambda b: (0, 0))],
        out_specs=[pl.BlockSpec((None, S, 2 * C), lambda b: (b, 0, 0)), pl.BlockSpec((None, S, C), lambda b: (b, 0, 0))],
        out_shape=[jax.ShapeDtypeStruct((B, S, 2 * C), BF16), jax.ShapeDtypeStruct((B, S, C), BF16)],
        compiler_params=_params(1),
    )(rest, wbd, ps, cw)


def _mix_local_bwd(rest, pooled, dycat, wbd, ps, cw):
    B, S, C4 = rest.shape
    C = C4 // 4
    gw = C // len(POOL_WINDOWS)

    def kern(r_ref, pooled_ref, d_ref, w_ref, ps_ref, cw_ref, dr_ref, dw_ref, dps_ref, dcw_ref):
        row = lax.broadcasted_iota(jnp.int32, (S, C), 0)
        group = lax.broadcasted_iota(jnp.int32, (S, C), 1) // gw
        dyp = d_ref[:, 0:C]
        dyc = d_ref[:, C:2 * C]
        pb = pooled_ref[...]
        dps = jnp.sum(dyp * _dot(pb, w_ref[...], NN), axis=0, keepdims=True)
        dzb = (dyp * ps_ref[...]).astype(BF16)
        dw = _dot(pb, dzb, TN)
        dpooled = _dot(dzb, w_ref[...], NT)
        g = dpooled / _pool_count(row, group)
        dr_ref[:, 0:C] = (_window(g, lambda v, d: _up(v, d, row, S), group) - dpooled).astype(BF16)
        cc, ch = r_ref[:, 2 * C:3 * C], r_ref[:, 3 * C:4 * C]
        uc = cc * ch
        u1, u2 = _down(uc, 1, row), _down(uc, 2, row)
        y = cw_ref[0:1, :] * u2 + cw_ref[1:2, :] * u1 + cw_ref[2:3, :] * uc
        dr_ref[:, C:2 * C] = (dyc * y).astype(BF16)
        dy = dyc * r_ref[:, C:2 * C]
        duc = cw_ref[0:1, :] * _up(dy, 2, row, S) + cw_ref[1:2, :] * _up(dy, 1, row, S) + cw_ref[2:3, :] * dy
        dr_ref[:, 2 * C:3 * C] = (duc * ch).astype(BF16)
        dr_ref[:, 3 * C:4 * C] = (duc * cc).astype(BF16)
        dcw = jnp.concatenate([jnp.sum(dy * u2, axis=0, keepdims=True), jnp.sum(dy * u1, axis=0, keepdims=True),
                               jnp.sum(dy * uc, axis=0, keepdims=True), jnp.zeros((5, C), F32)], axis=0)

        @pl.when(pl.program_id(0) == 0)
        def _():
            dw_ref[...] = dw
            dps_ref[...] = dps
            dcw_ref[...] = dcw

        @pl.when(pl.program_id(0) > 0)
        def _():
            dw_ref[...] += dw
            dps_ref[...] += dps
            dcw_ref[...] += dcw

    full = lambda shape: pl.BlockSpec(shape, lambda b: (0, 0))
    return pl.pallas_call(
        kern, name="mix_local_bwd", grid=(B,),
        in_specs=[pl.BlockSpec((None, S, C4), lambda b: (b, 0, 0)), pl.BlockSpec((None, S, C), lambda b: (b, 0, 0)),
                  pl.BlockSpec((None, S, 2 * C), lambda b: (b, 0, 1)), full((C, C)), full((1, C)), full((8, C))],
        out_specs=[pl.BlockSpec((None, S, C4), lambda b: (b, 0, 0)), full((C, C)), full((1, C)), full((8, C))],
        out_shape=[jax.ShapeDtypeStruct((B, S, C4), BF16), jax.ShapeDtypeStruct((C, C), F32),
                   jax.ShapeDtypeStruct((1, C), F32), jax.ShapeDtypeStruct((8, C), F32)],
        compiler_params=_params(1),
    )(rest, pooled, dycat, wbd, ps, cw)


def _adamw(w, gs, m, v):
    R, C = w.shape
    pieces = [p if isinstance(p, tuple) else (p,) for p in gs]
    owner = [s for s, p in enumerate(pieces) for _ in p]
    flat = [a for p in pieces for a in p]
    n = len(flat)
    rows = R // len(pieces)
    tr = _tile(rows, 256)
    per = rows // tr

    def kern(w_ref, *refs):
        g_refs, (m_ref, v_ref, g_out, d_ref, nm_ref, nv_ref) = refs[:n], refs[n:]
        vals, at = [], 0
        for p in pieces:
            vals.append(g_refs[at][...] if len(p) == 1 else g_refs[at][...] + g_refs[at + 1][...])
            at += len(p)
        gv = vals[0]
        for s in range(1, len(pieces)):
            gv = jnp.where(pl.program_id(0) // per == s, vals[s], gv)
        nm = ADAM_B1 * m_ref[...] + (1.0 - ADAM_B1) * gv
        nv = ADAM_B2 * v_ref[...] + (1.0 - ADAM_B2) * (gv * gv)
        m_hat = nm / (1.0 - ADAM_B1 ** ADAM_STEP)
        v_hat = nv / (1.0 - ADAM_B2 ** ADAM_STEP)
        g_out[...] = gv
        d_ref[...] = -ADAM_LR * (m_hat / (jnp.sqrt(v_hat) + ADAM_EPS) + ADAM_WD * w_ref[...])
        nm_ref[...] = nm
        nv_ref[...] = nv

    def piece(s):
        return pl.BlockSpec((tr, C), lambda i: (jnp.clip(i - s * per, 0, per - 1), 0))

    blk = pl.BlockSpec((tr, C), lambda i: (i, 0))
    return pl.pallas_call(
        kern, name="adamw", grid=(R // tr,), in_specs=[blk] + [piece(s) for s in owner] + [blk] * 2,
        out_specs=[blk] * 4, out_shape=[jax.ShapeDtypeStruct((R, C), F32)] * 4, compiler_params=_params(1),
    )(w, *flat, m, v)


def _place():
    x, y, c = lax.axis_index("x"), lax.axis_index("y"), lax.axis_index("c")
    return x, y, c, [(1 - x, y), (x, 1 - y), (1 - x, 1 - y)]


def _comm_call(name, body, operands, out_shape, n_sems, aliases=None):
    any_spec = pl.BlockSpec(memory_space=pl.ANY)
    return pl.pallas_call(
        body, name=name, in_specs=[any_spec] * len(operands), out_specs=[any_spec] * len(out_shape),
        out_shape=out_shape, input_output_aliases=aliases or {},
        scratch_shapes=[pltpu.SemaphoreType.DMA((n,)) for n in n_sems],
    )(*operands)


def _my_block():
    return 2 * lax.axis_index("x") + lax.axis_index("y")


def _place_shard(w, dtype, first=0, count=None):
    L, R, C = w.shape
    count = L if count is None else count
    tr = _tile(R, 512)

    def kern(w_ref, o_ref):
        o_ref[...] = w_ref[...].astype(dtype)

    return pl.pallas_call(
        kern, name="place_shard", grid=(count, R // tr),
        in_specs=[pl.BlockSpec((None, tr, C), lambda l, i: (first + l, i, 0))],
        out_specs=pl.BlockSpec((None, None, tr, C), lambda l, i: (l, _my_block(), i, 0)),
        out_shape=jax.ShapeDtypeStruct((count, N_CHIPS, R, C), dtype), compiler_params=_params(2),
    )(w)


HALF_ROWS = 16


def _rows(ref, half):
    hr = ref.shape[-2] // 2
    return ref.at[(slice(None),) * (len(ref.shape) - 2) + (pl.ds(half * hr, hr),)]


def _all_gather(bufs):
    n = len(bufs)

    def body(*refs):
        outs = refs[n:2 * n]
        send_sems, recv_sems = refs[2 * n:]
        x, y, c, chips = _place()
        sibling = (x, y, 1 - c)

        def remote(k, j, chip, half, to):
            blk = 2 * chip[0] + chip[1]
            if outs[k].shape[2] % (2 * HALF_ROWS) == 0:
                region = _rows(outs[k].at[:, blk], half)
            else:
                hl = outs[k].shape[0] // 2
                region = outs[k].at[pl.ds(half * hl, hl), blk]
            return pltpu.make_async_remote_copy(
                src_ref=region, dst_ref=region, send_sem=send_sems.at[6 * k + j],
                recv_sem=recv_sems.at[6 * k + j], device_id=to, device_id_type=MESH)

        first = [remote(k, j, (x, y), c, (*chip, c)) for k in range(n) for j, chip in enumerate(chips)]
        for cp in first:
            cp.start()
        passed = []
        for k in range(n):
            for j, chip in enumerate(chips):
                remote(k, j, chip, c, (x, y, c)).wait_recv()
                passed.append(remote(k, 3 + j, chip, c, sibling))
                passed[-1].start()
        for k in range(n):
            for j, chip in enumerate(chips):
                remote(k, 3 + j, chip, 1 - c, (x, y, c)).wait_recv()
        for cp in first + passed:
            cp.wait_send()

    out_shape = [jax.ShapeDtypeStruct(s.shape, s.dtype) for s in bufs]
    return _comm_call("all_gather_weights", body, bufs, out_shape, (6 * n, 6 * n), aliases={k: k for k in range(n)})


_HBM = pl.BlockSpec(memory_space=pltpu.HBM)
_SEM = pl.BlockSpec(memory_space=pltpu.SEMAPHORE)
_ANY = pl.BlockSpec(memory_space=pl.ANY)


def _split_start(name, bufs, n_copies, make_copies, after):
    n = len(bufs)

    def body(*refs):
        send_sems, recv_sems, token = refs[n + 1], refs[n + 2], refs[2 * n + 3]
        for cp in make_copies(refs[:n], send_sems, recv_sems):
            cp.start()
        token[...] = jnp.zeros_like(token)

    res = pl.pallas_call(
        body, name=name, in_specs=[_HBM] * n + [_ANY],
        out_shape=(pltpu.SemaphoreType.DMA((n_copies,)), pltpu.SemaphoreType.DMA((n_copies,)),
                   *[pltpu.HBM(b.shape, b.dtype) for b in bufs], jax.ShapeDtypeStruct((8, LANES), F32)),
        out_specs=(_SEM, _SEM, *[_HBM] * n, pl.BlockSpec(memory_space=pltpu.VMEM)),
        input_output_aliases={i: 2 + i for i in range(n)},
        compiler_params=pltpu.CompilerParams(has_side_effects=pltpu.SideEffectType.DATAFLOW_SIDE_EFFECTING),
    )(*[pltpu.with_memory_space_constraint(b, pltpu.HBM) for b in bufs], after)
    return res[0], res[1], list(res[2:2 + n]), res[2 + n]


def _split_wait(name, send_sems, recv_sems, bufs, make_copies, after):
    n = len(bufs)

    def body(*refs):
        for cp in make_copies(refs[:n], refs[n], refs[n + 1]):
            cp.wait_send()
            cp.wait_recv()

    return list(pl.pallas_call(
        body, name=name, in_specs=[_HBM] * n + [_SEM, _SEM, _ANY],
        out_shape=tuple(pltpu.HBM(b.shape, b.dtype) for b in bufs), out_specs=tuple([_HBM] * n),
        input_output_aliases={i: i for i in range(n)},
        compiler_params=pltpu.CompilerParams(has_side_effects=pltpu.SideEffectType.DATAFLOW_SIDE_EFFECTING),
    )(*bufs, send_sems, recv_sems, after))


def _gather_copies(refs, send_sems, recv_sems):
    x, y, c, chips = _place()
    return [pltpu.make_async_remote_copy(
        src_ref=ref.at[:, 2 * x + y], dst_ref=ref.at[:, 2 * x + y], send_sem=send_sems.at[3 * k + j],
        recv_sem=recv_sems.at[3 * k + j], device_id=(*chip, c), device_id_type=MESH)
        for k, ref in enumerate(refs) for j, chip in enumerate(chips)]


def _exchange_copies(refs, send_sems, recv_sems):
    n = len(refs) // 2
    x, y, c, chips = _place()
    return [pltpu.make_async_remote_copy(
        src_ref=refs[k].at[:, 2 * chip[0] + chip[1]], dst_ref=refs[n + k].at[j], send_sem=send_sems.at[3 * k + j],
        recv_sem=recv_sems.at[3 * k + j], device_id=(*chip, c), device_id_type=MESH)
        for k in range(n) for j, chip in enumerate(chips)]


def _rs_swap_halves(grads):
    n = len(grads)

    def body(*refs):
        ins, outs = refs[:n], refs[n:2 * n]
        send_sems, recv_sems = refs[2 * n:]
        x, y, c, _ = _place()
        copies = [pltpu.make_async_remote_copy(
            src_ref=_rows(ins[k], 1 - c), dst_ref=outs[k], send_sem=send_sems.at[k],
            recv_sem=recv_sems.at[k], device_id=(x, y, 1 - c), device_id_type=MESH) for k in range(n)]
        for cp in copies:
            cp.start()
        for cp in copies:
            cp.wait()

    out_shape = [jax.ShapeDtypeStruct(g.shape[:2] + (g.shape[2] // 2, g.shape[3]), g.dtype) for g in grads]
    return _comm_call("rs_swap_halves", body, grads, out_shape, (n, n))


def _rs_exchange(parts):
    n = len(parts)

    def body(*refs):
        ins, outs = refs[:n], refs[n:2 * n]
        send_sems, recv_sems = refs[2 * n:]
        x, y, c, chips = _place()
        copies = [pltpu.make_async_remote_copy(
            src_ref=ins[k].at[:, 2 * chip[0] + chip[1]], dst_ref=outs[k].at[j], send_sem=send_sems.at[3 * k + j],
            recv_sem=recv_sems.at[3 * k + j], device_id=(*chip, c), device_id_type=MESH)
            for k in range(n) for j, chip in enumerate(chips)]
        for cp in copies:
            cp.start()
        for cp in copies:
            cp.wait()

    out_shape = [jax.ShapeDtypeStruct((3, p.shape[0]) + p.shape[2:], p.dtype) for p in parts]
    return _comm_call("rs_exchange", body, parts, out_shape, (3 * n, 3 * n))


def _rs_share(bufs):
    n = len(bufs)

    def body(*refs):
        outs = refs[n:2 * n]
        send_sems, recv_sems = refs[2 * n:]
        x, y, c, _ = _place()

        def half(k, which):
            region = _rows(outs[k], which)
            return pltpu.make_async_remote_copy(
                src_ref=region, dst_ref=region, send_sem=send_sems.at[k], recv_sem=recv_sems.at[k],
                device_id=(x, y, 1 - c), device_id_type=MESH)

        sends = [half(k, c) for k in range(n)]
        for cp in sends:
            cp.start()
        for k in range(n):
            half(k, 1 - c).wait_recv()
        for cp in sends:
            cp.wait_send()

    out_shape = [jax.ShapeDtypeStruct(h.shape, h.dtype) for h in bufs]
    return _comm_call("rs_share", body, bufs, out_shape, (n, n), aliases={k: k for k in range(n)})


def _all_reduce_small(v):
    n = v.shape[0]

    def body(v_ref, o_ref, gbuf, send_sems, recv_sems):
        x, y, c, _ = _place()
        me = 4 * x + 2 * y + c
        gbuf[me] = v_ref[...]
        copies, waits = [], []
        for r in range(1, N_DEV):
            px = 1 - x if r & 4 else x
            py = 1 - y if r & 2 else y
            pc = 1 - c if r & 1 else c
            mk = functools.partial(pltpu.make_async_remote_copy, src_ref=v_ref, send_sem=send_sems.at[r - 1],
                                   recv_sem=recv_sems.at[r - 1], device_id=(px, py, pc), device_id_type=MESH)
            copies.append(mk(dst_ref=gbuf.at[me]))
            waits.append(mk(dst_ref=gbuf.at[4 * px + 2 * py + pc]))
        for cp in copies:
            cp.start()
        for cp in waits:
            cp.wait_recv()
        for cp in copies:
            cp.wait_send()
        acc = gbuf[0]
        for d in range(1, N_DEV):
            acc = acc + gbuf[d]
        o_ref[...] = acc

    vm = pl.BlockSpec(memory_space=pltpu.VMEM)
    return pl.pallas_call(
        body, name="all_reduce_small", in_specs=[vm], out_specs=vm, out_shape=jax.ShapeDtypeStruct(v.shape, F32),
        scratch_shapes=[pltpu.VMEM((N_DEV, n, LANES), F32), pltpu.SemaphoreType.DMA((N_DEV - 1,)),
                        pltpu.SemaphoreType.DMA((N_DEV - 1,))],
        compiler_params=pltpu.CompilerParams(vmem_limit_bytes=VMEM_LIMIT),
    )(v)


def _add_half(g, h1):
    L, nb, hr, C = h1.shape
    g3, h3 = g.reshape(L * nb, 2 * hr, C), h1.reshape(L * nb, hr, C)
    tr = _tile(hr, 512)

    def kern(g_ref, h_ref, o_ref):
        o_ref[...] = (g_ref[...].astype(F32) + h_ref[...].astype(F32)).astype(BF16)

    blk = pl.BlockSpec((None, tr, C), lambda l, i: (l, i, 0))
    out = pl.pallas_call(
        kern, name="rs_add_half", grid=(L * nb, hr // tr),
        in_specs=[pl.BlockSpec((None, tr, C), lambda l, i: (l, lax.axis_index("c") * (hr // tr) + i, 0)), blk],
        out_specs=blk, out_shape=jax.ShapeDtypeStruct(h3.shape, BF16), compiler_params=_params(2),
    )(g3, h3)
    return out.reshape(h1.shape)


def _add_blocks(p, h2, half=True):
    L, nb, hr, C = p.shape
    tr = _tile(hr, 512)
    shift = lambda: lax.axis_index("c") * (hr // tr) if half else 0

    def kern(p_ref, h0_ref, h1_ref, h2_ref, o_ref):
        o_ref[...] = ((p_ref[...].astype(F32) + h0_ref[...].astype(F32)) + h1_ref[...].astype(F32)) + h2_ref[...].astype(F32)

    def other(j):
        return pl.BlockSpec((None, None, tr, C), lambda l, i: (j, l, i, 0))

    return pl.pallas_call(
        kern, name="rs_add_blocks", grid=(L, hr // tr),
        in_specs=[pl.BlockSpec((None, None, tr, C), lambda l, i: (l, _my_block(), i, 0)), other(0), other(1), other(2)],
        out_specs=pl.BlockSpec((None, tr, C), lambda l, i: (l, shift() + i, 0)),
        out_shape=jax.ShapeDtypeStruct((L, (2 if half else 1) * hr, C), F32), compiler_params=_params(2),
    )(p, h2, h2, h2)


def _d2d_swap(arrays):
    n = len(arrays)

    def body(*refs):
        ins, outs = refs[:n], refs[n:2 * n]
        send_sems, recv_sems = refs[2 * n:]
        x, y, c, _ = _place()
        copies = [pltpu.make_async_remote_copy(
            src_ref=ins[k], dst_ref=outs[k], send_sem=send_sems.at[k], recv_sem=recv_sems.at[k],
            device_id=(x, y, 1 - c), device_id_type=MESH) for k in range(n)]
        for cp in copies:
            cp.start()
        for cp in copies:
            cp.wait()

    return _comm_call("d2d_swap", body, arrays, [jax.ShapeDtypeStruct(a.shape, a.dtype) for a in arrays], (n, n))


def _reduce_scatter(grads):
    sib = _rs_swap_halves(grads)
    parts = [_add_half(g, h) for g, h in zip(grads, sib)]
    others = _rs_exchange(parts)
    return _rs_share([_add_blocks(p, o) for p, o in zip(parts, others)])


WEIGHTS = ("norm_ffn1", "w_ffn1_in", "w_ffn1_out", "norm_mix", "w_mix_in", "b_forget", "w_pool", "pool_scale",
           "conv_w", "w_mix_out", "norm_ffn2", "w_ffn2_in", "w_ffn2_out", "norm_final")
BIG = ("w_ffn1_in", "w_ffn1_out", "w_mix_in", "w_mix_out", "w_ffn2_in", "w_ffn2_out")
SMALL = ("norm_ffn1", "norm_mix", "b_forget", "w_pool", "pool_scale", "conv_w", "norm_ffn2", "norm_final")


def _prep_weights(small, gathered, conv_w, D, first):
    DA, C, H = D // 2, D // 4, D // 2 // HEAD_DIM
    L = gathered["w_mix_in"].shape[0]
    small = {k: val[first:first + L] for k, val in small.items() if k != "norm_final"}
    gathered = dict(gathered, conv_w=conv_w[first:first + L])
    w_in = jnp.concatenate([gathered["w_mix_in"][:, b] for b in range(N_CHIPS)], axis=2)
    wqkv, wrest = w_in[:, :, :3 * DA], w_in[:, :, 3 * DA + H:]
    wf = jnp.pad(w_in[:, :, 3 * DA:3 * DA + H], ((0, 0), (0, 0), (0, LANES - H)))
    ng = len(POOL_WINDOWS)
    same_group = jnp.eye(ng, dtype=bool)[None, :, None, :, None]
    wbd = jnp.where(same_group, small["w_pool"][:, :, :, None, :], 0.0).reshape(L, C, C)
    cw = jnp.concatenate([gathered["conv_w"][:, b] for b in range(N_CHIPS)], axis=2)
    return dict(
        g1=small["norm_ffn1"], gm=small["norm_mix"], g2=small["norm_ffn2"],
        w1in=gathered["w_ffn1_in"], w1out=gathered["w_ffn1_out"].reshape(L, -1, D),
        wp=jnp.concatenate([wqkv, wrest, wf], axis=2), wmixout=gathered["w_mix_out"].reshape(L, D, D),
        bias=jnp.pad(small["b_forget"], ((0, 0), (0, LANES - H))), wbd=wbd.astype(BF16), ps=small["pool_scale"],
        cw=jnp.pad(cw, ((0, 0), (0, 8 - CONV_WIDTH), (0, 0))),
    )


def _layer_params(l, W):
    P = {k: (W[k], l) for k in ("w1in", "w1out", "wp", "wmixout")}
    P.update({k: W[k][l][None] for k in ("g1", "gm", "g2", "bias", "ps")})
    P.update(wbd=W["wbd"][l], cw=W["cw"][l])
    return P


def _ffn_fwd(x, g, w_in, w_out, token=None):
    h, jac, act = _ffn_up(x, g, w_in, token)
    return _ffn_out(act, w_out, x)[0], (x, h, jac, act)


def _ffn_bwd(dres, saved, g, w_in, w_out, token=None):
    x, h, jac, act = saved
    dgu, dx, dg = _ffn_bwd_main(dres, jac, x, g, w_out, w_in, token)
    dw_out = _ffn_dw_out(act, dres)[0]
    dw_in = _ffn_dw_in(h, dgu)[0]
    return dx, dg, dw_in, dw_out.reshape(N_CHIPS, -1, dw_out.shape[1])


def _mixer_fwd(x, P, B, S, tq):
    T, D = x.shape
    DA, C, H = D // 2, D // 4, D // 2 // HEAD_DIM
    hn, qkv, rest, fl = _mix_up(x, P["gm"], P["wp"], (3 * DA, 4 * C))
    qkv, rest, fl = qkv.reshape(B, S, 3 * DA), rest.reshape(B, S, 4 * C), fl.reshape(B, S, LANES)
    drow = _decay_fwd(fl, P["bias"]).reshape(B, 8, S // tq, tq)
    o, lse = _attn_fwd(qkv, drow, H, tq)
    ypc, pooled = _mix_local_fwd(rest, P["wbd"], P["ps"], P["cw"])
    x_out = _mix_out([o.reshape(T, DA), ypc.reshape(T, 2 * C)], P["wmixout"], x)
    return x_out, (x, hn, qkv, rest, fl, drow, o, lse, pooled, ypc)


def _mixer_bwd(dres, saved, P, B, S, tq):
    x, hn, qkv, rest, fl, drow, o, lse, pooled, ypc = saved
    T, D = x.shape
    DA, C, H = D // 2, D // 4, D // 2 // HEAD_DIM
    dycat = _proj("mix_out_bwd", dres, P["wmixout"], F32, NT).reshape(B, S, D)
    dw_out = _rows_dw("mix_out_dw", [o.reshape(T, DA), ypc.reshape(T, 2 * C)], dres, BF16)
    dq, dk, dv, ddrow, ddcol = _attn_bwd(qkv, drow, o, lse, dycat, H, tq)
    dfl, dbias = _decay_bwd(ddrow.reshape(B, 8, S), ddcol, fl, P["bias"], H)
    drest, dwbd, dps, dcw = _mix_local_bwd(rest, pooled, dycat, P["wbd"], P["ps"], P["cw"])
    pieces = [a.reshape(T, a.shape[-1]) for a in (dq, dk, dv, drest, dfl)]
    dwp = _pieces_dw("mix_in_dw", hn, pieces, F32, 512)
    dx, dg = _mix_in_bwd(pieces, x, P["gm"], dres, P["wp"])
    n_q, n_r = 3 * DA, 4 * C
    dw_in = jnp.concatenate([dwp[:, :n_q], dwp[:, n_q + n_r:n_q + n_r + H], dwp[:, n_q:n_q + n_r]], axis=1)
    dw_in = dw_in.reshape(D, N_CHIPS, -1).transpose(1, 0, 2).astype(BF16)
    ng = len(POOL_WINDOWS)
    same_group = jnp.eye(ng, dtype=bool)[:, None, :, None]
    dw_pool = jnp.where(same_group, dwbd.reshape(ng, C // ng, ng, C // ng), 0.0).sum(axis=2)
    small = dict(norm_mix=dg[0], b_forget=dbias[0, :H], w_pool=dw_pool, pool_scale=dps[0], conv_w=dcw[:CONV_WIDTH])
    return dx, small, dw_in, dw_out.reshape(N_CHIPS, -1, D)


EARLY = ("w_ffn1_in", "w_ffn1_out", "w_mix_in", "w_mix_out")
FFN2 = ("w_ffn2_in", "w_ffn2_out")


def _local_step(x, target, small, conv_w, pipe):
    B, S, D = x.shape
    L = small["norm_ffn1"].shape[0]
    tq = _tile(S, 256)
    xt = x.reshape(B * S, D)
    saved, params = [], []
    for l in range(L):
        P = _layer_params(0, _prep_weights(small, pipe.weights(l, xt), conv_w, D, l))
        xt, s1 = _ffn_fwd(xt, P["g1"], P["w1in"], P["w1out"], pipe.token(l))
        xt, s2 = _mixer_fwd(xt, P, B, S, tq)
        w2 = pipe.weights_ffn2(l, xt)
        P.update(w2in=(w2["w_ffn2_in"], 0), w2out=(w2["w_ffn2_out"].reshape(1, -1, D), 0))
        xt, s3 = _ffn_fwd(xt, P["g2"], P["w2in"], P["w2out"])
        saved.append((s1, s2, s3))
        params.append(P)
    dres, dgf, loss = _final_loss(xt, small["norm_final"][None], target.reshape(B * S, D))
    sm = {k: [None] * L for k in SMALL if k != "norm_final"}
    token = None
    for l in reversed(range(L)):
        P, (s1, s2, s3) = params[l], saved[l]
        big = {}
        dres, dg2, big["w_ffn2_in"], big["w_ffn2_out"] = _ffn_bwd(dres, s3, P["g2"], P["w2in"], P["w2out"], token)
        dres, smix, big["w_mix_in"], big["w_mix_out"] = _mixer_bwd(dres, s2, P, B, S, tq)
        big = {k: val[None] for k, val in big.items()}
        token = pipe.grads(l, FFN2 + EARLY[2:], big, dres) if l == 0 else None
        dres, dg1, dw_in, dw_out = _ffn_bwd(dres, s1, P["g1"], P["w1in"], P["w1out"], token)
        big.update(w_ffn1_in=dw_in[None], w_ffn1_out=dw_out[None])
        sm["norm_ffn1"][l], sm["norm_ffn2"][l] = dg1[0], dg2[0]
        for k, val in smix.items():
            sm[k][l] = val
        token = pipe.grads(l, EARLY[:2] if l == 0 else BIG, big, big["w_ffn1_in"])
    sm = {k: jnp.stack(val) for k, val in sm.items()}
    sm["norm_final"] = dgf[0]
    return loss[0, 0], dres.reshape(B, S, D), sm


def _sibling_copies(refs, send_sems, recv_sems):
    n = len(refs) // 2
    x, y, c, _ = _place()
    return [pltpu.make_async_remote_copy(
        src_ref=refs[k], dst_ref=refs[n + k], send_sem=send_sems.at[k], recv_sem=recv_sems.at[k],
        device_id=(x, y, 1 - c), device_id_type=MESH) for k in range(n)]


class _Pipeline:
    def __init__(self, w):
        self.w, self.n_layers = w, w[BIG[0]].shape[0]
        first = _all_gather([_place_shard(w[k], BF16, 0, 1) for k in EARLY] + [_place_shard(w["conv_w"], F32)])
        self.conv_w = first[-1]
        self._ready = dict(zip(EARLY, first[:-1]))
        self._ffn2 = self._start_gather("0b", FFN2, 0, first[0])
        self._next = (1, self._start_gather("1", BIG, 1, self._ffn2[1][3]))
        self._reduce, self._swaps = None, []
        self.reduced = [dict() for _ in range(self.n_layers)]

    def _start_gather(self, tag, kinds, l, after):
        placed = [_place_shard(self.w[k], BF16, l, 1) for k in kinds]
        return kinds, _split_start(f"gather_start_{tag}", placed, 3 * len(kinds), _gather_copies, after)

    def _wait_gather(self, tag, started, after):
        kinds, (send_sems, recv_sems, bufs, _) = started
        return dict(zip(kinds, _split_wait(f"gather_wait_{tag}", send_sems, recv_sems, bufs, _gather_copies, after)))

    def token(self, l):
        return self._next[1][1][3] if self._next is not None and self._next[0] == l + 1 else None

    def weights(self, l, after):
        if l == 0:
            return self._ready
        self._layer = self._wait_gather(str(l), self._next[1], after)
        first = next(iter(self._layer.values()))
        self._next = (l + 1, self._start_gather(str(l + 1), BIG, l + 1, first)) if l + 1 < self.n_layers else None
        return self._layer

    def weights_ffn2(self, l, after):
        return self._wait_gather("0b", self._ffn2, after) if l == 0 else self._layer

    def _finish_reduce(self, after):
        if self._reduce is None:
            return None
        tag, l, kinds, (send_sems, recv_sems, bufs, _) = self._reduce
        n = len(kinds)
        bufs = _split_wait(f"reduce_wait_{tag}", send_sems, recv_sems, bufs, _exchange_copies, after)
        mine = [_add_blocks(p, o, half=False) for p, o in zip(bufs[:n], bufs[n:])]
        lands = [lax.empty(q.shape, q.dtype) for q in mine]
        self._swaps.append((tag, l, kinds, _split_start(f"swap_start_{tag}", mine + lands, n, _sibling_copies, mine[0])))
        self._reduce = None
        return self._swaps[-1][3][3]

    def grads(self, l, kinds, big, after):
        swap_token = self._finish_reduce(after)
        grads = [big[k] for k in kinds]
        if l == 0 and kinds[0] == EARLY[0]:
            self.reduced[0].update(zip(kinds, _reduce_scatter(grads)))
            return None
        tag = f"{l}{'b' if len(kinds) < len(BIG) else ''}"
        lands = [lax.empty((3, g.shape[0]) + g.shape[2:], g.dtype) for g in grads]
        started = _split_start(f"reduce_start_{tag}", grads + lands, 3 * len(kinds), _exchange_copies,
                               grads[0] if swap_token is None else swap_token)
        self._reduce = (tag, l, kinds, started)
        return started[3]

    def finish(self, after):
        self._finish_reduce(after)
        for tag, l, kinds, (send_sems, recv_sems, bufs, _) in self._swaps:
            n = len(kinds)
            bufs = _split_wait(f"swap_wait_{tag}", send_sems, recv_sems, bufs, _sibling_copies, after)
            self.reduced[l].update(zip(kinds, zip(bufs[:n], bufs[n:])))
        return self.reduced


def _pack(parts, extra=()):
    flat = jnp.concatenate([p.reshape(-1) for p in parts] + [jnp.reshape(e, (1,)) for e in extra])
    n = -(-flat.shape[0] // (8 * LANES)) * 8
    return jnp.pad(flat, (0, n * LANES - flat.shape[0])).reshape(n, LANES)


def _unpack(buf, shapes):
    flat, out, at = buf.reshape(-1), [], 0
    for s in shapes:
        n = math.prod(s)
        out.append(flat[at:at + n].reshape(s))
        at += n
    return out, flat[at:]


def kernel(x, norm_ffn1, w_ffn1_in, w_ffn1_out, norm_mix, w_mix_in, b_forget, w_pool, pool_scale, conv_w, w_mix_out, norm_ffn2, w_ffn2_in, w_ffn2_out, norm_final, loss_target, m_norm_ffn1, m_w_ffn1_in, m_w_ffn1_out, m_norm_mix, m_w_mix_in, m_b_forget, m_w_pool, m_pool_scale, m_conv_w, m_w_mix_out, m_norm_ffn2, m_w_ffn2_in, m_w_ffn2_out, m_norm_final, v_norm_ffn1, v_w_ffn1_in, v_w_ffn1_out, v_norm_mix, v_w_mix_in, v_b_forget, v_w_pool, v_pool_scale, v_conv_w, v_w_mix_out, v_norm_ffn2, v_w_ffn2_in, v_w_ffn2_out, v_norm_final):
    w = dict(zip(WEIGHTS, (norm_ffn1, w_ffn1_in, w_ffn1_out, norm_mix, w_mix_in, b_forget, w_pool, pool_scale, conv_w, w_mix_out, norm_ffn2, w_ffn2_in, w_ffn2_out, norm_final)))
    m = dict(zip(WEIGHTS, (m_norm_ffn1, m_w_ffn1_in, m_w_ffn1_out, m_norm_mix, m_w_mix_in, m_b_forget, m_w_pool, m_pool_scale, m_conv_w, m_w_mix_out, m_norm_ffn2, m_w_ffn2_in, m_w_ffn2_out, m_norm_final)))
    v = dict(zip(WEIGHTS, (v_norm_ffn1, v_w_ffn1_in, v_w_ffn1_out, v_norm_mix, v_w_mix_in, v_b_forget, v_w_pool, v_pool_scale, v_conv_w, v_w_mix_out, v_norm_ffn2, v_w_ffn2_in, v_w_ffn2_out, v_norm_final)))
    block = 2 * lax.axis_index("x") + lax.axis_index("y")

    pipe = _Pipeline(w)
    small = {k: w[k] for k in SMALL}
    loss, grad_x, sm = _local_step(x, loss_target, small, pipe.conv_w, pipe)
    reduced = pipe.finish(grad_x)
    grads, order = {}, list(SMALL)
    total = _all_reduce_small(_pack([sm[k] for k in order], extra=(loss,)))
    parts, rest = _unpack(total, [sm[k].shape for k in order])
    grads.update(zip(order, parts))
    loss = rest[0]
    cs = conv_w.shape[2]
    grads["conv_w"] = lax.dynamic_slice_in_dim(grads["conv_w"], block * cs, cs, axis=2)

    delta, new_m, new_v = {}, {}, {}
    for k in BIG:
        two_d = lambda a: a.reshape(-1, a.shape[-1])
        pieces = [tuple(map(two_d, g)) if isinstance(g, tuple) else two_d(g) for g in (layer[k] for layer in reduced)]
        res = _adamw(two_d(w[k]), pieces, two_d(m[k]), two_d(v[k]))
        grads[k], delta[k], new_m[k], new_v[k] = [r.reshape(w[k].shape) for r in res]
    packed = [_pack([t[k] for k in order]) for t in (w, grads, m, v)]
    _, d, nm, nv = _adamw(packed[0], [packed[1]], packed[2], packed[3])
    shapes = [w[k].shape for k in order]
    for res, flat in ((delta, d), (new_m, nm), (new_v, nv)):
        res.update(zip(order, _unpack(flat, shapes)[0]))
    return (loss, grad_x, *[grads[k] for k in WEIGHTS], *[delta[k] for k in WEIGHTS],
            *[new_m[k] for k in WEIGHTS], *[new_v[k] for k in WEIGHTS])
```

```python
import functools
import math

import jax
import jax.numpy as jnp
from jax import lax
from jax.experimental import pallas as pl
from jax.experimental.pallas import tpu as pltpu

F32 = jnp.float32
BF16 = jnp.bfloat16
MESH = pl.DeviceIdType.MESH

HEAD_DIM = 64
POOL_WINDOWS = (2, 4, 8, 16)
CONV_WIDTH = 3
RMS_EPS = 1e-6
ADAM_LR = 0.001
ADAM_B1 = 0.9
ADAM_B2 = 0.999
ADAM_EPS = 1e-08
ADAM_WD = 0.01
ADAM_STEP = 10

LANES = 128
VMEM_LIMIT = 56 * 1024 * 1024
N_CHIPS = 4
N_DEV = 8

NN = (((1,), (0,)), ((), ()))
NT = (((1,), (1,)), ((), ()))
TN = (((0,), (0,)), ((), ()))


def _tile(n, pref):
    for t in range(pref - pref % 16, 15, -16):
        if n % t == 0:
            return t
    return n


def _params(n_grid):
    return pltpu.CompilerParams(dimension_semantics=("arbitrary",) * n_grid, vmem_limit_bytes=VMEM_LIMIT)


def _dot(a, b, dims):
    return lax.dot_general(a, b, dims, preferred_element_type=F32)


def _mm(name, dims, operands, in_specs, out_shape, out_specs, grid, acc_shape, epilogue):
    n_in, n_out, nk = len(operands), len(out_shape), grid[-1]

    def kern(*refs):
        extras, outs = refs[2:n_in], refs[n_in:n_in + n_out]
        if nk == 1:
            epilogue(_dot(refs[0][...].astype(BF16), refs[1][...].astype(BF16), dims), extras, outs)
            return
        acc = refs[n_in + n_out]
        k = pl.program_id(len(grid) - 1)

        @pl.when(k == 0)
        def _():
            acc[...] = jnp.zeros_like(acc)

        acc[...] += _dot(refs[0][...].astype(BF16), refs[1][...].astype(BF16), dims)

        @pl.when(k == nk - 1)
        def _():
            epilogue(acc[...], extras, outs)

    return pl.pallas_call(
        kern, name=name, grid=grid, in_specs=in_specs, out_specs=out_specs, out_shape=out_shape,
        scratch_shapes=[pltpu.VMEM(acc_shape, F32)] if nk > 1 else [],
        compiler_params=_params(len(grid)),
    )(*operands)


def _store(scale=None, dtype=None):
    def ep(acc, extras, outs):
        v = acc if scale is None else acc * scale
        outs[0][...] = v.astype(outs[0].dtype)
    return ep


def _residual(scale):
    def ep(acc, extras, outs):
        outs[0][...] = extras[0][...] + scale * acc
    return ep


def _rmsnorm_fwd(x, g):
    T, D = x.shape
    tr = _tile(T, 512)

    def kern(x_ref, g_ref, o_ref):
        xv = x_ref[...]
        r = lax.rsqrt(jnp.mean(xv * xv, axis=-1, keepdims=True) + RMS_EPS)
        o_ref[...] = (xv * r * g_ref[...]).astype(BF16)

    return pl.pallas_call(
        kern, name="rmsnorm_fwd", grid=(T // tr,),
        in_specs=[pl.BlockSpec((tr, D), lambda i: (i, 0)), pl.BlockSpec((1, D), lambda i: (0, 0))],
        out_specs=pl.BlockSpec((tr, D), lambda i: (i, 0)),
        out_shape=jax.ShapeDtypeStruct((T, D), BF16), compiler_params=_params(1),
    )(x, g)


def _rmsnorm_bwd(x, g, dh, dres):
    T, D = x.shape
    tr = _tile(T, 256)

    def kern(x_ref, g_ref, dh_ref, dres_ref, dx_ref, dg_ref):
        xv, dhv = x_ref[...], dh_ref[...]
        r = lax.rsqrt(jnp.mean(xv * xv, axis=-1, keepdims=True) + RMS_EPS)
        y = xv * r
        dy = dhv * g_ref[...]
        dx_ref[...] = dres_ref[...] + r * (dy - y * jnp.mean(dy * y, axis=-1, keepdims=True))
        part = jnp.sum(dhv * y, axis=0, keepdims=True)

        @pl.when(pl.program_id(0) == 0)
        def _():
            dg_ref[...] = part

        @pl.when(pl.program_id(0) > 0)
        def _():
            dg_ref[...] += part

    row = pl.BlockSpec((tr, D), lambda i: (i, 0))
    vec = pl.BlockSpec((1, D), lambda i: (0, 0))
    return pl.pallas_call(
        kern, name="rmsnorm_bwd", grid=(T // tr,), in_specs=[row, vec, row, row], out_specs=[row, vec],
        out_shape=[jax.ShapeDtypeStruct((T, D), F32), jax.ShapeDtypeStruct((1, D), F32)],
        compiler_params=_params(1),
    )(x, g, dh, dres)


def _final_loss(x, g, target):
    T, D = x.shape
    tr = _tile(T, 256)

    def kern(x_ref, g_ref, t_ref, dx_ref, dg_ref, loss_ref):
        xv = x_ref[...]
        r = lax.rsqrt(jnp.mean(xv * xv, axis=-1, keepdims=True) + RMS_EPS)
        y = xv * r
        err = y * g_ref[...] - t_ref[...]
        lpart = 0.5 * jnp.sum(jnp.mean(err * err, axis=-1, keepdims=True), axis=0, keepdims=True)
        dh = err * (1.0 / D)
        dy = dh * g_ref[...]
        dx_ref[...] = r * (dy - y * jnp.mean(dy * y, axis=-1, keepdims=True))
        part = jnp.sum(dh * y, axis=0, keepdims=True)
        lrow = jnp.broadcast_to(lpart, (1, LANES))

        @pl.when(pl.program_id(0) == 0)
        def _():
            dg_ref[...] = part
            loss_ref[...] = lrow

        @pl.when(pl.program_id(0) > 0)
        def _():
            dg_ref[...] += part
            loss_ref[...] += lrow

    row = pl.BlockSpec((tr, D), lambda i: (i, 0))
    vec = pl.BlockSpec((1, D), lambda i: (0, 0))
    return pl.pallas_call(
        kern, name="final_loss", grid=(T // tr,), in_specs=[row, vec, row],
        out_specs=[row, vec, pl.BlockSpec((1, LANES), lambda i: (0, 0))],
        out_shape=[jax.ShapeDtypeStruct((T, D), F32), jax.ShapeDtypeStruct((1, D), F32),
                   jax.ShapeDtypeStruct((1, LANES), F32)],
        compiler_params=_params(1),
    )(x, g, target)


def _ffn_in(h, w4):
    T, D = h.shape
    w4, l = w4
    Fh = w4.shape[3]
    F = 2 * Fh
    tm = _tile(T, 512)

    def kern(h_ref, wg_ref, wu_ref, jac_ref, act_ref):
        hv = h_ref[...]
        gate = _dot(hv, wg_ref[...], NN)
        up = _dot(hv, wu_ref[...], NN)
        sg = jax.nn.sigmoid(gate)
        silu = gate * sg
        jac_ref[0] = (up * (sg + silu * (1.0 - sg))).astype(BF16)
        jac_ref[1] = silu.astype(BF16)
        act_ref[...] = (silu * up).astype(BF16)

    return pl.pallas_call(
        kern, name="ffn_in", grid=(2, T // tm),
        in_specs=[pl.BlockSpec((tm, D), lambda j, i: (i, 0)),
                  pl.BlockSpec((None, None, D, Fh), lambda j, i: (l, j, 0, 0)),
                  pl.BlockSpec((None, None, D, Fh), lambda j, i: (l, 2 + j, 0, 0))],
        out_specs=[pl.BlockSpec((2, tm, Fh), lambda j, i: (0, i, j)),
                   pl.BlockSpec((tm, Fh), lambda j, i: (i, j))],
        out_shape=[jax.ShapeDtypeStruct((2, T, F), BF16), jax.ShapeDtypeStruct((T, F), BF16)],
        compiler_params=_params(2),
    )(h, w4, w4)


def _resident(shape, index_map):
    return pl.BlockSpec(shape, index_map, pipeline_mode=pl.Buffered(1))


def _token_operand(token):
    return ([], []) if token is None else ([token], [pl.BlockSpec(token.shape, lambda i: (0, 0))])


def _ffn_up(x, g, w4, token=None):
    T, D = x.shape
    w4, l = w4
    Fh = w4.shape[3]
    F = 2 * Fh
    tm = _tile(T, 512)
    tok_ops, tok_specs = _token_operand(token)

    def kern(x_ref, g_ref, w_ref, *rest):
        h_ref, jac_ref, act_ref = rest[len(tok_ops):]
        xv = x_ref[...]
        r = lax.rsqrt(jnp.mean(xv * xv, axis=-1, keepdims=True) + RMS_EPS)
        hv = (xv * r * g_ref[...]).astype(BF16)
        h_ref[...] = hv
        for j in range(2):
            cols = slice(j * Fh, (j + 1) * Fh)
            gate = _dot(hv, w_ref[j], NN)
            up = _dot(hv, w_ref[2 + j], NN)
            sg = jax.nn.sigmoid(gate)
            silu = gate * sg
            jac_ref[0, :, cols] = (up * (sg + silu * (1.0 - sg))).astype(BF16)
            jac_ref[1, :, cols] = silu.astype(BF16)
            act_ref[:, cols] = (silu * up).astype(BF16)

    return pl.pallas_call(
        kern, name="ffn_up", grid=(T // tm,),
        in_specs=[pl.BlockSpec((tm, D), lambda i: (i, 0)), pl.BlockSpec((1, D), lambda i: (0, 0)),
                  _resident((None, 4, D, Fh), lambda i: (l, 0, 0, 0))] + tok_specs,
        out_specs=[pl.BlockSpec((tm, D), lambda i: (i, 0)), pl.BlockSpec((2, tm, F), lambda i: (0, i, 0)),
                   pl.BlockSpec((tm, F), lambda i: (i, 0))],
        out_shape=[jax.ShapeDtypeStruct((T, D), BF16), jax.ShapeDtypeStruct((2, T, F), BF16),
                   jax.ShapeDtypeStruct((T, F), BF16)],
        compiler_params=_params(1),
    )(x, g, w4, *tok_ops)


def _ffn_bwd_main(dres, jac, x, g, w_out, w4, token=None):
    T, D = dres.shape
    w_out, l = w_out
    w4, _ = w4
    F = w_out.shape[1]
    Fh = F // 2
    tm = _tile(T, 256)
    tok_ops, tok_specs = _token_operand(token)

    def kern(d_ref, jac_ref, x_ref, g_ref, wo_ref, wi_ref, *rest):
        dgu_ref, dx_ref, dg_ref = rest[len(tok_ops):]
        dv = d_ref[...]
        d16 = dv.astype(BF16)
        dh = jnp.zeros((tm, D), F32)
        for j in range(2):
            cols = slice(j * Fh, (j + 1) * Fh)
            dact = 0.5 * _dot(d16, wo_ref[cols, :], NT)
            dgate = (dact * jac_ref[0, :, cols].astype(F32)).astype(BF16)
            dup = (dact * jac_ref[1, :, cols].astype(F32)).astype(BF16)
            dgu_ref[0, :, cols] = dgate
            dgu_ref[1, :, cols] = dup
            dh = dh + _dot(dgate, wi_ref[j], NT) + _dot(dup, wi_ref[2 + j], NT)
        xv = x_ref[...]
        r = lax.rsqrt(jnp.mean(xv * xv, axis=-1, keepdims=True) + RMS_EPS)
        y = xv * r
        dy = dh * g_ref[...]
        dx_ref[...] = dv + r * (dy - y * jnp.mean(dy * y, axis=-1, keepdims=True))
        part = jnp.sum(dh * y, axis=0, keepdims=True)

        @pl.when(pl.program_id(0) == 0)
        def _():
            dg_ref[...] = part

        @pl.when(pl.program_id(0) > 0)
        def _():
            dg_ref[...] += part

    row = pl.BlockSpec((tm, D), lambda i: (i, 0))
    vec = pl.BlockSpec((1, D), lambda i: (0, 0))
    wide = pl.BlockSpec((2, tm, F), lambda i: (0, i, 0))
    return pl.pallas_call(
        kern, name="ffn_bwd_main", grid=(T // tm,),
        in_specs=[row, wide, row, vec, _resident((None, F, D), lambda i: (l, 0, 0)),
                  _resident((None, 4, D, Fh), lambda i: (l, 0, 0, 0))] + tok_specs,
        out_specs=[wide, row, vec],
        out_shape=[jax.ShapeDtypeStruct((2, T, F), BF16), jax.ShapeDtypeStruct((T, D), F32),
                   jax.ShapeDtypeStruct((1, D), F32)],
        compiler_params=_params(1),
    )(dres, jac, x, g, w_out, w4, *tok_ops)


def _ffn_out(act, w_out, x):
    T, F = act.shape
    w_out, l = w_out
    D = w_out.shape[2]
    tm = _tile(T, 512)
    return _mm("ffn_out", NN, [act, w_out, x],
               [pl.BlockSpec((tm, F), lambda i, k: (i, 0)), pl.BlockSpec((None, F, D), lambda i, k: (l, 0, 0)),
                pl.BlockSpec((tm, D), lambda i, k: (i, 0))],
               [jax.ShapeDtypeStruct((T, D), F32)], [pl.BlockSpec((tm, D), lambda i, k: (i, 0))],
               (T // tm, 1), None, _residual(0.5))


def _ffn_bwd_act(dres, w_out, jac):
    T, D = dres.shape
    w_out, l = w_out
    F = w_out.shape[1]
    Fh = F // 2
    tm = _tile(T, 512)

    def kern(d_ref, w_ref, jac_ref, o_ref):
        dact = 0.5 * _dot(d_ref[...].astype(BF16), w_ref[...], NT)
        o_ref[0] = (dact * jac_ref[0].astype(F32)).astype(BF16)
        o_ref[1] = (dact * jac_ref[1].astype(F32)).astype(BF16)

    return pl.pallas_call(
        kern, name="ffn_bwd_act", grid=(2, T // tm),
        in_specs=[pl.BlockSpec((tm, D), lambda j, i: (i, 0)), pl.BlockSpec((None, Fh, D), lambda j, i: (l, j, 0)),
                  pl.BlockSpec((2, tm, Fh), lambda j, i: (0, i, j))],
        out_specs=pl.BlockSpec((2, tm, Fh), lambda j, i: (0, i, j)),
        out_shape=jax.ShapeDtypeStruct((2, T, F), BF16), compiler_params=_params(2),
    )(dres, w_out, jac)


def _ffn_dw_out(act, dres):
    T, F = act.shape
    D = dres.shape[1]
    tm, tk = F // 2, _tile(T, 1024)
    return _mm("ffn_dw_out", TN, [act, dres],
               [pl.BlockSpec((tk, tm), lambda i, k: (k, i)), pl.BlockSpec((tk, D), lambda i, k: (k, 0))],
               [jax.ShapeDtypeStruct((F, D), BF16)], [pl.BlockSpec((tm, D), lambda i, k: (i, 0))],
               (2, T // tk), (tm, D), _store(0.5))


def _ffn_dw_in(h, dgu):
    T, D = h.shape
    Fh = dgu.shape[2] // 2
    tk = _tile(T, 1024)
    return _mm("ffn_dw_in", TN, [h, dgu],
               [pl.BlockSpec((tk, D), lambda j, k: (k, 0)),
                pl.BlockSpec((None, tk, Fh), lambda j, k: (j // 2, k, j % 2))],
               [jax.ShapeDtypeStruct((4, D, Fh), BF16)], [pl.BlockSpec((None, D, Fh), lambda j, k: (j, 0, 0))],
               (4, T // tk), (D, Fh), _store())


def _ffn_dh(dgu, w4):
    T = dgu.shape[1]
    w4, l = w4
    D, Fh = w4.shape[2], w4.shape[3]
    tm = _tile(T, 1024)
    return _mm("ffn_dh", NT, [dgu, w4],
               [pl.BlockSpec((None, tm, Fh), lambda i, k: (k // 2, i, k % 2)),
                pl.BlockSpec((None, None, D, Fh), lambda i, k: (l, k, 0, 0))],
               [jax.ShapeDtypeStruct((T, D), F32)], [pl.BlockSpec((tm, D), lambda i, k: (i, 0))],
               (T // tm, 4), (tm, D), _store())


def _proj(name, a, w, out_dtype, dims=NN, extra=None, scale=None):
    T, K = a.shape
    w, l = w
    N = w.shape[2] if dims == NN else w.shape[1]
    tm = _tile(T, 512)
    ops = [a, w] + ([extra] if extra is not None else [])
    specs = [pl.BlockSpec((tm, K), lambda i, k: (i, 0)), pl.BlockSpec((None,) + w.shape[1:], lambda i, k: (l, 0, 0))]
    if extra is not None:
        specs.append(pl.BlockSpec((tm, N), lambda i, k: (i, 0)))
    ep = _residual(1.0) if extra is not None else _store(scale)
    return _mm(name, dims, ops, specs, [jax.ShapeDtypeStruct((T, N), out_dtype)],
               [pl.BlockSpec((tm, N), lambda i, k: (i, 0))], (T // tm, 1), None, ep)[0]


def _mix_up(x, g, wp, widths):
    T, D = x.shape
    wp, l = wp
    n_qkv, n_rest = widths
    NP = wp.shape[2]
    tm = _tile(T, 512)

    def kern(x_ref, g_ref, w_ref, h_ref, qkv_ref, rest_ref, fl_ref):
        xv = x_ref[...]
        r = lax.rsqrt(jnp.mean(xv * xv, axis=-1, keepdims=True) + RMS_EPS)
        hv = (xv * r * g_ref[...]).astype(BF16)
        h_ref[...] = hv
        qkv_ref[...] = _dot(hv, w_ref[:, 0:n_qkv], NN).astype(BF16)
        rest_ref[...] = _dot(hv, w_ref[:, n_qkv:n_qkv + n_rest], NN)
        fl_ref[...] = _dot(hv, w_ref[:, n_qkv + n_rest:NP], NN)

    row = lambda n: pl.BlockSpec((tm, n), lambda i: (i, 0))
    return pl.pallas_call(
        kern, name="mix_up", grid=(T // tm,),
        in_specs=[row(D), pl.BlockSpec((1, D), lambda i: (0, 0)), _resident((None, D, NP), lambda i: (l, 0, 0))],
        out_specs=[row(D), row(n_qkv), row(n_rest), row(LANES)],
        out_shape=[jax.ShapeDtypeStruct((T, D), BF16), jax.ShapeDtypeStruct((T, n_qkv), BF16),
                   jax.ShapeDtypeStruct((T, n_rest), F32), jax.ShapeDtypeStruct((T, LANES), F32)],
        compiler_params=_params(1),
    )(x, g, wp)


def _column_starts(pieces):
    starts, at = [], 0
    for p in pieces:
        starts.append(at)
        at += p.shape[1]
    return starts


def _mix_in_bwd(pieces, x, g, dres, wp):
    T, D = x.shape
    wp, l = wp
    NP = wp.shape[2]
    tm = _tile(T, 512)
    n, starts = len(pieces), _column_starts(pieces)

    def kern(*refs):
        x_ref, g_ref, d_ref, w_ref, dx_ref, dg_ref = refs[n:]
        dh = jnp.zeros((tm, D), F32)
        for p_ref, at in zip(refs[:n], starts):
            dh = dh + _dot(p_ref[...].astype(BF16), w_ref[:, at:at + p_ref.shape[1]], NT)
        xv = x_ref[...]
        r = lax.rsqrt(jnp.mean(xv * xv, axis=-1, keepdims=True) + RMS_EPS)
        y = xv * r
        dy = dh * g_ref[...]
        dx_ref[...] = d_ref[...] + r * (dy - y * jnp.mean(dy * y, axis=-1, keepdims=True))
        part = jnp.sum(dh * y, axis=0, keepdims=True)

        @pl.when(pl.program_id(0) == 0)
        def _():
            dg_ref[...] = part

        @pl.when(pl.program_id(0) > 0)
        def _():
            dg_ref[...] += part

    row = lambda n: pl.BlockSpec((tm, n), lambda i: (i, 0))
    vec = pl.BlockSpec((1, D), lambda i: (0, 0))
    return pl.pallas_call(
        kern, name="mix_in_bwd", grid=(T // tm,),
        in_specs=[row(p.shape[1]) for p in pieces] + [row(D), vec, row(D), _resident((None, D, NP), lambda i: (l, 0, 0))],
        out_specs=[row(D), vec],
        out_shape=[jax.ShapeDtypeStruct((T, D), F32), jax.ShapeDtypeStruct((1, D), F32)],
        compiler_params=_params(1),
    )(*pieces, x, g, dres, wp)


def _pieces_dw(name, a, pieces, out_dtype, tk_pref):
    T, M = a.shape
    n, starts = len(pieces), _column_starts(pieces)
    N = starts[-1] + pieces[-1].shape[1]
    tk = _tile(T, tk_pref)
    nk = T // tk

    def kern(a_ref, *refs):
        o_ref, acc = refs[n], refs[n + 1]
        k = pl.program_id(0)

        @pl.when(k == 0)
        def _():
            acc[...] = jnp.zeros_like(acc)

        av = a_ref[...].astype(BF16)
        for p_ref, at in zip(refs[:n], starts):
            acc[:, at:at + p_ref.shape[1]] += _dot(av, p_ref[...].astype(BF16), TN)

        @pl.when(k == nk - 1)
        def _():
            o_ref[...] = acc[...].astype(out_dtype)

    return pl.pallas_call(
        kern, name=name, grid=(nk,),
        in_specs=[pl.BlockSpec((tk, M), lambda k: (k, 0))] + [pl.BlockSpec((tk, p.shape[1]), lambda k: (k, 0)) for p in pieces],
        out_specs=pl.BlockSpec((M, N), lambda k: (0, 0)), out_shape=jax.ShapeDtypeStruct((M, N), out_dtype),
        scratch_shapes=[pltpu.VMEM((M, N), F32)], compiler_params=_params(1),
    )(a, *pieces)


def _rows_dw(name, pieces, d, out_dtype):
    T, N = d.shape
    n, starts = len(pieces), _column_starts(pieces)
    M = starts[-1] + pieces[-1].shape[1]
    tk = _tile(T, 1024)
    nk = T // tk

    def kern(*refs):
        d_ref, o_ref, acc = refs[n], refs[n + 1], refs[n + 2]
        k = pl.program_id(0)

        @pl.when(k == 0)
        def _():
            acc[...] = jnp.zeros_like(acc)

        dv = d_ref[...].astype(BF16)
        for p_ref, at in zip(refs[:n], starts):
            acc[at:at + p_ref.shape[1], :] += _dot(p_ref[...], dv, TN)

        @pl.when(k == nk - 1)
        def _():
            o_ref[...] = acc[...].astype(out_dtype)

    return pl.pallas_call(
        kern, name=name, grid=(nk,),
        in_specs=[pl.BlockSpec((tk, p.shape[1]), lambda k: (k, 0)) for p in pieces] + [pl.BlockSpec((tk, N), lambda k: (k, 0))],
        out_specs=pl.BlockSpec((M, N), lambda k: (0, 0)), out_shape=jax.ShapeDtypeStruct((M, N), out_dtype),
        scratch_shapes=[pltpu.VMEM((M, N), F32)], compiler_params=_params(1),
    )(*pieces, d)


def _mix_out(pieces, w, x):
    T, D = x.shape
    w, l = w
    n, starts = len(pieces), _column_starts(pieces)
    tm = _tile(T, 512)

    def kern(*refs):
        w_ref, x_ref, o_ref = refs[n:]
        acc = x_ref[...]
        for p_ref, at in zip(refs[:n], starts):
            acc = acc + _dot(p_ref[...], w_ref[at:at + p_ref.shape[1], :], NN)
        o_ref[...] = acc

    row = lambda m: pl.BlockSpec((tm, m), lambda i: (i, 0))
    return pl.pallas_call(
        kern, name="mix_out", grid=(T // tm,),
        in_specs=[row(p.shape[1]) for p in pieces] + [_resident((None,) + w.shape[1:], lambda i: (l, 0, 0)), row(D)],
        out_specs=row(D), out_shape=jax.ShapeDtypeStruct((T, D), F32), compiler_params=_params(1),
    )(*pieces, w, x)


def _dw(name, a, d, out_dtype):
    T, M = a.shape
    N = d.shape[1]
    tk = _tile(T, 1024 if M * N <= 1024 * 1408 else 512)
    return _mm(name, TN, [a, d],
               [pl.BlockSpec((tk, M), lambda i, k: (k, 0)), pl.BlockSpec((tk, N), lambda i, k: (k, 0))],
               [jax.ShapeDtypeStruct((M, N), out_dtype)], [pl.BlockSpec((M, N), lambda i, k: (0, 0))],
               (1, T // tk), (M, N), _store())[0]


def _log_sigmoid(z):
    return jnp.minimum(z, 0.0) - jnp.log(1.0 + jnp.exp(-jnp.abs(z)))


def _decay_fwd(fl, bias):
    B, S, _ = fl.shape

    def kern(fl_ref, b_ref, o_ref):
        d = _log_sigmoid(fl_ref[...] + b_ref[...])
        row = lax.broadcasted_iota(jnp.int32, (S, LANES), 0)
        sh = 1
        while sh < S:
            d = d + jnp.where(row >= sh, pltpu.roll(d, sh, 0), 0.0)
            sh *= 2
        o_ref[...] = d.T[0:8, :]

    return pl.pallas_call(
        kern, name="decay_fwd", grid=(B,),
        in_specs=[pl.BlockSpec((None, S, LANES), lambda b: (b, 0, 0)), pl.BlockSpec((1, LANES), lambda b: (0, 0))],
        out_specs=pl.BlockSpec((None, 8, S), lambda b: (b, 0, 0)),
        out_shape=jax.ShapeDtypeStruct((B, 8, S), F32), compiler_params=_params(1),
    )(fl, bias)


def _decay_bwd(ddrow, ddcol, fl, bias, n_heads):
    B, S, _ = fl.shape

    def kern(dd_ref, ddc_ref, fl_ref, b_ref, dfl_ref, db_ref):
        dd = jnp.concatenate([dd_ref[...], jnp.zeros((LANES - 8, S), F32)], axis=0).T + ddc_ref[...]
        row = lax.broadcasted_iota(jnp.int32, (S, LANES), 0)
        lane = lax.broadcasted_iota(jnp.int32, (S, LANES), 1)
        sh = 1
        while sh < S:
            dd = dd + jnp.where(row < S - sh, pltpu.roll(dd, S - sh, 0), 0.0)
            sh *= 2
        z = fl_ref[...] + b_ref[...]
        dfl = jnp.where(lane < n_heads, dd / (1.0 + jnp.exp(z)), 0.0)
        dfl_ref[...] = dfl
        part = jnp.sum(dfl, axis=0, keepdims=True)

        @pl.when(pl.program_id(0) == 0)
        def _():
            db_ref[...] = part

        @pl.when(pl.program_id(0) > 0)
        def _():
            db_ref[...] += part

    return pl.pallas_call(
        kern, name="decay_bwd", grid=(B,),
        in_specs=[pl.BlockSpec((None, 8, S), lambda b: (b, 0, 0)), pl.BlockSpec((None, S, LANES), lambda b: (b, 0, 0)),
                  pl.BlockSpec((None, S, LANES), lambda b: (b, 0, 0)), pl.BlockSpec((1, LANES), lambda b: (0, 0))],
        out_specs=[pl.BlockSpec((None, S, LANES), lambda b: (b, 0, 0)), pl.BlockSpec((1, LANES), lambda b: (0, 0))],
        out_shape=[jax.ShapeDtypeStruct((B, S, LANES), F32), jax.ShapeDtypeStruct((1, LANES), F32)],
        compiler_params=_params(1),
    )(ddrow, ddcol, fl, bias)


def _attn_fwd(qkv, drow, n_heads, tq):
    B, S, _ = qkv.shape
    DA = n_heads * HEAD_DIM
    scale = HEAD_DIM ** -0.5

    n_pairs = n_heads // 2

    def kern(q_ref, k_ref, v_ref, dr_ref, o_ref, lse_ref):
        i = pl.program_id(1)
        lane = lax.broadcasted_iota(jnp.int32, (tq, LANES), 1)
        low = lane < HEAD_DIM
        causal = lax.broadcasted_iota(jnp.int32, (tq, tq), 1) <= lax.broadcasted_iota(jnp.int32, (tq, tq), 0)
        qms = []
        for p in range(n_pairs):
            q2 = q_ref[:, LANES * p:LANES * (p + 1)] * scale
            qms += [jnp.where(low, q2, jnp.zeros_like(q2)), jnp.where(low, jnp.zeros_like(q2), q2)]

        def step(j, carry, masked):
            ms, ls, accs = carry
            ks = pl.multiple_of(j * tq, tq)
            new_m, new_l, new_acc = [], [], []
            for p in range(n_pairs):
                cols = slice(LANES * p, LANES * (p + 1))
                k2, v2 = k_ref[pl.ds(ks, tq), cols], v_ref[pl.ds(ks, tq), cols]
                alphas, pvs = [], []
                for h in (2 * p, 2 * p + 1):
                    s = _dot(qms[h], k2, NT) - dr_ref[h, pl.ds(j, 1), :]
                    if masked:
                        s = jnp.where(causal, s, -jnp.inf)
                    m_new = jnp.maximum(ms[h], jnp.max(s, axis=1, keepdims=True))
                    alpha = jnp.exp(ms[h] - m_new)
                    pm = jnp.exp(s - m_new)
                    new_m.append(m_new)
                    new_l.append(alpha * ls[h] + jnp.sum(pm, axis=1, keepdims=True))
                    alphas.append(alpha)
                    pvs.append(_dot(pm.astype(BF16), v2, NN))
                new_acc.append(jnp.where(low, alphas[0], alphas[1]) * accs[p] + jnp.where(low, pvs[0], pvs[1]))
            return tuple(new_m), tuple(new_l), tuple(new_acc)

        init = (tuple(jnp.full((tq, 1), -jnp.inf, F32) for _ in range(n_heads)),
                tuple(jnp.zeros((tq, 1), F32) for _ in range(n_heads)),
                tuple(jnp.zeros((tq, LANES), F32) for _ in range(n_pairs)))
        ms, ls, accs = step(i, lax.fori_loop(0, i, functools.partial(step, masked=False), init), True)
        lse_mat = jnp.zeros((tq, LANES), F32)
        for p in range(n_pairs):
            l0, l1 = ls[2 * p], ls[2 * p + 1]
            o_ref[:, LANES * p:LANES * (p + 1)] = (accs[p] / jnp.where(low, l0, l1)).astype(BF16)
            lse_mat = jnp.where(lane == 2 * p, ms[2 * p] + jnp.log(l0), lse_mat)
            lse_mat = jnp.where(lane == 2 * p + 1, ms[2 * p + 1] + jnp.log(l1), lse_mat)
        lse_ref[...] = lse_mat

    nq = S // tq
    return pl.pallas_call(
        kern, name="attn_fwd", grid=(B, nq),
        in_specs=[pl.BlockSpec((None, tq, DA), lambda b, i: (b, i, 0)),
                  pl.BlockSpec((None, S, DA), lambda b, i: (b, 0, 1)),
                  pl.BlockSpec((None, S, DA), lambda b, i: (b, 0, 2)),
                  pl.BlockSpec((None, 8, nq, tq), lambda b, i: (b, 0, 0, 0))],
        out_specs=[pl.BlockSpec((None, tq, DA), lambda b, i: (b, i, 0)),
                   pl.BlockSpec((None, tq, LANES), lambda b, i: (b, i, 0))],
        out_shape=[jax.ShapeDtypeStruct((B, S, DA), BF16), jax.ShapeDtypeStruct((B, S, LANES), F32)],
        compiler_params=_params(2),
    )(qkv, qkv, qkv, drow)


def _attn_bwd(qkv, drow, o, lse, dycat, n_heads, tq):
    B, S, _ = qkv.shape
    DA = n_heads * HEAD_DIM
    scale = HEAD_DIM ** -0.5
    nq = S // tq

    n_pairs = n_heads // 2

    def kern(q_ref, k_ref, v_ref, dr_ref, o_ref, lse_ref, do_ref, dq_ref, dk_ref, dv_ref, ddr_ref, ddc_ref,
             dk_acc, dv_acc, qm_s, dom_s, delta_s, rs_s, dq_s):
        i = pl.program_id(1)

        @pl.when(i == 0)
        def _():
            dk_acc[...] = jnp.zeros_like(dk_acc)
            dv_acc[...] = jnp.zeros_like(dv_acc)
            ddr_ref[...] = jnp.zeros_like(ddr_ref)

        lane = lax.broadcasted_iota(jnp.int32, (tq, LANES), 1)
        low = lane < HEAD_DIM
        causal = lax.broadcasted_iota(jnp.int32, (tq, tq), 1) <= lax.broadcasted_iota(jnp.int32, (tq, tq), 0)
        for p in range(n_pairs):
            cols = slice(LANES * p, LANES * (p + 1))
            q2 = q_ref[:, cols] * scale
            do_f = do_ref[:, cols]
            do2 = do_f.astype(BF16)
            prod = do_f * o_ref[:, cols].astype(F32)
            qm_s[2 * p] = jnp.where(low, q2, jnp.zeros_like(q2))
            qm_s[2 * p + 1] = jnp.where(low, jnp.zeros_like(q2), q2)
            dom_s[2 * p] = jnp.where(low, do2, jnp.zeros_like(do2))
            dom_s[2 * p + 1] = jnp.where(low, jnp.zeros_like(do2), do2)
            delta_s[2 * p] = jnp.sum(jnp.where(low, prod, 0.0), axis=1, keepdims=True)
            delta_s[2 * p + 1] = jnp.sum(jnp.where(low, 0.0, prod), axis=1, keepdims=True)
            dq_s[p] = jnp.zeros((tq, LANES), F32)
        rs_s[...] = jnp.zeros(rs_s.shape, F32)

        def step(j, masked):
            ks = pl.multiple_of(j * tq, tq)
            for p in range(n_pairs):
                cols = slice(LANES * p, LANES * (p + 1))
                k2, v2 = k_ref[pl.ds(ks, tq), cols], v_ref[pl.ds(ks, tq), cols]
                dvs, dks, dqs = [], [], []
                for h in (2 * p, 2 * p + 1):
                    qm, dom = qm_s[h], dom_s[h]
                    s = _dot(qm, k2, NT) - dr_ref[h, pl.ds(j, 1), :]
                    if masked:
                        s = jnp.where(causal, s, -jnp.inf)
                    pm = jnp.exp(s - lse_ref[:, h:h + 1])
                    ds = pm * (_dot(dom, v2, NT) - delta_s[h])
                    ddr_ref[h, pl.ds(j, 1), :] -= jnp.sum(ds, axis=0, keepdims=True)
                    rs_s[h] += jnp.sum(ds, axis=1, keepdims=True)
                    dsb = ds.astype(BF16)
                    dvs.append(_dot(pm.astype(BF16), dom, TN))
                    dks.append(_dot(dsb, qm, TN))
                    dqs.append(_dot(dsb, k2, NN))
                dv_acc[pl.ds(ks, tq), cols] += dvs[0] + dvs[1]
                dk_acc[pl.ds(ks, tq), cols] += dks[0] + dks[1]
                dq_s[p] += jnp.where(low, dqs[0], dqs[1])

        def body(j, carry):
            step(j, False)
            return carry

        lax.fori_loop(0, i, body, 0)
        step(i, True)
        ddc = jnp.zeros((tq, LANES), F32)
        for p in range(n_pairs):
            dq_ref[:, LANES * p:LANES * (p + 1)] = (dq_s[p] * scale).astype(BF16)
            ddc = jnp.where(lane == 2 * p, rs_s[2 * p], ddc)
            ddc = jnp.where(lane == 2 * p + 1, rs_s[2 * p + 1], ddc)
        ddc_ref[...] = ddc

        @pl.when(i == nq - 1)
        def _():
            dk_ref[...] = dk_acc[...].astype(BF16)
            dv_ref[...] = dv_acc[...].astype(BF16)

    tile = pl.BlockSpec((None, tq, DA), lambda b, i: (b, i, 0))
    seq = pl.BlockSpec((None, S, DA), lambda b, i: (b, 0, 0))
    dec = pl.BlockSpec((None, 8, nq, tq), lambda b, i: (b, 0, 0, 0))
    return pl.pallas_call(
        kern, name="attn_bwd", grid=(B, nq),
        in_specs=[tile, pl.BlockSpec((None, S, DA), lambda b, i: (b, 0, 1)),
                  pl.BlockSpec((None, S, DA), lambda b, i: (b, 0, 2)), dec, tile,
                  pl.BlockSpec((None, tq, LANES), lambda b, i: (b, i, 0)), tile],
        out_specs=[tile, seq, seq, dec, pl.BlockSpec((None, tq, LANES), lambda b, i: (b, i, 0))],
        out_shape=[jax.ShapeDtypeStruct((B, S, DA), BF16)] * 3 + [jax.ShapeDtypeStruct((B, 8, nq, tq), F32),
                                                                  jax.ShapeDtypeStruct((B, S, LANES), F32)],
        scratch_shapes=[pltpu.VMEM((S, DA), F32), pltpu.VMEM((S, DA), F32),
                        pltpu.VMEM((n_heads, tq, LANES), BF16), pltpu.VMEM((n_heads, tq, LANES), BF16),
                        pltpu.VMEM((n_heads, tq, 1), F32), pltpu.VMEM((n_heads, tq, 1), F32),
                        pltpu.VMEM((n_pairs, tq, LANES), F32)],
        compiler_params=_params(2),
    )(qkv, qkv, qkv, drow, o, lse, dycat)


def _down(v, d, row):
    return jnp.where(row >= d, pltpu.roll(v, d, 0), 0.0)


def _up(v, d, row, S):
    return jnp.where(row < S - d, pltpu.roll(v, S - d, 0), 0.0)


def _window(v, shift, group):
    sums, acc, d = [], v, 1
    for _ in POOL_WINDOWS:
        acc = acc + shift(acc, d)
        sums.append(acc)
        d *= 2
    out = sums[-1]
    for gi in range(len(POOL_WINDOWS) - 2, -1, -1):
        out = jnp.where(group == gi, sums[gi], out)
    return out


def _pool_count(row, group):
    w = jnp.full(row.shape, POOL_WINDOWS[-1], jnp.int32)
    for gi in range(len(POOL_WINDOWS) - 2, -1, -1):
        w = jnp.where(group == gi, POOL_WINDOWS[gi], w)
    return jnp.minimum(row + 1, w).astype(F32)


def _mix_local_fwd(rest, wbd, ps, cw):
    B, S, C4 = rest.shape
    C = C4 // 4
    gw = C // len(POOL_WINDOWS)

    def kern(r_ref, w_ref, ps_ref, cw_ref, y_ref, pooled_ref):
        row = lax.broadcasted_iota(jnp.int32, (S, C), 0)
        group = lax.broadcasted_iota(jnp.int32, (S, C), 1) // gw
        u = r_ref[:, 0:C]
        pooled = _window(u, lambda v, d: _down(v, d, row), group) / _pool_count(row, group) - u
        pb = pooled.astype(BF16)
        pooled_ref[...] = pb
        y_ref[:, 0:C] = (_dot(pb, w_ref[...], NN) * ps_ref[...]).astype(BF16)
        uc = r_ref[:, 2 * C:3 * C] * r_ref[:, 3 * C:4 * C]
        y = cw_ref[0:1, :] * _down(uc, 2, row) + cw_ref[1:2, :] * _down(uc, 1, row) + cw_ref[2:3, :] * uc
        y_ref[:, C:2 * C] = (r_ref[:, C:2 * C] * y).astype(BF16)

    return pl.pallas_call(
        kern, name="mix_local_fwd", grid=(B,),
        in_specs=[pl.BlockSpec((None, S, C4), lambda b: (b, 0, 0)), pl.BlockSpec((C, C), lambda b: (0, 0)),
                  pl.BlockSpec((1, C), lambda b: (0, 0)), pl.BlockSpec((8, C), lambda b: (0, 0))],
        out_specs=[pl.BlockSpec((None, S, 2 * C), lambda b: (b, 0, 0)), pl.BlockSpec((None, S, C), lambda b: (b, 0, 0))],
        out_shape=[jax.ShapeDtypeStruct((B, S, 2 * C), BF16), jax.ShapeDtypeStruct((B, S, C), BF16)],
        compiler_params=_params(1),
    )(rest, wbd, ps, cw)


def _mix_local_bwd(rest, pooled, dycat, wbd, ps, cw):
    B, S, C4 = rest.shape
    C = C4 // 4
    gw = C // len(POOL_WINDOWS)

    def kern(r_ref, pooled_ref, d_ref, w_ref, ps_ref, cw_ref, dr_ref, dw_ref, dps_ref, dcw_ref):
        row = lax.broadcasted_iota(jnp.int32, (S, C), 0)
        group = lax.broadcasted_iota(jnp.int32, (S, C), 1) // gw
        dyp = d_ref[:, 0:C]
        dyc = d_ref[:, C:2 * C]
        pb = pooled_ref[...]
        dps = jnp.sum(dyp * _dot(pb, w_ref[...], NN), axis=0, keepdims=True)
        dzb = (dyp * ps_ref[...]).astype(BF16)
        dw = _dot(pb, dzb, TN)
        dpooled = _dot(dzb, w_ref[...], NT)
        g = dpooled / _pool_count(row, group)
        dr_ref[:, 0:C] = (_window(g, lambda v, d: _up(v, d, row, S), group) - dpooled).astype(BF16)
        cc, ch = r_ref[:, 2 * C:3 * C], r_ref[:, 3 * C:4 * C]
        uc = cc * ch
        u1, u2 = _down(uc, 1, row), _down(uc, 2, row)
        y = cw_ref[0:1, :] * u2 + cw_ref[1:2, :] * u1 + cw_ref[2:3, :] * uc
        dr_ref[:, C:2 * C] = (dyc * y).astype(BF16)
        dy = dyc * r_ref[:, C:2 * C]
        duc = cw_ref[0:1, :] * _up(dy, 2, row, S) + cw_ref[1:2, :] * _up(dy, 1, row, S) + cw_ref[2:3, :] * dy
        dr_ref[:, 2 * C:3 * C] = (duc * ch).astype(BF16)
        dr_ref[:, 3 * C:4 * C] = (duc * cc).astype(BF16)
        dcw = jnp.concatenate([jnp.sum(dy * u2, axis=0, keepdims=True), jnp.sum(dy * u1, axis=0, keepdims=True),
                               jnp.sum(dy * uc, axis=0, keepdims=True), jnp.zeros((5, C), F32)], axis=0)

        @pl.when(pl.program_id(0) == 0)
        def _():
            dw_ref[...] = dw
            dps_ref[...] = dps
            dcw_ref[...] = dcw

        @pl.when(pl.program_id(0) > 0)
        def _():
            dw_ref[...] += dw
            dps_ref[...] += dps
            dcw_ref[...] += dcw

    full = lambda shape: pl.BlockSpec(shape, lambda b: (0, 0))
    return pl.pallas_call(
        kern, name="mix_local_bwd", grid=(B,),
        in_specs=[pl.BlockSpec((None, S, C4), lambda b: (b, 0, 0)), pl.BlockSpec((None, S, C), lambda b: (b, 0, 0)),
                  pl.BlockSpec((None, S, 2 * C), lambda b: (b, 0, 1)), full((C, C)), full((1, C)), full((8, C))],
        out_specs=[pl.BlockSpec((None, S, C4), lambda b: (b, 0, 0)), full((C, C)), full((1, C)), full((8, C))],
        out_shape=[jax.ShapeDtypeStruct((B, S, C4), BF16), jax.ShapeDtypeStruct((C, C), F32),
                   jax.ShapeDtypeStruct((1, C), F32), jax.ShapeDtypeStruct((8, C), F32)],
        compiler_params=_params(1),
    )(rest, pooled, dycat, wbd, ps, cw)


def _adamw(w, gs, m, v, token=None):
    R, C = w.shape
    pieces = [p if isinstance(p, tuple) else (p,) for p in gs]
    owner = [s for s, p in enumerate(pieces) for _ in p]
    flat = [a for p in pieces for a in p]
    n = len(flat)
    rows = R // len(pieces)
    tr = _tile(rows, 256)
    per = rows // tr
    tok_ops, tok_specs = _token_operand(token)

    def kern(w_ref, *refs):
        g_refs, (m_ref, v_ref), (g_out, d_ref, nm_ref, nv_ref) = refs[:n], refs[n:n + 2], refs[n + 2 + len(tok_ops):]
        vals, at = [], 0
        for p in pieces:
            vals.append(g_refs[at][...] if len(p) == 1 else g_refs[at][...] + g_refs[at + 1][...])
            at += len(p)
        gv = vals[0]
        for s in range(1, len(pieces)):
            gv = jnp.where(pl.program_id(0) // per == s, vals[s], gv)
        nm = ADAM_B1 * m_ref[...] + (1.0 - ADAM_B1) * gv
        nv = ADAM_B2 * v_ref[...] + (1.0 - ADAM_B2) * (gv * gv)
        m_hat = nm / (1.0 - ADAM_B1 ** ADAM_STEP)
        v_hat = nv / (1.0 - ADAM_B2 ** ADAM_STEP)
        g_out[...] = gv
        d_ref[...] = -ADAM_LR * (m_hat / (jnp.sqrt(v_hat) + ADAM_EPS) + ADAM_WD * w_ref[...])
        nm_ref[...] = nm
        nv_ref[...] = nv

    def piece(s):
        return pl.BlockSpec((tr, C), lambda i: (jnp.clip(i - s * per, 0, per - 1), 0))

    blk = pl.BlockSpec((tr, C), lambda i: (i, 0))
    return pl.pallas_call(
        kern, name="adamw", grid=(R // tr,), in_specs=[blk] + [piece(s) for s in owner] + [blk] * 2 + tok_specs,
        out_specs=[blk] * 4, out_shape=[jax.ShapeDtypeStruct((R, C), F32)] * 4, compiler_params=_params(1),
    )(w, *flat, m, v, *tok_ops)


def _place():
    x, y, c = lax.axis_index("x"), lax.axis_index("y"), lax.axis_index("c")
    return x, y, c, [(1 - x, y), (x, 1 - y), (1 - x, 1 - y)]


def _comm_call(name, body, operands, out_shape, n_sems, aliases=None):
    any_spec = pl.BlockSpec(memory_space=pl.ANY)
    return pl.pallas_call(
        body, name=name, in_specs=[any_spec] * len(operands), out_specs=[any_spec] * len(out_shape),
        out_shape=out_shape, input_output_aliases=aliases or {},
        scratch_shapes=[pltpu.SemaphoreType.DMA((n,)) for n in n_sems],
    )(*operands)


def _my_block():
    return 2 * lax.axis_index("x") + lax.axis_index("y")


def _place_shard(w, dtype, first=0, count=None):
    L, R, C = w.shape
    count = L if count is None else count
    tr = _tile(R, 512)

    def kern(w_ref, o_ref):
        o_ref[...] = w_ref[...].astype(dtype)

    return pl.pallas_call(
        kern, name="place_shard", grid=(count, R // tr),
        in_specs=[pl.BlockSpec((None, tr, C), lambda l, i: (first + l, i, 0))],
        out_specs=pl.BlockSpec((None, None, tr, C), lambda l, i: (l, _my_block(), i, 0)),
        out_shape=jax.ShapeDtypeStruct((count, N_CHIPS, R, C), dtype), compiler_params=_params(2),
    )(w)


HALF_ROWS = 16


def _rows(ref, half):
    hr = ref.shape[-2] // 2
    return ref.at[(slice(None),) * (len(ref.shape) - 2) + (pl.ds(half * hr, hr),)]


def _all_gather(bufs):
    n = len(bufs)

    def body(*refs):
        outs = refs[n:2 * n]
        send_sems, recv_sems = refs[2 * n:]
        x, y, c, chips = _place()
        sibling = (x, y, 1 - c)

        def remote(k, j, chip, half, to):
            blk = 2 * chip[0] + chip[1]
            if outs[k].shape[2] % (2 * HALF_ROWS) == 0:
                region = _rows(outs[k].at[:, blk], half)
            else:
                hl = outs[k].shape[0] // 2
                region = outs[k].at[pl.ds(half * hl, hl), blk]
            return pltpu.make_async_remote_copy(
                src_ref=region, dst_ref=region, send_sem=send_sems.at[6 * k + j],
                recv_sem=recv_sems.at[6 * k + j], device_id=to, device_id_type=MESH)

        first = [remote(k, j, (x, y), c, (*chip, c)) for k in range(n) for j, chip in enumerate(chips)]
        for cp in first:
            cp.start()
        passed = []
        for k in range(n):
            for j, chip in enumerate(chips):
                remote(k, j, chip, c, (x, y, c)).wait_recv()
                passed.append(remote(k, 3 + j, chip, c, sibling))
                passed[-1].start()
        for k in range(n):
            for j, chip in enumerate(chips):
                remote(k, 3 + j, chip, 1 - c, (x, y, c)).wait_recv()
        for cp in first + passed:
            cp.wait_send()

    out_shape = [jax.ShapeDtypeStruct(s.shape, s.dtype) for s in bufs]
    return _comm_call("all_gather_weights", body, bufs, out_shape, (6 * n, 6 * n), aliases={k: k for k in range(n)})


_HBM = pl.BlockSpec(memory_space=pltpu.HBM)
_SEM = pl.BlockSpec(memory_space=pltpu.SEMAPHORE)
_ANY = pl.BlockSpec(memory_space=pl.ANY)


def _split_start(name, bufs, n_copies, make_copies, after):
    n = len(bufs)

    def body(*refs):
        send_sems, recv_sems, token = refs[n + 1], refs[n + 2], refs[2 * n + 3]
        for cp in make_copies(refs[:n], send_sems, recv_sems):
            cp.start()
        token[...] = jnp.zeros_like(token)

    res = pl.pallas_call(
        body, name=name, in_specs=[_HBM] * n + [_ANY],
        out_shape=(pltpu.SemaphoreType.DMA((n_copies,)), pltpu.SemaphoreType.DMA((n_copies,)),
                   *[pltpu.HBM(b.shape, b.dtype) for b in bufs], jax.ShapeDtypeStruct((8, LANES), F32)),
        out_specs=(_SEM, _SEM, *[_HBM] * n, pl.BlockSpec(memory_space=pltpu.VMEM)),
        input_output_aliases={i: 2 + i for i in range(n)},
        compiler_params=pltpu.CompilerParams(has_side_effects=pltpu.SideEffectType.DATAFLOW_SIDE_EFFECTING),
    )(*[pltpu.with_memory_space_constraint(b, pltpu.HBM) for b in bufs], after)
    return res[0], res[1], list(res[2:2 + n]), res[2 + n]


def _split_wait(name, send_sems, recv_sems, bufs, make_copies, after):
    n = len(bufs)

    def body(*refs):
        for cp in make_copies(refs[:n], refs[n], refs[n + 1]):
            cp.wait_send()
            cp.wait_recv()

    return list(pl.pallas_call(
        body, name=name, in_specs=[_HBM] * n + [_SEM, _SEM, _ANY],
        out_shape=tuple(pltpu.HBM(b.shape, b.dtype) for b in bufs), out_specs=tuple([_HBM] * n),
        input_output_aliases={i: i for i in range(n)},
        compiler_params=pltpu.CompilerParams(has_side_effects=pltpu.SideEffectType.DATAFLOW_SIDE_EFFECTING),
    )(*bufs, send_sems, recv_sems, after))


def _gather_copies(refs, send_sems, recv_sems):
    x, y, c, chips = _place()
    return [pltpu.make_async_remote_copy(
        src_ref=ref.at[:, 2 * x + y], dst_ref=ref.at[:, 2 * x + y], send_sem=send_sems.at[3 * k + j],
        recv_sem=recv_sems.at[3 * k + j], device_id=(*chip, c), device_id_type=MESH)
        for k, ref in enumerate(refs) for j, chip in enumerate(chips)]


def _exchange_copies(refs, send_sems, recv_sems):
    n = len(refs) // 2
    x, y, c, chips = _place()
    return [pltpu.make_async_remote_copy(
        src_ref=refs[k].at[:, 2 * chip[0] + chip[1]], dst_ref=refs[n + k].at[j], send_sem=send_sems.at[3 * k + j],
        recv_sem=recv_sems.at[3 * k + j], device_id=(*chip, c), device_id_type=MESH)
        for k in range(n) for j, chip in enumerate(chips)]


def _rs_swap_halves(grads):
    n = len(grads)

    def body(*refs):
        ins, outs = refs[:n], refs[n:2 * n]
        send_sems, recv_sems = refs[2 * n:]
        x, y, c, _ = _place()
        copies = [pltpu.make_async_remote_copy(
            src_ref=_rows(ins[k], 1 - c), dst_ref=outs[k], send_sem=send_sems.at[k],
            recv_sem=recv_sems.at[k], device_id=(x, y, 1 - c), device_id_type=MESH) for k in range(n)]
        for cp in copies:
            cp.start()
        for cp in copies:
            cp.wait()

    out_shape = [jax.ShapeDtypeStruct(g.shape[:2] + (g.shape[2] // 2, g.shape[3]), g.dtype) for g in grads]
    return _comm_call("rs_swap_halves", body, grads, out_shape, (n, n))


def _rs_exchange(parts):
    n = len(parts)

    def body(*refs):
        ins, outs = refs[:n], refs[n:2 * n]
        send_sems, recv_sems = refs[2 * n:]
        x, y, c, chips = _place()
        copies = [pltpu.make_async_remote_copy(
            src_ref=ins[k].at[:, 2 * chip[0] + chip[1]], dst_ref=outs[k].at[j], send_sem=send_sems.at[3 * k + j],
            recv_sem=recv_sems.at[3 * k + j], device_id=(*chip, c), device_id_type=MESH)
            for k in range(n) for j, chip in enumerate(chips)]
        for cp in copies:
            cp.start()
        for cp in copies:
            cp.wait()

    out_shape = [jax.ShapeDtypeStruct((3, p.shape[0]) + p.shape[2:], p.dtype) for p in parts]
    return _comm_call("rs_exchange", body, parts, out_shape, (3 * n, 3 * n))


def _rs_share(bufs):
    n = len(bufs)

    def body(*refs):
        outs = refs[n:2 * n]
        send_sems, recv_sems = refs[2 * n:]
        x, y, c, _ = _place()

        def half(k, which):
            region = _rows(outs[k], which)
            return pltpu.make_async_remote_copy(
                src_ref=region, dst_ref=region, send_sem=send_sems.at[k], recv_sem=recv_sems.at[k],
                device_id=(x, y, 1 - c), device_id_type=MESH)

        sends = [half(k, c) for k in range(n)]
        for cp in sends:
            cp.start()
        for k in range(n):
            half(k, 1 - c).wait_recv()
        for cp in sends:
            cp.wait_send()

    out_shape = [jax.ShapeDtypeStruct(h.shape, h.dtype) for h in bufs]
    return _comm_call("rs_share", body, bufs, out_shape, (n, n), aliases={k: k for k in range(n)})


def _all_reduce_small(v):
    n = v.shape[0]

    def body(v_ref, o_ref, gbuf, send_sems, recv_sems):
        x, y, c, _ = _place()
        me = 4 * x + 2 * y + c
        gbuf[me] = v_ref[...]
        copies, waits = [], []
        for r in range(1, N_DEV):
            px = 1 - x if r & 4 else x
            py = 1 - y if r & 2 else y
            pc = 1 - c if r & 1 else c
            mk = functools.partial(pltpu.make_async_remote_copy, src_ref=v_ref, send_sem=send_sems.at[r - 1],
                                   recv_sem=recv_sems.at[r - 1], device_id=(px, py, pc), device_id_type=MESH)
            copies.append(mk(dst_ref=gbuf.at[me]))
            waits.append(mk(dst_ref=gbuf.at[4 * px + 2 * py + pc]))
        for cp in copies:
            cp.start()
        for cp in waits:
            cp.wait_recv()
        for cp in copies:
            cp.wait_send()
        acc = gbuf[0]
        for d in range(1, N_DEV):
            acc = acc + gbuf[d]
        o_ref[...] = acc

    vm = pl.BlockSpec(memory_space=pltpu.VMEM)
    return pl.pallas_call(
        body, name="all_reduce_small", in_specs=[vm], out_specs=vm, out_shape=jax.ShapeDtypeStruct(v.shape, F32),
        scratch_shapes=[pltpu.VMEM((N_DEV, n, LANES), F32), pltpu.SemaphoreType.DMA((N_DEV - 1,)),
                        pltpu.SemaphoreType.DMA((N_DEV - 1,))],
        compiler_params=pltpu.CompilerParams(vmem_limit_bytes=VMEM_LIMIT),
    )(v)


def _add_half(g, h1):
    L, nb, hr, C = h1.shape
    g3, h3 = g.reshape(L * nb, 2 * hr, C), h1.reshape(L * nb, hr, C)
    tr = _tile(hr, 512)

    def kern(g_ref, h_ref, o_ref):
        o_ref[...] = (g_ref[...].astype(F32) + h_ref[...].astype(F32)).astype(BF16)

    blk = pl.BlockSpec((None, tr, C), lambda l, i: (l, i, 0))
    out = pl.pallas_call(
        kern, name="rs_add_half", grid=(L * nb, hr // tr),
        in_specs=[pl.BlockSpec((None, tr, C), lambda l, i: (l, lax.axis_index("c") * (hr // tr) + i, 0)), blk],
        out_specs=blk, out_shape=jax.ShapeDtypeStruct(h3.shape, BF16), compiler_params=_params(2),
    )(g3, h3)
    return out.reshape(h1.shape)


def _add_blocks(p, h2, half=True):
    L, nb, hr, C = p.shape
    tr = _tile(hr, 512)
    shift = lambda: lax.axis_index("c") * (hr // tr) if half else 0

    def kern(p_ref, h0_ref, h1_ref, h2_ref, o_ref):
        o_ref[...] = ((p_ref[...].astype(F32) + h0_ref[...].astype(F32)) + h1_ref[...].astype(F32)) + h2_ref[...].astype(F32)

    def other(j):
        return pl.BlockSpec((None, None, tr, C), lambda l, i: (j, l, i, 0))

    return pl.pallas_call(
        kern, name="rs_add_blocks", grid=(L, hr // tr),
        in_specs=[pl.BlockSpec((None, None, tr, C), lambda l, i: (l, _my_block(), i, 0)), other(0), other(1), other(2)],
        out_specs=pl.BlockSpec((None, tr, C), lambda l, i: (l, shift() + i, 0)),
        out_shape=jax.ShapeDtypeStruct((L, (2 if half else 1) * hr, C), F32), compiler_params=_params(2),
    )(p, h2, h2, h2)


def _d2d_swap(arrays):
    n = len(arrays)

    def body(*refs):
        ins, outs = refs[:n], refs[n:2 * n]
        send_sems, recv_sems = refs[2 * n:]
        x, y, c, _ = _place()
        copies = [pltpu.make_async_remote_copy(
            src_ref=ins[k], dst_ref=outs[k], send_sem=send_sems.at[k], recv_sem=recv_sems.at[k],
            device_id=(x, y, 1 - c), device_id_type=MESH) for k in range(n)]
        for cp in copies:
            cp.start()
        for cp in copies:
            cp.wait()

    return _comm_call("d2d_swap", body, arrays, [jax.ShapeDtypeStruct(a.shape, a.dtype) for a in arrays], (n, n))


def _reduce_scatter(grads):
    sib = _rs_swap_halves(grads)
    parts = [_add_half(g, h) for g, h in zip(grads, sib)]
    others = _rs_exchange(parts)
    return _rs_share([_add_blocks(p, o) for p, o in zip(parts, others)])


WEIGHTS = ("norm_ffn1", "w_ffn1_in", "w_ffn1_out", "norm_mix", "w_mix_in", "b_forget", "w_pool", "pool_scale",
           "conv_w", "w_mix_out", "norm_ffn2", "w_ffn2_in", "w_ffn2_out", "norm_final")
BIG = ("w_ffn1_in", "w_ffn1_out", "w_mix_in", "w_mix_out", "w_ffn2_in", "w_ffn2_out")
SMALL = ("norm_ffn1", "norm_mix", "b_forget", "w_pool", "pool_scale", "conv_w", "norm_ffn2", "norm_final")


def _prep_weights(small, gathered, conv_w, D, first):
    DA, C, H = D // 2, D // 4, D // 2 // HEAD_DIM
    L = gathered["w_mix_in"].shape[0]
    small = {k: val[first:first + L] for k, val in small.items() if k != "norm_final"}
    gathered = dict(gathered, conv_w=conv_w[first:first + L])
    w_in = jnp.concatenate([gathered["w_mix_in"][:, b] for b in range(N_CHIPS)], axis=2)
    wqkv, wrest = w_in[:, :, :3 * DA], w_in[:, :, 3 * DA + H:]
    wf = jnp.pad(w_in[:, :, 3 * DA:3 * DA + H], ((0, 0), (0, 0), (0, LANES - H)))
    ng = len(POOL_WINDOWS)
    same_group = jnp.eye(ng, dtype=bool)[None, :, None, :, None]
    wbd = jnp.where(same_group, small["w_pool"][:, :, :, None, :], 0.0).reshape(L, C, C)
    cw = jnp.concatenate([gathered["conv_w"][:, b] for b in range(N_CHIPS)], axis=2)
    return dict(
        g1=small["norm_ffn1"], gm=small["norm_mix"], g2=small["norm_ffn2"],
        w1in=gathered["w_ffn1_in"], w1out=gathered["w_ffn1_out"].reshape(L, -1, D),
        wp=jnp.concatenate([wqkv, wrest, wf], axis=2), wmixout=gathered["w_mix_out"].reshape(L, D, D),
        bias=jnp.pad(small["b_forget"], ((0, 0), (0, LANES - H))), wbd=wbd.astype(BF16), ps=small["pool_scale"],
        cw=jnp.pad(cw, ((0, 0), (0, 8 - CONV_WIDTH), (0, 0))),
    )


def _layer_params(l, W):
    P = {k: (W[k], l) for k in ("w1in", "w1out", "wp", "wmixout")}
    P.update({k: W[k][l][None] for k in ("g1", "gm", "g2", "bias", "ps")})
    P.update(wbd=W["wbd"][l], cw=W["cw"][l])
    return P


def _ffn_fwd(x, g, w_in, w_out, token=None):
    h, jac, act = _ffn_up(x, g, w_in, token)
    return _ffn_out(act, w_out, x)[0], (x, h, jac, act)


def _ffn_bwd(dres, saved, g, w_in, w_out, token=None):
    x, h, jac, act = saved
    dgu, dx, dg = _ffn_bwd_main(dres, jac, x, g, w_out, w_in, token)
    dw_out = _ffn_dw_out(act, dres)[0]
    dw_in = _ffn_dw_in(h, dgu)[0]
    return dx, dg, dw_in, dw_out.reshape(N_CHIPS, -1, dw_out.shape[1])


def _mixer_fwd(x, P, B, S, tq):
    T, D = x.shape
    DA, C, H = D // 2, D // 4, D // 2 // HEAD_DIM
    hn, qkv, rest, fl = _mix_up(x, P["gm"], P["wp"], (3 * DA, 4 * C))
    qkv, rest, fl = qkv.reshape(B, S, 3 * DA), rest.reshape(B, S, 4 * C), fl.reshape(B, S, LANES)
    drow = _decay_fwd(fl, P["bias"]).reshape(B, 8, S // tq, tq)
    o, lse = _attn_fwd(qkv, drow, H, tq)
    ypc, pooled = _mix_local_fwd(rest, P["wbd"], P["ps"], P["cw"])
    x_out = _mix_out([o.reshape(T, DA), ypc.reshape(T, 2 * C)], P["wmixout"], x)
    return x_out, (x, hn, qkv, rest, fl, drow, o, lse, pooled, ypc)


def _mixer_bwd(dres, saved, P, B, S, tq):
    x, hn, qkv, rest, fl, drow, o, lse, pooled, ypc = saved
    T, D = x.shape
    DA, C, H = D // 2, D // 4, D // 2 // HEAD_DIM
    dycat = _proj("mix_out_bwd", dres, P["wmixout"], F32, NT).reshape(B, S, D)
    dw_out = _rows_dw("mix_out_dw", [o.reshape(T, DA), ypc.reshape(T, 2 * C)], dres, BF16)
    dq, dk, dv, ddrow, ddcol = _attn_bwd(qkv, drow, o, lse, dycat, H, tq)
    dfl, dbias = _decay_bwd(ddrow.reshape(B, 8, S), ddcol, fl, P["bias"], H)
    drest, dwbd, dps, dcw = _mix_local_bwd(rest, pooled, dycat, P["wbd"], P["ps"], P["cw"])
    pieces = [a.reshape(T, a.shape[-1]) for a in (dq, dk, dv, drest, dfl)]
    dwp = _pieces_dw("mix_in_dw", hn, pieces, F32, 512)
    dx, dg = _mix_in_bwd(pieces, x, P["gm"], dres, P["wp"])
    n_q, n_r = 3 * DA, 4 * C
    dw_in = jnp.concatenate([dwp[:, :n_q], dwp[:, n_q + n_r:n_q + n_r + H], dwp[:, n_q:n_q + n_r]], axis=1)
    dw_in = dw_in.reshape(D, N_CHIPS, -1).transpose(1, 0, 2).astype(BF16)
    ng = len(POOL_WINDOWS)
    same_group = jnp.eye(ng, dtype=bool)[:, None, :, None]
    dw_pool = jnp.where(same_group, dwbd.reshape(ng, C // ng, ng, C // ng), 0.0).sum(axis=2)
    small = dict(norm_mix=dg[0], b_forget=dbias[0, :H], w_pool=dw_pool, pool_scale=dps[0], conv_w=dcw[:CONV_WIDTH])
    return dx, small, dw_in, dw_out.reshape(N_CHIPS, -1, D)


EARLY = ("w_ffn1_in", "w_ffn1_out", "w_mix_in", "w_mix_out")
FFN2 = ("w_ffn2_in", "w_ffn2_out")


def _local_step(x, target, small, conv_w, pipe):
    B, S, D = x.shape
    L = small["norm_ffn1"].shape[0]
    tq = _tile(S, 256)
    xt = x.reshape(B * S, D)
    saved, params = [], []
    for l in range(L):
        P = _layer_params(0, _prep_weights(small, pipe.weights(l, xt), conv_w, D, l))
        xt, s1 = _ffn_fwd(xt, P["g1"], P["w1in"], P["w1out"], pipe.token(l))
        xt, s2 = _mixer_fwd(xt, P, B, S, tq)
        w2 = pipe.weights_ffn2(l, xt)
        P.update(w2in=(w2["w_ffn2_in"], 0), w2out=(w2["w_ffn2_out"].reshape(1, -1, D), 0))
        xt, s3 = _ffn_fwd(xt, P["g2"], P["w2in"], P["w2out"])
        saved.append((s1, s2, s3))
        params.append(P)
    dres, dgf, loss = _final_loss(xt, small["norm_final"][None], target.reshape(B * S, D))
    sm = {k: [None] * L for k in SMALL if k != "norm_final"}
    token = None
    for l in reversed(range(L)):
        P, (s1, s2, s3) = params[l], saved[l]
        big = {}
        dres, dg2, big["w_ffn2_in"], big["w_ffn2_out"] = _ffn_bwd(dres, s3, P["g2"], P["w2in"], P["w2out"], token)
        dres, smix, big["w_mix_in"], big["w_mix_out"] = _mixer_bwd(dres, s2, P, B, S, tq)
        big = {k: val[None] for k, val in big.items()}
        token = pipe.grads(l, FFN2 + EARLY[2:], big, dres) if l == 0 else None
        dres, dg1, dw_in, dw_out = _ffn_bwd(dres, s1, P["g1"], P["w1in"], P["w1out"], token)
        big.update(w_ffn1_in=dw_in[None], w_ffn1_out=dw_out[None])
        sm["norm_ffn1"][l], sm["norm_ffn2"][l] = dg1[0], dg2[0]
        for k, val in smix.items():
            sm[k][l] = val
        token = pipe.grads(l, EARLY[:2] if l == 0 else BIG, big, big["w_ffn1_in"])
    sm = {k: jnp.stack(val) for k, val in sm.items()}
    sm["norm_final"] = dgf[0]
    return loss[0, 0], dres.reshape(B, S, D), sm


def _sibling_copies(refs, send_sems, recv_sems):
    n = len(refs) // 2
    x, y, c, _ = _place()
    return [pltpu.make_async_remote_copy(
        src_ref=refs[k], dst_ref=refs[n + k], send_sem=send_sems.at[k], recv_sem=recv_sems.at[k],
        device_id=(x, y, 1 - c), device_id_type=MESH) for k in range(n)]


class _Pipeline:
    def __init__(self, w):
        self.w, self.n_layers = w, w[BIG[0]].shape[0]
        first = _all_gather([_place_shard(w[k], BF16, 0, 1) for k in EARLY] + [_place_shard(w["conv_w"], F32)])
        self.conv_w = first[-1]
        self._ready = dict(zip(EARLY, first[:-1]))
        self._ffn2 = self._start_gather("0b", FFN2, 0, first[0])
        self._next = (1, self._start_gather("1", BIG, 1, self._ffn2[1][3]))
        self._reduce, self._swaps = None, []
        self.reduced = [dict() for _ in range(self.n_layers)]

    def _start_gather(self, tag, kinds, l, after):
        placed = [_place_shard(self.w[k], BF16, l, 1) for k in kinds]
        return kinds, _split_start(f"gather_start_{tag}", placed, 3 * len(kinds), _gather_copies, after)

    def _wait_gather(self, tag, started, after):
        kinds, (send_sems, recv_sems, bufs, _) = started
        return dict(zip(kinds, _split_wait(f"gather_wait_{tag}", send_sems, recv_sems, bufs, _gather_copies, after)))

    def token(self, l):
        return self._next[1][1][3] if self._next is not None and self._next[0] == l + 1 else None

    def weights(self, l, after):
        if l == 0:
            return self._ready
        self._layer = self._wait_gather(str(l), self._next[1], after)
        first = next(iter(self._layer.values()))
        self._next = (l + 1, self._start_gather(str(l + 1), BIG, l + 1, first)) if l + 1 < self.n_layers else None
        return self._layer

    def weights_ffn2(self, l, after):
        return self._wait_gather("0b", self._ffn2, after) if l == 0 else self._layer

    def _finish_reduce(self, after):
        if self._reduce is None:
            return None
        tag, l, kinds, (send_sems, recv_sems, bufs, _) = self._reduce
        n = len(kinds)
        bufs = _split_wait(f"reduce_wait_{tag}", send_sems, recv_sems, bufs, _exchange_copies, after)
        mine = [_add_blocks(p, o, half=False) for p, o in zip(bufs[:n], bufs[n:])]
        lands = [lax.empty(q.shape, q.dtype) for q in mine]
        self._swaps.append((tag, l, kinds, _split_start(f"swap_start_{tag}", mine + lands, n, _sibling_copies, mine[0])))
        self._reduce = None
        return self._swaps[-1][3][3]

    def grads(self, l, kinds, big, after):
        swap_token = self._finish_reduce(after)
        grads = [big[k] for k in kinds]
        tag = str(l) if len(kinds) == len(BIG) else f"{l}_{kinds[0][2:]}"
        lands = [lax.empty((3, g.shape[0]) + g.shape[2:], g.dtype) for g in grads]
        started = _split_start(f"reduce_start_{tag}", grads + lands, 3 * len(kinds), _exchange_copies,
                               grads[0] if swap_token is None else swap_token)
        self._reduce = (tag, l, kinds, started)
        self.last_token = started[3]
        return started[3]

    def finish(self, after, last=False):
        if last:
            self._finish_reduce(after)
        for tag, l, kinds, (send_sems, recv_sems, bufs, _) in self._swaps:
            n = len(kinds)
            bufs = _split_wait(f"swap_wait_{tag}", send_sems, recv_sems, bufs, _sibling_copies, after)
            self.reduced[l].update(zip(kinds, zip(bufs[:n], bufs[n:])))
        self._swaps = []
        return self.reduced


def _pack(parts, extra=()):
    flat = jnp.concatenate([p.reshape(-1) for p in parts] + [jnp.reshape(e, (1,)) for e in extra])
    n = -(-flat.shape[0] // (8 * LANES)) * 8
    return jnp.pad(flat, (0, n * LANES - flat.shape[0])).reshape(n, LANES)


def _unpack(buf, shapes):
    flat, out, at = buf.reshape(-1), [], 0
    for s in shapes:
        n = math.prod(s)
        out.append(flat[at:at + n].reshape(s))
        at += n
    return out, flat[at:]


def kernel(x, norm_ffn1, w_ffn1_in, w_ffn1_out, norm_mix, w_mix_in, b_forget, w_pool, pool_scale, conv_w, w_mix_out, norm_ffn2, w_ffn2_in, w_ffn2_out, norm_final, loss_target, m_norm_ffn1, m_w_ffn1_in, m_w_ffn1_out, m_norm_mix, m_w_mix_in, m_b_forget, m_w_pool, m_pool_scale, m_conv_w, m_w_mix_out, m_norm_ffn2, m_w_ffn2_in, m_w_ffn2_out, m_norm_final, v_norm_ffn1, v_w_ffn1_in, v_w_ffn1_out, v_norm_mix, v_w_mix_in, v_b_forget, v_w_pool, v_pool_scale, v_conv_w, v_w_mix_out, v_norm_ffn2, v_w_ffn2_in, v_w_ffn2_out, v_norm_final):
    w = dict(zip(WEIGHTS, (norm_ffn1, w_ffn1_in, w_ffn1_out, norm_mix, w_mix_in, b_forget, w_pool, pool_scale, conv_w, w_mix_out, norm_ffn2, w_ffn2_in, w_ffn2_out, norm_final)))
    m = dict(zip(WEIGHTS, (m_norm_ffn1, m_w_ffn1_in, m_w_ffn1_out, m_norm_mix, m_w_mix_in, m_b_forget, m_w_pool, m_pool_scale, m_conv_w, m_w_mix_out, m_norm_ffn2, m_w_ffn2_in, m_w_ffn2_out, m_norm_final)))
    v = dict(zip(WEIGHTS, (v_norm_ffn1, v_w_ffn1_in, v_w_ffn1_out, v_norm_mix, v_w_mix_in, v_b_forget, v_w_pool, v_pool_scale, v_conv_w, v_w_mix_out, v_norm_ffn2, v_w_ffn2_in, v_w_ffn2_out, v_norm_final)))
    block = 2 * lax.axis_index("x") + lax.axis_index("y")

    pipe = _Pipeline(w)
    small = {k: w[k] for k in SMALL}
    loss, grad_x, sm = _local_step(x, loss_target, small, pipe.conv_w, pipe)
    grads, order = {}, list(SMALL)
    total = _all_reduce_small(_pack([sm[k] for k in order], extra=(loss,)))
    parts, rest = _unpack(total, [sm[k].shape for k in order])
    grads.update(zip(order, parts))
    loss = rest[0]
    cs = conv_w.shape[2]
    grads["conv_w"] = lax.dynamic_slice_in_dim(grads["conv_w"], block * cs, cs, axis=2)

    delta, new_m, new_v = {}, {}, {}

    def big_adamw(k, reduced, token=None):
        two_d = lambda a: a.reshape(-1, a.shape[-1])
        pieces = [tuple(map(two_d, layer[k])) for layer in reduced]
        res = _adamw(two_d(w[k]), pieces, two_d(m[k]), two_d(v[k]), token)
        grads[k], delta[k], new_m[k], new_v[k] = [r.reshape(w[k].shape) for r in res]

    reduced = pipe.finish(grad_x)
    for k in BIG[2:]:
        big_adamw(k, reduced, pipe.last_token)
    packed = [_pack([t[k] for k in order]) for t in (w, grads, m, v)]
    _, d, nm, nv = _adamw(packed[0], [packed[1]], packed[2], packed[3])
    shapes = [w[k].shape for k in order]
    for res, flat in ((delta, d), (new_m, nm), (new_v, nv)):
        res.update(zip(order, _unpack(flat, shapes)[0]))
    reduced = pipe.finish(new_v[BIG[-1]], last=True)
    for k in BIG[:2]:
        big_adamw(k, reduced)
    return (loss, grad_x, *[grads[k] for k in WEIGHTS], *[delta[k] for k in WEIGHTS],
            *[new_m[k] for k in WEIGHTS], *[new_v[k] for k in WEIGHTS])
```

```python
import functools
import math

import jax
import jax.numpy as jnp
from jax import lax
from jax.experimental import pallas as pl
from jax.experimental.pallas import tpu as pltpu

F32 = jnp.float32
BF16 = jnp.bfloat16
MESH = pl.DeviceIdType.MESH

HEAD_DIM = 64
POOL_WINDOWS = (2, 4, 8, 16)
CONV_WIDTH = 3
RMS_EPS = 1e-6
ADAM_LR = 0.001
ADAM_B1 = 0.9
ADAM_B2 = 0.999
ADAM_EPS = 1e-08
ADAM_WD = 0.01
ADAM_STEP = 10

LANES = 128
VMEM_LIMIT = 56 * 1024 * 1024
N_CHIPS = 4
N_DEV = 8

NN = (((1,), (0,)), ((), ()))
NT = (((1,), (1,)), ((), ()))
TN = (((0,), (0,)), ((), ()))


def _tile(n, pref):
    for t in range(pref - pref % 16, 15, -16):
        if n % t == 0:
            return t
    return n


def _params(n_grid):
    return pltpu.CompilerParams(dimension_semantics=("arbitrary",) * n_grid, vmem_limit_bytes=VMEM_LIMIT)


def _dot(a, b, dims):
    return lax.dot_general(a, b, dims, preferred_element_type=F32)


def _mm(name, dims, operands, in_specs, out_shape, out_specs, grid, acc_shape, epilogue):
    n_in, n_out, nk = len(operands), len(out_shape), grid[-1]

    def kern(*refs):
        extras, outs = refs[2:n_in], refs[n_in:n_in + n_out]
        if nk == 1:
            epilogue(_dot(refs[0][...].astype(BF16), refs[1][...].astype(BF16), dims), extras, outs)
            return
        acc = refs[n_in + n_out]
        k = pl.program_id(len(grid) - 1)

        @pl.when(k == 0)
        def _():
            acc[...] = jnp.zeros_like(acc)

        acc[...] += _dot(refs[0][...].astype(BF16), refs[1][...].astype(BF16), dims)

        @pl.when(k == nk - 1)
        def _():
            epilogue(acc[...], extras, outs)

    return pl.pallas_call(
        kern, name=name, grid=grid, in_specs=in_specs, out_specs=out_specs, out_shape=out_shape,
        scratch_shapes=[pltpu.VMEM(acc_shape, F32)] if nk > 1 else [],
        compiler_params=_params(len(grid)),
    )(*operands)


def _store(scale=None, dtype=None):
    def ep(acc, extras, outs):
        v = acc if scale is None else acc * scale
        outs[0][...] = v.astype(outs[0].dtype)
    return ep


def _residual(scale):
    def ep(acc, extras, outs):
        outs[0][...] = extras[0][...] + scale * acc
    return ep


def _rmsnorm_fwd(x, g):
    T, D = x.shape
    tr = _tile(T, 512)

    def kern(x_ref, g_ref, o_ref):
        xv = x_ref[...]
        r = lax.rsqrt(jnp.mean(xv * xv, axis=-1, keepdims=True) + RMS_EPS)
        o_ref[...] = (xv * r * g_ref[...]).astype(BF16)

    return pl.pallas_call(
        kern, name="rmsnorm_fwd", grid=(T // tr,),
        in_specs=[pl.BlockSpec((tr, D), lambda i: (i, 0)), pl.BlockSpec((1, D), lambda i: (0, 0))],
        out_specs=pl.BlockSpec((tr, D), lambda i: (i, 0)),
        out_shape=jax.ShapeDtypeStruct((T, D), BF16), compiler_params=_params(1),
    )(x, g)


def _rmsnorm_bwd(x, g, dh, dres):
    T, D = x.shape
    tr = _tile(T, 256)

    def kern(x_ref, g_ref, dh_ref, dres_ref, dx_ref, dg_ref):
        xv, dhv = x_ref[...], dh_ref[...]
        r = lax.rsqrt(jnp.mean(xv * xv, axis=-1, keepdims=True) + RMS_EPS)
        y = xv * r
        dy = dhv * g_ref[...]
        dx_ref[...] = dres_ref[...] + r * (dy - y * jnp.mean(dy * y, axis=-1, keepdims=True))
        part = jnp.sum(dhv * y, axis=0, keepdims=True)

        @pl.when(pl.program_id(0) == 0)
        def _():
            dg_ref[...] = part

        @pl.when(pl.program_id(0) > 0)
        def _():
            dg_ref[...] += part

    row = pl.BlockSpec((tr, D), lambda i: (i, 0))
    vec = pl.BlockSpec((1, D), lambda i: (0, 0))
    return pl.pallas_call(
        kern, name="rmsnorm_bwd", grid=(T // tr,), in_specs=[row, vec, row, row], out_specs=[row, vec],
        out_shape=[jax.ShapeDtypeStruct((T, D), F32), jax.ShapeDtypeStruct((1, D), F32)],
        compiler_params=_params(1),
    )(x, g, dh, dres)


def _final_loss(x, g, target):
    T, D = x.shape
    tr = _tile(T, 256)

    def kern(x_ref, g_ref, t_ref, dx_ref, dg_ref, loss_ref):
        xv = x_ref[...]
        r = lax.rsqrt(jnp.mean(xv * xv, axis=-1, keepdims=True) + RMS_EPS)
        y = xv * r
        err = y * g_ref[...] - t_ref[...]
        lpart = 0.5 * jnp.sum(jnp.mean(err * err, axis=-1, keepdims=True), axis=0, keepdims=True)
        dh = err * (1.0 / D)
        dy = dh * g_ref[...]
        dx_ref[...] = r * (dy - y * jnp.mean(dy * y, axis=-1, keepdims=True))
        part = jnp.sum(dh * y, axis=0, keepdims=True)
        lrow = jnp.broadcast_to(lpart, (1, LANES))

        @pl.when(pl.program_id(0) == 0)
        def _():
            dg_ref[...] = part
            loss_ref[...] = lrow

        @pl.when(pl.program_id(0) > 0)
        def _():
            dg_ref[...] += part
            loss_ref[...] += lrow

    row = pl.BlockSpec((tr, D), lambda i: (i, 0))
    vec = pl.BlockSpec((1, D), lambda i: (0, 0))
    return pl.pallas_call(
        kern, name="final_loss", grid=(T // tr,), in_specs=[row, vec, row],
        out_specs=[row, vec, pl.BlockSpec((1, LANES), lambda i: (0, 0))],
        out_shape=[jax.ShapeDtypeStruct((T, D), F32), jax.ShapeDtypeStruct((1, D), F32),
                   jax.ShapeDtypeStruct((1, LANES), F32)],
        compiler_params=_params(1),
    )(x, g, target)


def _ffn_in(h, w4):
    T, D = h.shape
    w4, l = w4
    Fh = w4.shape[3]
    F = 2 * Fh
    tm = _tile(T, 512)

    def kern(h_ref, wg_ref, wu_ref, jac_ref, act_ref):
        hv = h_ref[...]
        gate = _dot(hv, wg_ref[...], NN)
        up = _dot(hv, wu_ref[...], NN)
        sg = jax.nn.sigmoid(gate)
        silu = gate * sg
        jac_ref[0] = (up * (sg + silu * (1.0 - sg))).astype(BF16)
        jac_ref[1] = silu.astype(BF16)
        act_ref[...] = (silu * up).astype(BF16)

    return pl.pallas_call(
        kern, name="ffn_in", grid=(2, T // tm),
        in_specs=[pl.BlockSpec((tm, D), lambda j, i: (i, 0)),
                  pl.BlockSpec((None, None, D, Fh), lambda j, i: (l, j, 0, 0)),
                  pl.BlockSpec((None, None, D, Fh), lambda j, i: (l, 2 + j, 0, 0))],
        out_specs=[pl.BlockSpec((2, tm, Fh), lambda j, i: (0, i, j)),
                   pl.BlockSpec((tm, Fh), lambda j, i: (i, j))],
        out_shape=[jax.ShapeDtypeStruct((2, T, F), BF16), jax.ShapeDtypeStruct((T, F), BF16)],
        compiler_params=_params(2),
    )(h, w4, w4)


def _resident(shape, index_map):
    return pl.BlockSpec(shape, index_map, pipeline_mode=pl.Buffered(1))


def _token_operand(token):
    return ([], []) if token is None else ([token], [pl.BlockSpec(token.shape, lambda i: (0, 0))])


def _ffn_up(x, g, w4, token=None):
    T, D = x.shape
    w4, l = w4
    Fh = w4.shape[3]
    F = 2 * Fh
    tm = _tile(T, 512)
    tok_ops, tok_specs = _token_operand(token)

    def kern(x_ref, g_ref, w_ref, *rest):
        h_ref, jac_ref, act_ref = rest[len(tok_ops):]
        xv = x_ref[...]
        r = lax.rsqrt(jnp.mean(xv * xv, axis=-1, keepdims=True) + RMS_EPS)
        hv = (xv * r * g_ref[...]).astype(BF16)
        h_ref[...] = hv
        for j in range(2):
            cols = slice(j * Fh, (j + 1) * Fh)
            gate = _dot(hv, w_ref[j], NN)
            up = _dot(hv, w_ref[2 + j], NN)
            sg = jax.nn.sigmoid(gate)
            silu = gate * sg
            jac_ref[0, :, cols] = (up * (sg + silu * (1.0 - sg))).astype(BF16)
            jac_ref[1, :, cols] = silu.astype(BF16)
            act_ref[:, cols] = (silu * up).astype(BF16)

    return pl.pallas_call(
        kern, name="ffn_up", grid=(T // tm,),
        in_specs=[pl.BlockSpec((tm, D), lambda i: (i, 0)), pl.BlockSpec((1, D), lambda i: (0, 0)),
                  _resident((None, 4, D, Fh), lambda i: (l, 0, 0, 0))] + tok_specs,
        out_specs=[pl.BlockSpec((tm, D), lambda i: (i, 0)), pl.BlockSpec((2, tm, F), lambda i: (0, i, 0)),
                   pl.BlockSpec((tm, F), lambda i: (i, 0))],
        out_shape=[jax.ShapeDtypeStruct((T, D), BF16), jax.ShapeDtypeStruct((2, T, F), BF16),
                   jax.ShapeDtypeStruct((T, F), BF16)],
        compiler_params=_params(1),
    )(x, g, w4, *tok_ops)


def _ffn_bwd_main(dres, jac, x, g, w_out, w4, token=None):
    T, D = dres.shape
    w_out, l = w_out
    w4, _ = w4
    F = w_out.shape[1]
    Fh = F // 2
    tm = _tile(T, 256)
    tok_ops, tok_specs = _token_operand(token)

    def kern(d_ref, jac_ref, x_ref, g_ref, wo_ref, wi_ref, *rest):
        dgu_ref, dx_ref, dg_ref = rest[len(tok_ops):]
        dv = d_ref[...]
        d16 = dv.astype(BF16)
        dh = jnp.zeros((tm, D), F32)
        for j in range(2):
            cols = slice(j * Fh, (j + 1) * Fh)
            dact = 0.5 * _dot(d16, wo_ref[cols, :], NT)
            dgate = (dact * jac_ref[0, :, cols].astype(F32)).astype(BF16)
            dup = (dact * jac_ref[1, :, cols].astype(F32)).astype(BF16)
            dgu_ref[0, :, cols] = dgate
            dgu_ref[1, :, cols] = dup
            dh = dh + _dot(dgate, wi_ref[j], NT) + _dot(dup, wi_ref[2 + j], NT)
        xv = x_ref[...]
        r = lax.rsqrt(jnp.mean(xv * xv, axis=-1, keepdims=True) + RMS_EPS)
        y = xv * r
        dy = dh * g_ref[...]
        dx_ref[...] = dv + r * (dy - y * jnp.mean(dy * y, axis=-1, keepdims=True))
        part = jnp.sum(dh * y, axis=0, keepdims=True)

        @pl.when(pl.program_id(0) == 0)
        def _():
            dg_ref[...] = part

        @pl.when(pl.program_id(0) > 0)
        def _():
            dg_ref[...] += part

    row = pl.BlockSpec((tm, D), lambda i: (i, 0))
    vec = pl.BlockSpec((1, D), lambda i: (0, 0))
    wide = pl.BlockSpec((2, tm, F), lambda i: (0, i, 0))
    return pl.pallas_call(
        kern, name="ffn_bwd_main", grid=(T // tm,),
        in_specs=[row, wide, row, vec, _resident((None, F, D), lambda i: (l, 0, 0)),
                  _resident((None, 4, D, Fh), lambda i: (l, 0, 0, 0))] + tok_specs,
        out_specs=[wide, row, vec],
        out_shape=[jax.ShapeDtypeStruct((2, T, F), BF16), jax.ShapeDtypeStruct((T, D), F32),
                   jax.ShapeDtypeStruct((1, D), F32)],
        compiler_params=_params(1),
    )(dres, jac, x, g, w_out, w4, *tok_ops)


def _ffn_out(act, w_out, x):
    T, F = act.shape
    w_out, l = w_out
    D = w_out.shape[2]
    tm = _tile(T, 512)
    return _mm("ffn_out", NN, [act, w_out, x],
               [pl.BlockSpec((tm, F), lambda i, k: (i, 0)), pl.BlockSpec((None, F, D), lambda i, k: (l, 0, 0)),
                pl.BlockSpec((tm, D), lambda i, k: (i, 0))],
               [jax.ShapeDtypeStruct((T, D), F32)], [pl.BlockSpec((tm, D), lambda i, k: (i, 0))],
               (T // tm, 1), None, _residual(0.5))


def _ffn_bwd_act(dres, w_out, jac):
    T, D = dres.shape
    w_out, l = w_out
    F = w_out.shape[1]
    Fh = F // 2
    tm = _tile(T, 512)

    def kern(d_ref, w_ref, jac_ref, o_ref):
        dact = 0.5 * _dot(d_ref[...].astype(BF16), w_ref[...], NT)
        o_ref[0] = (dact * jac_ref[0].astype(F32)).astype(BF16)
        o_ref[1] = (dact * jac_ref[1].astype(F32)).astype(BF16)

    return pl.pallas_call(
        kern, name="ffn_bwd_act", grid=(2, T // tm),
        in_specs=[pl.BlockSpec((tm, D), lambda j, i: (i, 0)), pl.BlockSpec((None, Fh, D), lambda j, i: (l, j, 0)),
                  pl.BlockSpec((2, tm, Fh), lambda j, i: (0, i, j))],
        out_specs=pl.BlockSpec((2, tm, Fh), lambda j, i: (0, i, j)),
        out_shape=jax.ShapeDtypeStruct((2, T, F), BF16), compiler_params=_params(2),
    )(dres, w_out, jac)


def _ffn_dw_out(act, dres):
    T, F = act.shape
    D = dres.shape[1]
    tm, tk = F // 2, _tile(T, 1024)
    return _mm("ffn_dw_out", TN, [act, dres],
               [pl.BlockSpec((tk, tm), lambda i, k: (k, i)), pl.BlockSpec((tk, D), lambda i, k: (k, 0))],
               [jax.ShapeDtypeStruct((F, D), BF16)], [pl.BlockSpec((tm, D), lambda i, k: (i, 0))],
               (2, T // tk), (tm, D), _store(0.5))


def _ffn_dw_in(h, dgu):
    T, D = h.shape
    Fh = dgu.shape[2] // 2
    tk = _tile(T, 1024)
    return _mm("ffn_dw_in", TN, [h, dgu],
               [pl.BlockSpec((tk, D), lambda j, k: (k, 0)),
                pl.BlockSpec((None, tk, Fh), lambda j, k: (j // 2, k, j % 2))],
               [jax.ShapeDtypeStruct((4, D, Fh), BF16)], [pl.BlockSpec((None, D, Fh), lambda j, k: (j, 0, 0))],
               (4, T // tk), (D, Fh), _store())


def _ffn_dh(dgu, w4):
    T = dgu.shape[1]
    w4, l = w4
    D, Fh = w4.shape[2], w4.shape[3]
    tm = _tile(T, 1024)
    return _mm("ffn_dh", NT, [dgu, w4],
               [pl.BlockSpec((None, tm, Fh), lambda i, k: (k // 2, i, k % 2)),
                pl.BlockSpec((None, None, D, Fh), lambda i, k: (l, k, 0, 0))],
               [jax.ShapeDtypeStruct((T, D), F32)], [pl.BlockSpec((tm, D), lambda i, k: (i, 0))],
               (T // tm, 4), (tm, D), _store())


def _proj(name, a, w, out_dtype, dims=NN, extra=None, scale=None):
    T, K = a.shape
    w, l = w
    N = w.shape[2] if dims == NN else w.shape[1]
    tm = _tile(T, 512)
    ops = [a, w] + ([extra] if extra is not None else [])
    specs = [pl.BlockSpec((tm, K), lambda i, k: (i, 0)), pl.BlockSpec((None,) + w.shape[1:], lambda i, k: (l, 0, 0))]
    if extra is not None:
        specs.append(pl.BlockSpec((tm, N), lambda i, k: (i, 0)))
    ep = _residual(1.0) if extra is not None else _store(scale)
    return _mm(name, dims, ops, specs, [jax.ShapeDtypeStruct((T, N), out_dtype)],
               [pl.BlockSpec((tm, N), lambda i, k: (i, 0))], (T // tm, 1), None, ep)[0]


def _mix_up(x, g, wp, widths):
    T, D = x.shape
    wp, l = wp
    n_qkv, n_rest = widths
    NP = wp.shape[2]
    tm = _tile(T, 512)

    def kern(x_ref, g_ref, w_ref, h_ref, qkv_ref, rest_ref, fl_ref):
        xv = x_ref[...]
        r = lax.rsqrt(jnp.mean(xv * xv, axis=-1, keepdims=True) + RMS_EPS)
        hv = (xv * r * g_ref[...]).astype(BF16)
        h_ref[...] = hv
        qkv_ref[...] = _dot(hv, w_ref[:, 0:n_qkv], NN).astype(BF16)
        rest_ref[...] = _dot(hv, w_ref[:, n_qkv:n_qkv + n_rest], NN)
        fl_ref[...] = _dot(hv, w_ref[:, n_qkv + n_rest:NP], NN)

    row = lambda n: pl.BlockSpec((tm, n), lambda i: (i, 0))
    return pl.pallas_call(
        kern, name="mix_up", grid=(T // tm,),
        in_specs=[row(D), pl.BlockSpec((1, D), lambda i: (0, 0)), _resident((None, D, NP), lambda i: (l, 0, 0))],
        out_specs=[row(D), row(n_qkv), row(n_rest), row(LANES)],
        out_shape=[jax.ShapeDtypeStruct((T, D), BF16), jax.ShapeDtypeStruct((T, n_qkv), BF16),
                   jax.ShapeDtypeStruct((T, n_rest), F32), jax.ShapeDtypeStruct((T, LANES), F32)],
        compiler_params=_params(1),
    )(x, g, wp)


def _column_starts(pieces):
    starts, at = [], 0
    for p in pieces:
        starts.append(at)
        at += p.shape[1]
    return starts


def _mix_in_bwd(pieces, x, g, dres, wp):
    T, D = x.shape
    wp, l = wp
    NP = wp.shape[2]
    tm = _tile(T, 512)
    n, starts = len(pieces), _column_starts(pieces)

    def kern(*refs):
        x_ref, g_ref, d_ref, w_ref, dx_ref, dg_ref = refs[n:]
        dh = jnp.zeros((tm, D), F32)
        for p_ref, at in zip(refs[:n], starts):
            dh = dh + _dot(p_ref[...].astype(BF16), w_ref[:, at:at + p_ref.shape[1]], NT)
        xv = x_ref[...]
        r = lax.rsqrt(jnp.mean(xv * xv, axis=-1, keepdims=True) + RMS_EPS)
        y = xv * r
        dy = dh * g_ref[...]
        dx_ref[...] = d_ref[...] + r * (dy - y * jnp.mean(dy * y, axis=-1, keepdims=True))
        part = jnp.sum(dh * y, axis=0, keepdims=True)

        @pl.when(pl.program_id(0) == 0)
        def _():
            dg_ref[...] = part

        @pl.when(pl.program_id(0) > 0)
        def _():
            dg_ref[...] += part

    row = lambda n: pl.BlockSpec((tm, n), lambda i: (i, 0))
    vec = pl.BlockSpec((1, D), lambda i: (0, 0))
    return pl.pallas_call(
        kern, name="mix_in_bwd", grid=(T // tm,),
        in_specs=[row(p.shape[1]) for p in pieces] + [row(D), vec, row(D), _resident((None, D, NP), lambda i: (l, 0, 0))],
        out_specs=[row(D), vec],
        out_shape=[jax.ShapeDtypeStruct((T, D), F32), jax.ShapeDtypeStruct((1, D), F32)],
        compiler_params=_params(1),
    )(*pieces, x, g, dres, wp)


def _pieces_dw(name, a, pieces, out_dtype, tk_pref):
    T, M = a.shape
    n, starts = len(pieces), _column_starts(pieces)
    N = starts[-1] + pieces[-1].shape[1]
    tk = _tile(T, tk_pref)
    nk = T // tk

    def kern(a_ref, *refs):
        o_ref, acc = refs[n], refs[n + 1]
        k = pl.program_id(0)

        @pl.when(k == 0)
        def _():
            acc[...] = jnp.zeros_like(acc)

        av = a_ref[...].astype(BF16)
        for p_ref, at in zip(refs[:n], starts):
            acc[:, at:at + p_ref.shape[1]] += _dot(av, p_ref[...].astype(BF16), TN)

        @pl.when(k == nk - 1)
        def _():
            o_ref[...] = acc[...].astype(out_dtype)

    return pl.pallas_call(
        kern, name=name, grid=(nk,),
        in_specs=[pl.BlockSpec((tk, M), lambda k: (k, 0))] + [pl.BlockSpec((tk, p.shape[1]), lambda k: (k, 0)) for p in pieces],
        out_specs=pl.BlockSpec((M, N), lambda k: (0, 0)), out_shape=jax.ShapeDtypeStruct((M, N), out_dtype),
        scratch_shapes=[pltpu.VMEM((M, N), F32)], compiler_params=_params(1),
    )(a, *pieces)


def _rows_dw(name, pieces, d, out_dtype):
    T, N = d.shape
    n, starts = len(pieces), _column_starts(pieces)
    M = starts[-1] + pieces[-1].shape[1]
    tk = _tile(T, 1024)
    nk = T // tk

    def kern(*refs):
        d_ref, o_ref, acc = refs[n], refs[n + 1], refs[n + 2]
        k = pl.program_id(0)

        @pl.when(k == 0)
        def _():
            acc[...] = jnp.zeros_like(acc)

        dv = d_ref[...].astype(BF16)
        for p_ref, at in zip(refs[:n], starts):
            acc[at:at + p_ref.shape[1], :] += _dot(p_ref[...], dv, TN)

        @pl.when(k == nk - 1)
        def _():
            o_ref[...] = acc[...].astype(out_dtype)

    return pl.pallas_call(
        kern, name=name, grid=(nk,),
        in_specs=[pl.BlockSpec((tk, p.shape[1]), lambda k: (k, 0)) for p in pieces] + [pl.BlockSpec((tk, N), lambda k: (k, 0))],
        out_specs=pl.BlockSpec((M, N), lambda k: (0, 0)), out_shape=jax.ShapeDtypeStruct((M, N), out_dtype),
        scratch_shapes=[pltpu.VMEM((M, N), F32)], compiler_params=_params(1),
    )(*pieces, d)


def _mix_out(pieces, w, x):
    T, D = x.shape
    w, l = w
    n, starts = len(pieces), _column_starts(pieces)
    tm = _tile(T, 512)

    def kern(*refs):
        w_ref, x_ref, o_ref = refs[n:]
        acc = x_ref[...]
        for p_ref, at in zip(refs[:n], starts):
            acc = acc + _dot(p_ref[...], w_ref[at:at + p_ref.shape[1], :], NN)
        o_ref[...] = acc

    row = lambda m: pl.BlockSpec((tm, m), lambda i: (i, 0))
    return pl.pallas_call(
        kern, name="mix_out", grid=(T // tm,),
        in_specs=[row(p.shape[1]) for p in pieces] + [_resident((None,) + w.shape[1:], lambda i: (l, 0, 0)), row(D)],
        out_specs=row(D), out_shape=jax.ShapeDtypeStruct((T, D), F32), compiler_params=_params(1),
    )(*pieces, w, x)


def _dw(name, a, d, out_dtype):
    T, M = a.shape
    N = d.shape[1]
    tk = _tile(T, 1024 if M * N <= 1024 * 1408 else 512)
    return _mm(name, TN, [a, d],
               [pl.BlockSpec((tk, M), lambda i, k: (k, 0)), pl.BlockSpec((tk, N), lambda i, k: (k, 0))],
               [jax.ShapeDtypeStruct((M, N), out_dtype)], [pl.BlockSpec((M, N), lambda i, k: (0, 0))],
               (1, T // tk), (M, N), _store())[0]


def _log_sigmoid(z):
    return jnp.minimum(z, 0.0) - jnp.log(1.0 + jnp.exp(-jnp.abs(z)))


def _decay_fwd(fl, bias):
    B, S, _ = fl.shape

    def kern(fl_ref, b_ref, o_ref):
        d = _log_sigmoid(fl_ref[...] + b_ref[...])
        row = lax.broadcasted_iota(jnp.int32, (S, LANES), 0)
        sh = 1
        while sh < S:
            d = d + jnp.where(row >= sh, pltpu.roll(d, sh, 0), 0.0)
            sh *= 2
        o_ref[...] = d.T[0:8, :]

    return pl.pallas_call(
        kern, name="decay_fwd", grid=(B,),
        in_specs=[pl.BlockSpec((None, S, LANES), lambda b: (b, 0, 0)), pl.BlockSpec((1, LANES), lambda b: (0, 0))],
        out_specs=pl.BlockSpec((None, 8, S), lambda b: (b, 0, 0)),
        out_shape=jax.ShapeDtypeStruct((B, 8, S), F32), compiler_params=_params(1),
    )(fl, bias)


def _decay_bwd(ddrow, ddcol, fl, bias, n_heads):
    B, S, _ = fl.shape

    def kern(dd_ref, ddc_ref, fl_ref, b_ref, dfl_ref, db_ref):
        dd = jnp.concatenate([dd_ref[...], jnp.zeros((LANES - 8, S), F32)], axis=0).T + ddc_ref[...]
        row = lax.broadcasted_iota(jnp.int32, (S, LANES), 0)
        lane = lax.broadcasted_iota(jnp.int32, (S, LANES), 1)
        sh = 1
        while sh < S:
            dd = dd + jnp.where(row < S - sh, pltpu.roll(dd, S - sh, 0), 0.0)
            sh *= 2
        z = fl_ref[...] + b_ref[...]
        dfl = jnp.where(lane < n_heads, dd / (1.0 + jnp.exp(z)), 0.0)
        dfl_ref[...] = dfl
        part = jnp.sum(dfl, axis=0, keepdims=True)

        @pl.when(pl.program_id(0) == 0)
        def _():
            db_ref[...] = part

        @pl.when(pl.program_id(0) > 0)
        def _():
            db_ref[...] += part

    return pl.pallas_call(
        kern, name="decay_bwd", grid=(B,),
        in_specs=[pl.BlockSpec((None, 8, S), lambda b: (b, 0, 0)), pl.BlockSpec((None, S, LANES), lambda b: (b, 0, 0)),
                  pl.BlockSpec((None, S, LANES), lambda b: (b, 0, 0)), pl.BlockSpec((1, LANES), lambda b: (0, 0))],
        out_specs=[pl.BlockSpec((None, S, LANES), lambda b: (b, 0, 0)), pl.BlockSpec((1, LANES), lambda b: (0, 0))],
        out_shape=[jax.ShapeDtypeStruct((B, S, LANES), F32), jax.ShapeDtypeStruct((1, LANES), F32)],
        compiler_params=_params(1),
    )(ddrow, ddcol, fl, bias)


def _attn_fwd(qkv, drow, n_heads, tq):
    B, S, _ = qkv.shape
    DA = n_heads * HEAD_DIM
    scale = HEAD_DIM ** -0.5

    n_pairs = n_heads // 2

    def kern(q_ref, k_ref, v_ref, dr_ref, o_ref, lse_ref):
        i = pl.program_id(1)
        lane = lax.broadcasted_iota(jnp.int32, (tq, LANES), 1)
        low = lane < HEAD_DIM
        causal = lax.broadcasted_iota(jnp.int32, (tq, tq), 1) <= lax.broadcasted_iota(jnp.int32, (tq, tq), 0)
        qms = []
        for p in range(n_pairs):
            q2 = q_ref[:, LANES * p:LANES * (p + 1)] * scale
            qms += [jnp.where(low, q2, jnp.zeros_like(q2)), jnp.where(low, jnp.zeros_like(q2), q2)]

        def step(j, carry, masked):
            ms, ls, accs = carry
            ks = pl.multiple_of(j * tq, tq)
            new_m, new_l, new_acc = [], [], []
            for p in range(n_pairs):
                cols = slice(LANES * p, LANES * (p + 1))
                k2, v2 = k_ref[pl.ds(ks, tq), cols], v_ref[pl.ds(ks, tq), cols]
                alphas, pvs = [], []
                for h in (2 * p, 2 * p + 1):
                    s = _dot(qms[h], k2, NT) - dr_ref[h, pl.ds(j, 1), :]
                    if masked:
                        s = jnp.where(causal, s, -jnp.inf)
                    m_new = jnp.maximum(ms[h], jnp.max(s, axis=1, keepdims=True))
                    alpha = jnp.exp(ms[h] - m_new)
                    pm = jnp.exp(s - m_new)
                    new_m.append(m_new)
                    new_l.append(alpha * ls[h] + jnp.sum(pm, axis=1, keepdims=True))
                    alphas.append(alpha)
                    pvs.append(_dot(pm.astype(BF16), v2, NN))
                new_acc.append(jnp.where(low, alphas[0], alphas[1]) * accs[p] + jnp.where(low, pvs[0], pvs[1]))
            return tuple(new_m), tuple(new_l), tuple(new_acc)

        init = (tuple(jnp.full((tq, 1), -jnp.inf, F32) for _ in range(n_heads)),
                tuple(jnp.zeros((tq, 1), F32) for _ in range(n_heads)),
                tuple(jnp.zeros((tq, LANES), F32) for _ in range(n_pairs)))
        ms, ls, accs = step(i, lax.fori_loop(0, i, functools.partial(step, masked=False), init), True)
        lse_mat = jnp.zeros((tq, LANES), F32)
        for p in range(n_pairs):
            l0, l1 = ls[2 * p], ls[2 * p + 1]
            o_ref[:, LANES * p:LANES * (p + 1)] = (accs[p] / jnp.where(low, l0, l1)).astype(BF16)
            lse_mat = jnp.where(lane == 2 * p, ms[2 * p] + jnp.log(l0), lse_mat)
            lse_mat = jnp.where(lane == 2 * p + 1, ms[2 * p + 1] + jnp.log(l1), lse_mat)
        lse_ref[...] = lse_mat

    nq = S // tq
    return pl.pallas_call(
        kern, name="attn_fwd", grid=(B, nq),
        in_specs=[pl.BlockSpec((None, tq, DA), lambda b, i: (b, i, 0)),
                  pl.BlockSpec((None, S, DA), lambda b, i: (b, 0, 1)),
                  pl.BlockSpec((None, S, DA), lambda b, i: (b, 0, 2)),
                  pl.BlockSpec((None, 8, nq, tq), lambda b, i: (b, 0, 0, 0))],
        out_specs=[pl.BlockSpec((None, tq, DA), lambda b, i: (b, i, 0)),
                   pl.BlockSpec((None, tq, LANES), lambda b, i: (b, i, 0))],
        out_shape=[jax.ShapeDtypeStruct((B, S, DA), BF16), jax.ShapeDtypeStruct((B, S, LANES), F32)],
        compiler_params=_params(2),
    )(qkv, qkv, qkv, drow)


def _attn_bwd(qkv, drow, o, lse, dycat, n_heads, tq):
    B, S, _ = qkv.shape
    DA = n_heads * HEAD_DIM
    scale = HEAD_DIM ** -0.5
    nq = S // tq

    n_pairs = n_heads // 2

    def kern(q_ref, k_ref, v_ref, dr_ref, o_ref, lse_ref, do_ref, dq_ref, dk_ref, dv_ref, ddr_ref, ddc_ref,
             dk_acc, dv_acc, qm_s, dom_s, delta_s, rs_s, dq_s):
        i = pl.program_id(1)

        @pl.when(i == 0)
        def _():
            dk_acc[...] = jnp.zeros_like(dk_acc)
            dv_acc[...] = jnp.zeros_like(dv_acc)
            ddr_ref[...] = jnp.zeros_like(ddr_ref)

        lane = lax.broadcasted_iota(jnp.int32, (tq, LANES), 1)
        low = lane < HEAD_DIM
        causal = lax.broadcasted_iota(jnp.int32, (tq, tq), 1) <= lax.broadcasted_iota(jnp.int32, (tq, tq), 0)
        for p in range(n_pairs):
            cols = slice(LANES * p, LANES * (p + 1))
            q2 = q_ref[:, cols] * scale
            do_f = do_ref[:, cols]
            do2 = do_f.astype(BF16)
            prod = do_f * o_ref[:, cols].astype(F32)
            qm_s[2 * p] = jnp.where(low, q2, jnp.zeros_like(q2))
            qm_s[2 * p + 1] = jnp.where(low, jnp.zeros_like(q2), q2)
            dom_s[2 * p] = jnp.where(low, do2, jnp.zeros_like(do2))
            dom_s[2 * p + 1] = jnp.where(low, jnp.zeros_like(do2), do2)
            delta_s[2 * p] = jnp.sum(jnp.where(low, prod, 0.0), axis=1, keepdims=True)
            delta_s[2 * p + 1] = jnp.sum(jnp.where(low, 0.0, prod), axis=1, keepdims=True)
            dq_s[p] = jnp.zeros((tq, LANES), F32)
        rs_s[...] = jnp.zeros(rs_s.shape, F32)

        def step(j, masked):
            ks = pl.multiple_of(j * tq, tq)
            for p in range(n_pairs):
                cols = slice(LANES * p, LANES * (p + 1))
                k2, v2 = k_ref[pl.ds(ks, tq), cols], v_ref[pl.ds(ks, tq), cols]
                dvs, dks, dqs = [], [], []
                for h in (2 * p, 2 * p + 1):
                    qm, dom = qm_s[h], dom_s[h]
                    s = _dot(qm, k2, NT) - dr_ref[h, pl.ds(j, 1), :]
                    if masked:
                        s = jnp.where(causal, s, -jnp.inf)
                    pm = jnp.exp(s - lse_ref[:, h:h + 1])
                    ds = pm * (_dot(dom, v2, NT) - delta_s[h])
                    ddr_ref[h, pl.ds(j, 1), :] -= jnp.sum(ds, axis=0, keepdims=True)
                    rs_s[h] += jnp.sum(ds, axis=1, keepdims=True)
                    dsb = ds.astype(BF16)
                    dvs.append(_dot(pm.astype(BF16), dom, TN))
                    dks.append(_dot(dsb, qm, TN))
                    dqs.append(_dot(dsb, k2, NN))
                dv_acc[pl.ds(ks, tq), cols] += dvs[0] + dvs[1]
                dk_acc[pl.ds(ks, tq), cols] += dks[0] + dks[1]
                dq_s[p] += jnp.where(low, dqs[0], dqs[1])

        def body(j, carry):
            step(j, False)
            return carry

        lax.fori_loop(0, i, body, 0)
        step(i, True)
        ddc = jnp.zeros((tq, LANES), F32)
        for p in range(n_pairs):
            dq_ref[:, LANES * p:LANES * (p + 1)] = (dq_s[p] * scale).astype(BF16)
            ddc = jnp.where(lane == 2 * p, rs_s[2 * p], ddc)
            ddc = jnp.where(lane == 2 * p + 1, rs_s[2 * p + 1], ddc)
        ddc_ref[...] = ddc

        @pl.when(i == nq - 1)
        def _():
            dk_ref[...] = dk_acc[...].astype(BF16)
            dv_ref[...] = dv_acc[...].astype(BF16)

    tile = pl.BlockSpec((None, tq, DA), lambda b, i: (b, i, 0))
    seq = pl.BlockSpec((None, S, DA), lambda b, i: (b, 0, 0))
    dec = pl.BlockSpec((None, 8, nq, tq), lambda b, i: (b, 0, 0, 0))
    return pl.pallas_call(
        kern, name="attn_bwd", grid=(B, nq),
        in_specs=[tile, pl.BlockSpec((None, S, DA), lambda b, i: (b, 0, 1)),
                  pl.BlockSpec((None, S, DA), lambda b, i: (b, 0, 2)), dec, tile,
                  pl.BlockSpec((None, tq, LANES), lambda b, i: (b, i, 0)), tile],
        out_specs=[tile, seq, seq, dec, pl.BlockSpec((None, tq, LANES), lambda b, i: (b, i, 0))],
        out_shape=[jax.ShapeDtypeStruct((B, S, DA), BF16)] * 3 + [jax.ShapeDtypeStruct((B, 8, nq, tq), F32),
                                                                  jax.ShapeDtypeStruct((B, S, LANES), F32)],
        scratch_shapes=[pltpu.VMEM((S, DA), F32), pltpu.VMEM((S, DA), F32),
                        pltpu.VMEM((n_heads, tq, LANES), BF16), pltpu.VMEM((n_heads, tq, LANES), BF16),
                        pltpu.VMEM((n_heads, tq, 1), F32), pltpu.VMEM((n_heads, tq, 1), F32),
                        pltpu.VMEM((n_pairs, tq, LANES), F32)],
        compiler_params=_params(2),
    )(qkv, qkv, qkv, drow, o, lse, dycat)


def _down(v, d, row):
    return jnp.where(row >= d, pltpu.roll(v, d, 0), 0.0)


def _up(v, d, row, S):
    return jnp.where(row < S - d, pltpu.roll(v, S - d, 0), 0.0)


def _window(v, shift, group):
    sums, acc, d = [], v, 1
    for _ in POOL_WINDOWS:
        acc = acc + shift(acc, d)
        sums.append(acc)
        d *= 2
    out = sums[-1]
    for gi in range(len(POOL_WINDOWS) - 2, -1, -1):
        out = jnp.where(group == gi, sums[gi], out)
    return out


def _pool_count(row, group):
    w = jnp.full(row.shape, POOL_WINDOWS[-1], jnp.int32)
    for gi in range(len(POOL_WINDOWS) - 2, -1, -1):
        w = jnp.where(group == gi, POOL_WINDOWS[gi], w)
    return jnp.minimum(row + 1, w).astype(F32)


def _mix_local_fwd(rest, wbd, ps, cw):
    B, S, C4 = rest.shape
    C = C4 // 4
    gw = C // len(POOL_WINDOWS)

    def kern(r_ref, w_ref, ps_ref, cw_ref, y_ref, pooled_ref):
        row = lax.broadcasted_iota(jnp.int32, (S, C), 0)
        group = lax.broadcasted_iota(jnp.int32, (S, C), 1) // gw
        u = r_ref[:, 0:C]
        pooled = _window(u, lambda v, d: _down(v, d, row), group) / _pool_count(row, group) - u
        pb = pooled.astype(BF16)
        pooled_ref[...] = pb
        y_ref[:, 0:C] = (_dot(pb, w_ref[...], NN) * ps_ref[...]).astype(BF16)
        uc = r_ref[:, 2 * C:3 * C] * r_ref[:, 3 * C:4 * C]
        y = cw_ref[0:1, :] * _down(uc, 2, row) + cw_ref[1:2, :] * _down(uc, 1, row) + cw_ref[2:3, :] * uc
        y_ref[:, C:2 * C] = (r_ref[:, C:2 * C] * y).astype(BF16)

    return pl.pallas_call(
        kern, name="mix_local_fwd", grid=(B,),
        in_specs=[pl.BlockSpec((None, S, C4), lambda b: (b, 0, 0)), pl.BlockSpec((C, C), lambda b: (0, 0)),
                  pl.BlockSpec((1, C), lambda b: (0, 0)), pl.BlockSpec((8, C), lambda b: (0, 0))],
        out_specs=[pl.BlockSpec((None, S, 2 * C), lambda b: (b, 0, 0)), pl.BlockSpec((None, S, C), lambda b: (b, 0, 0))],
        out_shape=[jax.ShapeDtypeStruct((B, S, 2 * C), BF16), jax.ShapeDtypeStruct((B, S, C), BF16)],
        compiler_params=_params(1),
    )(rest, wbd, ps, cw)


def _mix_local_bwd(rest, pooled, dycat, wbd, ps, cw):
    B, S, C4 = rest.shape
    C = C4 // 4
    gw = C // len(POOL_WINDOWS)

    def kern(r_ref, pooled_ref, d_ref, w_ref, ps_ref, cw_ref, dr_ref, dw_ref, dps_ref, dcw_ref):
        row = lax.broadcasted_iota(jnp.int32, (S, C), 0)
        group = lax.broadcasted_iota(jnp.int32, (S, C), 1) // gw
        dyp = d_ref[:, 0:C]
        dyc = d_ref[:, C:2 * C]
        pb = pooled_ref[...]
        dps = jnp.sum(dyp * _dot(pb, w_ref[...], NN), axis=0, keepdims=True)
        dzb = (dyp * ps_ref[...]).astype(BF16)
        dw = _dot(pb, dzb, TN)
        dpooled = _dot(dzb, w_ref[...], NT)
        g = dpooled / _pool_count(row, group)
        dr_ref[:, 0:C] = (_window(g, lambda v, d: _up(v, d, row, S), group) - dpooled).astype(BF16)
        cc, ch = r_ref[:, 2 * C:3 * C], r_ref[:, 3 * C:4 * C]
        uc = cc * ch
        u1, u2 = _down(uc, 1, row), _down(uc, 2, row)
        y = cw_ref[0:1, :] * u2 + cw_ref[1:2, :] * u1 + cw_ref[2:3, :] * uc
        dr_ref[:, C:2 * C] = (dyc * y).astype(BF16)
        dy = dyc * r_ref[:, C:2 * C]
        duc = cw_ref[0:1, :] * _up(dy, 2, row, S) + cw_ref[1:2, :] * _up(dy, 1, row, S) + cw_ref[2:3, :] * dy
        dr_ref[:, 2 * C:3 * C] = (duc * ch).astype(BF16)
        dr_ref[:, 3 * C:4 * C] = (duc * cc).astype(BF16)
        dcw = jnp.concatenate([jnp.sum(dy * u2, axis=0, keepdims=True), jnp.sum(dy * u1, axis=0, keepdims=True),
                               jnp.sum(dy * uc, axis=0, keepdims=True), jnp.zeros((5, C), F32)], axis=0)

        @pl.when(pl.program_id(0) == 0)
        def _():
            dw_ref[...] = dw
            dps_ref[...] = dps
            dcw_ref[...] = dcw

        @pl.when(pl.program_id(0) > 0)
        def _():
            dw_ref[...] += dw
            dps_ref[...] += dps
            dcw_ref[...] += dcw

    full = lambda shape: pl.BlockSpec(shape, lambda b: (0, 0))
    return pl.pallas_call(
        kern, name="mix_local_bwd", grid=(B,),
        in_specs=[pl.BlockSpec((None, S, C4), lambda b: (b, 0, 0)), pl.BlockSpec((None, S, C), lambda b: (b, 0, 0)),
                  pl.BlockSpec((None, S, 2 * C), lambda b: (b, 0, 1)), full((C, C)), full((1, C)), full((8, C))],
        out_specs=[pl.BlockSpec((None, S, C4), lambda b: (b, 0, 0)), full((C, C)), full((1, C)), full((8, C))],
        out_shape=[jax.ShapeDtypeStruct((B, S, C4), BF16), jax.ShapeDtypeStruct((C, C), F32),
                   jax.ShapeDtypeStruct((1, C), F32), jax.ShapeDtypeStruct((8, C), F32)],
        compiler_params=_params(1),
    )(rest, pooled, dycat, wbd, ps, cw)


def _adamw(w, gs, m, v, token=None):
    R, C = w.shape
    pieces = [p if isinstance(p, tuple) else (p,) for p in gs]
    owner = [s for s, p in enumerate(pieces) for _ in p]
    flat = [a for p in pieces for a in p]
    n = len(flat)
    rows = R // len(pieces)
    tr = _tile(rows, 256)
    per = rows // tr
    tok_ops, tok_specs = _token_operand(token)

    def kern(w_ref, *refs):
        g_refs, (m_ref, v_ref), (g_out, d_ref, nm_ref, nv_ref) = refs[:n], refs[n:n + 2], refs[n + 2 + len(tok_ops):]
        vals, at = [], 0
        for p in pieces:
            vals.append(g_refs[at][...] if len(p) == 1 else g_refs[at][...] + g_refs[at + 1][...])
            at += len(p)
        gv = vals[0]
        for s in range(1, len(pieces)):
            gv = jnp.where(pl.program_id(0) // per == s, vals[s], gv)
        nm = ADAM_B1 * m_ref[...] + (1.0 - ADAM_B1) * gv
        nv = ADAM_B2 * v_ref[...] + (1.0 - ADAM_B2) * (gv * gv)
        m_hat = nm / (1.0 - ADAM_B1 ** ADAM_STEP)
        v_hat = nv / (1.0 - ADAM_B2 ** ADAM_STEP)
        g_out[...] = gv
        d_ref[...] = -ADAM_LR * (m_hat / (jnp.sqrt(v_hat) + ADAM_EPS) + ADAM_WD * w_ref[...])
        nm_ref[...] = nm
        nv_ref[...] = nv

    def piece(s):
        return pl.BlockSpec((tr, C), lambda i: (jnp.clip(i - s * per, 0, per - 1), 0))

    blk = pl.BlockSpec((tr, C), lambda i: (i, 0))
    return pl.pallas_call(
        kern, name="adamw", grid=(R // tr,), in_specs=[blk] + [piece(s) for s in owner] + [blk] * 2 + tok_specs,
        out_specs=[blk] * 4, out_shape=[jax.ShapeDtypeStruct((R, C), F32)] * 4, compiler_params=_params(1),
    )(w, *flat, m, v, *tok_ops)


def _place():
    x, y, c = lax.axis_index("x"), lax.axis_index("y"), lax.axis_index("c")
    return x, y, c, [(1 - x, y), (x, 1 - y), (1 - x, 1 - y)]


def _comm_call(name, body, operands, out_shape, n_sems, aliases=None):
    any_spec = pl.BlockSpec(memory_space=pl.ANY)
    return pl.pallas_call(
        body, name=name, in_specs=[any_spec] * len(operands), out_specs=[any_spec] * len(out_shape),
        out_shape=out_shape, input_output_aliases=aliases or {},
        scratch_shapes=[pltpu.SemaphoreType.DMA((n,)) for n in n_sems],
    )(*operands)


def _my_block():
    return 2 * lax.axis_index("x") + lax.axis_index("y")


def _place_shard(w, dtype, first=0, count=None):
    L, R, C = w.shape
    count = L if count is None else count
    tr = _tile(R, 512)

    def kern(w_ref, o_ref):
        o_ref[...] = w_ref[...].astype(dtype)

    return pl.pallas_call(
        kern, name="place_shard", grid=(count, R // tr),
        in_specs=[pl.BlockSpec((None, tr, C), lambda l, i: (first + l, i, 0))],
        out_specs=pl.BlockSpec((None, None, tr, C), lambda l, i: (l, _my_block(), i, 0)),
        out_shape=jax.ShapeDtypeStruct((count, N_CHIPS, R, C), dtype), compiler_params=_params(2),
    )(w)


HALF_ROWS = 16


def _rows(ref, half):
    hr = ref.shape[-2] // 2
    return ref.at[(slice(None),) * (len(ref.shape) - 2) + (pl.ds(half * hr, hr),)]


def _all_gather(bufs):
    n = len(bufs)

    def body(*refs):
        outs = refs[n:2 * n]
        send_sems, recv_sems = refs[2 * n:]
        x, y, c, chips = _place()
        sibling = (x, y, 1 - c)

        def remote(k, j, chip, half, to):
            blk = 2 * chip[0] + chip[1]
            if outs[k].shape[2] % (2 * HALF_ROWS) == 0:
                region = _rows(outs[k].at[:, blk], half)
            else:
                hl = outs[k].shape[0] // 2
                region = outs[k].at[pl.ds(half * hl, hl), blk]
            return pltpu.make_async_remote_copy(
                src_ref=region, dst_ref=region, send_sem=send_sems.at[6 * k + j],
                recv_sem=recv_sems.at[6 * k + j], device_id=to, device_id_type=MESH)

        first = [remote(k, j, (x, y), c, (*chip, c)) for k in range(n) for j, chip in enumerate(chips)]
        for cp in first:
            cp.start()
        passed = []
        for k in range(n):
            for j, chip in enumerate(chips):
                remote(k, j, chip, c, (x, y, c)).wait_recv()
                passed.append(remote(k, 3 + j, chip, c, sibling))
                passed[-1].start()
        for k in range(n):
            for j, chip in enumerate(chips):
                remote(k, 3 + j, chip, 1 - c, (x, y, c)).wait_recv()
        for cp in first + passed:
            cp.wait_send()

    out_shape = [jax.ShapeDtypeStruct(s.shape, s.dtype) for s in bufs]
    return _comm_call("all_gather_weights", body, bufs, out_shape, (6 * n, 6 * n), aliases={k: k for k in range(n)})


_HBM = pl.BlockSpec(memory_space=pltpu.HBM)
_SEM = pl.BlockSpec(memory_space=pltpu.SEMAPHORE)
_ANY = pl.BlockSpec(memory_space=pl.ANY)


def _split_start(name, bufs, n_copies, make_copies, after):
    n = len(bufs)

    def body(*refs):
        send_sems, recv_sems, token = refs[n + 1], refs[n + 2], refs[2 * n + 3]
        for cp in make_copies(refs[:n], send_sems, recv_sems):
            cp.start()
        token[...] = jnp.zeros_like(token)

    res = pl.pallas_call(
        body, name=name, in_specs=[_HBM] * n + [_ANY],
        out_shape=(pltpu.SemaphoreType.DMA((n_copies,)), pltpu.SemaphoreType.DMA((n_copies,)),
                   *[pltpu.HBM(b.shape, b.dtype) for b in bufs], jax.ShapeDtypeStruct((8, LANES), F32)),
        out_specs=(_SEM, _SEM, *[_HBM] * n, pl.BlockSpec(memory_space=pltpu.VMEM)),
        input_output_aliases={i: 2 + i for i in range(n)},
        compiler_params=pltpu.CompilerParams(has_side_effects=pltpu.SideEffectType.DATAFLOW_SIDE_EFFECTING),
    )(*[pltpu.with_memory_space_constraint(b, pltpu.HBM) for b in bufs], after)
    return res[0], res[1], list(res[2:2 + n]), res[2 + n]


def _split_wait(name, send_sems, recv_sems, bufs, make_copies, after):
    n = len(bufs)

    def body(*refs):
        for cp in make_copies(refs[:n], refs[n], refs[n + 1]):
            cp.wait_send()
            cp.wait_recv()

    return list(pl.pallas_call(
        body, name=name, in_specs=[_HBM] * n + [_SEM, _SEM, _ANY],
        out_shape=tuple(pltpu.HBM(b.shape, b.dtype) for b in bufs), out_specs=tuple([_HBM] * n),
        input_output_aliases={i: i for i in range(n)},
        compiler_params=pltpu.CompilerParams(has_side_effects=pltpu.SideEffectType.DATAFLOW_SIDE_EFFECTING),
    )(*bufs, send_sems, recv_sems, after))


def _gather_copies(refs, send_sems, recv_sems):
    x, y, c, chips = _place()
    return [pltpu.make_async_remote_copy(
        src_ref=ref.at[:, 2 * x + y], dst_ref=ref.at[:, 2 * x + y], send_sem=send_sems.at[3 * k + j],
        recv_sem=recv_sems.at[3 * k + j], device_id=(*chip, c), device_id_type=MESH)
        for k, ref in enumerate(refs) for j, chip in enumerate(chips)]


def _exchange_copies(refs, send_sems, recv_sems):
    n = len(refs) // 2
    x, y, c, chips = _place()
    return [pltpu.make_async_remote_copy(
        src_ref=refs[k].at[:, 2 * chip[0] + chip[1]], dst_ref=refs[n + k].at[j], send_sem=send_sems.at[3 * k + j],
        recv_sem=recv_sems.at[3 * k + j], device_id=(*chip, c), device_id_type=MESH)
        for k in range(n) for j, chip in enumerate(chips)]


def _rs_swap_halves(grads):
    n = len(grads)

    def body(*refs):
        ins, outs = refs[:n], refs[n:2 * n]
        send_sems, recv_sems = refs[2 * n:]
        x, y, c, _ = _place()
        copies = [pltpu.make_async_remote_copy(
            src_ref=_rows(ins[k], 1 - c), dst_ref=outs[k], send_sem=send_sems.at[k],
            recv_sem=recv_sems.at[k], device_id=(x, y, 1 - c), device_id_type=MESH) for k in range(n)]
        for cp in copies:
            cp.start()
        for cp in copies:
            cp.wait()

    out_shape = [jax.ShapeDtypeStruct(g.shape[:2] + (g.shape[2] // 2, g.shape[3]), g.dtype) for g in grads]
    return _comm_call("rs_swap_halves", body, grads, out_shape, (n, n))


def _rs_exchange(parts):
    n = len(parts)

    def body(*refs):
        ins, outs = refs[:n], refs[n:2 * n]
        send_sems, recv_sems = refs[2 * n:]
        x, y, c, chips = _place()
        copies = [pltpu.make_async_remote_copy(
            src_ref=ins[k].at[:, 2 * chip[0] + chip[1]], dst_ref=outs[k].at[j], send_sem=send_sems.at[3 * k + j],
            recv_sem=recv_sems.at[3 * k + j], device_id=(*chip, c), device_id_type=MESH)
            for k in range(n) for j, chip in enumerate(chips)]
        for cp in copies:
            cp.start()
        for cp in copies:
            cp.wait()

    out_shape = [jax.ShapeDtypeStruct((3, p.shape[0]) + p.shape[2:], p.dtype) for p in parts]
    return _comm_call("rs_exchange", body, parts, out_shape, (3 * n, 3 * n))


def _rs_share(bufs):
    n = len(bufs)

    def body(*refs):
        outs = refs[n:2 * n]
        send_sems, recv_sems = refs[2 * n:]
        x, y, c, _ = _place()

        def half(k, which):
            region = _rows(outs[k], which)
            return pltpu.make_async_remote_copy(
                src_ref=region, dst_ref=region, send_sem=send_sems.at[k], recv_sem=recv_sems.at[k],
                device_id=(x, y, 1 - c), device_id_type=MESH)

        sends = [half(k, c) for k in range(n)]
        for cp in sends:
            cp.start()
        for k in range(n):
            half(k, 1 - c).wait_recv()
        for cp in sends:
            cp.wait_send()

    out_shape = [jax.ShapeDtypeStruct(h.shape, h.dtype) for h in bufs]
    return _comm_call("rs_share", body, bufs, out_shape, (n, n), aliases={k: k for k in range(n)})


def _all_reduce_small(v, after):
    n = v.shape[0]

    def body(v_ref, after_ref, o_ref, gbuf, send_sems, recv_sems):
        x, y, c, _ = _place()
        me = 4 * x + 2 * y + c
        gbuf[me] = v_ref[...]
        copies, waits = [], []
        for r in range(1, N_DEV):
            px = 1 - x if r & 4 else x
            py = 1 - y if r & 2 else y
            pc = 1 - c if r & 1 else c
            mk = functools.partial(pltpu.make_async_remote_copy, src_ref=v_ref, send_sem=send_sems.at[r - 1],
                                   recv_sem=recv_sems.at[r - 1], device_id=(px, py, pc), device_id_type=MESH)
            copies.append(mk(dst_ref=gbuf.at[me]))
            waits.append(mk(dst_ref=gbuf.at[4 * px + 2 * py + pc]))
        for cp in copies:
            cp.start()
        for cp in waits:
            cp.wait_recv()
        for cp in copies:
            cp.wait_send()
        acc = gbuf[0]
        for d in range(1, N_DEV):
            acc = acc + gbuf[d]
        o_ref[...] = acc

    vm = pl.BlockSpec(memory_space=pltpu.VMEM)
    return pl.pallas_call(
        body, name="all_reduce_small", in_specs=[vm, _ANY], out_specs=vm, out_shape=jax.ShapeDtypeStruct(v.shape, F32),
        scratch_shapes=[pltpu.VMEM((N_DEV, n, LANES), F32), pltpu.SemaphoreType.DMA((N_DEV - 1,)),
                        pltpu.SemaphoreType.DMA((N_DEV - 1,))],
        compiler_params=pltpu.CompilerParams(vmem_limit_bytes=VMEM_LIMIT),
    )(v, after)


def _add_half(g, h1):
    L, nb, hr, C = h1.shape
    g3, h3 = g.reshape(L * nb, 2 * hr, C), h1.reshape(L * nb, hr, C)
    tr = _tile(hr, 512)

    def kern(g_ref, h_ref, o_ref):
        o_ref[...] = (g_ref[...].astype(F32) + h_ref[...].astype(F32)).astype(BF16)

    blk = pl.BlockSpec((None, tr, C), lambda l, i: (l, i, 0))
    out = pl.pallas_call(
        kern, name="rs_add_half", grid=(L * nb, hr // tr),
        in_specs=[pl.BlockSpec((None, tr, C), lambda l, i: (l, lax.axis_index("c") * (hr // tr) + i, 0)), blk],
        out_specs=blk, out_shape=jax.ShapeDtypeStruct(h3.shape, BF16), compiler_params=_params(2),
    )(g3, h3)
    return out.reshape(h1.shape)


def _add_blocks(p, h2, half=True):
    L, nb, hr, C = p.shape
    tr = _tile(hr, 512)
    shift = lambda: lax.axis_index("c") * (hr // tr) if half else 0

    def kern(p_ref, h0_ref, h1_ref, h2_ref, o_ref):
        o_ref[...] = ((p_ref[...].astype(F32) + h0_ref[...].astype(F32)) + h1_ref[...].astype(F32)) + h2_ref[...].astype(F32)

    def other(j):
        return pl.BlockSpec((None, None, tr, C), lambda l, i: (j, l, i, 0))

    return pl.pallas_call(
        kern, name="rs_add_blocks", grid=(L, hr // tr),
        in_specs=[pl.BlockSpec((None, None, tr, C), lambda l, i: (l, _my_block(), i, 0)), other(0), other(1), other(2)],
        out_specs=pl.BlockSpec((None, tr, C), lambda l, i: (l, shift() + i, 0)),
        out_shape=jax.ShapeDtypeStruct((L, (2 if half else 1) * hr, C), F32), compiler_params=_params(2),
    )(p, h2, h2, h2)


def _d2d_swap(arrays):
    n = len(arrays)

    def body(*refs):
        ins, outs = refs[:n], refs[n:2 * n]
        send_sems, recv_sems = refs[2 * n:]
        x, y, c, _ = _place()
        copies = [pltpu.make_async_remote_copy(
            src_ref=ins[k], dst_ref=outs[k], send_sem=send_sems.at[k], recv_sem=recv_sems.at[k],
            device_id=(x, y, 1 - c), device_id_type=MESH) for k in range(n)]
        for cp in copies:
            cp.start()
        for cp in copies:
            cp.wait()

    return _comm_call("d2d_swap", body, arrays, [jax.ShapeDtypeStruct(a.shape, a.dtype) for a in arrays], (n, n))


def _reduce_scatter(grads):
    sib = _rs_swap_halves(grads)
    parts = [_add_half(g, h) for g, h in zip(grads, sib)]
    others = _rs_exchange(parts)
    return _rs_share([_add_blocks(p, o) for p, o in zip(parts, others)])


WEIGHTS = ("norm_ffn1", "w_ffn1_in", "w_ffn1_out", "norm_mix", "w_mix_in", "b_forget", "w_pool", "pool_scale",
           "conv_w", "w_mix_out", "norm_ffn2", "w_ffn2_in", "w_ffn2_out", "norm_final")
BIG = ("w_ffn1_in", "w_ffn1_out", "w_mix_in", "w_mix_out", "w_ffn2_in", "w_ffn2_out")
SMALL = ("norm_ffn1", "norm_mix", "b_forget", "w_pool", "pool_scale", "conv_w", "norm_ffn2", "norm_final")


def _prep_weights(small, gathered, conv_w, D, first):
    DA, C, H = D // 2, D // 4, D // 2 // HEAD_DIM
    L = gathered["w_mix_in"].shape[0]
    small = {k: val[first:first + L] for k, val in small.items() if k != "norm_final"}
    gathered = dict(gathered, conv_w=conv_w[first:first + L])
    w_in = jnp.concatenate([gathered["w_mix_in"][:, b] for b in range(N_CHIPS)], axis=2)
    wqkv, wrest = w_in[:, :, :3 * DA], w_in[:, :, 3 * DA + H:]
    wf = jnp.pad(w_in[:, :, 3 * DA:3 * DA + H], ((0, 0), (0, 0), (0, LANES - H)))
    ng = len(POOL_WINDOWS)
    same_group = jnp.eye(ng, dtype=bool)[None, :, None, :, None]
    wbd = jnp.where(same_group, small["w_pool"][:, :, :, None, :], 0.0).reshape(L, C, C)
    cw = jnp.concatenate([gathered["conv_w"][:, b] for b in range(N_CHIPS)], axis=2)
    return dict(
        g1=small["norm_ffn1"], gm=small["norm_mix"], g2=small["norm_ffn2"],
        w1in=gathered["w_ffn1_in"], w1out=gathered["w_ffn1_out"].reshape(L, -1, D),
        wp=jnp.concatenate([wqkv, wrest, wf], axis=2), wmixout=gathered["w_mix_out"].reshape(L, D, D),
        bias=jnp.pad(small["b_forget"], ((0, 0), (0, LANES - H))), wbd=wbd.astype(BF16), ps=small["pool_scale"],
        cw=jnp.pad(cw, ((0, 0), (0, 8 - CONV_WIDTH), (0, 0))),
    )


def _layer_params(l, W):
    P = {k: (W[k], l) for k in ("w1in", "w1out", "wp", "wmixout")}
    P.update({k: W[k][l][None] for k in ("g1", "gm", "g2", "bias", "ps")})
    P.update(wbd=W["wbd"][l], cw=W["cw"][l])
    return P


def _ffn_fwd(x, g, w_in, w_out, token=None):
    h, jac, act = _ffn_up(x, g, w_in, token)
    return _ffn_out(act, w_out, x)[0], (x, h, jac, act)


def _ffn_bwd(dres, saved, g, w_in, w_out, token=None):
    x, h, jac, act = saved
    dgu, dx, dg = _ffn_bwd_main(dres, jac, x, g, w_out, w_in, token)
    dw_out = _ffn_dw_out(act, dres)[0]
    dw_in = _ffn_dw_in(h, dgu)[0]
    return dx, dg, dw_in, dw_out.reshape(N_CHIPS, -1, dw_out.shape[1])


def _mixer_fwd(x, P, B, S, tq):
    T, D = x.shape
    DA, C, H = D // 2, D // 4, D // 2 // HEAD_DIM
    hn, qkv, rest, fl = _mix_up(x, P["gm"], P["wp"], (3 * DA, 4 * C))
    qkv, rest, fl = qkv.reshape(B, S, 3 * DA), rest.reshape(B, S, 4 * C), fl.reshape(B, S, LANES)
    drow = _decay_fwd(fl, P["bias"]).reshape(B, 8, S // tq, tq)
    o, lse = _attn_fwd(qkv, drow, H, tq)
    ypc, pooled = _mix_local_fwd(rest, P["wbd"], P["ps"], P["cw"])
    x_out = _mix_out([o.reshape(T, DA), ypc.reshape(T, 2 * C)], P["wmixout"], x)
    return x_out, (x, hn, qkv, rest, fl, drow, o, lse, pooled, ypc)


def _mixer_bwd(dres, saved, P, B, S, tq):
    x, hn, qkv, rest, fl, drow, o, lse, pooled, ypc = saved
    T, D = x.shape
    DA, C, H = D // 2, D // 4, D // 2 // HEAD_DIM
    dycat = _proj("mix_out_bwd", dres, P["wmixout"], F32, NT).reshape(B, S, D)
    dw_out = _rows_dw("mix_out_dw", [o.reshape(T, DA), ypc.reshape(T, 2 * C)], dres, BF16)
    dq, dk, dv, ddrow, ddcol = _attn_bwd(qkv, drow, o, lse, dycat, H, tq)
    dfl, dbias = _decay_bwd(ddrow.reshape(B, 8, S), ddcol, fl, P["bias"], H)
    drest, dwbd, dps, dcw = _mix_local_bwd(rest, pooled, dycat, P["wbd"], P["ps"], P["cw"])
    pieces = [a.reshape(T, a.shape[-1]) for a in (dq, dk, dv, drest, dfl)]
    dwp = _pieces_dw("mix_in_dw", hn, pieces, F32, 512)
    dx, dg = _mix_in_bwd(pieces, x, P["gm"], dres, P["wp"])
    n_q, n_r = 3 * DA, 4 * C
    dw_in = jnp.concatenate([dwp[:, :n_q], dwp[:, n_q + n_r:n_q + n_r + H], dwp[:, n_q:n_q + n_r]], axis=1)
    dw_in = dw_in.reshape(D, N_CHIPS, -1).transpose(1, 0, 2).astype(BF16)
    ng = len(POOL_WINDOWS)
    same_group = jnp.eye(ng, dtype=bool)[:, None, :, None]
    dw_pool = jnp.where(same_group, dwbd.reshape(ng, C // ng, ng, C // ng), 0.0).sum(axis=2)
    small = dict(norm_mix=dg[0], b_forget=dbias[0, :H], w_pool=dw_pool, pool_scale=dps[0], conv_w=dcw[:CONV_WIDTH])
    return dx, small, dw_in, dw_out.reshape(N_CHIPS, -1, D)


EARLY = ("w_ffn1_in", "w_ffn1_out", "w_mix_in", "w_mix_out")
FFN2 = ("w_ffn2_in", "w_ffn2_out")


def _local_step(x, target, small, conv_w, pipe):
    B, S, D = x.shape
    L = small["norm_ffn1"].shape[0]
    tq = _tile(S, 256)
    xt = x.reshape(B * S, D)
    saved, params = [], []
    for l in range(L):
        P = _layer_params(0, _prep_weights(small, pipe.weights(l, xt), conv_w, D, l))
        xt, s1 = _ffn_fwd(xt, P["g1"], P["w1in"], P["w1out"], pipe.token(l))
        xt, s2 = _mixer_fwd(xt, P, B, S, tq)
        w2 = pipe.weights_ffn2(l, xt)
        P.update(w2in=(w2["w_ffn2_in"], 0), w2out=(w2["w_ffn2_out"].reshape(1, -1, D), 0))
        xt, s3 = _ffn_fwd(xt, P["g2"], P["w2in"], P["w2out"])
        saved.append((s1, s2, s3))
        params.append(P)
    dres, dgf, loss = _final_loss(xt, small["norm_final"][None], target.reshape(B * S, D))
    sm = {k: [None] * L for k in SMALL if k != "norm_final"}
    token = None
    for l in reversed(range(L)):
        P, (s1, s2, s3) = params[l], saved[l]
        big = {}
        dres, dg2, big["w_ffn2_in"], big["w_ffn2_out"] = _ffn_bwd(dres, s3, P["g2"], P["w2in"], P["w2out"], token)
        dres, smix, big["w_mix_in"], big["w_mix_out"] = _mixer_bwd(dres, s2, P, B, S, tq)
        big = {k: val[None] for k, val in big.items()}
        token = pipe.grads(l, FFN2 + EARLY[2:], big, dres) if l == 0 else None
        dres, dg1, dw_in, dw_out = _ffn_bwd(dres, s1, P["g1"], P["w1in"], P["w1out"], token)
        big.update(w_ffn1_in=dw_in[None], w_ffn1_out=dw_out[None])
        sm["norm_ffn1"][l], sm["norm_ffn2"][l] = dg1[0], dg2[0]
        for k, val in smix.items():
            sm[k][l] = val
        token = pipe.grads(l, EARLY[:2] if l == 0 else BIG, big, big["w_ffn1_in"])
    sm = {k: jnp.stack(val) for k, val in sm.items()}
    sm["norm_final"] = dgf[0]
    return loss[0, 0], dres.reshape(B, S, D), sm


def _sibling_copies(refs, send_sems, recv_sems):
    n = len(refs) // 2
    x, y, c, _ = _place()
    return [pltpu.make_async_remote_copy(
        src_ref=refs[k], dst_ref=refs[n + k], send_sem=send_sems.at[k], recv_sem=recv_sems.at[k],
        device_id=(x, y, 1 - c), device_id_type=MESH) for k in range(n)]


class _Pipeline:
    def __init__(self, w):
        self.w, self.n_layers = w, w[BIG[0]].shape[0]
        first = _all_gather([_place_shard(w[k], BF16, 0, 1) for k in EARLY] + [_place_shard(w["conv_w"], F32)])
        self.conv_w = first[-1]
        self._ready = dict(zip(EARLY, first[:-1]))
        self._ffn2 = self._start_gather("0b", FFN2, 0, first[0])
        self._next = (1, self._start_gather("1", BIG, 1, self._ffn2[1][3]))
        self._reduce, self._swaps = None, []
        self.reduced = [dict() for _ in range(self.n_layers)]

    def _start_gather(self, tag, kinds, l, after):
        placed = [_place_shard(self.w[k], BF16, l, 1) for k in kinds]
        return kinds, _split_start(f"gather_start_{tag}", placed, 3 * len(kinds), _gather_copies, after)

    def _wait_gather(self, tag, started, after):
        kinds, (send_sems, recv_sems, bufs, _) = started
        return dict(zip(kinds, _split_wait(f"gather_wait_{tag}", send_sems, recv_sems, bufs, _gather_copies, after)))

    def token(self, l):
        return self._next[1][1][3] if self._next is not None and self._next[0] == l + 1 else None

    def weights(self, l, after):
        if l == 0:
            return self._ready
        self._layer = self._wait_gather(str(l), self._next[1], after)
        first = next(iter(self._layer.values()))
        self._next = (l + 1, self._start_gather(str(l + 1), BIG, l + 1, first)) if l + 1 < self.n_layers else None
        return self._layer

    def weights_ffn2(self, l, after):
        return self._wait_gather("0b", self._ffn2, after) if l == 0 else self._layer

    def _finish_reduce(self, after):
        if self._reduce is None:
            return None
        tag, l, kinds, (send_sems, recv_sems, bufs, _) = self._reduce
        n = len(kinds)
        bufs = _split_wait(f"reduce_wait_{tag}", send_sems, recv_sems, bufs, _exchange_copies, after)
        mine = [_add_blocks(p, o, half=False) for p, o in zip(bufs[:n], bufs[n:])]
        lands = [lax.empty(q.shape, q.dtype) for q in mine]
        self._swaps.append((tag, l, kinds, _split_start(f"swap_start_{tag}", mine + lands, n, _sibling_copies, mine[0])))
        self._reduce = None
        return self._swaps[-1][3][3]

    def grads(self, l, kinds, big, after):
        swap_token = self._finish_reduce(after)
        grads = [big[k] for k in kinds]
        tag = str(l) if len(kinds) == len(BIG) else f"{l}_{kinds[0][2:]}"
        lands = [lax.empty((3, g.shape[0]) + g.shape[2:], g.dtype) for g in grads]
        started = _split_start(f"reduce_start_{tag}", grads + lands, 3 * len(kinds), _exchange_copies,
                               grads[0] if swap_token is None else swap_token)
        self._reduce = (tag, l, kinds, started)
        self.last_token = started[3]
        return started[3]

    def finish(self, after, last=False):
        if last:
            self._finish_reduce(after)
        for tag, l, kinds, (send_sems, recv_sems, bufs, _) in self._swaps:
            n = len(kinds)
            bufs = _split_wait(f"swap_wait_{tag}", send_sems, recv_sems, bufs, _sibling_copies, after)
            self.reduced[l].update(zip(kinds, zip(bufs[:n], bufs[n:])))
        self._swaps = []
        return self.reduced


def _pack(parts, extra=()):
    flat = jnp.concatenate([p.reshape(-1) for p in parts] + [jnp.reshape(e, (1,)) for e in extra])
    n = -(-flat.shape[0] // (8 * LANES)) * 8
    return jnp.pad(flat, (0, n * LANES - flat.shape[0])).reshape(n, LANES)


def _unpack(buf, shapes):
    flat, out, at = buf.reshape(-1), [], 0
    for s in shapes:
        n = math.prod(s)
        out.append(flat[at:at + n].reshape(s))
        at += n
    return out, flat[at:]


def kernel(x, norm_ffn1, w_ffn1_in, w_ffn1_out, norm_mix, w_mix_in, b_forget, w_pool, pool_scale, conv_w, w_mix_out, norm_ffn2, w_ffn2_in, w_ffn2_out, norm_final, loss_target, m_norm_ffn1, m_w_ffn1_in, m_w_ffn1_out, m_norm_mix, m_w_mix_in, m_b_forget, m_w_pool, m_pool_scale, m_conv_w, m_w_mix_out, m_norm_ffn2, m_w_ffn2_in, m_w_ffn2_out, m_norm_final, v_norm_ffn1, v_w_ffn1_in, v_w_ffn1_out, v_norm_mix, v_w_mix_in, v_b_forget, v_w_pool, v_pool_scale, v_conv_w, v_w_mix_out, v_norm_ffn2, v_w_ffn2_in, v_w_ffn2_out, v_norm_final):
    w = dict(zip(WEIGHTS, (norm_ffn1, w_ffn1_in, w_ffn1_out, norm_mix, w_mix_in, b_forget, w_pool, pool_scale, conv_w, w_mix_out, norm_ffn2, w_ffn2_in, w_ffn2_out, norm_final)))
    m = dict(zip(WEIGHTS, (m_norm_ffn1, m_w_ffn1_in, m_w_ffn1_out, m_norm_mix, m_w_mix_in, m_b_forget, m_w_pool, m_pool_scale, m_conv_w, m_w_mix_out, m_norm_ffn2, m_w_ffn2_in, m_w_ffn2_out, m_norm_final)))
    v = dict(zip(WEIGHTS, (v_norm_ffn1, v_w_ffn1_in, v_w_ffn1_out, v_norm_mix, v_w_mix_in, v_b_forget, v_w_pool, v_pool_scale, v_conv_w, v_w_mix_out, v_norm_ffn2, v_w_ffn2_in, v_w_ffn2_out, v_norm_final)))
    block = 2 * lax.axis_index("x") + lax.axis_index("y")

    pipe = _Pipeline(w)
    small = {k: w[k] for k in SMALL}
    loss, grad_x, sm = _local_step(x, loss_target, small, pipe.conv_w, pipe)
    grads, delta, new_m, new_v = {}, {}, {}, {}

    def big_adamw(k, reduced, token=None):
        two_d = lambda a: a.reshape(-1, a.shape[-1])
        pieces = [tuple(map(two_d, layer[k])) for layer in reduced]
        res = _adamw(two_d(w[k]), pieces, two_d(m[k]), two_d(v[k]), token)
        grads[k], delta[k], new_m[k], new_v[k] = [r.reshape(w[k].shape) for r in res]

    reduced = pipe.finish(grad_x)
    for k in BIG[2:]:
        big_adamw(k, reduced, pipe.last_token)
    reduced = pipe.finish(new_v[BIG[-1]], last=True)
    for k in BIG[:2]:
        big_adamw(k, reduced)

    order = list(SMALL)
    total = _all_reduce_small(_pack([sm[k] for k in order], extra=(loss,)), new_v[BIG[0]])
    parts, rest = _unpack(total, [sm[k].shape for k in order])
    grads.update(zip(order, parts))
    loss = rest[0]
    cs = conv_w.shape[2]
    grads["conv_w"] = lax.dynamic_slice_in_dim(grads["conv_w"], block * cs, cs, axis=2)
    packed = [_pack([t[k] for k in order]) for t in (w, grads, m, v)]
    _, d, nm, nv = _adamw(packed[0], [packed[1]], packed[2], packed[3])
    shapes = [w[k].shape for k in order]
    for res, flat in ((delta, d), (new_m, nm), (new_v, nv)):
        res.update(zip(order, _unpack(flat, shapes)[0]))
    return (loss, grad_x, *[grads[k] for k in WEIGHTS], *[delta[k] for k in WEIGHTS],
            *[new_m[k] for k in WEIGHTS], *[new_v[k] for k in WEIGHTS])
```

```python
import functools
import math

import jax
import jax.numpy as jnp
from jax import lax
from jax.experimental import pallas as pl
from jax.experimental.pallas import tpu as pltpu

F32 = jnp.float32
BF16 = jnp.bfloat16
MESH = pl.DeviceIdType.MESH

HEAD_DIM = 64
POOL_WINDOWS = (2, 4, 8, 16)
CONV_WIDTH = 3
RMS_EPS = 1e-6
ADAM_LR = 0.001
ADAM_B1 = 0.9
ADAM_B2 = 0.999
ADAM_EPS = 1e-08
ADAM_WD = 0.01
ADAM_STEP = 10

LANES = 128
VMEM_LIMIT = 56 * 1024 * 1024
N_CHIPS = 4
N_DEV = 8

NN = (((1,), (0,)), ((), ()))
NT = (((1,), (1,)), ((), ()))
TN = (((0,), (0,)), ((), ()))


def _tile(n, pref):
    for t in range(pref - pref % 16, 15, -16):
        if n % t == 0:
            return t
    return n


def _params(n_grid):
    return pltpu.CompilerParams(dimension_semantics=("arbitrary",) * n_grid, vmem_limit_bytes=VMEM_LIMIT)


def _dot(a, b, dims):
    return lax.dot_general(a, b, dims, preferred_element_type=F32)


def _mm(name, dims, operands, in_specs, out_shape, out_specs, grid, acc_shape, epilogue):
    n_in, n_out, nk = len(operands), len(out_shape), grid[-1]

    def kern(*refs):
        extras, outs = refs[2:n_in], refs[n_in:n_in + n_out]
        if nk == 1:
            epilogue(_dot(refs[0][...].astype(BF16), refs[1][...].astype(BF16), dims), extras, outs)
            return
        acc = refs[n_in + n_out]
        k = pl.program_id(len(grid) - 1)

        @pl.when(k == 0)
        def _():
            acc[...] = jnp.zeros_like(acc)

        acc[...] += _dot(refs[0][...].astype(BF16), refs[1][...].astype(BF16), dims)

        @pl.when(k == nk - 1)
        def _():
            epilogue(acc[...], extras, outs)

    return pl.pallas_call(
        kern, name=name, grid=grid, in_specs=in_specs, out_specs=out_specs, out_shape=out_shape,
        scratch_shapes=[pltpu.VMEM(acc_shape, F32)] if nk > 1 else [],
        compiler_params=_params(len(grid)),
    )(*operands)


def _store(scale=None, dtype=None):
    def ep(acc, extras, outs):
        v = acc if scale is None else acc * scale
        outs[0][...] = v.astype(outs[0].dtype)
    return ep


def _residual(scale):
    def ep(acc, extras, outs):
        outs[0][...] = extras[0][...] + scale * acc
    return ep


def _final_loss(x, g, target):
    T, D = x.shape
    tr = _tile(T, 256)

    def kern(x_ref, g_ref, t_ref, dx_ref, dg_ref, loss_ref):
        xv = x_ref[...]
        r = lax.rsqrt(jnp.mean(xv * xv, axis=-1, keepdims=True) + RMS_EPS)
        y = xv * r
        err = y * g_ref[...] - t_ref[...]
        lpart = 0.5 * jnp.sum(jnp.mean(err * err, axis=-1, keepdims=True), axis=0, keepdims=True)
        dh = err * (1.0 / D)
        dy = dh * g_ref[...]
        dx_ref[...] = r * (dy - y * jnp.mean(dy * y, axis=-1, keepdims=True))
        part = jnp.sum(dh * y, axis=0, keepdims=True)
        lrow = jnp.broadcast_to(lpart, (1, LANES))

        @pl.when(pl.program_id(0) == 0)
        def _():
            dg_ref[...] = part
            loss_ref[...] = lrow

        @pl.when(pl.program_id(0) > 0)
        def _():
            dg_ref[...] += part
            loss_ref[...] += lrow

    row = pl.BlockSpec((tr, D), lambda i: (i, 0))
    vec = pl.BlockSpec((1, D), lambda i: (0, 0))
    return pl.pallas_call(
        kern, name="final_loss", grid=(T // tr,), in_specs=[row, vec, row],
        out_specs=[row, vec, pl.BlockSpec((1, LANES), lambda i: (0, 0))],
        out_shape=[jax.ShapeDtypeStruct((T, D), F32), jax.ShapeDtypeStruct((1, D), F32),
                   jax.ShapeDtypeStruct((1, LANES), F32)],
        compiler_params=_params(1),
    )(x, g, target)


def _resident(shape, index_map):
    return pl.BlockSpec(shape, index_map, pipeline_mode=pl.Buffered(1))


def _token_operand(token):
    return ([], []) if token is None else ([token], [pl.BlockSpec(token.shape, lambda i: (0, 0))])


def _ffn_up(x, g, w4, token=None):
    T, D = x.shape
    w4, l = w4
    Fh = w4.shape[3]
    F = 2 * Fh
    tm = _tile(T, 512)
    tok_ops, tok_specs = _token_operand(token)

    def kern(x_ref, g_ref, w_ref, *rest):
        h_ref, jac_ref, act_ref = rest[len(tok_ops):]
        xv = x_ref[...]
        r = lax.rsqrt(jnp.mean(xv * xv, axis=-1, keepdims=True) + RMS_EPS)
        hv = (xv * r * g_ref[...]).astype(BF16)
        h_ref[...] = hv
        for j in range(2):
            cols = slice(j * Fh, (j + 1) * Fh)
            gate = _dot(hv, w_ref[j], NN)
            up = _dot(hv, w_ref[2 + j], NN)
            sg = jax.nn.sigmoid(gate)
            silu = gate * sg
            jac_ref[0, :, cols] = (up * (sg + silu * (1.0 - sg))).astype(BF16)
            jac_ref[1, :, cols] = silu.astype(BF16)
            act_ref[:, cols] = (silu * up).astype(BF16)

    return pl.pallas_call(
        kern, name="ffn_up", grid=(T // tm,),
        in_specs=[pl.BlockSpec((tm, D), lambda i: (i, 0)), pl.BlockSpec((1, D), lambda i: (0, 0)),
                  _resident((None, 4, D, Fh), lambda i: (l, 0, 0, 0))] + tok_specs,
        out_specs=[pl.BlockSpec((tm, D), lambda i: (i, 0)), pl.BlockSpec((2, tm, F), lambda i: (0, i, 0)),
                   pl.BlockSpec((tm, F), lambda i: (i, 0))],
        out_shape=[jax.ShapeDtypeStruct((T, D), BF16), jax.ShapeDtypeStruct((2, T, F), BF16),
                   jax.ShapeDtypeStruct((T, F), BF16)],
        compiler_params=_params(1),
    )(x, g, w4, *tok_ops)


def _ffn_bwd_main(dres, jac, x, g, w_out, w4, token=None):
    T, D = dres.shape
    w_out, l = w_out
    w4, _ = w4
    F = w_out.shape[1]
    Fh = F // 2
    tm = _tile(T, 256)
    tok_ops, tok_specs = _token_operand(token)

    def kern(d_ref, jac_ref, x_ref, g_ref, wo_ref, wi_ref, *rest):
        dgu_ref, dx_ref, dg_ref = rest[len(tok_ops):]
        dv = d_ref[...]
        d16 = dv.astype(BF16)
        dh = jnp.zeros((tm, D), F32)
        for j in range(2):
            cols = slice(j * Fh, (j + 1) * Fh)
            dact = 0.5 * _dot(d16, wo_ref[cols, :], NT)
            dgate = (dact * jac_ref[0, :, cols].astype(F32)).astype(BF16)
            dup = (dact * jac_ref[1, :, cols].astype(F32)).astype(BF16)
            dgu_ref[0, :, cols] = dgate
            dgu_ref[1, :, cols] = dup
            dh = dh + _dot(dgate, wi_ref[j], NT) + _dot(dup, wi_ref[2 + j], NT)
        xv = x_ref[...]
        r = lax.rsqrt(jnp.mean(xv * xv, axis=-1, keepdims=True) + RMS_EPS)
        y = xv * r
        dy = dh * g_ref[...]
        dx_ref[...] = dv + r * (dy - y * jnp.mean(dy * y, axis=-1, keepdims=True))
        part = jnp.sum(dh * y, axis=0, keepdims=True)

        @pl.when(pl.program_id(0) == 0)
        def _():
            dg_ref[...] = part

        @pl.when(pl.program_id(0) > 0)
        def _():
            dg_ref[...] += part

    row = pl.BlockSpec((tm, D), lambda i: (i, 0))
    vec = pl.BlockSpec((1, D), lambda i: (0, 0))
    wide = pl.BlockSpec((2, tm, F), lambda i: (0, i, 0))
    return pl.pallas_call(
        kern, name="ffn_bwd_main", grid=(T // tm,),
        in_specs=[row, wide, row, vec, _resident((None, F, D), lambda i: (l, 0, 0)),
                  _resident((None, 4, D, Fh), lambda i: (l, 0, 0, 0))] + tok_specs,
        out_specs=[wide, row, vec],
        out_shape=[jax.ShapeDtypeStruct((2, T, F), BF16), jax.ShapeDtypeStruct((T, D), F32),
                   jax.ShapeDtypeStruct((1, D), F32)],
        compiler_params=_params(1),
    )(dres, jac, x, g, w_out, w4, *tok_ops)


def _ffn_out(act, w_out, x):
    T, F = act.shape
    w_out, l = w_out
    D = w_out.shape[2]
    tm = _tile(T, 512)
    return _mm("ffn_out", NN, [act, w_out, x],
               [pl.BlockSpec((tm, F), lambda i, k: (i, 0)), pl.BlockSpec((None, F, D), lambda i, k: (l, 0, 0)),
                pl.BlockSpec((tm, D), lambda i, k: (i, 0))],
               [jax.ShapeDtypeStruct((T, D), F32)], [pl.BlockSpec((tm, D), lambda i, k: (i, 0))],
               (T // tm, 1), None, _residual(0.5))


def _ffn_dw_out(act, dres):
    T, F = act.shape
    D = dres.shape[1]
    tm, tk = F // 2, _tile(T, 1024)
    return _mm("ffn_dw_out", TN, [act, dres],
               [pl.BlockSpec((tk, tm), lambda i, k: (k, i)), pl.BlockSpec((tk, D), lambda i, k: (k, 0))],
               [jax.ShapeDtypeStruct((F, D), BF16)], [pl.BlockSpec((tm, D), lambda i, k: (i, 0))],
               (2, T // tk), (tm, D), _store(0.5))


def _ffn_dw_in(h, dgu):
    T, D = h.shape
    Fh = dgu.shape[2] // 2
    tk = _tile(T, 1024)
    return _mm("ffn_dw_in", TN, [h, dgu],
               [pl.BlockSpec((tk, D), lambda j, k: (k, 0)),
                pl.BlockSpec((None, tk, Fh), lambda j, k: (j // 2, k, j % 2))],
               [jax.ShapeDtypeStruct((4, D, Fh), BF16)], [pl.BlockSpec((None, D, Fh), lambda j, k: (j, 0, 0))],
               (4, T // tk), (D, Fh), _store())


def _proj(name, a, w, out_dtype, dims=NN, extra=None, scale=None):
    T, K = a.shape
    w, l = w
    N = w.shape[2] if dims == NN else w.shape[1]
    tm = _tile(T, 512)
    ops = [a, w] + ([extra] if extra is not None else [])
    specs = [pl.BlockSpec((tm, K), lambda i, k: (i, 0)), pl.BlockSpec((None,) + w.shape[1:], lambda i, k: (l, 0, 0))]
    if extra is not None:
        specs.append(pl.BlockSpec((tm, N), lambda i, k: (i, 0)))
    ep = _residual(1.0) if extra is not None else _store(scale)
    return _mm(name, dims, ops, specs, [jax.ShapeDtypeStruct((T, N), out_dtype)],
               [pl.BlockSpec((tm, N), lambda i, k: (i, 0))], (T // tm, 1), None, ep)[0]


def _mix_up(x, g, wp, widths):
    T, D = x.shape
    wp, l = wp
    n_qkv, n_rest = widths
    NP = wp.shape[2]
    tm = _tile(T, 512)

    def kern(x_ref, g_ref, w_ref, h_ref, qkv_ref, rest_ref, fl_ref):
        xv = x_ref[...]
        r = lax.rsqrt(jnp.mean(xv * xv, axis=-1, keepdims=True) + RMS_EPS)
        hv = (xv * r * g_ref[...]).astype(BF16)
        h_ref[...] = hv
        qkv_ref[...] = _dot(hv, w_ref[:, 0:n_qkv], NN).astype(BF16)
        rest_ref[...] = _dot(hv, w_ref[:, n_qkv:n_qkv + n_rest], NN)
        fl_ref[...] = _dot(hv, w_ref[:, n_qkv + n_rest:NP], NN)

    row = lambda n: pl.BlockSpec((tm, n), lambda i: (i, 0))
    return pl.pallas_call(
        kern, name="mix_up", grid=(T // tm,),
        in_specs=[row(D), pl.BlockSpec((1, D), lambda i: (0, 0)), _resident((None, D, NP), lambda i: (l, 0, 0))],
        out_specs=[row(D), row(n_qkv), row(n_rest), row(LANES)],
        out_shape=[jax.ShapeDtypeStruct((T, D), BF16), jax.ShapeDtypeStruct((T, n_qkv), BF16),
                   jax.ShapeDtypeStruct((T, n_rest), F32), jax.ShapeDtypeStruct((T, LANES), F32)],
        compiler_params=_params(1),
    )(x, g, wp)


def _column_starts(pieces):
    starts, at = [], 0
    for p in pieces:
        starts.append(at)
        at += p.shape[1]
    return starts


def _mix_in_bwd(pieces, x, g, dres, wp):
    T, D = x.shape
    wp, l = wp
    NP = wp.shape[2]
    tm = _tile(T, 512)
    n, starts = len(pieces), _column_starts(pieces)

    def kern(*refs):
        x_ref, g_ref, d_ref, w_ref, dx_ref, dg_ref = refs[n:]
        dh = jnp.zeros((tm, D), F32)
        for p_ref, at in zip(refs[:n], starts):
            dh = dh + _dot(p_ref[...].astype(BF16), w_ref[:, at:at + p_ref.shape[1]], NT)
        xv = x_ref[...]
        r = lax.rsqrt(jnp.mean(xv * xv, axis=-1, keepdims=True) + RMS_EPS)
        y = xv * r
        dy = dh * g_ref[...]
        dx_ref[...] = d_ref[...] + r * (dy - y * jnp.mean(dy * y, axis=-1, keepdims=True))
        part = jnp.sum(dh * y, axis=0, keepdims=True)

        @pl.when(pl.program_id(0) == 0)
        def _():
            dg_ref[...] = part

        @pl.when(pl.program_id(0) > 0)
        def _():
            dg_ref[...] += part

    row = lambda n: pl.BlockSpec((tm, n), lambda i: (i, 0))
    vec = pl.BlockSpec((1, D), lambda i: (0, 0))
    return pl.pallas_call(
        kern, name="mix_in_bwd", grid=(T // tm,),
        in_specs=[row(p.shape[1]) for p in pieces] + [row(D), vec, row(D), _resident((None, D, NP), lambda i: (l, 0, 0))],
        out_specs=[row(D), vec],
        out_shape=[jax.ShapeDtypeStruct((T, D), F32), jax.ShapeDtypeStruct((1, D), F32)],
        compiler_params=_params(1),
    )(*pieces, x, g, dres, wp)


def _pieces_dw(name, a, pieces, out_dtype, tk_pref):
    T, M = a.shape
    n, starts = len(pieces), _column_starts(pieces)
    N = starts[-1] + pieces[-1].shape[1]
    tk = _tile(T, tk_pref)
    nk = T // tk

    def kern(a_ref, *refs):
        o_ref, acc = refs[n], refs[n + 1]
        k = pl.program_id(0)

        @pl.when(k == 0)
        def _():
            acc[...] = jnp.zeros_like(acc)

        av = a_ref[...].astype(BF16)
        for p_ref, at in zip(refs[:n], starts):
            acc[:, at:at + p_ref.shape[1]] += _dot(av, p_ref[...].astype(BF16), TN)

        @pl.when(k == nk - 1)
        def _():
            o_ref[...] = acc[...].astype(out_dtype)

    return pl.pallas_call(
        kern, name=name, grid=(nk,),
        in_specs=[pl.BlockSpec((tk, M), lambda k: (k, 0))] + [pl.BlockSpec((tk, p.shape[1]), lambda k: (k, 0)) for p in pieces],
        out_specs=pl.BlockSpec((M, N), lambda k: (0, 0)), out_shape=jax.ShapeDtypeStruct((M, N), out_dtype),
        scratch_shapes=[pltpu.VMEM((M, N), F32)], compiler_params=_params(1),
    )(a, *pieces)


def _rows_dw(name, pieces, d, out_dtype):
    T, N = d.shape
    n, starts = len(pieces), _column_starts(pieces)
    M = starts[-1] + pieces[-1].shape[1]
    tk = _tile(T, 1024)
    nk = T // tk

    def kern(*refs):
        d_ref, o_ref, acc = refs[n], refs[n + 1], refs[n + 2]
        k = pl.program_id(0)

        @pl.when(k == 0)
        def _():
            acc[...] = jnp.zeros_like(acc)

        dv = d_ref[...].astype(BF16)
        for p_ref, at in zip(refs[:n], starts):
            acc[at:at + p_ref.shape[1], :] += _dot(p_ref[...], dv, TN)

        @pl.when(k == nk - 1)
        def _():
            o_ref[...] = acc[...].astype(out_dtype)

    return pl.pallas_call(
        kern, name=name, grid=(nk,),
        in_specs=[pl.BlockSpec((tk, p.shape[1]), lambda k: (k, 0)) for p in pieces] + [pl.BlockSpec((tk, N), lambda k: (k, 0))],
        out_specs=pl.BlockSpec((M, N), lambda k: (0, 0)), out_shape=jax.ShapeDtypeStruct((M, N), out_dtype),
        scratch_shapes=[pltpu.VMEM((M, N), F32)], compiler_params=_params(1),
    )(*pieces, d)


def _mix_out(pieces, w, x):
    T, D = x.shape
    w, l = w
    n, starts = len(pieces), _column_starts(pieces)
    tm = _tile(T, 512)

    def kern(*refs):
        w_ref, x_ref, o_ref = refs[n:]
        acc = x_ref[...]
        for p_ref, at in zip(refs[:n], starts):
            acc = acc + _dot(p_ref[...], w_ref[at:at + p_ref.shape[1], :], NN)
        o_ref[...] = acc

    row = lambda m: pl.BlockSpec((tm, m), lambda i: (i, 0))
    return pl.pallas_call(
        kern, name="mix_out", grid=(T // tm,),
        in_specs=[row(p.shape[1]) for p in pieces] + [_resident((None,) + w.shape[1:], lambda i: (l, 0, 0)), row(D)],
        out_specs=row(D), out_shape=jax.ShapeDtypeStruct((T, D), F32), compiler_params=_params(1),
    )(*pieces, w, x)


def _log_sigmoid(z):
    return jnp.minimum(z, 0.0) - jnp.log(1.0 + jnp.exp(-jnp.abs(z)))


def _decay_fwd(fl, bias):
    B, S, _ = fl.shape

    def kern(fl_ref, b_ref, o_ref):
        d = _log_sigmoid(fl_ref[...] + b_ref[...])
        row = lax.broadcasted_iota(jnp.int32, (S, LANES), 0)
        sh = 1
        while sh < S:
            d = d + jnp.where(row >= sh, pltpu.roll(d, sh, 0), 0.0)
            sh *= 2
        o_ref[...] = d.T[0:8, :]

    return pl.pallas_call(
        kern, name="decay_fwd", grid=(B,),
        in_specs=[pl.BlockSpec((None, S, LANES), lambda b: (b, 0, 0)), pl.BlockSpec((1, LANES), lambda b: (0, 0))],
        out_specs=pl.BlockSpec((None, 8, S), lambda b: (b, 0, 0)),
        out_shape=jax.ShapeDtypeStruct((B, 8, S), F32), compiler_params=_params(1),
    )(fl, bias)


def _decay_bwd(ddrow, ddcol, fl, bias, n_heads):
    B, S, _ = fl.shape

    def kern(dd_ref, ddc_ref, fl_ref, b_ref, dfl_ref, db_ref):
        dd = jnp.concatenate([dd_ref[...], jnp.zeros((LANES - 8, S), F32)], axis=0).T + ddc_ref[...]
        row = lax.broadcasted_iota(jnp.int32, (S, LANES), 0)
        lane = lax.broadcasted_iota(jnp.int32, (S, LANES), 1)
        sh = 1
        while sh < S:
            dd = dd + jnp.where(row < S - sh, pltpu.roll(dd, S - sh, 0), 0.0)
            sh *= 2
        z = fl_ref[...] + b_ref[...]
        dfl = jnp.where(lane < n_heads, dd / (1.0 + jnp.exp(z)), 0.0)
        dfl_ref[...] = dfl
        part = jnp.sum(dfl, axis=0, keepdims=True)

        @pl.when(pl.program_id(0) == 0)
        def _():
            db_ref[...] = part

        @pl.when(pl.program_id(0) > 0)
        def _():
            db_ref[...] += part

    return pl.pallas_call(
        kern, name="decay_bwd", grid=(B,),
        in_specs=[pl.BlockSpec((None, 8, S), lambda b: (b, 0, 0)), pl.BlockSpec((None, S, LANES), lambda b: (b, 0, 0)),
                  pl.BlockSpec((None, S, LANES), lambda b: (b, 0, 0)), pl.BlockSpec((1, LANES), lambda b: (0, 0))],
        out_specs=[pl.BlockSpec((None, S, LANES), lambda b: (b, 0, 0)), pl.BlockSpec((1, LANES), lambda b: (0, 0))],
        out_shape=[jax.ShapeDtypeStruct((B, S, LANES), F32), jax.ShapeDtypeStruct((1, LANES), F32)],
        compiler_params=_params(1),
    )(ddrow, ddcol, fl, bias)


def _attn_fwd(qkv, drow, n_heads, tq):
    B, S, _ = qkv.shape
    DA = n_heads * HEAD_DIM
    scale = HEAD_DIM ** -0.5

    n_pairs = n_heads // 2

    def kern(q_ref, k_ref, v_ref, dr_ref, o_ref, lse_ref):
        i = pl.program_id(1)
        lane = lax.broadcasted_iota(jnp.int32, (tq, LANES), 1)
        low = lane < HEAD_DIM
        causal = lax.broadcasted_iota(jnp.int32, (tq, tq), 1) <= lax.broadcasted_iota(jnp.int32, (tq, tq), 0)
        qms = []
        for p in range(n_pairs):
            q2 = q_ref[:, LANES * p:LANES * (p + 1)] * scale
            qms += [jnp.where(low, q2, jnp.zeros_like(q2)), jnp.where(low, jnp.zeros_like(q2), q2)]

        def step(j, carry, masked):
            ms, ls, accs = carry
            ks = pl.multiple_of(j * tq, tq)
            new_m, new_l, new_acc = [], [], []
            for p in range(n_pairs):
                cols = slice(LANES * p, LANES * (p + 1))
                k2, v2 = k_ref[pl.ds(ks, tq), cols], v_ref[pl.ds(ks, tq), cols]
                alphas, pvs = [], []
                for h in (2 * p, 2 * p + 1):
                    s = _dot(qms[h], k2, NT) - dr_ref[h, pl.ds(j, 1), :]
                    if masked:
                        s = jnp.where(causal, s, -jnp.inf)
                    m_new = jnp.maximum(ms[h], jnp.max(s, axis=1, keepdims=True))
                    alpha = jnp.exp(ms[h] - m_new)
                    pm = jnp.exp(s - m_new)
                    new_m.append(m_new)
                    new_l.append(alpha * ls[h] + jnp.sum(pm, axis=1, keepdims=True))
                    alphas.append(alpha)
                    pvs.append(_dot(pm.astype(BF16), v2, NN))
                new_acc.append(jnp.where(low, alphas[0], alphas[1]) * accs[p] + jnp.where(low, pvs[0], pvs[1]))
            return tuple(new_m), tuple(new_l), tuple(new_acc)

        init = (tuple(jnp.full((tq, 1), -jnp.inf, F32) for _ in range(n_heads)),
                tuple(jnp.zeros((tq, 1), F32) for _ in range(n_heads)),
                tuple(jnp.zeros((tq, LANES), F32) for _ in range(n_pairs)))
        ms, ls, accs = step(i, lax.fori_loop(0, i, functools.partial(step, masked=False), init), True)
        lse_mat = jnp.zeros((tq, LANES), F32)
        for p in range(n_pairs):
            l0, l1 = ls[2 * p], ls[2 * p + 1]
            o_ref[:, LANES * p:LANES * (p + 1)] = (accs[p] / jnp.where(low, l0, l1)).astype(BF16)
            lse_mat = jnp.where(lane == 2 * p, ms[2 * p] + jnp.log(l0), lse_mat)
            lse_mat = jnp.where(lane == 2 * p + 1, ms[2 * p + 1] + jnp.log(l1), lse_mat)
        lse_ref[...] = lse_mat

    nq = S // tq
    return pl.pallas_call(
        kern, name="attn_fwd", grid=(B, nq),
        in_specs=[pl.BlockSpec((None, tq, DA), lambda b, i: (b, i, 0)),
                  pl.BlockSpec((None, S, DA), lambda b, i: (b, 0, 1)),
                  pl.BlockSpec((None, S, DA), lambda b, i: (b, 0, 2)),
                  pl.BlockSpec((None, 8, nq, tq), lambda b, i: (b, 0, 0, 0))],
        out_specs=[pl.BlockSpec((None, tq, DA), lambda b, i: (b, i, 0)),
                   pl.BlockSpec((None, tq, LANES), lambda b, i: (b, i, 0))],
        out_shape=[jax.ShapeDtypeStruct((B, S, DA), BF16), jax.ShapeDtypeStruct((B, S, LANES), F32)],
        compiler_params=_params(2),
    )(qkv, qkv, qkv, drow)


def _attn_bwd(qkv, drow, o, lse, dycat, n_heads, tq):
    B, S, _ = qkv.shape
    DA = n_heads * HEAD_DIM
    scale = HEAD_DIM ** -0.5
    nq = S // tq

    n_pairs = n_heads // 2

    def kern(q_ref, k_ref, v_ref, dr_ref, o_ref, lse_ref, do_ref, dq_ref, dk_ref, dv_ref, ddr_ref, ddc_ref,
             dk_acc, dv_acc, qm_s, dom_s, delta_s, rs_s, dq_s):
        i = pl.program_id(1)

        @pl.when(i == 0)
        def _():
            dk_acc[...] = jnp.zeros_like(dk_acc)
            dv_acc[...] = jnp.zeros_like(dv_acc)
            ddr_ref[...] = jnp.zeros_like(ddr_ref)

        lane = lax.broadcasted_iota(jnp.int32, (tq, LANES), 1)
        low = lane < HEAD_DIM
        causal = lax.broadcasted_iota(jnp.int32, (tq, tq), 1) <= lax.broadcasted_iota(jnp.int32, (tq, tq), 0)
        for p in range(n_pairs):
            cols = slice(LANES * p, LANES * (p + 1))
            q2 = q_ref[:, cols] * scale
            do_f = do_ref[:, cols]
            do2 = do_f.astype(BF16)
            prod = do_f * o_ref[:, cols].astype(F32)
            qm_s[2 * p] = jnp.where(low, q2, jnp.zeros_like(q2))
            qm_s[2 * p + 1] = jnp.where(low, jnp.zeros_like(q2), q2)
            dom_s[2 * p] = jnp.where(low, do2, jnp.zeros_like(do2))
            dom_s[2 * p + 1] = jnp.where(low, jnp.zeros_like(do2), do2)
            delta_s[2 * p] = jnp.sum(jnp.where(low, prod, 0.0), axis=1, keepdims=True)
            delta_s[2 * p + 1] = jnp.sum(jnp.where(low, 0.0, prod), axis=1, keepdims=True)
            dq_s[p] = jnp.zeros((tq, LANES), F32)
        rs_s[...] = jnp.zeros(rs_s.shape, F32)

        def step(j, masked):
            ks = pl.multiple_of(j * tq, tq)
            for p in range(n_pairs):
                cols = slice(LANES * p, LANES * (p + 1))
                k2, v2 = k_ref[pl.ds(ks, tq), cols], v_ref[pl.ds(ks, tq), cols]
                dvs, dks, dqs = [], [], []
                for h in (2 * p, 2 * p + 1):
                    qm, dom = qm_s[h], dom_s[h]
                    s = _dot(qm, k2, NT) - dr_ref[h, pl.ds(j, 1), :]
                    if masked:
                        s = jnp.where(causal, s, -jnp.inf)
                    pm = jnp.exp(s - lse_ref[:, h:h + 1])
                    ds = pm * (_dot(dom, v2, NT) - delta_s[h])
                    ddr_ref[h, pl.ds(j, 1), :] -= jnp.sum(ds, axis=0, keepdims=True)
                    rs_s[h] += jnp.sum(ds, axis=1, keepdims=True)
                    dsb = ds.astype(BF16)
                    dvs.append(_dot(pm.astype(BF16), dom, TN))
                    dks.append(_dot(dsb, qm, TN))
                    dqs.append(_dot(dsb, k2, NN))
                dv_acc[pl.ds(ks, tq), cols] += dvs[0] + dvs[1]
                dk_acc[pl.ds(ks, tq), cols] += dks[0] + dks[1]
                dq_s[p] += jnp.where(low, dqs[0], dqs[1])

        def body(j, carry):
            step(j, False)
            return carry

        lax.fori_loop(0, i, body, 0)
        step(i, True)
        ddc = jnp.zeros((tq, LANES), F32)
        for p in range(n_pairs):
            dq_ref[:, LANES * p:LANES * (p + 1)] = (dq_s[p] * scale).astype(BF16)
            ddc = jnp.where(lane == 2 * p, rs_s[2 * p], ddc)
            ddc = jnp.where(lane == 2 * p + 1, rs_s[2 * p + 1], ddc)
        ddc_ref[...] = ddc

        @pl.when(i == nq - 1)
        def _():
            dk_ref[...] = dk_acc[...].astype(BF16)
            dv_ref[...] = dv_acc[...].astype(BF16)

    tile = pl.BlockSpec((None, tq, DA), lambda b, i: (b, i, 0))
    seq = pl.BlockSpec((None, S, DA), lambda b, i: (b, 0, 0))
    dec = pl.BlockSpec((None, 8, nq, tq), lambda b, i: (b, 0, 0, 0))
    return pl.pallas_call(
        kern, name="attn_bwd", grid=(B, nq),
        in_specs=[tile, pl.BlockSpec((None, S, DA), lambda b, i: (b, 0, 1)),
                  pl.BlockSpec((None, S, DA), lambda b, i: (b, 0, 2)), dec, tile,
                  pl.BlockSpec((None, tq, LANES), lambda b, i: (b, i, 0)), tile],
        out_specs=[tile, seq, seq, dec, pl.BlockSpec((None, tq, LANES), lambda b, i: (b, i, 0))],
        out_shape=[jax.ShapeDtypeStruct((B, S, DA), BF16)] * 3 + [jax.ShapeDtypeStruct((B, 8, nq, tq), F32),
                                                                  jax.ShapeDtypeStruct((B, S, LANES), F32)],
        scratch_shapes=[pltpu.VMEM((S, DA), F32), pltpu.VMEM((S, DA), F32),
                        pltpu.VMEM((n_heads, tq, LANES), BF16), pltpu.VMEM((n_heads, tq, LANES), BF16),
                        pltpu.VMEM((n_heads, tq, 1), F32), pltpu.VMEM((n_heads, tq, 1), F32),
                        pltpu.VMEM((n_pairs, tq, LANES), F32)],
        compiler_params=_params(2),
    )(qkv, qkv, qkv, drow, o, lse, dycat)


def _down(v, d, row):
    return jnp.where(row >= d, pltpu.roll(v, d, 0), 0.0)


def _up(v, d, row, S):
    return jnp.where(row < S - d, pltpu.roll(v, S - d, 0), 0.0)


def _window(v, shift, group):
    sums, acc, d = [], v, 1
    for _ in POOL_WINDOWS:
        acc = acc + shift(acc, d)
        sums.append(acc)
        d *= 2
    out = sums[-1]
    for gi in range(len(POOL_WINDOWS) - 2, -1, -1):
        out = jnp.where(group == gi, sums[gi], out)
    return out


def _pool_count(row, group):
    w = jnp.full(row.shape, POOL_WINDOWS[-1], jnp.int32)
    for gi in range(len(POOL_WINDOWS) - 2, -1, -1):
        w = jnp.where(group == gi, POOL_WINDOWS[gi], w)
    return jnp.minimum(row + 1, w).astype(F32)


def _mix_local_fwd(rest, wbd, ps, cw):
    B, S, C4 = rest.shape
    C = C4 // 4
    gw = C // len(POOL_WINDOWS)

    def kern(r_ref, w_ref, ps_ref, cw_ref, y_ref, pooled_ref):
        row = lax.broadcasted_iota(jnp.int32, (S, C), 0)
        group = lax.broadcasted_iota(jnp.int32, (S, C), 1) // gw
        u = r_ref[:, 0:C]
        pooled = _window(u, lambda v, d: _down(v, d, row), group) / _pool_count(row, group) - u
        pb = pooled.astype(BF16)
        pooled_ref[...] = pb
        y_ref[:, 0:C] = (_dot(pb, w_ref[...], NN) * ps_ref[...]).astype(BF16)
        uc = r_ref[:, 2 * C:3 * C] * r_ref[:, 3 * C:4 * C]
        y = cw_ref[0:1, :] * _down(uc, 2, row) + cw_ref[1:2, :] * _down(uc, 1, row) + cw_ref[2:3, :] * uc
        y_ref[:, C:2 * C] = (r_ref[:, C:2 * C] * y).astype(BF16)

    return pl.pallas_call(
        kern, name="mix_local_fwd", grid=(B,),
        in_specs=[pl.BlockSpec((None, S, C4), lambda b: (b, 0, 0)), pl.BlockSpec((C, C), lambda b: (0, 0)),
                  pl.BlockSpec((1, C), lambda b: (0, 0)), pl.BlockSpec((8, C), lambda b: (0, 0))],
        out_specs=[pl.BlockSpec((None, S, 2 * C), lambda b: (b, 0, 0)), pl.BlockSpec((None, S, C), lambda b: (b, 0, 0))],
        out_shape=[jax.ShapeDtypeStruct((B, S, 2 * C), BF16), jax.ShapeDtypeStruct((B, S, C), BF16)],
        compiler_params=_params(1),
    )(rest, wbd, ps, cw)


def _mix_local_bwd(rest, pooled, dycat, wbd, ps, cw):
    B, S, C4 = rest.shape
    C = C4 // 4
    gw = C // len(POOL_WINDOWS)

    def kern(r_ref, pooled_ref, d_ref, w_ref, ps_ref, cw_ref, dr_ref, dw_ref, dps_ref, dcw_ref):
        row = lax.broadcasted_iota(jnp.int32, (S, C), 0)
        group = lax.broadcasted_iota(jnp.int32, (S, C), 1) // gw
        dyp = d_ref[:, 0:C]
        dyc = d_ref[:, C:2 * C]
        pb = pooled_ref[...]
        dps = jnp.sum(dyp * _dot(pb, w_ref[...], NN), axis=0, keepdims=True)
        dzb = (dyp * ps_ref[...]).astype(BF16)
        dw = _dot(pb, dzb, TN)
        dpooled = _dot(dzb, w_ref[...], NT)
        g = dpooled / _pool_count(row, group)
        dr_ref[:, 0:C] = (_window(g, lambda v, d: _up(v, d, row, S), group) - dpooled).astype(BF16)
        cc, ch = r_ref[:, 2 * C:3 * C], r_ref[:, 3 * C:4 * C]
        uc = cc * ch
        u1, u2 = _down(uc, 1, row), _down(uc, 2, row)
        y = cw_ref[0:1, :] * u2 + cw_ref[1:2, :] * u1 + cw_ref[2:3, :] * uc
        dr_ref[:, C:2 * C] = (dyc * y).astype(BF16)
        dy = dyc * r_ref[:, C:2 * C]
        duc = cw_ref[0:1, :] * _up(dy, 2, row, S) + cw_ref[1:2, :] * _up(dy, 1, row, S) + cw_ref[2:3, :] * dy
        dr_ref[:, 2 * C:3 * C] = (duc * ch).astype(BF16)
        dr_ref[:, 3 * C:4 * C] = (duc * cc).astype(BF16)
        dcw = jnp.concatenate([jnp.sum(dy * u2, axis=0, keepdims=True), jnp.sum(dy * u1, axis=0, keepdims=True),
                               jnp.sum(dy * uc, axis=0, keepdims=True), jnp.zeros((5, C), F32)], axis=0)

        @pl.when(pl.program_id(0) == 0)
        def _():
            dw_ref[...] = dw
            dps_ref[...] = dps
            dcw_ref[...] = dcw

        @pl.when(pl.program_id(0) > 0)
        def _():
            dw_ref[...] += dw
            dps_ref[...] += dps
            dcw_ref[...] += dcw

    full = lambda shape: pl.BlockSpec(shape, lambda b: (0, 0))
    return pl.pallas_call(
        kern, name="mix_local_bwd", grid=(B,),
        in_specs=[pl.BlockSpec((None, S, C4), lambda b: (b, 0, 0)), pl.BlockSpec((None, S, C), lambda b: (b, 0, 0)),
                  pl.BlockSpec((None, S, 2 * C), lambda b: (b, 0, 1)), full((C, C)), full((1, C)), full((8, C))],
        out_specs=[pl.BlockSpec((None, S, C4), lambda b: (b, 0, 0)), full((C, C)), full((1, C)), full((8, C))],
        out_shape=[jax.ShapeDtypeStruct((B, S, C4), BF16), jax.ShapeDtypeStruct((C, C), F32),
                   jax.ShapeDtypeStruct((1, C), F32), jax.ShapeDtypeStruct((8, C), F32)],
        compiler_params=_params(1),
    )(rest, pooled, dycat, wbd, ps, cw)


def _adamw(w, gs, m, v, token=None):
    R, C = w.shape
    pieces = [p if isinstance(p, tuple) else (p,) for p in gs]
    owner = [s for s, p in enumerate(pieces) for _ in p]
    flat = [a for p in pieces for a in p]
    n = len(flat)
    rows = R // len(pieces)
    tr = _tile(rows, 256)
    per = rows // tr
    tok_ops, tok_specs = _token_operand(token)

    def kern(w_ref, *refs):
        g_refs, (m_ref, v_ref), (g_out, d_ref, nm_ref, nv_ref) = refs[:n], refs[n:n + 2], refs[n + 2 + len(tok_ops):]
        vals, at = [], 0
        for p in pieces:
            vals.append(g_refs[at][...] if len(p) == 1 else g_refs[at][...] + g_refs[at + 1][...])
            at += len(p)
        gv = vals[0]
        for s in range(1, len(pieces)):
            gv = jnp.where(pl.program_id(0) // per == s, vals[s], gv)
        nm = ADAM_B1 * m_ref[...] + (1.0 - ADAM_B1) * gv
        nv = ADAM_B2 * v_ref[...] + (1.0 - ADAM_B2) * (gv * gv)
        m_hat = nm / (1.0 - ADAM_B1 ** ADAM_STEP)
        v_hat = nv / (1.0 - ADAM_B2 ** ADAM_STEP)
        g_out[...] = gv
        d_ref[...] = -ADAM_LR * (m_hat / (jnp.sqrt(v_hat) + ADAM_EPS) + ADAM_WD * w_ref[...])
        nm_ref[...] = nm
        nv_ref[...] = nv

    def piece(s):
        return pl.BlockSpec((tr, C), lambda i: (jnp.clip(i - s * per, 0, per - 1), 0))

    blk = pl.BlockSpec((tr, C), lambda i: (i, 0))
    return pl.pallas_call(
        kern, name="adamw", grid=(R // tr,), in_specs=[blk] + [piece(s) for s in owner] + [blk] * 2 + tok_specs,
        out_specs=[blk] * 4, out_shape=[jax.ShapeDtypeStruct((R, C), F32)] * 4, compiler_params=_params(1),
    )(w, *flat, m, v, *tok_ops)


def _place():
    x, y, c = lax.axis_index("x"), lax.axis_index("y"), lax.axis_index("c")
    return x, y, c, [(1 - x, y), (x, 1 - y), (1 - x, 1 - y)]


def _comm_call(name, body, operands, out_shape, n_sems, aliases=None):
    any_spec = pl.BlockSpec(memory_space=pl.ANY)
    return pl.pallas_call(
        body, name=name, in_specs=[any_spec] * len(operands), out_specs=[any_spec] * len(out_shape),
        out_shape=out_shape, input_output_aliases=aliases or {},
        scratch_shapes=[pltpu.SemaphoreType.DMA((n,)) for n in n_sems],
    )(*operands)


def _my_block():
    return 2 * lax.axis_index("x") + lax.axis_index("y")


def _place_shard(w, dtype, first=0, count=None):
    L, R, C = w.shape
    count = L if count is None else count
    tr = _tile(R, 512)

    def kern(w_ref, o_ref):
        o_ref[...] = w_ref[...].astype(dtype)

    return pl.pallas_call(
        kern, name="place_shard", grid=(count, R // tr),
        in_specs=[pl.BlockSpec((None, tr, C), lambda l, i: (first + l, i, 0))],
        out_specs=pl.BlockSpec((None, None, tr, C), lambda l, i: (l, _my_block(), i, 0)),
        out_shape=jax.ShapeDtypeStruct((count, N_CHIPS, R, C), dtype), compiler_params=_params(2),
    )(w)


HALF_ROWS = 16


def _rows(ref, half):
    hr = ref.shape[-2] // 2
    return ref.at[(slice(None),) * (len(ref.shape) - 2) + (pl.ds(half * hr, hr),)]


def _all_gather(bufs):
    n = len(bufs)

    def body(*refs):
        outs = refs[n:2 * n]
        send_sems, recv_sems = refs[2 * n:]
        x, y, c, chips = _place()
        sibling = (x, y, 1 - c)

        def remote(k, j, chip, half, to):
            blk = 2 * chip[0] + chip[1]
            if outs[k].shape[2] % (2 * HALF_ROWS) == 0:
                region = _rows(outs[k].at[:, blk], half)
            else:
                hl = outs[k].shape[0] // 2
                region = outs[k].at[pl.ds(half * hl, hl), blk]
            return pltpu.make_async_remote_copy(
                src_ref=region, dst_ref=region, send_sem=send_sems.at[6 * k + j],
                recv_sem=recv_sems.at[6 * k + j], device_id=to, device_id_type=MESH)

        first = [remote(k, j, (x, y), c, (*chip, c)) for k in range(n) for j, chip in enumerate(chips)]
        for cp in first:
            cp.start()
        passed = []
        for k in range(n):
            for j, chip in enumerate(chips):
                remote(k, j, chip, c, (x, y, c)).wait_recv()
                passed.append(remote(k, 3 + j, chip, c, sibling))
                passed[-1].start()
        for k in range(n):
            for j, chip in enumerate(chips):
                remote(k, 3 + j, chip, 1 - c, (x, y, c)).wait_recv()
        for cp in first + passed:
            cp.wait_send()

    out_shape = [jax.ShapeDtypeStruct(s.shape, s.dtype) for s in bufs]
    return _comm_call("all_gather_weights", body, bufs, out_shape, (6 * n, 6 * n), aliases={k: k for k in range(n)})


_HBM = pl.BlockSpec(memory_space=pltpu.HBM)
_SEM = pl.BlockSpec(memory_space=pltpu.SEMAPHORE)
_ANY = pl.BlockSpec(memory_space=pl.ANY)


def _split_start(name, bufs, n_copies, make_copies, after):
    n = len(bufs)

    def body(*refs):
        send_sems, recv_sems, token = refs[n + 1], refs[n + 2], refs[2 * n + 3]
        for cp in make_copies(refs[:n], send_sems, recv_sems):
            cp.start()
        token[...] = jnp.zeros_like(token)

    res = pl.pallas_call(
        body, name=name, in_specs=[_HBM] * n + [_ANY],
        out_shape=(pltpu.SemaphoreType.DMA((n_copies,)), pltpu.SemaphoreType.DMA((n_copies,)),
                   *[pltpu.HBM(b.shape, b.dtype) for b in bufs], jax.ShapeDtypeStruct((8, LANES), F32)),
        out_specs=(_SEM, _SEM, *[_HBM] * n, pl.BlockSpec(memory_space=pltpu.VMEM)),
        input_output_aliases={i: 2 + i for i in range(n)},
        compiler_params=pltpu.CompilerParams(has_side_effects=pltpu.SideEffectType.DATAFLOW_SIDE_EFFECTING),
    )(*[pltpu.with_memory_space_constraint(b, pltpu.HBM) for b in bufs], after)
    return res[0], res[1], list(res[2:2 + n]), res[2 + n]


def _split_wait(name, send_sems, recv_sems, bufs, make_copies, after):
    n = len(bufs)

    def body(*refs):
        for cp in make_copies(refs[:n], refs[n], refs[n + 1]):
            cp.wait_send()
            cp.wait_recv()

    return list(pl.pallas_call(
        body, name=name, in_specs=[_HBM] * n + [_SEM, _SEM, _ANY],
        out_shape=tuple(pltpu.HBM(b.shape, b.dtype) for b in bufs), out_specs=tuple([_HBM] * n),
        input_output_aliases={i: i for i in range(n)},
        compiler_params=pltpu.CompilerParams(has_side_effects=pltpu.SideEffectType.DATAFLOW_SIDE_EFFECTING),
    )(*bufs, send_sems, recv_sems, after))


def _gather_copies(refs, send_sems, recv_sems):
    x, y, c, chips = _place()
    return [pltpu.make_async_remote_copy(
        src_ref=ref.at[:, 2 * x + y], dst_ref=ref.at[:, 2 * x + y], send_sem=send_sems.at[3 * k + j],
        recv_sem=recv_sems.at[3 * k + j], device_id=(*chip, c), device_id_type=MESH)
        for k, ref in enumerate(refs) for j, chip in enumerate(chips)]


def _exchange_copies(refs, send_sems, recv_sems):
    n = len(refs) // 2
    x, y, c, chips = _place()
    return [pltpu.make_async_remote_copy(
        src_ref=refs[k].at[:, 2 * chip[0] + chip[1]], dst_ref=refs[n + k].at[j], send_sem=send_sems.at[3 * k + j],
        recv_sem=recv_sems.at[3 * k + j], device_id=(*chip, c), device_id_type=MESH)
        for k in range(n) for j, chip in enumerate(chips)]


def _all_reduce_small(v, after):
    n = v.shape[0]

    def body(v_ref, after_ref, o_ref, gbuf, send_sems, recv_sems):
        x, y, c, _ = _place()
        me = 4 * x + 2 * y + c
        gbuf[me] = v_ref[...]
        copies, waits = [], []
        for r in range(1, N_DEV):
            px = 1 - x if r & 4 else x
            py = 1 - y if r & 2 else y
            pc = 1 - c if r & 1 else c
            mk = functools.partial(pltpu.make_async_remote_copy, src_ref=v_ref, send_sem=send_sems.at[r - 1],
                                   recv_sem=recv_sems.at[r - 1], device_id=(px, py, pc), device_id_type=MESH)
            copies.append(mk(dst_ref=gbuf.at[me]))
            waits.append(mk(dst_ref=gbuf.at[4 * px + 2 * py + pc]))
        for cp in copies:
            cp.start()
        for cp in waits:
            cp.wait_recv()
        for cp in copies:
            cp.wait_send()
        acc = gbuf[0]
        for d in range(1, N_DEV):
            acc = acc + gbuf[d]
        o_ref[...] = acc

    vm = pl.BlockSpec(memory_space=pltpu.VMEM)
    return pl.pallas_call(
        body, name="all_reduce_small", in_specs=[vm, _ANY], out_specs=vm, out_shape=jax.ShapeDtypeStruct(v.shape, F32),
        scratch_shapes=[pltpu.VMEM((N_DEV, n, LANES), F32), pltpu.SemaphoreType.DMA((N_DEV - 1,)),
                        pltpu.SemaphoreType.DMA((N_DEV - 1,))],
        compiler_params=pltpu.CompilerParams(vmem_limit_bytes=VMEM_LIMIT),
    )(v, after)


def _add_blocks(p, h2):
    L, nb, hr, C = p.shape
    tr = _tile(hr, 512)

    def kern(p_ref, h0_ref, h1_ref, h2_ref, o_ref):
        o_ref[...] = ((p_ref[...].astype(F32) + h0_ref[...].astype(F32)) + h1_ref[...].astype(F32)) + h2_ref[...].astype(F32)

    def other(j):
        return pl.BlockSpec((None, None, tr, C), lambda l, i: (j, l, i, 0))

    return pl.pallas_call(
        kern, name="rs_add_blocks", grid=(L, hr // tr),
        in_specs=[pl.BlockSpec((None, None, tr, C), lambda l, i: (l, _my_block(), i, 0)), other(0), other(1), other(2)],
        out_specs=pl.BlockSpec((None, tr, C), lambda l, i: (l, i, 0)),
        out_shape=jax.ShapeDtypeStruct((L, hr, C), F32), compiler_params=_params(2),
    )(p, h2, h2, h2)


WEIGHTS = ("norm_ffn1", "w_ffn1_in", "w_ffn1_out", "norm_mix", "w_mix_in", "b_forget", "w_pool", "pool_scale",
           "conv_w", "w_mix_out", "norm_ffn2", "w_ffn2_in", "w_ffn2_out", "norm_final")
BIG = ("w_ffn1_in", "w_ffn1_out", "w_mix_in", "w_mix_out", "w_ffn2_in", "w_ffn2_out")
SMALL = ("norm_ffn1", "norm_mix", "b_forget", "w_pool", "pool_scale", "conv_w", "norm_ffn2", "norm_final")


def _layer_params(small, gathered, conv_w, D, l):
    DA, C, H = D // 2, D // 4, D // 2 // HEAD_DIM
    P = {}
    if "w_ffn1_in" in gathered:
        P.update(g1=small["norm_ffn1"][l][None], w1in=(gathered["w_ffn1_in"], 0),
                 w1out=(gathered["w_ffn1_out"].reshape(1, -1, D), 0))
    if "w_ffn2_in" in gathered:
        P.update(g2=small["norm_ffn2"][l][None], w2in=(gathered["w_ffn2_in"], 0),
                 w2out=(gathered["w_ffn2_out"].reshape(1, -1, D), 0))
    if "w_mix_in" in gathered:
        w_in = jnp.concatenate([gathered["w_mix_in"][:, b] for b in range(N_CHIPS)], axis=2)
        wqkv, wrest = w_in[:, :, :3 * DA], w_in[:, :, 3 * DA + H:]
        wf = jnp.pad(w_in[:, :, 3 * DA:3 * DA + H], ((0, 0), (0, 0), (0, LANES - H)))
        ng = len(POOL_WINDOWS)
        same_group = jnp.eye(ng, dtype=bool)[:, None, :, None]
        wbd = jnp.where(same_group, small["w_pool"][l][:, :, None, :], 0.0).reshape(C, C)
        cw = jnp.concatenate([conv_w[l, b] for b in range(N_CHIPS)], axis=1)
        P.update(gm=small["norm_mix"][l][None], wp=(jnp.concatenate([wqkv, wrest, wf], axis=2), 0),
                 wmixout=(gathered["w_mix_out"].reshape(1, D, D), 0),
                 bias=jnp.pad(small["b_forget"][l][None], ((0, 0), (0, LANES - H))), wbd=wbd.astype(BF16),
                 ps=small["pool_scale"][l][None], cw=jnp.pad(cw, ((0, 8 - CONV_WIDTH), (0, 0))))
    return P


def _ffn_fwd(x, g, w_in, w_out, token=None):
    h, jac, act = _ffn_up(x, g, w_in, token)
    return _ffn_out(act, w_out, x)[0], (x, h, jac, act)


def _ffn_bwd(dres, saved, g, w_in, w_out, token=None):
    x, h, jac, act = saved
    dgu, dx, dg = _ffn_bwd_main(dres, jac, x, g, w_out, w_in, token)
    dw_out = _ffn_dw_out(act, dres)[0]
    dw_in = _ffn_dw_in(h, dgu)[0]
    return dx, dg, dw_in, dw_out.reshape(N_CHIPS, -1, dw_out.shape[1])


def _mixer_fwd(x, P, B, S, tq):
    T, D = x.shape
    DA, C, H = D // 2, D // 4, D // 2 // HEAD_DIM
    hn, qkv, rest, fl = _mix_up(x, P["gm"], P["wp"], (3 * DA, 4 * C))
    qkv, rest, fl = qkv.reshape(B, S, 3 * DA), rest.reshape(B, S, 4 * C), fl.reshape(B, S, LANES)
    drow = _decay_fwd(fl, P["bias"]).reshape(B, 8, S // tq, tq)
    o, lse = _attn_fwd(qkv, drow, H, tq)
    ypc, pooled = _mix_local_fwd(rest, P["wbd"], P["ps"], P["cw"])
    x_out = _mix_out([o.reshape(T, DA), ypc.reshape(T, 2 * C)], P["wmixout"], x)
    return x_out, (x, hn, qkv, rest, fl, drow, o, lse, pooled, ypc)


def _mixer_bwd(dres, saved, P, B, S, tq):
    x, hn, qkv, rest, fl, drow, o, lse, pooled, ypc = saved
    T, D = x.shape
    DA, C, H = D // 2, D // 4, D // 2 // HEAD_DIM
    dycat = _proj("mix_out_bwd", dres, P["wmixout"], F32, NT).reshape(B, S, D)
    dw_out = _rows_dw("mix_out_dw", [o.reshape(T, DA), ypc.reshape(T, 2 * C)], dres, BF16)
    dq, dk, dv, ddrow, ddcol = _attn_bwd(qkv, drow, o, lse, dycat, H, tq)
    dfl, dbias = _decay_bwd(ddrow.reshape(B, 8, S), ddcol, fl, P["bias"], H)
    drest, dwbd, dps, dcw = _mix_local_bwd(rest, pooled, dycat, P["wbd"], P["ps"], P["cw"])
    pieces = [a.reshape(T, a.shape[-1]) for a in (dq, dk, dv, drest, dfl)]
    dwp = _pieces_dw("mix_in_dw", hn, pieces, F32, 512)
    dx, dg = _mix_in_bwd(pieces, x, P["gm"], dres, P["wp"])
    n_q, n_r = 3 * DA, 4 * C
    dw_in = jnp.concatenate([dwp[:, :n_q], dwp[:, n_q + n_r:n_q + n_r + H], dwp[:, n_q:n_q + n_r]], axis=1)
    dw_in = dw_in.reshape(D, N_CHIPS, -1).transpose(1, 0, 2).astype(BF16)
    ng = len(POOL_WINDOWS)
    same_group = jnp.eye(ng, dtype=bool)[:, None, :, None]
    dw_pool = jnp.where(same_group, dwbd.reshape(ng, C // ng, ng, C // ng), 0.0).sum(axis=2)
    small = dict(norm_mix=dg[0], b_forget=dbias[0, :H], w_pool=dw_pool, pool_scale=dps[0], conv_w=dcw[:CONV_WIDTH])
    return dx, small, dw_in, dw_out.reshape(N_CHIPS, -1, D)


FFN1, MIX, FFN2 = BIG[:2], BIG[2:4], BIG[4:]


def _local_step(x, target, small, conv_w, pipe):
    B, S, D = x.shape
    L = small["norm_ffn1"].shape[0]
    tq = _tile(S, 256)
    xt = x.reshape(B * S, D)
    saved, params = [], []
    for l in range(L):
        P = _layer_params(small, pipe.weights(l, xt), conv_w, D, l)
        xt, s1 = _ffn_fwd(xt, P["g1"], P["w1in"], P["w1out"], pipe.token(l))
        P.update(_layer_params(small, pipe.weights_mix(l, xt), conv_w, D, l))
        xt, s2 = _mixer_fwd(xt, P, B, S, tq)
        P.update(_layer_params(small, pipe.weights_ffn2(l, xt), conv_w, D, l))
        xt, s3 = _ffn_fwd(xt, P["g2"], P["w2in"], P["w2out"])
        saved.append((s1, s2, s3))
        params.append(P)
    dres, dgf, loss = _final_loss(xt, small["norm_final"][None], target.reshape(B * S, D))
    sm = {k: [None] * L for k in SMALL if k != "norm_final"}
    token = None
    for l in reversed(range(L)):
        P, (s1, s2, s3) = params[l], saved[l]
        big = {}
        dres, dg2, big["w_ffn2_in"], big["w_ffn2_out"] = _ffn_bwd(dres, s3, P["g2"], P["w2in"], P["w2out"], token)
        dres, smix, big["w_mix_in"], big["w_mix_out"] = _mixer_bwd(dres, s2, P, B, S, tq)
        big = {k: val[None] for k, val in big.items()}
        token = pipe.grads(l, FFN2 + MIX, big, dres) if l == 0 else None
        dres, dg1, dw_in, dw_out = _ffn_bwd(dres, s1, P["g1"], P["w1in"], P["w1out"], token)
        big.update(w_ffn1_in=dw_in[None], w_ffn1_out=dw_out[None])
        sm["norm_ffn1"][l], sm["norm_ffn2"][l] = dg1[0], dg2[0]
        for k, val in smix.items():
            sm[k][l] = val
        token = pipe.grads(l, FFN1 if l == 0 else BIG, big, big["w_ffn1_in"])
    sm = {k: jnp.stack(val) for k, val in sm.items()}
    sm["norm_final"] = dgf[0]
    return loss[0, 0], dres.reshape(B, S, D), sm


def _sibling_copies(refs, send_sems, recv_sems):
    n = len(refs) // 2
    x, y, c, _ = _place()
    return [pltpu.make_async_remote_copy(
        src_ref=refs[k], dst_ref=refs[n + k], send_sem=send_sems.at[k], recv_sem=recv_sems.at[k],
        device_id=(x, y, 1 - c), device_id_type=MESH) for k in range(n)]


class _Pipeline:
    def __init__(self, w):
        self.w, self.n_layers = w, w[BIG[0]].shape[0]
        first = _all_gather([_place_shard(w[k], BF16, 0, 1) for k in FFN1] + [_place_shard(w["conv_w"], F32)])
        self.conv_w = first[-1]
        self._ready = dict(zip(FFN1, first[:-1]))
        self._mix = self._start_gather("0_mix", MIX, 0, first[0])
        self._ffn2 = self._start_gather("0_ffn2", FFN2, 0, self._mix[1][3])
        self._next = (1, self._start_gather("1", BIG, 1, self._ffn2[1][3]))
        self._reduce, self._swaps = None, []
        self.reduced = [dict() for _ in range(self.n_layers)]

    def _start_gather(self, tag, kinds, l, after):
        placed = [_place_shard(self.w[k], BF16, l, 1) for k in kinds]
        return kinds, _split_start(f"gather_start_{tag}", placed, 3 * len(kinds), _gather_copies, after)

    def _wait_gather(self, tag, started, after):
        kinds, (send_sems, recv_sems, bufs, _) = started
        return dict(zip(kinds, _split_wait(f"gather_wait_{tag}", send_sems, recv_sems, bufs, _gather_copies, after)))

    def token(self, l):
        return self._next[1][1][3] if self._next is not None and self._next[0] == l + 1 else None

    def weights(self, l, after):
        if l == 0:
            return self._ready
        got = self._wait_gather(str(l), self._next[1], after)
        self._next = (l + 1, self._start_gather(str(l + 1), BIG, l + 1, got[BIG[0]])) if l + 1 < self.n_layers else None
        return got

    def weights_mix(self, l, after):
        return self._wait_gather("0_mix", self._mix, after) if l == 0 else {}

    def weights_ffn2(self, l, after):
        return self._wait_gather("0_ffn2", self._ffn2, after) if l == 0 else {}

    def _finish_reduce(self, after):
        if self._reduce is None:
            return None
        tag, l, kinds, (send_sems, recv_sems, bufs, _) = self._reduce
        n = len(kinds)
        bufs = _split_wait(f"reduce_wait_{tag}", send_sems, recv_sems, bufs, _exchange_copies, after)
        mine = [_add_blocks(p, o) for p, o in zip(bufs[:n], bufs[n:])]
        lands = [lax.empty(q.shape, q.dtype) for q in mine]
        self._swaps.append((tag, l, kinds, _split_start(f"swap_start_{tag}", mine + lands, n, _sibling_copies, mine[0])))
        self._reduce = None
        return self._swaps[-1][3][3]

    def grads(self, l, kinds, big, after):
        swap_token = self._finish_reduce(after)
        grads = [big[k] for k in kinds]
        tag = str(l) if len(kinds) == len(BIG) else f"{l}_{kinds[0][2:]}"
        lands = [lax.empty((3, g.shape[0]) + g.shape[2:], g.dtype) for g in grads]
        started = _split_start(f"reduce_start_{tag}", grads + lands, 3 * len(kinds), _exchange_copies,
                               grads[0] if swap_token is None else swap_token)
        self._reduce = (tag, l, kinds, started)
        self.last_token = started[3]
        return started[3]

    def finish(self, after, last=False):
        if last:
            self._finish_reduce(after)
        for tag, l, kinds, (send_sems, recv_sems, bufs, _) in self._swaps:
            n = len(kinds)
            bufs = _split_wait(f"swap_wait_{tag}", send_sems, recv_sems, bufs, _sibling_copies, after)
            self.reduced[l].update(zip(kinds, zip(bufs[:n], bufs[n:])))
        self._swaps = []
        return self.reduced


def _pack(parts, extra=()):
    flat = jnp.concatenate([p.reshape(-1) for p in parts] + [jnp.reshape(e, (1,)) for e in extra])
    n = -(-flat.shape[0] // (8 * LANES)) * 8
    return jnp.pad(flat, (0, n * LANES - flat.shape[0])).reshape(n, LANES)


def _unpack(buf, shapes):
    flat, out, at = buf.reshape(-1), [], 0
    for s in shapes:
        n = math.prod(s)
        out.append(flat[at:at + n].reshape(s))
        at += n
    return out, flat[at:]


def kernel(x, norm_ffn1, w_ffn1_in, w_ffn1_out, norm_mix, w_mix_in, b_forget, w_pool, pool_scale, conv_w, w_mix_out, norm_ffn2, w_ffn2_in, w_ffn2_out, norm_final, loss_target, m_norm_ffn1, m_w_ffn1_in, m_w_ffn1_out, m_norm_mix, m_w_mix_in, m_b_forget, m_w_pool, m_pool_scale, m_conv_w, m_w_mix_out, m_norm_ffn2, m_w_ffn2_in, m_w_ffn2_out, m_norm_final, v_norm_ffn1, v_w_ffn1_in, v_w_ffn1_out, v_norm_mix, v_w_mix_in, v_b_forget, v_w_pool, v_pool_scale, v_conv_w, v_w_mix_out, v_norm_ffn2, v_w_ffn2_in, v_w_ffn2_out, v_norm_final):
    w = dict(zip(WEIGHTS, (norm_ffn1, w_ffn1_in, w_ffn1_out, norm_mix, w_mix_in, b_forget, w_pool, pool_scale, conv_w, w_mix_out, norm_ffn2, w_ffn2_in, w_ffn2_out, norm_final)))
    m = dict(zip(WEIGHTS, (m_norm_ffn1, m_w_ffn1_in, m_w_ffn1_out, m_norm_mix, m_w_mix_in, m_b_forget, m_w_pool, m_pool_scale, m_conv_w, m_w_mix_out, m_norm_ffn2, m_w_ffn2_in, m_w_ffn2_out, m_norm_final)))
    v = dict(zip(WEIGHTS, (v_norm_ffn1, v_w_ffn1_in, v_w_ffn1_out, v_norm_mix, v_w_mix_in, v_b_forget, v_w_pool, v_pool_scale, v_conv_w, v_w_mix_out, v_norm_ffn2, v_w_ffn2_in, v_w_ffn2_out, v_norm_final)))
    block = 2 * lax.axis_index("x") + lax.axis_index("y")

    pipe = _Pipeline(w)
    small = {k: w[k] for k in SMALL}
    loss, grad_x, sm = _local_step(x, loss_target, small, pipe.conv_w, pipe)
    grads, delta, new_m, new_v = {}, {}, {}, {}

    def big_adamw(k, reduced, token=None):
        two_d = lambda a: a.reshape(-1, a.shape[-1])
        pieces = [tuple(map(two_d, layer[k])) for layer in reduced]
        res = _adamw(two_d(w[k]), pieces, two_d(m[k]), two_d(v[k]), token)
        grads[k], delta[k], new_m[k], new_v[k] = [r.reshape(w[k].shape) for r in res]

    reduced = pipe.finish(grad_x)
    for k in BIG[2:]:
        big_adamw(k, reduced, pipe.last_token)
    reduced = pipe.finish(new_v[BIG[-1]], last=True)
    for k in BIG[:2]:
        big_adamw(k, reduced)

    order = list(SMALL)
    total = _all_reduce_small(_pack([sm[k] for k in order], extra=(loss,)), new_v[BIG[0]])
    parts, rest = _unpack(total, [sm[k].shape for k in order])
    grads.update(zip(order, parts))
    loss = rest[0]
    cs = conv_w.shape[2]
    grads["conv_w"] = lax.dynamic_slice_in_dim(grads["conv_w"], block * cs, cs, axis=2)
    packed = [_pack([t[k] for k in order]) for t in (w, grads, m, v)]
    _, d, nm, nv = _adamw(packed[0], [packed[1]], packed[2], packed[3])
    shapes = [w[k].shape for k in order]
    for res, flat in ((delta, d), (new_m, nm), (new_v, nv)):
        res.update(zip(order, _unpack(flat, shapes)[0]))
    return (loss, grad_x, *[grads[k] for k in WEIGHTS], *[delta[k] for k in WEIGHTS],
            *[new_m[k] for k in WEIGHTS], *[new_v[k] for k in WEIGHTS])
```

```python
import functools
import math

import jax
import jax.numpy as jnp
from jax import lax
from jax.experimental import pallas as pl
from jax.experimental.pallas import tpu as pltpu

F32 = jnp.float32
BF16 = jnp.bfloat16
MESH = pl.DeviceIdType.MESH

HEAD_DIM = 64
POOL_WINDOWS = (2, 4, 8, 16)
CONV_WIDTH = 3
RMS_EPS = 1e-6
ADAM_LR = 0.001
ADAM_B1 = 0.9
ADAM_B2 = 0.999
ADAM_EPS = 1e-08
ADAM_WD = 0.01
ADAM_STEP = 10

LANES = 128
VMEM_LIMIT = 56 * 1024 * 1024
N_CHIPS = 4
N_DEV = 8

NN = (((1,), (0,)), ((), ()))
NT = (((1,), (1,)), ((), ()))
TN = (((0,), (0,)), ((), ()))


def _tile(n, pref):
    for t in range(pref - pref % 16, 15, -16):
        if n % t == 0:
            return t
    return n


def _params(n_grid):
    return pltpu.CompilerParams(dimension_semantics=("arbitrary",) * n_grid, vmem_limit_bytes=VMEM_LIMIT)


def _dot(a, b, dims):
    return lax.dot_general(a, b, dims, preferred_element_type=F32)


def _mm(name, dims, operands, in_specs, out_shape, out_specs, grid, acc_shape, epilogue):
    n_in, n_out, nk = len(operands), len(out_shape), grid[-1]

    def kern(*refs):
        extras, outs = refs[2:n_in], refs[n_in:n_in + n_out]
        if nk == 1:
            epilogue(_dot(refs[0][...].astype(BF16), refs[1][...].astype(BF16), dims), extras, outs)
            return
        acc = refs[n_in + n_out]
        k = pl.program_id(len(grid) - 1)

        @pl.when(k == 0)
        def _():
            acc[...] = jnp.zeros_like(acc)

        acc[...] += _dot(refs[0][...].astype(BF16), refs[1][...].astype(BF16), dims)

        @pl.when(k == nk - 1)
        def _():
            epilogue(acc[...], extras, outs)

    return pl.pallas_call(
        kern, name=name, grid=grid, in_specs=in_specs, out_specs=out_specs, out_shape=out_shape,
        scratch_shapes=[pltpu.VMEM(acc_shape, F32)] if nk > 1 else [],
        compiler_params=_params(len(grid)),
    )(*operands)


def _store(scale=None, dtype=None):
    def ep(acc, extras, outs):
        v = acc if scale is None else acc * scale
        outs[0][...] = v.astype(outs[0].dtype)
    return ep


def _residual(scale):
    def ep(acc, extras, outs):
        outs[0][...] = extras[0][...] + scale * acc
    return ep


def _final_loss(x, g, target):
    T, D = x.shape
    tr = _tile(T, 256)

    def kern(x_ref, g_ref, t_ref, dx_ref, dg_ref, loss_ref):
        xv = x_ref[...]
        r = lax.rsqrt(jnp.mean(xv * xv, axis=-1, keepdims=True) + RMS_EPS)
        y = xv * r
        err = y * g_ref[...] - t_ref[...]
        lpart = 0.5 * jnp.sum(jnp.mean(err * err, axis=-1, keepdims=True), axis=0, keepdims=True)
        dh = err * (1.0 / D)
        dy = dh * g_ref[...]
        dx_ref[...] = r * (dy - y * jnp.mean(dy * y, axis=-1, keepdims=True))
        part = jnp.sum(dh * y, axis=0, keepdims=True)
        lrow = jnp.broadcast_to(lpart, (1, LANES))

        @pl.when(pl.program_id(0) == 0)
        def _():
            dg_ref[...] = part
            loss_ref[...] = lrow

        @pl.when(pl.program_id(0) > 0)
        def _():
            dg_ref[...] += part
            loss_ref[...] += lrow

    row = pl.BlockSpec((tr, D), lambda i: (i, 0))
    vec = pl.BlockSpec((1, D), lambda i: (0, 0))
    return pl.pallas_call(
        kern, name="final_loss", grid=(T // tr,), in_specs=[row, vec, row],
        out_specs=[row, vec, pl.BlockSpec((1, LANES), lambda i: (0, 0))],
        out_shape=[jax.ShapeDtypeStruct((T, D), F32), jax.ShapeDtypeStruct((1, D), F32),
                   jax.ShapeDtypeStruct((1, LANES), F32)],
        compiler_params=_params(1),
    )(x, g, target)


def _resident(shape, index_map):
    return pl.BlockSpec(shape, index_map, pipeline_mode=pl.Buffered(1))


def _token_operand(token):
    return ([], []) if token is None else ([token], [pl.BlockSpec(token.shape, lambda i: (0, 0))])


def _ffn_up(x, g, w4, token=None):
    T, D = x.shape
    w4, l = w4
    Fh = w4.shape[3]
    F = 2 * Fh
    tm = _tile(T, 512)
    tok_ops, tok_specs = _token_operand(token)

    def kern(x_ref, g_ref, w_ref, *rest):
        h_ref, jac_ref, act_ref = rest[len(tok_ops):]
        xv = x_ref[...]
        r = lax.rsqrt(jnp.mean(xv * xv, axis=-1, keepdims=True) + RMS_EPS)
        hv = (xv * r * g_ref[...]).astype(BF16)
        h_ref[...] = hv
        for j in range(2):
            cols = slice(j * Fh, (j + 1) * Fh)
            gate = _dot(hv, w_ref[j], NN)
            up = _dot(hv, w_ref[2 + j], NN)
            sg = jax.nn.sigmoid(gate)
            silu = gate * sg
            jac_ref[0, :, cols] = (up * (sg + silu * (1.0 - sg))).astype(BF16)
            jac_ref[1, :, cols] = silu.astype(BF16)
            act_ref[:, cols] = (silu * up).astype(BF16)

    return pl.pallas_call(
        kern, name="ffn_up", grid=(T // tm,),
        in_specs=[pl.BlockSpec((tm, D), lambda i: (i, 0)), pl.BlockSpec((1, D), lambda i: (0, 0)),
                  _resident((None, 4, D, Fh), lambda i: (l, 0, 0, 0))] + tok_specs,
        out_specs=[pl.BlockSpec((tm, D), lambda i: (i, 0)), pl.BlockSpec((2, tm, F), lambda i: (0, i, 0)),
                   pl.BlockSpec((tm, F), lambda i: (i, 0))],
        out_shape=[jax.ShapeDtypeStruct((T, D), BF16), jax.ShapeDtypeStruct((2, T, F), BF16),
                   jax.ShapeDtypeStruct((T, F), BF16)],
        compiler_params=_params(1),
    )(x, g, w4, *tok_ops)


def _ffn_bwd_main(dres, jac, x, g, w_out, w4, token=None):
    T, D = dres.shape
    w_out, l = w_out
    w4, _ = w4
    F = w_out.shape[1]
    Fh = F // 2
    tm = _tile(T, 512)
    tok_ops, tok_specs = _token_operand(token)

    def kern(d_ref, jac_ref, x_ref, g_ref, wo_ref, wi_ref, *rest):
        dgu_ref, dx_ref, dg_ref = rest[len(tok_ops):]
        dv = d_ref[...]
        d16 = dv.astype(BF16)
        dh = jnp.zeros((tm, D), F32)
        for j in range(2):
            cols = slice(j * Fh, (j + 1) * Fh)
            dact = 0.5 * _dot(d16, wo_ref[cols, :], NT)
            dgate = (dact * jac_ref[0, :, cols].astype(F32)).astype(BF16)
            dup = (dact * jac_ref[1, :, cols].astype(F32)).astype(BF16)
            dgu_ref[0, :, cols] = dgate
            dgu_ref[1, :, cols] = dup
            dh = dh + _dot(dgate, wi_ref[j], NT) + _dot(dup, wi_ref[2 + j], NT)
        xv = x_ref[...]
        r = lax.rsqrt(jnp.mean(xv * xv, axis=-1, keepdims=True) + RMS_EPS)
        y = xv * r
        dy = dh * g_ref[...]
        dx_ref[...] = dv + r * (dy - y * jnp.mean(dy * y, axis=-1, keepdims=True))
        part = jnp.sum(dh * y, axis=0, keepdims=True)

        @pl.when(pl.program_id(0) == 0)
        def _():
            dg_ref[...] = part

        @pl.when(pl.program_id(0) > 0)
        def _():
            dg_ref[...] += part

    row = pl.BlockSpec((tm, D), lambda i: (i, 0))
    vec = pl.BlockSpec((1, D), lambda i: (0, 0))
    wide = pl.BlockSpec((2, tm, F), lambda i: (0, i, 0))
    return pl.pallas_call(
        kern, name="ffn_bwd_main", grid=(T // tm,),
        in_specs=[row, wide, row, vec, _resident((None, F, D), lambda i: (l, 0, 0)),
                  _resident((None, 4, D, Fh), lambda i: (l, 0, 0, 0))] + tok_specs,
        out_specs=[wide, row, vec],
        out_shape=[jax.ShapeDtypeStruct((2, T, F), BF16), jax.ShapeDtypeStruct((T, D), F32),
                   jax.ShapeDtypeStruct((1, D), F32)],
        compiler_params=_params(1),
    )(dres, jac, x, g, w_out, w4, *tok_ops)


def _ffn_out(act, w_out, x):
    T, F = act.shape
    w_out, l = w_out
    D = w_out.shape[2]
    tm = _tile(T, 512)
    return _mm("ffn_out", NN, [act, w_out, x],
               [pl.BlockSpec((tm, F), lambda i, k: (i, 0)), pl.BlockSpec((None, F, D), lambda i, k: (l, 0, 0)),
                pl.BlockSpec((tm, D), lambda i, k: (i, 0))],
               [jax.ShapeDtypeStruct((T, D), F32)], [pl.BlockSpec((tm, D), lambda i, k: (i, 0))],
               (T // tm, 1), None, _residual(0.5))


def _ffn_dw_out(act, dres):
    T, F = act.shape
    D = dres.shape[1]
    tm, tk = F // 2, _tile(T, 1024)
    return _mm("ffn_dw_out", TN, [act, dres],
               [pl.BlockSpec((tk, tm), lambda i, k: (k, i)), pl.BlockSpec((tk, D), lambda i, k: (k, 0))],
               [jax.ShapeDtypeStruct((F, D), BF16)], [pl.BlockSpec((tm, D), lambda i, k: (i, 0))],
               (2, T // tk), (tm, D), _store(0.5))


def _ffn_dw_in(h, dgu):
    T, D = h.shape
    Fh = dgu.shape[2] // 2
    tk = _tile(T, 1024)
    return _mm("ffn_dw_in", TN, [h, dgu],
               [pl.BlockSpec((tk, D), lambda j, k: (k, 0)),
                pl.BlockSpec((None, tk, Fh), lambda j, k: (j // 2, k, j % 2))],
               [jax.ShapeDtypeStruct((4, D, Fh), BF16)], [pl.BlockSpec((None, D, Fh), lambda j, k: (j, 0, 0))],
               (4, T // tk), (D, Fh), _store())


def _proj(name, a, w, out_dtype, dims=NN, extra=None, scale=None):
    T, K = a.shape
    w, l = w
    N = w.shape[2] if dims == NN else w.shape[1]
    tm = _tile(T, 512)
    ops = [a, w] + ([extra] if extra is not None else [])
    specs = [pl.BlockSpec((tm, K), lambda i, k: (i, 0)), pl.BlockSpec((None,) + w.shape[1:], lambda i, k: (l, 0, 0))]
    if extra is not None:
        specs.append(pl.BlockSpec((tm, N), lambda i, k: (i, 0)))
    ep = _residual(1.0) if extra is not None else _store(scale)
    return _mm(name, dims, ops, specs, [jax.ShapeDtypeStruct((T, N), out_dtype)],
               [pl.BlockSpec((tm, N), lambda i, k: (i, 0))], (T // tm, 1), None, ep)[0]


def _mix_up(x, g, wp, widths):
    T, D = x.shape
    wp, l = wp
    n_qkv, n_rest = widths
    NP = wp.shape[2]
    tm = _tile(T, 512)

    def kern(x_ref, g_ref, w_ref, h_ref, qkv_ref, rest_ref, fl_ref):
        xv = x_ref[...]
        r = lax.rsqrt(jnp.mean(xv * xv, axis=-1, keepdims=True) + RMS_EPS)
        hv = (xv * r * g_ref[...]).astype(BF16)
        h_ref[...] = hv
        qkv_ref[...] = _dot(hv, w_ref[:, 0:n_qkv], NN).astype(BF16)
        rest_ref[...] = _dot(hv, w_ref[:, n_qkv:n_qkv + n_rest], NN)
        fl_ref[...] = _dot(hv, w_ref[:, n_qkv + n_rest:NP], NN)

    row = lambda n: pl.BlockSpec((tm, n), lambda i: (i, 0))
    return pl.pallas_call(
        kern, name="mix_up", grid=(T // tm,),
        in_specs=[row(D), pl.BlockSpec((1, D), lambda i: (0, 0)), _resident((None, D, NP), lambda i: (l, 0, 0))],
        out_specs=[row(D), row(n_qkv), row(n_rest), row(LANES)],
        out_shape=[jax.ShapeDtypeStruct((T, D), BF16), jax.ShapeDtypeStruct((T, n_qkv), BF16),
                   jax.ShapeDtypeStruct((T, n_rest), F32), jax.ShapeDtypeStruct((T, LANES), F32)],
        compiler_params=_params(1),
    )(x, g, wp)


def _column_starts(pieces):
    starts, at = [], 0
    for p in pieces:
        starts.append(at)
        at += p.shape[1]
    return starts


def _mix_in_bwd(pieces, x, g, dres, wp):
    T, D = x.shape
    wp, l = wp
    NP = wp.shape[2]
    tm = _tile(T, 512)
    n, starts = len(pieces), _column_starts(pieces)

    def kern(*refs):
        x_ref, g_ref, d_ref, w_ref, dx_ref, dg_ref = refs[n:]
        dh = jnp.zeros((tm, D), F32)
        for p_ref, at in zip(refs[:n], starts):
            dh = dh + _dot(p_ref[...].astype(BF16), w_ref[:, at:at + p_ref.shape[1]], NT)
        xv = x_ref[...]
        r = lax.rsqrt(jnp.mean(xv * xv, axis=-1, keepdims=True) + RMS_EPS)
        y = xv * r
        dy = dh * g_ref[...]
        dx_ref[...] = d_ref[...] + r * (dy - y * jnp.mean(dy * y, axis=-1, keepdims=True))
        part = jnp.sum(dh * y, axis=0, keepdims=True)

        @pl.when(pl.program_id(0) == 0)
        def _():
            dg_ref[...] = part

        @pl.when(pl.program_id(0) > 0)
        def _():
            dg_ref[...] += part

    row = lambda n: pl.BlockSpec((tm, n), lambda i: (i, 0))
    vec = pl.BlockSpec((1, D), lambda i: (0, 0))
    return pl.pallas_call(
        kern, name="mix_in_bwd", grid=(T // tm,),
        in_specs=[row(p.shape[1]) for p in pieces] + [row(D), vec, row(D), _resident((None, D, NP), lambda i: (l, 0, 0))],
        out_specs=[row(D), vec],
        out_shape=[jax.ShapeDtypeStruct((T, D), F32), jax.ShapeDtypeStruct((1, D), F32)],
        compiler_params=_params(1),
    )(*pieces, x, g, dres, wp)


def _pieces_dw(name, a, pieces, out_dtype, tk_pref):
    T, M = a.shape
    n, starts = len(pieces), _column_starts(pieces)
    N = starts[-1] + pieces[-1].shape[1]
    tk = _tile(T, tk_pref)
    nk = T // tk

    def kern(a_ref, *refs):
        o_ref, acc = refs[n], refs[n + 1]
        k = pl.program_id(0)

        @pl.when(k == 0)
        def _():
            acc[...] = jnp.zeros_like(acc)

        av = a_ref[...].astype(BF16)
        for p_ref, at in zip(refs[:n], starts):
            acc[:, at:at + p_ref.shape[1]] += _dot(av, p_ref[...].astype(BF16), TN)

        @pl.when(k == nk - 1)
        def _():
            o_ref[...] = acc[...].astype(out_dtype)

    return pl.pallas_call(
        kern, name=name, grid=(nk,),
        in_specs=[pl.BlockSpec((tk, M), lambda k: (k, 0))] + [pl.BlockSpec((tk, p.shape[1]), lambda k: (k, 0)) for p in pieces],
        out_specs=pl.BlockSpec((M, N), lambda k: (0, 0)), out_shape=jax.ShapeDtypeStruct((M, N), out_dtype),
        scratch_shapes=[pltpu.VMEM((M, N), F32)], compiler_params=_params(1),
    )(a, *pieces)


def _rows_dw(name, pieces, d, out_dtype):
    T, N = d.shape
    n, starts = len(pieces), _column_starts(pieces)
    M = starts[-1] + pieces[-1].shape[1]
    tk = _tile(T, 1024)
    nk = T // tk

    def kern(*refs):
        d_ref, o_ref, acc = refs[n], refs[n + 1], refs[n + 2]
        k = pl.program_id(0)

        @pl.when(k == 0)
        def _():
            acc[...] = jnp.zeros_like(acc)

        dv = d_ref[...].astype(BF16)
        for p_ref, at in zip(refs[:n], starts):
            acc[at:at + p_ref.shape[1], :] += _dot(p_ref[...], dv, TN)

        @pl.when(k == nk - 1)
        def _():
            o_ref[...] = acc[...].astype(out_dtype)

    return pl.pallas_call(
        kern, name=name, grid=(nk,),
        in_specs=[pl.BlockSpec((tk, p.shape[1]), lambda k: (k, 0)) for p in pieces] + [pl.BlockSpec((tk, N), lambda k: (k, 0))],
        out_specs=pl.BlockSpec((M, N), lambda k: (0, 0)), out_shape=jax.ShapeDtypeStruct((M, N), out_dtype),
        scratch_shapes=[pltpu.VMEM((M, N), F32)], compiler_params=_params(1),
    )(*pieces, d)


def _mix_out(pieces, w, x):
    T, D = x.shape
    w, l = w
    n, starts = len(pieces), _column_starts(pieces)
    tm = _tile(T, 512)

    def kern(*refs):
        w_ref, x_ref, o_ref = refs[n:]
        acc = x_ref[...]
        for p_ref, at in zip(refs[:n], starts):
            acc = acc + _dot(p_ref[...], w_ref[at:at + p_ref.shape[1], :], NN)
        o_ref[...] = acc

    row = lambda m: pl.BlockSpec((tm, m), lambda i: (i, 0))
    return pl.pallas_call(
        kern, name="mix_out", grid=(T // tm,),
        in_specs=[row(p.shape[1]) for p in pieces] + [_resident((None,) + w.shape[1:], lambda i: (l, 0, 0)), row(D)],
        out_specs=row(D), out_shape=jax.ShapeDtypeStruct((T, D), F32), compiler_params=_params(1),
    )(*pieces, w, x)


def _log_sigmoid(z):
    return jnp.minimum(z, 0.0) - jnp.log(1.0 + jnp.exp(-jnp.abs(z)))


def _decay_fwd(fl, bias):
    B, S, _ = fl.shape

    def kern(fl_ref, b_ref, o_ref):
        d = _log_sigmoid(fl_ref[...] + b_ref[...])
        row = lax.broadcasted_iota(jnp.int32, (S, LANES), 0)
        sh = 1
        while sh < S:
            d = d + jnp.where(row >= sh, pltpu.roll(d, sh, 0), 0.0)
            sh *= 2
        o_ref[...] = d.T[0:8, :]

    return pl.pallas_call(
        kern, name="decay_fwd", grid=(B,),
        in_specs=[pl.BlockSpec((None, S, LANES), lambda b: (b, 0, 0)), pl.BlockSpec((1, LANES), lambda b: (0, 0))],
        out_specs=pl.BlockSpec((None, 8, S), lambda b: (b, 0, 0)),
        out_shape=jax.ShapeDtypeStruct((B, 8, S), F32), compiler_params=_params(1),
    )(fl, bias)


def _decay_bwd(ddrow, ddcol, fl, bias, n_heads):
    B, S, _ = fl.shape

    def kern(dd_ref, ddc_ref, fl_ref, b_ref, dfl_ref, db_ref):
        dd = jnp.concatenate([dd_ref[...], jnp.zeros((LANES - 8, S), F32)], axis=0).T + ddc_ref[...]
        row = lax.broadcasted_iota(jnp.int32, (S, LANES), 0)
        lane = lax.broadcasted_iota(jnp.int32, (S, LANES), 1)
        sh = 1
        while sh < S:
            dd = dd + jnp.where(row < S - sh, pltpu.roll(dd, S - sh, 0), 0.0)
            sh *= 2
        z = fl_ref[...] + b_ref[...]
        dfl = jnp.where(lane < n_heads, dd / (1.0 + jnp.exp(z)), 0.0)
        dfl_ref[...] = dfl
        part = jnp.sum(dfl, axis=0, keepdims=True)

        @pl.when(pl.program_id(0) == 0)
        def _():
            db_ref[...] = part

        @pl.when(pl.program_id(0) > 0)
        def _():
            db_ref[...] += part

    return pl.pallas_call(
        kern, name="decay_bwd", grid=(B,),
        in_specs=[pl.BlockSpec((None, 8, S), lambda b: (b, 0, 0)), pl.BlockSpec((None, S, LANES), lambda b: (b, 0, 0)),
                  pl.BlockSpec((None, S, LANES), lambda b: (b, 0, 0)), pl.BlockSpec((1, LANES), lambda b: (0, 0))],
        out_specs=[pl.BlockSpec((None, S, LANES), lambda b: (b, 0, 0)), pl.BlockSpec((1, LANES), lambda b: (0, 0))],
        out_shape=[jax.ShapeDtypeStruct((B, S, LANES), F32), jax.ShapeDtypeStruct((1, LANES), F32)],
        compiler_params=_params(1),
    )(ddrow, ddcol, fl, bias)


def _attn_fwd(qkv, drow, n_heads, tq):
    B, S, _ = qkv.shape
    DA = n_heads * HEAD_DIM
    scale = HEAD_DIM ** -0.5

    n_pairs = n_heads // 2

    def kern(q_ref, k_ref, v_ref, dr_ref, o_ref, lse_ref):
        i = pl.program_id(1)
        lane = lax.broadcasted_iota(jnp.int32, (tq, LANES), 1)
        low = lane < HEAD_DIM
        causal = lax.broadcasted_iota(jnp.int32, (tq, tq), 1) <= lax.broadcasted_iota(jnp.int32, (tq, tq), 0)
        qms = []
        for p in range(n_pairs):
            q2 = q_ref[:, LANES * p:LANES * (p + 1)] * scale
            qms += [jnp.where(low, q2, jnp.zeros_like(q2)), jnp.where(low, jnp.zeros_like(q2), q2)]

        def step(j, carry, masked):
            ms, ls, accs = carry
            ks = pl.multiple_of(j * tq, tq)
            new_m, new_l, new_acc = [], [], []
            for p in range(n_pairs):
                cols = slice(LANES * p, LANES * (p + 1))
                k2, v2 = k_ref[pl.ds(ks, tq), cols], v_ref[pl.ds(ks, tq), cols]
                alphas, pvs = [], []
                for h in (2 * p, 2 * p + 1):
                    s = _dot(qms[h], k2, NT) - dr_ref[h, pl.ds(j, 1), :]
                    if masked:
                        s = jnp.where(causal, s, -jnp.inf)
                    m_new = jnp.maximum(ms[h], jnp.max(s, axis=1, keepdims=True))
                    alpha = jnp.exp(ms[h] - m_new)
                    pm = jnp.exp(s - m_new)
                    new_m.append(m_new)
                    new_l.append(alpha * ls[h] + jnp.sum(pm, axis=1, keepdims=True))
                    alphas.append(alpha)
                    pvs.append(_dot(pm.astype(BF16), v2, NN))
                new_acc.append(jnp.where(low, alphas[0], alphas[1]) * accs[p] + jnp.where(low, pvs[0], pvs[1]))
            return tuple(new_m), tuple(new_l), tuple(new_acc)

        init = (tuple(jnp.full((tq, 1), -jnp.inf, F32) for _ in range(n_heads)),
                tuple(jnp.zeros((tq, 1), F32) for _ in range(n_heads)),
                tuple(jnp.zeros((tq, LANES), F32) for _ in range(n_pairs)))
        ms, ls, accs = step(i, lax.fori_loop(0, i, functools.partial(step, masked=False), init), True)
        lse_mat = jnp.zeros((tq, LANES), F32)
        for p in range(n_pairs):
            l0, l1 = ls[2 * p], ls[2 * p + 1]
            o_ref[:, LANES * p:LANES * (p + 1)] = (accs[p] / jnp.where(low, l0, l1)).astype(BF16)
            lse_mat = jnp.where(lane == 2 * p, ms[2 * p] + jnp.log(l0), lse_mat)
            lse_mat = jnp.where(lane == 2 * p + 1, ms[2 * p + 1] + jnp.log(l1), lse_mat)
        lse_ref[...] = lse_mat

    nq = S // tq
    return pl.pallas_call(
        kern, name="attn_fwd", grid=(B, nq),
        in_specs=[pl.BlockSpec((None, tq, DA), lambda b, i: (b, i, 0)),
                  pl.BlockSpec((None, S, DA), lambda b, i: (b, 0, 1)),
                  pl.BlockSpec((None, S, DA), lambda b, i: (b, 0, 2)),
                  pl.BlockSpec((None, 8, nq, tq), lambda b, i: (b, 0, 0, 0))],
        out_specs=[pl.BlockSpec((None, tq, DA), lambda b, i: (b, i, 0)),
                   pl.BlockSpec((None, tq, LANES), lambda b, i: (b, i, 0))],
        out_shape=[jax.ShapeDtypeStruct((B, S, DA), BF16), jax.ShapeDtypeStruct((B, S, LANES), F32)],
        compiler_params=_params(2),
    )(qkv, qkv, qkv, drow)


def _attn_bwd(qkv, drow, o, lse, dycat, n_heads, tq):
    B, S, _ = qkv.shape
    DA = n_heads * HEAD_DIM
    scale = HEAD_DIM ** -0.5
    nq = S // tq

    n_pairs = n_heads // 2

    def kern(q_ref, k_ref, v_ref, dr_ref, o_ref, lse_ref, do_ref, dq_ref, dk_ref, dv_ref, ddr_ref, ddc_ref,
             dk_acc, dv_acc, qm_s, dom_s, delta_s, rs_s, dq_s):
        i = pl.program_id(1)

        @pl.when(i == 0)
        def _():
            dk_acc[...] = jnp.zeros_like(dk_acc)
            dv_acc[...] = jnp.zeros_like(dv_acc)
            ddr_ref[...] = jnp.zeros_like(ddr_ref)

        lane = lax.broadcasted_iota(jnp.int32, (tq, LANES), 1)
        low = lane < HEAD_DIM
        causal = lax.broadcasted_iota(jnp.int32, (tq, tq), 1) <= lax.broadcasted_iota(jnp.int32, (tq, tq), 0)
        for p in range(n_pairs):
            cols = slice(LANES * p, LANES * (p + 1))
            q2 = q_ref[:, cols] * scale
            do_f = do_ref[:, cols]
            do2 = do_f.astype(BF16)
            prod = do_f * o_ref[:, cols].astype(F32)
            qm_s[2 * p] = jnp.where(low, q2, jnp.zeros_like(q2))
            qm_s[2 * p + 1] = jnp.where(low, jnp.zeros_like(q2), q2)
            dom_s[2 * p] = jnp.where(low, do2, jnp.zeros_like(do2))
            dom_s[2 * p + 1] = jnp.where(low, jnp.zeros_like(do2), do2)
            delta_s[2 * p] = jnp.sum(jnp.where(low, prod, 0.0), axis=1, keepdims=True)
            delta_s[2 * p + 1] = jnp.sum(jnp.where(low, 0.0, prod), axis=1, keepdims=True)
            dq_s[p] = jnp.zeros((tq, LANES), F32)
        rs_s[...] = jnp.zeros(rs_s.shape, F32)

        def step(j, masked):
            ks = pl.multiple_of(j * tq, tq)
            for p in range(n_pairs):
                cols = slice(LANES * p, LANES * (p + 1))
                k2, v2 = k_ref[pl.ds(ks, tq), cols], v_ref[pl.ds(ks, tq), cols]
                dvs, dks, dqs = [], [], []
                for h in (2 * p, 2 * p + 1):
                    qm, dom = qm_s[h], dom_s[h]
                    s = _dot(qm, k2, NT) - dr_ref[h, pl.ds(j, 1), :]
                    if masked:
                        s = jnp.where(causal, s, -jnp.inf)
                    pm = jnp.exp(s - lse_ref[:, h:h + 1])
                    ds = pm * (_dot(dom, v2, NT) - delta_s[h])
                    ddr_ref[h, pl.ds(j, 1), :] -= jnp.sum(ds, axis=0, keepdims=True)
                    rs_s[h] += jnp.sum(ds, axis=1, keepdims=True)
                    dsb = ds.astype(BF16)
                    dvs.append(_dot(pm.astype(BF16), dom, TN))
                    dks.append(_dot(dsb, qm, TN))
                    dqs.append(_dot(dsb, k2, NN))
                dv_acc[pl.ds(ks, tq), cols] += dvs[0] + dvs[1]
                dk_acc[pl.ds(ks, tq), cols] += dks[0] + dks[1]
                dq_s[p] += jnp.where(low, dqs[0], dqs[1])

        def body(j, carry):
            step(j, False)
            return carry

        lax.fori_loop(0, i, body, 0)
        step(i, True)
        ddc = jnp.zeros((tq, LANES), F32)
        for p in range(n_pairs):
            dq_ref[:, LANES * p:LANES * (p + 1)] = (dq_s[p] * scale).astype(BF16)
            ddc = jnp.where(lane == 2 * p, rs_s[2 * p], ddc)
            ddc = jnp.where(lane == 2 * p + 1, rs_s[2 * p + 1], ddc)
        ddc_ref[...] = ddc

        @pl.when(i == nq - 1)
        def _():
            dk_ref[...] = dk_acc[...].astype(BF16)
            dv_ref[...] = dv_acc[...].astype(BF16)

    tile = pl.BlockSpec((None, tq, DA), lambda b, i: (b, i, 0))
    seq = pl.BlockSpec((None, S, DA), lambda b, i: (b, 0, 0))
    dec = pl.BlockSpec((None, 8, nq, tq), lambda b, i: (b, 0, 0, 0))
    return pl.pallas_call(
        kern, name="attn_bwd", grid=(B, nq),
        in_specs=[tile, pl.BlockSpec((None, S, DA), lambda b, i: (b, 0, 1)),
                  pl.BlockSpec((None, S, DA), lambda b, i: (b, 0, 2)), dec, tile,
                  pl.BlockSpec((None, tq, LANES), lambda b, i: (b, i, 0)), tile],
        out_specs=[tile, seq, seq, dec, pl.BlockSpec((None, tq, LANES), lambda b, i: (b, i, 0))],
        out_shape=[jax.ShapeDtypeStruct((B, S, DA), BF16)] * 3 + [jax.ShapeDtypeStruct((B, 8, nq, tq), F32),
                                                                  jax.ShapeDtypeStruct((B, S, LANES), F32)],
        scratch_shapes=[pltpu.VMEM((S, DA), F32), pltpu.VMEM((S, DA), F32),
                        pltpu.VMEM((n_heads, tq, LANES), BF16), pltpu.VMEM((n_heads, tq, LANES), BF16),
                        pltpu.VMEM((n_heads, tq, 1), F32), pltpu.VMEM((n_heads, tq, 1), F32),
                        pltpu.VMEM((n_pairs, tq, LANES), F32)],
        compiler_params=_params(2),
    )(qkv, qkv, qkv, drow, o, lse, dycat)


def _down(v, d, row):
    return jnp.where(row >= d, pltpu.roll(v, d, 0), 0.0)


def _up(v, d, row, S):
    return jnp.where(row < S - d, pltpu.roll(v, S - d, 0), 0.0)


def _window(v, shift, group):
    sums, acc, d = [], v, 1
    for _ in POOL_WINDOWS:
        acc = acc + shift(acc, d)
        sums.append(acc)
        d *= 2
    out = sums[-1]
    for gi in range(len(POOL_WINDOWS) - 2, -1, -1):
        out = jnp.where(group == gi, sums[gi], out)
    return out


def _pool_count(row, group):
    w = jnp.full(row.shape, POOL_WINDOWS[-1], jnp.int32)
    for gi in range(len(POOL_WINDOWS) - 2, -1, -1):
        w = jnp.where(group == gi, POOL_WINDOWS[gi], w)
    return jnp.minimum(row + 1, w).astype(F32)


def _mix_local_fwd(rest, wbd, ps, cw):
    B, S, C4 = rest.shape
    C = C4 // 4
    gw = C // len(POOL_WINDOWS)

    def kern(r_ref, w_ref, ps_ref, cw_ref, y_ref, pooled_ref):
        row = lax.broadcasted_iota(jnp.int32, (S, C), 0)
        group = lax.broadcasted_iota(jnp.int32, (S, C), 1) // gw
        u = r_ref[:, 0:C]
        pooled = _window(u, lambda v, d: _down(v, d, row), group) / _pool_count(row, group) - u
        pb = pooled.astype(BF16)
        pooled_ref[...] = pb
        y_ref[:, 0:C] = (_dot(pb, w_ref[...], NN) * ps_ref[...]).astype(BF16)
        uc = r_ref[:, 2 * C:3 * C] * r_ref[:, 3 * C:4 * C]
        y = cw_ref[0:1, :] * _down(uc, 2, row) + cw_ref[1:2, :] * _down(uc, 1, row) + cw_ref[2:3, :] * uc
        y_ref[:, C:2 * C] = (r_ref[:, C:2 * C] * y).astype(BF16)

    return pl.pallas_call(
        kern, name="mix_local_fwd", grid=(B,),
        in_specs=[pl.BlockSpec((None, S, C4), lambda b: (b, 0, 0)), pl.BlockSpec((C, C), lambda b: (0, 0)),
                  pl.BlockSpec((1, C), lambda b: (0, 0)), pl.BlockSpec((8, C), lambda b: (0, 0))],
        out_specs=[pl.BlockSpec((None, S, 2 * C), lambda b: (b, 0, 0)), pl.BlockSpec((None, S, C), lambda b: (b, 0, 0))],
        out_shape=[jax.ShapeDtypeStruct((B, S, 2 * C), BF16), jax.ShapeDtypeStruct((B, S, C), BF16)],
        compiler_params=_params(1),
    )(rest, wbd, ps, cw)


def _mix_local_bwd(rest, pooled, dycat, wbd, ps, cw):
    B, S, C4 = rest.shape
    C = C4 // 4
    gw = C // len(POOL_WINDOWS)

    def kern(r_ref, pooled_ref, d_ref, w_ref, ps_ref, cw_ref, dr_ref, dw_ref, dps_ref, dcw_ref):
        row = lax.broadcasted_iota(jnp.int32, (S, C), 0)
        group = lax.broadcasted_iota(jnp.int32, (S, C), 1) // gw
        dyp = d_ref[:, 0:C]
        dyc = d_ref[:, C:2 * C]
        pb = pooled_ref[...]
        dps = jnp.sum(dyp * _dot(pb, w_ref[...], NN), axis=0, keepdims=True)
        dzb = (dyp * ps_ref[...]).astype(BF16)
        dw = _dot(pb, dzb, TN)
        dpooled = _dot(dzb, w_ref[...], NT)
        g = dpooled / _pool_count(row, group)
        dr_ref[:, 0:C] = (_window(g, lambda v, d: _up(v, d, row, S), group) - dpooled).astype(BF16)
        cc, ch = r_ref[:, 2 * C:3 * C], r_ref[:, 3 * C:4 * C]
        uc = cc * ch
        u1, u2 = _down(uc, 1, row), _down(uc, 2, row)
        y = cw_ref[0:1, :] * u2 + cw_ref[1:2, :] * u1 + cw_ref[2:3, :] * uc
        dr_ref[:, C:2 * C] = (dyc * y).astype(BF16)
        dy = dyc * r_ref[:, C:2 * C]
        duc = cw_ref[0:1, :] * _up(dy, 2, row, S) + cw_ref[1:2, :] * _up(dy, 1, row, S) + cw_ref[2:3, :] * dy
        dr_ref[:, 2 * C:3 * C] = (duc * ch).astype(BF16)
        dr_ref[:, 3 * C:4 * C] = (duc * cc).astype(BF16)
        dcw = jnp.concatenate([jnp.sum(dy * u2, axis=0, keepdims=True), jnp.sum(dy * u1, axis=0, keepdims=True),
                               jnp.sum(dy * uc, axis=0, keepdims=True), jnp.zeros((5, C), F32)], axis=0)

        @pl.when(pl.program_id(0) == 0)
        def _():
            dw_ref[...] = dw
            dps_ref[...] = dps
            dcw_ref[...] = dcw

        @pl.when(pl.program_id(0) > 0)
        def _():
            dw_ref[...] += dw
            dps_ref[...] += dps
            dcw_ref[...] += dcw

    full = lambda shape: pl.BlockSpec(shape, lambda b: (0, 0))
    return pl.pallas_call(
        kern, name="mix_local_bwd", grid=(B,),
        in_specs=[pl.BlockSpec((None, S, C4), lambda b: (b, 0, 0)), pl.BlockSpec((None, S, C), lambda b: (b, 0, 0)),
                  pl.BlockSpec((None, S, 2 * C), lambda b: (b, 0, 1)), full((C, C)), full((1, C)), full((8, C))],
        out_specs=[pl.BlockSpec((None, S, C4), lambda b: (b, 0, 0)), full((C, C)), full((1, C)), full((8, C))],
        out_shape=[jax.ShapeDtypeStruct((B, S, C4), BF16), jax.ShapeDtypeStruct((C, C), F32),
                   jax.ShapeDtypeStruct((1, C), F32), jax.ShapeDtypeStruct((8, C), F32)],
        compiler_params=_params(1),
    )(rest, pooled, dycat, wbd, ps, cw)


def _adamw(w, gs, m, v, token=None):
    R, C = w.shape
    pieces = [p if isinstance(p, tuple) else (p,) for p in gs]
    owner = [s for s, p in enumerate(pieces) for _ in p]
    flat = [a for p in pieces for a in p]
    n = len(flat)
    rows = R // len(pieces)
    tr = _tile(rows, 256)
    per = rows // tr
    tok_ops, tok_specs = _token_operand(token)

    def kern(w_ref, *refs):
        g_refs, (m_ref, v_ref), (g_out, d_ref, nm_ref, nv_ref) = refs[:n], refs[n:n + 2], refs[n + 2 + len(tok_ops):]
        vals, at = [], 0
        for p in pieces:
            vals.append(g_refs[at][...] if len(p) == 1 else g_refs[at][...] + g_refs[at + 1][...])
            at += len(p)
        gv = vals[0]
        for s in range(1, len(pieces)):
            gv = jnp.where(pl.program_id(0) // per == s, vals[s], gv)
        nm = ADAM_B1 * m_ref[...] + (1.0 - ADAM_B1) * gv
        nv = ADAM_B2 * v_ref[...] + (1.0 - ADAM_B2) * (gv * gv)
        m_hat = nm / (1.0 - ADAM_B1 ** ADAM_STEP)
        v_hat = nv / (1.0 - ADAM_B2 ** ADAM_STEP)
        g_out[...] = gv
        d_ref[...] = -ADAM_LR * (m_hat / (jnp.sqrt(v_hat) + ADAM_EPS) + ADAM_WD * w_ref[...])
        nm_ref[...] = nm
        nv_ref[...] = nv

    def piece(s):
        return pl.BlockSpec((tr, C), lambda i: (jnp.clip(i - s * per, 0, per - 1), 0))

    blk = pl.BlockSpec((tr, C), lambda i: (i, 0))
    return pl.pallas_call(
        kern, name="adamw", grid=(R // tr,), in_specs=[blk] + [piece(s) for s in owner] + [blk] * 2 + tok_specs,
        out_specs=[blk] * 4, out_shape=[jax.ShapeDtypeStruct((R, C), F32)] * 4, compiler_params=_params(1),
    )(w, *flat, m, v, *tok_ops)


def _place():
    x, y, c = lax.axis_index("x"), lax.axis_index("y"), lax.axis_index("c")
    return x, y, c, [(1 - x, y), (x, 1 - y), (1 - x, 1 - y)]


def _comm_call(name, body, operands, out_shape, n_sems, aliases=None):
    any_spec = pl.BlockSpec(memory_space=pl.ANY)
    return pl.pallas_call(
        body, name=name, in_specs=[any_spec] * len(operands), out_specs=[any_spec] * len(out_shape),
        out_shape=out_shape, input_output_aliases=aliases or {},
        scratch_shapes=[pltpu.SemaphoreType.DMA((n,)) for n in n_sems],
    )(*operands)


def _my_block():
    return 2 * lax.axis_index("x") + lax.axis_index("y")


def _place_shard(w, dtype, first=0, count=None):
    L, R, C = w.shape
    count = L if count is None else count
    tr = _tile(R, 512)

    def kern(w_ref, o_ref):
        o_ref[...] = w_ref[...].astype(dtype)

    return pl.pallas_call(
        kern, name="place_shard", grid=(count, R // tr),
        in_specs=[pl.BlockSpec((None, tr, C), lambda l, i: (first + l, i, 0))],
        out_specs=pl.BlockSpec((None, None, tr, C), lambda l, i: (l, _my_block(), i, 0)),
        out_shape=jax.ShapeDtypeStruct((count, N_CHIPS, R, C), dtype), compiler_params=_params(2),
    )(w)


HALF_ROWS = 16


def _rows(ref, half):
    hr = ref.shape[-2] // 2
    return ref.at[(slice(None),) * (len(ref.shape) - 2) + (pl.ds(half * hr, hr),)]


def _all_gather(bufs):
    n = len(bufs)

    def body(*refs):
        outs = refs[n:2 * n]
        send_sems, recv_sems = refs[2 * n:]
        x, y, c, chips = _place()
        sibling = (x, y, 1 - c)

        def remote(k, j, chip, half, to):
            blk = 2 * chip[0] + chip[1]
            if outs[k].shape[2] % (2 * HALF_ROWS) == 0:
                region = _rows(outs[k].at[:, blk], half)
            else:
                hl = outs[k].shape[0] // 2
                region = outs[k].at[pl.ds(half * hl, hl), blk]
            return pltpu.make_async_remote_copy(
                src_ref=region, dst_ref=region, send_sem=send_sems.at[6 * k + j],
                recv_sem=recv_sems.at[6 * k + j], device_id=to, device_id_type=MESH)

        first = [remote(k, j, (x, y), c, (*chip, c)) for k in range(n) for j, chip in enumerate(chips)]
        for cp in first:
            cp.start()
        passed = []
        for k in range(n):
            for j, chip in enumerate(chips):
                remote(k, j, chip, c, (x, y, c)).wait_recv()
                passed.append(remote(k, 3 + j, chip, c, sibling))
                passed[-1].start()
        for k in range(n):
            for j, chip in enumerate(chips):
                remote(k, 3 + j, chip, 1 - c, (x, y, c)).wait_recv()
        for cp in first + passed:
            cp.wait_send()

    out_shape = [jax.ShapeDtypeStruct(s.shape, s.dtype) for s in bufs]
    return _comm_call("all_gather_weights", body, bufs, out_shape, (6 * n, 6 * n), aliases={k: k for k in range(n)})


_HBM = pl.BlockSpec(memory_space=pltpu.HBM)
_SEM = pl.BlockSpec(memory_space=pltpu.SEMAPHORE)
_ANY = pl.BlockSpec(memory_space=pl.ANY)


def _split_start(name, bufs, n_copies, make_copies, after):
    n = len(bufs)

    def body(*refs):
        send_sems, recv_sems, token = refs[n + 1], refs[n + 2], refs[2 * n + 3]
        for cp in make_copies(refs[:n], send_sems, recv_sems):
            cp.start()
        token[...] = jnp.zeros_like(token)

    res = pl.pallas_call(
        body, name=name, in_specs=[_HBM] * n + [_ANY],
        out_shape=(pltpu.SemaphoreType.DMA((n_copies,)), pltpu.SemaphoreType.DMA((n_copies,)),
                   *[pltpu.HBM(b.shape, b.dtype) for b in bufs], jax.ShapeDtypeStruct((8, LANES), F32)),
        out_specs=(_SEM, _SEM, *[_HBM] * n, pl.BlockSpec(memory_space=pltpu.VMEM)),
        input_output_aliases={i: 2 + i for i in range(n)},
        compiler_params=pltpu.CompilerParams(has_side_effects=pltpu.SideEffectType.DATAFLOW_SIDE_EFFECTING),
    )(*[pltpu.with_memory_space_constraint(b, pltpu.HBM) for b in bufs], after)
    return res[0], res[1], list(res[2:2 + n]), res[2 + n]


def _split_wait(name, send_sems, recv_sems, bufs, make_copies, after):
    n = len(bufs)

    def body(*refs):
        for cp in make_copies(refs[:n], refs[n], refs[n + 1]):
            cp.wait_send()
            cp.wait_recv()

    return list(pl.pallas_call(
        body, name=name, in_specs=[_HBM] * n + [_SEM, _SEM, _ANY],
        out_shape=tuple(pltpu.HBM(b.shape, b.dtype) for b in bufs), out_specs=tuple([_HBM] * n),
        input_output_aliases={i: i for i in range(n)},
        compiler_params=pltpu.CompilerParams(has_side_effects=pltpu.SideEffectType.DATAFLOW_SIDE_EFFECTING),
    )(*bufs, send_sems, recv_sems, after))


def _gather_copies(refs, send_sems, recv_sems):
    x, y, c, chips = _place()
    return [pltpu.make_async_remote_copy(
        src_ref=ref.at[:, 2 * x + y], dst_ref=ref.at[:, 2 * x + y], send_sem=send_sems.at[3 * k + j],
        recv_sem=recv_sems.at[3 * k + j], device_id=(*chip, c), device_id_type=MESH)
        for k, ref in enumerate(refs) for j, chip in enumerate(chips)]


def _exchange_copies(refs, send_sems, recv_sems):
    n = len(refs) // 2
    x, y, c, chips = _place()
    return [pltpu.make_async_remote_copy(
        src_ref=refs[k].at[:, 2 * chip[0] + chip[1]], dst_ref=refs[n + k].at[j], send_sem=send_sems.at[3 * k + j],
        recv_sem=recv_sems.at[3 * k + j], device_id=(*chip, c), device_id_type=MESH)
        for k in range(n) for j, chip in enumerate(chips)]


def _all_reduce_small(v, after):
    n = v.shape[0]

    def body(v_ref, after_ref, o_ref, gbuf, send_sems, recv_sems):
        x, y, c, _ = _place()
        me = 4 * x + 2 * y + c
        gbuf[me] = v_ref[...]
        copies, waits = [], []
        for r in range(1, N_DEV):
            px = 1 - x if r & 4 else x
            py = 1 - y if r & 2 else y
            pc = 1 - c if r & 1 else c
            mk = functools.partial(pltpu.make_async_remote_copy, src_ref=v_ref, send_sem=send_sems.at[r - 1],
                                   recv_sem=recv_sems.at[r - 1], device_id=(px, py, pc), device_id_type=MESH)
            copies.append(mk(dst_ref=gbuf.at[me]))
            waits.append(mk(dst_ref=gbuf.at[4 * px + 2 * py + pc]))
        for cp in copies:
            cp.start()
        for cp in waits:
            cp.wait_recv()
        for cp in copies:
            cp.wait_send()
        acc = gbuf[0]
        for d in range(1, N_DEV):
            acc = acc + gbuf[d]
        o_ref[...] = acc

    vm = pl.BlockSpec(memory_space=pltpu.VMEM)
    return pl.pallas_call(
        body, name="all_reduce_small", in_specs=[vm, _ANY], out_specs=vm, out_shape=jax.ShapeDtypeStruct(v.shape, F32),
        scratch_shapes=[pltpu.VMEM((N_DEV, n, LANES), F32), pltpu.SemaphoreType.DMA((N_DEV - 1,)),
                        pltpu.SemaphoreType.DMA((N_DEV - 1,))],
        compiler_params=pltpu.CompilerParams(vmem_limit_bytes=VMEM_LIMIT),
    )(v, after)


def _add_blocks(p, h2):
    L, nb, hr, C = p.shape
    tr = _tile(hr, 512)

    def kern(p_ref, h0_ref, h1_ref, h2_ref, o_ref):
        o_ref[...] = ((p_ref[...].astype(F32) + h0_ref[...].astype(F32)) + h1_ref[...].astype(F32)) + h2_ref[...].astype(F32)

    def other(j):
        return pl.BlockSpec((None, None, tr, C), lambda l, i: (j, l, i, 0))

    return pl.pallas_call(
        kern, name="rs_add_blocks", grid=(L, hr // tr),
        in_specs=[pl.BlockSpec((None, None, tr, C), lambda l, i: (l, _my_block(), i, 0)), other(0), other(1), other(2)],
        out_specs=pl.BlockSpec((None, tr, C), lambda l, i: (l, i, 0)),
        out_shape=jax.ShapeDtypeStruct((L, hr, C), F32), compiler_params=_params(2),
    )(p, h2, h2, h2)


WEIGHTS = ("norm_ffn1", "w_ffn1_in", "w_ffn1_out", "norm_mix", "w_mix_in", "b_forget", "w_pool", "pool_scale",
           "conv_w", "w_mix_out", "norm_ffn2", "w_ffn2_in", "w_ffn2_out", "norm_final")
BIG = ("w_ffn1_in", "w_ffn1_out", "w_mix_in", "w_mix_out", "w_ffn2_in", "w_ffn2_out")
SMALL = ("norm_ffn1", "norm_mix", "b_forget", "w_pool", "pool_scale", "conv_w", "norm_ffn2", "norm_final")


def _layer_params(small, gathered, conv_w, D, l):
    DA, C, H = D // 2, D // 4, D // 2 // HEAD_DIM
    P = {}
    if "w_ffn1_in" in gathered:
        P.update(g1=small["norm_ffn1"][l][None], w1in=(gathered["w_ffn1_in"], 0),
                 w1out=(gathered["w_ffn1_out"].reshape(1, -1, D), 0))
    if "w_ffn2_in" in gathered:
        P.update(g2=small["norm_ffn2"][l][None], w2in=(gathered["w_ffn2_in"], 0),
                 w2out=(gathered["w_ffn2_out"].reshape(1, -1, D), 0))
    if "w_mix_in" in gathered:
        w_in = jnp.concatenate([gathered["w_mix_in"][:, b] for b in range(N_CHIPS)], axis=2)
        wqkv, wrest = w_in[:, :, :3 * DA], w_in[:, :, 3 * DA + H:]
        wf = jnp.pad(w_in[:, :, 3 * DA:3 * DA + H], ((0, 0), (0, 0), (0, LANES - H)))
        ng = len(POOL_WINDOWS)
        same_group = jnp.eye(ng, dtype=bool)[:, None, :, None]
        wbd = jnp.where(same_group, small["w_pool"][l][:, :, None, :], 0.0).reshape(C, C)
        cw = jnp.concatenate([conv_w[l, b] for b in range(N_CHIPS)], axis=1)
        P.update(gm=small["norm_mix"][l][None], wp=(jnp.concatenate([wqkv, wrest, wf], axis=2), 0),
                 wmixout=(gathered["w_mix_out"].reshape(1, D, D), 0),
                 bias=jnp.pad(small["b_forget"][l][None], ((0, 0), (0, LANES - H))), wbd=wbd.astype(BF16),
                 ps=small["pool_scale"][l][None], cw=jnp.pad(cw, ((0, 8 - CONV_WIDTH), (0, 0))))
    return P


def _ffn_fwd(x, g, w_in, w_out, token=None):
    h, jac, act = _ffn_up(x, g, w_in, token)
    return _ffn_out(act, w_out, x)[0], (x, h, jac, act)


def _ffn_bwd(dres, saved, g, w_in, w_out, token=None):
    x, h, jac, act = saved
    dgu, dx, dg = _ffn_bwd_main(dres, jac, x, g, w_out, w_in, token)
    dw_out = _ffn_dw_out(act, dres)[0]
    dw_in = _ffn_dw_in(h, dgu)[0]
    return dx, dg, dw_in, dw_out.reshape(N_CHIPS, -1, dw_out.shape[1])


def _mixer_fwd(x, P, B, S, tq):
    T, D = x.shape
    DA, C, H = D // 2, D // 4, D // 2 // HEAD_DIM
    hn, qkv, rest, fl = _mix_up(x, P["gm"], P["wp"], (3 * DA, 4 * C))
    qkv, rest, fl = qkv.reshape(B, S, 3 * DA), rest.reshape(B, S, 4 * C), fl.reshape(B, S, LANES)
    drow = _decay_fwd(fl, P["bias"]).reshape(B, 8, S // tq, tq)
    o, lse = _attn_fwd(qkv, drow, H, tq)
    ypc, pooled = _mix_local_fwd(rest, P["wbd"], P["ps"], P["cw"])
    x_out = _mix_out([o.reshape(T, DA), ypc.reshape(T, 2 * C)], P["wmixout"], x)
    return x_out, (x, hn, qkv, rest, fl, drow, o, lse, pooled, ypc)


def _mixer_bwd(dres, saved, P, B, S, tq):
    x, hn, qkv, rest, fl, drow, o, lse, pooled, ypc = saved
    T, D = x.shape
    DA, C, H = D // 2, D // 4, D // 2 // HEAD_DIM
    dycat = _proj("mix_out_bwd", dres, P["wmixout"], F32, NT).reshape(B, S, D)
    dw_out = _rows_dw("mix_out_dw", [o.reshape(T, DA), ypc.reshape(T, 2 * C)], dres, BF16)
    dq, dk, dv, ddrow, ddcol = _attn_bwd(qkv, drow, o, lse, dycat, H, tq)
    dfl, dbias = _decay_bwd(ddrow.reshape(B, 8, S), ddcol, fl, P["bias"], H)
    drest, dwbd, dps, dcw = _mix_local_bwd(rest, pooled, dycat, P["wbd"], P["ps"], P["cw"])
    pieces = [a.reshape(T, a.shape[-1]) for a in (dq, dk, dv, drest, dfl)]
    dwp = _pieces_dw("mix_in_dw", hn, pieces, F32, 512)
    dx, dg = _mix_in_bwd(pieces, x, P["gm"], dres, P["wp"])
    n_q, n_r = 3 * DA, 4 * C
    dw_in = jnp.concatenate([dwp[:, :n_q], dwp[:, n_q + n_r:n_q + n_r + H], dwp[:, n_q:n_q + n_r]], axis=1)
    dw_in = dw_in.reshape(D, N_CHIPS, -1).transpose(1, 0, 2).astype(BF16)
    ng = len(POOL_WINDOWS)
    same_group = jnp.eye(ng, dtype=bool)[:, None, :, None]
    dw_pool = jnp.where(same_group, dwbd.reshape(ng, C // ng, ng, C // ng), 0.0).sum(axis=2)
    small = dict(norm_mix=dg[0], b_forget=dbias[0, :H], w_pool=dw_pool, pool_scale=dps[0], conv_w=dcw[:CONV_WIDTH])
    return dx, small, dw_in, dw_out.reshape(N_CHIPS, -1, D)


FFN1, MIX, FFN2 = BIG[:2], BIG[2:4], BIG[4:]


def _local_step(x, target, small, conv_w, pipe):
    B, S, D = x.shape
    L = small["norm_ffn1"].shape[0]
    tq = _tile(S, 512)
    xt = x.reshape(B * S, D)
    saved, params = [], []
    for l in range(L):
        P = _layer_params(small, pipe.weights(l, xt), conv_w, D, l)
        xt, s1 = _ffn_fwd(xt, P["g1"], P["w1in"], P["w1out"], pipe.token(l))
        P.update(_layer_params(small, pipe.weights_mix(l, xt), conv_w, D, l))
        xt, s2 = _mixer_fwd(xt, P, B, S, tq)
        P.update(_layer_params(small, pipe.weights_ffn2(l, xt), conv_w, D, l))
        xt, s3 = _ffn_fwd(xt, P["g2"], P["w2in"], P["w2out"])
        saved.append((s1, s2, s3))
        params.append(P)
    dres, dgf, loss = _final_loss(xt, small["norm_final"][None], target.reshape(B * S, D))
    sm = {k: [None] * L for k in SMALL if k != "norm_final"}
    token = None
    for l in reversed(range(L)):
        P, (s1, s2, s3) = params[l], saved[l]
        big = {}
        dres, dg2, big["w_ffn2_in"], big["w_ffn2_out"] = _ffn_bwd(dres, s3, P["g2"], P["w2in"], P["w2out"], token)
        dres, smix, big["w_mix_in"], big["w_mix_out"] = _mixer_bwd(dres, s2, P, B, S, tq)
        big = {k: val[None] for k, val in big.items()}
        token = pipe.grads(l, FFN2 + MIX, big, dres) if l == 0 else None
        dres, dg1, dw_in, dw_out = _ffn_bwd(dres, s1, P["g1"], P["w1in"], P["w1out"], token)
        big.update(w_ffn1_in=dw_in[None], w_ffn1_out=dw_out[None])
        sm["norm_ffn1"][l], sm["norm_ffn2"][l] = dg1[0], dg2[0]
        for k, val in smix.items():
            sm[k][l] = val
        token = pipe.grads(l, FFN1 if l == 0 else BIG, big, big["w_ffn1_in"])
    sm = {k: jnp.stack(val) for k, val in sm.items()}
    sm["norm_final"] = dgf[0]
    return loss[0, 0], dres.reshape(B, S, D), sm


def _sibling_copies(refs, send_sems, recv_sems):
    n = len(refs) // 2
    x, y, c, _ = _place()
    return [pltpu.make_async_remote_copy(
        src_ref=refs[k], dst_ref=refs[n + k], send_sem=send_sems.at[k], recv_sem=recv_sems.at[k],
        device_id=(x, y, 1 - c), device_id_type=MESH) for k in range(n)]


class _Pipeline:
    def __init__(self, w):
        self.w, self.n_layers = w, w[BIG[0]].shape[0]
        first = _all_gather([_place_shard(w[k], BF16, 0, 1) for k in FFN1] + [_place_shard(w["conv_w"], F32)])
        self.conv_w = first[-1]
        self._ready = dict(zip(FFN1, first[:-1]))
        self._mix = self._start_gather("0_mix", MIX, 0, first[0])
        self._ffn2 = self._start_gather("0_ffn2", FFN2, 0, self._mix[1][3])
        self._next = (1, self._start_gather("1", BIG, 1, self._ffn2[1][3]))
        self._reduce, self._swaps = None, []
        self.reduced = [dict() for _ in range(self.n_layers)]

    def _start_gather(self, tag, kinds, l, after):
        placed = [_place_shard(self.w[k], BF16, l, 1) for k in kinds]
        return kinds, _split_start(f"gather_start_{tag}", placed, 3 * len(kinds), _gather_copies, after)

    def _wait_gather(self, tag, started, after):
        kinds, (send_sems, recv_sems, bufs, _) = started
        return dict(zip(kinds, _split_wait(f"gather_wait_{tag}", send_sems, recv_sems, bufs, _gather_copies, after)))

    def token(self, l):
        return self._next[1][1][3] if self._next is not None and self._next[0] == l + 1 else None

    def weights(self, l, after):
        if l == 0:
            return self._ready
        got = self._wait_gather(str(l), self._next[1], after)
        self._next = (l + 1, self._start_gather(str(l + 1), BIG, l + 1, got[BIG[0]])) if l + 1 < self.n_layers else None
        return got

    def weights_mix(self, l, after):
        return self._wait_gather("0_mix", self._mix, after) if l == 0 else {}

    def weights_ffn2(self, l, after):
        return self._wait_gather("0_ffn2", self._ffn2, after) if l == 0 else {}

    def _finish_reduce(self, after):
        if self._reduce is None:
            return None
        tag, l, kinds, (send_sems, recv_sems, bufs, _) = self._reduce
        n = len(kinds)
        bufs = _split_wait(f"reduce_wait_{tag}", send_sems, recv_sems, bufs, _exchange_copies, after)
        mine = [_add_blocks(p, o) for p, o in zip(bufs[:n], bufs[n:])]
        lands = [lax.empty(q.shape, q.dtype) for q in mine]
        self._swaps.append((tag, l, kinds, _split_start(f"swap_start_{tag}", mine + lands, n, _sibling_copies, mine[0])))
        self._reduce = None
        return self._swaps[-1][3][3]

    def grads(self, l, kinds, big, after):
        swap_token = self._finish_reduce(after)
        grads = [big[k] for k in kinds]
        tag = str(l) if len(kinds) == len(BIG) else f"{l}_{kinds[0][2:]}"
        lands = [lax.empty((3, g.shape[0]) + g.shape[2:], g.dtype) for g in grads]
        started = _split_start(f"reduce_start_{tag}", grads + lands, 3 * len(kinds), _exchange_copies,
                               grads[0] if swap_token is None else swap_token)
        self._reduce = (tag, l, kinds, started)
        self.last_token = started[3]
        return started[3]

    def finish(self, after, last=False):
        if last:
            self._finish_reduce(after)
        for tag, l, kinds, (send_sems, recv_sems, bufs, _) in self._swaps:
            n = len(kinds)
            bufs = _split_wait(f"swap_wait_{tag}", send_sems, recv_sems, bufs, _sibling_copies, after)
            self.reduced[l].update(zip(kinds, zip(bufs[:n], bufs[n:])))
        self._swaps = []
        return self.reduced


def _pack(parts, extra=()):
    flat = jnp.concatenate([p.reshape(-1) for p in parts] + [jnp.reshape(e, (1,)) for e in extra])
    n = -(-flat.shape[0] // (8 * LANES)) * 8
    return jnp.pad(flat, (0, n * LANES - flat.shape[0])).reshape(n, LANES)


def _unpack(buf, shapes):
    flat, out, at = buf.reshape(-1), [], 0
    for s in shapes:
        n = math.prod(s)
        out.append(flat[at:at + n].reshape(s))
        at += n
    return out, flat[at:]


def kernel(x, norm_ffn1, w_ffn1_in, w_ffn1_out, norm_mix, w_mix_in, b_forget, w_pool, pool_scale, conv_w, w_mix_out, norm_ffn2, w_ffn2_in, w_ffn2_out, norm_final, loss_target, m_norm_ffn1, m_w_ffn1_in, m_w_ffn1_out, m_norm_mix, m_w_mix_in, m_b_forget, m_w_pool, m_pool_scale, m_conv_w, m_w_mix_out, m_norm_ffn2, m_w_ffn2_in, m_w_ffn2_out, m_norm_final, v_norm_ffn1, v_w_ffn1_in, v_w_ffn1_out, v_norm_mix, v_w_mix_in, v_b_forget, v_w_pool, v_pool_scale, v_conv_w, v_w_mix_out, v_norm_ffn2, v_w_ffn2_in, v_w_ffn2_out, v_norm_final):
    w = dict(zip(WEIGHTS, (norm_ffn1, w_ffn1_in, w_ffn1_out, norm_mix, w_mix_in, b_forget, w_pool, pool_scale, conv_w, w_mix_out, norm_ffn2, w_ffn2_in, w_ffn2_out, norm_final)))
    m = dict(zip(WEIGHTS, (m_norm_ffn1, m_w_ffn1_in, m_w_ffn1_out, m_norm_mix, m_w_mix_in, m_b_forget, m_w_pool, m_pool_scale, m_conv_w, m_w_mix_out, m_norm_ffn2, m_w_ffn2_in, m_w_ffn2_out, m_norm_final)))
    v = dict(zip(WEIGHTS, (v_norm_ffn1, v_w_ffn1_in, v_w_ffn1_out, v_norm_mix, v_w_mix_in, v_b_forget, v_w_pool, v_pool_scale, v_conv_w, v_w_mix_out, v_norm_ffn2, v_w_ffn2_in, v_w_ffn2_out, v_norm_final)))
    block = 2 * lax.axis_index("x") + lax.axis_index("y")

    pipe = _Pipeline(w)
    small = {k: w[k] for k in SMALL}
    loss, grad_x, sm = _local_step(x, loss_target, small, pipe.conv_w, pipe)
    grads, delta, new_m, new_v = {}, {}, {}, {}

    def big_adamw(k, reduced, token=None):
        two_d = lambda a: a.reshape(-1, a.shape[-1])
        pieces = [tuple(map(two_d, layer[k])) for layer in reduced]
        res = _adamw(two_d(w[k]), pieces, two_d(m[k]), two_d(v[k]), token)
        grads[k], delta[k], new_m[k], new_v[k] = [r.reshape(w[k].shape) for r in res]

    reduced = pipe.finish(grad_x)
    for k in BIG[2:]:
        big_adamw(k, reduced, pipe.last_token)
    reduced = pipe.finish(new_v[BIG[-1]], last=True)
    for k in BIG[:2]:
        big_adamw(k, reduced)

    order = list(SMALL)
    total = _all_reduce_small(_pack([sm[k] for k in order], extra=(loss,)), new_v[BIG[0]])
    parts, rest = _unpack(total, [sm[k].shape for k in order])
    grads.update(zip(order, parts))
    loss = rest[0]
    cs = conv_w.shape[2]
    grads["conv_w"] = lax.dynamic_slice_in_dim(grads["conv_w"], block * cs, cs, axis=2)
    packed = [_pack([t[k] for k in order]) for t in (w, grads, m, v)]
    _, d, nm, nv = _adamw(packed[0], [packed[1]], packed[2], packed[3])
    shapes = [w[k].shape for k in order]
    for res, flat in ((delta, d), (new_m, nm), (new_v, nv)):
        res.update(zip(order, _unpack(flat, shapes)[0]))
    return (loss, grad_x, *[grads[k] for k in WEIGHTS], *[delta[k] for k in WEIGHTS],
            *[new_m[k] for k in WEIGHTS], *[new_v[k] for k in WEIGHTS])
```

```python
import functools
import math

import jax
import jax.numpy as jnp
from jax import lax
from jax.experimental import pallas as pl
from jax.experimental.pallas import tpu as pltpu

F32 = jnp.float32
BF16 = jnp.bfloat16
MESH = pl.DeviceIdType.MESH

HEAD_DIM = 64
POOL_WINDOWS = (2, 4, 8, 16)
CONV_WIDTH = 3
RMS_EPS = 1e-6
ADAM_LR = 0.001
ADAM_B1 = 0.9
ADAM_B2 = 0.999
ADAM_EPS = 1e-08
ADAM_WD = 0.01
ADAM_STEP = 10

LANES = 128
VMEM_LIMIT = 56 * 1024 * 1024
N_CHIPS = 4
N_DEV = 8

NN = (((1,), (0,)), ((), ()))
NT = (((1,), (1,)), ((), ()))
TN = (((0,), (0,)), ((), ()))


def _tile(n, pref):
    for t in range(pref - pref % 16, 15, -16):
        if n % t == 0:
            return t
    return n


def _params(n_grid):
    return pltpu.CompilerParams(dimension_semantics=("arbitrary",) * n_grid, vmem_limit_bytes=VMEM_LIMIT)


def _dot(a, b, dims):
    return lax.dot_general(a, b, dims, preferred_element_type=F32)


def _mm(name, dims, operands, in_specs, out_shape, out_specs, grid, acc_shape, epilogue):
    n_in, n_out, nk = len(operands), len(out_shape), grid[-1]

    def kern(*refs):
        extras, outs = refs[2:n_in], refs[n_in:n_in + n_out]
        if nk == 1:
            epilogue(_dot(refs[0][...].astype(BF16), refs[1][...].astype(BF16), dims), extras, outs)
            return
        acc = refs[n_in + n_out]
        k = pl.program_id(len(grid) - 1)

        @pl.when(k == 0)
        def _():
            acc[...] = jnp.zeros_like(acc)

        acc[...] += _dot(refs[0][...].astype(BF16), refs[1][...].astype(BF16), dims)

        @pl.when(k == nk - 1)
        def _():
            epilogue(acc[...], extras, outs)

    return pl.pallas_call(
        kern, name=name, grid=grid, in_specs=in_specs, out_specs=out_specs, out_shape=out_shape,
        scratch_shapes=[pltpu.VMEM(acc_shape, F32)] if nk > 1 else [],
        compiler_params=_params(len(grid)),
    )(*operands)


def _store(scale=None, dtype=None):
    def ep(acc, extras, outs):
        v = acc if scale is None else acc * scale
        outs[0][...] = v.astype(outs[0].dtype)
    return ep


def _residual(scale):
    def ep(acc, extras, outs):
        outs[0][...] = extras[0][...] + scale * acc
    return ep


def _final_loss(x, g, target):
    T, D = x.shape
    tr = _tile(T, 256)

    def kern(x_ref, g_ref, t_ref, dx_ref, dg_ref, loss_ref):
        xv = x_ref[...]
        r = lax.rsqrt(jnp.mean(xv * xv, axis=-1, keepdims=True) + RMS_EPS)
        y = xv * r
        err = y * g_ref[...] - t_ref[...]
        lpart = 0.5 * jnp.sum(jnp.mean(err * err, axis=-1, keepdims=True), axis=0, keepdims=True)
        dh = err * (1.0 / D)
        dy = dh * g_ref[...]
        dx_ref[...] = r * (dy - y * jnp.mean(dy * y, axis=-1, keepdims=True))
        part = jnp.sum(dh * y, axis=0, keepdims=True)
        lrow = jnp.broadcast_to(lpart, (1, LANES))

        @pl.when(pl.program_id(0) == 0)
        def _():
            dg_ref[...] = part
            loss_ref[...] = lrow

        @pl.when(pl.program_id(0) > 0)
        def _():
            dg_ref[...] += part
            loss_ref[...] += lrow

    row = pl.BlockSpec((tr, D), lambda i: (i, 0))
    vec = pl.BlockSpec((1, D), lambda i: (0, 0))
    return pl.pallas_call(
        kern, name="final_loss", grid=(T // tr,), in_specs=[row, vec, row],
        out_specs=[row, vec, pl.BlockSpec((1, LANES), lambda i: (0, 0))],
        out_shape=[jax.ShapeDtypeStruct((T, D), F32), jax.ShapeDtypeStruct((1, D), F32),
                   jax.ShapeDtypeStruct((1, LANES), F32)],
        compiler_params=_params(1),
    )(x, g, target)


def _resident(shape, index_map):
    return pl.BlockSpec(shape, index_map, pipeline_mode=pl.Buffered(1))


def _token_operand(token):
    return ([], []) if token is None else ([token], [pl.BlockSpec(token.shape, lambda i: (0, 0))])


def _ffn_up(x, g, w4, token=None):
    T, D = x.shape
    w4, l = w4
    Fh = w4.shape[3]
    F = 2 * Fh
    tm = _tile(T, 512)
    tok_ops, tok_specs = _token_operand(token)

    def kern(x_ref, g_ref, w_ref, *rest):
        h_ref, jac_ref, act_ref = rest[len(tok_ops):]
        xv = x_ref[...]
        r = lax.rsqrt(jnp.mean(xv * xv, axis=-1, keepdims=True) + RMS_EPS)
        hv = (xv * r * g_ref[...]).astype(BF16)
        h_ref[...] = hv
        for j in range(2):
            cols = slice(j * Fh, (j + 1) * Fh)
            gate = _dot(hv, w_ref[j], NN)
            up = _dot(hv, w_ref[2 + j], NN)
            sg = jax.nn.sigmoid(gate)
            silu = gate * sg
            jac_ref[0, :, cols] = (up * (sg + silu * (1.0 - sg))).astype(BF16)
            jac_ref[1, :, cols] = silu.astype(BF16)
            act_ref[:, cols] = (silu * up).astype(BF16)

    return pl.pallas_call(
        kern, name="ffn_up", grid=(T // tm,),
        in_specs=[pl.BlockSpec((tm, D), lambda i: (i, 0)), pl.BlockSpec((1, D), lambda i: (0, 0)),
                  _resident((None, 4, D, Fh), lambda i: (l, 0, 0, 0))] + tok_specs,
        out_specs=[pl.BlockSpec((tm, D), lambda i: (i, 0)), pl.BlockSpec((2, tm, F), lambda i: (0, i, 0)),
                   pl.BlockSpec((tm, F), lambda i: (i, 0))],
        out_shape=[jax.ShapeDtypeStruct((T, D), BF16), jax.ShapeDtypeStruct((2, T, F), BF16),
                   jax.ShapeDtypeStruct((T, F), BF16)],
        compiler_params=_params(1),
    )(x, g, w4, *tok_ops)


def _ffn_bwd_main(dres, jac, x, g, w_out, w4, token=None):
    T, D = dres.shape
    w_out, l = w_out
    w4, _ = w4
    F = w_out.shape[1]
    Fh = F // 2
    tm = _tile(T, 512)
    tok_ops, tok_specs = _token_operand(token)

    def kern(d_ref, jac_ref, x_ref, g_ref, wo_ref, wi_ref, *rest):
        dgu_ref, dx_ref, dg_ref = rest[len(tok_ops):]
        dv = d_ref[...]
        d16 = dv.astype(BF16)
        dh = jnp.zeros((tm, D), F32)
        for j in range(2):
            cols = slice(j * Fh, (j + 1) * Fh)
            dact = 0.5 * _dot(d16, wo_ref[cols, :], NT)
            dgate = (dact * jac_ref[0, :, cols].astype(F32)).astype(BF16)
            dup = (dact * jac_ref[1, :, cols].astype(F32)).astype(BF16)
            dgu_ref[0, :, cols] = dgate
            dgu_ref[1, :, cols] = dup
            dh = dh + _dot(dgate, wi_ref[j], NT) + _dot(dup, wi_ref[2 + j], NT)
        xv = x_ref[...]
        r = lax.rsqrt(jnp.mean(xv * xv, axis=-1, keepdims=True) + RMS_EPS)
        y = xv * r
        dy = dh * g_ref[...]
        dx_ref[...] = dv + r * (dy - y * jnp.mean(dy * y, axis=-1, keepdims=True))
        part = jnp.sum(dh * y, axis=0, keepdims=True)

        @pl.when(pl.program_id(0) == 0)
        def _():
            dg_ref[...] = part

        @pl.when(pl.program_id(0) > 0)
        def _():
            dg_ref[...] += part

    row = pl.BlockSpec((tm, D), lambda i: (i, 0))
    vec = pl.BlockSpec((1, D), lambda i: (0, 0))
    wide = pl.BlockSpec((2, tm, F), lambda i: (0, i, 0))
    return pl.pallas_call(
        kern, name="ffn_bwd_main", grid=(T // tm,),
        in_specs=[row, wide, row, vec, _resident((None, F, D), lambda i: (l, 0, 0)),
                  _resident((None, 4, D, Fh), lambda i: (l, 0, 0, 0))] + tok_specs,
        out_specs=[wide, row, vec],
        out_shape=[jax.ShapeDtypeStruct((2, T, F), BF16), jax.ShapeDtypeStruct((T, D), F32),
                   jax.ShapeDtypeStruct((1, D), F32)],
        compiler_params=_params(1),
    )(dres, jac, x, g, w_out, w4, *tok_ops)


def _ffn_out(act, w_out, x):
    T, F = act.shape
    w_out, l = w_out
    D = w_out.shape[2]
    tm = _tile(T, 512)
    return _mm("ffn_out", NN, [act, w_out, x],
               [pl.BlockSpec((tm, F), lambda i, k: (i, 0)), pl.BlockSpec((None, F, D), lambda i, k: (l, 0, 0)),
                pl.BlockSpec((tm, D), lambda i, k: (i, 0))],
               [jax.ShapeDtypeStruct((T, D), F32)], [pl.BlockSpec((tm, D), lambda i, k: (i, 0))],
               (T // tm, 1), None, _residual(0.5))


def _ffn_dw_out(act, dres):
    T, F = act.shape
    D = dres.shape[1]
    tm, tk = F // 2, _tile(T, 2048)
    return _mm("ffn_dw_out", TN, [act, dres],
               [pl.BlockSpec((tk, tm), lambda i, k: (k, i)), pl.BlockSpec((tk, D), lambda i, k: (k, 0))],
               [jax.ShapeDtypeStruct((F, D), BF16)], [pl.BlockSpec((tm, D), lambda i, k: (i, 0))],
               (2, T // tk), (tm, D), _store(0.5))


def _ffn_dw_in(h, dgu):
    T, D = h.shape
    Fh = dgu.shape[2] // 2
    tk = _tile(T, 2048)
    return _mm("ffn_dw_in", TN, [h, dgu],
               [pl.BlockSpec((tk, D), lambda j, k: (k, 0)),
                pl.BlockSpec((None, tk, Fh), lambda j, k: (j // 2, k, j % 2))],
               [jax.ShapeDtypeStruct((4, D, Fh), BF16)], [pl.BlockSpec((None, D, Fh), lambda j, k: (j, 0, 0))],
               (4, T // tk), (D, Fh), _store())


def _proj(name, a, w, out_dtype, dims=NN, extra=None, scale=None):
    T, K = a.shape
    w, l = w
    N = w.shape[2] if dims == NN else w.shape[1]
    tm = _tile(T, 512)
    ops = [a, w] + ([extra] if extra is not None else [])
    specs = [pl.BlockSpec((tm, K), lambda i, k: (i, 0)), pl.BlockSpec((None,) + w.shape[1:], lambda i, k: (l, 0, 0))]
    if extra is not None:
        specs.append(pl.BlockSpec((tm, N), lambda i, k: (i, 0)))
    ep = _residual(1.0) if extra is not None else _store(scale)
    return _mm(name, dims, ops, specs, [jax.ShapeDtypeStruct((T, N), out_dtype)],
               [pl.BlockSpec((tm, N), lambda i, k: (i, 0))], (T // tm, 1), None, ep)[0]


def _mix_up(x, g, wp, widths):
    T, D = x.shape
    wp, l = wp
    n_qkv, n_rest = widths
    NP = wp.shape[2]
    tm = _tile(T, 512)

    def kern(x_ref, g_ref, w_ref, h_ref, qkv_ref, rest_ref, fl_ref):
        xv = x_ref[...]
        r = lax.rsqrt(jnp.mean(xv * xv, axis=-1, keepdims=True) + RMS_EPS)
        hv = (xv * r * g_ref[...]).astype(BF16)
        h_ref[...] = hv
        qkv_ref[...] = _dot(hv, w_ref[:, 0:n_qkv], NN).astype(BF16)
        rest_ref[...] = _dot(hv, w_ref[:, n_qkv:n_qkv + n_rest], NN)
        fl_ref[...] = _dot(hv, w_ref[:, n_qkv + n_rest:NP], NN)

    row = lambda n: pl.BlockSpec((tm, n), lambda i: (i, 0))
    return pl.pallas_call(
        kern, name="mix_up", grid=(T // tm,),
        in_specs=[row(D), pl.BlockSpec((1, D), lambda i: (0, 0)), _resident((None, D, NP), lambda i: (l, 0, 0))],
        out_specs=[row(D), row(n_qkv), row(n_rest), row(LANES)],
        out_shape=[jax.ShapeDtypeStruct((T, D), BF16), jax.ShapeDtypeStruct((T, n_qkv), BF16),
                   jax.ShapeDtypeStruct((T, n_rest), F32), jax.ShapeDtypeStruct((T, LANES), F32)],
        compiler_params=_params(1),
    )(x, g, wp)


def _column_starts(pieces):
    starts, at = [], 0
    for p in pieces:
        starts.append(at)
        at += p.shape[1]
    return starts


def _mix_in_bwd(pieces, x, g, dres, wp):
    T, D = x.shape
    wp, l = wp
    NP = wp.shape[2]
    tm = _tile(T, 512)
    n, starts = len(pieces), _column_starts(pieces)

    def kern(*refs):
        x_ref, g_ref, d_ref, w_ref, dx_ref, dg_ref = refs[n:]
        dh = jnp.zeros((tm, D), F32)
        for p_ref, at in zip(refs[:n], starts):
            dh = dh + _dot(p_ref[...].astype(BF16), w_ref[:, at:at + p_ref.shape[1]], NT)
        xv = x_ref[...]
        r = lax.rsqrt(jnp.mean(xv * xv, axis=-1, keepdims=True) + RMS_EPS)
        y = xv * r
        dy = dh * g_ref[...]
        dx_ref[...] = d_ref[...] + r * (dy - y * jnp.mean(dy * y, axis=-1, keepdims=True))
        part = jnp.sum(dh * y, axis=0, keepdims=True)

        @pl.when(pl.program_id(0) == 0)
        def _():
            dg_ref[...] = part

        @pl.when(pl.program_id(0) > 0)
        def _():
            dg_ref[...] += part

    row = lambda n: pl.BlockSpec((tm, n), lambda i: (i, 0))
    vec = pl.BlockSpec((1, D), lambda i: (0, 0))
    return pl.pallas_call(
        kern, name="mix_in_bwd", grid=(T // tm,),
        in_specs=[row(p.shape[1]) for p in pieces] + [row(D), vec, row(D), _resident((None, D, NP), lambda i: (l, 0, 0))],
        out_specs=[row(D), vec],
        out_shape=[jax.ShapeDtypeStruct((T, D), F32), jax.ShapeDtypeStruct((1, D), F32)],
        compiler_params=_params(1),
    )(*pieces, x, g, dres, wp)


def _pieces_dw(name, a, pieces, out_dtype, tk_pref):
    T, M = a.shape
    n, starts = len(pieces), _column_starts(pieces)
    N = starts[-1] + pieces[-1].shape[1]
    tk = _tile(T, tk_pref)
    nk = T // tk

    def kern(a_ref, *refs):
        o_ref, acc = refs[n], refs[n + 1]
        k = pl.program_id(0)

        @pl.when(k == 0)
        def _():
            acc[...] = jnp.zeros_like(acc)

        av = a_ref[...].astype(BF16)
        for p_ref, at in zip(refs[:n], starts):
            acc[:, at:at + p_ref.shape[1]] += _dot(av, p_ref[...].astype(BF16), TN)

        @pl.when(k == nk - 1)
        def _():
            o_ref[...] = acc[...].astype(out_dtype)

    return pl.pallas_call(
        kern, name=name, grid=(nk,),
        in_specs=[pl.BlockSpec((tk, M), lambda k: (k, 0))] + [pl.BlockSpec((tk, p.shape[1]), lambda k: (k, 0)) for p in pieces],
        out_specs=pl.BlockSpec((M, N), lambda k: (0, 0)), out_shape=jax.ShapeDtypeStruct((M, N), out_dtype),
        scratch_shapes=[pltpu.VMEM((M, N), F32)], compiler_params=_params(1),
    )(a, *pieces)


def _rows_dw(name, pieces, d, out_dtype):
    T, N = d.shape
    n, starts = len(pieces), _column_starts(pieces)
    M = starts[-1] + pieces[-1].shape[1]
    tk = _tile(T, 2048)
    nk = T // tk

    def kern(*refs):
        d_ref, o_ref, acc = refs[n], refs[n + 1], refs[n + 2]
        k = pl.program_id(0)

        @pl.when(k == 0)
        def _():
            acc[...] = jnp.zeros_like(acc)

        dv = d_ref[...].astype(BF16)
        for p_ref, at in zip(refs[:n], starts):
            acc[at:at + p_ref.shape[1], :] += _dot(p_ref[...], dv, TN)

        @pl.when(k == nk - 1)
        def _():
            o_ref[...] = acc[...].astype(out_dtype)

    return pl.pallas_call(
        kern, name=name, grid=(nk,),
        in_specs=[pl.BlockSpec((tk, p.shape[1]), lambda k: (k, 0)) for p in pieces] + [pl.BlockSpec((tk, N), lambda k: (k, 0))],
        out_specs=pl.BlockSpec((M, N), lambda k: (0, 0)), out_shape=jax.ShapeDtypeStruct((M, N), out_dtype),
        scratch_shapes=[pltpu.VMEM((M, N), F32)], compiler_params=_params(1),
    )(*pieces, d)


def _mix_out(pieces, w, x):
    T, D = x.shape
    w, l = w
    n, starts = len(pieces), _column_starts(pieces)
    tm = _tile(T, 512)

    def kern(*refs):
        w_ref, x_ref, o_ref = refs[n:]
        acc = x_ref[...]
        for p_ref, at in zip(refs[:n], starts):
            acc = acc + _dot(p_ref[...], w_ref[at:at + p_ref.shape[1], :], NN)
        o_ref[...] = acc

    row = lambda m: pl.BlockSpec((tm, m), lambda i: (i, 0))
    return pl.pallas_call(
        kern, name="mix_out", grid=(T // tm,),
        in_specs=[row(p.shape[1]) for p in pieces] + [_resident((None,) + w.shape[1:], lambda i: (l, 0, 0)), row(D)],
        out_specs=row(D), out_shape=jax.ShapeDtypeStruct((T, D), F32), compiler_params=_params(1),
    )(*pieces, w, x)


def _log_sigmoid(z):
    return jnp.minimum(z, 0.0) - jnp.log(1.0 + jnp.exp(-jnp.abs(z)))


def _decay_fwd(fl, bias):
    B, S, _ = fl.shape

    def kern(fl_ref, b_ref, o_ref):
        d = _log_sigmoid(fl_ref[...] + b_ref[...])
        row = lax.broadcasted_iota(jnp.int32, (S, LANES), 0)
        sh = 1
        while sh < S:
            d = d + jnp.where(row >= sh, pltpu.roll(d, sh, 0), 0.0)
            sh *= 2
        o_ref[...] = d.T[0:8, :]

    return pl.pallas_call(
        kern, name="decay_fwd", grid=(B,),
        in_specs=[pl.BlockSpec((None, S, LANES), lambda b: (b, 0, 0)), pl.BlockSpec((1, LANES), lambda b: (0, 0))],
        out_specs=pl.BlockSpec((None, 8, S), lambda b: (b, 0, 0)),
        out_shape=jax.ShapeDtypeStruct((B, 8, S), F32), compiler_params=_params(1),
    )(fl, bias)


def _decay_bwd(ddrow, ddcol, fl, bias, n_heads):
    B, S, _ = fl.shape

    def kern(dd_ref, ddc_ref, fl_ref, b_ref, dfl_ref, db_ref):
        dd = jnp.concatenate([dd_ref[...], jnp.zeros((LANES - 8, S), F32)], axis=0).T + ddc_ref[...]
        row = lax.broadcasted_iota(jnp.int32, (S, LANES), 0)
        lane = lax.broadcasted_iota(jnp.int32, (S, LANES), 1)
        sh = 1
        while sh < S:
            dd = dd + jnp.where(row < S - sh, pltpu.roll(dd, S - sh, 0), 0.0)
            sh *= 2
        z = fl_ref[...] + b_ref[...]
        dfl = jnp.where(lane < n_heads, dd / (1.0 + jnp.exp(z)), 0.0)
        dfl_ref[...] = dfl
        part = jnp.sum(dfl, axis=0, keepdims=True)

        @pl.when(pl.program_id(0) == 0)
        def _():
            db_ref[...] = part

        @pl.when(pl.program_id(0) > 0)
        def _():
            db_ref[...] += part

    return pl.pallas_call(
        kern, name="decay_bwd", grid=(B,),
        in_specs=[pl.BlockSpec((None, 8, S), lambda b: (b, 0, 0)), pl.BlockSpec((None, S, LANES), lambda b: (b, 0, 0)),
                  pl.BlockSpec((None, S, LANES), lambda b: (b, 0, 0)), pl.BlockSpec((1, LANES), lambda b: (0, 0))],
        out_specs=[pl.BlockSpec((None, S, LANES), lambda b: (b, 0, 0)), pl.BlockSpec((1, LANES), lambda b: (0, 0))],
        out_shape=[jax.ShapeDtypeStruct((B, S, LANES), F32), jax.ShapeDtypeStruct((1, LANES), F32)],
        compiler_params=_params(1),
    )(ddrow, ddcol, fl, bias)


def _attn_fwd(qkv, drow, n_heads, tq):
    B, S, _ = qkv.shape
    DA = n_heads * HEAD_DIM
    scale = HEAD_DIM ** -0.5

    n_pairs = n_heads // 2

    def kern(q_ref, k_ref, v_ref, dr_ref, o_ref, lse_ref):
        i = pl.program_id(1)
        lane = lax.broadcasted_iota(jnp.int32, (tq, LANES), 1)
        low = lane < HEAD_DIM
        causal = lax.broadcasted_iota(jnp.int32, (tq, tq), 1) <= lax.broadcasted_iota(jnp.int32, (tq, tq), 0)
        qms = []
        for p in range(n_pairs):
            q2 = q_ref[:, LANES * p:LANES * (p + 1)] * scale
            qms += [jnp.where(low, q2, jnp.zeros_like(q2)), jnp.where(low, jnp.zeros_like(q2), q2)]

        def step(j, carry, masked):
            ms, ls, accs = carry
            ks = pl.multiple_of(j * tq, tq)
            new_m, new_l, new_acc = [], [], []
            for p in range(n_pairs):
                cols = slice(LANES * p, LANES * (p + 1))
                k2, v2 = k_ref[pl.ds(ks, tq), cols], v_ref[pl.ds(ks, tq), cols]
                alphas, pvs = [], []
                for h in (2 * p, 2 * p + 1):
                    s = _dot(qms[h], k2, NT) - dr_ref[h, pl.ds(j, 1), :]
                    if masked:
                        s = jnp.where(causal, s, -jnp.inf)
                    m_new = jnp.maximum(ms[h], jnp.max(s, axis=1, keepdims=True))
                    alpha = jnp.exp(ms[h] - m_new)
                    pm = jnp.exp(s - m_new)
                    new_m.append(m_new)
                    new_l.append(alpha * ls[h] + jnp.sum(pm, axis=1, keepdims=True))
                    alphas.append(alpha)
                    pvs.append(_dot(pm.astype(BF16), v2, NN))
                new_acc.append(jnp.where(low, alphas[0], alphas[1]) * accs[p] + jnp.where(low, pvs[0], pvs[1]))
            return tuple(new_m), tuple(new_l), tuple(new_acc)

        init = (tuple(jnp.full((tq, 1), -jnp.inf, F32) for _ in range(n_heads)),
                tuple(jnp.zeros((tq, 1), F32) for _ in range(n_heads)),
                tuple(jnp.zeros((tq, LANES), F32) for _ in range(n_pairs)))
        ms, ls, accs = step(i, lax.fori_loop(0, i, functools.partial(step, masked=False), init), True)
        lse_mat = jnp.zeros((tq, LANES), F32)
        for p in range(n_pairs):
            l0, l1 = ls[2 * p], ls[2 * p + 1]
            o_ref[:, LANES * p:LANES * (p + 1)] = (accs[p] / jnp.where(low, l0, l1)).astype(BF16)
            lse_mat = jnp.where(lane == 2 * p, ms[2 * p] + jnp.log(l0), lse_mat)
            lse_mat = jnp.where(lane == 2 * p + 1, ms[2 * p + 1] + jnp.log(l1), lse_mat)
        lse_ref[...] = lse_mat

    nq = S // tq
    return pl.pallas_call(
        kern, name="attn_fwd", grid=(B, nq),
        in_specs=[pl.BlockSpec((None, tq, DA), lambda b, i: (b, i, 0)),
                  pl.BlockSpec((None, S, DA), lambda b, i: (b, 0, 1)),
                  pl.BlockSpec((None, S, DA), lambda b, i: (b, 0, 2)),
                  pl.BlockSpec((None, 8, nq, tq), lambda b, i: (b, 0, 0, 0))],
        out_specs=[pl.BlockSpec((None, tq, DA), lambda b, i: (b, i, 0)),
                   pl.BlockSpec((None, tq, LANES), lambda b, i: (b, i, 0))],
        out_shape=[jax.ShapeDtypeStruct((B, S, DA), BF16), jax.ShapeDtypeStruct((B, S, LANES), F32)],
        compiler_params=_params(2),
    )(qkv, qkv, qkv, drow)


def _attn_bwd(qkv, drow, o, lse, dycat, n_heads, tq):
    B, S, _ = qkv.shape
    DA = n_heads * HEAD_DIM
    scale = HEAD_DIM ** -0.5
    nq = S // tq

    n_pairs = n_heads // 2

    def kern(q_ref, k_ref, v_ref, dr_ref, o_ref, lse_ref, do_ref, dq_ref, dk_ref, dv_ref, ddr_ref, ddc_ref,
             dk_acc, dv_acc, qm_s, dom_s, delta_s, rs_s, dq_s):
        i = pl.program_id(1)

        @pl.when(i == 0)
        def _():
            dk_acc[...] = jnp.zeros_like(dk_acc)
            dv_acc[...] = jnp.zeros_like(dv_acc)
            ddr_ref[...] = jnp.zeros_like(ddr_ref)

        lane = lax.broadcasted_iota(jnp.int32, (tq, LANES), 1)
        low = lane < HEAD_DIM
        causal = lax.broadcasted_iota(jnp.int32, (tq, tq), 1) <= lax.broadcasted_iota(jnp.int32, (tq, tq), 0)
        for p in range(n_pairs):
            cols = slice(LANES * p, LANES * (p + 1))
            q2 = q_ref[:, cols] * scale
            do_f = do_ref[:, cols]
            do2 = do_f.astype(BF16)
            prod = do_f * o_ref[:, cols].astype(F32)
            qm_s[2 * p] = jnp.where(low, q2, jnp.zeros_like(q2))
            qm_s[2 * p + 1] = jnp.where(low, jnp.zeros_like(q2), q2)
            dom_s[2 * p] = jnp.where(low, do2, jnp.zeros_like(do2))
            dom_s[2 * p + 1] = jnp.where(low, jnp.zeros_like(do2), do2)
            delta_s[2 * p] = jnp.sum(jnp.where(low, prod, 0.0), axis=1, keepdims=True)
            delta_s[2 * p + 1] = jnp.sum(jnp.where(low, 0.0, prod), axis=1, keepdims=True)
            dq_s[p] = jnp.zeros((tq, LANES), F32)
        rs_s[...] = jnp.zeros(rs_s.shape, F32)

        def step(j, masked):
            ks = pl.multiple_of(j * tq, tq)
            for p in range(n_pairs):
                cols = slice(LANES * p, LANES * (p + 1))
                k2, v2 = k_ref[pl.ds(ks, tq), cols], v_ref[pl.ds(ks, tq), cols]
                dvs, dks, dqs = [], [], []
                for h in (2 * p, 2 * p + 1):
                    qm, dom = qm_s[h], dom_s[h]
                    s = _dot(qm, k2, NT) - dr_ref[h, pl.ds(j, 1), :]
                    if masked:
                        s = jnp.where(causal, s, -jnp.inf)
                    pm = jnp.exp(s - lse_ref[:, h:h + 1])
                    ds = pm * (_dot(dom, v2, NT) - delta_s[h])
                    ddr_ref[h, pl.ds(j, 1), :] -= jnp.sum(ds, axis=0, keepdims=True)
                    rs_s[h] += jnp.sum(ds, axis=1, keepdims=True)
                    dsb = ds.astype(BF16)
                    dvs.append(_dot(pm.astype(BF16), dom, TN))
                    dks.append(_dot(dsb, qm, TN))
                    dqs.append(_dot(dsb, k2, NN))
                dv_acc[pl.ds(ks, tq), cols] += dvs[0] + dvs[1]
                dk_acc[pl.ds(ks, tq), cols] += dks[0] + dks[1]
                dq_s[p] += jnp.where(low, dqs[0], dqs[1])

        def body(j, carry):
            step(j, False)
            return carry

        lax.fori_loop(0, i, body, 0)
        step(i, True)
        ddc = jnp.zeros((tq, LANES), F32)
        for p in range(n_pairs):
            dq_ref[:, LANES * p:LANES * (p + 1)] = (dq_s[p] * scale).astype(BF16)
            ddc = jnp.where(lane == 2 * p, rs_s[2 * p], ddc)
            ddc = jnp.where(lane == 2 * p + 1, rs_s[2 * p + 1], ddc)
        ddc_ref[...] = ddc

        @pl.when(i == nq - 1)
        def _():
            dk_ref[...] = dk_acc[...].astype(BF16)
            dv_ref[...] = dv_acc[...].astype(BF16)

    tile = pl.BlockSpec((None, tq, DA), lambda b, i: (b, i, 0))
    seq = pl.BlockSpec((None, S, DA), lambda b, i: (b, 0, 0))
    dec = pl.BlockSpec((None, 8, nq, tq), lambda b, i: (b, 0, 0, 0))
    return pl.pallas_call(
        kern, name="attn_bwd", grid=(B, nq),
        in_specs=[tile, pl.BlockSpec((None, S, DA), lambda b, i: (b, 0, 1)),
                  pl.BlockSpec((None, S, DA), lambda b, i: (b, 0, 2)), dec, tile,
                  pl.BlockSpec((None, tq, LANES), lambda b, i: (b, i, 0)), tile],
        out_specs=[tile, seq, seq, dec, pl.BlockSpec((None, tq, LANES), lambda b, i: (b, i, 0))],
        out_shape=[jax.ShapeDtypeStruct((B, S, DA), BF16)] * 3 + [jax.ShapeDtypeStruct((B, 8, nq, tq), F32),
                                                                  jax.ShapeDtypeStruct((B, S, LANES), F32)],
        scratch_shapes=[pltpu.VMEM((S, DA), F32), pltpu.VMEM((S, DA), F32),
                        pltpu.VMEM((n_heads, tq, LANES), BF16), pltpu.VMEM((n_heads, tq, LANES), BF16),
                        pltpu.VMEM((n_heads, tq, 1), F32), pltpu.VMEM((n_heads, tq, 1), F32),
                        pltpu.VMEM((n_pairs, tq, LANES), F32)],
        compiler_params=_params(2),
    )(qkv, qkv, qkv, drow, o, lse, dycat)


def _down(v, d, row):
    return jnp.where(row >= d, pltpu.roll(v, d, 0), 0.0)


def _up(v, d, row, S):
    return jnp.where(row < S - d, pltpu.roll(v, S - d, 0), 0.0)


def _window(v, shift, group):
    sums, acc, d = [], v, 1
    for _ in POOL_WINDOWS:
        acc = acc + shift(acc, d)
        sums.append(acc)
        d *= 2
    out = sums[-1]
    for gi in range(len(POOL_WINDOWS) - 2, -1, -1):
        out = jnp.where(group == gi, sums[gi], out)
    return out


def _pool_count(row, group):
    w = jnp.full(row.shape, POOL_WINDOWS[-1], jnp.int32)
    for gi in range(len(POOL_WINDOWS) - 2, -1, -1):
        w = jnp.where(group == gi, POOL_WINDOWS[gi], w)
    return jnp.minimum(row + 1, w).astype(F32)


def _mix_local_fwd(rest, wbd, ps, cw):
    B, S, C4 = rest.shape
    C = C4 // 4
    gw = C // len(POOL_WINDOWS)

    def kern(r_ref, w_ref, ps_ref, cw_ref, y_ref, pooled_ref):
        row = lax.broadcasted_iota(jnp.int32, (S, C), 0)
        group = lax.broadcasted_iota(jnp.int32, (S, C), 1) // gw
        u = r_ref[:, 0:C]
        pooled = _window(u, lambda v, d: _down(v, d, row), group) / _pool_count(row, group) - u
        pb = pooled.astype(BF16)
        pooled_ref[...] = pb
        y_ref[:, 0:C] = (_dot(pb, w_ref[...], NN) * ps_ref[...]).astype(BF16)
        uc = r_ref[:, 2 * C:3 * C] * r_ref[:, 3 * C:4 * C]
        y = cw_ref[0:1, :] * _down(uc, 2, row) + cw_ref[1:2, :] * _down(uc, 1, row) + cw_ref[2:3, :] * uc
        y_ref[:, C:2 * C] = (r_ref[:, C:2 * C] * y).astype(BF16)

    return pl.pallas_call(
        kern, name="mix_local_fwd", grid=(B,),
        in_specs=[pl.BlockSpec((None, S, C4), lambda b: (b, 0, 0)), pl.BlockSpec((C, C), lambda b: (0, 0)),
                  pl.BlockSpec((1, C), lambda b: (0, 0)), pl.BlockSpec((8, C), lambda b: (0, 0))],
        out_specs=[pl.BlockSpec((None, S, 2 * C), lambda b: (b, 0, 0)), pl.BlockSpec((None, S, C), lambda b: (b, 0, 0))],
        out_shape=[jax.ShapeDtypeStruct((B, S, 2 * C), BF16), jax.ShapeDtypeStruct((B, S, C), BF16)],
        compiler_params=_params(1),
    )(rest, wbd, ps, cw)


def _mix_local_bwd(rest, pooled, dycat, wbd, ps, cw):
    B, S, C4 = rest.shape
    C = C4 // 4
    gw = C // len(POOL_WINDOWS)

    def kern(r_ref, pooled_ref, d_ref, w_ref, ps_ref, cw_ref, dr_ref, dw_ref, dps_ref, dcw_ref):
        row = lax.broadcasted_iota(jnp.int32, (S, C), 0)
        group = lax.broadcasted_iota(jnp.int32, (S, C), 1) // gw
        dyp = d_ref[:, 0:C]
        dyc = d_ref[:, C:2 * C]
        pb = pooled_ref[...]
        dps = jnp.sum(dyp * _dot(pb, w_ref[...], NN), axis=0, keepdims=True)
        dzb = (dyp * ps_ref[...]).astype(BF16)
        dw = _dot(pb, dzb, TN)
        dpooled = _dot(dzb, w_ref[...], NT)
        g = dpooled / _pool_count(row, group)
        dr_ref[:, 0:C] = (_window(g, lambda v, d: _up(v, d, row, S), group) - dpooled).astype(BF16)
        cc, ch = r_ref[:, 2 * C:3 * C], r_ref[:, 3 * C:4 * C]
        uc = cc * ch
        u1, u2 = _down(uc, 1, row), _down(uc, 2, row)
        y = cw_ref[0:1, :] * u2 + cw_ref[1:2, :] * u1 + cw_ref[2:3, :] * uc
        dr_ref[:, C:2 * C] = (dyc * y).astype(BF16)
        dy = dyc * r_ref[:, C:2 * C]
        duc = cw_ref[0:1, :] * _up(dy, 2, row, S) + cw_ref[1:2, :] * _up(dy, 1, row, S) + cw_ref[2:3, :] * dy
        dr_ref[:, 2 * C:3 * C] = (duc * ch).astype(BF16)
        dr_ref[:, 3 * C:4 * C] = (duc * cc).astype(BF16)
        dcw = jnp.concatenate([jnp.sum(dy * u2, axis=0, keepdims=True), jnp.sum(dy * u1, axis=0, keepdims=True),
                               jnp.sum(dy * uc, axis=0, keepdims=True), jnp.zeros((5, C), F32)], axis=0)

        @pl.when(pl.program_id(0) == 0)
        def _():
            dw_ref[...] = dw
            dps_ref[...] = dps
            dcw_ref[...] = dcw

        @pl.when(pl.program_id(0) > 0)
        def _():
            dw_ref[...] += dw
            dps_ref[...] += dps
            dcw_ref[...] += dcw

    full = lambda shape: pl.BlockSpec(shape, lambda b: (0, 0))
    return pl.pallas_call(
        kern, name="mix_local_bwd", grid=(B,),
        in_specs=[pl.BlockSpec((None, S, C4), lambda b: (b, 0, 0)), pl.BlockSpec((None, S, C), lambda b: (b, 0, 0)),
                  pl.BlockSpec((None, S, 2 * C), lambda b: (b, 0, 1)), full((C, C)), full((1, C)), full((8, C))],
        out_specs=[pl.BlockSpec((None, S, C4), lambda b: (b, 0, 0)), full((C, C)), full((1, C)), full((8, C))],
        out_shape=[jax.ShapeDtypeStruct((B, S, C4), BF16), jax.ShapeDtypeStruct((C, C), F32),
                   jax.ShapeDtypeStruct((1, C), F32), jax.ShapeDtypeStruct((8, C), F32)],
        compiler_params=_params(1),
    )(rest, pooled, dycat, wbd, ps, cw)


def _adamw(w, gs, m, v, token=None):
    R, C = w.shape
    pieces = [p if isinstance(p, tuple) else (p,) for p in gs]
    owner = [s for s, p in enumerate(pieces) for _ in p]
    flat = [a for p in pieces for a in p]
    n = len(flat)
    rows = R // len(pieces)
    tr = _tile(rows, 256)
    per = rows // tr
    tok_ops, tok_specs = _token_operand(token)

    def kern(w_ref, *refs):
        g_refs, (m_ref, v_ref), (g_out, d_ref, nm_ref, nv_ref) = refs[:n], refs[n:n + 2], refs[n + 2 + len(tok_ops):]
        vals, at = [], 0
        for p in pieces:
            vals.append(g_refs[at][...] if len(p) == 1 else g_refs[at][...] + g_refs[at + 1][...])
            at += len(p)
        gv = vals[0]
        for s in range(1, len(pieces)):
            gv = jnp.where(pl.program_id(0) // per == s, vals[s], gv)
        nm = ADAM_B1 * m_ref[...] + (1.0 - ADAM_B1) * gv
        nv = ADAM_B2 * v_ref[...] + (1.0 - ADAM_B2) * (gv * gv)
        m_hat = nm / (1.0 - ADAM_B1 ** ADAM_STEP)
        v_hat = nv / (1.0 - ADAM_B2 ** ADAM_STEP)
        g_out[...] = gv
        d_ref[...] = -ADAM_LR * (m_hat / (jnp.sqrt(v_hat) + ADAM_EPS) + ADAM_WD * w_ref[...])
        nm_ref[...] = nm
        nv_ref[...] = nv

    def piece(s):
        return pl.BlockSpec((tr, C), lambda i: (jnp.clip(i - s * per, 0, per - 1), 0))

    blk = pl.BlockSpec((tr, C), lambda i: (i, 0))
    return pl.pallas_call(
        kern, name="adamw", grid=(R // tr,), in_specs=[blk] + [piece(s) for s in owner] + [blk] * 2 + tok_specs,
        out_specs=[blk] * 4, out_shape=[jax.ShapeDtypeStruct((R, C), F32)] * 4, compiler_params=_params(1),
    )(w, *flat, m, v, *tok_ops)


def _place():
    x, y, c = lax.axis_index("x"), lax.axis_index("y"), lax.axis_index("c")
    return x, y, c, [(1 - x, y), (x, 1 - y), (1 - x, 1 - y)]


def _comm_call(name, body, operands, out_shape, n_sems, aliases=None):
    any_spec = pl.BlockSpec(memory_space=pl.ANY)
    return pl.pallas_call(
        body, name=name, in_specs=[any_spec] * len(operands), out_specs=[any_spec] * len(out_shape),
        out_shape=out_shape, input_output_aliases=aliases or {},
        scratch_shapes=[pltpu.SemaphoreType.DMA((n,)) for n in n_sems],
    )(*operands)


def _my_block():
    return 2 * lax.axis_index("x") + lax.axis_index("y")


def _place_shard(w, dtype, first=0, count=None):
    L, R, C = w.shape
    count = L if count is None else count
    tr = _tile(R, 512)

    def kern(w_ref, o_ref):
        o_ref[...] = w_ref[...].astype(dtype)

    return pl.pallas_call(
        kern, name="place_shard", grid=(count, R // tr),
        in_specs=[pl.BlockSpec((None, tr, C), lambda l, i: (first + l, i, 0))],
        out_specs=pl.BlockSpec((None, None, tr, C), lambda l, i: (l, _my_block(), i, 0)),
        out_shape=jax.ShapeDtypeStruct((count, N_CHIPS, R, C), dtype), compiler_params=_params(2),
    )(w)


HALF_ROWS = 16


def _rows(ref, half):
    hr = ref.shape[-2] // 2
    return ref.at[(slice(None),) * (len(ref.shape) - 2) + (pl.ds(half * hr, hr),)]


def _all_gather(bufs):
    n = len(bufs)

    def body(*refs):
        outs = refs[n:2 * n]
        send_sems, recv_sems = refs[2 * n:]
        x, y, c, chips = _place()
        sibling = (x, y, 1 - c)

        def remote(k, j, chip, half, to):
            blk = 2 * chip[0] + chip[1]
            if outs[k].shape[2] % (2 * HALF_ROWS) == 0:
                region = _rows(outs[k].at[:, blk], half)
            else:
                hl = outs[k].shape[0] // 2
                region = outs[k].at[pl.ds(half * hl, hl), blk]
            return pltpu.make_async_remote_copy(
                src_ref=region, dst_ref=region, send_sem=send_sems.at[6 * k + j],
                recv_sem=recv_sems.at[6 * k + j], device_id=to, device_id_type=MESH)

        first = [remote(k, j, (x, y), c, (*chip, c)) for k in range(n) for j, chip in enumerate(chips)]
        for cp in first:
            cp.start()
        passed = []
        for k in range(n):
            for j, chip in enumerate(chips):
                remote(k, j, chip, c, (x, y, c)).wait_recv()
                passed.append(remote(k, 3 + j, chip, c, sibling))
                passed[-1].start()
        for k in range(n):
            for j, chip in enumerate(chips):
                remote(k, 3 + j, chip, 1 - c, (x, y, c)).wait_recv()
        for cp in first + passed:
            cp.wait_send()

    out_shape = [jax.ShapeDtypeStruct(s.shape, s.dtype) for s in bufs]
    return _comm_call("all_gather_weights", body, bufs, out_shape, (6 * n, 6 * n), aliases={k: k for k in range(n)})


_HBM = pl.BlockSpec(memory_space=pltpu.HBM)
_SEM = pl.BlockSpec(memory_space=pltpu.SEMAPHORE)
_ANY = pl.BlockSpec(memory_space=pl.ANY)


def _split_start(name, bufs, n_copies, make_copies, after):
    n = len(bufs)

    def body(*refs):
        send_sems, recv_sems, token = refs[n + 1], refs[n + 2], refs[2 * n + 3]
        for cp in make_copies(refs[:n], send_sems, recv_sems):
            cp.start()
        token[...] = jnp.zeros_like(token)

    res = pl.pallas_call(
        body, name=name, in_specs=[_HBM] * n + [_ANY],
        out_shape=(pltpu.SemaphoreType.DMA((n_copies,)), pltpu.SemaphoreType.DMA((n_copies,)),
                   *[pltpu.HBM(b.shape, b.dtype) for b in bufs], jax.ShapeDtypeStruct((8, LANES), F32)),
        out_specs=(_SEM, _SEM, *[_HBM] * n, pl.BlockSpec(memory_space=pltpu.VMEM)),
        input_output_aliases={i: 2 + i for i in range(n)},
        compiler_params=pltpu.CompilerParams(has_side_effects=pltpu.SideEffectType.DATAFLOW_SIDE_EFFECTING),
    )(*[pltpu.with_memory_space_constraint(b, pltpu.HBM) for b in bufs], after)
    return res[0], res[1], list(res[2:2 + n]), res[2 + n]


def _split_wait(name, send_sems, recv_sems, bufs, make_copies, after):
    n = len(bufs)

    def body(*refs):
        for cp in make_copies(refs[:n], refs[n], refs[n + 1]):
            cp.wait_send()
            cp.wait_recv()

    return list(pl.pallas_call(
        body, name=name, in_specs=[_HBM] * n + [_SEM, _SEM, _ANY],
        out_shape=tuple(pltpu.HBM(b.shape, b.dtype) for b in bufs), out_specs=tuple([_HBM] * n),
        input_output_aliases={i: i for i in range(n)},
        compiler_params=pltpu.CompilerParams(has_side_effects=pltpu.SideEffectType.DATAFLOW_SIDE_EFFECTING),
    )(*bufs, send_sems, recv_sems, after))


def _gather_copies(refs, send_sems, recv_sems):
    x, y, c, chips = _place()
    return [pltpu.make_async_remote_copy(
        src_ref=ref.at[:, 2 * x + y], dst_ref=ref.at[:, 2 * x + y], send_sem=send_sems.at[3 * k + j],
        recv_sem=recv_sems.at[3 * k + j], device_id=(*chip, c), device_id_type=MESH)
        for k, ref in enumerate(refs) for j, chip in enumerate(chips)]


def _exchange_copies(refs, send_sems, recv_sems):
    n = len(refs) // 2
    x, y, c, chips = _place()
    return [pltpu.make_async_remote_copy(
        src_ref=refs[k].at[:, 2 * chip[0] + chip[1]], dst_ref=refs[n + k].at[j], send_sem=send_sems.at[3 * k + j],
        recv_sem=recv_sems.at[3 * k + j], device_id=(*chip, c), device_id_type=MESH)
        for k in range(n) for j, chip in enumerate(chips)]


def _all_reduce_small(v, after):
    n = v.shape[0]

    def body(v_ref, after_ref, o_ref, gbuf, send_sems, recv_sems):
        x, y, c, _ = _place()
        me = 4 * x + 2 * y + c
        gbuf[me] = v_ref[...]
        copies, waits = [], []
        for r in range(1, N_DEV):
            px = 1 - x if r & 4 else x
            py = 1 - y if r & 2 else y
            pc = 1 - c if r & 1 else c
            mk = functools.partial(pltpu.make_async_remote_copy, src_ref=v_ref, send_sem=send_sems.at[r - 1],
                                   recv_sem=recv_sems.at[r - 1], device_id=(px, py, pc), device_id_type=MESH)
            copies.append(mk(dst_ref=gbuf.at[me]))
            waits.append(mk(dst_ref=gbuf.at[4 * px + 2 * py + pc]))
        for cp in copies:
            cp.start()
        for cp in waits:
            cp.wait_recv()
        for cp in copies:
            cp.wait_send()
        acc = gbuf[0]
        for d in range(1, N_DEV):
            acc = acc + gbuf[d]
        o_ref[...] = acc

    vm = pl.BlockSpec(memory_space=pltpu.VMEM)
    return pl.pallas_call(
        body, name="all_reduce_small", in_specs=[vm, _ANY], out_specs=vm, out_shape=jax.ShapeDtypeStruct(v.shape, F32),
        scratch_shapes=[pltpu.VMEM((N_DEV, n, LANES), F32), pltpu.SemaphoreType.DMA((N_DEV - 1,)),
                        pltpu.SemaphoreType.DMA((N_DEV - 1,))],
        compiler_params=pltpu.CompilerParams(vmem_limit_bytes=VMEM_LIMIT),
    )(v, after)


def _add_blocks(p, h2):
    L, nb, hr, C = p.shape
    tr = _tile(hr, 512)

    def kern(p_ref, h0_ref, h1_ref, h2_ref, o_ref):
        o_ref[...] = ((p_ref[...].astype(F32) + h0_ref[...].astype(F32)) + h1_ref[...].astype(F32)) + h2_ref[...].astype(F32)

    def other(j):
        return pl.BlockSpec((None, None, tr, C), lambda l, i: (j, l, i, 0))

    return pl.pallas_call(
        kern, name="rs_add_blocks", grid=(L, hr // tr),
        in_specs=[pl.BlockSpec((None, None, tr, C), lambda l, i: (l, _my_block(), i, 0)), other(0), other(1), other(2)],
        out_specs=pl.BlockSpec((None, tr, C), lambda l, i: (l, i, 0)),
        out_shape=jax.ShapeDtypeStruct((L, hr, C), F32), compiler_params=_params(2),
    )(p, h2, h2, h2)


WEIGHTS = ("norm_ffn1", "w_ffn1_in", "w_ffn1_out", "norm_mix", "w_mix_in", "b_forget", "w_pool", "pool_scale",
           "conv_w", "w_mix_out", "norm_ffn2", "w_ffn2_in", "w_ffn2_out", "norm_final")
BIG = ("w_ffn1_in", "w_ffn1_out", "w_mix_in", "w_mix_out", "w_ffn2_in", "w_ffn2_out")
SMALL = ("norm_ffn1", "norm_mix", "b_forget", "w_pool", "pool_scale", "conv_w", "norm_ffn2", "norm_final")


def _layer_params(small, gathered, conv_w, D, l):
    DA, C, H = D // 2, D // 4, D // 2 // HEAD_DIM
    P = {}
    if "w_ffn1_in" in gathered:
        P.update(g1=small["norm_ffn1"][l][None], w1in=(gathered["w_ffn1_in"], 0),
                 w1out=(gathered["w_ffn1_out"].reshape(1, -1, D), 0))
    if "w_ffn2_in" in gathered:
        P.update(g2=small["norm_ffn2"][l][None], w2in=(gathered["w_ffn2_in"], 0),
                 w2out=(gathered["w_ffn2_out"].reshape(1, -1, D), 0))
    if "w_mix_in" in gathered:
        w_in = jnp.concatenate([gathered["w_mix_in"][:, b] for b in range(N_CHIPS)], axis=2)
        wqkv, wrest = w_in[:, :, :3 * DA], w_in[:, :, 3 * DA + H:]
        wf = jnp.pad(w_in[:, :, 3 * DA:3 * DA + H], ((0, 0), (0, 0), (0, LANES - H)))
        ng = len(POOL_WINDOWS)
        same_group = jnp.eye(ng, dtype=bool)[:, None, :, None]
        wbd = jnp.where(same_group, small["w_pool"][l][:, :, None, :], 0.0).reshape(C, C)
        cw = jnp.concatenate([conv_w[l, b] for b in range(N_CHIPS)], axis=1)
        P.update(gm=small["norm_mix"][l][None], wp=(jnp.concatenate([wqkv, wrest, wf], axis=2), 0),
                 wmixout=(gathered["w_mix_out"].reshape(1, D, D), 0),
                 bias=jnp.pad(small["b_forget"][l][None], ((0, 0), (0, LANES - H))), wbd=wbd.astype(BF16),
                 ps=small["pool_scale"][l][None], cw=jnp.pad(cw, ((0, 8 - CONV_WIDTH), (0, 0))))
    return P


def _ffn_fwd(x, g, w_in, w_out, token=None):
    h, jac, act = _ffn_up(x, g, w_in, token)
    return _ffn_out(act, w_out, x)[0], (x, h, jac, act)


def _ffn_bwd(dres, saved, g, w_in, w_out, token=None):
    x, h, jac, act = saved
    dgu, dx, dg = _ffn_bwd_main(dres, jac, x, g, w_out, w_in, token)
    dw_out = _ffn_dw_out(act, dres)[0]
    dw_in = _ffn_dw_in(h, dgu)[0]
    return dx, dg, dw_in, dw_out.reshape(N_CHIPS, -1, dw_out.shape[1])


def _mixer_fwd(x, P, B, S, tq):
    T, D = x.shape
    DA, C, H = D // 2, D // 4, D // 2 // HEAD_DIM
    hn, qkv, rest, fl = _mix_up(x, P["gm"], P["wp"], (3 * DA, 4 * C))
    qkv, rest, fl = qkv.reshape(B, S, 3 * DA), rest.reshape(B, S, 4 * C), fl.reshape(B, S, LANES)
    drow = _decay_fwd(fl, P["bias"]).reshape(B, 8, S // tq, tq)
    o, lse = _attn_fwd(qkv, drow, H, tq)
    ypc, pooled = _mix_local_fwd(rest, P["wbd"], P["ps"], P["cw"])
    x_out = _mix_out([o.reshape(T, DA), ypc.reshape(T, 2 * C)], P["wmixout"], x)
    return x_out, (x, hn, qkv, rest, fl, drow, o, lse, pooled, ypc)


def _mixer_bwd(dres, saved, P, B, S, tq):
    x, hn, qkv, rest, fl, drow, o, lse, pooled, ypc = saved
    T, D = x.shape
    DA, C, H = D // 2, D // 4, D // 2 // HEAD_DIM
    dycat = _proj("mix_out_bwd", dres, P["wmixout"], F32, NT).reshape(B, S, D)
    dw_out = _rows_dw("mix_out_dw", [o.reshape(T, DA), ypc.reshape(T, 2 * C)], dres, BF16)
    dq, dk, dv, ddrow, ddcol = _attn_bwd(qkv, drow, o, lse, dycat, H, tq)
    dfl, dbias = _decay_bwd(ddrow.reshape(B, 8, S), ddcol, fl, P["bias"], H)
    drest, dwbd, dps, dcw = _mix_local_bwd(rest, pooled, dycat, P["wbd"], P["ps"], P["cw"])
    pieces = [a.reshape(T, a.shape[-1]) for a in (dq, dk, dv, drest, dfl)]
    dwp = _pieces_dw("mix_in_dw", hn, pieces, F32, 1024)
    dx, dg = _mix_in_bwd(pieces, x, P["gm"], dres, P["wp"])
    n_q, n_r = 3 * DA, 4 * C
    dw_in = jnp.concatenate([dwp[:, :n_q], dwp[:, n_q + n_r:n_q + n_r + H], dwp[:, n_q:n_q + n_r]], axis=1)
    dw_in = dw_in.reshape(D, N_CHIPS, -1).transpose(1, 0, 2).astype(BF16)
    ng = len(POOL_WINDOWS)
    same_group = jnp.eye(ng, dtype=bool)[:, None, :, None]
    dw_pool = jnp.where(same_group, dwbd.reshape(ng, C // ng, ng, C // ng), 0.0).sum(axis=2)
    small = dict(norm_mix=dg[0], b_forget=dbias[0, :H], w_pool=dw_pool, pool_scale=dps[0], conv_w=dcw[:CONV_WIDTH])
    return dx, small, dw_in, dw_out.reshape(N_CHIPS, -1, D)


FFN1, MIX, FFN2 = BIG[:2], BIG[2:4], BIG[4:]


def _local_step(x, target, small, conv_w, pipe):
    B, S, D = x.shape
    L = small["norm_ffn1"].shape[0]
    tq = _tile(S, 512)
    xt = x.reshape(B * S, D)
    saved, params = [], []
    for l in range(L):
        P = _layer_params(small, pipe.weights(l, xt), conv_w, D, l)
        xt, s1 = _ffn_fwd(xt, P["g1"], P["w1in"], P["w1out"], pipe.token(l))
        P.update(_layer_params(small, pipe.weights_mix(l, xt), conv_w, D, l))
        xt, s2 = _mixer_fwd(xt, P, B, S, tq)
        P.update(_layer_params(small, pipe.weights_ffn2(l, xt), conv_w, D, l))
        xt, s3 = _ffn_fwd(xt, P["g2"], P["w2in"], P["w2out"])
        saved.append((s1, s2, s3))
        params.append(P)
    dres, dgf, loss = _final_loss(xt, small["norm_final"][None], target.reshape(B * S, D))
    sm = {k: [None] * L for k in SMALL if k != "norm_final"}
    token = None
    for l in reversed(range(L)):
        P, (s1, s2, s3) = params[l], saved[l]
        big = {}
        dres, dg2, big["w_ffn2_in"], big["w_ffn2_out"] = _ffn_bwd(dres, s3, P["g2"], P["w2in"], P["w2out"], token)
        dres, smix, big["w_mix_in"], big["w_mix_out"] = _mixer_bwd(dres, s2, P, B, S, tq)
        big = {k: val[None] for k, val in big.items()}
        token = pipe.grads(l, FFN2 + MIX, big, dres) if l == 0 else None
        dres, dg1, dw_in, dw_out = _ffn_bwd(dres, s1, P["g1"], P["w1in"], P["w1out"], token)
        big.update(w_ffn1_in=dw_in[None], w_ffn1_out=dw_out[None])
        sm["norm_ffn1"][l], sm["norm_ffn2"][l] = dg1[0], dg2[0]
        for k, val in smix.items():
            sm[k][l] = val
        token = pipe.grads(l, FFN1 if l == 0 else BIG, big, big["w_ffn1_in"])
    sm = {k: jnp.stack(val) for k, val in sm.items()}
    sm["norm_final"] = dgf[0]
    return loss[0, 0], dres.reshape(B, S, D), sm


def _sibling_copies(refs, send_sems, recv_sems):
    n = len(refs) // 2
    x, y, c, _ = _place()
    return [pltpu.make_async_remote_copy(
        src_ref=refs[k], dst_ref=refs[n + k], send_sem=send_sems.at[k], recv_sem=recv_sems.at[k],
        device_id=(x, y, 1 - c), device_id_type=MESH) for k in range(n)]


class _Pipeline:
    def __init__(self, w):
        self.w, self.n_layers = w, w[BIG[0]].shape[0]
        first = _all_gather([_place_shard(w[k], BF16, 0, 1) for k in FFN1] + [_place_shard(w["conv_w"], F32)])
        self.conv_w = first[-1]
        self._ready = dict(zip(FFN1, first[:-1]))
        self._mix = self._start_gather("0_mix", MIX, 0, first[0])
        self._ffn2 = self._start_gather("0_ffn2", FFN2, 0, self._mix[1][3])
        self._next = (1, self._start_gather("1", BIG, 1, self._ffn2[1][3]))
        self._reduce, self._swaps = None, []
        self.reduced = [dict() for _ in range(self.n_layers)]

    def _start_gather(self, tag, kinds, l, after):
        placed = [_place_shard(self.w[k], BF16, l, 1) for k in kinds]
        return kinds, _split_start(f"gather_start_{tag}", placed, 3 * len(kinds), _gather_copies, after)

    def _wait_gather(self, tag, started, after):
        kinds, (send_sems, recv_sems, bufs, _) = started
        return dict(zip(kinds, _split_wait(f"gather_wait_{tag}", send_sems, recv_sems, bufs, _gather_copies, after)))

    def token(self, l):
        return self._next[1][1][3] if self._next is not None and self._next[0] == l + 1 else None

    def weights(self, l, after):
        if l == 0:
            return self._ready
        got = self._wait_gather(str(l), self._next[1], after)
        self._next = (l + 1, self._start_gather(str(l + 1), BIG, l + 1, got[BIG[0]])) if l + 1 < self.n_layers else None
        return got

    def weights_mix(self, l, after):
        return self._wait_gather("0_mix", self._mix, after) if l == 0 else {}

    def weights_ffn2(self, l, after):
        return self._wait_gather("0_ffn2", self._ffn2, after) if l == 0 else {}

    def _finish_reduce(self, after):
        if self._reduce is None:
            return None
        tag, l, kinds, (send_sems, recv_sems, bufs, _) = self._reduce
        n = len(kinds)
        bufs = _split_wait(f"reduce_wait_{tag}", send_sems, recv_sems, bufs, _exchange_copies, after)
        mine = [_add_blocks(p, o) for p, o in zip(bufs[:n], bufs[n:])]
        lands = [lax.empty(q.shape, q.dtype) for q in mine]
        self._swaps.append((tag, l, kinds, _split_start(f"swap_start_{tag}", mine + lands, n, _sibling_copies, mine[0])))
        self._reduce = None
        return self._swaps[-1][3][3]

    def grads(self, l, kinds, big, after):
        swap_token = self._finish_reduce(after)
        grads = [big[k] for k in kinds]
        tag = str(l) if len(kinds) == len(BIG) else f"{l}_{kinds[0][2:]}"
        lands = [lax.empty((3, g.shape[0]) + g.shape[2:], g.dtype) for g in grads]
        started = _split_start(f"reduce_start_{tag}", grads + lands, 3 * len(kinds), _exchange_copies,
                               grads[0] if swap_token is None else swap_token)
        self._reduce = (tag, l, kinds, started)
        self.last_token = started[3]
        return started[3]

    def finish(self, after, last=False):
        if last:
            self._finish_reduce(after)
        for tag, l, kinds, (send_sems, recv_sems, bufs, _) in self._swaps:
            n = len(kinds)
            bufs = _split_wait(f"swap_wait_{tag}", send_sems, recv_sems, bufs, _sibling_copies, after)
            self.reduced[l].update(zip(kinds, zip(bufs[:n], bufs[n:])))
        self._swaps = []
        return self.reduced


def _pack(parts, extra=()):
    flat = jnp.concatenate([p.reshape(-1) for p in parts] + [jnp.reshape(e, (1,)) for e in extra])
    n = -(-flat.shape[0] // (8 * LANES)) * 8
    return jnp.pad(flat, (0, n * LANES - flat.shape[0])).reshape(n, LANES)


def _unpack(buf, shapes):
    flat, out, at = buf.reshape(-1), [], 0
    for s in shapes:
        n = math.prod(s)
        out.append(flat[at:at + n].reshape(s))
        at += n
    return out, flat[at:]


def kernel(x, norm_ffn1, w_ffn1_in, w_ffn1_out, norm_mix, w_mix_in, b_forget, w_pool, pool_scale, conv_w, w_mix_out, norm_ffn2, w_ffn2_in, w_ffn2_out, norm_final, loss_target, m_norm_ffn1, m_w_ffn1_in, m_w_ffn1_out, m_norm_mix, m_w_mix_in, m_b_forget, m_w_pool, m_pool_scale, m_conv_w, m_w_mix_out, m_norm_ffn2, m_w_ffn2_in, m_w_ffn2_out, m_norm_final, v_norm_ffn1, v_w_ffn1_in, v_w_ffn1_out, v_norm_mix, v_w_mix_in, v_b_forget, v_w_pool, v_pool_scale, v_conv_w, v_w_mix_out, v_norm_ffn2, v_w_ffn2_in, v_w_ffn2_out, v_norm_final):
    w = dict(zip(WEIGHTS, (norm_ffn1, w_ffn1_in, w_ffn1_out, norm_mix, w_mix_in, b_forget, w_pool, pool_scale, conv_w, w_mix_out, norm_ffn2, w_ffn2_in, w_ffn2_out, norm_final)))
    m = dict(zip(WEIGHTS, (m_norm_ffn1, m_w_ffn1_in, m_w_ffn1_out, m_norm_mix, m_w_mix_in, m_b_forget, m_w_pool, m_pool_scale, m_conv_w, m_w_mix_out, m_norm_ffn2, m_w_ffn2_in, m_w_ffn2_out, m_norm_final)))
    v = dict(zip(WEIGHTS, (v_norm_ffn1, v_w_ffn1_in, v_w_ffn1_out, v_norm_mix, v_w_mix_in, v_b_forget, v_w_pool, v_pool_scale, v_conv_w, v_w_mix_out, v_norm_ffn2, v_w_ffn2_in, v_w_ffn2_out, v_norm_final)))
    block = 2 * lax.axis_index("x") + lax.axis_index("y")

    pipe = _Pipeline(w)
    small = {k: w[k] for k in SMALL}
    loss, grad_x, sm = _local_step(x, loss_target, small, pipe.conv_w, pipe)
    grads, delta, new_m, new_v = {}, {}, {}, {}

    def big_adamw(k, reduced, token=None):
        two_d = lambda a: a.reshape(-1, a.shape[-1])
        pieces = [tuple(map(two_d, layer[k])) for layer in reduced]
        res = _adamw(two_d(w[k]), pieces, two_d(m[k]), two_d(v[k]), token)
        grads[k], delta[k], new_m[k], new_v[k] = [r.reshape(w[k].shape) for r in res]

    reduced = pipe.finish(grad_x)
    for k in BIG[2:]:
        big_adamw(k, reduced, pipe.last_token)
    reduced = pipe.finish(new_v[BIG[-1]], last=True)
    for k in BIG[:2]:
        big_adamw(k, reduced)

    order = list(SMALL)
    total = _all_reduce_small(_pack([sm[k] for k in order], extra=(loss,)), new_v[BIG[0]])
    parts, rest = _unpack(total, [sm[k].shape for k in order])
    grads.update(zip(order, parts))
    loss = rest[0]
    cs = conv_w.shape[2]
    grads["conv_w"] = lax.dynamic_slice_in_dim(grads["conv_w"], block * cs, cs, axis=2)
    packed = [_pack([t[k] for k in order]) for t in (w, grads, m, v)]
    _, d, nm, nv = _adamw(packed[0], [packed[1]], packed[2], packed[3])
    shapes = [w[k].shape for k in order]
    for res, flat in ((delta, d), (new_m, nm), (new_v, nv)):
        res.update(zip(order, _unpack(flat, shapes)[0]))
    return (loss, grad_x, *[grads[k] for k in WEIGHTS], *[delta[k] for k in WEIGHTS],
            *[new_m[k] for k in WEIGHTS], *[new_v[k] for k in WEIGHTS])
```

```python
import functools
import math

import jax
import jax.numpy as jnp
from jax import lax
from jax.experimental import pallas as pl
from jax.experimental.pallas import tpu as pltpu

F32 = jnp.float32
BF16 = jnp.bfloat16
MESH = pl.DeviceIdType.MESH

HEAD_DIM = 64
POOL_WINDOWS = (2, 4, 8, 16)
CONV_WIDTH = 3
RMS_EPS = 1e-6
ADAM_LR = 0.001
ADAM_B1 = 0.9
ADAM_B2 = 0.999
ADAM_EPS = 1e-08
ADAM_WD = 0.01
ADAM_STEP = 10

LANES = 128
VMEM_LIMIT = 56 * 1024 * 1024
N_CHIPS = 4
N_DEV = 8

NN = (((1,), (0,)), ((), ()))
NT = (((1,), (1,)), ((), ()))
TN = (((0,), (0,)), ((), ()))


def _tile(n, pref):
    for t in range(pref - pref % 16, 15, -16):
        if n % t == 0:
            return t
    return n


def _params(n_grid):
    return pltpu.CompilerParams(dimension_semantics=("arbitrary",) * n_grid, vmem_limit_bytes=VMEM_LIMIT)


def _dot(a, b, dims):
    return lax.dot_general(a, b, dims, preferred_element_type=F32)


def _mm(name, dims, operands, in_specs, out_shape, out_specs, grid, acc_shape, epilogue):
    n_in, n_out, nk = len(operands), len(out_shape), grid[-1]

    def kern(*refs):
        extras, outs = refs[2:n_in], refs[n_in:n_in + n_out]
        if nk == 1:
            epilogue(_dot(refs[0][...].astype(BF16), refs[1][...].astype(BF16), dims), extras, outs)
            return
        acc = refs[n_in + n_out]
        k = pl.program_id(len(grid) - 1)

        @pl.when(k == 0)
        def _():
            acc[...] = jnp.zeros_like(acc)

        acc[...] += _dot(refs[0][...].astype(BF16), refs[1][...].astype(BF16), dims)

        @pl.when(k == nk - 1)
        def _():
            epilogue(acc[...], extras, outs)

    return pl.pallas_call(
        kern, name=name, grid=grid, in_specs=in_specs, out_specs=out_specs, out_shape=out_shape,
        scratch_shapes=[pltpu.VMEM(acc_shape, F32)] if nk > 1 else [],
        compiler_params=_params(len(grid)),
    )(*operands)


def _store(scale=None, dtype=None):
    def ep(acc, extras, outs):
        v = acc if scale is None else acc * scale
        outs[0][...] = v.astype(outs[0].dtype)
    return ep


def _residual(scale):
    def ep(acc, extras, outs):
        outs[0][...] = extras[0][...] + scale * acc
    return ep


def _final_loss(x, g, target):
    T, D = x.shape
    tr = _tile(T, 256)

    def kern(x_ref, g_ref, t_ref, dx_ref, dg_ref, loss_ref):
        xv = x_ref[...]
        r = lax.rsqrt(jnp.mean(xv * xv, axis=-1, keepdims=True) + RMS_EPS)
        y = xv * r
        err = y * g_ref[...] - t_ref[...]
        lpart = 0.5 * jnp.sum(jnp.mean(err * err, axis=-1, keepdims=True), axis=0, keepdims=True)
        dh = err * (1.0 / D)
        dy = dh * g_ref[...]
        dx_ref[...] = r * (dy - y * jnp.mean(dy * y, axis=-1, keepdims=True))
        part = jnp.sum(dh * y, axis=0, keepdims=True)
        lrow = jnp.broadcast_to(lpart, (1, LANES))

        @pl.when(pl.program_id(0) == 0)
        def _():
            dg_ref[...] = part
            loss_ref[...] = lrow

        @pl.when(pl.program_id(0) > 0)
        def _():
            dg_ref[...] += part
            loss_ref[...] += lrow

    row = pl.BlockSpec((tr, D), lambda i: (i, 0))
    vec = pl.BlockSpec((1, D), lambda i: (0, 0))
    return pl.pallas_call(
        kern, name="final_loss", grid=(T // tr,), in_specs=[row, vec, row],
        out_specs=[row, vec, pl.BlockSpec((1, LANES), lambda i: (0, 0))],
        out_shape=[jax.ShapeDtypeStruct((T, D), F32), jax.ShapeDtypeStruct((1, D), F32),
                   jax.ShapeDtypeStruct((1, LANES), F32)],
        compiler_params=_params(1),
    )(x, g, target)


def _resident(shape, index_map):
    return pl.BlockSpec(shape, index_map, pipeline_mode=pl.Buffered(1))


def _token_operand(token):
    return ([], []) if token is None else ([token], [pl.BlockSpec(token.shape, lambda i: (0, 0))])


def _ffn_up(x, g, w4, token=None):
    T, D = x.shape
    w4, l = w4
    Fh = w4.shape[3]
    F = 2 * Fh
    tm = _tile(T, 512)
    tok_ops, tok_specs = _token_operand(token)

    def kern(x_ref, g_ref, w_ref, *rest):
        h_ref, jac_ref, act_ref = rest[len(tok_ops):]
        xv = x_ref[...]
        r = lax.rsqrt(jnp.mean(xv * xv, axis=-1, keepdims=True) + RMS_EPS)
        hv = (xv * r * g_ref[...]).astype(BF16)
        h_ref[...] = hv
        for j in range(2):
            cols = slice(j * Fh, (j + 1) * Fh)
            gate = _dot(hv, w_ref[j], NN)
            up = _dot(hv, w_ref[2 + j], NN)
            sg = jax.nn.sigmoid(gate)
            silu = gate * sg
            jac_ref[0, :, cols] = (up * (sg + silu * (1.0 - sg))).astype(BF16)
            jac_ref[1, :, cols] = silu.astype(BF16)
            act_ref[:, cols] = (silu * up).astype(BF16)

    return pl.pallas_call(
        kern, name="ffn_up", grid=(T // tm,),
        in_specs=[pl.BlockSpec((tm, D), lambda i: (i, 0)), pl.BlockSpec((1, D), lambda i: (0, 0)),
                  _resident((None, 4, D, Fh), lambda i: (l, 0, 0, 0))] + tok_specs,
        out_specs=[pl.BlockSpec((tm, D), lambda i: (i, 0)), pl.BlockSpec((2, tm, F), lambda i: (0, i, 0)),
                   pl.BlockSpec((tm, F), lambda i: (i, 0))],
        out_shape=[jax.ShapeDtypeStruct((T, D), BF16), jax.ShapeDtypeStruct((2, T, F), BF16),
                   jax.ShapeDtypeStruct((T, F), BF16)],
        compiler_params=_params(1),
    )(x, g, w4, *tok_ops)


def _ffn_bwd_main(dres, jac, x, g, w_out, w4, token=None):
    T, D = dres.shape
    w_out, l = w_out
    w4, _ = w4
    F = w_out.shape[1]
    Fh = F // 2
    tm = _tile(T, 512)
    tok_ops, tok_specs = _token_operand(token)

    def kern(d_ref, jac_ref, x_ref, g_ref, wo_ref, wi_ref, *rest):
        dgu_ref, dx_ref, dg_ref = rest[len(tok_ops):]
        dv = d_ref[...]
        d16 = dv.astype(BF16)
        dh = jnp.zeros((tm, D), F32)
        for j in range(2):
            cols = slice(j * Fh, (j + 1) * Fh)
            dact = 0.5 * _dot(d16, wo_ref[cols, :], NT)
            dgate = (dact * jac_ref[0, :, cols].astype(F32)).astype(BF16)
            dup = (dact * jac_ref[1, :, cols].astype(F32)).astype(BF16)
            dgu_ref[0, :, cols] = dgate
            dgu_ref[1, :, cols] = dup
            dh = dh + _dot(dgate, wi_ref[j], NT) + _dot(dup, wi_ref[2 + j], NT)
        xv = x_ref[...]
        r = lax.rsqrt(jnp.mean(xv * xv, axis=-1, keepdims=True) + RMS_EPS)
        y = xv * r
        dy = dh * g_ref[...]
        dx_ref[...] = dv + r * (dy - y * jnp.mean(dy * y, axis=-1, keepdims=True))
        part = jnp.sum(dh * y, axis=0, keepdims=True)

        @pl.when(pl.program_id(0) == 0)
        def _():
            dg_ref[...] = part

        @pl.when(pl.program_id(0) > 0)
        def _():
            dg_ref[...] += part

    row = pl.BlockSpec((tm, D), lambda i: (i, 0))
    vec = pl.BlockSpec((1, D), lambda i: (0, 0))
    wide = pl.BlockSpec((2, tm, F), lambda i: (0, i, 0))
    return pl.pallas_call(
        kern, name="ffn_bwd_main", grid=(T // tm,),
        in_specs=[row, wide, row, vec, _resident((None, F, D), lambda i: (l, 0, 0)),
                  _resident((None, 4, D, Fh), lambda i: (l, 0, 0, 0))] + tok_specs,
        out_specs=[wide, row, vec],
        out_shape=[jax.ShapeDtypeStruct((2, T, F), BF16), jax.ShapeDtypeStruct((T, D), F32),
                   jax.ShapeDtypeStruct((1, D), F32)],
        compiler_params=_params(1),
    )(dres, jac, x, g, w_out, w4, *tok_ops)


def _ffn_out(act, w_out, x):
    T, F = act.shape
    w_out, l = w_out
    D = w_out.shape[2]
    tm = _tile(T, 512)
    return _mm("ffn_out", NN, [act, w_out, x],
               [pl.BlockSpec((tm, F), lambda i, k: (i, 0)), pl.BlockSpec((None, F, D), lambda i, k: (l, 0, 0)),
                pl.BlockSpec((tm, D), lambda i, k: (i, 0))],
               [jax.ShapeDtypeStruct((T, D), F32)], [pl.BlockSpec((tm, D), lambda i, k: (i, 0))],
               (T // tm, 1), None, _residual(0.5))


def _ffn_dw_out(act, dres):
    T, F = act.shape
    D = dres.shape[1]
    tm, tk = F // 2, _tile(T, 2048)
    return _mm("ffn_dw_out", TN, [act, dres],
               [pl.BlockSpec((tk, tm), lambda i, k: (k, i)), pl.BlockSpec((tk, D), lambda i, k: (k, 0))],
               [jax.ShapeDtypeStruct((F, D), BF16)], [pl.BlockSpec((tm, D), lambda i, k: (i, 0))],
               (2, T // tk), (tm, D), _store(0.5))


def _ffn_dw_in(h, dgu):
    T, D = h.shape
    Fh = dgu.shape[2] // 2
    tk = _tile(T, 2048)
    return _mm("ffn_dw_in", TN, [h, dgu],
               [pl.BlockSpec((tk, D), lambda j, k: (k, 0)),
                pl.BlockSpec((None, tk, Fh), lambda j, k: (j // 2, k, j % 2))],
               [jax.ShapeDtypeStruct((4, D, Fh), BF16)], [pl.BlockSpec((None, D, Fh), lambda j, k: (j, 0, 0))],
               (4, T // tk), (D, Fh), _store())


def _proj(name, a, w, out_dtype, dims=NN, extra=None, scale=None):
    T, K = a.shape
    w, l = w
    N = w.shape[2] if dims == NN else w.shape[1]
    tm = _tile(T, 512)
    ops = [a, w] + ([extra] if extra is not None else [])
    specs = [pl.BlockSpec((tm, K), lambda i, k: (i, 0)), pl.BlockSpec((None,) + w.shape[1:], lambda i, k: (l, 0, 0))]
    if extra is not None:
        specs.append(pl.BlockSpec((tm, N), lambda i, k: (i, 0)))
    ep = _residual(1.0) if extra is not None else _store(scale)
    return _mm(name, dims, ops, specs, [jax.ShapeDtypeStruct((T, N), out_dtype)],
               [pl.BlockSpec((tm, N), lambda i, k: (i, 0))], (T // tm, 1), None, ep)[0]


def _mix_up(x, g, wp, widths):
    T, D = x.shape
    wp, l = wp
    n_qkv, n_rest = widths
    NP = wp.shape[2]
    tm = _tile(T, 512)

    def kern(x_ref, g_ref, w_ref, h_ref, qkv_ref, rest_ref, fl_ref):
        xv = x_ref[...]
        r = lax.rsqrt(jnp.mean(xv * xv, axis=-1, keepdims=True) + RMS_EPS)
        hv = (xv * r * g_ref[...]).astype(BF16)
        h_ref[...] = hv
        qkv_ref[...] = _dot(hv, w_ref[:, 0:n_qkv], NN).astype(BF16)
        rest_ref[...] = _dot(hv, w_ref[:, n_qkv:n_qkv + n_rest], NN)
        fl_ref[...] = _dot(hv, w_ref[:, n_qkv + n_rest:NP], NN)

    row = lambda n: pl.BlockSpec((tm, n), lambda i: (i, 0))
    return pl.pallas_call(
        kern, name="mix_up", grid=(T // tm,),
        in_specs=[row(D), pl.BlockSpec((1, D), lambda i: (0, 0)), _resident((None, D, NP), lambda i: (l, 0, 0))],
        out_specs=[row(D), row(n_qkv), row(n_rest), row(LANES)],
        out_shape=[jax.ShapeDtypeStruct((T, D), BF16), jax.ShapeDtypeStruct((T, n_qkv), BF16),
                   jax.ShapeDtypeStruct((T, n_rest), F32), jax.ShapeDtypeStruct((T, LANES), F32)],
        compiler_params=_params(1),
    )(x, g, wp)


def _column_starts(pieces):
    starts, at = [], 0
    for p in pieces:
        starts.append(at)
        at += p.shape[1]
    return starts


def _mix_in_bwd(pieces, x, g, dres, wp):
    T, D = x.shape
    wp, l = wp
    NP = wp.shape[2]
    tm = _tile(T, 512)
    n, starts = len(pieces), _column_starts(pieces)

    def kern(*refs):
        x_ref, g_ref, d_ref, w_ref, dx_ref, dg_ref = refs[n:]
        dh = jnp.zeros((tm, D), F32)
        for p_ref, at in zip(refs[:n], starts):
            dh = dh + _dot(p_ref[...].astype(BF16), w_ref[:, at:at + p_ref.shape[1]], NT)
        xv = x_ref[...]
        r = lax.rsqrt(jnp.mean(xv * xv, axis=-1, keepdims=True) + RMS_EPS)
        y = xv * r
        dy = dh * g_ref[...]
        dx_ref[...] = d_ref[...] + r * (dy - y * jnp.mean(dy * y, axis=-1, keepdims=True))
        part = jnp.sum(dh * y, axis=0, keepdims=True)

        @pl.when(pl.program_id(0) == 0)
        def _():
            dg_ref[...] = part

        @pl.when(pl.program_id(0) > 0)
        def _():
            dg_ref[...] += part

    row = lambda n: pl.BlockSpec((tm, n), lambda i: (i, 0))
    vec = pl.BlockSpec((1, D), lambda i: (0, 0))
    return pl.pallas_call(
        kern, name="mix_in_bwd", grid=(T // tm,),
        in_specs=[row(p.shape[1]) for p in pieces] + [row(D), vec, row(D), _resident((None, D, NP), lambda i: (l, 0, 0))],
        out_specs=[row(D), vec],
        out_shape=[jax.ShapeDtypeStruct((T, D), F32), jax.ShapeDtypeStruct((1, D), F32)],
        compiler_params=_params(1),
    )(*pieces, x, g, dres, wp)


def _pieces_dw(name, a, pieces, out_dtype, tk_pref):
    T, M = a.shape
    n, starts = len(pieces), _column_starts(pieces)
    N = starts[-1] + pieces[-1].shape[1]
    tk = _tile(T, tk_pref)
    nk = T // tk

    def kern(a_ref, *refs):
        o_ref, acc = refs[n], refs[n + 1]
        k = pl.program_id(0)

        @pl.when(k == 0)
        def _():
            acc[...] = jnp.zeros_like(acc)

        av = a_ref[...].astype(BF16)
        for p_ref, at in zip(refs[:n], starts):
            acc[:, at:at + p_ref.shape[1]] += _dot(av, p_ref[...].astype(BF16), TN)

        @pl.when(k == nk - 1)
        def _():
            o_ref[...] = acc[...].astype(out_dtype)

    return pl.pallas_call(
        kern, name=name, grid=(nk,),
        in_specs=[pl.BlockSpec((tk, M), lambda k: (k, 0))] + [pl.BlockSpec((tk, p.shape[1]), lambda k: (k, 0)) for p in pieces],
        out_specs=pl.BlockSpec((M, N), lambda k: (0, 0)), out_shape=jax.ShapeDtypeStruct((M, N), out_dtype),
        scratch_shapes=[pltpu.VMEM((M, N), F32)], compiler_params=_params(1),
    )(a, *pieces)


def _rows_dw(name, pieces, d, out_dtype):
    T, N = d.shape
    n, starts = len(pieces), _column_starts(pieces)
    M = starts[-1] + pieces[-1].shape[1]
    tk = _tile(T, 2048)
    nk = T // tk

    def kern(*refs):
        d_ref, o_ref, acc = refs[n], refs[n + 1], refs[n + 2]
        k = pl.program_id(0)

        @pl.when(k == 0)
        def _():
            acc[...] = jnp.zeros_like(acc)

        dv = d_ref[...].astype(BF16)
        for p_ref, at in zip(refs[:n], starts):
            acc[at:at + p_ref.shape[1], :] += _dot(p_ref[...], dv, TN)

        @pl.when(k == nk - 1)
        def _():
            o_ref[...] = acc[...].astype(out_dtype)

    return pl.pallas_call(
        kern, name=name, grid=(nk,),
        in_specs=[pl.BlockSpec((tk, p.shape[1]), lambda k: (k, 0)) for p in pieces] + [pl.BlockSpec((tk, N), lambda k: (k, 0))],
        out_specs=pl.BlockSpec((M, N), lambda k: (0, 0)), out_shape=jax.ShapeDtypeStruct((M, N), out_dtype),
        scratch_shapes=[pltpu.VMEM((M, N), F32)], compiler_params=_params(1),
    )(*pieces, d)


def _mix_out(pieces, w, x):
    T, D = x.shape
    w, l = w
    n, starts = len(pieces), _column_starts(pieces)
    tm = _tile(T, 512)

    def kern(*refs):
        w_ref, x_ref, o_ref = refs[n:]
        acc = x_ref[...]
        for p_ref, at in zip(refs[:n], starts):
            acc = acc + _dot(p_ref[...], w_ref[at:at + p_ref.shape[1], :], NN)
        o_ref[...] = acc

    row = lambda m: pl.BlockSpec((tm, m), lambda i: (i, 0))
    return pl.pallas_call(
        kern, name="mix_out", grid=(T // tm,),
        in_specs=[row(p.shape[1]) for p in pieces] + [_resident((None,) + w.shape[1:], lambda i: (l, 0, 0)), row(D)],
        out_specs=row(D), out_shape=jax.ShapeDtypeStruct((T, D), F32), compiler_params=_params(1),
    )(*pieces, w, x)


def _log_sigmoid(z):
    return jnp.minimum(z, 0.0) - jnp.log(1.0 + jnp.exp(-jnp.abs(z)))


def _decay_fwd(fl, bias):
    B, S, _ = fl.shape

    def kern(fl_ref, b_ref, o_ref):
        d = _log_sigmoid(fl_ref[...] + b_ref[...])
        row = lax.broadcasted_iota(jnp.int32, (S, LANES), 0)
        sh = 1
        while sh < S:
            d = d + jnp.where(row >= sh, pltpu.roll(d, sh, 0), 0.0)
            sh *= 2
        o_ref[...] = d.T[0:8, :]

    return pl.pallas_call(
        kern, name="decay_fwd", grid=(B,),
        in_specs=[pl.BlockSpec((None, S, LANES), lambda b: (b, 0, 0)), pl.BlockSpec((1, LANES), lambda b: (0, 0))],
        out_specs=pl.BlockSpec((None, 8, S), lambda b: (b, 0, 0)),
        out_shape=jax.ShapeDtypeStruct((B, 8, S), F32), compiler_params=_params(1),
    )(fl, bias)


def _decay_bwd(ddrow, ddcol, fl, bias, n_heads):
    B, S, _ = fl.shape

    def kern(dd_ref, ddc_ref, fl_ref, b_ref, dfl_ref, db_ref):
        dd = jnp.concatenate([dd_ref[...], jnp.zeros((LANES - 8, S), F32)], axis=0).T + ddc_ref[...]
        row = lax.broadcasted_iota(jnp.int32, (S, LANES), 0)
        lane = lax.broadcasted_iota(jnp.int32, (S, LANES), 1)
        sh = 1
        while sh < S:
            dd = dd + jnp.where(row < S - sh, pltpu.roll(dd, S - sh, 0), 0.0)
            sh *= 2
        z = fl_ref[...] + b_ref[...]
        dfl = jnp.where(lane < n_heads, dd / (1.0 + jnp.exp(z)), 0.0)
        dfl_ref[...] = dfl
        part = jnp.sum(dfl, axis=0, keepdims=True)

        @pl.when(pl.program_id(0) == 0)
        def _():
            db_ref[...] = part

        @pl.when(pl.program_id(0) > 0)
        def _():
            db_ref[...] += part

    return pl.pallas_call(
        kern, name="decay_bwd", grid=(B,),
        in_specs=[pl.BlockSpec((None, 8, S), lambda b: (b, 0, 0)), pl.BlockSpec((None, S, LANES), lambda b: (b, 0, 0)),
                  pl.BlockSpec((None, S, LANES), lambda b: (b, 0, 0)), pl.BlockSpec((1, LANES), lambda b: (0, 0))],
        out_specs=[pl.BlockSpec((None, S, LANES), lambda b: (b, 0, 0)), pl.BlockSpec((1, LANES), lambda b: (0, 0))],
        out_shape=[jax.ShapeDtypeStruct((B, S, LANES), F32), jax.ShapeDtypeStruct((1, LANES), F32)],
        compiler_params=_params(1),
    )(ddrow, ddcol, fl, bias)


def _attn_fwd(qkv, drow, n_heads, tq):
    B, S, _ = qkv.shape
    DA = n_heads * HEAD_DIM
    scale = HEAD_DIM ** -0.5

    n_pairs = n_heads // 2

    def kern(q_ref, k_ref, v_ref, dr_ref, o_ref, lse_ref):
        i = pl.program_id(1)
        lane = lax.broadcasted_iota(jnp.int32, (tq, LANES), 1)
        low = lane < HEAD_DIM
        causal = lax.broadcasted_iota(jnp.int32, (tq, tq), 1) <= lax.broadcasted_iota(jnp.int32, (tq, tq), 0)
        qms = []
        for p in range(n_pairs):
            q2 = q_ref[:, LANES * p:LANES * (p + 1)] * scale
            qms += [jnp.where(low, q2, jnp.zeros_like(q2)), jnp.where(low, jnp.zeros_like(q2), q2)]

        def step(j, carry, masked):
            ms, ls, accs = carry
            ks = pl.multiple_of(j * tq, tq)
            new_m, new_l, new_acc = [], [], []
            for p in range(n_pairs):
                cols = slice(LANES * p, LANES * (p + 1))
                k2, v2 = k_ref[pl.ds(ks, tq), cols], v_ref[pl.ds(ks, tq), cols]
                alphas, pvs = [], []
                for h in (2 * p, 2 * p + 1):
                    s = _dot(qms[h], k2, NT) - dr_ref[h, pl.ds(j, 1), :]
                    if masked:
                        s = jnp.where(causal, s, -jnp.inf)
                    m_new = jnp.maximum(ms[h], jnp.max(s, axis=1, keepdims=True))
                    alpha = jnp.exp(ms[h] - m_new)
                    pm = jnp.exp(s - m_new)
                    new_m.append(m_new)
                    new_l.append(alpha * ls[h] + jnp.sum(pm, axis=1, keepdims=True))
                    alphas.append(alpha)
                    pvs.append(_dot(pm.astype(BF16), v2, NN))
                new_acc.append(jnp.where(low, alphas[0], alphas[1]) * accs[p] + jnp.where(low, pvs[0], pvs[1]))
            return tuple(new_m), tuple(new_l), tuple(new_acc)

        init = (tuple(jnp.full((tq, 1), -jnp.inf, F32) for _ in range(n_heads)),
                tuple(jnp.zeros((tq, 1), F32) for _ in range(n_heads)),
                tuple(jnp.zeros((tq, LANES), F32) for _ in range(n_pairs)))
        ms, ls, accs = step(i, lax.fori_loop(0, i, functools.partial(step, masked=False), init), True)
        lse_mat = jnp.zeros((tq, LANES), F32)
        for p in range(n_pairs):
            l0, l1 = ls[2 * p], ls[2 * p + 1]
            o_ref[:, LANES * p:LANES * (p + 1)] = (accs[p] / jnp.where(low, l0, l1)).astype(BF16)
            lse_mat = jnp.where(lane == 2 * p, ms[2 * p] + jnp.log(l0), lse_mat)
            lse_mat = jnp.where(lane == 2 * p + 1, ms[2 * p + 1] + jnp.log(l1), lse_mat)
        lse_ref[...] = lse_mat

    nq = S // tq
    return pl.pallas_call(
        kern, name="attn_fwd", grid=(B, nq),
        in_specs=[pl.BlockSpec((None, tq, DA), lambda b, i: (b, i, 0)),
                  pl.BlockSpec((None, S, DA), lambda b, i: (b, 0, 1)),
                  pl.BlockSpec((None, S, DA), lambda b, i: (b, 0, 2)),
                  pl.BlockSpec((None, 8, nq, tq), lambda b, i: (b, 0, 0, 0))],
        out_specs=[pl.BlockSpec((None, tq, DA), lambda b, i: (b, i, 0)),
                   pl.BlockSpec((None, tq, LANES), lambda b, i: (b, i, 0))],
        out_shape=[jax.ShapeDtypeStruct((B, S, DA), BF16), jax.ShapeDtypeStruct((B, S, LANES), F32)],
        compiler_params=_params(2),
    )(qkv, qkv, qkv, drow)


def _attn_bwd(qkv, drow, o, lse, dycat, n_heads, tq):
    B, S, _ = qkv.shape
    DA = n_heads * HEAD_DIM
    scale = HEAD_DIM ** -0.5
    nq = S // tq

    n_pairs = n_heads // 2

    def kern(q_ref, k_ref, v_ref, dr_ref, o_ref, lse_ref, do_ref, dq_ref, dk_ref, dv_ref, ddr_ref, ddc_ref,
             dk_acc, dv_acc, qm_s, dom_s, delta_s, rs_s, dq_s):
        i = pl.program_id(1)

        @pl.when(i == 0)
        def _():
            dk_acc[...] = jnp.zeros_like(dk_acc)
            dv_acc[...] = jnp.zeros_like(dv_acc)
            ddr_ref[...] = jnp.zeros_like(ddr_ref)

        lane = lax.broadcasted_iota(jnp.int32, (tq, LANES), 1)
        low = lane < HEAD_DIM
        causal = lax.broadcasted_iota(jnp.int32, (tq, tq), 1) <= lax.broadcasted_iota(jnp.int32, (tq, tq), 0)
        for p in range(n_pairs):
            cols = slice(LANES * p, LANES * (p + 1))
            q2 = q_ref[:, cols] * scale
            do_f = do_ref[:, cols]
            do2 = do_f.astype(BF16)
            prod = do_f * o_ref[:, cols].astype(F32)
            qm_s[2 * p] = jnp.where(low, q2, jnp.zeros_like(q2))
            qm_s[2 * p + 1] = jnp.where(low, jnp.zeros_like(q2), q2)
            dom_s[2 * p] = jnp.where(low, do2, jnp.zeros_like(do2))
            dom_s[2 * p + 1] = jnp.where(low, jnp.zeros_like(do2), do2)
            delta_s[2 * p] = jnp.sum(jnp.where(low, prod, 0.0), axis=1, keepdims=True)
            delta_s[2 * p + 1] = jnp.sum(jnp.where(low, 0.0, prod), axis=1, keepdims=True)
            dq_s[p] = jnp.zeros((tq, LANES), F32)
        rs_s[...] = jnp.zeros(rs_s.shape, F32)

        def step(j, masked):
            ks = pl.multiple_of(j * tq, tq)
            for p in range(n_pairs):
                cols = slice(LANES * p, LANES * (p + 1))
                k2, v2 = k_ref[pl.ds(ks, tq), cols], v_ref[pl.ds(ks, tq), cols]
                dvs, dks, dqs = [], [], []
                for h in (2 * p, 2 * p + 1):
                    qm, dom = qm_s[h], dom_s[h]
                    s = _dot(qm, k2, NT) - dr_ref[h, pl.ds(j, 1), :]
                    if masked:
                        s = jnp.where(causal, s, -jnp.inf)
                    pm = jnp.exp(s - lse_ref[:, h:h + 1])
                    ds = pm * (_dot(dom, v2, NT) - delta_s[h])
                    ddr_ref[h, pl.ds(j, 1), :] -= jnp.sum(ds, axis=0, keepdims=True)
                    rs_s[h] += jnp.sum(ds, axis=1, keepdims=True)
                    dsb = ds.astype(BF16)
                    dvs.append(_dot(pm.astype(BF16), dom, TN))
                    dks.append(_dot(dsb, qm, TN))
                    dqs.append(_dot(dsb, k2, NN))
                dv_acc[pl.ds(ks, tq), cols] += dvs[0] + dvs[1]
                dk_acc[pl.ds(ks, tq), cols] += dks[0] + dks[1]
                dq_s[p] += jnp.where(low, dqs[0], dqs[1])

        def body(j, carry):
            step(j, False)
            return carry

        lax.fori_loop(0, i, body, 0)
        step(i, True)
        ddc = jnp.zeros((tq, LANES), F32)
        for p in range(n_pairs):
            dq_ref[:, LANES * p:LANES * (p + 1)] = (dq_s[p] * scale).astype(BF16)
            ddc = jnp.where(lane == 2 * p, rs_s[2 * p], ddc)
            ddc = jnp.where(lane == 2 * p + 1, rs_s[2 * p + 1], ddc)
        ddc_ref[...] = ddc

        @pl.when(i == nq - 1)
        def _():
            dk_ref[...] = dk_acc[...].astype(BF16)
            dv_ref[...] = dv_acc[...].astype(BF16)

    tile = pl.BlockSpec((None, tq, DA), lambda b, i: (b, i, 0))
    seq = pl.BlockSpec((None, S, DA), lambda b, i: (b, 0, 0))
    dec = pl.BlockSpec((None, 8, nq, tq), lambda b, i: (b, 0, 0, 0))
    return pl.pallas_call(
        kern, name="attn_bwd", grid=(B, nq),
        in_specs=[tile, pl.BlockSpec((None, S, DA), lambda b, i: (b, 0, 1)),
                  pl.BlockSpec((None, S, DA), lambda b, i: (b, 0, 2)), dec, tile,
                  pl.BlockSpec((None, tq, LANES), lambda b, i: (b, i, 0)), tile],
        out_specs=[tile, seq, seq, dec, pl.BlockSpec((None, tq, LANES), lambda b, i: (b, i, 0))],
        out_shape=[jax.ShapeDtypeStruct((B, S, DA), BF16)] * 3 + [jax.ShapeDtypeStruct((B, 8, nq, tq), F32),
                                                                  jax.ShapeDtypeStruct((B, S, LANES), F32)],
        scratch_shapes=[pltpu.VMEM((S, DA), F32), pltpu.VMEM((S, DA), F32),
                        pltpu.VMEM((n_heads, tq, LANES), BF16), pltpu.VMEM((n_heads, tq, LANES), BF16),
                        pltpu.VMEM((n_heads, tq, 1), F32), pltpu.VMEM((n_heads, tq, 1), F32),
                        pltpu.VMEM((n_pairs, tq, LANES), F32)],
        compiler_params=_params(2),
    )(qkv, qkv, qkv, drow, o, lse, dycat)


def _down(v, d, row):
    return jnp.where(row >= d, pltpu.roll(v, d, 0), 0.0)


def _up(v, d, row, S):
    return jnp.where(row < S - d, pltpu.roll(v, S - d, 0), 0.0)


def _window(v, shift, group):
    sums, acc, d = [], v, 1
    for _ in POOL_WINDOWS:
        acc = acc + shift(acc, d)
        sums.append(acc)
        d *= 2
    out = sums[-1]
    for gi in range(len(POOL_WINDOWS) - 2, -1, -1):
        out = jnp.where(group == gi, sums[gi], out)
    return out


def _pool_count(row, group):
    w = jnp.full(row.shape, POOL_WINDOWS[-1], jnp.int32)
    for gi in range(len(POOL_WINDOWS) - 2, -1, -1):
        w = jnp.where(group == gi, POOL_WINDOWS[gi], w)
    return jnp.minimum(row + 1, w).astype(F32)


def _mix_local_fwd(rest, wbd, ps, cw):
    B, S, C4 = rest.shape
    C = C4 // 4
    gw = C // len(POOL_WINDOWS)

    def kern(r_ref, w_ref, ps_ref, cw_ref, y_ref, pooled_ref):
        row = lax.broadcasted_iota(jnp.int32, (S, C), 0)
        group = lax.broadcasted_iota(jnp.int32, (S, C), 1) // gw
        u = r_ref[:, 0:C]
        pooled = _window(u, lambda v, d: _down(v, d, row), group) / _pool_count(row, group) - u
        pb = pooled.astype(BF16)
        pooled_ref[...] = pb
        y_ref[:, 0:C] = (_dot(pb, w_ref[...], NN) * ps_ref[...]).astype(BF16)
        uc = r_ref[:, 2 * C:3 * C] * r_ref[:, 3 * C:4 * C]
        y = cw_ref[0:1, :] * _down(uc, 2, row) + cw_ref[1:2, :] * _down(uc, 1, row) + cw_ref[2:3, :] * uc
        y_ref[:, C:2 * C] = (r_ref[:, C:2 * C] * y).astype(BF16)

    return pl.pallas_call(
        kern, name="mix_local_fwd", grid=(B,),
        in_specs=[pl.BlockSpec((None, S, C4), lambda b: (b, 0, 0)), pl.BlockSpec((C, C), lambda b: (0, 0)),
                  pl.BlockSpec((1, C), lambda b: (0, 0)), pl.BlockSpec((8, C), lambda b: (0, 0))],
        out_specs=[pl.BlockSpec((None, S, 2 * C), lambda b: (b, 0, 0)), pl.BlockSpec((None, S, C), lambda b: (b, 0, 0))],
        out_shape=[jax.ShapeDtypeStruct((B, S, 2 * C), BF16), jax.ShapeDtypeStruct((B, S, C), BF16)],
        compiler_params=_params(1),
    )(rest, wbd, ps, cw)


def _mix_local_bwd(rest, pooled, dycat, wbd, ps, cw):
    B, S, C4 = rest.shape
    C = C4 // 4
    gw = C // len(POOL_WINDOWS)

    def kern(r_ref, pooled_ref, d_ref, w_ref, ps_ref, cw_ref, dr_ref, dw_ref, dps_ref, dcw_ref):
        row = lax.broadcasted_iota(jnp.int32, (S, C), 0)
        group = lax.broadcasted_iota(jnp.int32, (S, C), 1) // gw
        dyp = d_ref[:, 0:C]
        dyc = d_ref[:, C:2 * C]
        pb = pooled_ref[...]
        dps = jnp.sum(dyp * _dot(pb, w_ref[...], NN), axis=0, keepdims=True)
        dzb = (dyp * ps_ref[...]).astype(BF16)
        dw = _dot(pb, dzb, TN)
        dpooled = _dot(dzb, w_ref[...], NT)
        g = dpooled / _pool_count(row, group)
        dr_ref[:, 0:C] = (_window(g, lambda v, d: _up(v, d, row, S), group) - dpooled).astype(BF16)
        cc, ch = r_ref[:, 2 * C:3 * C], r_ref[:, 3 * C:4 * C]
        uc = cc * ch
        u1, u2 = _down(uc, 1, row), _down(uc, 2, row)
        y = cw_ref[0:1, :] * u2 + cw_ref[1:2, :] * u1 + cw_ref[2:3, :] * uc
        dr_ref[:, C:2 * C] = (dyc * y).astype(BF16)
        dy = dyc * r_ref[:, C:2 * C]
        duc = cw_ref[0:1, :] * _up(dy, 2, row, S) + cw_ref[1:2, :] * _up(dy, 1, row, S) + cw_ref[2:3, :] * dy
        dr_ref[:, 2 * C:3 * C] = (duc * ch).astype(BF16)
        dr_ref[:, 3 * C:4 * C] = (duc * cc).astype(BF16)
        dcw = jnp.concatenate([jnp.sum(dy * u2, axis=0, keepdims=True), jnp.sum(dy * u1, axis=0, keepdims=True),
                               jnp.sum(dy * uc, axis=0, keepdims=True), jnp.zeros((5, C), F32)], axis=0)

        @pl.when(pl.program_id(0) == 0)
        def _():
            dw_ref[...] = dw
            dps_ref[...] = dps
            dcw_ref[...] = dcw

        @pl.when(pl.program_id(0) > 0)
        def _():
            dw_ref[...] += dw
            dps_ref[...] += dps
            dcw_ref[...] += dcw

    full = lambda shape: pl.BlockSpec(shape, lambda b: (0, 0))
    return pl.pallas_call(
        kern, name="mix_local_bwd", grid=(B,),
        in_specs=[pl.BlockSpec((None, S, C4), lambda b: (b, 0, 0)), pl.BlockSpec((None, S, C), lambda b: (b, 0, 0)),
                  pl.BlockSpec((None, S, 2 * C), lambda b: (b, 0, 1)), full((C, C)), full((1, C)), full((8, C))],
        out_specs=[pl.BlockSpec((None, S, C4), lambda b: (b, 0, 0)), full((C, C)), full((1, C)), full((8, C))],
        out_shape=[jax.ShapeDtypeStruct((B, S, C4), BF16), jax.ShapeDtypeStruct((C, C), F32),
                   jax.ShapeDtypeStruct((1, C), F32), jax.ShapeDtypeStruct((8, C), F32)],
        compiler_params=_params(1),
    )(rest, pooled, dycat, wbd, ps, cw)


def _adamw(w, gs, m, v, token=None):
    R, C = w.shape
    pieces = [p if isinstance(p, tuple) else (p,) for p in gs]
    owner = [s for s, p in enumerate(pieces) for _ in p]
    flat = [a for p in pieces for a in p]
    n = len(flat)
    rows = R // len(pieces)
    tr = _tile(rows, 256)
    per = rows // tr
    tok_ops, tok_specs = _token_operand(token)

    def kern(w_ref, *refs):
        g_refs, (m_ref, v_ref), (g_out, d_ref, nm_ref, nv_ref) = refs[:n], refs[n:n + 2], refs[n + 2 + len(tok_ops):]
        vals, at = [], 0
        for p in pieces:
            vals.append(g_refs[at][...] if len(p) == 1 else g_refs[at][...] + g_refs[at + 1][...])
            at += len(p)
        gv = vals[0]
        for s in range(1, len(pieces)):
            gv = jnp.where(pl.program_id(0) // per == s, vals[s], gv)
        nm = ADAM_B1 * m_ref[...] + (1.0 - ADAM_B1) * gv
        nv = ADAM_B2 * v_ref[...] + (1.0 - ADAM_B2) * (gv * gv)
        m_hat = nm / (1.0 - ADAM_B1 ** ADAM_STEP)
        v_hat = nv / (1.0 - ADAM_B2 ** ADAM_STEP)
        g_out[...] = gv
        d_ref[...] = -ADAM_LR * (m_hat / (jnp.sqrt(v_hat) + ADAM_EPS) + ADAM_WD * w_ref[...])
        nm_ref[...] = nm
        nv_ref[...] = nv

    def piece(s):
        return pl.BlockSpec((tr, C), lambda i: (jnp.clip(i - s * per, 0, per - 1), 0))

    blk = pl.BlockSpec((tr, C), lambda i: (i, 0))
    return pl.pallas_call(
        kern, name="adamw", grid=(R // tr,), in_specs=[blk] + [piece(s) for s in owner] + [blk] * 2 + tok_specs,
        out_specs=[blk] * 4, out_shape=[jax.ShapeDtypeStruct((R, C), F32)] * 4, compiler_params=_params(1),
    )(w, *flat, m, v, *tok_ops)


def _place():
    x, y, c = lax.axis_index("x"), lax.axis_index("y"), lax.axis_index("c")
    return x, y, c, [(1 - x, y), (x, 1 - y), (1 - x, 1 - y)]


def _comm_call(name, body, operands, out_shape, n_sems, aliases=None):
    any_spec = pl.BlockSpec(memory_space=pl.ANY)
    return pl.pallas_call(
        body, name=name, in_specs=[any_spec] * len(operands), out_specs=[any_spec] * len(out_shape),
        out_shape=out_shape, input_output_aliases=aliases or {},
        scratch_shapes=[pltpu.SemaphoreType.DMA((n,)) for n in n_sems],
    )(*operands)


def _my_block():
    return 2 * lax.axis_index("x") + lax.axis_index("y")


def _place_shard(w, dtype, first=0, count=None):
    L, R, C = w.shape
    count = L if count is None else count
    tr = _tile(R, 512)

    def kern(w_ref, o_ref):
        o_ref[...] = w_ref[...].astype(dtype)

    return pl.pallas_call(
        kern, name="place_shard", grid=(count, R // tr),
        in_specs=[pl.BlockSpec((None, tr, C), lambda l, i: (first + l, i, 0))],
        out_specs=pl.BlockSpec((None, None, tr, C), lambda l, i: (l, _my_block(), i, 0)),
        out_shape=jax.ShapeDtypeStruct((count, N_CHIPS, R, C), dtype), compiler_params=_params(2),
    )(w)


HALF_ROWS = 16


def _rows(ref, half):
    hr = ref.shape[-2] // 2
    return ref.at[(slice(None),) * (len(ref.shape) - 2) + (pl.ds(half * hr, hr),)]


def _all_gather(bufs):
    n = len(bufs)

    def body(*refs):
        outs = refs[n:2 * n]
        send_sems, recv_sems = refs[2 * n:]
        x, y, c, chips = _place()
        sibling = (x, y, 1 - c)

        def remote(k, j, chip, half, to):
            blk = 2 * chip[0] + chip[1]
            if outs[k].shape[2] % (2 * HALF_ROWS) == 0:
                region = _rows(outs[k].at[:, blk], half)
            else:
                hl = outs[k].shape[0] // 2
                region = outs[k].at[pl.ds(half * hl, hl), blk]
            return pltpu.make_async_remote_copy(
                src_ref=region, dst_ref=region, send_sem=send_sems.at[6 * k + j],
                recv_sem=recv_sems.at[6 * k + j], device_id=to, device_id_type=MESH)

        first = [remote(k, j, (x, y), c, (*chip, c)) for k in range(n) for j, chip in enumerate(chips)]
        for cp in first:
            cp.start()
        passed = []
        for k in range(n):
            for j, chip in enumerate(chips):
                remote(k, j, chip, c, (x, y, c)).wait_recv()
                passed.append(remote(k, 3 + j, chip, c, sibling))
                passed[-1].start()
        for k in range(n):
            for j, chip in enumerate(chips):
                remote(k, 3 + j, chip, 1 - c, (x, y, c)).wait_recv()
        for cp in first + passed:
            cp.wait_send()

    out_shape = [jax.ShapeDtypeStruct(s.shape, s.dtype) for s in bufs]
    return _comm_call("all_gather_weights", body, bufs, out_shape, (6 * n, 6 * n), aliases={k: k for k in range(n)})


_HBM = pl.BlockSpec(memory_space=pltpu.HBM)
_SEM = pl.BlockSpec(memory_space=pltpu.SEMAPHORE)
_ANY = pl.BlockSpec(memory_space=pl.ANY)


def _split_start(name, bufs, n_copies, make_copies, after):
    n = len(bufs)

    def body(*refs):
        send_sems, recv_sems, token = refs[n + 1], refs[n + 2], refs[2 * n + 3]
        for cp in make_copies(refs[:n], send_sems, recv_sems):
            cp.start()
        token[...] = jnp.zeros_like(token)

    res = pl.pallas_call(
        body, name=name, in_specs=[_HBM] * n + [_ANY],
        out_shape=(pltpu.SemaphoreType.DMA((n_copies,)), pltpu.SemaphoreType.DMA((n_copies,)),
                   *[pltpu.HBM(b.shape, b.dtype) for b in bufs], jax.ShapeDtypeStruct((8, LANES), F32)),
        out_specs=(_SEM, _SEM, *[_HBM] * n, pl.BlockSpec(memory_space=pltpu.VMEM)),
        input_output_aliases={i: 2 + i for i in range(n)},
        compiler_params=pltpu.CompilerParams(has_side_effects=pltpu.SideEffectType.DATAFLOW_SIDE_EFFECTING),
    )(*[pltpu.with_memory_space_constraint(b, pltpu.HBM) for b in bufs], after)
    return res[0], res[1], list(res[2:2 + n]), res[2 + n]


def _split_wait(name, send_sems, recv_sems, bufs, make_copies, after):
    n = len(bufs)

    def body(*refs):
        for cp in make_copies(refs[:n], refs[n], refs[n + 1]):
            cp.wait_send()
            cp.wait_recv()

    return list(pl.pallas_call(
        body, name=name, in_specs=[_HBM] * n + [_SEM, _SEM, _ANY],
        out_shape=tuple(pltpu.HBM(b.shape, b.dtype) for b in bufs), out_specs=tuple([_HBM] * n),
        input_output_aliases={i: i for i in range(n)},
        compiler_params=pltpu.CompilerParams(has_side_effects=pltpu.SideEffectType.DATAFLOW_SIDE_EFFECTING),
    )(*bufs, send_sems, recv_sems, after))


def _gather_copies(refs, send_sems, recv_sems):
    x, y, c, chips = _place()
    return [pltpu.make_async_remote_copy(
        src_ref=ref.at[:, 2 * x + y], dst_ref=ref.at[:, 2 * x + y], send_sem=send_sems.at[3 * k + j],
        recv_sem=recv_sems.at[3 * k + j], device_id=(*chip, c), device_id_type=MESH)
        for k, ref in enumerate(refs) for j, chip in enumerate(chips)]


def _exchange_copies(refs, send_sems, recv_sems):
    n = len(refs) // 2
    x, y, c, chips = _place()
    return [pltpu.make_async_remote_copy(
        src_ref=refs[k].at[:, 2 * chip[0] + chip[1]], dst_ref=refs[n + k].at[j], send_sem=send_sems.at[3 * k + j],
        recv_sem=recv_sems.at[3 * k + j], device_id=(*chip, c), device_id_type=MESH)
        for k in range(n) for j, chip in enumerate(chips)]


def _all_reduce_small(v, after):
    n = v.shape[0]

    def body(v_ref, after_ref, o_ref, gbuf, send_sems, recv_sems):
        x, y, c, _ = _place()
        me = 4 * x + 2 * y + c
        gbuf[me] = v_ref[...]
        copies, waits = [], []
        for r in range(1, N_DEV):
            px = 1 - x if r & 4 else x
            py = 1 - y if r & 2 else y
            pc = 1 - c if r & 1 else c
            mk = functools.partial(pltpu.make_async_remote_copy, src_ref=v_ref, send_sem=send_sems.at[r - 1],
                                   recv_sem=recv_sems.at[r - 1], device_id=(px, py, pc), device_id_type=MESH)
            copies.append(mk(dst_ref=gbuf.at[me]))
            waits.append(mk(dst_ref=gbuf.at[4 * px + 2 * py + pc]))
        for cp in copies:
            cp.start()
        for cp in waits:
            cp.wait_recv()
        for cp in copies:
            cp.wait_send()
        acc = gbuf[0]
        for d in range(1, N_DEV):
            acc = acc + gbuf[d]
        o_ref[...] = acc

    vm = pl.BlockSpec(memory_space=pltpu.VMEM)
    return pl.pallas_call(
        body, name="all_reduce_small", in_specs=[vm, _ANY], out_specs=vm, out_shape=jax.ShapeDtypeStruct(v.shape, F32),
        scratch_shapes=[pltpu.VMEM((N_DEV, n, LANES), F32), pltpu.SemaphoreType.DMA((N_DEV - 1,)),
                        pltpu.SemaphoreType.DMA((N_DEV - 1,))],
        compiler_params=pltpu.CompilerParams(vmem_limit_bytes=VMEM_LIMIT),
    )(v, after)


def _add_blocks(p, h2):
    L, nb, hr, C = p.shape
    tr = _tile(hr, 512)

    def kern(p_ref, h0_ref, h1_ref, h2_ref, o_ref):
        o_ref[...] = ((p_ref[...].astype(F32) + h0_ref[...].astype(F32)) + h1_ref[...].astype(F32)) + h2_ref[...].astype(F32)

    def other(j):
        return pl.BlockSpec((None, None, tr, C), lambda l, i: (j, l, i, 0))

    return pl.pallas_call(
        kern, name="rs_add_blocks", grid=(L, hr // tr),
        in_specs=[pl.BlockSpec((None, None, tr, C), lambda l, i: (l, _my_block(), i, 0)), other(0), other(1), other(2)],
        out_specs=pl.BlockSpec((None, tr, C), lambda l, i: (l, i, 0)),
        out_shape=jax.ShapeDtypeStruct((L, hr, C), F32), compiler_params=_params(2),
    )(p, h2, h2, h2)


WEIGHTS = ("norm_ffn1", "w_ffn1_in", "w_ffn1_out", "norm_mix", "w_mix_in", "b_forget", "w_pool", "pool_scale",
           "conv_w", "w_mix_out", "norm_ffn2", "w_ffn2_in", "w_ffn2_out", "norm_final")
BIG = ("w_ffn1_in", "w_ffn1_out", "w_mix_in", "w_mix_out", "w_ffn2_in", "w_ffn2_out")
SMALL = ("norm_ffn1", "norm_mix", "b_forget", "w_pool", "pool_scale", "conv_w", "norm_ffn2", "norm_final")


def _layer_params(small, gathered, conv_w, D, l):
    DA, C, H = D // 2, D // 4, D // 2 // HEAD_DIM
    P = {}
    if "w_ffn1_in" in gathered:
        P.update(g1=small["norm_ffn1"][l][None], w1in=(gathered["w_ffn1_in"], 0),
                 w1out=(gathered["w_ffn1_out"].reshape(1, -1, D), 0))
    if "w_ffn2_in" in gathered:
        P.update(g2=small["norm_ffn2"][l][None], w2in=(gathered["w_ffn2_in"], 0),
                 w2out=(gathered["w_ffn2_out"].reshape(1, -1, D), 0))
    if "w_mix_in" in gathered:
        w_in = jnp.concatenate([gathered["w_mix_in"][:, b] for b in range(N_CHIPS)], axis=2)
        wqkv, wrest = w_in[:, :, :3 * DA], w_in[:, :, 3 * DA + H:]
        wf = jnp.pad(w_in[:, :, 3 * DA:3 * DA + H], ((0, 0), (0, 0), (0, LANES - H)))
        ng = len(POOL_WINDOWS)
        same_group = jnp.eye(ng, dtype=bool)[:, None, :, None]
        wbd = jnp.where(same_group, small["w_pool"][l][:, :, None, :], 0.0).reshape(C, C)
        cw = jnp.concatenate([conv_w[l, b] for b in range(N_CHIPS)], axis=1)
        P.update(gm=small["norm_mix"][l][None], wp=(jnp.concatenate([wqkv, wrest, wf], axis=2), 0),
                 wmixout=(gathered["w_mix_out"].reshape(1, D, D), 0),
                 bias=jnp.pad(small["b_forget"][l][None], ((0, 0), (0, LANES - H))), wbd=wbd.astype(BF16),
                 ps=small["pool_scale"][l][None], cw=jnp.pad(cw, ((0, 8 - CONV_WIDTH), (0, 0))))
    return P


def _ffn_fwd(x, g, w_in, w_out, token=None):
    h, jac, act = _ffn_up(x, g, w_in, token)
    return _ffn_out(act, w_out, x)[0], (x, h, jac, act)


def _ffn_bwd(dres, saved, g, w_in, w_out, token=None):
    x, h, jac, act = saved
    dgu, dx, dg = _ffn_bwd_main(dres, jac, x, g, w_out, w_in, token)
    dw_out = _ffn_dw_out(act, dres)[0]
    dw_in = _ffn_dw_in(h, dgu)[0]
    return dx, dg, dw_in, dw_out.reshape(N_CHIPS, -1, dw_out.shape[1])


def _mixer_fwd(x, P, B, S, tq):
    T, D = x.shape
    DA, C, H = D // 2, D // 4, D // 2 // HEAD_DIM
    hn, qkv, rest, fl = _mix_up(x, P["gm"], P["wp"], (3 * DA, 4 * C))
    qkv, rest, fl = qkv.reshape(B, S, 3 * DA), rest.reshape(B, S, 4 * C), fl.reshape(B, S, LANES)
    drow = _decay_fwd(fl, P["bias"]).reshape(B, 8, S // tq, tq)
    o, lse = _attn_fwd(qkv, drow, H, tq)
    ypc, pooled = _mix_local_fwd(rest, P["wbd"], P["ps"], P["cw"])
    x_out = _mix_out([o.reshape(T, DA), ypc.reshape(T, 2 * C)], P["wmixout"], x)
    return x_out, (x, hn, qkv, rest, fl, drow, o, lse, pooled, ypc)


def _mixer_bwd(dres, saved, P, B, S, tq):
    x, hn, qkv, rest, fl, drow, o, lse, pooled, ypc = saved
    T, D = x.shape
    DA, C, H = D // 2, D // 4, D // 2 // HEAD_DIM
    dycat = _proj("mix_out_bwd", dres, P["wmixout"], F32, NT).reshape(B, S, D)
    dw_out = _rows_dw("mix_out_dw", [o.reshape(T, DA), ypc.reshape(T, 2 * C)], dres, BF16)
    dq, dk, dv, ddrow, ddcol = _attn_bwd(qkv, drow, o, lse, dycat, H, tq)
    dfl, dbias = _decay_bwd(ddrow.reshape(B, 8, S), ddcol, fl, P["bias"], H)
    drest, dwbd, dps, dcw = _mix_local_bwd(rest, pooled, dycat, P["wbd"], P["ps"], P["cw"])
    pieces = [a.reshape(T, a.shape[-1]) for a in (dq, dk, dv, drest, dfl)]
    dwp = _pieces_dw("mix_in_dw", hn, pieces, BF16, 1024)
    dx, dg = _mix_in_bwd(pieces, x, P["gm"], dres, P["wp"])
    n_q, n_r = 3 * DA, 4 * C
    dw_in = jnp.concatenate([dwp[:, :n_q], dwp[:, n_q + n_r:n_q + n_r + H], dwp[:, n_q:n_q + n_r]], axis=1)
    dw_in = dw_in.reshape(D, N_CHIPS, -1).transpose(1, 0, 2)
    ng = len(POOL_WINDOWS)
    same_group = jnp.eye(ng, dtype=bool)[:, None, :, None]
    dw_pool = jnp.where(same_group, dwbd.reshape(ng, C // ng, ng, C // ng), 0.0).sum(axis=2)
    small = dict(norm_mix=dg[0], b_forget=dbias[0, :H], w_pool=dw_pool, pool_scale=dps[0], conv_w=dcw[:CONV_WIDTH])
    return dx, small, dw_in, dw_out.reshape(N_CHIPS, -1, D)


FFN1, MIX, FFN2 = BIG[:2], BIG[2:4], BIG[4:]


def _local_step(x, target, small, conv_w, pipe):
    B, S, D = x.shape
    L = small["norm_ffn1"].shape[0]
    tq = _tile(S, 512)
    xt = x.reshape(B * S, D)
    saved, params = [], []
    for l in range(L):
        P = _layer_params(small, pipe.weights(l, xt), conv_w, D, l)
        xt, s1 = _ffn_fwd(xt, P["g1"], P["w1in"], P["w1out"], pipe.token(l))
        P.update(_layer_params(small, pipe.weights_mix(l, xt), conv_w, D, l))
        xt, s2 = _mixer_fwd(xt, P, B, S, tq)
        P.update(_layer_params(small, pipe.weights_ffn2(l, xt), conv_w, D, l))
        xt, s3 = _ffn_fwd(xt, P["g2"], P["w2in"], P["w2out"])
        saved.append((s1, s2, s3))
        params.append(P)
    dres, dgf, loss = _final_loss(xt, small["norm_final"][None], target.reshape(B * S, D))
    sm = {k: [None] * L for k in SMALL if k != "norm_final"}
    token = None
    for l in reversed(range(L)):
        P, (s1, s2, s3) = params[l], saved[l]
        big = {}
        dres, dg2, big["w_ffn2_in"], big["w_ffn2_out"] = _ffn_bwd(dres, s3, P["g2"], P["w2in"], P["w2out"], token)
        dres, smix, big["w_mix_in"], big["w_mix_out"] = _mixer_bwd(dres, s2, P, B, S, tq)
        big = {k: val[None] for k, val in big.items()}
        token = pipe.grads(l, FFN2 + MIX, big, dres) if l == 0 else None
        dres, dg1, dw_in, dw_out = _ffn_bwd(dres, s1, P["g1"], P["w1in"], P["w1out"], token)
        big.update(w_ffn1_in=dw_in[None], w_ffn1_out=dw_out[None])
        sm["norm_ffn1"][l], sm["norm_ffn2"][l] = dg1[0], dg2[0]
        for k, val in smix.items():
            sm[k][l] = val
        token = pipe.grads(l, FFN1 if l == 0 else BIG, big, big["w_ffn1_in"])
    sm = {k: jnp.stack(val) for k, val in sm.items()}
    sm["norm_final"] = dgf[0]
    return loss[0, 0], dres.reshape(B, S, D), sm


def _sibling_copies(refs, send_sems, recv_sems):
    n = len(refs) // 2
    x, y, c, _ = _place()
    return [pltpu.make_async_remote_copy(
        src_ref=refs[k], dst_ref=refs[n + k], send_sem=send_sems.at[k], recv_sem=recv_sems.at[k],
        device_id=(x, y, 1 - c), device_id_type=MESH) for k in range(n)]


class _Pipeline:
    def __init__(self, w):
        self.w, self.n_layers = w, w[BIG[0]].shape[0]
        first = _all_gather([_place_shard(w[k], BF16, 0, 1) for k in FFN1] + [_place_shard(w["conv_w"], F32)])
        self.conv_w = first[-1]
        self._ready = dict(zip(FFN1, first[:-1]))
        self._mix = self._start_gather("0_mix", MIX, 0, first[0])
        self._ffn2 = self._start_gather("0_ffn2", FFN2, 0, self._mix[1][3])
        self._next = (1, self._start_gather("1", BIG, 1, self._ffn2[1][3]))
        self._reduce, self._swaps = None, []
        self.reduced = [dict() for _ in range(self.n_layers)]

    def _start_gather(self, tag, kinds, l, after):
        placed = [_place_shard(self.w[k], BF16, l, 1) for k in kinds]
        return kinds, _split_start(f"gather_start_{tag}", placed, 3 * len(kinds), _gather_copies, after)

    def _wait_gather(self, tag, started, after):
        kinds, (send_sems, recv_sems, bufs, _) = started
        return dict(zip(kinds, _split_wait(f"gather_wait_{tag}", send_sems, recv_sems, bufs, _gather_copies, after)))

    def token(self, l):
        return self._next[1][1][3] if self._next is not None and self._next[0] == l + 1 else None

    def weights(self, l, after):
        if l == 0:
            return self._ready
        got = self._wait_gather(str(l), self._next[1], after)
        self._next = (l + 1, self._start_gather(str(l + 1), BIG, l + 1, got[BIG[0]])) if l + 1 < self.n_layers else None
        return got

    def weights_mix(self, l, after):
        return self._wait_gather("0_mix", self._mix, after) if l == 0 else {}

    def weights_ffn2(self, l, after):
        return self._wait_gather("0_ffn2", self._ffn2, after) if l == 0 else {}

    def _finish_reduce(self, after):
        if self._reduce is None:
            return None
        tag, l, kinds, (send_sems, recv_sems, bufs, _) = self._reduce
        n = len(kinds)
        bufs = _split_wait(f"reduce_wait_{tag}", send_sems, recv_sems, bufs, _exchange_copies, after)
        mine = [_add_blocks(p, o) for p, o in zip(bufs[:n], bufs[n:])]
        lands = [lax.empty(q.shape, q.dtype) for q in mine]
        self._swaps.append((tag, l, kinds, _split_start(f"swap_start_{tag}", mine + lands, n, _sibling_copies, mine[0])))
        self._reduce = None
        return self._swaps[-1][3][3]

    def grads(self, l, kinds, big, after):
        swap_token = self._finish_reduce(after)
        grads = [big[k] for k in kinds]
        tag = str(l) if len(kinds) == len(BIG) else f"{l}_{kinds[0][2:]}"
        lands = [lax.empty((3, g.shape[0]) + g.shape[2:], g.dtype) for g in grads]
        started = _split_start(f"reduce_start_{tag}", grads + lands, 3 * len(kinds), _exchange_copies,
                               grads[0] if swap_token is None else swap_token)
        self._reduce = (tag, l, kinds, started)
        self.last_token = started[3]
        return started[3]

    def finish(self, after, last=False):
        if last:
            self._finish_reduce(after)
        for tag, l, kinds, (send_sems, recv_sems, bufs, _) in self._swaps:
            n = len(kinds)
            bufs = _split_wait(f"swap_wait_{tag}", send_sems, recv_sems, bufs, _sibling_copies, after)
            self.reduced[l].update(zip(kinds, zip(bufs[:n], bufs[n:])))
        self._swaps = []
        return self.reduced


def _pack(parts, extra=()):
    flat = jnp.concatenate([p.reshape(-1) for p in parts] + [jnp.reshape(e, (1,)) for e in extra])
    n = -(-flat.shape[0] // (8 * LANES)) * 8
    return jnp.pad(flat, (0, n * LANES - flat.shape[0])).reshape(n, LANES)


def _unpack(buf, shapes):
    flat, out, at = buf.reshape(-1), [], 0
    for s in shapes:
        n = math.prod(s)
        out.append(flat[at:at + n].reshape(s))
        at += n
    return out, flat[at:]


def kernel(x, norm_ffn1, w_ffn1_in, w_ffn1_out, norm_mix, w_mix_in, b_forget, w_pool, pool_scale, conv_w, w_mix_out, norm_ffn2, w_ffn2_in, w_ffn2_out, norm_final, loss_target, m_norm_ffn1, m_w_ffn1_in, m_w_ffn1_out, m_norm_mix, m_w_mix_in, m_b_forget, m_w_pool, m_pool_scale, m_conv_w, m_w_mix_out, m_norm_ffn2, m_w_ffn2_in, m_w_ffn2_out, m_norm_final, v_norm_ffn1, v_w_ffn1_in, v_w_ffn1_out, v_norm_mix, v_w_mix_in, v_b_forget, v_w_pool, v_pool_scale, v_conv_w, v_w_mix_out, v_norm_ffn2, v_w_ffn2_in, v_w_ffn2_out, v_norm_final):
    w = dict(zip(WEIGHTS, (norm_ffn1, w_ffn1_in, w_ffn1_out, norm_mix, w_mix_in, b_forget, w_pool, pool_scale, conv_w, w_mix_out, norm_ffn2, w_ffn2_in, w_ffn2_out, norm_final)))
    m = dict(zip(WEIGHTS, (m_norm_ffn1, m_w_ffn1_in, m_w_ffn1_out, m_norm_mix, m_w_mix_in, m_b_forget, m_w_pool, m_pool_scale, m_conv_w, m_w_mix_out, m_norm_ffn2, m_w_ffn2_in, m_w_ffn2_out, m_norm_final)))
    v = dict(zip(WEIGHTS, (v_norm_ffn1, v_w_ffn1_in, v_w_ffn1_out, v_norm_mix, v_w_mix_in, v_b_forget, v_w_pool, v_pool_scale, v_conv_w, v_w_mix_out, v_norm_ffn2, v_w_ffn2_in, v_w_ffn2_out, v_norm_final)))
    block = 2 * lax.axis_index("x") + lax.axis_index("y")

    pipe = _Pipeline(w)
    small = {k: w[k] for k in SMALL}
    loss, grad_x, sm = _local_step(x, loss_target, small, pipe.conv_w, pipe)
    grads, delta, new_m, new_v = {}, {}, {}, {}

    def big_adamw(k, reduced, token=None):
        two_d = lambda a: a.reshape(-1, a.shape[-1])
        pieces = [tuple(map(two_d, layer[k])) for layer in reduced]
        res = _adamw(two_d(w[k]), pieces, two_d(m[k]), two_d(v[k]), token)
        grads[k], delta[k], new_m[k], new_v[k] = [r.reshape(w[k].shape) for r in res]

    reduced = pipe.finish(grad_x)
    for k in BIG[2:]:
        big_adamw(k, reduced, pipe.last_token)
    reduced = pipe.finish(new_v[BIG[-1]], last=True)
    for k in BIG[:2]:
        big_adamw(k, reduced)

    order = list(SMALL)
    total = _all_reduce_small(_pack([sm[k] for k in order], extra=(loss,)), new_v[BIG[0]])
    parts, rest = _unpack(total, [sm[k].shape for k in order])
    grads.update(zip(order, parts))
    loss = rest[0]
    cs = conv_w.shape[2]
    grads["conv_w"] = lax.dynamic_slice_in_dim(grads["conv_w"], block * cs, cs, axis=2)
    packed = [_pack([t[k] for k in order]) for t in (w, grads, m, v)]
    _, d, nm, nv = _adamw(packed[0], [packed[1]], packed[2], packed[3])
    shapes = [w[k].shape for k in order]
    for res, flat in ((delta, d), (new_m, nm), (new_v, nv)):
        res.update(zip(order, _unpack(flat, shapes)[0]))
    return (loss, grad_x, *[grads[k] for k in WEIGHTS], *[delta[k] for k in WEIGHTS],
            *[new_m[k] for k in WEIGHTS], *[new_v[k] for k in WEIGHTS])
```

```python
import functools
import math

import jax
import jax.numpy as jnp
from jax import lax
from jax.experimental import pallas as pl
from jax.experimental.pallas import tpu as pltpu

F32 = jnp.float32
BF16 = jnp.bfloat16
MESH = pl.DeviceIdType.MESH

HEAD_DIM = 64
POOL_WINDOWS = (2, 4, 8, 16)
CONV_WIDTH = 3
RMS_EPS = 1e-6
ADAM_LR = 0.001
ADAM_B1 = 0.9
ADAM_B2 = 0.999
ADAM_EPS = 1e-08
ADAM_WD = 0.01
ADAM_STEP = 10

LANES = 128
VMEM_LIMIT = 56 * 1024 * 1024
N_CHIPS = 4
N_DEV = 8

NN = (((1,), (0,)), ((), ()))
NT = (((1,), (1,)), ((), ()))
TN = (((0,), (0,)), ((), ()))


def _tile(n, pref):
    for t in range(pref - pref % 16, 15, -16):
        if n % t == 0:
            return t
    return n


def _params(n_grid):
    return pltpu.CompilerParams(dimension_semantics=("arbitrary",) * n_grid, vmem_limit_bytes=VMEM_LIMIT)


def _dot(a, b, dims):
    return lax.dot_general(a, b, dims, preferred_element_type=F32)


def _mm(name, dims, operands, in_specs, out_shape, out_specs, grid, acc_shape, epilogue):
    n_in, n_out, nk = len(operands), len(out_shape), grid[-1]

    def kern(*refs):
        extras, outs = refs[2:n_in], refs[n_in:n_in + n_out]
        if nk == 1:
            epilogue(_dot(refs[0][...].astype(BF16), refs[1][...].astype(BF16), dims), extras, outs)
            return
        acc = refs[n_in + n_out]
        k = pl.program_id(len(grid) - 1)

        @pl.when(k == 0)
        def _():
            acc[...] = jnp.zeros_like(acc)

        acc[...] += _dot(refs[0][...].astype(BF16), refs[1][...].astype(BF16), dims)

        @pl.when(k == nk - 1)
        def _():
            epilogue(acc[...], extras, outs)

    return pl.pallas_call(
        kern, name=name, grid=grid, in_specs=in_specs, out_specs=out_specs, out_shape=out_shape,
        scratch_shapes=[pltpu.VMEM(acc_shape, F32)] if nk > 1 else [],
        compiler_params=_params(len(grid)),
    )(*operands)


def _store(scale=None, dtype=None):
    def ep(acc, extras, outs):
        v = acc if scale is None else acc * scale
        outs[0][...] = v.astype(outs[0].dtype)
    return ep


def _residual(scale):
    def ep(acc, extras, outs):
        outs[0][...] = extras[0][...] + scale * acc
    return ep


def _final_loss(x, g, target):
    T, D = x.shape
    tr = _tile(T, 256)

    def kern(x_ref, g_ref, t_ref, dx_ref, dg_ref, loss_ref):
        xv = x_ref[...]
        r = lax.rsqrt(jnp.mean(xv * xv, axis=-1, keepdims=True) + RMS_EPS)
        y = xv * r
        err = y * g_ref[...] - t_ref[...]
        lpart = 0.5 * jnp.sum(jnp.mean(err * err, axis=-1, keepdims=True), axis=0, keepdims=True)
        dh = err * (1.0 / D)
        dy = dh * g_ref[...]
        dx_ref[...] = r * (dy - y * jnp.mean(dy * y, axis=-1, keepdims=True))
        part = jnp.sum(dh * y, axis=0, keepdims=True)
        lrow = jnp.broadcast_to(lpart, (1, LANES))

        @pl.when(pl.program_id(0) == 0)
        def _():
            dg_ref[...] = part
            loss_ref[...] = lrow

        @pl.when(pl.program_id(0) > 0)
        def _():
            dg_ref[...] += part
            loss_ref[...] += lrow

    row = pl.BlockSpec((tr, D), lambda i: (i, 0))
    vec = pl.BlockSpec((1, D), lambda i: (0, 0))
    return pl.pallas_call(
        kern, name="final_loss", grid=(T // tr,), in_specs=[row, vec, row],
        out_specs=[row, vec, pl.BlockSpec((1, LANES), lambda i: (0, 0))],
        out_shape=[jax.ShapeDtypeStruct((T, D), F32), jax.ShapeDtypeStruct((1, D), F32),
                   jax.ShapeDtypeStruct((1, LANES), F32)],
        compiler_params=_params(1),
    )(x, g, target)


def _resident(shape, index_map):
    return pl.BlockSpec(shape, index_map, pipeline_mode=pl.Buffered(1))


def _token_operand(token):
    return ([], []) if token is None else ([token], [pl.BlockSpec(token.shape, lambda i: (0, 0))])


def _ffn_up(x, g, w4, token=None):
    T, D = x.shape
    w4, l = w4
    Fh = w4.shape[3]
    F = 2 * Fh
    tm = _tile(T, 512)
    tok_ops, tok_specs = _token_operand(token)

    def kern(x_ref, g_ref, w_ref, *rest):
        h_ref, jac_ref, act_ref = rest[len(tok_ops):]
        xv = x_ref[...]
        r = lax.rsqrt(jnp.mean(xv * xv, axis=-1, keepdims=True) + RMS_EPS)
        hv = (xv * r * g_ref[...]).astype(BF16)
        h_ref[...] = hv
        for j in range(2):
            cols = slice(j * Fh, (j + 1) * Fh)
            gate = _dot(hv, w_ref[j], NN)
            up = _dot(hv, w_ref[2 + j], NN)
            sg = jax.nn.sigmoid(gate)
            silu = gate * sg
            jac_ref[0, :, cols] = (up * (sg + silu * (1.0 - sg))).astype(BF16)
            jac_ref[1, :, cols] = silu.astype(BF16)
            act_ref[:, cols] = (silu * up).astype(BF16)

    return pl.pallas_call(
        kern, name="ffn_up", grid=(T // tm,),
        in_specs=[pl.BlockSpec((tm, D), lambda i: (i, 0)), pl.BlockSpec((1, D), lambda i: (0, 0)),
                  _resident((None, 4, D, Fh), lambda i: (l, 0, 0, 0))] + tok_specs,
        out_specs=[pl.BlockSpec((tm, D), lambda i: (i, 0)), pl.BlockSpec((2, tm, F), lambda i: (0, i, 0)),
                   pl.BlockSpec((tm, F), lambda i: (i, 0))],
        out_shape=[jax.ShapeDtypeStruct((T, D), BF16), jax.ShapeDtypeStruct((2, T, F), BF16),
                   jax.ShapeDtypeStruct((T, F), BF16)],
        compiler_params=_params(1),
    )(x, g, w4, *tok_ops)


def _ffn_bwd_main(dres, jac, x, g, w_out, w4, token=None):
    T, D = dres.shape
    w_out, l = w_out
    w4, _ = w4
    F = w_out.shape[1]
    Fh = F // 2
    tm = _tile(T, 512)
    tok_ops, tok_specs = _token_operand(token)

    def kern(d_ref, jac_ref, x_ref, g_ref, wo_ref, wi_ref, *rest):
        dgu_ref, dx_ref, dg_ref = rest[len(tok_ops):]
        dv = d_ref[...]
        d16 = dv.astype(BF16)
        dh = jnp.zeros((tm, D), F32)
        for j in range(2):
            cols = slice(j * Fh, (j + 1) * Fh)
            dact = 0.5 * _dot(d16, wo_ref[cols, :], NT)
            dgate = (dact * jac_ref[0, :, cols].astype(F32)).astype(BF16)
            dup = (dact * jac_ref[1, :, cols].astype(F32)).astype(BF16)
            dgu_ref[0, :, cols] = dgate
            dgu_ref[1, :, cols] = dup
            dh = dh + _dot(dgate, wi_ref[j], NT) + _dot(dup, wi_ref[2 + j], NT)
        xv = x_ref[...]
        r = lax.rsqrt(jnp.mean(xv * xv, axis=-1, keepdims=True) + RMS_EPS)
        y = xv * r
        dy = dh * g_ref[...]
        dx_ref[...] = dv + r * (dy - y * jnp.mean(dy * y, axis=-1, keepdims=True))
        part = jnp.sum(dh * y, axis=0, keepdims=True)

        @pl.when(pl.program_id(0) == 0)
        def _():
            dg_ref[...] = part

        @pl.when(pl.program_id(0) > 0)
        def _():
            dg_ref[...] += part

    row = pl.BlockSpec((tm, D), lambda i: (i, 0))
    vec = pl.BlockSpec((1, D), lambda i: (0, 0))
    wide = pl.BlockSpec((2, tm, F), lambda i: (0, i, 0))
    return pl.pallas_call(
        kern, name="ffn_bwd_main", grid=(T // tm,),
        in_specs=[row, wide, row, vec, _resident((None, F, D), lambda i: (l, 0, 0)),
                  _resident((None, 4, D, Fh), lambda i: (l, 0, 0, 0))] + tok_specs,
        out_specs=[wide, row, vec],
        out_shape=[jax.ShapeDtypeStruct((2, T, F), BF16), jax.ShapeDtypeStruct((T, D), F32),
                   jax.ShapeDtypeStruct((1, D), F32)],
        compiler_params=_params(1),
    )(dres, jac, x, g, w_out, w4, *tok_ops)


def _ffn_out(act, w_out, x):
    T, F = act.shape
    w_out, l = w_out
    D = w_out.shape[2]
    tm = _tile(T, 512)
    return _mm("ffn_out", NN, [act, w_out, x],
               [pl.BlockSpec((tm, F), lambda i, k: (i, 0)), pl.BlockSpec((None, F, D), lambda i, k: (l, 0, 0)),
                pl.BlockSpec((tm, D), lambda i, k: (i, 0))],
               [jax.ShapeDtypeStruct((T, D), F32)], [pl.BlockSpec((tm, D), lambda i, k: (i, 0))],
               (T // tm, 1), None, _residual(0.5))


def _ffn_dw_out(act, dres):
    T, F = act.shape
    D = dres.shape[1]
    tm, tk = F // 2, _tile(T, 2048)
    return _mm("ffn_dw_out", TN, [act, dres],
               [pl.BlockSpec((tk, tm), lambda i, k: (k, i)), pl.BlockSpec((tk, D), lambda i, k: (k, 0))],
               [jax.ShapeDtypeStruct((F, D), BF16)], [pl.BlockSpec((tm, D), lambda i, k: (i, 0))],
               (2, T // tk), (tm, D), _store(0.5))


def _ffn_dw_in(h, dgu):
    T, D = h.shape
    Fh = dgu.shape[2] // 2
    tk = _tile(T, 2048)
    return _mm("ffn_dw_in", TN, [h, dgu],
               [pl.BlockSpec((tk, D), lambda j, k: (k, 0)),
                pl.BlockSpec((None, tk, Fh), lambda j, k: (j // 2, k, j % 2))],
               [jax.ShapeDtypeStruct((4, D, Fh), BF16)], [pl.BlockSpec((None, D, Fh), lambda j, k: (j, 0, 0))],
               (4, T // tk), (D, Fh), _store())


def _proj(name, a, w, out_dtype, dims=NN, extra=None, scale=None, w_rows=None):
    T, K = a.shape
    w, l = w
    n_rows, j_rows = w_rows if w_rows is not None else (w.shape[1], 0)
    N = w.shape[2] if dims == NN else n_rows
    tm = _tile(T, 512)
    ops = [a, w] + ([extra] if extra is not None else [])
    specs = [pl.BlockSpec((tm, K), lambda i, k: (i, 0)),
             pl.BlockSpec((None, n_rows, w.shape[2]), lambda i, k: (l, j_rows, 0))]
    if extra is not None:
        specs.append(pl.BlockSpec((tm, N), lambda i, k: (i, 0)))
    ep = _residual(1.0) if extra is not None else _store(scale)
    return _mm(name, dims, ops, specs, [jax.ShapeDtypeStruct((T, N), out_dtype)],
               [pl.BlockSpec((tm, N), lambda i, k: (i, 0))], (T // tm, 1), None, ep)[0]


def _mix_up(x, g, wp, widths):
    T, D = x.shape
    wp, l = wp
    n_qkv, n_rest = widths
    NP = wp.shape[2]
    tm = _tile(T, 512)

    def kern(x_ref, g_ref, w_ref, h_ref, qkv_ref, rest_ref, fl_ref):
        xv = x_ref[...]
        r = lax.rsqrt(jnp.mean(xv * xv, axis=-1, keepdims=True) + RMS_EPS)
        hv = (xv * r * g_ref[...]).astype(BF16)
        h_ref[...] = hv
        qkv_ref[...] = _dot(hv, w_ref[:, 0:n_qkv], NN).astype(BF16)
        rest_ref[...] = _dot(hv, w_ref[:, n_qkv:n_qkv + n_rest], NN)
        fl_ref[...] = _dot(hv, w_ref[:, n_qkv + n_rest:NP], NN)

    row = lambda n: pl.BlockSpec((tm, n), lambda i: (i, 0))
    return pl.pallas_call(
        kern, name="mix_up", grid=(T // tm,),
        in_specs=[row(D), pl.BlockSpec((1, D), lambda i: (0, 0)), _resident((None, D, NP), lambda i: (l, 0, 0))],
        out_specs=[row(D), row(n_qkv), row(n_rest), row(LANES)],
        out_shape=[jax.ShapeDtypeStruct((T, D), BF16), jax.ShapeDtypeStruct((T, n_qkv), BF16),
                   jax.ShapeDtypeStruct((T, n_rest), F32), jax.ShapeDtypeStruct((T, LANES), F32)],
        compiler_params=_params(1),
    )(x, g, wp)


def _column_starts(pieces):
    starts, at = [], 0
    for p in pieces:
        starts.append(at)
        at += p.shape[1]
    return starts


def _mix_in_bwd(pieces, x, g, dres, wp):
    T, D = x.shape
    wp, l = wp
    NP = wp.shape[2]
    tm = _tile(T, 512)
    n, starts = len(pieces), _column_starts(pieces)

    def kern(*refs):
        x_ref, g_ref, d_ref, w_ref, dx_ref, dg_ref = refs[n:]
        dh = jnp.zeros((tm, D), F32)
        for p_ref, at in zip(refs[:n], starts):
            dh = dh + _dot(p_ref[...].astype(BF16), w_ref[:, at:at + p_ref.shape[1]], NT)
        xv = x_ref[...]
        r = lax.rsqrt(jnp.mean(xv * xv, axis=-1, keepdims=True) + RMS_EPS)
        y = xv * r
        dy = dh * g_ref[...]
        dx_ref[...] = d_ref[...] + r * (dy - y * jnp.mean(dy * y, axis=-1, keepdims=True))
        part = jnp.sum(dh * y, axis=0, keepdims=True)

        @pl.when(pl.program_id(0) == 0)
        def _():
            dg_ref[...] = part

        @pl.when(pl.program_id(0) > 0)
        def _():
            dg_ref[...] += part

    row = lambda n: pl.BlockSpec((tm, n), lambda i: (i, 0))
    vec = pl.BlockSpec((1, D), lambda i: (0, 0))
    return pl.pallas_call(
        kern, name="mix_in_bwd", grid=(T // tm,),
        in_specs=[row(p.shape[1]) for p in pieces] + [row(D), vec, row(D), _resident((None, D, NP), lambda i: (l, 0, 0))],
        out_specs=[row(D), vec],
        out_shape=[jax.ShapeDtypeStruct((T, D), F32), jax.ShapeDtypeStruct((1, D), F32)],
        compiler_params=_params(1),
    )(*pieces, x, g, dres, wp)


def _pieces_dw(name, a, pieces, out_dtype, tk_pref):
    T, M = a.shape
    n, starts = len(pieces), _column_starts(pieces)
    N = starts[-1] + pieces[-1].shape[1]
    tk = _tile(T, tk_pref)
    nk = T // tk

    def kern(a_ref, *refs):
        o_ref, acc = refs[n], refs[n + 1]
        k = pl.program_id(0)

        @pl.when(k == 0)
        def _():
            acc[...] = jnp.zeros_like(acc)

        av = a_ref[...].astype(BF16)
        for p_ref, at in zip(refs[:n], starts):
            acc[:, at:at + p_ref.shape[1]] += _dot(av, p_ref[...].astype(BF16), TN)

        @pl.when(k == nk - 1)
        def _():
            o_ref[...] = acc[...].astype(out_dtype)

    return pl.pallas_call(
        kern, name=name, grid=(nk,),
        in_specs=[pl.BlockSpec((tk, M), lambda k: (k, 0))] + [pl.BlockSpec((tk, p.shape[1]), lambda k: (k, 0)) for p in pieces],
        out_specs=pl.BlockSpec((M, N), lambda k: (0, 0)), out_shape=jax.ShapeDtypeStruct((M, N), out_dtype),
        scratch_shapes=[pltpu.VMEM((M, N), F32)], compiler_params=_params(1),
    )(a, *pieces)


def _rows_dw(name, pieces, d, out_dtype):
    T, N = d.shape
    n, starts = len(pieces), _column_starts(pieces)
    M = starts[-1] + pieces[-1].shape[1]
    tk = _tile(T, 2048)
    nk = T // tk

    def kern(*refs):
        d_ref, o_ref, acc = refs[n], refs[n + 1], refs[n + 2]
        k = pl.program_id(0)

        @pl.when(k == 0)
        def _():
            acc[...] = jnp.zeros_like(acc)

        dv = d_ref[...].astype(BF16)
        for p_ref, at in zip(refs[:n], starts):
            acc[at:at + p_ref.shape[1], :] += _dot(p_ref[...], dv, TN)

        @pl.when(k == nk - 1)
        def _():
            o_ref[...] = acc[...].astype(out_dtype)

    return pl.pallas_call(
        kern, name=name, grid=(nk,),
        in_specs=[pl.BlockSpec((tk, p.shape[1]), lambda k: (k, 0)) for p in pieces] + [pl.BlockSpec((tk, N), lambda k: (k, 0))],
        out_specs=pl.BlockSpec((M, N), lambda k: (0, 0)), out_shape=jax.ShapeDtypeStruct((M, N), out_dtype),
        scratch_shapes=[pltpu.VMEM((M, N), F32)], compiler_params=_params(1),
    )(*pieces, d)


def _mix_out(pieces, w, x):
    T, D = x.shape
    w, l = w
    n, starts = len(pieces), _column_starts(pieces)
    tm = _tile(T, 512)

    def kern(*refs):
        w_ref, x_ref, o_ref = refs[n:]
        acc = x_ref[...]
        for p_ref, at in zip(refs[:n], starts):
            acc = acc + _dot(p_ref[...], w_ref[at:at + p_ref.shape[1], :], NN)
        o_ref[...] = acc

    row = lambda m: pl.BlockSpec((tm, m), lambda i: (i, 0))
    return pl.pallas_call(
        kern, name="mix_out", grid=(T // tm,),
        in_specs=[row(p.shape[1]) for p in pieces] + [_resident((None,) + w.shape[1:], lambda i: (l, 0, 0)), row(D)],
        out_specs=row(D), out_shape=jax.ShapeDtypeStruct((T, D), F32), compiler_params=_params(1),
    )(*pieces, w, x)


def _log_sigmoid(z):
    return jnp.minimum(z, 0.0) - jnp.log(1.0 + jnp.exp(-jnp.abs(z)))


def _decay_fwd(fl, bias):
    B, S, _ = fl.shape

    def kern(fl_ref, b_ref, o_ref):
        d = _log_sigmoid(fl_ref[...] + b_ref[...])
        row = lax.broadcasted_iota(jnp.int32, (S, LANES), 0)
        sh = 1
        while sh < S:
            d = d + jnp.where(row >= sh, pltpu.roll(d, sh, 0), 0.0)
            sh *= 2
        o_ref[...] = d.T[0:8, :]

    return pl.pallas_call(
        kern, name="decay_fwd", grid=(B,),
        in_specs=[pl.BlockSpec((None, S, LANES), lambda b: (b, 0, 0)), pl.BlockSpec((1, LANES), lambda b: (0, 0))],
        out_specs=pl.BlockSpec((None, 8, S), lambda b: (b, 0, 0)),
        out_shape=jax.ShapeDtypeStruct((B, 8, S), F32), compiler_params=_params(1),
    )(fl, bias)


def _decay_bwd(ddrow, ddcol, fl, bias, n_heads):
    B, S, _ = fl.shape

    def kern(dd_ref, ddc_ref, fl_ref, b_ref, dfl_ref, db_ref):
        dd = jnp.concatenate([dd_ref[...], jnp.zeros((LANES - 8, S), F32)], axis=0).T + ddc_ref[...]
        row = lax.broadcasted_iota(jnp.int32, (S, LANES), 0)
        lane = lax.broadcasted_iota(jnp.int32, (S, LANES), 1)
        sh = 1
        while sh < S:
            dd = dd + jnp.where(row < S - sh, pltpu.roll(dd, S - sh, 0), 0.0)
            sh *= 2
        z = fl_ref[...] + b_ref[...]
        dfl = jnp.where(lane < n_heads, dd / (1.0 + jnp.exp(z)), 0.0)
        dfl_ref[...] = dfl
        part = jnp.sum(dfl, axis=0, keepdims=True)

        @pl.when(pl.program_id(0) == 0)
        def _():
            db_ref[...] = part

        @pl.when(pl.program_id(0) > 0)
        def _():
            db_ref[...] += part

    return pl.pallas_call(
        kern, name="decay_bwd", grid=(B,),
        in_specs=[pl.BlockSpec((None, 8, S), lambda b: (b, 0, 0)), pl.BlockSpec((None, S, LANES), lambda b: (b, 0, 0)),
                  pl.BlockSpec((None, S, LANES), lambda b: (b, 0, 0)), pl.BlockSpec((1, LANES), lambda b: (0, 0))],
        out_specs=[pl.BlockSpec((None, S, LANES), lambda b: (b, 0, 0)), pl.BlockSpec((1, LANES), lambda b: (0, 0))],
        out_shape=[jax.ShapeDtypeStruct((B, S, LANES), F32), jax.ShapeDtypeStruct((1, LANES), F32)],
        compiler_params=_params(1),
    )(ddrow, ddcol, fl, bias)


def _attn_fwd(qkv, drow, n_heads, tq):
    B, S, _ = qkv.shape
    DA = n_heads * HEAD_DIM
    scale = HEAD_DIM ** -0.5

    n_pairs = n_heads // 2

    def kern(q_ref, k_ref, v_ref, dr_ref, o_ref, lse_ref):
        i = pl.program_id(1)
        lane = lax.broadcasted_iota(jnp.int32, (tq, LANES), 1)
        low = lane < HEAD_DIM
        causal = lax.broadcasted_iota(jnp.int32, (tq, tq), 1) <= lax.broadcasted_iota(jnp.int32, (tq, tq), 0)
        qms = []
        for p in range(n_pairs):
            q2 = q_ref[:, LANES * p:LANES * (p + 1)] * scale
            qms += [jnp.where(low, q2, jnp.zeros_like(q2)), jnp.where(low, jnp.zeros_like(q2), q2)]

        def step(j, carry, masked):
            ms, ls, accs = carry
            ks = pl.multiple_of(j * tq, tq)
            new_m, new_l, new_acc = [], [], []
            for p in range(n_pairs):
                cols = slice(LANES * p, LANES * (p + 1))
                k2, v2 = k_ref[pl.ds(ks, tq), cols], v_ref[pl.ds(ks, tq), cols]
                alphas, pvs = [], []
                for h in (2 * p, 2 * p + 1):
                    s = _dot(qms[h], k2, NT) - dr_ref[h, pl.ds(j, 1), :]
                    if masked:
                        s = jnp.where(causal, s, -jnp.inf)
                    m_new = jnp.maximum(ms[h], jnp.max(s, axis=1, keepdims=True))
                    alpha = jnp.exp(ms[h] - m_new)
                    pm = jnp.exp(s - m_new)
                    new_m.append(m_new)
                    new_l.append(alpha * ls[h] + jnp.sum(pm, axis=1, keepdims=True))
                    alphas.append(alpha)
                    pvs.append(_dot(pm.astype(BF16), v2, NN))
                new_acc.append(jnp.where(low, alphas[0], alphas[1]) * accs[p] + jnp.where(low, pvs[0], pvs[1]))
            return tuple(new_m), tuple(new_l), tuple(new_acc)

        init = (tuple(jnp.full((tq, 1), -jnp.inf, F32) for _ in range(n_heads)),
                tuple(jnp.zeros((tq, 1), F32) for _ in range(n_heads)),
                tuple(jnp.zeros((tq, LANES), F32) for _ in range(n_pairs)))
        ms, ls, accs = step(i, lax.fori_loop(0, i, functools.partial(step, masked=False), init), True)
        lse_mat = jnp.zeros((tq, LANES), F32)
        for p in range(n_pairs):
            l0, l1 = ls[2 * p], ls[2 * p + 1]
            o_ref[:, LANES * p:LANES * (p + 1)] = (accs[p] / jnp.where(low, l0, l1)).astype(BF16)
            lse_mat = jnp.where(lane == 2 * p, ms[2 * p] + jnp.log(l0), lse_mat)
            lse_mat = jnp.where(lane == 2 * p + 1, ms[2 * p + 1] + jnp.log(l1), lse_mat)
        lse_ref[...] = lse_mat

    nq = S // tq
    return pl.pallas_call(
        kern, name="attn_fwd", grid=(B, nq),
        in_specs=[pl.BlockSpec((None, tq, DA), lambda b, i: (b, i, 0)),
                  pl.BlockSpec((None, S, DA), lambda b, i: (b, 0, 1)),
                  pl.BlockSpec((None, S, DA), lambda b, i: (b, 0, 2)),
                  pl.BlockSpec((None, 8, nq, tq), lambda b, i: (b, 0, 0, 0))],
        out_specs=[pl.BlockSpec((None, tq, DA), lambda b, i: (b, i, 0)),
                   pl.BlockSpec((None, tq, LANES), lambda b, i: (b, i, 0))],
        out_shape=[jax.ShapeDtypeStruct((B, S, DA), BF16), jax.ShapeDtypeStruct((B, S, LANES), F32)],
        compiler_params=_params(2),
    )(qkv, qkv, qkv, drow)


def _attn_bwd(qkv, drow, o, lse, dres, w_out, n_heads, tq):
    B, S, _ = qkv.shape
    D = dres.shape[2]
    w_out, l_out = w_out
    DA = n_heads * HEAD_DIM
    scale = HEAD_DIM ** -0.5
    nq = S // tq

    n_pairs = n_heads // 2

    def kern(q_ref, k_ref, v_ref, dr_ref, o_ref, lse_ref, d_ref, w_ref, dq_ref, dk_ref, dv_ref, ddr_ref, ddc_ref,
             dk_acc, dv_acc, qm_s, dom_s, delta_s, rs_s, dq_s, do_ref):
        i = pl.program_id(1)
        do_ref[...] = _dot(d_ref[...].astype(BF16), w_ref[...], NT)

        @pl.when(i == 0)
        def _():
            dk_acc[...] = jnp.zeros_like(dk_acc)
            dv_acc[...] = jnp.zeros_like(dv_acc)
            ddr_ref[...] = jnp.zeros_like(ddr_ref)

        lane = lax.broadcasted_iota(jnp.int32, (tq, LANES), 1)
        low = lane < HEAD_DIM
        causal = lax.broadcasted_iota(jnp.int32, (tq, tq), 1) <= lax.broadcasted_iota(jnp.int32, (tq, tq), 0)
        for p in range(n_pairs):
            cols = slice(LANES * p, LANES * (p + 1))
            q2 = q_ref[:, cols] * scale
            do_f = do_ref[:, cols]
            do2 = do_f.astype(BF16)
            prod = do_f * o_ref[:, cols].astype(F32)
            qm_s[2 * p] = jnp.where(low, q2, jnp.zeros_like(q2))
            qm_s[2 * p + 1] = jnp.where(low, jnp.zeros_like(q2), q2)
            dom_s[2 * p] = jnp.where(low, do2, jnp.zeros_like(do2))
            dom_s[2 * p + 1] = jnp.where(low, jnp.zeros_like(do2), do2)
            delta_s[2 * p] = jnp.sum(jnp.where(low, prod, 0.0), axis=1, keepdims=True)
            delta_s[2 * p + 1] = jnp.sum(jnp.where(low, 0.0, prod), axis=1, keepdims=True)
            dq_s[p] = jnp.zeros((tq, LANES), F32)
        rs_s[...] = jnp.zeros(rs_s.shape, F32)

        def step(j, masked):
            ks = pl.multiple_of(j * tq, tq)
            for p in range(n_pairs):
                cols = slice(LANES * p, LANES * (p + 1))
                k2, v2 = k_ref[pl.ds(ks, tq), cols], v_ref[pl.ds(ks, tq), cols]
                dvs, dks, dqs = [], [], []
                for h in (2 * p, 2 * p + 1):
                    qm, dom = qm_s[h], dom_s[h]
                    s = _dot(qm, k2, NT) - dr_ref[h, pl.ds(j, 1), :]
                    if masked:
                        s = jnp.where(causal, s, -jnp.inf)
                    pm = jnp.exp(s - lse_ref[:, h:h + 1])
                    ds = pm * (_dot(dom, v2, NT) - delta_s[h])
                    ddr_ref[h, pl.ds(j, 1), :] -= jnp.sum(ds, axis=0, keepdims=True)
                    rs_s[h] += jnp.sum(ds, axis=1, keepdims=True)
                    dsb = ds.astype(BF16)
                    dvs.append(_dot(pm.astype(BF16), dom, TN))
                    dks.append(_dot(dsb, qm, TN))
                    dqs.append(_dot(dsb, k2, NN))
                dv_acc[pl.ds(ks, tq), cols] += dvs[0] + dvs[1]
                dk_acc[pl.ds(ks, tq), cols] += dks[0] + dks[1]
                dq_s[p] += jnp.where(low, dqs[0], dqs[1])

        def body(j, carry):
            step(j, False)
            return carry

        lax.fori_loop(0, i, body, 0)
        step(i, True)
        ddc = jnp.zeros((tq, LANES), F32)
        for p in range(n_pairs):
            dq_ref[:, LANES * p:LANES * (p + 1)] = (dq_s[p] * scale).astype(BF16)
            ddc = jnp.where(lane == 2 * p, rs_s[2 * p], ddc)
            ddc = jnp.where(lane == 2 * p + 1, rs_s[2 * p + 1], ddc)
        ddc_ref[...] = ddc

        @pl.when(i == nq - 1)
        def _():
            dk_ref[...] = dk_acc[...].astype(BF16)
            dv_ref[...] = dv_acc[...].astype(BF16)

    tile = pl.BlockSpec((None, tq, DA), lambda b, i: (b, i, 0))
    seq = pl.BlockSpec((None, S, DA), lambda b, i: (b, 0, 0))
    dec = pl.BlockSpec((None, 8, nq, tq), lambda b, i: (b, 0, 0, 0))
    return pl.pallas_call(
        kern, name="attn_bwd", grid=(B, nq),
        in_specs=[tile, pl.BlockSpec((None, S, DA), lambda b, i: (b, 0, 1)),
                  pl.BlockSpec((None, S, DA), lambda b, i: (b, 0, 2)), dec, tile,
                  pl.BlockSpec((None, tq, LANES), lambda b, i: (b, i, 0)),
                  pl.BlockSpec((None, tq, D), lambda b, i: (b, i, 0)),
                  _resident((None, DA, D), lambda b, i: (l_out, 0, 0))],
        out_specs=[tile, seq, seq, dec, pl.BlockSpec((None, tq, LANES), lambda b, i: (b, i, 0))],
        out_shape=[jax.ShapeDtypeStruct((B, S, DA), BF16)] * 3 + [jax.ShapeDtypeStruct((B, 8, nq, tq), F32),
                                                                  jax.ShapeDtypeStruct((B, S, LANES), F32)],
        scratch_shapes=[pltpu.VMEM((S, DA), F32), pltpu.VMEM((S, DA), F32),
                        pltpu.VMEM((n_heads, tq, LANES), BF16), pltpu.VMEM((n_heads, tq, LANES), BF16),
                        pltpu.VMEM((n_heads, tq, 1), F32), pltpu.VMEM((n_heads, tq, 1), F32),
                        pltpu.VMEM((n_pairs, tq, LANES), F32), pltpu.VMEM((tq, DA), F32)],
        compiler_params=_params(2),
    )(qkv, qkv, qkv, drow, o, lse, dres, w_out)


def _down(v, d, row):
    return jnp.where(row >= d, pltpu.roll(v, d, 0), 0.0)


def _up(v, d, row, S):
    return jnp.where(row < S - d, pltpu.roll(v, S - d, 0), 0.0)


def _window(v, shift, group):
    sums, acc, d = [], v, 1
    for _ in POOL_WINDOWS:
        acc = acc + shift(acc, d)
        sums.append(acc)
        d *= 2
    out = sums[-1]
    for gi in range(len(POOL_WINDOWS) - 2, -1, -1):
        out = jnp.where(group == gi, sums[gi], out)
    return out


def _pool_count(row, group):
    w = jnp.full(row.shape, POOL_WINDOWS[-1], jnp.int32)
    for gi in range(len(POOL_WINDOWS) - 2, -1, -1):
        w = jnp.where(group == gi, POOL_WINDOWS[gi], w)
    return jnp.minimum(row + 1, w).astype(F32)


def _mix_local_fwd(rest, wbd, ps, cw):
    B, S, C4 = rest.shape
    C = C4 // 4
    gw = C // len(POOL_WINDOWS)

    def kern(r_ref, w_ref, ps_ref, cw_ref, y_ref, pooled_ref):
        row = lax.broadcasted_iota(jnp.int32, (S, C), 0)
        group = lax.broadcasted_iota(jnp.int32, (S, C), 1) // gw
        u = r_ref[:, 0:C]
        pooled = _window(u, lambda v, d: _down(v, d, row), group) / _pool_count(row, group) - u
        pb = pooled.astype(BF16)
        pooled_ref[...] = pb
        y_ref[:, 0:C] = (_dot(pb, w_ref[...], NN) * ps_ref[...]).astype(BF16)
        uc = r_ref[:, 2 * C:3 * C] * r_ref[:, 3 * C:4 * C]
        y = cw_ref[0:1, :] * _down(uc, 2, row) + cw_ref[1:2, :] * _down(uc, 1, row) + cw_ref[2:3, :] * uc
        y_ref[:, C:2 * C] = (r_ref[:, C:2 * C] * y).astype(BF16)

    return pl.pallas_call(
        kern, name="mix_local_fwd", grid=(B,),
        in_specs=[pl.BlockSpec((None, S, C4), lambda b: (b, 0, 0)), pl.BlockSpec((C, C), lambda b: (0, 0)),
                  pl.BlockSpec((1, C), lambda b: (0, 0)), pl.BlockSpec((8, C), lambda b: (0, 0))],
        out_specs=[pl.BlockSpec((None, S, 2 * C), lambda b: (b, 0, 0)), pl.BlockSpec((None, S, C), lambda b: (b, 0, 0))],
        out_shape=[jax.ShapeDtypeStruct((B, S, 2 * C), BF16), jax.ShapeDtypeStruct((B, S, C), BF16)],
        compiler_params=_params(1),
    )(rest, wbd, ps, cw)


def _mix_local_bwd(rest, pooled, dycat, wbd, ps, cw):
    B, S, C4 = rest.shape
    C = C4 // 4
    gw = C // len(POOL_WINDOWS)

    def kern(r_ref, pooled_ref, d_ref, w_ref, ps_ref, cw_ref, dr_ref, dw_ref, dps_ref, dcw_ref):
        row = lax.broadcasted_iota(jnp.int32, (S, C), 0)
        group = lax.broadcasted_iota(jnp.int32, (S, C), 1) // gw
        dyp = d_ref[:, 0:C]
        dyc = d_ref[:, C:2 * C]
        pb = pooled_ref[...]
        dps = jnp.sum(dyp * _dot(pb, w_ref[...], NN), axis=0, keepdims=True)
        dzb = (dyp * ps_ref[...]).astype(BF16)
        dw = _dot(pb, dzb, TN)
        dpooled = _dot(dzb, w_ref[...], NT)
        g = dpooled / _pool_count(row, group)
        dr_ref[:, 0:C] = (_window(g, lambda v, d: _up(v, d, row, S), group) - dpooled).astype(BF16)
        cc, ch = r_ref[:, 2 * C:3 * C], r_ref[:, 3 * C:4 * C]
        uc = cc * ch
        u1, u2 = _down(uc, 1, row), _down(uc, 2, row)
        y = cw_ref[0:1, :] * u2 + cw_ref[1:2, :] * u1 + cw_ref[2:3, :] * uc
        dr_ref[:, C:2 * C] = (dyc * y).astype(BF16)
        dy = dyc * r_ref[:, C:2 * C]
        duc = cw_ref[0:1, :] * _up(dy, 2, row, S) + cw_ref[1:2, :] * _up(dy, 1, row, S) + cw_ref[2:3, :] * dy
        dr_ref[:, 2 * C:3 * C] = (duc * ch).astype(BF16)
        dr_ref[:, 3 * C:4 * C] = (duc * cc).astype(BF16)
        dcw = jnp.concatenate([jnp.sum(dy * u2, axis=0, keepdims=True), jnp.sum(dy * u1, axis=0, keepdims=True),
                               jnp.sum(dy * uc, axis=0, keepdims=True), jnp.zeros((5, C), F32)], axis=0)

        @pl.when(pl.program_id(0) == 0)
        def _():
            dw_ref[...] = dw
            dps_ref[...] = dps
            dcw_ref[...] = dcw

        @pl.when(pl.program_id(0) > 0)
        def _():
            dw_ref[...] += dw
            dps_ref[...] += dps
            dcw_ref[...] += dcw

    full = lambda shape: pl.BlockSpec(shape, lambda b: (0, 0))
    return pl.pallas_call(
        kern, name="mix_local_bwd", grid=(B,),
        in_specs=[pl.BlockSpec((None, S, C4), lambda b: (b, 0, 0)), pl.BlockSpec((None, S, C), lambda b: (b, 0, 0)),
                  pl.BlockSpec((None, S, 2 * C), lambda b: (b, 0, 0)), full((C, C)), full((1, C)), full((8, C))],
        out_specs=[pl.BlockSpec((None, S, C4), lambda b: (b, 0, 0)), full((C, C)), full((1, C)), full((8, C))],
        out_shape=[jax.ShapeDtypeStruct((B, S, C4), BF16), jax.ShapeDtypeStruct((C, C), F32),
                   jax.ShapeDtypeStruct((1, C), F32), jax.ShapeDtypeStruct((8, C), F32)],
        compiler_params=_params(1),
    )(rest, pooled, dycat, wbd, ps, cw)


def _adamw(w, gs, m, v, token=None):
    R, C = w.shape
    pieces = [p if isinstance(p, tuple) else (p,) for p in gs]
    owner = [s for s, p in enumerate(pieces) for _ in p]
    flat = [a for p in pieces for a in p]
    n = len(flat)
    rows = R // len(pieces)
    tr = _tile(rows, 256)
    per = rows // tr
    tok_ops, tok_specs = _token_operand(token)

    def kern(w_ref, *refs):
        g_refs, (m_ref, v_ref), (g_out, d_ref, nm_ref, nv_ref) = refs[:n], refs[n:n + 2], refs[n + 2 + len(tok_ops):]
        vals, at = [], 0
        for p in pieces:
            vals.append(g_refs[at][...] if len(p) == 1 else g_refs[at][...] + g_refs[at + 1][...])
            at += len(p)
        gv = vals[0]
        for s in range(1, len(pieces)):
            gv = jnp.where(pl.program_id(0) // per == s, vals[s], gv)
        nm = ADAM_B1 * m_ref[...] + (1.0 - ADAM_B1) * gv
        nv = ADAM_B2 * v_ref[...] + (1.0 - ADAM_B2) * (gv * gv)
        m_hat = nm / (1.0 - ADAM_B1 ** ADAM_STEP)
        v_hat = nv / (1.0 - ADAM_B2 ** ADAM_STEP)
        g_out[...] = gv
        d_ref[...] = -ADAM_LR * (m_hat / (jnp.sqrt(v_hat) + ADAM_EPS) + ADAM_WD * w_ref[...])
        nm_ref[...] = nm
        nv_ref[...] = nv

    def piece(s):
        return pl.BlockSpec((tr, C), lambda i: (jnp.clip(i - s * per, 0, per - 1), 0))

    blk = pl.BlockSpec((tr, C), lambda i: (i, 0))
    return pl.pallas_call(
        kern, name="adamw", grid=(R // tr,), in_specs=[blk] + [piece(s) for s in owner] + [blk] * 2 + tok_specs,
        out_specs=[blk] * 4, out_shape=[jax.ShapeDtypeStruct((R, C), F32)] * 4, compiler_params=_params(1),
    )(w, *flat, m, v, *tok_ops)


def _place():
    x, y, c = lax.axis_index("x"), lax.axis_index("y"), lax.axis_index("c")
    return x, y, c, [(1 - x, y), (x, 1 - y), (1 - x, 1 - y)]


def _comm_call(name, body, operands, out_shape, n_sems, aliases=None):
    any_spec = pl.BlockSpec(memory_space=pl.ANY)
    return pl.pallas_call(
        body, name=name, in_specs=[any_spec] * len(operands), out_specs=[any_spec] * len(out_shape),
        out_shape=out_shape, input_output_aliases=aliases or {},
        scratch_shapes=[pltpu.SemaphoreType.DMA((n,)) for n in n_sems],
    )(*operands)


def _my_block():
    return 2 * lax.axis_index("x") + lax.axis_index("y")


def _place_shard(w, dtype, first=0, count=None):
    L, R, C = w.shape
    count = L if count is None else count
    tr = _tile(R, 512)

    def kern(w_ref, o_ref):
        o_ref[...] = w_ref[...].astype(dtype)

    return pl.pallas_call(
        kern, name="place_shard", grid=(count, R // tr),
        in_specs=[pl.BlockSpec((None, tr, C), lambda l, i: (first + l, i, 0))],
        out_specs=pl.BlockSpec((None, None, tr, C), lambda l, i: (l, _my_block(), i, 0)),
        out_shape=jax.ShapeDtypeStruct((count, N_CHIPS, R, C), dtype), compiler_params=_params(2),
    )(w)


HALF_ROWS = 16


def _rows(ref, half):
    hr = ref.shape[-2] // 2
    return ref.at[(slice(None),) * (len(ref.shape) - 2) + (pl.ds(half * hr, hr),)]


def _all_gather(bufs):
    n = len(bufs)

    def body(*refs):
        outs = refs[n:2 * n]
        send_sems, recv_sems = refs[2 * n:]
        x, y, c, chips = _place()
        sibling = (x, y, 1 - c)

        def remote(k, j, chip, half, to):
            blk = 2 * chip[0] + chip[1]
            if outs[k].shape[2] % (2 * HALF_ROWS) == 0:
                region = _rows(outs[k].at[:, blk], half)
            else:
                hl = outs[k].shape[0] // 2
                region = outs[k].at[pl.ds(half * hl, hl), blk]
            return pltpu.make_async_remote_copy(
                src_ref=region, dst_ref=region, send_sem=send_sems.at[6 * k + j],
                recv_sem=recv_sems.at[6 * k + j], device_id=to, device_id_type=MESH)

        first = [remote(k, j, (x, y), c, (*chip, c)) for k in range(n) for j, chip in enumerate(chips)]
        for cp in first:
            cp.start()
        passed = []
        for k in range(n):
            for j, chip in enumerate(chips):
                remote(k, j, chip, c, (x, y, c)).wait_recv()
                passed.append(remote(k, 3 + j, chip, c, sibling))
                passed[-1].start()
        for k in range(n):
            for j, chip in enumerate(chips):
                remote(k, 3 + j, chip, 1 - c, (x, y, c)).wait_recv()
        for cp in first + passed:
            cp.wait_send()

    out_shape = [jax.ShapeDtypeStruct(s.shape, s.dtype) for s in bufs]
    return _comm_call("all_gather_weights", body, bufs, out_shape, (6 * n, 6 * n), aliases={k: k for k in range(n)})


_HBM = pl.BlockSpec(memory_space=pltpu.HBM)
_SEM = pl.BlockSpec(memory_space=pltpu.SEMAPHORE)
_ANY = pl.BlockSpec(memory_space=pl.ANY)


def _split_start(name, bufs, n_copies, make_copies, after):
    n = len(bufs)

    def body(*refs):
        send_sems, recv_sems, token = refs[n + 1], refs[n + 2], refs[2 * n + 3]
        for cp in make_copies(refs[:n], send_sems, recv_sems):
            cp.start()
        token[...] = jnp.zeros_like(token)

    res = pl.pallas_call(
        body, name=name, in_specs=[_HBM] * n + [_ANY],
        out_shape=(pltpu.SemaphoreType.DMA((n_copies,)), pltpu.SemaphoreType.DMA((n_copies,)),
                   *[pltpu.HBM(b.shape, b.dtype) for b in bufs], jax.ShapeDtypeStruct((8, LANES), F32)),
        out_specs=(_SEM, _SEM, *[_HBM] * n, pl.BlockSpec(memory_space=pltpu.VMEM)),
        input_output_aliases={i: 2 + i for i in range(n)},
        compiler_params=pltpu.CompilerParams(has_side_effects=pltpu.SideEffectType.DATAFLOW_SIDE_EFFECTING),
    )(*[pltpu.with_memory_space_constraint(b, pltpu.HBM) for b in bufs], after)
    return res[0], res[1], list(res[2:2 + n]), res[2 + n]


def _split_wait(name, send_sems, recv_sems, bufs, make_copies, after):
    n = len(bufs)

    def body(*refs):
        for cp in make_copies(refs[:n], refs[n], refs[n + 1]):
            cp.wait_send()
            cp.wait_recv()

    return list(pl.pallas_call(
        body, name=name, in_specs=[_HBM] * n + [_SEM, _SEM, _ANY],
        out_shape=tuple(pltpu.HBM(b.shape, b.dtype) for b in bufs), out_specs=tuple([_HBM] * n),
        input_output_aliases={i: i for i in range(n)},
        compiler_params=pltpu.CompilerParams(has_side_effects=pltpu.SideEffectType.DATAFLOW_SIDE_EFFECTING),
    )(*bufs, send_sems, recv_sems, after))


def _gather_copies(refs, send_sems, recv_sems):
    x, y, c, chips = _place()
    return [pltpu.make_async_remote_copy(
        src_ref=ref.at[:, 2 * x + y], dst_ref=ref.at[:, 2 * x + y], send_sem=send_sems.at[3 * k + j],
        recv_sem=recv_sems.at[3 * k + j], device_id=(*chip, c), device_id_type=MESH)
        for k, ref in enumerate(refs) for j, chip in enumerate(chips)]


def _exchange_copies(refs, send_sems, recv_sems):
    n = len(refs) // 2
    x, y, c, chips = _place()
    return [pltpu.make_async_remote_copy(
        src_ref=refs[k].at[:, 2 * chip[0] + chip[1]], dst_ref=refs[n + k].at[j], send_sem=send_sems.at[3 * k + j],
        recv_sem=recv_sems.at[3 * k + j], device_id=(*chip, c), device_id_type=MESH)
        for k in range(n) for j, chip in enumerate(chips)]


def _all_reduce_small(v, after):
    n = v.shape[0]

    def body(v_ref, after_ref, o_ref, gbuf, send_sems, recv_sems):
        x, y, c, _ = _place()
        me = 4 * x + 2 * y + c
        gbuf[me] = v_ref[...]
        copies, waits = [], []
        for r in range(1, N_DEV):
            px = 1 - x if r & 4 else x
            py = 1 - y if r & 2 else y
            pc = 1 - c if r & 1 else c
            mk = functools.partial(pltpu.make_async_remote_copy, src_ref=v_ref, send_sem=send_sems.at[r - 1],
                                   recv_sem=recv_sems.at[r - 1], device_id=(px, py, pc), device_id_type=MESH)
            copies.append(mk(dst_ref=gbuf.at[me]))
            waits.append(mk(dst_ref=gbuf.at[4 * px + 2 * py + pc]))
        for cp in copies:
            cp.start()
        for cp in waits:
            cp.wait_recv()
        for cp in copies:
            cp.wait_send()
        acc = gbuf[0]
        for d in range(1, N_DEV):
            acc = acc + gbuf[d]
        o_ref[...] = acc

    vm = pl.BlockSpec(memory_space=pltpu.VMEM)
    return pl.pallas_call(
        body, name="all_reduce_small", in_specs=[vm, _ANY], out_specs=vm, out_shape=jax.ShapeDtypeStruct(v.shape, F32),
        scratch_shapes=[pltpu.VMEM((N_DEV, n, LANES), F32), pltpu.SemaphoreType.DMA((N_DEV - 1,)),
                        pltpu.SemaphoreType.DMA((N_DEV - 1,))],
        compiler_params=pltpu.CompilerParams(vmem_limit_bytes=VMEM_LIMIT),
    )(v, after)


def _add_blocks(p, h2):
    L, nb, hr, C = p.shape
    tr = _tile(hr, 512)

    def kern(p_ref, h0_ref, h1_ref, h2_ref, o_ref):
        o_ref[...] = ((p_ref[...].astype(F32) + h0_ref[...].astype(F32)) + h1_ref[...].astype(F32)) + h2_ref[...].astype(F32)

    def other(j):
        return pl.BlockSpec((None, None, tr, C), lambda l, i: (j, l, i, 0))

    return pl.pallas_call(
        kern, name="rs_add_blocks", grid=(L, hr // tr),
        in_specs=[pl.BlockSpec((None, None, tr, C), lambda l, i: (l, _my_block(), i, 0)), other(0), other(1), other(2)],
        out_specs=pl.BlockSpec((None, tr, C), lambda l, i: (l, i, 0)),
        out_shape=jax.ShapeDtypeStruct((L, hr, C), F32), compiler_params=_params(2),
    )(p, h2, h2, h2)


WEIGHTS = ("norm_ffn1", "w_ffn1_in", "w_ffn1_out", "norm_mix", "w_mix_in", "b_forget", "w_pool", "pool_scale",
           "conv_w", "w_mix_out", "norm_ffn2", "w_ffn2_in", "w_ffn2_out", "norm_final")
BIG = ("w_ffn1_in", "w_ffn1_out", "w_mix_in", "w_mix_out", "w_ffn2_in", "w_ffn2_out")
SMALL = ("norm_ffn1", "norm_mix", "b_forget", "w_pool", "pool_scale", "conv_w", "norm_ffn2", "norm_final")


def _layer_params(small, gathered, conv_w, D, l):
    DA, C, H = D // 2, D // 4, D // 2 // HEAD_DIM
    P = {}
    if "w_ffn1_in" in gathered:
        P.update(g1=small["norm_ffn1"][l][None], w1in=(gathered["w_ffn1_in"], 0),
                 w1out=(gathered["w_ffn1_out"].reshape(1, -1, D), 0))
    if "w_ffn2_in" in gathered:
        P.update(g2=small["norm_ffn2"][l][None], w2in=(gathered["w_ffn2_in"], 0),
                 w2out=(gathered["w_ffn2_out"].reshape(1, -1, D), 0))
    if "w_mix_in" in gathered:
        w_in = jnp.concatenate([gathered["w_mix_in"][:, b] for b in range(N_CHIPS)], axis=2)
        wqkv, wrest = w_in[:, :, :3 * DA], w_in[:, :, 3 * DA + H:]
        wf = jnp.pad(w_in[:, :, 3 * DA:3 * DA + H], ((0, 0), (0, 0), (0, LANES - H)))
        ng = len(POOL_WINDOWS)
        same_group = jnp.eye(ng, dtype=bool)[:, None, :, None]
        wbd = jnp.where(same_group, small["w_pool"][l][:, :, None, :], 0.0).reshape(C, C)
        cw = jnp.concatenate([conv_w[l, b] for b in range(N_CHIPS)], axis=1)
        P.update(gm=small["norm_mix"][l][None], wp=(jnp.concatenate([wqkv, wrest, wf], axis=2), 0),
                 wmixout=(gathered["w_mix_out"].reshape(1, D, D), 0),
                 bias=jnp.pad(small["b_forget"][l][None], ((0, 0), (0, LANES - H))), wbd=wbd.astype(BF16),
                 ps=small["pool_scale"][l][None], cw=jnp.pad(cw, ((0, 8 - CONV_WIDTH), (0, 0))))
    return P


def _ffn_fwd(x, g, w_in, w_out, token=None):
    h, jac, act = _ffn_up(x, g, w_in, token)
    return _ffn_out(act, w_out, x)[0], (x, h, jac, act)


def _ffn_bwd(dres, saved, g, w_in, w_out, token=None):
    x, h, jac, act = saved
    dgu, dx, dg = _ffn_bwd_main(dres, jac, x, g, w_out, w_in, token)
    dw_out = _ffn_dw_out(act, dres)[0]
    dw_in = _ffn_dw_in(h, dgu)[0]
    return dx, dg, dw_in, dw_out.reshape(N_CHIPS, -1, dw_out.shape[1])


def _mixer_fwd(x, P, B, S, tq):
    T, D = x.shape
    DA, C, H = D // 2, D // 4, D // 2 // HEAD_DIM
    hn, qkv, rest, fl = _mix_up(x, P["gm"], P["wp"], (3 * DA, 4 * C))
    qkv, rest, fl = qkv.reshape(B, S, 3 * DA), rest.reshape(B, S, 4 * C), fl.reshape(B, S, LANES)
    drow = _decay_fwd(fl, P["bias"]).reshape(B, 8, S // tq, tq)
    o, lse = _attn_fwd(qkv, drow, H, tq)
    ypc, pooled = _mix_local_fwd(rest, P["wbd"], P["ps"], P["cw"])
    x_out = _mix_out([o.reshape(T, DA), ypc.reshape(T, 2 * C)], P["wmixout"], x)
    return x_out, (x, hn, qkv, rest, fl, drow, o, lse, pooled, ypc)


def _mixer_bwd(dres, saved, P, B, S, tq):
    x, hn, qkv, rest, fl, drow, o, lse, pooled, ypc = saved
    T, D = x.shape
    DA, C, H = D // 2, D // 4, D // 2 // HEAD_DIM
    dypc = _proj("mix_out_bwd", dres, P["wmixout"], F32, NT, w_rows=(2 * C, 1)).reshape(B, S, 2 * C)
    dw_out = _rows_dw("mix_out_dw", [o.reshape(T, DA), ypc.reshape(T, 2 * C)], dres, BF16)
    dq, dk, dv, ddrow, ddcol = _attn_bwd(qkv, drow, o, lse, dres.reshape(B, S, D), P["wmixout"], H, tq)
    dfl, dbias = _decay_bwd(ddrow.reshape(B, 8, S), ddcol, fl, P["bias"], H)
    drest, dwbd, dps, dcw = _mix_local_bwd(rest, pooled, dypc, P["wbd"], P["ps"], P["cw"])
    pieces = [a.reshape(T, a.shape[-1]) for a in (dq, dk, dv, drest, dfl)]
    dwp = _pieces_dw("mix_in_dw", hn, pieces, BF16, 1024)
    dx, dg = _mix_in_bwd(pieces, x, P["gm"], dres, P["wp"])
    n_q, n_r = 3 * DA, 4 * C
    dw_in = jnp.concatenate([dwp[:, :n_q], dwp[:, n_q + n_r:n_q + n_r + H], dwp[:, n_q:n_q + n_r]], axis=1)
    dw_in = dw_in.reshape(D, N_CHIPS, -1).transpose(1, 0, 2)
    ng = len(POOL_WINDOWS)
    same_group = jnp.eye(ng, dtype=bool)[:, None, :, None]
    dw_pool = jnp.where(same_group, dwbd.reshape(ng, C // ng, ng, C // ng), 0.0).sum(axis=2)
    small = dict(norm_mix=dg[0], b_forget=dbias[0, :H], w_pool=dw_pool, pool_scale=dps[0], conv_w=dcw[:CONV_WIDTH])
    return dx, small, dw_in, dw_out.reshape(N_CHIPS, -1, D)


FFN1, MIX, FFN2 = BIG[:2], BIG[2:4], BIG[4:]


def _local_step(x, target, small, conv_w, pipe):
    B, S, D = x.shape
    L = small["norm_ffn1"].shape[0]
    tq = _tile(S, 512)
    xt = x.reshape(B * S, D)
    saved, params = [], []
    for l in range(L):
        P = _layer_params(small, pipe.weights(l, xt), conv_w, D, l)
        xt, s1 = _ffn_fwd(xt, P["g1"], P["w1in"], P["w1out"], pipe.token(l))
        P.update(_layer_params(small, pipe.weights_mix(l, xt), conv_w, D, l))
        xt, s2 = _mixer_fwd(xt, P, B, S, tq)
        P.update(_layer_params(small, pipe.weights_ffn2(l, xt), conv_w, D, l))
        xt, s3 = _ffn_fwd(xt, P["g2"], P["w2in"], P["w2out"])
        saved.append((s1, s2, s3))
        params.append(P)
    dres, dgf, loss = _final_loss(xt, small["norm_final"][None], target.reshape(B * S, D))
    sm = {k: [None] * L for k in SMALL if k != "norm_final"}
    token = None
    for l in reversed(range(L)):
        P, (s1, s2, s3) = params[l], saved[l]
        big = {}
        dres, dg2, big["w_ffn2_in"], big["w_ffn2_out"] = _ffn_bwd(dres, s3, P["g2"], P["w2in"], P["w2out"], token)
        dres, smix, big["w_mix_in"], big["w_mix_out"] = _mixer_bwd(dres, s2, P, B, S, tq)
        big = {k: val[None] for k, val in big.items()}
        token = pipe.grads(l, FFN2 + MIX, big, dres) if l == 0 else None
        dres, dg1, dw_in, dw_out = _ffn_bwd(dres, s1, P["g1"], P["w1in"], P["w1out"], token)
        big.update(w_ffn1_in=dw_in[None], w_ffn1_out=dw_out[None])
        sm["norm_ffn1"][l], sm["norm_ffn2"][l] = dg1[0], dg2[0]
        for k, val in smix.items():
            sm[k][l] = val
        token = pipe.grads(l, FFN1 if l == 0 else BIG, big, big["w_ffn1_in"])
    sm = {k: jnp.stack(val) for k, val in sm.items()}
    sm["norm_final"] = dgf[0]
    return loss[0, 0], dres.reshape(B, S, D), sm


def _sibling_copies(refs, send_sems, recv_sems):
    n = len(refs) // 2
    x, y, c, _ = _place()
    return [pltpu.make_async_remote_copy(
        src_ref=refs[k], dst_ref=refs[n + k], send_sem=send_sems.at[k], recv_sem=recv_sems.at[k],
        device_id=(x, y, 1 - c), device_id_type=MESH) for k in range(n)]


class _Pipeline:
    def __init__(self, w):
        self.w, self.n_layers = w, w[BIG[0]].shape[0]
        first = _all_gather([_place_shard(w[k], BF16, 0, 1) for k in FFN1] + [_place_shard(w["conv_w"], F32)])
        self.conv_w = first[-1]
        self._ready = dict(zip(FFN1, first[:-1]))
        self._mix = self._start_gather("0_mix", MIX, 0, first[0])
        self._ffn2 = self._start_gather("0_ffn2", FFN2, 0, self._mix[1][3])
        self._next = (1, self._start_gather("1", BIG, 1, self._ffn2[1][3]))
        self._reduce, self._swaps = None, []
        self.reduced = [dict() for _ in range(self.n_layers)]

    def _start_gather(self, tag, kinds, l, after):
        placed = [_place_shard(self.w[k], BF16, l, 1) for k in kinds]
        return kinds, _split_start(f"gather_start_{tag}", placed, 3 * len(kinds), _gather_copies, after)

    def _wait_gather(self, tag, started, after):
        kinds, (send_sems, recv_sems, bufs, _) = started
        return dict(zip(kinds, _split_wait(f"gather_wait_{tag}", send_sems, recv_sems, bufs, _gather_copies, after)))

    def token(self, l):
        return self._next[1][1][3] if self._next is not None and self._next[0] == l + 1 else None

    def weights(self, l, after):
        if l == 0:
            return self._ready
        got = self._wait_gather(str(l), self._next[1], after)
        self._next = (l + 1, self._start_gather(str(l + 1), BIG, l + 1, got[BIG[0]])) if l + 1 < self.n_layers else None
        return got

    def weights_mix(self, l, after):
        return self._wait_gather("0_mix", self._mix, after) if l == 0 else {}

    def weights_ffn2(self, l, after):
        return self._wait_gather("0_ffn2", self._ffn2, after) if l == 0 else {}

    def _finish_reduce(self, after):
        if self._reduce is None:
            return None
        tag, l, kinds, (send_sems, recv_sems, bufs, _) = self._reduce
        n = len(kinds)
        bufs = _split_wait(f"reduce_wait_{tag}", send_sems, recv_sems, bufs, _exchange_copies, after)
        mine = [_add_blocks(p, o) for p, o in zip(bufs[:n], bufs[n:])]
        lands = [lax.empty(q.shape, q.dtype) for q in mine]
        self._swaps.append((tag, l, kinds, _split_start(f"swap_start_{tag}", mine + lands, n, _sibling_copies, mine[0])))
        self._reduce = None
        return self._swaps[-1][3][3]

    def grads(self, l, kinds, big, after):
        swap_token = self._finish_reduce(after)
        grads = [big[k] for k in kinds]
        tag = str(l) if len(kinds) == len(BIG) else f"{l}_{kinds[0][2:]}"
        lands = [lax.empty((3, g.shape[0]) + g.shape[2:], g.dtype) for g in grads]
        started = _split_start(f"reduce_start_{tag}", grads + lands, 3 * len(kinds), _exchange_copies,
                               grads[0] if swap_token is None else swap_token)
        self._reduce = (tag, l, kinds, started)
        self.last_token = started[3]
        return started[3]

    def finish(self, after, last=False):
        if last:
            self._finish_reduce(after)
        for tag, l, kinds, (send_sems, recv_sems, bufs, _) in self._swaps:
            n = len(kinds)
            bufs = _split_wait(f"swap_wait_{tag}", send_sems, recv_sems, bufs, _sibling_copies, after)
            self.reduced[l].update(zip(kinds, zip(bufs[:n], bufs[n:])))
        self._swaps = []
        return self.reduced


def _pack(parts, extra=()):
    flat = jnp.concatenate([p.reshape(-1) for p in parts] + [jnp.reshape(e, (1,)) for e in extra])
    n = -(-flat.shape[0] // (8 * LANES)) * 8
    return jnp.pad(flat, (0, n * LANES - flat.shape[0])).reshape(n, LANES)


def _unpack(buf, shapes):
    flat, out, at = buf.reshape(-1), [], 0
    for s in shapes:
        n = math.prod(s)
        out.append(flat[at:at + n].reshape(s))
        at += n
    return out, flat[at:]


def kernel(x, norm_ffn1, w_ffn1_in, w_ffn1_out, norm_mix, w_mix_in, b_forget, w_pool, pool_scale, conv_w, w_mix_out, norm_ffn2, w_ffn2_in, w_ffn2_out, norm_final, loss_target, m_norm_ffn1, m_w_ffn1_in, m_w_ffn1_out, m_norm_mix, m_w_mix_in, m_b_forget, m_w_pool, m_pool_scale, m_conv_w, m_w_mix_out, m_norm_ffn2, m_w_ffn2_in, m_w_ffn2_out, m_norm_final, v_norm_ffn1, v_w_ffn1_in, v_w_ffn1_out, v_norm_mix, v_w_mix_in, v_b_forget, v_w_pool, v_pool_scale, v_conv_w, v_w_mix_out, v_norm_ffn2, v_w_ffn2_in, v_w_ffn2_out, v_norm_final):
    w = dict(zip(WEIGHTS, (norm_ffn1, w_ffn1_in, w_ffn1_out, norm_mix, w_mix_in, b_forget, w_pool, pool_scale, conv_w, w_mix_out, norm_ffn2, w_ffn2_in, w_ffn2_out, norm_final)))
    m = dict(zip(WEIGHTS, (m_norm_ffn1, m_w_ffn1_in, m_w_ffn1_out, m_norm_mix, m_w_mix_in, m_b_forget, m_w_pool, m_pool_scale, m_conv_w, m_w_mix_out, m_norm_ffn2, m_w_ffn2_in, m_w_ffn2_out, m_norm_final)))
    v = dict(zip(WEIGHTS, (v_norm_ffn1, v_w_ffn1_in, v_w_ffn1_out, v_norm_mix, v_w_mix_in, v_b_forget, v_w_pool, v_pool_scale, v_conv_w, v_w_mix_out, v_norm_ffn2, v_w_ffn2_in, v_w_ffn2_out, v_norm_final)))
    block = 2 * lax.axis_index("x") + lax.axis_index("y")

    pipe = _Pipeline(w)
    small = {k: w[k] for k in SMALL}
    loss, grad_x, sm = _local_step(x, loss_target, small, pipe.conv_w, pipe)
    grads, delta, new_m, new_v = {}, {}, {}, {}

    def big_adamw(k, reduced, token=None):
        two_d = lambda a: a.reshape(-1, a.shape[-1])
        pieces = [tuple(map(two_d, layer[k])) for layer in reduced]
        res = _adamw(two_d(w[k]), pieces, two_d(m[k]), two_d(v[k]), token)
        grads[k], delta[k], new_m[k], new_v[k] = [r.reshape(w[k].shape) for r in res]

    reduced = pipe.finish(grad_x)
    for k in BIG[2:]:
        big_adamw(k, reduced, pipe.last_token)
    reduced = pipe.finish(new_v[BIG[-1]], last=True)
    for k in BIG[:2]:
        big_adamw(k, reduced)

    order = list(SMALL)
    total = _all_reduce_small(_pack([sm[k] for k in order], extra=(loss,)), new_v[BIG[0]])
    parts, rest = _unpack(total, [sm[k].shape for k in order])
    grads.update(zip(order, parts))
    loss = rest[0]
    cs = conv_w.shape[2]
    grads["conv_w"] = lax.dynamic_slice_in_dim(grads["conv_w"], block * cs, cs, axis=2)
    packed = [_pack([t[k] for k in order]) for t in (w, grads, m, v)]
    _, d, nm, nv = _adamw(packed[0], [packed[1]], packed[2], packed[3])
    shapes = [w[k].shape for k in order]
    for res, flat in ((delta, d), (new_m, nm), (new_v, nv)):
        res.update(zip(order, _unpack(flat, shapes)[0]))
    return (loss, grad_x, *[grads[k] for k in WEIGHTS], *[delta[k] for k in WEIGHTS],
            *[new_m[k] for k in WEIGHTS], *[new_v[k] for k in WEIGHTS])
```

```python
import functools
import math

import jax
import jax.numpy as jnp
from jax import lax
from jax.experimental import pallas as pl
from jax.experimental.pallas import tpu as pltpu

F32 = jnp.float32
BF16 = jnp.bfloat16
MESH = pl.DeviceIdType.MESH

HEAD_DIM = 64
POOL_WINDOWS = (2, 4, 8, 16)
CONV_WIDTH = 3
RMS_EPS = 1e-6
ADAM_LR = 0.001
ADAM_B1 = 0.9
ADAM_B2 = 0.999
ADAM_EPS = 1e-08
ADAM_WD = 0.01
ADAM_STEP = 10

LANES = 128
VMEM_LIMIT = 56 * 1024 * 1024
N_CHIPS = 4
N_DEV = 8

NN = (((1,), (0,)), ((), ()))
NT = (((1,), (1,)), ((), ()))
TN = (((0,), (0,)), ((), ()))


def _tile(n, pref):
    for t in range(pref - pref % 16, 15, -16):
        if n % t == 0:
            return t
    return n


def _params(n_grid):
    return pltpu.CompilerParams(dimension_semantics=("arbitrary",) * n_grid, vmem_limit_bytes=VMEM_LIMIT)


def _dot(a, b, dims):
    return lax.dot_general(a, b, dims, preferred_element_type=F32)


def _mm(name, dims, operands, in_specs, out_shape, out_specs, grid, acc_shape, epilogue):
    n_in, n_out, nk = len(operands), len(out_shape), grid[-1]

    def kern(*refs):
        extras, outs = refs[2:n_in], refs[n_in:n_in + n_out]
        if nk == 1:
            epilogue(_dot(refs[0][...].astype(BF16), refs[1][...].astype(BF16), dims), extras, outs)
            return
        acc = refs[n_in + n_out]
        k = pl.program_id(len(grid) - 1)

        @pl.when(k == 0)
        def _():
            acc[...] = jnp.zeros_like(acc)

        acc[...] += _dot(refs[0][...].astype(BF16), refs[1][...].astype(BF16), dims)

        @pl.when(k == nk - 1)
        def _():
            epilogue(acc[...], extras, outs)

    return pl.pallas_call(
        kern, name=name, grid=grid, in_specs=in_specs, out_specs=out_specs, out_shape=out_shape,
        scratch_shapes=[pltpu.VMEM(acc_shape, F32)] if nk > 1 else [],
        compiler_params=_params(len(grid)),
    )(*operands)


def _store(scale=None, dtype=None):
    def ep(acc, extras, outs):
        v = acc if scale is None else acc * scale
        outs[0][...] = v.astype(outs[0].dtype)
    return ep


def _residual(scale):
    def ep(acc, extras, outs):
        outs[0][...] = extras[0][...] + scale * acc
    return ep


def _final_loss(x, g, target):
    T, D = x.shape
    tr = _tile(T, 256)

    def kern(x_ref, g_ref, t_ref, dx_ref, dg_ref, loss_ref):
        xv = x_ref[...]
        r = lax.rsqrt(jnp.mean(xv * xv, axis=-1, keepdims=True) + RMS_EPS)
        y = xv * r
        err = y * g_ref[...] - t_ref[...]
        lpart = 0.5 * jnp.sum(jnp.mean(err * err, axis=-1, keepdims=True), axis=0, keepdims=True)
        dh = err * (1.0 / D)
        dy = dh * g_ref[...]
        dx_ref[...] = r * (dy - y * jnp.mean(dy * y, axis=-1, keepdims=True))
        part = jnp.sum(dh * y, axis=0, keepdims=True)
        lrow = jnp.broadcast_to(lpart, (1, LANES))

        @pl.when(pl.program_id(0) == 0)
        def _():
            dg_ref[...] = part
            loss_ref[...] = lrow

        @pl.when(pl.program_id(0) > 0)
        def _():
            dg_ref[...] += part
            loss_ref[...] += lrow

    row = pl.BlockSpec((tr, D), lambda i: (i, 0))
    vec = pl.BlockSpec((1, D), lambda i: (0, 0))
    return pl.pallas_call(
        kern, name="final_loss", grid=(T // tr,), in_specs=[row, vec, row],
        out_specs=[row, vec, pl.BlockSpec((1, LANES), lambda i: (0, 0))],
        out_shape=[jax.ShapeDtypeStruct((T, D), F32), jax.ShapeDtypeStruct((1, D), F32),
                   jax.ShapeDtypeStruct((1, LANES), F32)],
        compiler_params=_params(1),
    )(x, g, target)


def _resident(shape, index_map):
    return pl.BlockSpec(shape, index_map, pipeline_mode=pl.Buffered(1))


def _token_operand(token):
    return ([], []) if token is None else ([token], [pl.BlockSpec(token.shape, lambda i: (0, 0))])


def _ffn_up(x, g, w4, token=None):
    T, D = x.shape
    w4, l = w4
    Fh = w4.shape[3]
    F = 2 * Fh
    tm = _tile(T, 512)
    tok_ops, tok_specs = _token_operand(token)

    def kern(x_ref, g_ref, w_ref, *rest):
        h_ref, jac_ref, act_ref = rest[len(tok_ops):]
        xv = x_ref[...]
        r = lax.rsqrt(jnp.mean(xv * xv, axis=-1, keepdims=True) + RMS_EPS)
        hv = (xv * r * g_ref[...]).astype(BF16)
        h_ref[...] = hv
        for j in range(2):
            cols = slice(j * Fh, (j + 1) * Fh)
            gate = _dot(hv, w_ref[j], NN)
            up = _dot(hv, w_ref[2 + j], NN)
            sg = jax.nn.sigmoid(gate)
            silu = gate * sg
            jac_ref[0, :, cols] = (up * (sg + silu * (1.0 - sg))).astype(BF16)
            jac_ref[1, :, cols] = silu.astype(BF16)
            act_ref[:, cols] = (silu * up).astype(BF16)

    return pl.pallas_call(
        kern, name="ffn_up", grid=(T // tm,),
        in_specs=[pl.BlockSpec((tm, D), lambda i: (i, 0)), pl.BlockSpec((1, D), lambda i: (0, 0)),
                  _resident((None, 4, D, Fh), lambda i: (l, 0, 0, 0))] + tok_specs,
        out_specs=[pl.BlockSpec((tm, D), lambda i: (i, 0)), pl.BlockSpec((2, tm, F), lambda i: (0, i, 0)),
                   pl.BlockSpec((tm, F), lambda i: (i, 0))],
        out_shape=[jax.ShapeDtypeStruct((T, D), BF16), jax.ShapeDtypeStruct((2, T, F), BF16),
                   jax.ShapeDtypeStruct((T, F), BF16)],
        compiler_params=_params(1),
    )(x, g, w4, *tok_ops)


def _ffn_bwd_main(dres, jac, x, g, w_out, w4, token=None):
    T, D = dres.shape
    w_out, l = w_out
    w4, _ = w4
    F = w_out.shape[1]
    Fh = F // 2
    tm = _tile(T, 512)
    tok_ops, tok_specs = _token_operand(token)

    def kern(d_ref, jac_ref, x_ref, g_ref, wo_ref, wi_ref, *rest):
        dgu_ref, dx_ref, dg_ref = rest[len(tok_ops):]
        dv = d_ref[...]
        d16 = dv.astype(BF16)
        dh = jnp.zeros((tm, D), F32)
        for j in range(2):
            cols = slice(j * Fh, (j + 1) * Fh)
            dact = 0.5 * _dot(d16, wo_ref[cols, :], NT)
            dgate = (dact * jac_ref[0, :, cols].astype(F32)).astype(BF16)
            dup = (dact * jac_ref[1, :, cols].astype(F32)).astype(BF16)
            dgu_ref[0, :, cols] = dgate
            dgu_ref[1, :, cols] = dup
            dh = dh + _dot(dgate, wi_ref[j], NT) + _dot(dup, wi_ref[2 + j], NT)
        xv = x_ref[...]
        r = lax.rsqrt(jnp.mean(xv * xv, axis=-1, keepdims=True) + RMS_EPS)
        y = xv * r
        dy = dh * g_ref[...]
        dx_ref[...] = dv + r * (dy - y * jnp.mean(dy * y, axis=-1, keepdims=True))
        part = jnp.sum(dh * y, axis=0, keepdims=True)

        @pl.when(pl.program_id(0) == 0)
        def _():
            dg_ref[...] = part

        @pl.when(pl.program_id(0) > 0)
        def _():
            dg_ref[...] += part

    row = pl.BlockSpec((tm, D), lambda i: (i, 0))
    vec = pl.BlockSpec((1, D), lambda i: (0, 0))
    wide = pl.BlockSpec((2, tm, F), lambda i: (0, i, 0))
    return pl.pallas_call(
        kern, name="ffn_bwd_main", grid=(T // tm,),
        in_specs=[row, wide, row, vec, _resident((None, F, D), lambda i: (l, 0, 0)),
                  _resident((None, 4, D, Fh), lambda i: (l, 0, 0, 0))] + tok_specs,
        out_specs=[wide, row, vec],
        out_shape=[jax.ShapeDtypeStruct((2, T, F), BF16), jax.ShapeDtypeStruct((T, D), F32),
                   jax.ShapeDtypeStruct((1, D), F32)],
        compiler_params=_params(1),
    )(dres, jac, x, g, w_out, w4, *tok_ops)


def _ffn_out(act, w_out, x):
    T, F = act.shape
    w_out, l = w_out
    D = w_out.shape[2]
    tm = _tile(T, 512)
    return _mm("ffn_out", NN, [act, w_out, x],
               [pl.BlockSpec((tm, F), lambda i, k: (i, 0)), pl.BlockSpec((None, F, D), lambda i, k: (l, 0, 0)),
                pl.BlockSpec((tm, D), lambda i, k: (i, 0))],
               [jax.ShapeDtypeStruct((T, D), F32)], [pl.BlockSpec((tm, D), lambda i, k: (i, 0))],
               (T // tm, 1), None, _residual(0.5))


def _ffn_dw_out(act, dres):
    T, F = act.shape
    D = dres.shape[1]
    tm, tk = F // 2, _tile(T, 2048)
    return _mm("ffn_dw_out", TN, [act, dres],
               [pl.BlockSpec((tk, tm), lambda i, k: (k, i)), pl.BlockSpec((tk, D), lambda i, k: (k, 0))],
               [jax.ShapeDtypeStruct((F, D), BF16)], [pl.BlockSpec((tm, D), lambda i, k: (i, 0))],
               (2, T // tk), (tm, D), _store(0.5))


def _ffn_dw_in(h, dgu):
    T, D = h.shape
    Fh = dgu.shape[2] // 2
    tk = _tile(T, 2048)
    return _mm("ffn_dw_in", TN, [h, dgu],
               [pl.BlockSpec((tk, D), lambda j, k: (k, 0)),
                pl.BlockSpec((None, tk, Fh), lambda j, k: (j // 2, k, j % 2))],
               [jax.ShapeDtypeStruct((4, D, Fh), BF16)], [pl.BlockSpec((None, D, Fh), lambda j, k: (j, 0, 0))],
               (4, T // tk), (D, Fh), _store())


def _proj(name, a, w, out_dtype, dims=NN, extra=None, scale=None):
    T, K = a.shape
    w, l = w
    N = w.shape[2] if dims == NN else w.shape[1]
    tm = _tile(T, 512)
    ops = [a, w] + ([extra] if extra is not None else [])
    specs = [pl.BlockSpec((tm, K), lambda i, k: (i, 0)), pl.BlockSpec((None,) + w.shape[1:], lambda i, k: (l, 0, 0))]
    if extra is not None:
        specs.append(pl.BlockSpec((tm, N), lambda i, k: (i, 0)))
    ep = _residual(1.0) if extra is not None else _store(scale)
    return _mm(name, dims, ops, specs, [jax.ShapeDtypeStruct((T, N), out_dtype)],
               [pl.BlockSpec((tm, N), lambda i, k: (i, 0))], (T // tm, 1), None, ep)[0]


def _mix_up(x, g, wp, widths):
    T, D = x.shape
    wp, l = wp
    n_qkv, n_rest = widths
    NP = wp.shape[2]
    tm = _tile(T, 512)

    def kern(x_ref, g_ref, w_ref, h_ref, qkv_ref, rest_ref, fl_ref):
        xv = x_ref[...]
        r = lax.rsqrt(jnp.mean(xv * xv, axis=-1, keepdims=True) + RMS_EPS)
        hv = (xv * r * g_ref[...]).astype(BF16)
        h_ref[...] = hv
        qkv_ref[...] = _dot(hv, w_ref[:, 0:n_qkv], NN).astype(BF16)
        rest_ref[...] = _dot(hv, w_ref[:, n_qkv:n_qkv + n_rest], NN)
        fl_ref[...] = _dot(hv, w_ref[:, n_qkv + n_rest:NP], NN)

    row = lambda n: pl.BlockSpec((tm, n), lambda i: (i, 0))
    return pl.pallas_call(
        kern, name="mix_up", grid=(T // tm,),
        in_specs=[row(D), pl.BlockSpec((1, D), lambda i: (0, 0)), _resident((None, D, NP), lambda i: (l, 0, 0))],
        out_specs=[row(D), row(n_qkv), row(n_rest), row(LANES)],
        out_shape=[jax.ShapeDtypeStruct((T, D), BF16), jax.ShapeDtypeStruct((T, n_qkv), BF16),
                   jax.ShapeDtypeStruct((T, n_rest), F32), jax.ShapeDtypeStruct((T, LANES), F32)],
        compiler_params=_params(1),
    )(x, g, wp)


def _column_starts(pieces):
    starts, at = [], 0
    for p in pieces:
        starts.append(at)
        at += p.shape[1]
    return starts


def _mix_in_bwd(pieces, x, g, dres, wp):
    T, D = x.shape
    wp, l = wp
    NP = wp.shape[2]
    tm = _tile(T, 512)
    n, starts = len(pieces), _column_starts(pieces)

    def kern(*refs):
        x_ref, g_ref, d_ref, w_ref, dx_ref, dg_ref = refs[n:]
        dh = jnp.zeros((tm, D), F32)
        for p_ref, at in zip(refs[:n], starts):
            dh = dh + _dot(p_ref[...].astype(BF16), w_ref[:, at:at + p_ref.shape[1]], NT)
        xv = x_ref[...]
        r = lax.rsqrt(jnp.mean(xv * xv, axis=-1, keepdims=True) + RMS_EPS)
        y = xv * r
        dy = dh * g_ref[...]
        dx_ref[...] = d_ref[...] + r * (dy - y * jnp.mean(dy * y, axis=-1, keepdims=True))
        part = jnp.sum(dh * y, axis=0, keepdims=True)

        @pl.when(pl.program_id(0) == 0)
        def _():
            dg_ref[...] = part

        @pl.when(pl.program_id(0) > 0)
        def _():
            dg_ref[...] += part

    row = lambda n: pl.BlockSpec((tm, n), lambda i: (i, 0))
    vec = pl.BlockSpec((1, D), lambda i: (0, 0))
    return pl.pallas_call(
        kern, name="mix_in_bwd", grid=(T // tm,),
        in_specs=[row(p.shape[1]) for p in pieces] + [row(D), vec, row(D), _resident((None, D, NP), lambda i: (l, 0, 0))],
        out_specs=[row(D), vec],
        out_shape=[jax.ShapeDtypeStruct((T, D), F32), jax.ShapeDtypeStruct((1, D), F32)],
        compiler_params=_params(1),
    )(*pieces, x, g, dres, wp)


def _pieces_dw(name, a, pieces, out_dtype, tk_pref):
    T, M = a.shape
    n, starts = len(pieces), _column_starts(pieces)
    N = starts[-1] + pieces[-1].shape[1]
    tk = _tile(T, tk_pref)
    nk = T // tk

    def kern(a_ref, *refs):
        o_ref, acc = refs[n], refs[n + 1]
        k = pl.program_id(0)

        @pl.when(k == 0)
        def _():
            acc[...] = jnp.zeros_like(acc)

        av = a_ref[...].astype(BF16)
        for p_ref, at in zip(refs[:n], starts):
            acc[:, at:at + p_ref.shape[1]] += _dot(av, p_ref[...].astype(BF16), TN)

        @pl.when(k == nk - 1)
        def _():
            o_ref[...] = acc[...].astype(out_dtype)

    return pl.pallas_call(
        kern, name=name, grid=(nk,),
        in_specs=[pl.BlockSpec((tk, M), lambda k: (k, 0))] + [pl.BlockSpec((tk, p.shape[1]), lambda k: (k, 0)) for p in pieces],
        out_specs=pl.BlockSpec((M, N), lambda k: (0, 0)), out_shape=jax.ShapeDtypeStruct((M, N), out_dtype),
        scratch_shapes=[pltpu.VMEM((M, N), F32)], compiler_params=_params(1),
    )(a, *pieces)


def _rows_dw(name, pieces, d, out_dtype):
    T, N = d.shape
    n, starts = len(pieces), _column_starts(pieces)
    M = starts[-1] + pieces[-1].shape[1]
    tk = _tile(T, 2048)
    nk = T // tk

    def kern(*refs):
        d_ref, o_ref, acc = refs[n], refs[n + 1], refs[n + 2]
        k = pl.program_id(0)

        @pl.when(k == 0)
        def _():
            acc[...] = jnp.zeros_like(acc)

        dv = d_ref[...].astype(BF16)
        for p_ref, at in zip(refs[:n], starts):
            acc[at:at + p_ref.shape[1], :] += _dot(p_ref[...], dv, TN)

        @pl.when(k == nk - 1)
        def _():
            o_ref[...] = acc[...].astype(out_dtype)

    return pl.pallas_call(
        kern, name=name, grid=(nk,),
        in_specs=[pl.BlockSpec((tk, p.shape[1]), lambda k: (k, 0)) for p in pieces] + [pl.BlockSpec((tk, N), lambda k: (k, 0))],
        out_specs=pl.BlockSpec((M, N), lambda k: (0, 0)), out_shape=jax.ShapeDtypeStruct((M, N), out_dtype),
        scratch_shapes=[pltpu.VMEM((M, N), F32)], compiler_params=_params(1),
    )(*pieces, d)


def _mix_out(pieces, w, x):
    T, D = x.shape
    w, l = w
    n, starts = len(pieces), _column_starts(pieces)
    tm = _tile(T, 512)

    def kern(*refs):
        w_ref, x_ref, o_ref = refs[n:]
        acc = x_ref[...]
        for p_ref, at in zip(refs[:n], starts):
            acc = acc + _dot(p_ref[...], w_ref[at:at + p_ref.shape[1], :], NN)
        o_ref[...] = acc

    row = lambda m: pl.BlockSpec((tm, m), lambda i: (i, 0))
    return pl.pallas_call(
        kern, name="mix_out", grid=(T // tm,),
        in_specs=[row(p.shape[1]) for p in pieces] + [_resident((None,) + w.shape[1:], lambda i: (l, 0, 0)), row(D)],
        out_specs=row(D), out_shape=jax.ShapeDtypeStruct((T, D), F32), compiler_params=_params(1),
    )(*pieces, w, x)


def _log_sigmoid(z):
    return jnp.minimum(z, 0.0) - jnp.log(1.0 + jnp.exp(-jnp.abs(z)))


def _decay_fwd(fl, bias):
    B, S, _ = fl.shape

    def kern(fl_ref, b_ref, o_ref):
        d = _log_sigmoid(fl_ref[...] + b_ref[...])
        row = lax.broadcasted_iota(jnp.int32, (S, LANES), 0)
        sh = 1
        while sh < S:
            d = d + jnp.where(row >= sh, pltpu.roll(d, sh, 0), 0.0)
            sh *= 2
        o_ref[...] = d.T[0:8, :]

    return pl.pallas_call(
        kern, name="decay_fwd", grid=(B,),
        in_specs=[pl.BlockSpec((None, S, LANES), lambda b: (b, 0, 0)), pl.BlockSpec((1, LANES), lambda b: (0, 0))],
        out_specs=pl.BlockSpec((None, 8, S), lambda b: (b, 0, 0)),
        out_shape=jax.ShapeDtypeStruct((B, 8, S), F32), compiler_params=_params(1),
    )(fl, bias)


def _decay_bwd(ddrow, ddcol, fl, bias, n_heads):
    B, S, _ = fl.shape

    def kern(dd_ref, ddc_ref, fl_ref, b_ref, dfl_ref, db_ref):
        dd = jnp.concatenate([dd_ref[...], jnp.zeros((LANES - 8, S), F32)], axis=0).T + ddc_ref[...]
        row = lax.broadcasted_iota(jnp.int32, (S, LANES), 0)
        lane = lax.broadcasted_iota(jnp.int32, (S, LANES), 1)
        sh = 1
        while sh < S:
            dd = dd + jnp.where(row < S - sh, pltpu.roll(dd, S - sh, 0), 0.0)
            sh *= 2
        z = fl_ref[...] + b_ref[...]
        dfl = jnp.where(lane < n_heads, dd / (1.0 + jnp.exp(z)), 0.0)
        dfl_ref[...] = dfl
        part = jnp.sum(dfl, axis=0, keepdims=True)

        @pl.when(pl.program_id(0) == 0)
        def _():
            db_ref[...] = part

        @pl.when(pl.program_id(0) > 0)
        def _():
            db_ref[...] += part

    return pl.pallas_call(
        kern, name="decay_bwd", grid=(B,),
        in_specs=[pl.BlockSpec((None, 8, S), lambda b: (b, 0, 0)), pl.BlockSpec((None, S, LANES), lambda b: (b, 0, 0)),
                  pl.BlockSpec((None, S, LANES), lambda b: (b, 0, 0)), pl.BlockSpec((1, LANES), lambda b: (0, 0))],
        out_specs=[pl.BlockSpec((None, S, LANES), lambda b: (b, 0, 0)), pl.BlockSpec((1, LANES), lambda b: (0, 0))],
        out_shape=[jax.ShapeDtypeStruct((B, S, LANES), F32), jax.ShapeDtypeStruct((1, LANES), F32)],
        compiler_params=_params(1),
    )(ddrow, ddcol, fl, bias)


def _attn_fwd(qkv, drow, n_heads, tq):
    B, S, _ = qkv.shape
    DA = n_heads * HEAD_DIM
    scale = HEAD_DIM ** -0.5

    n_pairs = n_heads // 2

    def kern(q_ref, k_ref, v_ref, dr_ref, o_ref, lse_ref):
        i = pl.program_id(1)
        lane = lax.broadcasted_iota(jnp.int32, (tq, LANES), 1)
        low = lane < HEAD_DIM
        causal = lax.broadcasted_iota(jnp.int32, (tq, tq), 1) <= lax.broadcasted_iota(jnp.int32, (tq, tq), 0)
        qms = []
        for p in range(n_pairs):
            q2 = q_ref[:, LANES * p:LANES * (p + 1)] * scale
            qms += [jnp.where(low, q2, jnp.zeros_like(q2)), jnp.where(low, jnp.zeros_like(q2), q2)]

        def step(j, carry, masked):
            ms, ls, accs = carry
            ks = pl.multiple_of(j * tq, tq)
            new_m, new_l, new_acc = [], [], []
            for p in range(n_pairs):
                cols = slice(LANES * p, LANES * (p + 1))
                k2, v2 = k_ref[pl.ds(ks, tq), cols], v_ref[pl.ds(ks, tq), cols]
                alphas, pvs = [], []
                for h in (2 * p, 2 * p + 1):
                    s = _dot(qms[h], k2, NT) - dr_ref[h, pl.ds(j, 1), :]
                    if masked:
                        s = jnp.where(causal, s, -jnp.inf)
                    m_new = jnp.maximum(ms[h], jnp.max(s, axis=1, keepdims=True))
                    alpha = jnp.exp(ms[h] - m_new)
                    pm = jnp.exp(s - m_new)
                    new_m.append(m_new)
                    new_l.append(alpha * ls[h] + jnp.sum(pm, axis=1, keepdims=True))
                    alphas.append(alpha)
                    pvs.append(_dot(pm.astype(BF16), v2, NN))
                new_acc.append(jnp.where(low, alphas[0], alphas[1]) * accs[p] + jnp.where(low, pvs[0], pvs[1]))
            return tuple(new_m), tuple(new_l), tuple(new_acc)

        init = (tuple(jnp.full((tq, 1), -jnp.inf, F32) for _ in range(n_heads)),
                tuple(jnp.zeros((tq, 1), F32) for _ in range(n_heads)),
                tuple(jnp.zeros((tq, LANES), F32) for _ in range(n_pairs)))
        ms, ls, accs = step(i, lax.fori_loop(0, i, functools.partial(step, masked=False), init), True)
        lse_mat = jnp.zeros((tq, LANES), F32)
        for p in range(n_pairs):
            l0, l1 = ls[2 * p], ls[2 * p + 1]
            o_ref[:, LANES * p:LANES * (p + 1)] = (accs[p] / jnp.where(low, l0, l1)).astype(BF16)
            lse_mat = jnp.where(lane == 2 * p, ms[2 * p] + jnp.log(l0), lse_mat)
            lse_mat = jnp.where(lane == 2 * p + 1, ms[2 * p + 1] + jnp.log(l1), lse_mat)
        lse_ref[...] = lse_mat

    nq = S // tq
    return pl.pallas_call(
        kern, name="attn_fwd", grid=(B, nq),
        in_specs=[pl.BlockSpec((None, tq, DA), lambda b, i: (b, i, 0)),
                  pl.BlockSpec((None, S, DA), lambda b, i: (b, 0, 1)),
                  pl.BlockSpec((None, S, DA), lambda b, i: (b, 0, 2)),
                  pl.BlockSpec((None, 8, nq, tq), lambda b, i: (b, 0, 0, 0))],
        out_specs=[pl.BlockSpec((None, tq, DA), lambda b, i: (b, i, 0)),
                   pl.BlockSpec((None, tq, LANES), lambda b, i: (b, i, 0))],
        out_shape=[jax.ShapeDtypeStruct((B, S, DA), BF16), jax.ShapeDtypeStruct((B, S, LANES), F32)],
        compiler_params=_params(2),
    )(qkv, qkv, qkv, drow)


def _attn_bwd(qkv, drow, o, lse, dycat, n_heads, tq):
    B, S, _ = qkv.shape
    DA = n_heads * HEAD_DIM
    scale = HEAD_DIM ** -0.5
    nq = S // tq

    n_pairs = n_heads // 2

    def kern(q_ref, k_ref, v_ref, dr_ref, o_ref, lse_ref, do_ref, dq_ref, dk_ref, dv_ref, ddr_ref, ddc_ref,
             dk_acc, dv_acc, qm_s, dom_s, delta_s, rs_s, dq_s):
        i = pl.program_id(1)

        @pl.when(i == 0)
        def _():
            dk_acc[...] = jnp.zeros_like(dk_acc)
            dv_acc[...] = jnp.zeros_like(dv_acc)
            ddr_ref[...] = jnp.zeros_like(ddr_ref)

        lane = lax.broadcasted_iota(jnp.int32, (tq, LANES), 1)
        low = lane < HEAD_DIM
        causal = lax.broadcasted_iota(jnp.int32, (tq, tq), 1) <= lax.broadcasted_iota(jnp.int32, (tq, tq), 0)
        for p in range(n_pairs):
            cols = slice(LANES * p, LANES * (p + 1))
            q2 = q_ref[:, cols] * scale
            do2 = do_ref[:, cols]
            do_f = do2.astype(F32)
            prod = do_f * o_ref[:, cols].astype(F32)
            qm_s[2 * p] = jnp.where(low, q2, jnp.zeros_like(q2))
            qm_s[2 * p + 1] = jnp.where(low, jnp.zeros_like(q2), q2)
            dom_s[2 * p] = jnp.where(low, do2, jnp.zeros_like(do2))
            dom_s[2 * p + 1] = jnp.where(low, jnp.zeros_like(do2), do2)
            delta_s[2 * p] = jnp.sum(jnp.where(low, prod, 0.0), axis=1, keepdims=True)
            delta_s[2 * p + 1] = jnp.sum(jnp.where(low, 0.0, prod), axis=1, keepdims=True)
            dq_s[p] = jnp.zeros((tq, LANES), F32)
        rs_s[...] = jnp.zeros(rs_s.shape, F32)

        def step(j, masked):
            ks = pl.multiple_of(j * tq, tq)
            for p in range(n_pairs):
                cols = slice(LANES * p, LANES * (p + 1))
                k2, v2 = k_ref[pl.ds(ks, tq), cols], v_ref[pl.ds(ks, tq), cols]
                dvs, dks, dqs = [], [], []
                for h in (2 * p, 2 * p + 1):
                    qm, dom = qm_s[h], dom_s[h]
                    s = _dot(qm, k2, NT) - dr_ref[h, pl.ds(j, 1), :]
                    if masked:
                        s = jnp.where(causal, s, -jnp.inf)
                    pm = jnp.exp(s - lse_ref[:, h:h + 1])
                    ds = pm * (_dot(dom, v2, NT) - delta_s[h])
                    ddr_ref[h, pl.ds(j, 1), :] -= jnp.sum(ds, axis=0, keepdims=True)
                    rs_s[h] += jnp.sum(ds, axis=1, keepdims=True)
                    dsb = ds.astype(BF16)
                    dvs.append(_dot(pm.astype(BF16), dom, TN))
                    dks.append(_dot(dsb, qm, TN))
                    dqs.append(_dot(dsb, k2, NN))
                dv_acc[pl.ds(ks, tq), cols] += dvs[0] + dvs[1]
                dk_acc[pl.ds(ks, tq), cols] += dks[0] + dks[1]
                dq_s[p] += jnp.where(low, dqs[0], dqs[1])

        def body(j, carry):
            step(j, False)
            return carry

        lax.fori_loop(0, i, body, 0)
        step(i, True)
        ddc = jnp.zeros((tq, LANES), F32)
        for p in range(n_pairs):
            dq_ref[:, LANES * p:LANES * (p + 1)] = (dq_s[p] * scale).astype(BF16)
            ddc = jnp.where(lane == 2 * p, rs_s[2 * p], ddc)
            ddc = jnp.where(lane == 2 * p + 1, rs_s[2 * p + 1], ddc)
        ddc_ref[...] = ddc

        @pl.when(i == nq - 1)
        def _():
            dk_ref[...] = dk_acc[...].astype(BF16)
            dv_ref[...] = dv_acc[...].astype(BF16)

    tile = pl.BlockSpec((None, tq, DA), lambda b, i: (b, i, 0))
    seq = pl.BlockSpec((None, S, DA), lambda b, i: (b, 0, 0))
    dec = pl.BlockSpec((None, 8, nq, tq), lambda b, i: (b, 0, 0, 0))
    return pl.pallas_call(
        kern, name="attn_bwd", grid=(B, nq),
        in_specs=[tile, pl.BlockSpec((None, S, DA), lambda b, i: (b, 0, 1)),
                  pl.BlockSpec((None, S, DA), lambda b, i: (b, 0, 2)), dec, tile,
                  pl.BlockSpec((None, tq, LANES), lambda b, i: (b, i, 0)), tile],
        out_specs=[tile, seq, seq, dec, pl.BlockSpec((None, tq, LANES), lambda b, i: (b, i, 0))],
        out_shape=[jax.ShapeDtypeStruct((B, S, DA), BF16)] * 3 + [jax.ShapeDtypeStruct((B, 8, nq, tq), F32),
                                                                  jax.ShapeDtypeStruct((B, S, LANES), F32)],
        scratch_shapes=[pltpu.VMEM((S, DA), F32), pltpu.VMEM((S, DA), F32),
                        pltpu.VMEM((n_heads, tq, LANES), BF16), pltpu.VMEM((n_heads, tq, LANES), BF16),
                        pltpu.VMEM((n_heads, tq, 1), F32), pltpu.VMEM((n_heads, tq, 1), F32),
                        pltpu.VMEM((n_pairs, tq, LANES), F32)],
        compiler_params=_params(2),
    )(qkv, qkv, qkv, drow, o, lse, dycat)


def _down(v, d, row):
    return jnp.where(row >= d, pltpu.roll(v, d, 0), 0.0)


def _up(v, d, row, S):
    return jnp.where(row < S - d, pltpu.roll(v, S - d, 0), 0.0)


def _window(v, shift, group):
    sums, acc, d = [], v, 1
    for _ in POOL_WINDOWS:
        acc = acc + shift(acc, d)
        sums.append(acc)
        d *= 2
    out = sums[-1]
    for gi in range(len(POOL_WINDOWS) - 2, -1, -1):
        out = jnp.where(group == gi, sums[gi], out)
    return out


def _pool_count(row, group):
    w = jnp.full(row.shape, POOL_WINDOWS[-1], jnp.int32)
    for gi in range(len(POOL_WINDOWS) - 2, -1, -1):
        w = jnp.where(group == gi, POOL_WINDOWS[gi], w)
    return jnp.minimum(row + 1, w).astype(F32)


def _mix_local_fwd(rest, wbd, ps, cw):
    B, S, C4 = rest.shape
    C = C4 // 4
    gw = C // len(POOL_WINDOWS)

    def kern(r_ref, w_ref, ps_ref, cw_ref, y_ref, pooled_ref):
        row = lax.broadcasted_iota(jnp.int32, (S, C), 0)
        group = lax.broadcasted_iota(jnp.int32, (S, C), 1) // gw
        u = r_ref[:, 0:C]
        pooled = _window(u, lambda v, d: _down(v, d, row), group) / _pool_count(row, group) - u
        pb = pooled.astype(BF16)
        pooled_ref[...] = pb
        y_ref[:, 0:C] = (_dot(pb, w_ref[...], NN) * ps_ref[...]).astype(BF16)
        uc = r_ref[:, 2 * C:3 * C] * r_ref[:, 3 * C:4 * C]
        y = cw_ref[0:1, :] * _down(uc, 2, row) + cw_ref[1:2, :] * _down(uc, 1, row) + cw_ref[2:3, :] * uc
        y_ref[:, C:2 * C] = (r_ref[:, C:2 * C] * y).astype(BF16)

    return pl.pallas_call(
        kern, name="mix_local_fwd", grid=(B,),
        in_specs=[pl.BlockSpec((None, S, C4), lambda b: (b, 0, 0)), pl.BlockSpec((C, C), lambda b: (0, 0)),
                  pl.BlockSpec((1, C), lambda b: (0, 0)), pl.BlockSpec((8, C), lambda b: (0, 0))],
        out_specs=[pl.BlockSpec((None, S, 2 * C), lambda b: (b, 0, 0)), pl.BlockSpec((None, S, C), lambda b: (b, 0, 0))],
        out_shape=[jax.ShapeDtypeStruct((B, S, 2 * C), BF16), jax.ShapeDtypeStruct((B, S, C), BF16)],
        compiler_params=_params(1),
    )(rest, wbd, ps, cw)


def _mix_local_bwd(rest, pooled, dycat, wbd, ps, cw):
    B, S, C4 = rest.shape
    C = C4 // 4
    gw = C // len(POOL_WINDOWS)

    def kern(r_ref, pooled_ref, d_ref, w_ref, ps_ref, cw_ref, dr_ref, dw_ref, dps_ref, dcw_ref):
        row = lax.broadcasted_iota(jnp.int32, (S, C), 0)
        group = lax.broadcasted_iota(jnp.int32, (S, C), 1) // gw
        dyp = d_ref[:, 0:C].astype(F32)
        dyc = d_ref[:, C:2 * C].astype(F32)
        pb = pooled_ref[...]
        dps = jnp.sum(dyp * _dot(pb, w_ref[...], NN), axis=0, keepdims=True)
        dzb = (dyp * ps_ref[...]).astype(BF16)
        dw = _dot(pb, dzb, TN)
        dpooled = _dot(dzb, w_ref[...], NT)
        g = dpooled / _pool_count(row, group)
        dr_ref[:, 0:C] = (_window(g, lambda v, d: _up(v, d, row, S), group) - dpooled).astype(BF16)
        cc, ch = r_ref[:, 2 * C:3 * C], r_ref[:, 3 * C:4 * C]
        uc = cc * ch
        u1, u2 = _down(uc, 1, row), _down(uc, 2, row)
        y = cw_ref[0:1, :] * u2 + cw_ref[1:2, :] * u1 + cw_ref[2:3, :] * uc
        dr_ref[:, C:2 * C] = (dyc * y).astype(BF16)
        dy = dyc * r_ref[:, C:2 * C]
        duc = cw_ref[0:1, :] * _up(dy, 2, row, S) + cw_ref[1:2, :] * _up(dy, 1, row, S) + cw_ref[2:3, :] * dy
        dr_ref[:, 2 * C:3 * C] = (duc * ch).astype(BF16)
        dr_ref[:, 3 * C:4 * C] = (duc * cc).astype(BF16)
        dcw = jnp.concatenate([jnp.sum(dy * u2, axis=0, keepdims=True), jnp.sum(dy * u1, axis=0, keepdims=True),
                               jnp.sum(dy * uc, axis=0, keepdims=True), jnp.zeros((5, C), F32)], axis=0)

        @pl.when(pl.program_id(0) == 0)
        def _():
            dw_ref[...] = dw
            dps_ref[...] = dps
            dcw_ref[...] = dcw

        @pl.when(pl.program_id(0) > 0)
        def _():
            dw_ref[...] += dw
            dps_ref[...] += dps
            dcw_ref[...] += dcw

    full = lambda shape: pl.BlockSpec(shape, lambda b: (0, 0))
    return pl.pallas_call(
        kern, name="mix_local_bwd", grid=(B,),
        in_specs=[pl.BlockSpec((None, S, C4), lambda b: (b, 0, 0)), pl.BlockSpec((None, S, C), lambda b: (b, 0, 0)),
                  pl.BlockSpec((None, S, 2 * C), lambda b: (b, 0, 1)), full((C, C)), full((1, C)), full((8, C))],
        out_specs=[pl.BlockSpec((None, S, C4), lambda b: (b, 0, 0)), full((C, C)), full((1, C)), full((8, C))],
        out_shape=[jax.ShapeDtypeStruct((B, S, C4), BF16), jax.ShapeDtypeStruct((C, C), F32),
                   jax.ShapeDtypeStruct((1, C), F32), jax.ShapeDtypeStruct((8, C), F32)],
        compiler_params=_params(1),
    )(rest, pooled, dycat, wbd, ps, cw)


def _adamw(w, gs, m, v, token=None):
    R, C = w.shape
    pieces = [p if isinstance(p, tuple) else (p,) for p in gs]
    owner = [s for s, p in enumerate(pieces) for _ in p]
    flat = [a for p in pieces for a in p]
    n = len(flat)
    rows = R // len(pieces)
    tr = _tile(rows, 256)
    per = rows // tr
    tok_ops, tok_specs = _token_operand(token)

    def kern(w_ref, *refs):
        g_refs, (m_ref, v_ref), (g_out, d_ref, nm_ref, nv_ref) = refs[:n], refs[n:n + 2], refs[n + 2 + len(tok_ops):]
        vals, at = [], 0
        for p in pieces:
            vals.append(g_refs[at][...] if len(p) == 1 else g_refs[at][...] + g_refs[at + 1][...])
            at += len(p)
        gv = vals[0]
        for s in range(1, len(pieces)):
            gv = jnp.where(pl.program_id(0) // per == s, vals[s], gv)
        nm = ADAM_B1 * m_ref[...] + (1.0 - ADAM_B1) * gv
        nv = ADAM_B2 * v_ref[...] + (1.0 - ADAM_B2) * (gv * gv)
        m_hat = nm / (1.0 - ADAM_B1 ** ADAM_STEP)
        v_hat = nv / (1.0 - ADAM_B2 ** ADAM_STEP)
        g_out[...] = gv
        d_ref[...] = -ADAM_LR * (m_hat / (jnp.sqrt(v_hat) + ADAM_EPS) + ADAM_WD * w_ref[...])
        nm_ref[...] = nm
        nv_ref[...] = nv

    def piece(s):
        return pl.BlockSpec((tr, C), lambda i: (jnp.clip(i - s * per, 0, per - 1), 0))

    blk = pl.BlockSpec((tr, C), lambda i: (i, 0))
    return pl.pallas_call(
        kern, name="adamw", grid=(R // tr,), in_specs=[blk] + [piece(s) for s in owner] + [blk] * 2 + tok_specs,
        out_specs=[blk] * 4, out_shape=[jax.ShapeDtypeStruct((R, C), F32)] * 4, compiler_params=_params(1),
    )(w, *flat, m, v, *tok_ops)


def _place():
    x, y, c = lax.axis_index("x"), lax.axis_index("y"), lax.axis_index("c")
    return x, y, c, [(1 - x, y), (x, 1 - y), (1 - x, 1 - y)]


def _comm_call(name, body, operands, out_shape, n_sems, aliases=None):
    any_spec = pl.BlockSpec(memory_space=pl.ANY)
    return pl.pallas_call(
        body, name=name, in_specs=[any_spec] * len(operands), out_specs=[any_spec] * len(out_shape),
        out_shape=out_shape, input_output_aliases=aliases or {},
        scratch_shapes=[pltpu.SemaphoreType.DMA((n,)) for n in n_sems],
    )(*operands)


def _my_block():
    return 2 * lax.axis_index("x") + lax.axis_index("y")


def _place_shard(w, dtype, first=0, count=None):
    L, R, C = w.shape
    count = L if count is None else count
    tr = _tile(R, 512)

    def kern(w_ref, o_ref):
        o_ref[...] = w_ref[...].astype(dtype)

    return pl.pallas_call(
        kern, name="place_shard", grid=(count, R // tr),
        in_specs=[pl.BlockSpec((None, tr, C), lambda l, i: (first + l, i, 0))],
        out_specs=pl.BlockSpec((None, None, tr, C), lambda l, i: (l, _my_block(), i, 0)),
        out_shape=jax.ShapeDtypeStruct((count, N_CHIPS, R, C), dtype), compiler_params=_params(2),
    )(w)


HALF_ROWS = 16


def _rows(ref, half):
    hr = ref.shape[-2] // 2
    return ref.at[(slice(None),) * (len(ref.shape) - 2) + (pl.ds(half * hr, hr),)]


def _all_gather(bufs):
    n = len(bufs)

    def body(*refs):
        outs = refs[n:2 * n]
        send_sems, recv_sems = refs[2 * n:]
        x, y, c, chips = _place()
        sibling = (x, y, 1 - c)

        def remote(k, j, chip, half, to):
            blk = 2 * chip[0] + chip[1]
            if outs[k].shape[2] % (2 * HALF_ROWS) == 0:
                region = _rows(outs[k].at[:, blk], half)
            else:
                hl = outs[k].shape[0] // 2
                region = outs[k].at[pl.ds(half * hl, hl), blk]
            return pltpu.make_async_remote_copy(
                src_ref=region, dst_ref=region, send_sem=send_sems.at[6 * k + j],
                recv_sem=recv_sems.at[6 * k + j], device_id=to, device_id_type=MESH)

        first = [remote(k, j, (x, y), c, (*chip, c)) for k in range(n) for j, chip in enumerate(chips)]
        for cp in first:
            cp.start()
        passed = []
        for k in range(n):
            for j, chip in enumerate(chips):
                remote(k, j, chip, c, (x, y, c)).wait_recv()
                passed.append(remote(k, 3 + j, chip, c, sibling))
                passed[-1].start()
        for k in range(n):
            for j, chip in enumerate(chips):
                remote(k, 3 + j, chip, 1 - c, (x, y, c)).wait_recv()
        for cp in first + passed:
            cp.wait_send()

    out_shape = [jax.ShapeDtypeStruct(s.shape, s.dtype) for s in bufs]
    return _comm_call("all_gather_weights", body, bufs, out_shape, (6 * n, 6 * n), aliases={k: k for k in range(n)})


_HBM = pl.BlockSpec(memory_space=pltpu.HBM)
_SEM = pl.BlockSpec(memory_space=pltpu.SEMAPHORE)
_ANY = pl.BlockSpec(memory_space=pl.ANY)


def _split_start(name, bufs, n_copies, make_copies, after):
    n = len(bufs)

    def body(*refs):
        send_sems, recv_sems, token = refs[n + 1], refs[n + 2], refs[2 * n + 3]
        for cp in make_copies(refs[:n], send_sems, recv_sems):
            cp.start()
        token[...] = jnp.zeros_like(token)

    res = pl.pallas_call(
        body, name=name, in_specs=[_HBM] * n + [_ANY],
        out_shape=(pltpu.SemaphoreType.DMA((n_copies,)), pltpu.SemaphoreType.DMA((n_copies,)),
                   *[pltpu.HBM(b.shape, b.dtype) for b in bufs], jax.ShapeDtypeStruct((8, LANES), F32)),
        out_specs=(_SEM, _SEM, *[_HBM] * n, pl.BlockSpec(memory_space=pltpu.VMEM)),
        input_output_aliases={i: 2 + i for i in range(n)},
        compiler_params=pltpu.CompilerParams(has_side_effects=pltpu.SideEffectType.DATAFLOW_SIDE_EFFECTING),
    )(*[pltpu.with_memory_space_constraint(b, pltpu.HBM) for b in bufs], after)
    return res[0], res[1], list(res[2:2 + n]), res[2 + n]


def _split_wait(name, send_sems, recv_sems, bufs, make_copies, after):
    n = len(bufs)

    def body(*refs):
        for cp in make_copies(refs[:n], refs[n], refs[n + 1]):
            cp.wait_send()
            cp.wait_recv()

    return list(pl.pallas_call(
        body, name=name, in_specs=[_HBM] * n + [_SEM, _SEM, _ANY],
        out_shape=tuple(pltpu.HBM(b.shape, b.dtype) for b in bufs), out_specs=tuple([_HBM] * n),
        input_output_aliases={i: i for i in range(n)},
        compiler_params=pltpu.CompilerParams(has_side_effects=pltpu.SideEffectType.DATAFLOW_SIDE_EFFECTING),
    )(*bufs, send_sems, recv_sems, after))


def _gather_copies(refs, send_sems, recv_sems):
    x, y, c, chips = _place()
    return [pltpu.make_async_remote_copy(
        src_ref=ref.at[:, 2 * x + y], dst_ref=ref.at[:, 2 * x + y], send_sem=send_sems.at[3 * k + j],
        recv_sem=recv_sems.at[3 * k + j], device_id=(*chip, c), device_id_type=MESH)
        for k, ref in enumerate(refs) for j, chip in enumerate(chips)]


def _exchange_copies(refs, send_sems, recv_sems):
    n = len(refs) // 2
    x, y, c, chips = _place()
    return [pltpu.make_async_remote_copy(
        src_ref=refs[k].at[:, 2 * chip[0] + chip[1]], dst_ref=refs[n + k].at[j], send_sem=send_sems.at[3 * k + j],
        recv_sem=recv_sems.at[3 * k + j], device_id=(*chip, c), device_id_type=MESH)
        for k in range(n) for j, chip in enumerate(chips)]


def _all_reduce_small(v, after):
    n = v.shape[0]

    def body(v_ref, after_ref, o_ref, gbuf, send_sems, recv_sems):
        x, y, c, _ = _place()
        me = 4 * x + 2 * y + c
        gbuf[me] = v_ref[...]
        copies, waits = [], []
        for r in range(1, N_DEV):
            px = 1 - x if r & 4 else x
            py = 1 - y if r & 2 else y
            pc = 1 - c if r & 1 else c
            mk = functools.partial(pltpu.make_async_remote_copy, src_ref=v_ref, send_sem=send_sems.at[r - 1],
                                   recv_sem=recv_sems.at[r - 1], device_id=(px, py, pc), device_id_type=MESH)
            copies.append(mk(dst_ref=gbuf.at[me]))
            waits.append(mk(dst_ref=gbuf.at[4 * px + 2 * py + pc]))
        for cp in copies:
            cp.start()
        for cp in waits:
            cp.wait_recv()
        for cp in copies:
            cp.wait_send()
        acc = gbuf[0]
        for d in range(1, N_DEV):
            acc = acc + gbuf[d]
        o_ref[...] = acc

    vm = pl.BlockSpec(memory_space=pltpu.VMEM)
    return pl.pallas_call(
        body, name="all_reduce_small", in_specs=[vm, _ANY], out_specs=vm, out_shape=jax.ShapeDtypeStruct(v.shape, F32),
        scratch_shapes=[pltpu.VMEM((N_DEV, n, LANES), F32), pltpu.SemaphoreType.DMA((N_DEV - 1,)),
                        pltpu.SemaphoreType.DMA((N_DEV - 1,))],
        compiler_params=pltpu.CompilerParams(vmem_limit_bytes=VMEM_LIMIT),
    )(v, after)


def _add_blocks(p, h2):
    L, nb, hr, C = p.shape
    tr = _tile(hr, 512)

    def kern(p_ref, h0_ref, h1_ref, h2_ref, o_ref):
        o_ref[...] = ((p_ref[...].astype(F32) + h0_ref[...].astype(F32)) + h1_ref[...].astype(F32)) + h2_ref[...].astype(F32)

    def other(j):
        return pl.BlockSpec((None, None, tr, C), lambda l, i: (j, l, i, 0))

    return pl.pallas_call(
        kern, name="rs_add_blocks", grid=(L, hr // tr),
        in_specs=[pl.BlockSpec((None, None, tr, C), lambda l, i: (l, _my_block(), i, 0)), other(0), other(1), other(2)],
        out_specs=pl.BlockSpec((None, tr, C), lambda l, i: (l, i, 0)),
        out_shape=jax.ShapeDtypeStruct((L, hr, C), F32), compiler_params=_params(2),
    )(p, h2, h2, h2)


WEIGHTS = ("norm_ffn1", "w_ffn1_in", "w_ffn1_out", "norm_mix", "w_mix_in", "b_forget", "w_pool", "pool_scale",
           "conv_w", "w_mix_out", "norm_ffn2", "w_ffn2_in", "w_ffn2_out", "norm_final")
BIG = ("w_ffn1_in", "w_ffn1_out", "w_mix_in", "w_mix_out", "w_ffn2_in", "w_ffn2_out")
SMALL = ("norm_ffn1", "norm_mix", "b_forget", "w_pool", "pool_scale", "conv_w", "norm_ffn2", "norm_final")


def _layer_params(small, gathered, conv_w, D, l):
    DA, C, H = D // 2, D // 4, D // 2 // HEAD_DIM
    P = {}
    if "w_ffn1_in" in gathered:
        P.update(g1=small["norm_ffn1"][l][None], w1in=(gathered["w_ffn1_in"], 0),
                 w1out=(gathered["w_ffn1_out"].reshape(1, -1, D), 0))
    if "w_ffn2_in" in gathered:
        P.update(g2=small["norm_ffn2"][l][None], w2in=(gathered["w_ffn2_in"], 0),
                 w2out=(gathered["w_ffn2_out"].reshape(1, -1, D), 0))
    if "w_mix_in" in gathered:
        w_in = jnp.concatenate([gathered["w_mix_in"][:, b] for b in range(N_CHIPS)], axis=2)
        wqkv, wrest = w_in[:, :, :3 * DA], w_in[:, :, 3 * DA + H:]
        wf = jnp.pad(w_in[:, :, 3 * DA:3 * DA + H], ((0, 0), (0, 0), (0, LANES - H)))
        ng = len(POOL_WINDOWS)
        same_group = jnp.eye(ng, dtype=bool)[:, None, :, None]
        wbd = jnp.where(same_group, small["w_pool"][l][:, :, None, :], 0.0).reshape(C, C)
        cw = jnp.concatenate([conv_w[l, b] for b in range(N_CHIPS)], axis=1)
        P.update(gm=small["norm_mix"][l][None], wp=(jnp.concatenate([wqkv, wrest, wf], axis=2), 0),
                 wmixout=(gathered["w_mix_out"].reshape(1, D, D), 0),
                 bias=jnp.pad(small["b_forget"][l][None], ((0, 0), (0, LANES - H))), wbd=wbd.astype(BF16),
                 ps=small["pool_scale"][l][None], cw=jnp.pad(cw, ((0, 8 - CONV_WIDTH), (0, 0))))
    return P


def _ffn_fwd(x, g, w_in, w_out, token=None):
    h, jac, act = _ffn_up(x, g, w_in, token)
    return _ffn_out(act, w_out, x)[0], (x, h, jac, act)


def _ffn_bwd(dres, saved, g, w_in, w_out, token=None):
    x, h, jac, act = saved
    dgu, dx, dg = _ffn_bwd_main(dres, jac, x, g, w_out, w_in, token)
    dw_out = _ffn_dw_out(act, dres)[0]
    dw_in = _ffn_dw_in(h, dgu)[0]
    return dx, dg, dw_in, dw_out.reshape(N_CHIPS, -1, dw_out.shape[1])


def _mixer_fwd(x, P, B, S, tq):
    T, D = x.shape
    DA, C, H = D // 2, D // 4, D // 2 // HEAD_DIM
    hn, qkv, rest, fl = _mix_up(x, P["gm"], P["wp"], (3 * DA, 4 * C))
    qkv, rest, fl = qkv.reshape(B, S, 3 * DA), rest.reshape(B, S, 4 * C), fl.reshape(B, S, LANES)
    drow = _decay_fwd(fl, P["bias"]).reshape(B, 8, S // tq, tq)
    o, lse = _attn_fwd(qkv, drow, H, tq)
    ypc, pooled = _mix_local_fwd(rest, P["wbd"], P["ps"], P["cw"])
    x_out = _mix_out([o.reshape(T, DA), ypc.reshape(T, 2 * C)], P["wmixout"], x)
    return x_out, (x, hn, qkv, rest, fl, drow, o, lse, pooled, ypc)


def _mixer_bwd(dres, saved, P, B, S, tq):
    x, hn, qkv, rest, fl, drow, o, lse, pooled, ypc = saved
    T, D = x.shape
    DA, C, H = D // 2, D // 4, D // 2 // HEAD_DIM
    dycat = _proj("mix_out_bwd", dres, P["wmixout"], BF16, NT).reshape(B, S, D)
    dw_out = _rows_dw("mix_out_dw", [o.reshape(T, DA), ypc.reshape(T, 2 * C)], dres, BF16)
    dq, dk, dv, ddrow, ddcol = _attn_bwd(qkv, drow, o, lse, dycat, H, tq)
    dfl, dbias = _decay_bwd(ddrow.reshape(B, 8, S), ddcol, fl, P["bias"], H)
    drest, dwbd, dps, dcw = _mix_local_bwd(rest, pooled, dycat, P["wbd"], P["ps"], P["cw"])
    pieces = [a.reshape(T, a.shape[-1]) for a in (dq, dk, dv, drest, dfl)]
    dwp = _pieces_dw("mix_in_dw", hn, pieces, BF16, 1024)
    dx, dg = _mix_in_bwd(pieces, x, P["gm"], dres, P["wp"])
    n_q, n_r = 3 * DA, 4 * C
    dw_in = jnp.concatenate([dwp[:, :n_q], dwp[:, n_q + n_r:n_q + n_r + H], dwp[:, n_q:n_q + n_r]], axis=1)
    dw_in = dw_in.reshape(D, N_CHIPS, -1).transpose(1, 0, 2)
    ng = len(POOL_WINDOWS)
    same_group = jnp.eye(ng, dtype=bool)[:, None, :, None]
    dw_pool = jnp.where(same_group, dwbd.reshape(ng, C // ng, ng, C // ng), 0.0).sum(axis=2)
    small = dict(norm_mix=dg[0], b_forget=dbias[0, :H], w_pool=dw_pool, pool_scale=dps[0], conv_w=dcw[:CONV_WIDTH])
    return dx, small, dw_in, dw_out.reshape(N_CHIPS, -1, D)


FFN1, MIX, FFN2 = BIG[:2], BIG[2:4], BIG[4:]


def _local_step(x, target, small, conv_w, pipe):
    B, S, D = x.shape
    L = small["norm_ffn1"].shape[0]
    tq = _tile(S, 512)
    xt = x.reshape(B * S, D)
    saved, params = [], []
    for l in range(L):
        P = _layer_params(small, pipe.weights(l, xt), conv_w, D, l)
        xt, s1 = _ffn_fwd(xt, P["g1"], P["w1in"], P["w1out"], pipe.token(l))
        P.update(_layer_params(small, pipe.weights_mix(l, xt), conv_w, D, l))
        xt, s2 = _mixer_fwd(xt, P, B, S, tq)
        P.update(_layer_params(small, pipe.weights_ffn2(l, xt), conv_w, D, l))
        xt, s3 = _ffn_fwd(xt, P["g2"], P["w2in"], P["w2out"])
        saved.append((s1, s2, s3))
        params.append(P)
    dres, dgf, loss = _final_loss(xt, small["norm_final"][None], target.reshape(B * S, D))
    sm = {k: [None] * L for k in SMALL if k != "norm_final"}
    token = None
    for l in reversed(range(L)):
        P, (s1, s2, s3) = params[l], saved[l]
        big = {}
        dres, dg2, big["w_ffn2_in"], big["w_ffn2_out"] = _ffn_bwd(dres, s3, P["g2"], P["w2in"], P["w2out"], token)
        dres, smix, big["w_mix_in"], big["w_mix_out"] = _mixer_bwd(dres, s2, P, B, S, tq)
        big = {k: val[None] for k, val in big.items()}
        token = pipe.grads(l, FFN2 + MIX, big, dres) if l == 0 else None
        dres, dg1, dw_in, dw_out = _ffn_bwd(dres, s1, P["g1"], P["w1in"], P["w1out"], token)
        big.update(w_ffn1_in=dw_in[None], w_ffn1_out=dw_out[None])
        sm["norm_ffn1"][l], sm["norm_ffn2"][l] = dg1[0], dg2[0]
        for k, val in smix.items():
            sm[k][l] = val
        token = pipe.grads(l, FFN1 if l == 0 else BIG, big, big["w_ffn1_in"])
    sm = {k: jnp.stack(val) for k, val in sm.items()}
    sm["norm_final"] = dgf[0]
    return loss[0, 0], dres.reshape(B, S, D), sm


def _sibling_copies(refs, send_sems, recv_sems):
    n = len(refs) // 2
    x, y, c, _ = _place()
    return [pltpu.make_async_remote_copy(
        src_ref=refs[k], dst_ref=refs[n + k], send_sem=send_sems.at[k], recv_sem=recv_sems.at[k],
        device_id=(x, y, 1 - c), device_id_type=MESH) for k in range(n)]


class _Pipeline:
    def __init__(self, w):
        self.w, self.n_layers = w, w[BIG[0]].shape[0]
        first = _all_gather([_place_shard(w[k], BF16, 0, 1) for k in FFN1] + [_place_shard(w["conv_w"], F32)])
        self.conv_w = first[-1]
        self._ready = dict(zip(FFN1, first[:-1]))
        self._mix = self._start_gather("0_mix", MIX, 0, first[0])
        self._ffn2 = self._start_gather("0_ffn2", FFN2, 0, self._mix[1][3])
        self._next = (1, self._start_gather("1", BIG, 1, self._ffn2[1][3]))
        self._reduce, self._swaps = None, []
        self.reduced = [dict() for _ in range(self.n_layers)]

    def _start_gather(self, tag, kinds, l, after):
        placed = [_place_shard(self.w[k], BF16, l, 1) for k in kinds]
        return kinds, _split_start(f"gather_start_{tag}", placed, 3 * len(kinds), _gather_copies, after)

    def _wait_gather(self, tag, started, after):
        kinds, (send_sems, recv_sems, bufs, _) = started
        return dict(zip(kinds, _split_wait(f"gather_wait_{tag}", send_sems, recv_sems, bufs, _gather_copies, after)))

    def token(self, l):
        return self._next[1][1][3] if self._next is not None and self._next[0] == l + 1 else None

    def weights(self, l, after):
        if l == 0:
            return self._ready
        got = self._wait_gather(str(l), self._next[1], after)
        self._next = (l + 1, self._start_gather(str(l + 1), BIG, l + 1, got[BIG[0]])) if l + 1 < self.n_layers else None
        return got

    def weights_mix(self, l, after):
        return self._wait_gather("0_mix", self._mix, after) if l == 0 else {}

    def weights_ffn2(self, l, after):
        return self._wait_gather("0_ffn2", self._ffn2, after) if l == 0 else {}

    def _finish_reduce(self, after):
        if self._reduce is None:
            return None
        tag, l, kinds, (send_sems, recv_sems, bufs, _) = self._reduce
        n = len(kinds)
        bufs = _split_wait(f"reduce_wait_{tag}", send_sems, recv_sems, bufs, _exchange_copies, after)
        mine = [_add_blocks(p, o) for p, o in zip(bufs[:n], bufs[n:])]
        lands = [lax.empty(q.shape, q.dtype) for q in mine]
        self._swaps.append((tag, l, kinds, _split_start(f"swap_start_{tag}", mine + lands, n, _sibling_copies, mine[0])))
        self._reduce = None
        return self._swaps[-1][3][3]

    def grads(self, l, kinds, big, after):
        swap_token = self._finish_reduce(after)
        grads = [big[k] for k in kinds]
        tag = str(l) if len(kinds) == len(BIG) else f"{l}_{kinds[0][2:]}"
        lands = [lax.empty((3, g.shape[0]) + g.shape[2:], g.dtype) for g in grads]
        started = _split_start(f"reduce_start_{tag}", grads + lands, 3 * len(kinds), _exchange_copies,
                               grads[0] if swap_token is None else swap_token)
        self._reduce = (tag, l, kinds, started)
        self.last_token = started[3]
        return started[3]

    def finish(self, after, last=False):
        if last:
            self._finish_reduce(after)
        for tag, l, kinds, (send_sems, recv_sems, bufs, _) in self._swaps:
            n = len(kinds)
            bufs = _split_wait(f"swap_wait_{tag}", send_sems, recv_sems, bufs, _sibling_copies, after)
            self.reduced[l].update(zip(kinds, zip(bufs[:n], bufs[n:])))
        self._swaps = []
        return self.reduced


def _pack(parts, extra=()):
    flat = jnp.concatenate([p.reshape(-1) for p in parts] + [jnp.reshape(e, (1,)) for e in extra])
    n = -(-flat.shape[0] // (8 * LANES)) * 8
    return jnp.pad(flat, (0, n * LANES - flat.shape[0])).reshape(n, LANES)


def _unpack(buf, shapes):
    flat, out, at = buf.reshape(-1), [], 0
    for s in shapes:
        n = math.prod(s)
        out.append(flat[at:at + n].reshape(s))
        at += n
    return out, flat[at:]


def kernel(x, norm_ffn1, w_ffn1_in, w_ffn1_out, norm_mix, w_mix_in, b_forget, w_pool, pool_scale, conv_w, w_mix_out, norm_ffn2, w_ffn2_in, w_ffn2_out, norm_final, loss_target, m_norm_ffn1, m_w_ffn1_in, m_w_ffn1_out, m_norm_mix, m_w_mix_in, m_b_forget, m_w_pool, m_pool_scale, m_conv_w, m_w_mix_out, m_norm_ffn2, m_w_ffn2_in, m_w_ffn2_out, m_norm_final, v_norm_ffn1, v_w_ffn1_in, v_w_ffn1_out, v_norm_mix, v_w_mix_in, v_b_forget, v_w_pool, v_pool_scale, v_conv_w, v_w_mix_out, v_norm_ffn2, v_w_ffn2_in, v_w_ffn2_out, v_norm_final):
    w = dict(zip(WEIGHTS, (norm_ffn1, w_ffn1_in, w_ffn1_out, norm_mix, w_mix_in, b_forget, w_pool, pool_scale, conv_w, w_mix_out, norm_ffn2, w_ffn2_in, w_ffn2_out, norm_final)))
    m = dict(zip(WEIGHTS, (m_norm_ffn1, m_w_ffn1_in, m_w_ffn1_out, m_norm_mix, m_w_mix_in, m_b_forget, m_w_pool, m_pool_scale, m_conv_w, m_w_mix_out, m_norm_ffn2, m_w_ffn2_in, m_w_ffn2_out, m_norm_final)))
    v = dict(zip(WEIGHTS, (v_norm_ffn1, v_w_ffn1_in, v_w_ffn1_out, v_norm_mix, v_w_mix_in, v_b_forget, v_w_pool, v_pool_scale, v_conv_w, v_w_mix_out, v_norm_ffn2, v_w_ffn2_in, v_w_ffn2_out, v_norm_final)))
    block = 2 * lax.axis_index("x") + lax.axis_index("y")

    pipe = _Pipeline(w)
    small = {k: w[k] for k in SMALL}
    loss, grad_x, sm = _local_step(x, loss_target, small, pipe.conv_w, pipe)
    grads, delta, new_m, new_v = {}, {}, {}, {}

    def big_adamw(k, reduced, token=None):
        two_d = lambda a: a.reshape(-1, a.shape[-1])
        pieces = [tuple(map(two_d, layer[k])) for layer in reduced]
        res = _adamw(two_d(w[k]), pieces, two_d(m[k]), two_d(v[k]), token)
        grads[k], delta[k], new_m[k], new_v[k] = [r.reshape(w[k].shape) for r in res]

    reduced = pipe.finish(grad_x)
    for k in BIG[2:]:
        big_adamw(k, reduced, pipe.last_token)
    reduced = pipe.finish(new_v[BIG[-1]], last=True)
    for k in BIG[:2]:
        big_adamw(k, reduced)

    order = list(SMALL)
    total = _all_reduce_small(_pack([sm[k] for k in order], extra=(loss,)), new_v[BIG[0]])
    parts, rest = _unpack(total, [sm[k].shape for k in order])
    grads.update(zip(order, parts))
    loss = rest[0]
    cs = conv_w.shape[2]
    grads["conv_w"] = lax.dynamic_slice_in_dim(grads["conv_w"], block * cs, cs, axis=2)
    packed = [_pack([t[k] for k in order]) for t in (w, grads, m, v)]
    _, d, nm, nv = _adamw(packed[0], [packed[1]], packed[2], packed[3])
    shapes = [w[k].shape for k in order]
    for res, flat in ((delta, d), (new_m, nm), (new_v, nv)):
        res.update(zip(order, _unpack(flat, shapes)[0]))
    return (loss, grad_x, *[grads[k] for k in WEIGHTS], *[delta[k] for k in WEIGHTS],
            *[new_m[k] for k in WEIGHTS], *[new_v[k] for k in WEIGHTS])
```
